```python
import math
import jax
import jax.numpy as jnp
from jax import lax
import numpy as np

D_MODEL = 1024
BATCH = 1
SEQ = 16384
DEPTH = 1

HEAD_DIM = 64
N_HEADS_FOX = 8
N_HEADS_MOBA = 8
FOX_WIDTH = N_HEADS_FOX * HEAD_DIM
MOBA_WIDTH = N_HEADS_MOBA * HEAD_DIM
MIX_WIDTH = FOX_WIDTH + MOBA_WIDTH
IN_WIDTH = 3 * FOX_WIDTH + 3 * MOBA_WIDTH + N_HEADS_FOX
Q_BLOCK = 128
MOBA_BLOCK = 256
MOBA_TOPK = 3
MOBA_Q_CHUNK = 64
NUM_BUCKETS = 32
MAX_DISTANCE = 128
N_EXPERTS = 32
TOP_K = 4
D_EXPERT = D_MODEL
SWIGLU_LIMIT = 7.0
SWIGLU_ALPHA = 1.702
EXPERT_ROW_BLOCK = 128
RMS_EPS = 1e-6

kernel_name = "hybrid_fox_moba_moe_block"


def rms_norm(x, g):
    x32 = x.astype(jnp.float32)
    y = x32 * lax.rsqrt(jnp.mean(x32 * x32, axis=-1, keepdims=True) + RMS_EPS)
    return (y * g.astype(jnp.float32)).astype(x.dtype)


def t5_bucket(dist):
    dist = jnp.maximum(dist, 0)
    max_exact = NUM_BUCKETS // 2
    d = jnp.maximum(dist, 1).astype(jnp.float32)
    large = max_exact + (jnp.log(d / max_exact) / math.log(MAX_DISTANCE / max_exact)
                         * (NUM_BUCKETS - max_exact)).astype(jnp.int32)
    large = jnp.minimum(large, NUM_BUCKETS - 1)
    return jnp.where(dist < max_exact, dist, large)


def fox_attention(q, k, v, log_f):
    B, S, H, Dh = q.shape
    scale = Dh ** -0.5
    cum = jnp.cumsum(log_f, axis=1)
    cum_h = cum.transpose(0, 2, 1)
    kh = k.transpose(0, 2, 1, 3)
    vh = v.transpose(0, 2, 1, 3)
    nq = S // Q_BLOCK
    qb = q.reshape(B, nq, Q_BLOCK, H, Dh).transpose(1, 0, 3, 2, 4)
    cb = cum.reshape(B, nq, Q_BLOCK, H).transpose(1, 0, 3, 2)
    key_pos = jnp.arange(S)

    def block(args):
        i, qi, ci = args
        s = jnp.einsum('bhqd,bhkd->bhqk', qi, kh).astype(jnp.float32) * scale
        s = s + ci[..., None] - cum_h[:, :, None, :]
        qpos = i * Q_BLOCK + jnp.arange(Q_BLOCK)
        s = jnp.where(key_pos[None, :] <= qpos[:, None], s, -jnp.inf)
        p = jax.nn.softmax(s, axis=-1)
        return jnp.einsum('bhqk,bhkd->bhqd', p.astype(vh.dtype), vh)

    out = lax.map(block, (jnp.arange(nq), qb, cb))
    return out.transpose(1, 0, 3, 2, 4).reshape(B, S, H, Dh)


def moba_attention(q, k, v, rel_bias):
    B, S, H, Dh = q.shape
    scale = Dh ** -0.5
    s_pad = -(-S // MOBA_BLOCK) * MOBA_BLOCK
    pad = ((0, 0), (0, s_pad - S), (0, 0), (0, 0))
    q, k, v = jnp.pad(q, pad), jnp.pad(k, pad), jnp.pad(v, pad)
    nb = s_pad // MOBA_BLOCK
    k_sel_n = min(MOBA_TOPK, nb)
    kb = k.reshape(B, nb, MOBA_BLOCK, H, Dh).transpose(0, 3, 1, 2, 4)
    vb = v.reshape(B, nb, MOBA_BLOCK, H, Dh).transpose(0, 3, 1, 2, 4)
    kmean = jnp.mean(kb.astype(jnp.float32), axis=3)
    nq = s_pad // MOBA_Q_CHUNK
    qc = q.reshape(B, nq, MOBA_Q_CHUNK, H, Dh).transpose(1, 0, 2, 3, 4)
    bi = jnp.arange(B)[:, None, None, None]
    hi = jnp.arange(H)[None, None, :, None]
    hi5 = jnp.arange(H)[None, None, :, None, None]
    blk_pos = jnp.arange(MOBA_BLOCK)

    def chunk(args):
        i, qi = args
        qpos = i * MOBA_Q_CHUNK + jnp.arange(MOBA_Q_CHUNK)
        own = (i * MOBA_Q_CHUNK) // MOBA_BLOCK
        gate = jnp.einsum('bqhd,bhnd->bqhn', qi.astype(jnp.float32), kmean)
        gate = jnp.where(jnp.arange(nb) < own, gate, -jnp.inf)
        _, idx = lax.top_k(gate, k_sel_n)
        valid = jnp.arange(k_sel_n) < own
        ksel = kb[bi, hi, idx]
        vsel = vb[bi, hi, idx]
        s_sel = jnp.einsum('bqhd,bqhjpd->bqhjp', qi, ksel).astype(jnp.float32) * scale
        kpos_sel = idx[..., None] * MOBA_BLOCK + blk_pos
        dist_sel = qpos[None, :, None, None, None] - kpos_sel
        s_sel = s_sel + rel_bias[t5_bucket(dist_sel), hi5].astype(jnp.float32)
        s_sel = jnp.where(valid[None, None, None, :, None], s_sel, -jnp.inf)
        k_own = lax.dynamic_index_in_dim(kb, own, axis=2, keepdims=False)
        v_own = lax.dynamic_index_in_dim(vb, own, axis=2, keepdims=False)
        s_own = jnp.einsum('bqhd,bhpd->bqhp', qi, k_own).astype(jnp.float32) * scale
        dist_own = qpos[:, None] - (own * MOBA_BLOCK + blk_pos)[None, :]
        bias_own = rel_bias[t5_bucket(dist_own)].transpose(0, 2, 1)
        s_own = s_own + bias_own[None].astype(jnp.float32)
        s_own = jnp.where((dist_own >= 0)[None, :, None, :], s_own, -jnp.inf)
        logits = jnp.concatenate(
            [s_sel.reshape(B, MOBA_Q_CHUNK, H, k_sel_n * MOBA_BLOCK), s_own], axis=-1)
        p = jax.nn.softmax(logits, axis=-1).astype(vb.dtype)
        p_sel = p[..., :k_sel_n * MOBA_BLOCK].reshape(B, MOBA_Q_CHUNK, H, k_sel_n, MOBA_BLOCK)
        p_own = p[..., k_sel_n * MOBA_BLOCK:]
        return (jnp.einsum('bqhjp,bqhjpd->bqhd', p_sel, vsel)
                + jnp.einsum('bqhp,bhpd->bqhd', p_own, v_own))

    out = lax.map(chunk, (jnp.arange(nq), qc))
    return out.transpose(1, 0, 2, 3, 4).reshape(B, s_pad, H, Dh)[:, :S]


def clamped_swiglu(hdn):
    x_glu = jnp.minimum(hdn[..., :D_EXPERT], SWIGLU_LIMIT)
    x_lin = jnp.clip(hdn[..., D_EXPERT:], -SWIGLU_LIMIT, SWIGLU_LIMIT)
    return x_glu * jax.nn.sigmoid(SWIGLU_ALPHA * x_glu) * (x_lin + 1)


def moe_ffn(h, w_router, b_router, w_gate_up, b_gate_up, w_down, b_down):
    B, S, D = h.shape
    T = B * S
    xt = h.reshape(T, D)
    logits = (xt @ w_router).astype(jnp.float32) + b_router.astype(jnp.float32)
    top_v, top_i = lax.top_k(logits, TOP_K)
    gates = jax.nn.softmax(top_v, axis=-1)
    A = T * TOP_K
    e_flat = top_i.reshape(A)
    tok_flat = jnp.arange(A) // TOP_K
    g_flat = gates.reshape(A)
    order = jnp.argsort(e_flat)
    e_sorted, tok_sorted, g_sorted = e_flat[order], tok_flat[order], g_flat[order]
    counts = jnp.bincount(e_flat, length=N_EXPERTS)
    padded = ((counts + EXPERT_ROW_BLOCK - 1) // EXPERT_ROW_BLOCK) * EXPERT_ROW_BLOCK
    starts = jnp.cumsum(counts) - counts
    pends = jnp.cumsum(padded)
    pstarts = pends - padded
    dest = pstarts[e_sorted] + jnp.arange(A) - starts[e_sorted]
    n_rows = A + N_EXPERTS * EXPERT_ROW_BLOCK
    n_blk = n_rows // EXPERT_ROW_BLOCK
    rows = jnp.zeros((n_rows, D), xt.dtype).at[dest].set(xt[tok_sorted])
    block_e = jnp.minimum(
        jnp.searchsorted(pends, jnp.arange(n_blk) * EXPERT_ROW_BLOCK, side='right'), N_EXPERTS - 1)

    def expert_block(args):
        xb, e = args
        hdn = xb @ w_gate_up[e] + b_gate_up[e]
        return clamped_swiglu(hdn) @ w_down[e] + b_down[e]

    y = lax.map(expert_block, (rows.reshape(n_blk, EXPERT_ROW_BLOCK, D), block_e)).reshape(n_rows, D)
    out = jnp.zeros((T, D), y.dtype).at[tok_sorted].add(y[dest] * g_sorted[:, None].astype(y.dtype))
    return out.reshape(B, S, D)


def setup_inputs(seed: int = 0) -> dict:
    key = jax.random.key(seed)
    ks = jax.random.split(key, 20)
    nrm = jax.random.normal
    f32 = jnp.float32
    return {
        "x": nrm(ks[0], (BATCH, SEQ, D_MODEL), f32),
        "c": nrm(ks[1], (BATCH, D_MODEL), f32),
        "w_ada": nrm(ks[2], (DEPTH, D_MODEL, 6 * D_MODEL), f32) * (0.5 * D_MODEL ** -0.5),
        "b_ada": nrm(ks[3], (DEPTH, 6 * D_MODEL), f32) * 0.01,
        "g_pre_mix": 1.0 + 0.05 * nrm(ks[4], (DEPTH, D_MODEL), f32),
        "g_post_mix": 1.0 + 0.05 * nrm(ks[5], (DEPTH, D_MODEL), f32),
        "w_in": nrm(ks[6], (DEPTH, D_MODEL, IN_WIDTH), f32) * D_MODEL ** -0.5,
        "b_forget": nrm(ks[7], (DEPTH, N_HEADS_FOX), f32) * 0.1,
        "rel_bias": nrm(ks[8], (NUM_BUCKETS, N_HEADS_MOBA), f32) * 0.5,
        "w_out": nrm(ks[9], (DEPTH, MIX_WIDTH, D_MODEL), f32) * MIX_WIDTH ** -0.5,
        "g_pre_ffn": 1.0 + 0.05 * nrm(ks[10], (DEPTH, D_MODEL), f32),
        "g_post_ffn": 1.0 + 0.05 * nrm(ks[11], (DEPTH, D_MODEL), f32),
        "w_router": nrm(ks[12], (DEPTH, D_MODEL, N_EXPERTS), f32) * D_MODEL ** -0.5,
        "b_router": nrm(ks[13], (DEPTH, N_EXPERTS), f32) * 0.01,
        "w_gate_up": nrm(ks[14], (DEPTH, N_EXPERTS, D_MODEL, 2 * D_EXPERT), f32) * D_MODEL ** -0.5,
        "b_gate_up": nrm(ks[15], (DEPTH, N_EXPERTS, 2 * D_EXPERT), f32) * 0.01,
        "w_down": nrm(ks[16], (DEPTH, N_EXPERTS, D_EXPERT, D_MODEL), f32) * D_EXPERT ** -0.5,
        "b_down": nrm(ks[17], (DEPTH, N_EXPERTS, D_MODEL), f32) * 0.01,
    }


def reference(x, c, w_ada, b_ada, g_pre_mix, g_post_mix, w_in, b_forget, rel_bias, w_out,
              g_pre_ffn, g_post_ffn, w_router, b_router, w_gate_up, b_gate_up, w_down, b_down):
    B, S, D = x.shape
    cond = jax.nn.silu(c)
    for l in range(DEPTH):
        mod = cond @ w_ada[l] + b_ada[l]
        sh_m, sc_m, gt_m, sh_f, sc_f, gt_f = jnp.split(mod, 6, axis=-1)
        h = rms_norm(x, g_pre_mix[l]) * (1 + sc_m[:, None, :]) + sh_m[:, None, :]
        proj = h @ w_in[l]
        o = np.cumsum([FOX_WIDTH] * 3 + [MOBA_WIDTH] * 3)
        q_a, k_a, v_a = proj[..., :o[0]], proj[..., o[0]:o[1]], proj[..., o[1]:o[2]]
        q_b, k_b, v_b = proj[..., o[2]:o[3]], proj[..., o[3]:o[4]], proj[..., o[4]:o[5]]
        f_logit = proj[..., o[5]:]
        log_f = jax.nn.log_sigmoid(f_logit.astype(jnp.float32) + b_forget[l].astype(jnp.float32))
        hs_a = (B, S, N_HEADS_FOX, HEAD_DIM)
        hs_b = (B, S, N_HEADS_MOBA, HEAD_DIM)
        y_a = fox_attention(q_a.reshape(hs_a), k_a.reshape(hs_a), v_a.reshape(hs_a), log_f)
        y_b = moba_attention(q_b.reshape(hs_b), k_b.reshape(hs_b), v_b.reshape(hs_b), rel_bias)
        mix = jnp.concatenate([y_a.reshape(B, S, FOX_WIDTH), y_b.reshape(B, S, MOBA_WIDTH)], axis=-1)
        x = x + gt_m[:, None, :] * rms_norm(mix @ w_out[l], g_post_mix[l])
        h = rms_norm(x, g_pre_ffn[l]) * (1 + sc_f[:, None, :]) + sh_f[:, None, :]
        y = moe_ffn(h, w_router[l], b_router[l], w_gate_up[l], b_gate_up[l], w_down[l], b_down[l])
        x = x + gt_f[:, None, :] * rms_norm(y, g_post_ffn[l])
    return x
```

```python
import functools
import math

import numpy as np
import jax
import jax.numpy as jnp
from jax import lax
from jax.experimental import pallas as pl
from jax.experimental.pallas import tpu as pltpu

F32 = jnp.float32
BF16 = jnp.bfloat16
I32 = jnp.int32

HEAD_DIM = 64
N_HEADS_FOX = 8
N_HEADS_MOBA = 8
PAIR = 2 * HEAD_DIM
MOBA_BLOCK = 256
MOBA_TOPK = 3
NUM_BUCKETS = 32
MAX_DISTANCE = 128
N_EXPERTS = 32
TOP_K = 4
SWIGLU_LIMIT = 7.0
SWIGLU_ALPHA = 1.702
RMS_EPS = 1e-6
NEG = -(2.0 ** 100)
VMEM_LIMIT = 56 * 1024 * 1024

ROW_TILE = 512
FOX_TILE = 512
EXPERT_BLOCK = 256
COMBINE_TILE = 256

NT_DIMS = (((1,), (1,)), ((), ()))


def _cparams(sem):
    return pltpu.CompilerParams(dimension_semantics=sem, vmem_limit_bytes=VMEM_LIMIT)


def _rms(x):
    return x * lax.rsqrt(jnp.mean(x * x, axis=-1, keepdims=True) + RMS_EPS)


def _adaln_kernel(c_ref, w_ref, b_ref, o_ref):
    c = c_ref[...]
    cond = c * jax.nn.sigmoid(c)
    o_ref[...] = jnp.dot(cond, w_ref[...], preferred_element_type=F32,
                         precision=lax.Precision.HIGHEST) + b_ref[...]


def _adaln(c, w_ada, b_ada):
    d = c.shape[-1]
    n = w_ada.shape[-1]
    c8 = jnp.broadcast_to(c.reshape(1, d), (8, d))
    out = pl.pallas_call(
        _adaln_kernel,
        out_shape=jax.ShapeDtypeStruct((8, n), F32),
        grid=(n // d,),
        in_specs=[pl.BlockSpec((8, d), lambda j: (0, 0)),
                  pl.BlockSpec((d, d), lambda j: (0, j)),
                  pl.BlockSpec((1, d), lambda j: (0, j))],
        out_specs=pl.BlockSpec((8, d), lambda j: (0, j)),
        compiler_params=_cparams(("arbitrary",)),
        name="adaln",
    )(c8, w_ada, b_ada.reshape(1, n))
    return out[0:1]


def _inproj_kernel(x_ref, g_ref, sc_ref, sh_ref, w_ref, wf_ref, bf_ref,
                   qkv_ref, cum_ref, sel_ref, km_scr, carry_scr):
    i = pl.program_id(0)
    tm = x_ref.shape[0]
    nblk = km_scr.shape[0]

    @pl.when(i == 0)
    def _():
        km_scr[...] = jnp.zeros_like(km_scr)
        carry_scr[...] = jnp.zeros_like(carry_scr)

    x = x_ref[...]
    h = _rms(x) * g_ref[...] * (1.0 + sc_ref[...]) + sh_ref[...]
    hb = h.astype(BF16)

    width = N_HEADS_FOX * HEAD_DIM
    qb = kb = None
    for c in range(6):
        pc = jnp.dot(hb, w_ref[:, c * width:(c + 1) * width], preferred_element_type=F32)
        if c == 0 or c == 3:
            qkv_ref[:, c * width:(c + 1) * width] = (pc * (HEAD_DIM ** -0.5)).astype(BF16)
        else:
            qkv_ref[:, c * width:(c + 1) * width] = pc.astype(BF16)
        if c == 3:
            qb = pc
        if c == 4:
            kb = pc

    ft = lax.dot_general(wf_ref[...], h, NT_DIMS, preferred_element_type=F32,
                         precision=lax.Precision.HIGHEST)
    z = ft + bf_ref[...]
    logf = -(jnp.maximum(-z, 0.0) + jnp.log1p(jnp.exp(-jnp.abs(z))))
    lane = lax.broadcasted_iota(I32, logf.shape, 1)
    cs = logf
    sh = 1
    while sh < tm:
        cs = cs + jnp.where(lane >= sh, pltpu.roll(cs, sh, axis=1), 0.0)
        sh *= 2
    cs = cs + carry_scr[...]
    cum_ref[...] = cs
    carry_scr[...] = cs[:, tm - 1:tm]

    nb_tile = tm // MOBA_BLOCK
    for b in range(nb_tile):
        kmean = jnp.sum(kb[b * MOBA_BLOCK:(b + 1) * MOBA_BLOCK], axis=0, keepdims=True) * (1.0 / MOBA_BLOCK)
        km_scr[pl.ds(i * nb_tile + b, 1), :] = kmean

    km = km_scr[...]
    lane_h = lax.broadcasted_iota(I32, km.shape, 1) // HEAD_DIM
    blk = lax.broadcasted_iota(I32, (nblk, tm), 0)
    col = lax.broadcasted_iota(I32, (nblk, tm), 1)
    own = i * nb_tile + col // MOBA_BLOCK
    for hd in range(N_HEADS_MOBA):
        kmh = jnp.where(lane_h == hd, km, 0.0)
        g = lax.dot_general(kmh, qb, NT_DIMS, preferred_element_type=F32,
                            precision=lax.Precision.HIGHEST)
        g = jnp.where(blk < own, g, -jnp.inf)
        sel = jnp.zeros(g.shape, dtype=jnp.bool_)
        for _ in range(MOBA_TOPK):
            m = jnp.max(g, axis=0, keepdims=True)
            first = jnp.min(jnp.where(g == m, blk, nblk), axis=0, keepdims=True)
            pick = (blk == first) & (m > -jnp.inf)
            sel = sel | pick
            g = jnp.where(pick, -jnp.inf, g)
        sel_ref[hd] = jnp.where(sel, 0.0, NEG)


def _inproj(x2, g, sc, sh, w_qkv, w_f, b_f):
    s, d = x2.shape
    tm = ROW_TILE
    nblk = s // MOBA_BLOCK
    n = w_qkv.shape[1]
    return pl.pallas_call(
        _inproj_kernel,
        out_shape=(jax.ShapeDtypeStruct((s, n), BF16),
                   jax.ShapeDtypeStruct((N_HEADS_FOX, s), F32),
                   jax.ShapeDtypeStruct((N_HEADS_MOBA, nblk, s), F32)),
        grid=(s // tm,),
        in_specs=[pl.BlockSpec((tm, d), lambda i: (i, 0)),
                  pl.BlockSpec((1, d), lambda i: (0, 0)),
                  pl.BlockSpec((1, d), lambda i: (0, 0)),
                  pl.BlockSpec((1, d), lambda i: (0, 0)),
                  pl.BlockSpec((d, n), lambda i: (0, 0)),
                  pl.BlockSpec((N_HEADS_FOX, d), lambda i: (0, 0)),
                  pl.BlockSpec((N_HEADS_FOX, 1), lambda i: (0, 0))],
        out_specs=(pl.BlockSpec((tm, n), lambda i: (i, 0)),
                   pl.BlockSpec((N_HEADS_FOX, tm), lambda i: (0, i)),
                   pl.BlockSpec((N_HEADS_MOBA, nblk, tm), lambda i: (0, 0, i))),
        scratch_shapes=[pltpu.VMEM((nblk, N_HEADS_MOBA * HEAD_DIM), F32),
                        pltpu.VMEM((N_HEADS_FOX, 1), F32)],
        compiler_params=_cparams(("arbitrary",)),
        name="inproj",
    )(x2, g, sc, sh, w_qkv, w_f, b_f)


def _online_step(s, vt, m_ref, l_ref, a_ref, hd):
    m_old = m_ref[hd]
    m_new = jnp.maximum(m_old, jnp.max(s, axis=1, keepdims=True))
    alpha = jnp.exp(m_old - m_new)
    p = jnp.exp(s - m_new)
    l_ref[hd] = alpha * l_ref[hd] + jnp.sum(p, axis=1, keepdims=True)
    a_ref[hd] = alpha * a_ref[hd] + jnp.dot(p.astype(BF16), vt, preferred_element_type=F32)
    m_ref[hd] = m_new


def _init_state(m_ref, l_ref, a_ref):
    m_ref[...] = jnp.full(m_ref.shape, NEG, F32)
    l_ref[...] = jnp.zeros(l_ref.shape, F32)
    a_ref[...] = jnp.zeros(a_ref.shape, F32)


def _finish(o_ref, l_ref, a_ref):
    lane = lax.broadcasted_iota(I32, o_ref.shape, 1)
    o0 = a_ref[0] / l_ref[0]
    o1 = a_ref[1] / l_ref[1]
    o_ref[...] = jnp.where(lane < HEAD_DIM, o0, o1).astype(o_ref.dtype)


def _fox_kernel(q_ref, k_ref, v_ref, cum_ref, o_ref, m_ref, l_ref, a_ref):
    i = pl.program_id(1)
    t = q_ref.shape[0]
    q2 = q_ref[...]
    lane = lax.broadcasted_iota(I32, q2.shape, 1)
    qh = (jnp.where(lane < HEAD_DIM, q2, jnp.zeros_like(q2)),
          jnp.where(lane >= HEAD_DIM, q2, jnp.zeros_like(q2)))
    q0 = pl.multiple_of(i * t, t)
    cq = [cum_ref[0, hd:hd + 1, pl.ds(q0, t)][:, 0:1] for hd in range(2)]
    _init_state(m_ref, l_ref, a_ref)

    def tile(j, masked):
        k0 = pl.multiple_of(j * t, t)
        kt = k_ref[pl.ds(k0, t), :]
        vt = v_ref[pl.ds(k0, t), :]
        for hd in range(2):
            s = lax.dot_general(qh[hd], kt, NT_DIMS, preferred_element_type=F32)
            s = s + (cq[hd] - cum_ref[0, hd:hd + 1, pl.ds(k0, t)])
            if masked:
                r = lax.broadcasted_iota(I32, s.shape, 0)
                c = lax.broadcasted_iota(I32, s.shape, 1)
                s = jnp.where(c <= r, s, NEG)
            _online_step(s, vt, m_ref, l_ref, a_ref, hd)

    def body(j, carry):
        tile(j, False)
        return carry

    lax.fori_loop(0, i, body, 0)
    tile(i, True)
    _finish(o_ref, l_ref, a_ref)


def _fox(qkv, cum):
    s = qkv.shape[0]
    t = FOX_TILE
    npair = N_HEADS_FOX // 2
    cum3 = cum.reshape(npair, 2, s)
    return pl.pallas_call(
        _fox_kernel,
        out_shape=jax.ShapeDtypeStruct((s, npair * PAIR), BF16),
        grid=(npair, s // t),
        in_specs=[pl.BlockSpec((t, PAIR), lambda p, i: (i, p)),
                  pl.BlockSpec((s, PAIR), lambda p, i: (0, npair + p)),
                  pl.BlockSpec((s, PAIR), lambda p, i: (0, 2 * npair + p)),
                  pl.BlockSpec((1, 2, s), lambda p, i: (p, 0, 0))],
        out_specs=pl.BlockSpec((t, PAIR), lambda p, i: (i, p)),
        scratch_shapes=[pltpu.VMEM((2, t, 1), F32), pltpu.VMEM((2, t, 1), F32),
                        pltpu.VMEM((2, t, PAIR), F32)],
        compiler_params=_cparams(("arbitrary", "arbitrary")),
        name="fox",
    )(qkv, qkv, qkv, cum3)


def _t5_bucket_np(dist):
    dist = np.maximum(dist, 0)
    max_exact = NUM_BUCKETS // 2
    d = np.maximum(dist, 1).astype(np.float32)
    large = max_exact + (np.log(d / np.float32(max_exact)) / np.float32(math.log(MAX_DISTANCE / max_exact))
                         * np.float32(NUM_BUCKETS - max_exact)).astype(np.int32)
    large = np.minimum(large, NUM_BUCKETS - 1)
    return np.where(dist < max_exact, dist, large).astype(np.int32)


def _moba_kernel(rb_ref, q_ref, k_ref, v_ref, sel_ref, bkt_ref, o_ref,
                 m_ref, l_ref, a_ref, bias_scr):
    p = pl.program_id(0)
    i = pl.program_id(1)
    t = q_ref.shape[0]

    @pl.when(i == 0)
    def _():
        for hd in range(2):
            h = 2 * p + hd
            far = rb_ref[(NUM_BUCKETS - 1) * N_HEADS_MOBA + h]
            for w in range(2):
                bkt = bkt_ref[w]
                acc = jnp.zeros(bkt.shape, F32)
                for kk in range(NUM_BUCKETS):
                    acc = acc + jnp.where(bkt == kk, rb_ref[kk * N_HEADS_MOBA + h], 0.0)
                acc = acc - far
                if w == 1:
                    r = lax.broadcasted_iota(I32, acc.shape, 0)
                    c = lax.broadcasted_iota(I32, acc.shape, 1)
                    acc = jnp.where(c <= r, acc, NEG)
                bias_scr[hd, w] = acc

    q2 = q_ref[...]
    lane = lax.broadcasted_iota(I32, q2.shape, 1)
    lo = lane < HEAD_DIM
    zq = jnp.zeros_like(q2)
    qh = (jnp.where(lo, q2, zq), jnp.where(lo, zq, q2))
    nblk = sel_ref.shape[1]
    sel_t = []
    for hd in range(2):
        st = sel_ref[hd].T
        if nblk < HEAD_DIM:
            st = jnp.concatenate([st, jnp.zeros((t, HEAD_DIM - nblk), F32)], axis=1)
        sel_t.append(st)
    selp = jnp.concatenate(sel_t, axis=1).astype(BF16)
    zs = jnp.zeros_like(selp)
    qx = (jnp.concatenate([qh[0], jnp.where(lo, selp, zs)], axis=1),
          jnp.concatenate([qh[1], jnp.where(lo, zs, selp)], axis=1))
    _init_state(m_ref, l_ref, a_ref)

    def past_tile(j, w):
        k0 = pl.multiple_of(j * t, t)
        kt = k_ref[pl.ds(k0, t), :]
        vt = v_ref[pl.ds(k0, t), :]
        onehot = jnp.where((lane % HEAD_DIM) == j, 1.0, 0.0).astype(BF16)
        kx = jnp.concatenate([kt, onehot], axis=1)
        for hd in range(2):
            s = lax.dot_general(qx[hd], kx, NT_DIMS, preferred_element_type=F32)
            if w is not None:
                s = s + bias_scr[hd, w]
            _online_step(s, vt, m_ref, l_ref, a_ref, hd)

    def body(j, carry):
        past_tile(j, None)
        return carry

    lax.fori_loop(0, i - 1, body, 0)

    @pl.when(i >= 1)
    def _():
        past_tile(i - 1, 0)

    k0 = pl.multiple_of(i * t, t)
    kt = k_ref[pl.ds(k0, t), :]
    vt = v_ref[pl.ds(k0, t), :]
    for hd in range(2):
        s = lax.dot_general(qh[hd], kt, NT_DIMS, preferred_element_type=F32)
        s = s + bias_scr[hd, 1]
        _online_step(s, vt, m_ref, l_ref, a_ref, hd)
    _finish(o_ref, l_ref, a_ref)


def _moba(qkv, sel, rel_bias):
    s = qkv.shape[0]
    t = MOBA_BLOCK
    npair = N_HEADS_MOBA // 2
    nblk = s // t
    base = 3 * (N_HEADS_FOX // 2)
    a = np.arange(t)[:, None]
    b = np.arange(t)[None, :]
    bkt = jnp.asarray(np.stack([_t5_bucket_np(t + a - b), _t5_bucket_np(a - b)]))
    grid_spec = pltpu.PrefetchScalarGridSpec(
        num_scalar_prefetch=1,
        grid=(npair, nblk),
        in_specs=[pl.BlockSpec((t, PAIR), lambda p, i, rb: (i, base + p)),
                  pl.BlockSpec((s, PAIR), lambda p, i, rb: (0, base + npair + p)),
                  pl.BlockSpec((s, PAIR), lambda p, i, rb: (0, base + 2 * npair + p)),
                  pl.BlockSpec((2, nblk, t), lambda p, i, rb: (p, 0, i)),
                  pl.BlockSpec((2, t, t), lambda p, i, rb: (0, 0, 0))],
        out_specs=pl.BlockSpec((t, PAIR), lambda p, i, rb: (i, p)),
        scratch_shapes=[pltpu.VMEM((2, t, 1), F32), pltpu.VMEM((2, t, 1), F32),
                        pltpu.VMEM((2, t, PAIR), F32), pltpu.VMEM((2, 2, t, t), F32)],
    )
    return pl.pallas_call(
        _moba_kernel,
        out_shape=jax.ShapeDtypeStruct((s, npair * PAIR), BF16),
        grid_spec=grid_spec,
        compiler_params=_cparams(("arbitrary", "arbitrary")),
        name="moba",
    )(rel_bias.reshape(-1), qkv, qkv, qkv, sel, bkt)


def _outproj_kernel(ya_ref, yb_ref, x_ref, wa_ref, wb_ref, gpost_ref, gt_ref, gpre_ref,
                    sc_ref, sh_ref, wr_ref, br_ref,
                    x1_ref, h2_ref, idx_ref, rank_ref, gate_ref, cnt_ref, carry_scr):
    i = pl.program_id(0)
    tm = x_ref.shape[0]

    @pl.when(i == 0)
    def _():
        carry_scr[...] = jnp.zeros_like(carry_scr)

    y = (jnp.dot(ya_ref[...], wa_ref[...], preferred_element_type=F32)
         + jnp.dot(yb_ref[...], wb_ref[...], preferred_element_type=F32))
    x1 = x_ref[...] + gt_ref[...] * (_rms(y) * gpost_ref[...])
    x1_ref[...] = x1
    h2 = _rms(x1) * gpre_ref[...] * (1.0 + sc_ref[...]) + sh_ref[...]
    h2_ref[...] = h2

    logits = jnp.dot(h2, wr_ref[...], preferred_element_type=F32,
                     precision=lax.Precision.HIGHEST) + br_ref[...]
    ne = logits.shape[1]
    lane = lax.broadcasted_iota(I32, logits.shape, 1)
    lane4 = lax.broadcasted_iota(I32, (tm, TOP_K), 1)
    g = logits
    mask = jnp.zeros(logits.shape, F32)
    vals, idxs = [], []
    for _ in range(TOP_K):
        m = jnp.max(g, axis=1, keepdims=True)
        first = jnp.min(jnp.where(g == m, lane, ne), axis=1, keepdims=True)
        pick = lane == first
        mask = jnp.where(pick, 1.0, mask)
        g = jnp.where(pick, -jnp.inf, g)
        vals.append(m)
        idxs.append(first)
    ex = [jnp.exp(v - vals[0]) for v in vals]
    den = ex[0] + ex[1] + ex[2] + ex[3]

    r = lax.broadcasted_iota(I32, (tm, tm), 0)
    c = lax.broadcasted_iota(I32, (tm, tm), 1)
    tril = jnp.where(c < r, 1.0, 0.0).astype(BF16)
    before = jnp.dot(tril, mask.astype(BF16), preferred_element_type=F32) + carry_scr[...]
    total = carry_scr[...] + jnp.sum(mask, axis=0, keepdims=True)
    carry_scr[...] = total
    cnt_ref[...] = jnp.broadcast_to(total, cnt_ref.shape)

    ranks = [jnp.sum(jnp.where(lane == ix, before, 0.0), axis=1, keepdims=True) for ix in idxs]

    def pack4(cols):
        return jnp.where(lane4 == 0, cols[0],
                         jnp.where(lane4 == 1, cols[1], jnp.where(lane4 == 2, cols[2], cols[3])))

    idx_ref[...] = pack4(idxs)
    rank_ref[...] = pack4(ranks).astype(I32)
    gate_ref[...] = pack4([e / den for e in ex])


def _outproj(mix_a, mix_b, x2, w_a, w_b, gpost, gt, gpre, sc, sh, w_router, b_router):
    s, d = x2.shape
    tm = ROW_TILE
    ne = w_router.shape[1]
    wa = mix_a.shape[1]
    row = lambda i: (i, 0)
    fix = lambda i: (0, 0)
    vec = pl.BlockSpec((1, d), fix)
    return pl.pallas_call(
        _outproj_kernel,
        out_shape=(jax.ShapeDtypeStruct((s, d), F32),
                   jax.ShapeDtypeStruct((s, d), F32),
                   jax.ShapeDtypeStruct((s, TOP_K), I32),
                   jax.ShapeDtypeStruct((s, TOP_K), I32),
                   jax.ShapeDtypeStruct((s, TOP_K), F32),
                   jax.ShapeDtypeStruct((8, ne), F32)),
        grid=(s // tm,),
        in_specs=[pl.BlockSpec((tm, wa), row), pl.BlockSpec((tm, wa), row), pl.BlockSpec((tm, d), row),
                  pl.BlockSpec((wa, d), fix), pl.BlockSpec((wa, d), fix),
                  vec, vec, vec, vec, vec,
                  pl.BlockSpec((d, ne), fix), pl.BlockSpec((1, ne), fix)],
        out_specs=(pl.BlockSpec((tm, d), row), pl.BlockSpec((tm, d), row),
                   pl.BlockSpec((tm, TOP_K), row), pl.BlockSpec((tm, TOP_K), row),
                   pl.BlockSpec((tm, TOP_K), row), pl.BlockSpec((8, ne), fix)),
        scratch_shapes=[pltpu.VMEM((1, ne), F32)],
        compiler_params=_cparams(("arbitrary",)),
        name="outproj",
    )(mix_a, mix_b, x2, w_a, w_b, gpost, gt, gpre, sc, sh, w_router, b_router)


def _dispatch_kernel(pstart_ref, idx_ref, rank_ref, h_ref, xs_in_ref, xs_ref, sem):
    del xs_in_ref
    tm = h_ref.shape[0]

    def row_copy(r, k):
        a = r * TOP_K + k
        dst = pstart_ref[idx_ref[a]] + rank_ref[a]
        return pltpu.make_async_copy(h_ref.at[pl.ds(r, 1)], xs_ref.at[pl.ds(dst, 1)], sem)

    def start(r, carry):
        for k in range(TOP_K):
            row_copy(r, k).start()
        return carry

    def wait(r, carry):
        for k in range(TOP_K):
            row_copy(r, k).wait()
        return carry

    lax.fori_loop(0, tm, start, 0)
    lax.fori_loop(0, tm, wait, 0)


def _dispatch(pstart, idx_flat, rank_flat, h2, n_rows):
    s, d = h2.shape
    tm = COMBINE_TILE
    xs0 = jnp.zeros((n_rows, d), F32)
    grid_spec = pltpu.PrefetchScalarGridSpec(
        num_scalar_prefetch=1,
        grid=(s // tm,),
        in_specs=[pl.BlockSpec((tm * TOP_K,), lambda i, ps: (i,), memory_space=pltpu.SMEM),
                  pl.BlockSpec((tm * TOP_K,), lambda i, ps: (i,), memory_space=pltpu.SMEM),
                  pl.BlockSpec((tm, d), lambda i, ps: (i, 0)),
                  pl.BlockSpec(memory_space=pl.ANY)],
        out_specs=pl.BlockSpec(memory_space=pl.ANY),
        scratch_shapes=[pltpu.SemaphoreType.DMA],
    )
    return pl.pallas_call(
        _dispatch_kernel,
        out_shape=jax.ShapeDtypeStruct((n_rows, d), F32),
        grid_spec=grid_spec,
        input_output_aliases={4: 0},
        compiler_params=_cparams(("arbitrary",)),
        name="dispatch",
    )(pstart, idx_flat, rank_flat, h2, xs0)


def _experts_kernel(be_ref, nu_ref, xs_ref, wgu_ref, bgu_ref, wd_ref, bd_ref, y_ref,
                    wgu_bf, wd_bf):
    b = pl.program_id(0)
    d_exp = wd_ref.shape[1]
    prev = be_ref[jnp.maximum(b - 1, 0)]
    changed = (b == 0) | (be_ref[b] != prev)

    @pl.when((b < nu_ref[0]) & changed)
    def _():
        rows = 128

        def cast_gu(c, carry):
            r0 = pl.multiple_of(c * rows, rows)
            wgu_bf[pl.ds(r0, rows), :] = wgu_ref[0, pl.ds(r0, rows), :].astype(BF16)
            return carry

        def cast_d(c, carry):
            r0 = pl.multiple_of(c * rows, rows)
            wd_bf[pl.ds(r0, rows), :] = wd_ref[0, pl.ds(r0, rows), :].astype(BF16)
            return carry

        lax.fori_loop(0, wgu_ref.shape[1] // rows, cast_gu, 0)
        lax.fori_loop(0, wd_ref.shape[1] // rows, cast_d, 0)

    @pl.when(b < nu_ref[0])
    def _():
        xb = xs_ref[...].astype(BF16)
        hdn = jnp.dot(xb, wgu_bf[...], preferred_element_type=F32) + bgu_ref[0]
        x_glu = jnp.minimum(hdn[:, :d_exp], SWIGLU_LIMIT)
        x_lin = jnp.clip(hdn[:, d_exp:], -SWIGLU_LIMIT, SWIGLU_LIMIT)
        act = x_glu * jax.nn.sigmoid(SWIGLU_ALPHA * x_glu) * (x_lin + 1.0)
        y_ref[...] = jnp.dot(act.astype(BF16), wd_bf[...], preferred_element_type=F32) + bd_ref[0]

    @pl.when(b >= nu_ref[0])
    def _():
        y_ref[...] = jnp.zeros_like(y_ref)


def _experts(block_e, n_used, xs, w_gate_up, b_gate_up, w_down, b_down):
    n_rows, d = xs.shape
    bm = EXPERT_BLOCK
    n_blk = n_rows // bm
    ne, _, two_de = w_gate_up.shape
    de = w_down.shape[1]

    def blk(b, be, nu):
        return jnp.minimum(b, nu[0] - 1)

    grid_spec = pltpu.PrefetchScalarGridSpec(
        num_scalar_prefetch=2,
        grid=(n_blk,),
        in_specs=[pl.BlockSpec((bm, d), lambda b, be, nu: (blk(b, be, nu), 0)),
                  pl.BlockSpec((1, d, two_de), lambda b, be, nu: (be[blk(b, be, nu)], 0, 0)),
                  pl.BlockSpec((1, 1, two_de), lambda b, be, nu: (be[blk(b, be, nu)], 0, 0)),
                  pl.BlockSpec((1, de, d), lambda b, be, nu: (be[blk(b, be, nu)], 0, 0)),
                  pl.BlockSpec((1, 1, d), lambda b, be, nu: (be[blk(b, be, nu)], 0, 0))],
        out_specs=pl.BlockSpec((bm, d), lambda b, be, nu: (b, 0)),
        scratch_shapes=[pltpu.VMEM((d, two_de), BF16), pltpu.VMEM((de, d), BF16)],
    )
    return pl.pallas_call(
        _experts_kernel,
        out_shape=jax.ShapeDtypeStruct((n_rows, d), F32),
        grid_spec=grid_spec,
        compiler_params=_cparams(("arbitrary",)),
        name="experts",
    )(block_e, n_used, xs, w_gate_up, b_gate_up.reshape(ne, 1, two_de), w_down, b_down.reshape(ne, 1, d))


def _combine_kernel(pstart_ref, idx_ref, rank_ref, y_ref, gate_ref, x1_ref, gt_ref, gpost_ref,
                    o_ref, buf, sem):
    tm = x1_ref.shape[0]

    def row_copy(r, k):
        a = r * TOP_K + k
        src = pstart_ref[idx_ref[a]] + rank_ref[a]
        return pltpu.make_async_copy(y_ref.at[pl.ds(src, 1)], buf.at[k, pl.ds(r, 1)], sem)

    def start(r, carry):
        for k in range(TOP_K):
            row_copy(r, k).start()
        return carry

    def wait(r, carry):
        for k in range(TOP_K):
            row_copy(r, k).wait()
        return carry

    lax.fori_loop(0, tm, start, 0)
    lax.fori_loop(0, tm, wait, 0)

    gate = gate_ref[...]
    acc = gate[:, 0:1] * buf[0]
    for k in range(1, TOP_K):
        acc = acc + gate[:, k:k + 1] * buf[k]
    o_ref[...] = x1_ref[...] + gt_ref[...] * (_rms(acc) * gpost_ref[...])


def _combine(pstart, idx_flat, rank_flat, y, gate4, x1, gt, gpost):
    s, d = x1.shape
    tm = COMBINE_TILE
    grid_spec = pltpu.PrefetchScalarGridSpec(
        num_scalar_prefetch=1,
        grid=(s // tm,),
        in_specs=[pl.BlockSpec((tm * TOP_K,), lambda i, ps: (i,), memory_space=pltpu.SMEM),
                  pl.BlockSpec((tm * TOP_K,), lambda i, ps: (i,), memory_space=pltpu.SMEM),
                  pl.BlockSpec(memory_space=pl.ANY),
                  pl.BlockSpec((tm, TOP_K), lambda i, ps: (i, 0)),
                  pl.BlockSpec((tm, d), lambda i, ps: (i, 0)),
                  pl.BlockSpec((1, d), lambda i, ps: (0, 0)),
                  pl.BlockSpec((1, d), lambda i, ps: (0, 0))],
        out_specs=pl.BlockSpec((tm, d), lambda i, ps: (i, 0)),
        scratch_shapes=[pltpu.VMEM((TOP_K, tm, d), F32), pltpu.SemaphoreType.DMA],
    )
    return pl.pallas_call(
        _combine_kernel,
        out_shape=jax.ShapeDtypeStruct((s, d), F32),
        grid_spec=grid_spec,
        compiler_params=_cparams(("arbitrary",)),
        name="combine",
    )(pstart, idx_flat, rank_flat, y, gate4, x1, gt, gpost)


def _layer(x2, mod, g_pre_mix, g_post_mix, w_in, b_forget, rel_bias, w_out,
           g_pre_ffn, g_post_ffn, w_router, b_router, w_gate_up, b_gate_up, w_down, b_down):
    s, d = x2.shape
    sh_m, sc_m, gt_m, sh_f, sc_f, gt_f = [mod[:, k * d:(k + 1) * d] for k in range(6)]
    n_qkv = 3 * (N_HEADS_FOX + N_HEADS_MOBA) * HEAD_DIM
    fox_w = N_HEADS_FOX * HEAD_DIM

    w_qkv = w_in[:, :n_qkv].astype(BF16)
    w_f = w_in[:, n_qkv:].T
    qkv, cum, sel = _inproj(x2, g_pre_mix.reshape(1, d), sc_m, sh_m, w_qkv, w_f,
                            b_forget.reshape(-1, 1))
    y_a = _fox(qkv, cum)
    y_b = _moba(qkv, sel, rel_bias)

    w_out_bf = w_out.astype(BF16)
    x1, h2, idx4, rank4, gate4, cnt = _outproj(
        y_a, y_b, x2, w_out_bf[:fox_w], w_out_bf[fox_w:], g_post_mix.reshape(1, d), gt_m,
        g_pre_ffn.reshape(1, d), sc_f, sh_f, w_router, b_router.reshape(1, -1))

    bm = EXPERT_BLOCK
    counts = cnt[0].astype(I32)
    pblk = (counts + bm - 1) // bm
    pend_blk = jnp.cumsum(pblk)
    pstart = ((pend_blk - pblk) * bm).astype(I32)
    n_rows = s * TOP_K + N_EXPERTS * bm
    n_blk = n_rows // bm
    block_e = jnp.minimum(jnp.searchsorted(pend_blk, jnp.arange(n_blk), side='right'),
                          N_EXPERTS - 1).astype(I32)
    n_used = pend_blk[-1:].astype(I32)

    idx_flat = idx4.reshape(-1)
    rank_flat = rank4.reshape(-1)
    xs = _dispatch(pstart, idx_flat, rank_flat, h2, n_rows)
    y = _experts(block_e, n_used, xs, w_gate_up, b_gate_up, w_down, b_down)
    return _combine(pstart, idx_flat, rank_flat, y, gate4, x1, gt_f, g_post_ffn.reshape(1, d))


def kernel(x, c, w_ada, b_ada, g_pre_mix, g_post_mix, w_in, b_forget, rel_bias, w_out, g_pre_ffn, g_post_ffn, w_router, b_router, w_gate_up, b_gate_up, w_down, b_down):
    bsz, s, d = x.shape
    depth = w_ada.shape[0]
    outs = []
    for bi in range(bsz):
        x2 = x[bi]
        for l in range(depth):
            mod = _adaln(c[bi:bi + 1], w_ada[l], b_ada[l])
            x2 = _layer(x2, mod, g_pre_mix[l], g_post_mix[l], w_in[l], b_forget[l], rel_bias, w_out[l],
                        g_pre_ffn[l], g_post_ffn[l], w_router[l], b_router[l], w_gate_up[l], b_gate_up[l],
                        w_down[l], b_down[l])
        outs.append(x2)
    return jnp.stack(outs)
```

```python
import functools
import math

import numpy as np
import jax
import jax.numpy as jnp
from jax import lax
from jax.experimental import pallas as pl
from jax.experimental.pallas import tpu as pltpu

F32 = jnp.float32
BF16 = jnp.bfloat16
I32 = jnp.int32

HEAD_DIM = 64
N_HEADS_FOX = 8
N_HEADS_MOBA = 8
PAIR = 2 * HEAD_DIM
MOBA_BLOCK = 256
MOBA_TOPK = 3
NUM_BUCKETS = 32
MAX_DISTANCE = 128
N_EXPERTS = 32
TOP_K = 4
SWIGLU_LIMIT = 7.0
SWIGLU_ALPHA = 1.702
RMS_EPS = 1e-6
NEG = -(2.0 ** 100)
EXP_UNDERFLOW = 90.0
VMEM_LIMIT = 56 * 1024 * 1024

ROW_TILE = 512
FOX_TILE = 512
EXPERT_BLOCK = 256
COMBINE_TILE = 256

NT_DIMS = (((1,), (1,)), ((), ()))


def _cparams(sem):
    return pltpu.CompilerParams(dimension_semantics=sem, vmem_limit_bytes=VMEM_LIMIT)


def _rms(x):
    return x * lax.rsqrt(jnp.mean(x * x, axis=-1, keepdims=True) + RMS_EPS)


def _adaln_kernel(c_ref, w_ref, b_ref, o_ref):
    c = c_ref[...]
    cond = c * jax.nn.sigmoid(c)
    o_ref[...] = jnp.dot(cond, w_ref[...], preferred_element_type=F32,
                         precision=lax.Precision.HIGHEST) + b_ref[...]


def _adaln(c, w_ada, b_ada):
    d = c.shape[-1]
    n = w_ada.shape[-1]
    c8 = jnp.broadcast_to(c.reshape(1, d), (8, d))
    out = pl.pallas_call(
        _adaln_kernel,
        out_shape=jax.ShapeDtypeStruct((8, n), F32),
        grid=(n // d,),
        in_specs=[pl.BlockSpec((8, d), lambda j: (0, 0)),
                  pl.BlockSpec((d, d), lambda j: (0, j)),
                  pl.BlockSpec((1, d), lambda j: (0, j))],
        out_specs=pl.BlockSpec((8, d), lambda j: (0, j)),
        compiler_params=_cparams(("arbitrary",)),
        name="adaln",
    )(c8, w_ada, b_ada.reshape(1, n))
    return out[0:1]


def _inproj_kernel(x_ref, g_ref, sc_ref, sh_ref, w_ref, wf_ref, bf_ref,
                   qkv_ref, cum_ref, sel_ref, nrm_ref, km_scr, carry_scr):
    i = pl.program_id(0)
    tm = x_ref.shape[0]
    nblk = km_scr.shape[0]

    @pl.when(i == 0)
    def _():
        km_scr[...] = jnp.zeros_like(km_scr)
        carry_scr[...] = jnp.zeros_like(carry_scr)

    x = x_ref[...]
    h = _rms(x) * g_ref[...] * (1.0 + sc_ref[...]) + sh_ref[...]
    hb = h.astype(BF16)

    width = N_HEADS_FOX * HEAD_DIM
    hsel = jnp.where(lax.broadcasted_iota(I32, (width, N_HEADS_FOX), 0) // HEAD_DIM
                     == lax.broadcasted_iota(I32, (width, N_HEADS_FOX), 1), 1.0, 0.0)
    qb = kb = None
    for c in range(6):
        pc = jnp.dot(hb, w_ref[:, c * width:(c + 1) * width], preferred_element_type=F32)
        if c == 0 or c == 3:
            pcb = (pc * (HEAD_DIM ** -0.5)).astype(BF16)
        else:
            pcb = pc.astype(BF16)
        qkv_ref[:, c * width:(c + 1) * width] = pcb
        if c < 2:
            sq = pcb.astype(F32)
            n2 = jnp.dot(sq * sq, hsel, preferred_element_type=F32, precision=lax.Precision.HIGHEST)
            nrm_ref[0, c:c + 1, :] = jnp.max(n2, axis=0, keepdims=True)
        if c == 3:
            qb = pc
        if c == 4:
            kb = pc

    ft = lax.dot_general(wf_ref[...], h, NT_DIMS, preferred_element_type=F32,
                         precision=lax.Precision.HIGHEST)
    z = ft + bf_ref[...]
    logf = -(jnp.maximum(-z, 0.0) + jnp.log1p(jnp.exp(-jnp.abs(z))))
    lane = lax.broadcasted_iota(I32, logf.shape, 1)
    cs = logf
    sh = 1
    while sh < tm:
        cs = cs + jnp.where(lane >= sh, pltpu.roll(cs, sh, axis=1), 0.0)
        sh *= 2
    cs = cs + carry_scr[...]
    cum_ref[...] = cs
    carry_scr[...] = cs[:, tm - 1:tm]

    nb_tile = tm // MOBA_BLOCK
    for b in range(nb_tile):
        kmean = jnp.sum(kb[b * MOBA_BLOCK:(b + 1) * MOBA_BLOCK], axis=0, keepdims=True) * (1.0 / MOBA_BLOCK)
        km_scr[pl.ds(i * nb_tile + b, 1), :] = kmean

    km = km_scr[...]
    lane_h = lax.broadcasted_iota(I32, km.shape, 1) // HEAD_DIM
    blk = lax.broadcasted_iota(I32, (nblk, tm), 0)
    col = lax.broadcasted_iota(I32, (nblk, tm), 1)
    own = i * nb_tile + col // MOBA_BLOCK
    for hd in range(N_HEADS_MOBA):
        kmh = jnp.where(lane_h == hd, km, 0.0)
        g = lax.dot_general(kmh, qb, NT_DIMS, preferred_element_type=F32,
                            precision=lax.Precision.HIGHEST)
        g = jnp.where(blk < own, g, -jnp.inf)
        sel = jnp.zeros(g.shape, dtype=jnp.bool_)
        for _ in range(MOBA_TOPK):
            m = jnp.max(g, axis=0, keepdims=True)
            first = jnp.min(jnp.where(g == m, blk, nblk), axis=0, keepdims=True)
            pick = (blk == first) & (m > -jnp.inf)
            sel = sel | pick
            g = jnp.where(pick, -jnp.inf, g)
        sel_ref[hd] = jnp.where(sel, 0.0, NEG)


def _inproj(x2, g, sc, sh, w_qkv, w_f, b_f):
    s, d = x2.shape
    tm = ROW_TILE
    nblk = s // MOBA_BLOCK
    n = w_qkv.shape[1]
    return pl.pallas_call(
        _inproj_kernel,
        out_shape=(jax.ShapeDtypeStruct((s, n), BF16),
                   jax.ShapeDtypeStruct((N_HEADS_FOX, s), F32),
                   jax.ShapeDtypeStruct((N_HEADS_MOBA, nblk, s), F32),
                   jax.ShapeDtypeStruct((s // tm, 2, N_HEADS_FOX), F32)),
        grid=(s // tm,),
        in_specs=[pl.BlockSpec((tm, d), lambda i: (i, 0)),
                  pl.BlockSpec((1, d), lambda i: (0, 0)),
                  pl.BlockSpec((1, d), lambda i: (0, 0)),
                  pl.BlockSpec((1, d), lambda i: (0, 0)),
                  pl.BlockSpec((d, n), lambda i: (0, 0)),
                  pl.BlockSpec((N_HEADS_FOX, d), lambda i: (0, 0)),
                  pl.BlockSpec((N_HEADS_FOX, 1), lambda i: (0, 0))],
        out_specs=(pl.BlockSpec((tm, n), lambda i: (i, 0)),
                   pl.BlockSpec((N_HEADS_FOX, tm), lambda i: (0, i)),
                   pl.BlockSpec((N_HEADS_MOBA, nblk, tm), lambda i: (0, 0, i)),
                   pl.BlockSpec((1, 2, N_HEADS_FOX), lambda i: (i, 0, 0))),
        scratch_shapes=[pltpu.VMEM((nblk, N_HEADS_MOBA * HEAD_DIM), F32),
                        pltpu.VMEM((N_HEADS_FOX, 1), F32)],
        compiler_params=_cparams(("arbitrary",)),
        name="inproj",
    )(x2, g, sc, sh, w_qkv, w_f, b_f)


def _online_step(s, vt, m_ref, l_ref, a_ref, hd):
    m_old = m_ref[hd]
    m_new = jnp.maximum(m_old, jnp.max(s, axis=1, keepdims=True))
    alpha = jnp.exp(m_old - m_new)
    p = jnp.exp(s - m_new)
    l_ref[hd] = alpha * l_ref[hd] + jnp.sum(p, axis=1, keepdims=True)
    a_ref[hd] = alpha * a_ref[hd] + jnp.dot(p.astype(BF16), vt, preferred_element_type=F32)
    m_ref[hd] = m_new


def _init_state(m_ref, l_ref, a_ref):
    m_ref[...] = jnp.full(m_ref.shape, NEG, F32)
    l_ref[...] = jnp.zeros(l_ref.shape, F32)
    a_ref[...] = jnp.zeros(a_ref.shape, F32)


def _finish(o_ref, l_ref, a_ref):
    lane = lax.broadcasted_iota(I32, o_ref.shape, 1)
    o0 = a_ref[0] / l_ref[0]
    o1 = a_ref[1] / l_ref[1]
    o_ref[...] = jnp.where(lane < HEAD_DIM, o0, o1).astype(o_ref.dtype)


def _fox_kernel(jlo_ref, q_ref, k_ref, v_ref, cum_ref, o_ref, m_ref, l_ref, a_ref):
    i = pl.program_id(1)
    j_first = jlo_ref[pl.program_id(0) * pl.num_programs(1) + i]
    t = q_ref.shape[0]
    q2 = q_ref[...]
    lane = lax.broadcasted_iota(I32, q2.shape, 1)
    qh = (jnp.where(lane < HEAD_DIM, q2, jnp.zeros_like(q2)),
          jnp.where(lane >= HEAD_DIM, q2, jnp.zeros_like(q2)))
    q0 = pl.multiple_of(i * t, t)
    cq = [cum_ref[0, hd:hd + 1, pl.ds(q0, t)][:, 0:1] for hd in range(2)]
    _init_state(m_ref, l_ref, a_ref)

    def tile(j, masked):
        k0 = pl.multiple_of(j * t, t)
        kt = k_ref[pl.ds(k0, t), :]
        vt = v_ref[pl.ds(k0, t), :]
        for hd in range(2):
            s = lax.dot_general(qh[hd], kt, NT_DIMS, preferred_element_type=F32)
            s = s + (cq[hd] - cum_ref[0, hd:hd + 1, pl.ds(k0, t)])
            if masked:
                r = lax.broadcasted_iota(I32, s.shape, 0)
                c = lax.broadcasted_iota(I32, s.shape, 1)
                s = jnp.where(c <= r, s, NEG)
            _online_step(s, vt, m_ref, l_ref, a_ref, hd)

    def body(j, carry):
        tile(j, False)
        return carry

    lax.fori_loop(j_first, i, body, 0)
    tile(i, True)
    _finish(o_ref, l_ref, a_ref)


def _fox_first_tile(cum, nrm, t):
    nt = nrm.shape[0]
    cend = cum[:, t - 1::t]
    cprev = jnp.concatenate([jnp.zeros((cend.shape[0], 1), F32), cend[:, :-1]], axis=1)
    qn = jnp.sqrt(nrm[:, 0, :]).T
    kn = jnp.sqrt(nrm[:, 1, :]).T
    gap = (1.02 * qn[:, :, None] * (kn[:, None, :] + kn[:, :, None])
           + cprev[:, :, None] - cend[:, None, :])
    jj = jnp.arange(nt)[None, None, :]
    ii = jnp.arange(nt)[None, :, None]
    needed = (jj < ii) & jnp.logical_not(gap <= -EXP_UNDERFLOW)
    needed = needed[0::2] | needed[1::2]
    first = jnp.min(jnp.where(needed, jj, ii), axis=2)
    return first.reshape(-1).astype(I32)


def _fox(qkv, cum, nrm):
    s = qkv.shape[0]
    t = FOX_TILE
    npair = N_HEADS_FOX // 2
    cum3 = cum.reshape(npair, 2, s)
    jlo = _fox_first_tile(cum, nrm, t)
    grid_spec = pltpu.PrefetchScalarGridSpec(
        num_scalar_prefetch=1,
        grid=(npair, s // t),
        in_specs=[pl.BlockSpec((t, PAIR), lambda p, i, jl: (i, p)),
                  pl.BlockSpec((s, PAIR), lambda p, i, jl: (0, npair + p)),
                  pl.BlockSpec((s, PAIR), lambda p, i, jl: (0, 2 * npair + p)),
                  pl.BlockSpec((1, 2, s), lambda p, i, jl: (p, 0, 0))],
        out_specs=pl.BlockSpec((t, PAIR), lambda p, i, jl: (i, p)),
        scratch_shapes=[pltpu.VMEM((2, t, 1), F32), pltpu.VMEM((2, t, 1), F32),
                        pltpu.VMEM((2, t, PAIR), F32)],
    )
    return pl.pallas_call(
        _fox_kernel,
        out_shape=jax.ShapeDtypeStruct((s, npair * PAIR), BF16),
        grid_spec=grid_spec,
        compiler_params=_cparams(("arbitrary", "arbitrary")),
        name="fox",
    )(jlo, qkv, qkv, qkv, cum3)


def _t5_bucket_np(dist):
    dist = np.maximum(dist, 0)
    max_exact = NUM_BUCKETS // 2
    d = np.maximum(dist, 1).astype(np.float32)
    large = max_exact + (np.log(d / np.float32(max_exact)) / np.float32(math.log(MAX_DISTANCE / max_exact))
                         * np.float32(NUM_BUCKETS - max_exact)).astype(np.int32)
    large = np.minimum(large, NUM_BUCKETS - 1)
    return np.where(dist < max_exact, dist, large).astype(np.int32)


def _moba_kernel(rb_ref, q_ref, k_ref, v_ref, sel_ref, bkt_ref, o_ref,
                 m_ref, l_ref, a_ref, bias_scr):
    p = pl.program_id(0)
    i = pl.program_id(1)
    t = q_ref.shape[0]

    @pl.when(i == 0)
    def _():
        for hd in range(2):
            h = 2 * p + hd
            far = rb_ref[(NUM_BUCKETS - 1) * N_HEADS_MOBA + h]
            for w in range(2):
                bkt = bkt_ref[w]
                acc = jnp.zeros(bkt.shape, F32)
                for kk in range(NUM_BUCKETS):
                    acc = acc + jnp.where(bkt == kk, rb_ref[kk * N_HEADS_MOBA + h], 0.0)
                acc = acc - far
                if w == 1:
                    r = lax.broadcasted_iota(I32, acc.shape, 0)
                    c = lax.broadcasted_iota(I32, acc.shape, 1)
                    acc = jnp.where(c <= r, acc, NEG)
                bias_scr[hd, w] = acc

    q2 = q_ref[...]
    lane = lax.broadcasted_iota(I32, q2.shape, 1)
    lo = lane < HEAD_DIM
    zq = jnp.zeros_like(q2)
    qh = (jnp.where(lo, q2, zq), jnp.where(lo, zq, q2))
    nblk = sel_ref.shape[1]
    sel_t = []
    for hd in range(2):
        st = sel_ref[hd].T
        if nblk < HEAD_DIM:
            st = jnp.concatenate([st, jnp.zeros((t, HEAD_DIM - nblk), F32)], axis=1)
        sel_t.append(st)
    selp = jnp.concatenate(sel_t, axis=1).astype(BF16)
    zs = jnp.zeros_like(selp)
    qx = (jnp.concatenate([qh[0], jnp.where(lo, selp, zs)], axis=1),
          jnp.concatenate([qh[1], jnp.where(lo, zs, selp)], axis=1))
    _init_state(m_ref, l_ref, a_ref)

    def past_tile(j, w):
        k0 = pl.multiple_of(j * t, t)
        kt = k_ref[pl.ds(k0, t), :]
        vt = v_ref[pl.ds(k0, t), :]
        onehot = jnp.where((lane % HEAD_DIM) == j, 1.0, 0.0).astype(BF16)
        kx = jnp.concatenate([kt, onehot], axis=1)
        for hd in range(2):
            s = lax.dot_general(qx[hd], kx, NT_DIMS, preferred_element_type=F32)
            if w is not None:
                s = s + bias_scr[hd, w]
            _online_step(s, vt, m_ref, l_ref, a_ref, hd)

    def body(j, carry):
        past_tile(j, None)
        return carry

    lax.fori_loop(0, i - 1, body, 0)

    @pl.when(i >= 1)
    def _():
        past_tile(i - 1, 0)

    k0 = pl.multiple_of(i * t, t)
    kt = k_ref[pl.ds(k0, t), :]
    vt = v_ref[pl.ds(k0, t), :]
    for hd in range(2):
        s = lax.dot_general(qh[hd], kt, NT_DIMS, preferred_element_type=F32)
        s = s + bias_scr[hd, 1]
        _online_step(s, vt, m_ref, l_ref, a_ref, hd)
    _finish(o_ref, l_ref, a_ref)


def _moba(qkv, sel, rel_bias):
    s = qkv.shape[0]
    t = MOBA_BLOCK
    npair = N_HEADS_MOBA // 2
    nblk = s // t
    base = 3 * (N_HEADS_FOX // 2)
    a = np.arange(t)[:, None]
    b = np.arange(t)[None, :]
    bkt = jnp.asarray(np.stack([_t5_bucket_np(t + a - b), _t5_bucket_np(a - b)]))
    grid_spec = pltpu.PrefetchScalarGridSpec(
        num_scalar_prefetch=1,
        grid=(npair, nblk),
        in_specs=[pl.BlockSpec((t, PAIR), lambda p, i, rb: (i, base + p)),
                  pl.BlockSpec((s, PAIR), lambda p, i, rb: (0, base + npair + p)),
                  pl.BlockSpec((s, PAIR), lambda p, i, rb: (0, base + 2 * npair + p)),
                  pl.BlockSpec((2, nblk, t), lambda p, i, rb: (p, 0, i)),
                  pl.BlockSpec((2, t, t), lambda p, i, rb: (0, 0, 0))],
        out_specs=pl.BlockSpec((t, PAIR), lambda p, i, rb: (i, p)),
        scratch_shapes=[pltpu.VMEM((2, t, 1), F32), pltpu.VMEM((2, t, 1), F32),
                        pltpu.VMEM((2, t, PAIR), F32), pltpu.VMEM((2, 2, t, t), F32)],
    )
    return pl.pallas_call(
        _moba_kernel,
        out_shape=jax.ShapeDtypeStruct((s, npair * PAIR), BF16),
        grid_spec=grid_spec,
        compiler_params=_cparams(("arbitrary", "arbitrary")),
        name="moba",
    )(rel_bias.reshape(-1), qkv, qkv, qkv, sel, bkt)


def _outproj_kernel(ya_ref, yb_ref, x_ref, wa_ref, wb_ref, gpost_ref, gt_ref, gpre_ref,
                    sc_ref, sh_ref, wr_ref, br_ref,
                    x1_ref, h2_ref, idx_ref, rank_ref, gate_ref, cnt_ref, carry_scr):
    i = pl.program_id(0)
    tm = x_ref.shape[0]

    @pl.when(i == 0)
    def _():
        carry_scr[...] = jnp.zeros_like(carry_scr)

    y = (jnp.dot(ya_ref[...], wa_ref[...], preferred_element_type=F32)
         + jnp.dot(yb_ref[...], wb_ref[...], preferred_element_type=F32))
    x1 = x_ref[...] + gt_ref[...] * (_rms(y) * gpost_ref[...])
    x1_ref[...] = x1
    h2 = _rms(x1) * gpre_ref[...] * (1.0 + sc_ref[...]) + sh_ref[...]
    h2_ref[...] = h2

    logits = jnp.dot(h2, wr_ref[...], preferred_element_type=F32,
                     precision=lax.Precision.HIGHEST) + br_ref[...]
    ne = logits.shape[1]
    lane = lax.broadcasted_iota(I32, logits.shape, 1)
    lane4 = lax.broadcasted_iota(I32, (tm, TOP_K), 1)
    g = logits
    mask = jnp.zeros(logits.shape, F32)
    vals, idxs = [], []
    for _ in range(TOP_K):
        m = jnp.max(g, axis=1, keepdims=True)
        first = jnp.min(jnp.where(g == m, lane, ne), axis=1, keepdims=True)
        pick = lane == first
        mask = jnp.where(pick, 1.0, mask)
        g = jnp.where(pick, -jnp.inf, g)
        vals.append(m)
        idxs.append(first)
    ex = [jnp.exp(v - vals[0]) for v in vals]
    den = ex[0] + ex[1] + ex[2] + ex[3]

    r = lax.broadcasted_iota(I32, (tm, tm), 0)
    c = lax.broadcasted_iota(I32, (tm, tm), 1)
    tril = jnp.where(c < r, 1.0, 0.0).astype(BF16)
    before = jnp.dot(tril, mask.astype(BF16), preferred_element_type=F32) + carry_scr[...]
    total = carry_scr[...] + jnp.sum(mask, axis=0, keepdims=True)
    carry_scr[...] = total
    cnt_ref[...] = jnp.broadcast_to(total, cnt_ref.shape)

    ranks = [jnp.sum(jnp.where(lane == ix, before, 0.0), axis=1, keepdims=True) for ix in idxs]

    def pack4(cols):
        return jnp.where(lane4 == 0, cols[0],
                         jnp.where(lane4 == 1, cols[1], jnp.where(lane4 == 2, cols[2], cols[3])))

    idx_ref[...] = pack4(idxs)
    rank_ref[...] = pack4(ranks).astype(I32)
    gate_ref[...] = pack4([e / den for e in ex])


def _outproj(mix_a, mix_b, x2, w_a, w_b, gpost, gt, gpre, sc, sh, w_router, b_router):
    s, d = x2.shape
    tm = ROW_TILE
    ne = w_router.shape[1]
    wa = mix_a.shape[1]
    row = lambda i: (i, 0)
    fix = lambda i: (0, 0)
    vec = pl.BlockSpec((1, d), fix)
    return pl.pallas_call(
        _outproj_kernel,
        out_shape=(jax.ShapeDtypeStruct((s, d), F32),
                   jax.ShapeDtypeStruct((s, d), F32),
                   jax.ShapeDtypeStruct((s, TOP_K), I32),
                   jax.ShapeDtypeStruct((s, TOP_K), I32),
                   jax.ShapeDtypeStruct((s, TOP_K), F32),
                   jax.ShapeDtypeStruct((8, ne), F32)),
        grid=(s // tm,),
        in_specs=[pl.BlockSpec((tm, wa), row), pl.BlockSpec((tm, wa), row), pl.BlockSpec((tm, d), row),
                  pl.BlockSpec((wa, d), fix), pl.BlockSpec((wa, d), fix),
                  vec, vec, vec, vec, vec,
                  pl.BlockSpec((d, ne), fix), pl.BlockSpec((1, ne), fix)],
        out_specs=(pl.BlockSpec((tm, d), row), pl.BlockSpec((tm, d), row),
                   pl.BlockSpec((tm, TOP_K), row), pl.BlockSpec((tm, TOP_K), row),
                   pl.BlockSpec((tm, TOP_K), row), pl.BlockSpec((8, ne), fix)),
        scratch_shapes=[pltpu.VMEM((1, ne), F32)],
        compiler_params=_cparams(("arbitrary",)),
        name="outproj",
    )(mix_a, mix_b, x2, w_a, w_b, gpost, gt, gpre, sc, sh, w_router, b_router)


def _dispatch_kernel(pstart_ref, idx_ref, rank_ref, h_ref, xs_in_ref, xs_ref, sem):
    del xs_in_ref
    tm = h_ref.shape[0]

    def row_copy(r, k):
        a = r * TOP_K + k
        dst = pstart_ref[idx_ref[a]] + rank_ref[a]
        return pltpu.make_async_copy(h_ref.at[pl.ds(r, 1)], xs_ref.at[pl.ds(dst, 1)], sem)

    def start(r, carry):
        for k in range(TOP_K):
            row_copy(r, k).start()
        return carry

    def wait(r, carry):
        for k in range(TOP_K):
            row_copy(r, k).wait()
        return carry

    lax.fori_loop(0, tm, start, 0)
    lax.fori_loop(0, tm, wait, 0)


def _dispatch(pstart, idx_flat, rank_flat, h2, n_rows):
    s, d = h2.shape
    tm = COMBINE_TILE
    xs0 = jnp.zeros((n_rows, d), F32)
    grid_spec = pltpu.PrefetchScalarGridSpec(
        num_scalar_prefetch=1,
        grid=(s // tm,),
        in_specs=[pl.BlockSpec((tm * TOP_K,), lambda i, ps: (i,), memory_space=pltpu.SMEM),
                  pl.BlockSpec((tm * TOP_K,), lambda i, ps: (i,), memory_space=pltpu.SMEM),
                  pl.BlockSpec((tm, d), lambda i, ps: (i, 0)),
                  pl.BlockSpec(memory_space=pl.ANY)],
        out_specs=pl.BlockSpec(memory_space=pl.ANY),
        scratch_shapes=[pltpu.SemaphoreType.DMA],
    )
    return pl.pallas_call(
        _dispatch_kernel,
        out_shape=jax.ShapeDtypeStruct((n_rows, d), F32),
        grid_spec=grid_spec,
        input_output_aliases={4: 0},
        compiler_params=_cparams(("arbitrary",)),
        name="dispatch",
    )(pstart, idx_flat, rank_flat, h2, xs0)


def _experts_kernel(be_ref, nu_ref, xs_ref, wgu_ref, bgu_ref, wd_ref, bd_ref, y_ref,
                    wgu_bf, wd_bf):
    b = pl.program_id(0)
    d_exp = wd_ref.shape[1]
    prev = be_ref[jnp.maximum(b - 1, 0)]
    changed = (b == 0) | (be_ref[b] != prev)

    @pl.when((b < nu_ref[0]) & changed)
    def _():
        rows = 128

        def cast_gu(c, carry):
            r0 = pl.multiple_of(c * rows, rows)
            wgu_bf[pl.ds(r0, rows), :] = wgu_ref[0, pl.ds(r0, rows), :].astype(BF16)
            return carry

        def cast_d(c, carry):
            r0 = pl.multiple_of(c * rows, rows)
            wd_bf[pl.ds(r0, rows), :] = wd_ref[0, pl.ds(r0, rows), :].astype(BF16)
            return carry

        lax.fori_loop(0, wgu_ref.shape[1] // rows, cast_gu, 0)
        lax.fori_loop(0, wd_ref.shape[1] // rows, cast_d, 0)

    @pl.when(b < nu_ref[0])
    def _():
        xb = xs_ref[...].astype(BF16)
        hdn = jnp.dot(xb, wgu_bf[...], preferred_element_type=F32) + bgu_ref[0]
        x_glu = jnp.minimum(hdn[:, :d_exp], SWIGLU_LIMIT)
        x_lin = jnp.clip(hdn[:, d_exp:], -SWIGLU_LIMIT, SWIGLU_LIMIT)
        act = x_glu * jax.nn.sigmoid(SWIGLU_ALPHA * x_glu) * (x_lin + 1.0)
        y_ref[...] = jnp.dot(act.astype(BF16), wd_bf[...], preferred_element_type=F32) + bd_ref[0]

    @pl.when(b >= nu_ref[0])
    def _():
        y_ref[...] = jnp.zeros_like(y_ref)


def _experts(block_e, n_used, xs, w_gate_up, b_gate_up, w_down, b_down):
    n_rows, d = xs.shape
    bm = EXPERT_BLOCK
    n_blk = n_rows // bm
    ne, _, two_de = w_gate_up.shape
    de = w_down.shape[1]

    def blk(b, be, nu):
        return jnp.minimum(b, nu[0] - 1)

    grid_spec = pltpu.PrefetchScalarGridSpec(
        num_scalar_prefetch=2,
        grid=(n_blk,),
        in_specs=[pl.BlockSpec((bm, d), lambda b, be, nu: (blk(b, be, nu), 0)),
                  pl.BlockSpec((1, d, two_de), lambda b, be, nu: (be[blk(b, be, nu)], 0, 0)),
                  pl.BlockSpec((1, 1, two_de), lambda b, be, nu: (be[blk(b, be, nu)], 0, 0)),
                  pl.BlockSpec((1, de, d), lambda b, be, nu: (be[blk(b, be, nu)], 0, 0)),
                  pl.BlockSpec((1, 1, d), lambda b, be, nu: (be[blk(b, be, nu)], 0, 0))],
        out_specs=pl.BlockSpec((bm, d), lambda b, be, nu: (b, 0)),
        scratch_shapes=[pltpu.VMEM((d, two_de), BF16), pltpu.VMEM((de, d), BF16)],
    )
    return pl.pallas_call(
        _experts_kernel,
        out_shape=jax.ShapeDtypeStruct((n_rows, d), F32),
        grid_spec=grid_spec,
        compiler_params=_cparams(("arbitrary",)),
        name="experts",
    )(block_e, n_used, xs, w_gate_up, b_gate_up.reshape(ne, 1, two_de), w_down, b_down.reshape(ne, 1, d))


def _combine_kernel(pstart_ref, idx_ref, rank_ref, y_ref, gate_ref, x1_ref, gt_ref, gpost_ref,
                    o_ref, buf, sem):
    tm = x1_ref.shape[0]

    def row_copy(r, k):
        a = r * TOP_K + k
        src = pstart_ref[idx_ref[a]] + rank_ref[a]
        return pltpu.make_async_copy(y_ref.at[pl.ds(src, 1)], buf.at[k, pl.ds(r, 1)], sem)

    def start(r, carry):
        for k in range(TOP_K):
            row_copy(r, k).start()
        return carry

    def wait(r, carry):
        for k in range(TOP_K):
            row_copy(r, k).wait()
        return carry

    lax.fori_loop(0, tm, start, 0)
    lax.fori_loop(0, tm, wait, 0)

    gate = gate_ref[...]
    acc = gate[:, 0:1] * buf[0]
    for k in range(1, TOP_K):
        acc = acc + gate[:, k:k + 1] * buf[k]
    o_ref[...] = x1_ref[...] + gt_ref[...] * (_rms(acc) * gpost_ref[...])


def _combine(pstart, idx_flat, rank_flat, y, gate4, x1, gt, gpost):
    s, d = x1.shape
    tm = COMBINE_TILE
    grid_spec = pltpu.PrefetchScalarGridSpec(
        num_scalar_prefetch=1,
        grid=(s // tm,),
        in_specs=[pl.BlockSpec((tm * TOP_K,), lambda i, ps: (i,), memory_space=pltpu.SMEM),
                  pl.BlockSpec((tm * TOP_K,), lambda i, ps: (i,), memory_space=pltpu.SMEM),
                  pl.BlockSpec(memory_space=pl.ANY),
                  pl.BlockSpec((tm, TOP_K), lambda i, ps: (i, 0)),
                  pl.BlockSpec((tm, d), lambda i, ps: (i, 0)),
                  pl.BlockSpec((1, d), lambda i, ps: (0, 0)),
                  pl.BlockSpec((1, d), lambda i, ps: (0, 0))],
        out_specs=pl.BlockSpec((tm, d), lambda i, ps: (i, 0)),
        scratch_shapes=[pltpu.VMEM((TOP_K, tm, d), F32), pltpu.SemaphoreType.DMA],
    )
    return pl.pallas_call(
        _combine_kernel,
        out_shape=jax.ShapeDtypeStruct((s, d), F32),
        grid_spec=grid_spec,
        compiler_params=_cparams(("arbitrary",)),
        name="combine",
    )(pstart, idx_flat, rank_flat, y, gate4, x1, gt, gpost)


def _layer(x2, mod, g_pre_mix, g_post_mix, w_in, b_forget, rel_bias, w_out,
           g_pre_ffn, g_post_ffn, w_router, b_router, w_gate_up, b_gate_up, w_down, b_down):
    s, d = x2.shape
    sh_m, sc_m, gt_m, sh_f, sc_f, gt_f = [mod[:, k * d:(k + 1) * d] for k in range(6)]
    n_qkv = 3 * (N_HEADS_FOX + N_HEADS_MOBA) * HEAD_DIM
    fox_w = N_HEADS_FOX * HEAD_DIM

    w_qkv = w_in[:, :n_qkv].astype(BF16)
    w_f = w_in[:, n_qkv:].T
    qkv, cum, sel, nrm = _inproj(x2, g_pre_mix.reshape(1, d), sc_m, sh_m, w_qkv, w_f,
                                 b_forget.reshape(-1, 1))
    y_a = _fox(qkv, cum, nrm)
    y_b = _moba(qkv, sel, rel_bias)

    w_out_bf = w_out.astype(BF16)
    x1, h2, idx4, rank4, gate4, cnt = _outproj(
        y_a, y_b, x2, w_out_bf[:fox_w], w_out_bf[fox_w:], g_post_mix.reshape(1, d), gt_m,
        g_pre_ffn.reshape(1, d), sc_f, sh_f, w_router, b_router.reshape(1, -1))

    bm = EXPERT_BLOCK
    counts = cnt[0].astype(I32)
    pblk = (counts + bm - 1) // bm
    pend_blk = jnp.cumsum(pblk)
    pstart = ((pend_blk - pblk) * bm).astype(I32)
    n_rows = s * TOP_K + N_EXPERTS * bm
    n_blk = n_rows // bm
    block_e = jnp.minimum(jnp.searchsorted(pend_blk, jnp.arange(n_blk), side='right'),
                          N_EXPERTS - 1).astype(I32)
    n_used = pend_blk[-1:].astype(I32)

    idx_flat = idx4.reshape(-1)
    rank_flat = rank4.reshape(-1)
    xs = _dispatch(pstart, idx_flat, rank_flat, h2, n_rows)
    y = _experts(block_e, n_used, xs, w_gate_up, b_gate_up, w_down, b_down)
    return _combine(pstart, idx_flat, rank_flat, y, gate4, x1, gt_f, g_post_ffn.reshape(1, d))


def kernel(x, c, w_ada, b_ada, g_pre_mix, g_post_mix, w_in, b_forget, rel_bias, w_out, g_pre_ffn, g_post_ffn, w_router, b_router, w_gate_up, b_gate_up, w_down, b_down):
    bsz, s, d = x.shape
    depth = w_ada.shape[0]
    outs = []
    for bi in range(bsz):
        x2 = x[bi]
        for l in range(depth):
            mod = _adaln(c[bi:bi + 1], w_ada[l], b_ada[l])
            x2 = _layer(x2, mod, g_pre_mix[l], g_post_mix[l], w_in[l], b_forget[l], rel_bias, w_out[l],
                        g_pre_ffn[l], g_post_ffn[l], w_router[l], b_router[l], w_gate_up[l], b_gate_up[l],
                        w_down[l], b_down[l])
        outs.append(x2)
    return jnp.stack(outs)
```

```python
import functools
import math

import numpy as np
import jax
import jax.numpy as jnp
from jax import lax
from jax.experimental import pallas as pl
from jax.experimental.pallas import tpu as pltpu

F32 = jnp.float32
BF16 = jnp.bfloat16
I32 = jnp.int32

HEAD_DIM = 64
N_HEADS_FOX = 8
N_HEADS_MOBA = 8
PAIR = 2 * HEAD_DIM
MOBA_BLOCK = 256
MOBA_TOPK = 3
NUM_BUCKETS = 32
MAX_DISTANCE = 128
N_EXPERTS = 32
TOP_K = 4
SWIGLU_LIMIT = 7.0
SWIGLU_ALPHA = 1.702
RMS_EPS = 1e-6
NEG = -(2.0 ** 100)
M_INIT = -(2.0 ** 99)
LOG2E = math.log2(math.e)
SUM_ROWS = 16
EXP_UNDERFLOW = 90.0
VMEM_LIMIT = 56 * 1024 * 1024

ROW_TILE = 512
FOX_TILE = 512
EXPERT_BLOCK = 256
COMBINE_TILE = 256

NT_DIMS = (((1,), (1,)), ((), ()))


def _cparams(sem):
    return pltpu.CompilerParams(dimension_semantics=sem, vmem_limit_bytes=VMEM_LIMIT)


def _rms(x):
    return x * lax.rsqrt(jnp.mean(x * x, axis=-1, keepdims=True) + RMS_EPS)


def _adaln_kernel(c_ref, w_ref, b_ref, o_ref):
    c = c_ref[...]
    cond = c * jax.nn.sigmoid(c)
    o_ref[...] = jnp.dot(cond, w_ref[...], preferred_element_type=F32,
                         precision=lax.Precision.HIGHEST) + b_ref[...]


def _adaln(c, w_ada, b_ada):
    d = c.shape[-1]
    n = w_ada.shape[-1]
    c8 = jnp.broadcast_to(c.reshape(1, d), (8, d))
    out = pl.pallas_call(
        _adaln_kernel,
        out_shape=jax.ShapeDtypeStruct((8, n), F32),
        grid=(n // d,),
        in_specs=[pl.BlockSpec((8, d), lambda j: (0, 0)),
                  pl.BlockSpec((d, d), lambda j: (0, j)),
                  pl.BlockSpec((1, d), lambda j: (0, j))],
        out_specs=pl.BlockSpec((8, d), lambda j: (0, j)),
        compiler_params=_cparams(("arbitrary",)),
        name="adaln",
    )(c8, w_ada, b_ada.reshape(1, n))
    return out[0:1]


def _inproj_kernel(x_ref, g_ref, sc_ref, sh_ref, w_ref, wf_ref, bf_ref,
                   qkv_ref, cum_ref, sel_ref, nrm_ref, tr_ref, km_scr, carry_scr):
    i = pl.program_id(0)
    tm = x_ref.shape[0]
    nblk = km_scr.shape[0]

    @pl.when(i == 0)
    def _():
        km_scr[...] = jnp.zeros_like(km_scr)
        carry_scr[...] = jnp.zeros_like(carry_scr)

    x = x_ref[...]
    h = _rms(x) * g_ref[...] * (1.0 + sc_ref[...]) + sh_ref[...]
    hb = h.astype(BF16)

    width = N_HEADS_FOX * HEAD_DIM
    hsel = jnp.where(lax.broadcasted_iota(I32, (width, N_HEADS_FOX), 0) // HEAD_DIM
                     == lax.broadcasted_iota(I32, (width, N_HEADS_FOX), 1), 1.0, 0.0)
    qb = kb = None
    for c in range(6):
        pc = jnp.dot(hb, w_ref[:, c * width:(c + 1) * width], preferred_element_type=F32)
        if c == 0:
            pc_s = pc * (HEAD_DIM ** -0.5)
        if c < 3:
            pcb = (pc_s if c == 0 else pc).astype(BF16)
            qkv_ref[:, c * width:(c + 1) * width] = pcb
        if c < 2:
            sq = pcb.astype(F32)
            n2 = jnp.dot(sq * sq, hsel, preferred_element_type=F32, precision=lax.Precision.HIGHEST)
            nrm_ref[0, c:c + 1, :] = jnp.max(n2, axis=0, keepdims=True)
        if c == 3:
            qb = pc
            tr_ref[0:width, :] = (pc * (LOG2E * HEAD_DIM ** -0.5)).T.astype(BF16)
        if c == 4:
            kb = pc
            qkv_ref[:, 3 * width:4 * width] = pc.astype(BF16)
        if c == 5:
            tr_ref[width:2 * width, :] = pc.T.astype(BF16)

    ft = lax.dot_general(wf_ref[...], h, NT_DIMS, preferred_element_type=F32,
                         precision=lax.Precision.HIGHEST)
    z = ft + bf_ref[...]
    logf = -(jnp.maximum(-z, 0.0) + jnp.log1p(jnp.exp(-jnp.abs(z))))
    lane = lax.broadcasted_iota(I32, logf.shape, 1)
    cs = logf
    sh = 1
    while sh < tm:
        cs = cs + jnp.where(lane >= sh, pltpu.roll(cs, sh, axis=1), 0.0)
        sh *= 2
    cs = cs + carry_scr[...]
    cum_ref[...] = cs
    carry_scr[...] = cs[:, tm - 1:tm]

    nb_tile = tm // MOBA_BLOCK
    for b in range(nb_tile):
        kmean = jnp.sum(kb[b * MOBA_BLOCK:(b + 1) * MOBA_BLOCK], axis=0, keepdims=True) * (1.0 / MOBA_BLOCK)
        km_scr[pl.ds(i * nb_tile + b, 1), :] = kmean

    km = km_scr[...]
    lane_h = lax.broadcasted_iota(I32, km.shape, 1) // HEAD_DIM
    blk = lax.broadcasted_iota(I32, (nblk, tm), 0)
    col = lax.broadcasted_iota(I32, (nblk, tm), 1)
    own = i * nb_tile + col // MOBA_BLOCK
    for hd in range(N_HEADS_MOBA):
        kmh = jnp.where(lane_h == hd, km, 0.0)
        g = lax.dot_general(kmh, qb, NT_DIMS, preferred_element_type=F32,
                            precision=lax.Precision.HIGHEST)
        g = jnp.where(blk < own, g, -jnp.inf)
        sel = jnp.zeros(g.shape, dtype=jnp.bool_)
        for _ in range(MOBA_TOPK):
            m = jnp.max(g, axis=0, keepdims=True)
            first = jnp.min(jnp.where(g == m, blk, nblk), axis=0, keepdims=True)
            pick = (blk == first) & (m > -jnp.inf)
            sel = sel | pick
            g = jnp.where(pick, -jnp.inf, g)
        sel_ref[hd] = jnp.where(sel, 0.0, NEG)


def _inproj(x2, g, sc, sh, w_qkv, w_f, b_f):
    s, d = x2.shape
    tm = ROW_TILE
    nblk = s // MOBA_BLOCK
    n = w_qkv.shape[1]
    n_rows_out = 4 * N_HEADS_FOX * HEAD_DIM
    return pl.pallas_call(
        _inproj_kernel,
        out_shape=(jax.ShapeDtypeStruct((s, n_rows_out), BF16),
                   jax.ShapeDtypeStruct((N_HEADS_FOX, s), F32),
                   jax.ShapeDtypeStruct((N_HEADS_MOBA, nblk, s), F32),
                   jax.ShapeDtypeStruct((s // tm, 2, N_HEADS_FOX), F32),
                   jax.ShapeDtypeStruct((2 * N_HEADS_MOBA * HEAD_DIM, s), BF16)),
        grid=(s // tm,),
        in_specs=[pl.BlockSpec((tm, d), lambda i: (i, 0)),
                  pl.BlockSpec((1, d), lambda i: (0, 0)),
                  pl.BlockSpec((1, d), lambda i: (0, 0)),
                  pl.BlockSpec((1, d), lambda i: (0, 0)),
                  pl.BlockSpec((d, n), lambda i: (0, 0)),
                  pl.BlockSpec((N_HEADS_FOX, d), lambda i: (0, 0)),
                  pl.BlockSpec((N_HEADS_FOX, 1), lambda i: (0, 0))],
        out_specs=(pl.BlockSpec((tm, n_rows_out), lambda i: (i, 0)),
                   pl.BlockSpec((N_HEADS_FOX, tm), lambda i: (0, i)),
                   pl.BlockSpec((N_HEADS_MOBA, nblk, tm), lambda i: (0, 0, i)),
                   pl.BlockSpec((1, 2, N_HEADS_FOX), lambda i: (i, 0, 0)),
                   pl.BlockSpec((2 * N_HEADS_MOBA * HEAD_DIM, tm), lambda i: (0, i))),
        scratch_shapes=[pltpu.VMEM((nblk, N_HEADS_MOBA * HEAD_DIM), F32),
                        pltpu.VMEM((N_HEADS_FOX, 1), F32)],
        compiler_params=_cparams(("arbitrary",)),
        name="inproj",
    )(x2, g, sc, sh, w_qkv, w_f, b_f)


def _online_step(s, vt, m_ref, l_ref, a_ref, hd):
    m_old = m_ref[hd]
    m_new = jnp.maximum(m_old, jnp.max(s, axis=1, keepdims=True))
    alpha = jnp.exp(m_old - m_new)
    p = jnp.exp(s - m_new)
    l_ref[hd] = alpha * l_ref[hd] + jnp.sum(p, axis=1, keepdims=True)
    a_ref[hd] = alpha * a_ref[hd] + jnp.dot(p.astype(BF16), vt, preferred_element_type=F32)
    m_ref[hd] = m_new


def _init_state(m_ref, l_ref, a_ref):
    m_ref[...] = jnp.full(m_ref.shape, NEG, F32)
    l_ref[...] = jnp.zeros(l_ref.shape, F32)
    a_ref[...] = jnp.zeros(a_ref.shape, F32)


def _finish(o_ref, l_ref, a_ref):
    lane = lax.broadcasted_iota(I32, o_ref.shape, 1)
    o0 = a_ref[0] / l_ref[0]
    o1 = a_ref[1] / l_ref[1]
    o_ref[...] = jnp.where(lane < HEAD_DIM, o0, o1).astype(o_ref.dtype)


def _fox_kernel(jlo_ref, q_ref, k_ref, v_ref, cum_ref, o_ref, m_ref, l_ref, a_ref):
    i = pl.program_id(1)
    j_first = jlo_ref[pl.program_id(0) * pl.num_programs(1) + i]
    t = q_ref.shape[0]
    q2 = q_ref[...]
    lane = lax.broadcasted_iota(I32, q2.shape, 1)
    qh = (jnp.where(lane < HEAD_DIM, q2, jnp.zeros_like(q2)),
          jnp.where(lane >= HEAD_DIM, q2, jnp.zeros_like(q2)))
    q0 = pl.multiple_of(i * t, t)
    cq = [cum_ref[0, hd:hd + 1, pl.ds(q0, t)][:, 0:1] for hd in range(2)]
    _init_state(m_ref, l_ref, a_ref)

    def tile(j, masked):
        k0 = pl.multiple_of(j * t, t)
        kt = k_ref[pl.ds(k0, t), :]
        vt = v_ref[pl.ds(k0, t), :]
        for hd in range(2):
            s = lax.dot_general(qh[hd], kt, NT_DIMS, preferred_element_type=F32)
            s = s + (cq[hd] - cum_ref[0, hd:hd + 1, pl.ds(k0, t)])
            if masked:
                r = lax.broadcasted_iota(I32, s.shape, 0)
                c = lax.broadcasted_iota(I32, s.shape, 1)
                s = jnp.where(c <= r, s, NEG)
            _online_step(s, vt, m_ref, l_ref, a_ref, hd)

    def body(j, carry):
        tile(j, False)
        return carry

    lax.fori_loop(j_first, i, body, 0)
    tile(i, True)
    _finish(o_ref, l_ref, a_ref)


def _fox_first_tile(cum, nrm, t):
    nt = nrm.shape[0]
    cend = cum[:, t - 1::t]
    cprev = jnp.concatenate([jnp.zeros((cend.shape[0], 1), F32), cend[:, :-1]], axis=1)
    qn = jnp.sqrt(nrm[:, 0, :]).T
    kn = jnp.sqrt(nrm[:, 1, :]).T
    gap = (1.02 * qn[:, :, None] * (kn[:, None, :] + kn[:, :, None])
           + cprev[:, :, None] - cend[:, None, :])
    jj = jnp.arange(nt)[None, None, :]
    ii = jnp.arange(nt)[None, :, None]
    needed = (jj < ii) & jnp.logical_not(gap <= -EXP_UNDERFLOW)
    needed = needed[0::2] | needed[1::2]
    first = jnp.min(jnp.where(needed, jj, ii), axis=2)
    return first.reshape(-1).astype(I32)


def _fox(qkv, cum, nrm):
    s = qkv.shape[0]
    t = FOX_TILE
    npair = N_HEADS_FOX // 2
    cum3 = cum.reshape(npair, 2, s)
    jlo = _fox_first_tile(cum, nrm, t)
    grid_spec = pltpu.PrefetchScalarGridSpec(
        num_scalar_prefetch=1,
        grid=(npair, s // t),
        in_specs=[pl.BlockSpec((t, PAIR), lambda p, i, jl: (i, p)),
                  pl.BlockSpec((s, PAIR), lambda p, i, jl: (0, npair + p)),
                  pl.BlockSpec((s, PAIR), lambda p, i, jl: (0, 2 * npair + p)),
                  pl.BlockSpec((1, 2, s), lambda p, i, jl: (p, 0, 0))],
        out_specs=pl.BlockSpec((t, PAIR), lambda p, i, jl: (i, p)),
        scratch_shapes=[pltpu.VMEM((2, t, 1), F32), pltpu.VMEM((2, t, 1), F32),
                        pltpu.VMEM((2, t, PAIR), F32)],
    )
    return pl.pallas_call(
        _fox_kernel,
        out_shape=jax.ShapeDtypeStruct((s, npair * PAIR), BF16),
        grid_spec=grid_spec,
        compiler_params=_cparams(("arbitrary", "arbitrary")),
        name="fox",
    )(jlo, qkv, qkv, qkv, cum3)


def _t5_bucket_np(dist):
    dist = np.maximum(dist, 0)
    max_exact = NUM_BUCKETS // 2
    d = np.maximum(dist, 1).astype(np.float32)
    large = max_exact + (np.log(d / np.float32(max_exact)) / np.float32(math.log(MAX_DISTANCE / max_exact))
                         * np.float32(NUM_BUCKETS - max_exact)).astype(np.int32)
    large = np.minimum(large, NUM_BUCKETS - 1)
    return np.where(dist < max_exact, dist, large).astype(np.int32)


def _moba_kernel(rb_ref, qt_ref, k_ref, vt_ref, sel_ref, bkt_ref, o_ref,
                 m_ref, a_ref, bias_scr, sa_ref, sb_ref, pa_ref, pb_ref, ala_ref, alb_ref):
    p = pl.program_id(0)
    i = pl.program_id(1)
    t = qt_ref.shape[1]

    @pl.when(i == 0)
    def _():
        for hd in range(2):
            h = 2 * p + hd
            far = rb_ref[(NUM_BUCKETS - 1) * N_HEADS_MOBA + h]
            for w in range(2):
                bkt = bkt_ref[w]
                acc = jnp.zeros(bkt.shape, F32)
                for kk in range(NUM_BUCKETS):
                    acc = acc + jnp.where(bkt == kk, rb_ref[kk * N_HEADS_MOBA + h], 0.0)
                acc = (acc - far) * LOG2E
                if w == 1:
                    r = lax.broadcasted_iota(I32, acc.shape, 0)
                    c = lax.broadcasted_iota(I32, acc.shape, 1)
                    acc = jnp.where(r <= c, acc, NEG)
                bias_scr[hd, w] = acc

    qt = qt_ref[...]
    top = lax.broadcasted_iota(I32, qt.shape, 0) < HEAD_DIM
    zq = jnp.zeros_like(qt)
    qth = (jnp.where(top, qt, zq), jnp.where(top, zq, qt))
    m_ref[...] = jnp.full(m_ref.shape, M_INIT, F32)
    a_ref[...] = jnp.zeros(a_ref.shape, F32)

    er = lax.broadcasted_iota(I32, (SUM_ROWS, 2 * t), 0)
    ec = lax.broadcasted_iota(I32, (SUM_ROWS, 2 * t), 1)
    ones_rows = jnp.where(((er == 0) & (ec < t)) | ((er == 1) & (ec >= t)), 1.0, 0.0).astype(BF16)
    arow = lax.broadcasted_iota(I32, a_ref.shape, 0)
    head0_rows = (arow < HEAD_DIM) | (arow == PAIR)

    n_far = jnp.maximum(i - 1, 0)

    def produce(j, s_buf):
        k0 = pl.multiple_of(j * t, t)
        kt = k_ref[pl.ds(k0, t), :]
        for hd in range(2):
            s_buf[hd] = jnp.dot(kt, qth[hd], preferred_element_type=F32)

    def softmax(j, s_buf, p_buf, al_buf, valid, w):
        for hd in range(2):
            s = s_buf[hd]
            if w is not None:
                s = s + bias_scr[hd, w]
            smax = jnp.max(s, axis=0, keepdims=True)
            m_old = m_ref[hd]
            if valid is None:
                m_new = jnp.maximum(m_old, smax)
                shift = m_new
            else:
                selrow = jnp.where(valid, sel_ref[hd, pl.ds(j, 1), :], NEG)
                m_new = jnp.maximum(m_old, smax + selrow)
                shift = m_new - selrow
            al_buf[hd] = jnp.exp2(m_old - m_new)
            m_ref[hd] = m_new
            p_buf[hd * t:(hd + 1) * t, :] = jnp.exp2(s - shift).astype(BF16)

    def accumulate(j, p_buf, al_buf):
        k0 = pl.multiple_of(j * t, t)
        vt = vt_ref[:, pl.ds(k0, t)]
        zv = jnp.zeros_like(vt)
        vcat = jnp.concatenate([jnp.where(top, vt, zv), jnp.where(top, zv, vt)], axis=1)
        vcat = jnp.concatenate([vcat, ones_rows], axis=0)
        a_ref[...] = (a_ref[...] * jnp.where(head0_rows, al_buf[0], al_buf[1])
                      + jnp.dot(vcat, p_buf[...], preferred_element_type=F32))

    pb_ref[...] = jnp.zeros_like(pb_ref)
    alb_ref[...] = jnp.ones_like(alb_ref)
    produce(0, sa_ref)

    def body(g, carry):
        j0 = 2 * g
        produce(j0 + 1, sb_ref)
        softmax(j0, sa_ref, pa_ref, ala_ref, j0 < n_far, None)
        accumulate(jnp.maximum(j0 - 1, 0), pb_ref, alb_ref)
        produce(j0 + 2, sa_ref)
        accumulate(j0, pa_ref, ala_ref)
        softmax(j0 + 1, sb_ref, pb_ref, alb_ref, j0 + 1 < n_far, None)
        return carry

    n_pairs = (n_far + 1) // 2
    lax.fori_loop(0, n_pairs, body, 0)

    prev = jnp.maximum(i - 1, 0)
    produce(prev, sa_ref)
    produce(i, sb_ref)
    softmax(prev, sa_ref, pa_ref, ala_ref, i >= 1, 0)
    accumulate(jnp.maximum(2 * n_pairs - 1, 0), pb_ref, alb_ref)
    softmax(i, sb_ref, pb_ref, alb_ref, None, 1)
    accumulate(prev, pa_ref, ala_ref)
    accumulate(i, pb_ref, alb_ref)
    out_t = a_ref[0:PAIR, :] / jnp.where(top, a_ref[PAIR:PAIR + 1, :], a_ref[PAIR + 1:PAIR + 2, :])
    o_ref[...] = out_t.T.astype(o_ref.dtype)


def _moba(qkv, tr, sel, rel_bias):
    s = qkv.shape[0]
    t = MOBA_BLOCK
    npair = N_HEADS_MOBA // 2
    nblk = s // t
    kcol = 3 * (N_HEADS_FOX // 2)
    a = np.arange(t)[None, :]
    b = np.arange(t)[:, None]
    bkt = jnp.asarray(np.stack([_t5_bucket_np(t + a - b), _t5_bucket_np(a - b)]))
    grid_spec = pltpu.PrefetchScalarGridSpec(
        num_scalar_prefetch=1,
        grid=(npair, nblk),
        in_specs=[pl.BlockSpec((PAIR, t), lambda p, i, rb: (p, i)),
                  pl.BlockSpec((s, PAIR), lambda p, i, rb: (0, kcol + p)),
                  pl.BlockSpec((PAIR, s), lambda p, i, rb: (npair + p, 0)),
                  pl.BlockSpec((2, nblk, t), lambda p, i, rb: (p, 0, i)),
                  pl.BlockSpec((2, t, t), lambda p, i, rb: (0, 0, 0))],
        out_specs=pl.BlockSpec((t, PAIR), lambda p, i, rb: (i, p)),
        scratch_shapes=[pltpu.VMEM((2, 1, t), F32),
                        pltpu.VMEM((PAIR + SUM_ROWS, t), F32), pltpu.VMEM((2, 2, t, t), F32),
                        pltpu.VMEM((2, t, t), F32), pltpu.VMEM((2, t, t), F32),
                        pltpu.VMEM((2 * t, t), BF16), pltpu.VMEM((2 * t, t), BF16),
                        pltpu.VMEM((2, 1, t), F32), pltpu.VMEM((2, 1, t), F32)],
    )
    return pl.pallas_call(
        _moba_kernel,
        out_shape=jax.ShapeDtypeStruct((s, npair * PAIR), BF16),
        grid_spec=grid_spec,
        compiler_params=_cparams(("arbitrary", "arbitrary")),
        name="moba",
    )(rel_bias.reshape(-1), tr, qkv, tr, sel, bkt)


def _outproj_kernel(ya_ref, yb_ref, x_ref, wa_ref, wb_ref, gpost_ref, gt_ref, gpre_ref,
                    sc_ref, sh_ref, wr_ref, br_ref,
                    x1_ref, h2_ref, idx_ref, rank_ref, gate_ref, cnt_ref, carry_scr):
    i = pl.program_id(0)
    tm = x_ref.shape[0]

    @pl.when(i == 0)
    def _():
        carry_scr[...] = jnp.zeros_like(carry_scr)

    y = (jnp.dot(ya_ref[...], wa_ref[...], preferred_element_type=F32)
         + jnp.dot(yb_ref[...], wb_ref[...], preferred_element_type=F32))
    x1 = x_ref[...] + gt_ref[...] * (_rms(y) * gpost_ref[...])
    x1_ref[...] = x1
    h2 = _rms(x1) * gpre_ref[...] * (1.0 + sc_ref[...]) + sh_ref[...]
    h2_ref[...] = h2

    logits = jnp.dot(h2, wr_ref[...], preferred_element_type=F32,
                     precision=lax.Precision.HIGHEST) + br_ref[...]
    ne = logits.shape[1]
    lane = lax.broadcasted_iota(I32, logits.shape, 1)
    lane4 = lax.broadcasted_iota(I32, (tm, TOP_K), 1)
    g = logits
    mask = jnp.zeros(logits.shape, F32)
    vals, idxs = [], []
    for _ in range(TOP_K):
        m = jnp.max(g, axis=1, keepdims=True)
        first = jnp.min(jnp.where(g == m, lane, ne), axis=1, keepdims=True)
        pick = lane == first
        mask = jnp.where(pick, 1.0, mask)
        g = jnp.where(pick, -jnp.inf, g)
        vals.append(m)
        idxs.append(first)
    ex = [jnp.exp(v - vals[0]) for v in vals]
    den = ex[0] + ex[1] + ex[2] + ex[3]

    r = lax.broadcasted_iota(I32, (tm, tm), 0)
    c = lax.broadcasted_iota(I32, (tm, tm), 1)
    tril = jnp.where(c < r, 1.0, 0.0).astype(BF16)
    before = jnp.dot(tril, mask.astype(BF16), preferred_element_type=F32) + carry_scr[...]
    total = carry_scr[...] + jnp.sum(mask, axis=0, keepdims=True)
    carry_scr[...] = total
    cnt_ref[...] = jnp.broadcast_to(total, cnt_ref.shape)

    ranks = [jnp.sum(jnp.where(lane == ix, before, 0.0), axis=1, keepdims=True) for ix in idxs]

    def pack4(cols):
        return jnp.where(lane4 == 0, cols[0],
                         jnp.where(lane4 == 1, cols[1], jnp.where(lane4 == 2, cols[2], cols[3])))

    idx_ref[...] = pack4(idxs)
    rank_ref[...] = pack4(ranks).astype(I32)
    gate_ref[...] = pack4([e / den for e in ex])


def _outproj(mix_a, mix_b, x2, w_a, w_b, gpost, gt, gpre, sc, sh, w_router, b_router):
    s, d = x2.shape
    tm = ROW_TILE
    ne = w_router.shape[1]
    wa = mix_a.shape[1]
    row = lambda i: (i, 0)
    fix = lambda i: (0, 0)
    vec = pl.BlockSpec((1, d), fix)
    return pl.pallas_call(
        _outproj_kernel,
        out_shape=(jax.ShapeDtypeStruct((s, d), F32),
                   jax.ShapeDtypeStruct((s, d), F32),
                   jax.ShapeDtypeStruct((s, TOP_K), I32),
                   jax.ShapeDtypeStruct((s, TOP_K), I32),
                   jax.ShapeDtypeStruct((s, TOP_K), F32),
                   jax.ShapeDtypeStruct((8, ne), F32)),
        grid=(s // tm,),
        in_specs=[pl.BlockSpec((tm, wa), row), pl.BlockSpec((tm, wa), row), pl.BlockSpec((tm, d), row),
                  pl.BlockSpec((wa, d), fix), pl.BlockSpec((wa, d), fix),
                  vec, vec, vec, vec, vec,
                  pl.BlockSpec((d, ne), fix), pl.BlockSpec((1, ne), fix)],
        out_specs=(pl.BlockSpec((tm, d), row), pl.BlockSpec((tm, d), row),
                   pl.BlockSpec((tm, TOP_K), row), pl.BlockSpec((tm, TOP_K), row),
                   pl.BlockSpec((tm, TOP_K), row), pl.BlockSpec((8, ne), fix)),
        scratch_shapes=[pltpu.VMEM((1, ne), F32)],
        compiler_params=_cparams(("arbitrary",)),
        name="outproj",
    )(mix_a, mix_b, x2, w_a, w_b, gpost, gt, gpre, sc, sh, w_router, b_router)


def _dispatch_kernel(pstart_ref, idx_ref, rank_ref, h_ref, xs_in_ref, xs_ref, sem):
    del xs_in_ref
    tm = h_ref.shape[0]

    def row_copy(r, k):
        a = r * TOP_K + k
        dst = pstart_ref[idx_ref[a]] + rank_ref[a]
        return pltpu.make_async_copy(h_ref.at[pl.ds(r, 1)], xs_ref.at[pl.ds(dst, 1)], sem)

    def start(r, carry):
        for k in range(TOP_K):
            row_copy(r, k).start()
        return carry

    def wait(r, carry):
        for k in range(TOP_K):
            row_copy(r, k).wait()
        return carry

    lax.fori_loop(0, tm, start, 0)
    lax.fori_loop(0, tm, wait, 0)


def _dispatch(pstart, idx_flat, rank_flat, h2, n_rows):
    s, d = h2.shape
    tm = COMBINE_TILE
    xs0 = jnp.zeros((n_rows, d), F32)
    grid_spec = pltpu.PrefetchScalarGridSpec(
        num_scalar_prefetch=1,
        grid=(s // tm,),
        in_specs=[pl.BlockSpec((tm * TOP_K,), lambda i, ps: (i,), memory_space=pltpu.SMEM),
                  pl.BlockSpec((tm * TOP_K,), lambda i, ps: (i,), memory_space=pltpu.SMEM),
                  pl.BlockSpec((tm, d), lambda i, ps: (i, 0)),
                  pl.BlockSpec(memory_space=pl.ANY)],
        out_specs=pl.BlockSpec(memory_space=pl.ANY),
        scratch_shapes=[pltpu.SemaphoreType.DMA],
    )
    return pl.pallas_call(
        _dispatch_kernel,
        out_shape=jax.ShapeDtypeStruct((n_rows, d), F32),
        grid_spec=grid_spec,
        input_output_aliases={4: 0},
        compiler_params=_cparams(("arbitrary",)),
        name="dispatch",
    )(pstart, idx_flat, rank_flat, h2, xs0)


def _experts_kernel(be_ref, nu_ref, xs_ref, wgu_ref, bgu_ref, wd_ref, bd_ref, y_ref,
                    wgu_bf, wd_bf):
    b = pl.program_id(0)
    d_exp = wd_ref.shape[1]
    prev = be_ref[jnp.maximum(b - 1, 0)]
    changed = (b == 0) | (be_ref[b] != prev)

    @pl.when((b < nu_ref[0]) & changed)
    def _():
        rows = 128

        def cast_gu(c, carry):
            r0 = pl.multiple_of(c * rows, rows)
            wgu_bf[pl.ds(r0, rows), :] = wgu_ref[0, pl.ds(r0, rows), :].astype(BF16)
            return carry

        def cast_d(c, carry):
            r0 = pl.multiple_of(c * rows, rows)
            wd_bf[pl.ds(r0, rows), :] = wd_ref[0, pl.ds(r0, rows), :].astype(BF16)
            return carry

        lax.fori_loop(0, wgu_ref.shape[1] // rows, cast_gu, 0)
        lax.fori_loop(0, wd_ref.shape[1] // rows, cast_d, 0)

    @pl.when(b < nu_ref[0])
    def _():
        xb = xs_ref[...].astype(BF16)
        hdn = jnp.dot(xb, wgu_bf[...], preferred_element_type=F32) + bgu_ref[0]
        x_glu = jnp.minimum(hdn[:, :d_exp], SWIGLU_LIMIT)
        x_lin = jnp.clip(hdn[:, d_exp:], -SWIGLU_LIMIT, SWIGLU_LIMIT)
        act = x_glu * jax.nn.sigmoid(SWIGLU_ALPHA * x_glu) * (x_lin + 1.0)
        y_ref[...] = jnp.dot(act.astype(BF16), wd_bf[...], preferred_element_type=F32) + bd_ref[0]

    @pl.when(b >= nu_ref[0])
    def _():
        y_ref[...] = jnp.zeros_like(y_ref)


def _experts(block_e, n_used, xs, w_gate_up, b_gate_up, w_down, b_down):
    n_rows, d = xs.shape
    bm = EXPERT_BLOCK
    n_blk = n_rows // bm
    ne, _, two_de = w_gate_up.shape
    de = w_down.shape[1]

    def blk(b, be, nu):
        return jnp.minimum(b, nu[0] - 1)

    grid_spec = pltpu.PrefetchScalarGridSpec(
        num_scalar_prefetch=2,
        grid=(n_blk,),
        in_specs=[pl.BlockSpec((bm, d), lambda b, be, nu: (blk(b, be, nu), 0)),
                  pl.BlockSpec((1, d, two_de), lambda b, be, nu: (be[blk(b, be, nu)], 0, 0)),
                  pl.BlockSpec((1, 1, two_de), lambda b, be, nu: (be[blk(b, be, nu)], 0, 0)),
                  pl.BlockSpec((1, de, d), lambda b, be, nu: (be[blk(b, be, nu)], 0, 0)),
                  pl.BlockSpec((1, 1, d), lambda b, be, nu: (be[blk(b, be, nu)], 0, 0))],
        out_specs=pl.BlockSpec((bm, d), lambda b, be, nu: (b, 0)),
        scratch_shapes=[pltpu.VMEM((d, two_de), BF16), pltpu.VMEM((de, d), BF16)],
    )
    return pl.pallas_call(
        _experts_kernel,
        out_shape=jax.ShapeDtypeStruct((n_rows, d), F32),
        grid_spec=grid_spec,
        compiler_params=_cparams(("arbitrary",)),
        name="experts",
    )(block_e, n_used, xs, w_gate_up, b_gate_up.reshape(ne, 1, two_de), w_down, b_down.reshape(ne, 1, d))


def _combine_kernel(pstart_ref, idx_ref, rank_ref, y_ref, gate_ref, x1_ref, gt_ref, gpost_ref,
                    o_ref, buf, sem):
    tm = x1_ref.shape[0]

    def row_copy(r, k):
        a = r * TOP_K + k
        src = pstart_ref[idx_ref[a]] + rank_ref[a]
        return pltpu.make_async_copy(y_ref.at[pl.ds(src, 1)], buf.at[k, pl.ds(r, 1)], sem)

    def start(r, carry):
        for k in range(TOP_K):
            row_copy(r, k).start()
        return carry

    def wait(r, carry):
        for k in range(TOP_K):
            row_copy(r, k).wait()
        return carry

    lax.fori_loop(0, tm, start, 0)
    lax.fori_loop(0, tm, wait, 0)

    gate = gate_ref[...]
    acc = gate[:, 0:1] * buf[0]
    for k in range(1, TOP_K):
        acc = acc + gate[:, k:k + 1] * buf[k]
    o_ref[...] = x1_ref[...] + gt_ref[...] * (_rms(acc) * gpost_ref[...])


def _combine(pstart, idx_flat, rank_flat, y, gate4, x1, gt, gpost):
    s, d = x1.shape
    tm = COMBINE_TILE
    grid_spec = pltpu.PrefetchScalarGridSpec(
        num_scalar_prefetch=1,
        grid=(s // tm,),
        in_specs=[pl.BlockSpec((tm * TOP_K,), lambda i, ps: (i,), memory_space=pltpu.SMEM),
                  pl.BlockSpec((tm * TOP_K,), lambda i, ps: (i,), memory_space=pltpu.SMEM),
                  pl.BlockSpec(memory_space=pl.ANY),
                  pl.BlockSpec((tm, TOP_K), lambda i, ps: (i, 0)),
                  pl.BlockSpec((tm, d), lambda i, ps: (i, 0)),
                  pl.BlockSpec((1, d), lambda i, ps: (0, 0)),
                  pl.BlockSpec((1, d), lambda i, ps: (0, 0))],
        out_specs=pl.BlockSpec((tm, d), lambda i, ps: (i, 0)),
        scratch_shapes=[pltpu.VMEM((TOP_K, tm, d), F32), pltpu.SemaphoreType.DMA],
    )
    return pl.pallas_call(
        _combine_kernel,
        out_shape=jax.ShapeDtypeStruct((s, d), F32),
        grid_spec=grid_spec,
        compiler_params=_cparams(("arbitrary",)),
        name="combine",
    )(pstart, idx_flat, rank_flat, y, gate4, x1, gt, gpost)


def _layer(x2, mod, g_pre_mix, g_post_mix, w_in, b_forget, rel_bias, w_out,
           g_pre_ffn, g_post_ffn, w_router, b_router, w_gate_up, b_gate_up, w_down, b_down):
    s, d = x2.shape
    sh_m, sc_m, gt_m, sh_f, sc_f, gt_f = [mod[:, k * d:(k + 1) * d] for k in range(6)]
    n_qkv = 3 * (N_HEADS_FOX + N_HEADS_MOBA) * HEAD_DIM
    fox_w = N_HEADS_FOX * HEAD_DIM

    w_qkv = w_in[:, :n_qkv].astype(BF16)
    w_f = w_in[:, n_qkv:].T
    qkv, cum, sel, nrm, tr = _inproj(x2, g_pre_mix.reshape(1, d), sc_m, sh_m, w_qkv, w_f,
                                     b_forget.reshape(-1, 1))
    y_a = _fox(qkv, cum, nrm)
    y_b = _moba(qkv, tr, sel, rel_bias)

    w_out_bf = w_out.astype(BF16)
    x1, h2, idx4, rank4, gate4, cnt = _outproj(
        y_a, y_b, x2, w_out_bf[:fox_w], w_out_bf[fox_w:], g_post_mix.reshape(1, d), gt_m,
        g_pre_ffn.reshape(1, d), sc_f, sh_f, w_router, b_router.reshape(1, -1))

    bm = EXPERT_BLOCK
    counts = cnt[0].astype(I32)
    pblk = (counts + bm - 1) // bm
    pend_blk = jnp.cumsum(pblk)
    pstart = ((pend_blk - pblk) * bm).astype(I32)
    n_rows = s * TOP_K + N_EXPERTS * bm
    n_blk = n_rows // bm
    block_e = jnp.minimum(jnp.searchsorted(pend_blk, jnp.arange(n_blk), side='right'),
                          N_EXPERTS - 1).astype(I32)
    n_used = pend_blk[-1:].astype(I32)

    idx_flat = idx4.reshape(-1)
    rank_flat = rank4.reshape(-1)
    xs = _dispatch(pstart, idx_flat, rank_flat, h2, n_rows)
    y = _experts(block_e, n_used, xs, w_gate_up, b_gate_up, w_down, b_down)
    return _combine(pstart, idx_flat, rank_flat, y, gate4, x1, gt_f, g_post_ffn.reshape(1, d))


def kernel(x, c, w_ada, b_ada, g_pre_mix, g_post_mix, w_in, b_forget, rel_bias, w_out, g_pre_ffn, g_post_ffn, w_router, b_router, w_gate_up, b_gate_up, w_down, b_down):
    bsz, s, d = x.shape
    depth = w_ada.shape[0]
    outs = []
    for bi in range(bsz):
        x2 = x[bi]
        for l in range(depth):
            mod = _adaln(c[bi:bi + 1], w_ada[l], b_ada[l])
            x2 = _layer(x2, mod, g_pre_mix[l], g_post_mix[l], w_in[l], b_forget[l], rel_bias, w_out[l],
                        g_pre_ffn[l], g_post_ffn[l], w_router[l], b_router[l], w_gate_up[l], b_gate_up[l],
                        w_down[l], b_down[l])
        outs.append(x2)
    return jnp.stack(outs)
```

```python
import functools
import math

import numpy as np
import jax
import jax.numpy as jnp
from jax import lax
from jax.experimental import pallas as pl
from jax.experimental.pallas import tpu as pltpu

F32 = jnp.float32
BF16 = jnp.bfloat16
I32 = jnp.int32

HEAD_DIM = 64
N_HEADS_FOX = 8
N_HEADS_MOBA = 8
PAIR = 2 * HEAD_DIM
MOBA_BLOCK = 256
MOBA_TOPK = 3
NUM_BUCKETS = 32
MAX_DISTANCE = 128
N_EXPERTS = 32
TOP_K = 4
SWIGLU_LIMIT = 7.0
SWIGLU_ALPHA = 1.702
RMS_EPS = 1e-6
NEG = -(2.0 ** 100)
M_INIT = -(2.0 ** 99)
LOG2E = math.log2(math.e)
SUM_ROWS = 16
EXP_UNDERFLOW = 90.0
VMEM_LIMIT = 56 * 1024 * 1024

ROW_TILE = 512
FOX_TILE = 256
EXPERT_BLOCK = 256
COMBINE_TILE = 256

NT_DIMS = (((1,), (1,)), ((), ()))


def _cparams(sem):
    return pltpu.CompilerParams(dimension_semantics=sem, vmem_limit_bytes=VMEM_LIMIT)


def _rms(x):
    return x * lax.rsqrt(jnp.mean(x * x, axis=-1, keepdims=True) + RMS_EPS)


def _adaln_kernel(c_ref, w_ref, b_ref, o_ref):
    c = c_ref[...]
    cond = c * jax.nn.sigmoid(c)
    o_ref[...] = jnp.dot(cond, w_ref[...], preferred_element_type=F32,
                         precision=lax.Precision.HIGHEST) + b_ref[...]


def _adaln(c, w_ada, b_ada):
    d = c.shape[-1]
    n = w_ada.shape[-1]
    c8 = jnp.broadcast_to(c.reshape(1, d), (8, d))
    out = pl.pallas_call(
        _adaln_kernel,
        out_shape=jax.ShapeDtypeStruct((8, n), F32),
        grid=(n // d,),
        in_specs=[pl.BlockSpec((8, d), lambda j: (0, 0)),
                  pl.BlockSpec((d, d), lambda j: (0, j)),
                  pl.BlockSpec((1, d), lambda j: (0, j))],
        out_specs=pl.BlockSpec((8, d), lambda j: (0, j)),
        compiler_params=_cparams(("arbitrary",)),
        name="adaln",
    )(c8, w_ada, b_ada.reshape(1, n))
    return out[0:1]


def _inproj_kernel(x_ref, g_ref, sc_ref, sh_ref, w_ref, wf_ref, bf_ref,
                   qkv_ref, cum_ref, sel_ref, nrm_ref, tr_ref, km_scr, carry_scr):
    i = pl.program_id(0)
    tm = x_ref.shape[0]
    nblk = km_scr.shape[0]

    @pl.when(i == 0)
    def _():
        km_scr[...] = jnp.zeros_like(km_scr)
        carry_scr[...] = jnp.zeros_like(carry_scr)

    x = x_ref[...]
    h = _rms(x) * g_ref[...] * (1.0 + sc_ref[...]) + sh_ref[...]
    hb = h.astype(BF16)

    width = N_HEADS_FOX * HEAD_DIM
    hsel = jnp.where(lax.broadcasted_iota(I32, (width, N_HEADS_FOX), 0) // HEAD_DIM
                     == lax.broadcasted_iota(I32, (width, N_HEADS_FOX), 1), 1.0, 0.0)
    qb = kb = None
    for c in range(6):
        pc = jnp.dot(hb, w_ref[:, c * width:(c + 1) * width], preferred_element_type=F32)
        if c == 0:
            pc_s = pc * (HEAD_DIM ** -0.5)
        if c < 3:
            pcb = (pc_s if c == 0 else pc).astype(BF16)
            qkv_ref[:, c * width:(c + 1) * width] = pcb
        if c < 2:
            sq = pcb.astype(F32)
            n2 = jnp.dot((sq * sq).astype(BF16), hsel.astype(BF16), preferred_element_type=F32)
            nrm_ref[0, c:c + 1, :] = jnp.max(n2, axis=0, keepdims=True)
        if c == 3:
            qb = pc
            tr_ref[0:width, :] = (pc * (LOG2E * HEAD_DIM ** -0.5)).T.astype(BF16)
        if c == 4:
            kb = pc
            qkv_ref[:, 3 * width:4 * width] = pc.astype(BF16)
        if c == 5:
            tr_ref[width:2 * width, :] = pc.T.astype(BF16)

    ft = lax.dot_general(wf_ref[...], h, NT_DIMS, preferred_element_type=F32,
                         precision=lax.Precision.HIGHEST)
    z = ft + bf_ref[...]
    logf = -(jnp.maximum(-z, 0.0) + jnp.log1p(jnp.exp(-jnp.abs(z))))
    lane = lax.broadcasted_iota(I32, logf.shape, 1)
    cs = logf
    sh = 1
    while sh < tm:
        cs = cs + jnp.where(lane >= sh, pltpu.roll(cs, sh, axis=1), 0.0)
        sh *= 2
    cs = cs + carry_scr[...]
    cum_ref[...] = cs
    carry_scr[...] = cs[:, tm - 1:tm]

    nb_tile = tm // MOBA_BLOCK
    for b in range(nb_tile):
        kmean = jnp.sum(kb[b * MOBA_BLOCK:(b + 1) * MOBA_BLOCK], axis=0, keepdims=True) * (1.0 / MOBA_BLOCK)
        km_scr[pl.ds(i * nb_tile + b, 1), :] = kmean

    km = km_scr[...]
    lane_h = lax.broadcasted_iota(I32, km.shape, 1) // HEAD_DIM
    blk = lax.broadcasted_iota(I32, (nblk, tm), 0)
    col = lax.broadcasted_iota(I32, (nblk, tm), 1)
    own = i * nb_tile + col // MOBA_BLOCK
    for hd in range(N_HEADS_MOBA):
        kmh = jnp.where(lane_h == hd, km, 0.0)
        g = lax.dot_general(kmh, qb, NT_DIMS, preferred_element_type=F32,
                            precision=lax.Precision.HIGHEST)
        g = jnp.where(blk < own, g, -jnp.inf)
        sel = jnp.zeros(g.shape, dtype=jnp.bool_)
        for _ in range(MOBA_TOPK):
            m = jnp.max(g, axis=0, keepdims=True)
            first = jnp.min(jnp.where(g == m, blk, nblk), axis=0, keepdims=True)
            pick = (blk == first) & (m > -jnp.inf)
            sel = sel | pick
            g = jnp.where(pick, -jnp.inf, g)
        sel_ref[hd] = jnp.where(sel, 0.0, NEG)


def _inproj(x2, g, sc, sh, w_qkv, w_f, b_f):
    s, d = x2.shape
    tm = ROW_TILE
    nblk = s // MOBA_BLOCK
    n = w_qkv.shape[1]
    n_rows_out = 4 * N_HEADS_FOX * HEAD_DIM
    return pl.pallas_call(
        _inproj_kernel,
        out_shape=(jax.ShapeDtypeStruct((s, n_rows_out), BF16),
                   jax.ShapeDtypeStruct((N_HEADS_FOX, s), F32),
                   jax.ShapeDtypeStruct((N_HEADS_MOBA, nblk, s), F32),
                   jax.ShapeDtypeStruct((s // tm, 2, N_HEADS_FOX), F32),
                   jax.ShapeDtypeStruct((2 * N_HEADS_MOBA * HEAD_DIM, s), BF16)),
        grid=(s // tm,),
        in_specs=[pl.BlockSpec((tm, d), lambda i: (i, 0)),
                  pl.BlockSpec((1, d), lambda i: (0, 0)),
                  pl.BlockSpec((1, d), lambda i: (0, 0)),
                  pl.BlockSpec((1, d), lambda i: (0, 0)),
                  pl.BlockSpec((d, n), lambda i: (0, 0)),
                  pl.BlockSpec((N_HEADS_FOX, d), lambda i: (0, 0)),
                  pl.BlockSpec((N_HEADS_FOX, 1), lambda i: (0, 0))],
        out_specs=(pl.BlockSpec((tm, n_rows_out), lambda i: (i, 0)),
                   pl.BlockSpec((N_HEADS_FOX, tm), lambda i: (0, i)),
                   pl.BlockSpec((N_HEADS_MOBA, nblk, tm), lambda i: (0, 0, i)),
                   pl.BlockSpec((1, 2, N_HEADS_FOX), lambda i: (i, 0, 0)),
                   pl.BlockSpec((2 * N_HEADS_MOBA * HEAD_DIM, tm), lambda i: (0, i))),
        scratch_shapes=[pltpu.VMEM((nblk, N_HEADS_MOBA * HEAD_DIM), F32),
                        pltpu.VMEM((N_HEADS_FOX, 1), F32)],
        compiler_params=_cparams(("arbitrary",)),
        name="inproj",
    )(x2, g, sc, sh, w_qkv, w_f, b_f)


def _online_step(s, vt, m_ref, l_ref, a_ref, hd):
    m_old = m_ref[hd]
    m_new = jnp.maximum(m_old, jnp.max(s, axis=1, keepdims=True))
    alpha = jnp.exp(m_old - m_new)
    p = jnp.exp(s - m_new)
    l_ref[hd] = alpha * l_ref[hd] + jnp.sum(p, axis=1, keepdims=True)
    a_ref[hd] = alpha * a_ref[hd] + jnp.dot(p.astype(BF16), vt, preferred_element_type=F32)
    m_ref[hd] = m_new


def _init_state(m_ref, l_ref, a_ref):
    m_ref[...] = jnp.full(m_ref.shape, NEG, F32)
    l_ref[...] = jnp.zeros(l_ref.shape, F32)
    a_ref[...] = jnp.zeros(a_ref.shape, F32)


def _finish(o_ref, l_ref, a_ref):
    lane = lax.broadcasted_iota(I32, o_ref.shape, 1)
    o0 = a_ref[0] / l_ref[0]
    o1 = a_ref[1] / l_ref[1]
    o_ref[...] = jnp.where(lane < HEAD_DIM, o0, o1).astype(o_ref.dtype)


def _fox_kernel(jlo_ref, q_ref, k_ref, v_ref, cum_ref, o_ref, m_ref, l_ref, a_ref):
    i = pl.program_id(1)
    j_first = jlo_ref[pl.program_id(0) * pl.num_programs(1) + i]
    t = q_ref.shape[0]
    q2 = q_ref[...]
    lane = lax.broadcasted_iota(I32, q2.shape, 1)
    qh = (jnp.where(lane < HEAD_DIM, q2, jnp.zeros_like(q2)),
          jnp.where(lane >= HEAD_DIM, q2, jnp.zeros_like(q2)))
    q0 = pl.multiple_of(i * t, t)
    cq = [cum_ref[0, hd:hd + 1, pl.ds(q0, t)][:, 0:1] for hd in range(2)]
    _init_state(m_ref, l_ref, a_ref)

    def tile(j, masked):
        k0 = pl.multiple_of(j * t, t)
        kt = k_ref[pl.ds(k0, t), :]
        vt = v_ref[pl.ds(k0, t), :]
        for hd in range(2):
            s = lax.dot_general(qh[hd], kt, NT_DIMS, preferred_element_type=F32)
            s = s + (cq[hd] - cum_ref[0, hd:hd + 1, pl.ds(k0, t)])
            if masked:
                r = lax.broadcasted_iota(I32, s.shape, 0)
                c = lax.broadcasted_iota(I32, s.shape, 1)
                s = jnp.where(c <= r, s, NEG)
            _online_step(s, vt, m_ref, l_ref, a_ref, hd)

    def body(j, carry):
        tile(j, False)
        return carry

    lax.fori_loop(j_first, i, body, 0)
    tile(i, True)
    _finish(o_ref, l_ref, a_ref)


def _fox_first_tile(cum, nrm, t):
    cend = cum[:, t - 1::t]
    nt = cend.shape[1]
    cprev = jnp.concatenate([jnp.zeros((cend.shape[0], 1), F32), cend[:, :-1]], axis=1)
    rep = nt // nrm.shape[0]
    qn = jnp.repeat(jnp.sqrt(nrm[:, 0, :]).T, rep, axis=1)
    kn = jnp.repeat(jnp.sqrt(nrm[:, 1, :]).T, rep, axis=1)
    gap = (1.02 * qn[:, :, None] * (kn[:, None, :] + kn[:, :, None])
           + cprev[:, :, None] - cend[:, None, :])
    jj = jnp.arange(nt)[None, None, :]
    ii = jnp.arange(nt)[None, :, None]
    needed = (jj < ii) & jnp.logical_not(gap <= -EXP_UNDERFLOW)
    needed = needed[0::2] | needed[1::2]
    first = jnp.min(jnp.where(needed, jj, ii), axis=2)
    return first.reshape(-1).astype(I32)


def _fox(qkv, cum, nrm):
    s = qkv.shape[0]
    t = FOX_TILE
    npair = N_HEADS_FOX // 2
    cum3 = cum.reshape(npair, 2, s)
    jlo = _fox_first_tile(cum, nrm, t)
    grid_spec = pltpu.PrefetchScalarGridSpec(
        num_scalar_prefetch=1,
        grid=(npair, s // t),
        in_specs=[pl.BlockSpec((t, PAIR), lambda p, i, jl: (i, p)),
                  pl.BlockSpec((s, PAIR), lambda p, i, jl: (0, npair + p)),
                  pl.BlockSpec((s, PAIR), lambda p, i, jl: (0, 2 * npair + p)),
                  pl.BlockSpec((1, 2, s), lambda p, i, jl: (p, 0, 0))],
        out_specs=pl.BlockSpec((t, PAIR), lambda p, i, jl: (i, p)),
        scratch_shapes=[pltpu.VMEM((2, t, 1), F32), pltpu.VMEM((2, t, 1), F32),
                        pltpu.VMEM((2, t, PAIR), F32)],
    )
    return pl.pallas_call(
        _fox_kernel,
        out_shape=jax.ShapeDtypeStruct((s, npair * PAIR), BF16),
        grid_spec=grid_spec,
        compiler_params=_cparams(("arbitrary", "arbitrary")),
        name="fox",
    )(jlo, qkv, qkv, qkv, cum3)


def _t5_bucket_np(dist):
    dist = np.maximum(dist, 0)
    max_exact = NUM_BUCKETS // 2
    d = np.maximum(dist, 1).astype(np.float32)
    large = max_exact + (np.log(d / np.float32(max_exact)) / np.float32(math.log(MAX_DISTANCE / max_exact))
                         * np.float32(NUM_BUCKETS - max_exact)).astype(np.int32)
    large = np.minimum(large, NUM_BUCKETS - 1)
    return np.where(dist < max_exact, dist, large).astype(np.int32)


def _moba_kernel(rb_ref, qt_ref, k_ref, vt_ref, sel_ref, bkt_ref, o_ref,
                 m_ref, a_ref, bias_scr, sa_ref, sb_ref, pa_ref, pb_ref, ala_ref, alb_ref):
    p = pl.program_id(0)
    i = pl.program_id(1)
    t = qt_ref.shape[1]

    @pl.when(i == 0)
    def _():
        for hd in range(2):
            h = 2 * p + hd
            far = rb_ref[(NUM_BUCKETS - 1) * N_HEADS_MOBA + h]
            for w in range(2):
                bkt = bkt_ref[w]
                acc = jnp.zeros(bkt.shape, F32)
                for kk in range(NUM_BUCKETS):
                    acc = acc + jnp.where(bkt == kk, rb_ref[kk * N_HEADS_MOBA + h], 0.0)
                acc = (acc - far) * LOG2E
                if w == 1:
                    r = lax.broadcasted_iota(I32, acc.shape, 0)
                    c = lax.broadcasted_iota(I32, acc.shape, 1)
                    acc = jnp.where(r <= c, acc, NEG)
                bias_scr[hd, w] = acc

    qt = qt_ref[...]
    top = lax.broadcasted_iota(I32, qt.shape, 0) < HEAD_DIM
    zq = jnp.zeros_like(qt)
    qth = (jnp.where(top, qt, zq), jnp.where(top, zq, qt))
    m_ref[...] = jnp.full(m_ref.shape, M_INIT, F32)
    a_ref[...] = jnp.zeros(a_ref.shape, F32)

    er = lax.broadcasted_iota(I32, (SUM_ROWS, 2 * t), 0)
    ec = lax.broadcasted_iota(I32, (SUM_ROWS, 2 * t), 1)
    ones_rows = jnp.where(((er == 0) & (ec < t)) | ((er == 1) & (ec >= t)), 1.0, 0.0).astype(BF16)
    arow = lax.broadcasted_iota(I32, a_ref.shape, 0)
    head0_rows = (arow < HEAD_DIM) | (arow == PAIR)

    n_far = jnp.maximum(i - 1, 0)

    def produce(j, s_buf):
        k0 = pl.multiple_of(j * t, t)
        kt = k_ref[pl.ds(k0, t), :]
        for hd in range(2):
            s_buf[hd] = jnp.dot(kt, qth[hd], preferred_element_type=F32)

    def softmax(j, s_buf, p_buf, al_buf, valid, w):
        for hd in range(2):
            s = s_buf[hd]
            if w is not None:
                s = s + bias_scr[hd, w]
            smax = jnp.max(s, axis=0, keepdims=True)
            m_old = m_ref[hd]
            if valid is None:
                m_new = jnp.maximum(m_old, smax)
                shift = m_new
            else:
                selrow = jnp.where(valid, sel_ref[hd, pl.ds(j, 1), :], NEG)
                m_new = jnp.maximum(m_old, smax + selrow)
                shift = m_new - selrow
            al_buf[hd] = jnp.exp2(m_old - m_new)
            m_ref[hd] = m_new
            p_buf[hd * t:(hd + 1) * t, :] = jnp.exp2(s - shift).astype(BF16)

    def accumulate(j, p_buf, al_buf):
        k0 = pl.multiple_of(j * t, t)
        vt = vt_ref[:, pl.ds(k0, t)]
        zv = jnp.zeros_like(vt)
        vcat = jnp.concatenate([jnp.where(top, vt, zv), jnp.where(top, zv, vt)], axis=1)
        vcat = jnp.concatenate([vcat, ones_rows], axis=0)
        a_ref[...] = (a_ref[...] * jnp.where(head0_rows, al_buf[0], al_buf[1])
                      + jnp.dot(vcat, p_buf[...], preferred_element_type=F32))

    pb_ref[...] = jnp.zeros_like(pb_ref)
    alb_ref[...] = jnp.ones_like(alb_ref)
    produce(0, sa_ref)

    def body(g, carry):
        j0 = 2 * g
        produce(j0 + 1, sb_ref)
        softmax(j0, sa_ref, pa_ref, ala_ref, j0 < n_far, None)
        accumulate(jnp.maximum(j0 - 1, 0), pb_ref, alb_ref)
        produce(j0 + 2, sa_ref)
        accumulate(j0, pa_ref, ala_ref)
        softmax(j0 + 1, sb_ref, pb_ref, alb_ref, j0 + 1 < n_far, None)
        return carry

    n_pairs = (n_far + 1) // 2
    lax.fori_loop(0, n_pairs, body, 0)

    prev = jnp.maximum(i - 1, 0)
    produce(prev, sa_ref)
    produce(i, sb_ref)
    softmax(prev, sa_ref, pa_ref, ala_ref, i >= 1, 0)
    accumulate(jnp.maximum(2 * n_pairs - 1, 0), pb_ref, alb_ref)
    softmax(i, sb_ref, pb_ref, alb_ref, None, 1)
    accumulate(prev, pa_ref, ala_ref)
    accumulate(i, pb_ref, alb_ref)
    out_t = a_ref[0:PAIR, :] / jnp.where(top, a_ref[PAIR:PAIR + 1, :], a_ref[PAIR + 1:PAIR + 2, :])
    o_ref[...] = out_t.T.astype(o_ref.dtype)


def _moba(qkv, tr, sel, rel_bias):
    s = qkv.shape[0]
    t = MOBA_BLOCK
    npair = N_HEADS_MOBA // 2
    nblk = s // t
    kcol = 3 * (N_HEADS_FOX // 2)
    a = np.arange(t)[None, :]
    b = np.arange(t)[:, None]
    bkt = jnp.asarray(np.stack([_t5_bucket_np(t + a - b), _t5_bucket_np(a - b)]))
    grid_spec = pltpu.PrefetchScalarGridSpec(
        num_scalar_prefetch=1,
        grid=(npair, nblk),
        in_specs=[pl.BlockSpec((PAIR, t), lambda p, i, rb: (p, i)),
                  pl.BlockSpec((s, PAIR), lambda p, i, rb: (0, kcol + p)),
                  pl.BlockSpec((PAIR, s), lambda p, i, rb: (npair + p, 0)),
                  pl.BlockSpec((2, nblk, t), lambda p, i, rb: (p, 0, i)),
                  pl.BlockSpec((2, t, t), lambda p, i, rb: (0, 0, 0))],
        out_specs=pl.BlockSpec((t, PAIR), lambda p, i, rb: (i, p)),
        scratch_shapes=[pltpu.VMEM((2, 1, t), F32),
                        pltpu.VMEM((PAIR + SUM_ROWS, t), F32), pltpu.VMEM((2, 2, t, t), F32),
                        pltpu.VMEM((2, t, t), F32), pltpu.VMEM((2, t, t), F32),
                        pltpu.VMEM((2 * t, t), BF16), pltpu.VMEM((2 * t, t), BF16),
                        pltpu.VMEM((2, 1, t), F32), pltpu.VMEM((2, 1, t), F32)],
    )
    return pl.pallas_call(
        _moba_kernel,
        out_shape=jax.ShapeDtypeStruct((s, npair * PAIR), BF16),
        grid_spec=grid_spec,
        compiler_params=_cparams(("arbitrary", "arbitrary")),
        name="moba",
    )(rel_bias.reshape(-1), tr, qkv, tr, sel, bkt)


def _outproj_kernel(ya_ref, yb_ref, x_ref, wa_ref, wb_ref, gpost_ref, gt_ref, gpre_ref,
                    sc_ref, sh_ref, wr_ref, br_ref,
                    x1_ref, h2_ref, idx_ref, rank_ref, gate_ref, cnt_ref, carry_scr):
    i = pl.program_id(0)
    tm = x_ref.shape[0]

    @pl.when(i == 0)
    def _():
        carry_scr[...] = jnp.zeros_like(carry_scr)

    y = (jnp.dot(ya_ref[...], wa_ref[...], preferred_element_type=F32)
         + jnp.dot(yb_ref[...], wb_ref[...], preferred_element_type=F32))
    x1 = x_ref[...] + gt_ref[...] * (_rms(y) * gpost_ref[...])
    x1_ref[...] = x1
    h2 = _rms(x1) * gpre_ref[...] * (1.0 + sc_ref[...]) + sh_ref[...]
    h2_ref[...] = h2

    logits = jnp.dot(h2, wr_ref[...], preferred_element_type=F32,
                     precision=lax.Precision.HIGHEST) + br_ref[...]
    ne = logits.shape[1]
    lane = lax.broadcasted_iota(I32, logits.shape, 1)
    lane4 = lax.broadcasted_iota(I32, (tm, TOP_K), 1)
    g = logits
    mask = jnp.zeros(logits.shape, F32)
    vals, idxs = [], []
    for _ in range(TOP_K):
        m = jnp.max(g, axis=1, keepdims=True)
        first = jnp.min(jnp.where(g == m, lane, ne), axis=1, keepdims=True)
        pick = lane == first
        mask = jnp.where(pick, 1.0, mask)
        g = jnp.where(pick, -jnp.inf, g)
        vals.append(m)
        idxs.append(first)
    ex = [jnp.exp(v - vals[0]) for v in vals]
    den = ex[0] + ex[1] + ex[2] + ex[3]

    r = lax.broadcasted_iota(I32, (tm, tm), 0)
    c = lax.broadcasted_iota(I32, (tm, tm), 1)
    tril = jnp.where(c < r, 1.0, 0.0).astype(BF16)
    before = jnp.dot(tril, mask.astype(BF16), preferred_element_type=F32) + carry_scr[...]
    total = carry_scr[...] + jnp.sum(mask, axis=0, keepdims=True)
    carry_scr[...] = total
    cnt_ref[...] = jnp.broadcast_to(total, cnt_ref.shape)

    ranks = [jnp.sum(jnp.where(lane == ix, before, 0.0), axis=1, keepdims=True) for ix in idxs]

    def pack4(cols):
        return jnp.where(lane4 == 0, cols[0],
                         jnp.where(lane4 == 1, cols[1], jnp.where(lane4 == 2, cols[2], cols[3])))

    idx_ref[...] = pack4(idxs)
    rank_ref[...] = pack4(ranks).astype(I32)
    gate_ref[...] = pack4([e / den for e in ex])


def _outproj(mix_a, mix_b, x2, w_a, w_b, gpost, gt, gpre, sc, sh, w_router, b_router):
    s, d = x2.shape
    tm = ROW_TILE
    ne = w_router.shape[1]
    wa = mix_a.shape[1]
    row = lambda i: (i, 0)
    fix = lambda i: (0, 0)
    vec = pl.BlockSpec((1, d), fix)
    return pl.pallas_call(
        _outproj_kernel,
        out_shape=(jax.ShapeDtypeStruct((s, d), F32),
                   jax.ShapeDtypeStruct((s, d), F32),
                   jax.ShapeDtypeStruct((s, TOP_K), I32),
                   jax.ShapeDtypeStruct((s, TOP_K), I32),
                   jax.ShapeDtypeStruct((s, TOP_K), F32),
                   jax.ShapeDtypeStruct((8, ne), F32)),
        grid=(s // tm,),
        in_specs=[pl.BlockSpec((tm, wa), row), pl.BlockSpec((tm, wa), row), pl.BlockSpec((tm, d), row),
                  pl.BlockSpec((wa, d), fix), pl.BlockSpec((wa, d), fix),
                  vec, vec, vec, vec, vec,
                  pl.BlockSpec((d, ne), fix), pl.BlockSpec((1, ne), fix)],
        out_specs=(pl.BlockSpec((tm, d), row), pl.BlockSpec((tm, d), row),
                   pl.BlockSpec((tm, TOP_K), row), pl.BlockSpec((tm, TOP_K), row),
                   pl.BlockSpec((tm, TOP_K), row), pl.BlockSpec((8, ne), fix)),
        scratch_shapes=[pltpu.VMEM((1, ne), F32)],
        compiler_params=_cparams(("arbitrary",)),
        name="outproj",
    )(mix_a, mix_b, x2, w_a, w_b, gpost, gt, gpre, sc, sh, w_router, b_router)


def _dispatch_kernel(pstart_ref, pblk_ref, nu_ref, idx_ref, rank_ref, h_ref, xs_ref,
                     zero_scr, sem, zsem):
    tm = h_ref.shape[0]
    bm = zero_scr.shape[0]
    n_blk = xs_ref.shape[0] // bm

    @pl.when(pl.program_id(0) == 0)
    def _():
        zero_scr[...] = jnp.zeros_like(zero_scr)

        def zero_copy(row0):
            return pltpu.make_async_copy(zero_scr, xs_ref.at[pl.ds(pl.multiple_of(row0, bm), bm)], zsem)

        for phase in range(2):
            for e in range(N_EXPERTS):
                last = pstart_ref[e] + (pblk_ref[e] - 1) * bm
                tail = (n_blk - N_EXPERTS + e) * bm
                for cond, row0 in ((pblk_ref[e] > 0, last), (n_blk - N_EXPERTS + e >= nu_ref[0], tail)):
                    @pl.when(cond)
                    def _():
                        if phase == 0:
                            zero_copy(row0).start()
                        else:
                            zero_copy(row0).wait()

    def row_copy(r, k):
        a = r * TOP_K + k
        dst = pstart_ref[idx_ref[a]] + rank_ref[a]
        return pltpu.make_async_copy(h_ref.at[pl.ds(r, 1)], xs_ref.at[pl.ds(dst, 1)], sem)

    def start(r, carry):
        for k in range(TOP_K):
            row_copy(r, k).start(priority=k % 2)
        return carry

    def wait(r, carry):
        for k in range(TOP_K):
            row_copy(r, k).wait()
        return carry

    lax.fori_loop(0, tm, start, 0)
    lax.fori_loop(0, tm, wait, 0)


def _dispatch(pstart, pblk, n_used, idx_flat, rank_flat, h2, n_rows):
    s, d = h2.shape
    tm = COMBINE_TILE
    grid_spec = pltpu.PrefetchScalarGridSpec(
        num_scalar_prefetch=3,
        grid=(s // tm,),
        in_specs=[pl.BlockSpec((tm * TOP_K,), lambda i, *_: (i,), memory_space=pltpu.SMEM),
                  pl.BlockSpec((tm * TOP_K,), lambda i, *_: (i,), memory_space=pltpu.SMEM),
                  pl.BlockSpec((tm, d), lambda i, *_: (i, 0))],
        out_specs=pl.BlockSpec(memory_space=pl.ANY),
        scratch_shapes=[pltpu.VMEM((EXPERT_BLOCK, d), F32), pltpu.SemaphoreType.DMA,
                        pltpu.SemaphoreType.DMA],
    )
    return pl.pallas_call(
        _dispatch_kernel,
        out_shape=jax.ShapeDtypeStruct((n_rows, d), F32),
        grid_spec=grid_spec,
        compiler_params=_cparams(("arbitrary",)),
        name="dispatch",
    )(pstart, pblk, n_used, idx_flat, rank_flat, h2)


def _experts_kernel(be_ref, nu_ref, xs_ref, wgu_ref, bgu_ref, wd_ref, bd_ref, y_ref,
                    wgu_bf, wd_bf):
    b = pl.program_id(0)
    d_exp = wd_ref.shape[1]
    prev = be_ref[jnp.maximum(b - 1, 0)]
    changed = (b == 0) | (be_ref[b] != prev)

    @pl.when((b < nu_ref[0]) & changed)
    def _():
        rows = 128

        def cast_gu(c, carry):
            r0 = pl.multiple_of(c * rows, rows)
            wgu_bf[pl.ds(r0, rows), :] = wgu_ref[0, pl.ds(r0, rows), :].astype(BF16)
            return carry

        def cast_d(c, carry):
            r0 = pl.multiple_of(c * rows, rows)
            wd_bf[pl.ds(r0, rows), :] = wd_ref[0, pl.ds(r0, rows), :].astype(BF16)
            return carry

        lax.fori_loop(0, wgu_ref.shape[1] // rows, cast_gu, 0)
        lax.fori_loop(0, wd_ref.shape[1] // rows, cast_d, 0)

    @pl.when(b < nu_ref[0])
    def _():
        xb = xs_ref[...].astype(BF16)
        hdn = jnp.dot(xb, wgu_bf[...], preferred_element_type=F32) + bgu_ref[0]
        x_glu = jnp.minimum(hdn[:, :d_exp], SWIGLU_LIMIT)
        x_lin = jnp.clip(hdn[:, d_exp:], -SWIGLU_LIMIT, SWIGLU_LIMIT)
        act = x_glu * jax.nn.sigmoid(SWIGLU_ALPHA * x_glu) * (x_lin + 1.0)
        y_ref[...] = jnp.dot(act.astype(BF16), wd_bf[...], preferred_element_type=F32) + bd_ref[0]

    @pl.when(b >= nu_ref[0])
    def _():
        y_ref[...] = jnp.zeros_like(y_ref)


def _experts(block_e, n_used, xs, w_gate_up, b_gate_up, w_down, b_down):
    n_rows, d = xs.shape
    bm = EXPERT_BLOCK
    n_blk = n_rows // bm
    ne, _, two_de = w_gate_up.shape
    de = w_down.shape[1]

    def blk(b, be, nu):
        return jnp.minimum(b, nu[0] - 1)

    grid_spec = pltpu.PrefetchScalarGridSpec(
        num_scalar_prefetch=2,
        grid=(n_blk,),
        in_specs=[pl.BlockSpec((bm, d), lambda b, be, nu: (blk(b, be, nu), 0)),
                  pl.BlockSpec((1, d, two_de), lambda b, be, nu: (be[blk(b, be, nu)], 0, 0)),
                  pl.BlockSpec((1, 1, two_de), lambda b, be, nu: (be[blk(b, be, nu)], 0, 0)),
                  pl.BlockSpec((1, de, d), lambda b, be, nu: (be[blk(b, be, nu)], 0, 0)),
                  pl.BlockSpec((1, 1, d), lambda b, be, nu: (be[blk(b, be, nu)], 0, 0))],
        out_specs=pl.BlockSpec((bm, d), lambda b, be, nu: (b, 0)),
        scratch_shapes=[pltpu.VMEM((d, two_de), BF16), pltpu.VMEM((de, d), BF16)],
    )
    return pl.pallas_call(
        _experts_kernel,
        out_shape=jax.ShapeDtypeStruct((n_rows, d), F32),
        grid_spec=grid_spec,
        compiler_params=_cparams(("arbitrary",)),
        name="experts",
    )(block_e, n_used, xs, w_gate_up, b_gate_up.reshape(ne, 1, two_de), w_down, b_down.reshape(ne, 1, d))


def _combine_kernel(pstart_ref, idx_ref, rank_ref, y_ref, gate_ref, x1_ref, gt_ref, gpost_ref,
                    o_ref, buf, sem):
    tm = x1_ref.shape[0]

    def row_copy(r, k):
        a = r * TOP_K + k
        src = pstart_ref[idx_ref[a]] + rank_ref[a]
        return pltpu.make_async_copy(y_ref.at[pl.ds(src, 1)], buf.at[k, pl.ds(r, 1)], sem)

    def start(r, carry):
        for k in range(TOP_K):
            row_copy(r, k).start(priority=k % 2)
        return carry

    def wait(r, carry):
        for k in range(TOP_K):
            row_copy(r, k).wait()
        return carry

    lax.fori_loop(0, tm, start, 0)
    lax.fori_loop(0, tm, wait, 0)

    gate = gate_ref[...]
    acc = gate[:, 0:1] * buf[0]
    for k in range(1, TOP_K):
        acc = acc + gate[:, k:k + 1] * buf[k]
    o_ref[...] = x1_ref[...] + gt_ref[...] * (_rms(acc) * gpost_ref[...])


def _combine(pstart, idx_flat, rank_flat, y, gate4, x1, gt, gpost):
    s, d = x1.shape
    tm = COMBINE_TILE
    grid_spec = pltpu.PrefetchScalarGridSpec(
        num_scalar_prefetch=1,
        grid=(s // tm,),
        in_specs=[pl.BlockSpec((tm * TOP_K,), lambda i, ps: (i,), memory_space=pltpu.SMEM),
                  pl.BlockSpec((tm * TOP_K,), lambda i, ps: (i,), memory_space=pltpu.SMEM),
                  pl.BlockSpec(memory_space=pl.ANY),
                  pl.BlockSpec((tm, TOP_K), lambda i, ps: (i, 0)),
                  pl.BlockSpec((tm, d), lambda i, ps: (i, 0)),
                  pl.BlockSpec((1, d), lambda i, ps: (0, 0)),
                  pl.BlockSpec((1, d), lambda i, ps: (0, 0))],
        out_specs=pl.BlockSpec((tm, d), lambda i, ps: (i, 0)),
        scratch_shapes=[pltpu.VMEM((TOP_K, tm, d), F32), pltpu.SemaphoreType.DMA],
    )
    return pl.pallas_call(
        _combine_kernel,
        out_shape=jax.ShapeDtypeStruct((s, d), F32),
        grid_spec=grid_spec,
        compiler_params=_cparams(("arbitrary",)),
        name="combine",
    )(pstart, idx_flat, rank_flat, y, gate4, x1, gt, gpost)


def _layer(x2, mod, g_pre_mix, g_post_mix, w_in, b_forget, rel_bias, w_out,
           g_pre_ffn, g_post_ffn, w_router, b_router, w_gate_up, b_gate_up, w_down, b_down):
    s, d = x2.shape
    sh_m, sc_m, gt_m, sh_f, sc_f, gt_f = [mod[:, k * d:(k + 1) * d] for k in range(6)]
    n_qkv = 3 * (N_HEADS_FOX + N_HEADS_MOBA) * HEAD_DIM
    fox_w = N_HEADS_FOX * HEAD_DIM

    w_qkv = w_in[:, :n_qkv].astype(BF16)
    w_f = w_in[:, n_qkv:].T
    qkv, cum, sel, nrm, tr = _inproj(x2, g_pre_mix.reshape(1, d), sc_m, sh_m, w_qkv, w_f,
                                     b_forget.reshape(-1, 1))
    y_a = _fox(qkv, cum, nrm)
    y_b = _moba(qkv, tr, sel, rel_bias)

    w_out_bf = w_out.astype(BF16)
    x1, h2, idx4, rank4, gate4, cnt = _outproj(
        y_a, y_b, x2, w_out_bf[:fox_w], w_out_bf[fox_w:], g_post_mix.reshape(1, d), gt_m,
        g_pre_ffn.reshape(1, d), sc_f, sh_f, w_router, b_router.reshape(1, -1))

    bm = EXPERT_BLOCK
    counts = cnt[0].astype(I32)
    pblk = (counts + bm - 1) // bm
    pend_blk = jnp.cumsum(pblk)
    pstart = ((pend_blk - pblk) * bm).astype(I32)
    n_rows = s * TOP_K + N_EXPERTS * bm
    n_blk = n_rows // bm
    block_e = jnp.minimum(jnp.sum(pend_blk[None, :] <= jnp.arange(n_blk)[:, None], axis=1),
                          N_EXPERTS - 1).astype(I32)
    n_used = pend_blk[-1:].astype(I32)

    idx_flat = idx4.reshape(-1)
    rank_flat = rank4.reshape(-1)
    xs = _dispatch(pstart, pblk.astype(I32), n_used, idx_flat, rank_flat, h2, n_rows)
    y = _experts(block_e, n_used, xs, w_gate_up, b_gate_up, w_down, b_down)
    return _combine(pstart, idx_flat, rank_flat, y, gate4, x1, gt_f, g_post_ffn.reshape(1, d))


def kernel(x, c, w_ada, b_ada, g_pre_mix, g_post_mix, w_in, b_forget, rel_bias, w_out, g_pre_ffn, g_post_ffn, w_router, b_router, w_gate_up, b_gate_up, w_down, b_down):
    bsz, s, d = x.shape
    depth = w_ada.shape[0]
    outs = []
    for bi in range(bsz):
        x2 = x[bi]
        for l in range(depth):
            mod = _adaln(c[bi:bi + 1], w_ada[l], b_ada[l])
            x2 = _layer(x2, mod, g_pre_mix[l], g_post_mix[l], w_in[l], b_forget[l], rel_bias, w_out[l],
                        g_pre_ffn[l], g_post_ffn[l], w_router[l], b_router[l], w_gate_up[l], b_gate_up[l],
                        w_down[l], b_down[l])
        outs.append(x2)
    return jnp.stack(outs)
```

```python
import functools
import math

import numpy as np
import jax
import jax.numpy as jnp
from jax import lax
from jax.experimental import pallas as pl
from jax.experimental.pallas import tpu as pltpu

F32 = jnp.float32
BF16 = jnp.bfloat16
I32 = jnp.int32

HEAD_DIM = 64
N_HEADS_FOX = 8
N_HEADS_MOBA = 8
PAIR = 2 * HEAD_DIM
MOBA_BLOCK = 256
MOBA_TOPK = 3
NUM_BUCKETS = 32
MAX_DISTANCE = 128
N_EXPERTS = 32
TOP_K = 4
SWIGLU_LIMIT = 7.0
SWIGLU_ALPHA = 1.702
RMS_EPS = 1e-6
NEG = -(2.0 ** 100)
M_INIT = -(2.0 ** 99)
LOG2E = math.log2(math.e)
SUM_ROWS = 16
EXP_UNDERFLOW = 90.0
VMEM_LIMIT = 56 * 1024 * 1024

ROW_TILE = 512
FOX_TILE = 256
EXPERT_BLOCK = 256
COMBINE_TILE = 256

NT_DIMS = (((1,), (1,)), ((), ()))


def _cparams(sem):
    return pltpu.CompilerParams(dimension_semantics=sem, vmem_limit_bytes=VMEM_LIMIT)


def _rms(x):
    return x * lax.rsqrt(jnp.mean(x * x, axis=-1, keepdims=True) + RMS_EPS)


def _adaln_kernel(c_ref, w_ref, b_ref, o_ref):
    c = c_ref[...]
    cond = c * jax.nn.sigmoid(c)
    o_ref[...] = jnp.dot(cond, w_ref[...], preferred_element_type=F32,
                         precision=lax.Precision.HIGHEST) + b_ref[...]


def _adaln(c, w_ada, b_ada):
    d = c.shape[-1]
    n = w_ada.shape[-1]
    c8 = jnp.broadcast_to(c.reshape(1, d), (8, d))
    out = pl.pallas_call(
        _adaln_kernel,
        out_shape=jax.ShapeDtypeStruct((8, n), F32),
        grid=(n // d,),
        in_specs=[pl.BlockSpec((8, d), lambda j: (0, 0)),
                  pl.BlockSpec((d, d), lambda j: (0, j)),
                  pl.BlockSpec((1, d), lambda j: (0, j))],
        out_specs=pl.BlockSpec((8, d), lambda j: (0, j)),
        compiler_params=_cparams(("arbitrary",)),
        name="adaln",
    )(c8, w_ada, b_ada.reshape(1, n))
    return out[0:1]


def _inproj_kernel(x_ref, g_ref, sc_ref, sh_ref, w_ref, wf_ref, bf_ref,
                   qkv_ref, cum_ref, sel_ref, nrm_ref, tr_ref, km_scr, carry_scr):
    i = pl.program_id(0)
    tm = x_ref.shape[0]
    nblk = km_scr.shape[0]

    @pl.when(i == 0)
    def _():
        km_scr[...] = jnp.zeros_like(km_scr)
        carry_scr[...] = jnp.zeros_like(carry_scr)

    x = x_ref[...]
    h = _rms(x) * g_ref[...] * (1.0 + sc_ref[...]) + sh_ref[...]
    hb = h.astype(BF16)

    width = N_HEADS_FOX * HEAD_DIM
    hsel = jnp.where(lax.broadcasted_iota(I32, (width, N_HEADS_FOX), 0) // HEAD_DIM
                     == lax.broadcasted_iota(I32, (width, N_HEADS_FOX), 1), 1.0, 0.0)
    qb = kb = None
    for c in range(6):
        pc = jnp.dot(hb, w_ref[:, c * width:(c + 1) * width], preferred_element_type=F32)
        if c == 0:
            pc_s = pc * (HEAD_DIM ** -0.5)
        if c < 3:
            pcb = (pc_s if c == 0 else pc).astype(BF16)
            qkv_ref[:, c * width:(c + 1) * width] = pcb
        if c < 2:
            sq = pcb.astype(F32)
            n2 = jnp.dot((sq * sq).astype(BF16), hsel.astype(BF16), preferred_element_type=F32)
            nrm_ref[0, c:c + 1, :] = jnp.max(n2, axis=0, keepdims=True)
        if c == 3:
            qb = pc
            tr_ref[0:width, :] = (pc * (LOG2E * HEAD_DIM ** -0.5)).T.astype(BF16)
        if c == 4:
            kb = pc
            qkv_ref[:, 3 * width:4 * width] = pc.astype(BF16)
        if c == 5:
            tr_ref[width:2 * width, :] = pc.T.astype(BF16)

    ft = lax.dot_general(wf_ref[...], h, NT_DIMS, preferred_element_type=F32,
                         precision=lax.Precision.HIGHEST)
    z = ft + bf_ref[...]
    logf = -(jnp.maximum(-z, 0.0) + jnp.log1p(jnp.exp(-jnp.abs(z))))
    lane = lax.broadcasted_iota(I32, logf.shape, 1)
    cs = logf
    sh = 1
    while sh < tm:
        cs = cs + jnp.where(lane >= sh, pltpu.roll(cs, sh, axis=1), 0.0)
        sh *= 2
    cs = cs + carry_scr[...]
    cum_ref[...] = cs
    carry_scr[...] = cs[:, tm - 1:tm]

    nb_tile = tm // MOBA_BLOCK
    for b in range(nb_tile):
        kmean = jnp.sum(kb[b * MOBA_BLOCK:(b + 1) * MOBA_BLOCK], axis=0, keepdims=True) * (1.0 / MOBA_BLOCK)
        km_scr[pl.ds(i * nb_tile + b, 1), :] = kmean

    km = km_scr[...]
    lane_h = lax.broadcasted_iota(I32, km.shape, 1) // HEAD_DIM
    blk = lax.broadcasted_iota(I32, (nblk, tm), 0)
    col = lax.broadcasted_iota(I32, (nblk, tm), 1)
    own = i * nb_tile + col // MOBA_BLOCK
    for hd in range(N_HEADS_MOBA):
        kmh = jnp.where(lane_h == hd, km, 0.0)
        g = lax.dot_general(kmh, qb, NT_DIMS, preferred_element_type=F32,
                            precision=lax.Precision.HIGHEST)
        g = jnp.where(blk < own, g, -jnp.inf)
        sel = jnp.zeros(g.shape, dtype=jnp.bool_)
        for _ in range(MOBA_TOPK):
            m = jnp.max(g, axis=0, keepdims=True)
            first = jnp.min(jnp.where(g == m, blk, nblk), axis=0, keepdims=True)
            pick = (blk == first) & (m > -jnp.inf)
            sel = sel | pick
            g = jnp.where(pick, -jnp.inf, g)
        sel_ref[hd] = jnp.where(sel, 0.0, NEG)


def _inproj(x2, g, sc, sh, w_qkv, w_f, b_f):
    s, d = x2.shape
    tm = ROW_TILE
    nblk = s // MOBA_BLOCK
    n = w_qkv.shape[1]
    n_rows_out = 4 * N_HEADS_FOX * HEAD_DIM
    return pl.pallas_call(
        _inproj_kernel,
        out_shape=(jax.ShapeDtypeStruct((s, n_rows_out), BF16),
                   jax.ShapeDtypeStruct((N_HEADS_FOX, s), F32),
                   jax.ShapeDtypeStruct((N_HEADS_MOBA, nblk, s), F32),
                   jax.ShapeDtypeStruct((s // tm, 2, N_HEADS_FOX), F32),
                   jax.ShapeDtypeStruct((2 * N_HEADS_MOBA * HEAD_DIM, s), BF16)),
        grid=(s // tm,),
        in_specs=[pl.BlockSpec((tm, d), lambda i: (i, 0)),
                  pl.BlockSpec((1, d), lambda i: (0, 0)),
                  pl.BlockSpec((1, d), lambda i: (0, 0)),
                  pl.BlockSpec((1, d), lambda i: (0, 0)),
                  pl.BlockSpec((d, n), lambda i: (0, 0)),
                  pl.BlockSpec((N_HEADS_FOX, d), lambda i: (0, 0)),
                  pl.BlockSpec((N_HEADS_FOX, 1), lambda i: (0, 0))],
        out_specs=(pl.BlockSpec((tm, n_rows_out), lambda i: (i, 0)),
                   pl.BlockSpec((N_HEADS_FOX, tm), lambda i: (0, i)),
                   pl.BlockSpec((N_HEADS_MOBA, nblk, tm), lambda i: (0, 0, i)),
                   pl.BlockSpec((1, 2, N_HEADS_FOX), lambda i: (i, 0, 0)),
                   pl.BlockSpec((2 * N_HEADS_MOBA * HEAD_DIM, tm), lambda i: (0, i))),
        scratch_shapes=[pltpu.VMEM((nblk, N_HEADS_MOBA * HEAD_DIM), F32),
                        pltpu.VMEM((N_HEADS_FOX, 1), F32)],
        compiler_params=_cparams(("arbitrary",)),
        name="inproj",
    )(x2, g, sc, sh, w_qkv, w_f, b_f)


def _online_step(s, vt, m_ref, l_ref, a_ref, hd):
    m_old = m_ref[hd]
    m_new = jnp.maximum(m_old, jnp.max(s, axis=1, keepdims=True))
    alpha = jnp.exp(m_old - m_new)
    p = jnp.exp(s - m_new)
    l_ref[hd] = alpha * l_ref[hd] + jnp.sum(p, axis=1, keepdims=True)
    a_ref[hd] = alpha * a_ref[hd] + jnp.dot(p.astype(BF16), vt, preferred_element_type=F32)
    m_ref[hd] = m_new


def _init_state(m_ref, l_ref, a_ref):
    m_ref[...] = jnp.full(m_ref.shape, NEG, F32)
    l_ref[...] = jnp.zeros(l_ref.shape, F32)
    a_ref[...] = jnp.zeros(a_ref.shape, F32)


def _finish(o_ref, l_ref, a_ref):
    lane = lax.broadcasted_iota(I32, o_ref.shape, 1)
    o0 = a_ref[0] / l_ref[0]
    o1 = a_ref[1] / l_ref[1]
    o_ref[...] = jnp.where(lane < HEAD_DIM, o0, o1).astype(o_ref.dtype)


def _fox_kernel(jlo_ref, q_ref, k_ref, v_ref, cum_ref, o_ref, m_ref, l_ref, a_ref):
    i = pl.program_id(1)
    j_first = jlo_ref[pl.program_id(0) * pl.num_programs(1) + i]
    t = q_ref.shape[0]
    q2 = q_ref[...]
    lane = lax.broadcasted_iota(I32, q2.shape, 1)
    qh = (jnp.where(lane < HEAD_DIM, q2, jnp.zeros_like(q2)),
          jnp.where(lane >= HEAD_DIM, q2, jnp.zeros_like(q2)))
    q0 = pl.multiple_of(i * t, t)
    cq = [cum_ref[0, hd:hd + 1, pl.ds(q0, t)][:, 0:1] for hd in range(2)]
    _init_state(m_ref, l_ref, a_ref)

    def tile(j, masked):
        k0 = pl.multiple_of(j * t, t)
        kt = k_ref[pl.ds(k0, t), :]
        vt = v_ref[pl.ds(k0, t), :]
        for hd in range(2):
            s = lax.dot_general(qh[hd], kt, NT_DIMS, preferred_element_type=F32)
            s = s + (cq[hd] - cum_ref[0, hd:hd + 1, pl.ds(k0, t)])
            if masked:
                r = lax.broadcasted_iota(I32, s.shape, 0)
                c = lax.broadcasted_iota(I32, s.shape, 1)
                s = jnp.where(c <= r, s, NEG)
            _online_step(s, vt, m_ref, l_ref, a_ref, hd)

    def body(j, carry):
        tile(j, False)
        return carry

    lax.fori_loop(j_first, i, body, 0)
    tile(i, True)
    _finish(o_ref, l_ref, a_ref)


def _fox_first_tile(cum, nrm, t):
    cend = cum[:, t - 1::t]
    nt = cend.shape[1]
    cprev = jnp.concatenate([jnp.zeros((cend.shape[0], 1), F32), cend[:, :-1]], axis=1)
    rep = nt // nrm.shape[0]
    qn = jnp.repeat(jnp.sqrt(nrm[:, 0, :]).T, rep, axis=1)
    kn = jnp.repeat(jnp.sqrt(nrm[:, 1, :]).T, rep, axis=1)
    gap = (1.02 * qn[:, :, None] * (kn[:, None, :] + kn[:, :, None])
           + cprev[:, :, None] - cend[:, None, :])
    jj = jnp.arange(nt)[None, None, :]
    ii = jnp.arange(nt)[None, :, None]
    needed = (jj < ii) & jnp.logical_not(gap <= -EXP_UNDERFLOW)
    needed = needed[0::2] | needed[1::2]
    first = jnp.min(jnp.where(needed, jj, ii), axis=2)
    return first.reshape(-1).astype(I32)


def _fox(qkv, cum, nrm):
    s = qkv.shape[0]
    t = FOX_TILE
    npair = N_HEADS_FOX // 2
    cum3 = cum.reshape(npair, 2, s)
    jlo = _fox_first_tile(cum, nrm, t)
    grid_spec = pltpu.PrefetchScalarGridSpec(
        num_scalar_prefetch=1,
        grid=(npair, s // t),
        in_specs=[pl.BlockSpec((t, PAIR), lambda p, i, jl: (i, p)),
                  pl.BlockSpec((s, PAIR), lambda p, i, jl: (0, npair + p)),
                  pl.BlockSpec((s, PAIR), lambda p, i, jl: (0, 2 * npair + p)),
                  pl.BlockSpec((1, 2, s), lambda p, i, jl: (p, 0, 0))],
        out_specs=pl.BlockSpec((t, PAIR), lambda p, i, jl: (i, p)),
        scratch_shapes=[pltpu.VMEM((2, t, 1), F32), pltpu.VMEM((2, t, 1), F32),
                        pltpu.VMEM((2, t, PAIR), F32)],
    )
    return pl.pallas_call(
        _fox_kernel,
        out_shape=jax.ShapeDtypeStruct((s, npair * PAIR), BF16),
        grid_spec=grid_spec,
        compiler_params=_cparams(("arbitrary", "arbitrary")),
        name="fox",
    )(jlo, qkv, qkv, qkv, cum3)


def _t5_bucket_np(dist):
    dist = np.maximum(dist, 0)
    max_exact = NUM_BUCKETS // 2
    d = np.maximum(dist, 1).astype(np.float32)
    large = max_exact + (np.log(d / np.float32(max_exact)) / np.float32(math.log(MAX_DISTANCE / max_exact))
                         * np.float32(NUM_BUCKETS - max_exact)).astype(np.int32)
    large = np.minimum(large, NUM_BUCKETS - 1)
    return np.where(dist < max_exact, dist, large).astype(np.int32)


def _moba_kernel(rb_ref, qt_ref, k_ref, vt_ref, sel_ref, bkt_ref, o_ref,
                 m_ref, a_ref, bias_scr, sa_ref, sb_ref, pa_ref, pb_ref, ala_ref, alb_ref):
    p = pl.program_id(0)
    g = pl.program_id(1)
    tq = qt_ref.shape[1]
    t = tq // 2

    @pl.when(g == 0)
    def _():
        r = lax.broadcasted_iota(I32, (t, t), 0)
        c = lax.broadcasted_iota(I32, (t, t), 1)
        zero = jnp.zeros((t, t), F32)
        for hd in range(2):
            h = 2 * p + hd
            far = rb_ref[(NUM_BUCKETS - 1) * N_HEADS_MOBA + h]
            tiles = []
            for w in range(2):
                bkt = bkt_ref[w]
                acc = jnp.zeros(bkt.shape, F32)
                for kk in range(NUM_BUCKETS):
                    acc = acc + jnp.where(bkt == kk, rb_ref[kk * N_HEADS_MOBA + h], 0.0)
                tiles.append((acc - far) * LOG2E)
            prev_t = tiles[0]
            own_t = jnp.where(r <= c, tiles[1], NEG)
            bias_scr[hd, 0] = jnp.concatenate([prev_t, zero], axis=1)
            bias_scr[hd, 1] = jnp.concatenate([own_t, prev_t], axis=1)
            bias_scr[hd, 2] = jnp.concatenate([zero, own_t], axis=1)

    qt = qt_ref[...]
    top = lax.broadcasted_iota(I32, qt.shape, 0) < HEAD_DIM
    zq = jnp.zeros_like(qt)
    qth = (jnp.where(top, qt, zq), jnp.where(top, zq, qt))
    m_ref[...] = jnp.full(m_ref.shape, M_INIT, F32)
    a_ref[...] = jnp.zeros(a_ref.shape, F32)

    er = lax.broadcasted_iota(I32, (SUM_ROWS, 2 * t), 0)
    ec = lax.broadcasted_iota(I32, (SUM_ROWS, 2 * t), 1)
    ones_rows = jnp.where(((er == 0) & (ec < t)) | ((er == 1) & (ec >= t)), 1.0, 0.0).astype(BF16)
    arow = lax.broadcasted_iota(I32, a_ref.shape, 0)
    head0_rows = (arow < HEAD_DIM) | (arow == PAIR)
    vtop = lax.broadcasted_iota(I32, (PAIR, t), 0) < HEAD_DIM
    in_a = lax.broadcasted_iota(I32, (1, tq), 1) < t

    n_far = jnp.maximum(2 * g - 1, 0)

    def produce(j, s_buf):
        k0 = pl.multiple_of(j * t, t)
        kt = k_ref[pl.ds(k0, t), :]
        for hd in range(2):
            s_buf[hd] = jnp.dot(kt, qth[hd], preferred_element_type=F32)

    def softmax(s_buf, p_buf, al_buf, selrows, w):
        for hd in range(2):
            s = s_buf[hd]
            if w is not None:
                s = s + bias_scr[hd, w]
            smax = jnp.max(s, axis=0, keepdims=True)
            m_old = m_ref[hd]
            m_new = jnp.maximum(m_old, smax + selrows[hd])
            shift = m_new - selrows[hd]
            al_buf[hd] = jnp.exp2(m_old - m_new)
            m_ref[hd] = m_new
            p_buf[hd * t:(hd + 1) * t, :] = jnp.exp2(s - shift).astype(BF16)

    def far_rows(j):
        return [jnp.where(j < n_far, sel_ref[hd, pl.ds(j, 1), :], NEG) for hd in range(2)]

    def accumulate(j, p_buf, al_buf):
        k0 = pl.multiple_of(j * t, t)
        vt = vt_ref[:, pl.ds(k0, t)]
        zv = jnp.zeros_like(vt)
        vcat = jnp.concatenate([jnp.where(vtop, vt, zv), jnp.where(vtop, zv, vt)], axis=1)
        vcat = jnp.concatenate([vcat, ones_rows], axis=0)
        a_ref[...] = (a_ref[...] * jnp.where(head0_rows, al_buf[0], al_buf[1])
                      + jnp.dot(vcat, p_buf[...], preferred_element_type=F32))

    pb_ref[...] = jnp.zeros_like(pb_ref)
    alb_ref[...] = jnp.ones_like(alb_ref)
    produce(0, sa_ref)

    def body(u, carry):
        j0 = 2 * u
        produce(j0 + 1, sb_ref)
        softmax(sa_ref, pa_ref, ala_ref, far_rows(j0), None)
        accumulate(jnp.maximum(j0 - 1, 0), pb_ref, alb_ref)
        produce(j0 + 2, sa_ref)
        accumulate(j0, pa_ref, ala_ref)
        softmax(sb_ref, pb_ref, alb_ref, far_rows(j0 + 1), None)
        return carry

    n_pairs = (n_far + 1) // 2
    lax.fori_loop(0, n_pairs, body, 0)

    j1 = jnp.maximum(2 * g - 1, 0)
    j2 = 2 * g
    j3 = 2 * g + 1
    rows1 = [jnp.where(g >= 1, sel_ref[hd, pl.ds(j1, 1), :], NEG) for hd in range(2)]
    rows2 = [jnp.where(in_a, 0.0, sel_ref[hd, pl.ds(j2, 1), :]) for hd in range(2)]
    rows3 = [jnp.where(in_a, NEG, 0.0)] * 2
    produce(j1, sa_ref)
    produce(j2, sb_ref)
    softmax(sa_ref, pa_ref, ala_ref, rows1, 0)
    accumulate(jnp.maximum(2 * n_pairs - 1, 0), pb_ref, alb_ref)
    produce(j3, sa_ref)
    softmax(sb_ref, pb_ref, alb_ref, rows2, 1)
    accumulate(j1, pa_ref, ala_ref)
    softmax(sa_ref, pa_ref, ala_ref, rows3, 2)
    accumulate(j2, pb_ref, alb_ref)
    accumulate(j3, pa_ref, ala_ref)
    out_t = a_ref[0:PAIR, :] / jnp.where(top, a_ref[PAIR:PAIR + 1, :], a_ref[PAIR + 1:PAIR + 2, :])
    o_ref[...] = out_t.T.astype(o_ref.dtype)


def _moba(qkv, tr, sel, rel_bias):
    s = qkv.shape[0]
    t = MOBA_BLOCK
    npair = N_HEADS_MOBA // 2
    nblk = s // t
    kcol = 3 * (N_HEADS_FOX // 2)
    a = np.arange(t)[None, :]
    b = np.arange(t)[:, None]
    bkt = jnp.asarray(np.stack([_t5_bucket_np(t + a - b), _t5_bucket_np(a - b)]))
    tq = 2 * t
    grid_spec = pltpu.PrefetchScalarGridSpec(
        num_scalar_prefetch=1,
        grid=(npair, s // tq),
        in_specs=[pl.BlockSpec((PAIR, tq), lambda p, i, rb: (p, i)),
                  pl.BlockSpec((s, PAIR), lambda p, i, rb: (0, kcol + p)),
                  pl.BlockSpec((PAIR, s), lambda p, i, rb: (npair + p, 0)),
                  pl.BlockSpec((2, nblk, tq), lambda p, i, rb: (p, 0, i)),
                  pl.BlockSpec((2, t, t), lambda p, i, rb: (0, 0, 0))],
        out_specs=pl.BlockSpec((tq, PAIR), lambda p, i, rb: (i, p)),
        scratch_shapes=[pltpu.VMEM((2, 1, tq), F32),
                        pltpu.VMEM((PAIR + SUM_ROWS, tq), F32), pltpu.VMEM((2, 3, t, tq), F32),
                        pltpu.VMEM((2, t, tq), F32), pltpu.VMEM((2, t, tq), F32),
                        pltpu.VMEM((2 * t, tq), BF16), pltpu.VMEM((2 * t, tq), BF16),
                        pltpu.VMEM((2, 1, tq), F32), pltpu.VMEM((2, 1, tq), F32)],
    )
    return pl.pallas_call(
        _moba_kernel,
        out_shape=jax.ShapeDtypeStruct((s, npair * PAIR), BF16),
        grid_spec=grid_spec,
        compiler_params=_cparams(("arbitrary", "arbitrary")),
        name="moba",
    )(rel_bias.reshape(-1), tr, qkv, tr, sel, bkt)


def _outproj_kernel(ya_ref, yb_ref, x_ref, wa_ref, wb_ref, gpost_ref, gt_ref, gpre_ref,
                    sc_ref, sh_ref, wr_ref, br_ref,
                    x1_ref, h2_ref, idx_ref, rank_ref, gate_ref, cnt_ref, carry_scr):
    i = pl.program_id(0)
    tm = x_ref.shape[0]

    @pl.when(i == 0)
    def _():
        carry_scr[...] = jnp.zeros_like(carry_scr)

    y = (jnp.dot(ya_ref[...], wa_ref[...], preferred_element_type=F32)
         + jnp.dot(yb_ref[...], wb_ref[...], preferred_element_type=F32))
    x1 = x_ref[...] + gt_ref[...] * (_rms(y) * gpost_ref[...])
    x1_ref[...] = x1
    h2 = _rms(x1) * gpre_ref[...] * (1.0 + sc_ref[...]) + sh_ref[...]
    h2_ref[...] = h2

    logits = jnp.dot(h2, wr_ref[...], preferred_element_type=F32,
                     precision=lax.Precision.HIGHEST) + br_ref[...]
    ne = logits.shape[1]
    lane = lax.broadcasted_iota(I32, logits.shape, 1)
    lane4 = lax.broadcasted_iota(I32, (tm, TOP_K), 1)
    g = logits
    mask = jnp.zeros(logits.shape, F32)
    vals, picks = [], []
    for _ in range(TOP_K):
        m = jnp.max(g, axis=1, keepdims=True)
        first = jnp.min(jnp.where(g == m, lane, ne), axis=1, keepdims=True)
        pick = lane == first
        mask = jnp.where(pick, 1.0, mask)
        g = jnp.where(pick, -jnp.inf, g)
        vals.append(m)
        picks.append(pick)
    ex = [jnp.exp(v - vals[0]) for v in vals]
    den = ex[0] + ex[1] + ex[2] + ex[3]
    gates = [e / den for e in ex]

    r = lax.broadcasted_iota(I32, (tm, tm), 0)
    c = lax.broadcasted_iota(I32, (tm, tm), 1)
    tril = jnp.where(c < r, 1.0, 0.0).astype(BF16)
    before = jnp.dot(tril, mask.astype(BF16), preferred_element_type=F32) + carry_scr[...]
    total = carry_scr[...] + jnp.sum(mask, axis=0, keepdims=True)
    carry_scr[...] = total
    cnt_ref[...] = jnp.broadcast_to(total, cnt_ref.shape)

    def pack4(cols):
        return jnp.where(lane4 == 0, cols[0],
                         jnp.where(lane4 == 1, cols[1], jnp.where(lane4 == 2, cols[2], cols[3])))

    ranks = [jnp.sum(jnp.where(pk, before, 0.0), axis=1, keepdims=True) for pk in picks]
    idxs = [jnp.sum(jnp.where(pk, lane, 0), axis=1, keepdims=True) for pk in picks]
    idx_ref[...] = pack4(idxs)
    rank_ref[...] = pack4(ranks).astype(I32)
    gate_ref[...] = pack4(gates)


def _outproj(mix_a, mix_b, x2, w_a, w_b, gpost, gt, gpre, sc, sh, w_router, b_router):
    s, d = x2.shape
    tm = ROW_TILE
    ne = w_router.shape[1]
    wa = mix_a.shape[1]
    row = lambda i: (i, 0)
    fix = lambda i: (0, 0)
    vec = pl.BlockSpec((1, d), fix)
    return pl.pallas_call(
        _outproj_kernel,
        out_shape=(jax.ShapeDtypeStruct((s, d), F32),
                   jax.ShapeDtypeStruct((s, d), F32),
                   jax.ShapeDtypeStruct((s, TOP_K), I32),
                   jax.ShapeDtypeStruct((s, TOP_K), I32),
                   jax.ShapeDtypeStruct((s, TOP_K), F32),
                   jax.ShapeDtypeStruct((8, ne), F32)),
        grid=(s // tm,),
        in_specs=[pl.BlockSpec((tm, wa), row), pl.BlockSpec((tm, wa), row), pl.BlockSpec((tm, d), row),
                  pl.BlockSpec((wa, d), fix), pl.BlockSpec((wa, d), fix),
                  vec, vec, vec, vec, vec,
                  pl.BlockSpec((d, ne), fix), pl.BlockSpec((1, ne), fix)],
        out_specs=(pl.BlockSpec((tm, d), row), pl.BlockSpec((tm, d), row),
                   pl.BlockSpec((tm, TOP_K), row), pl.BlockSpec((tm, TOP_K), row),
                   pl.BlockSpec((tm, TOP_K), row), pl.BlockSpec((8, ne), fix)),
        scratch_shapes=[pltpu.VMEM((1, ne), F32)],
        compiler_params=_cparams(("arbitrary",)),
        name="outproj",
    )(mix_a, mix_b, x2, w_a, w_b, gpost, gt, gpre, sc, sh, w_router, b_router)


def _dest_kernel(idx_ref, rank_ref, pstart_ref, o_ref):
    idx = idx_ref[...]
    tm = idx.shape[0]
    ne = pstart_ref.shape[1]
    lane = lax.broadcasted_iota(I32, (tm, ne), 1)
    lane4 = lax.broadcasted_iota(I32, idx.shape, 1)
    out = rank_ref[...]
    for k in range(TOP_K):
        start = jnp.sum(jnp.where(lane == idx[:, k:k + 1], pstart_ref[...], 0), axis=1, keepdims=True)
        out = out + jnp.where(lane4 == k, start, 0)
    o_ref[...] = out


def _dest(idx4, rank4, pstart):
    s = idx4.shape[0]
    tm = min(4 * ROW_TILE, s)
    row = lambda i: (i, 0)
    return pl.pallas_call(
        _dest_kernel,
        out_shape=jax.ShapeDtypeStruct((s, TOP_K), I32),
        grid=(s // tm,),
        in_specs=[pl.BlockSpec((tm, TOP_K), row), pl.BlockSpec((tm, TOP_K), row),
                  pl.BlockSpec((1, N_EXPERTS), lambda i: (0, 0))],
        out_specs=pl.BlockSpec((tm, TOP_K), row),
        compiler_params=_cparams(("arbitrary",)),
        name="dest",
    )(idx4, rank4, pstart.reshape(1, -1))


def _dispatch_kernel(pstart_ref, pblk_ref, nu_ref, dest_ref, h_ref, xs_ref,
                     zero_scr, sem, zsem):
    tm = h_ref.shape[0]
    bm = zero_scr.shape[0]
    n_blk = xs_ref.shape[0] // bm

    @pl.when(pl.program_id(0) == 0)
    def _():
        zero_scr[...] = jnp.zeros_like(zero_scr)

        def zero_copy(row0):
            return pltpu.make_async_copy(zero_scr, xs_ref.at[pl.ds(pl.multiple_of(row0, bm), bm)], zsem)

        for phase in range(2):
            for e in range(N_EXPERTS):
                last = pstart_ref[e] + (pblk_ref[e] - 1) * bm
                tail = (n_blk - N_EXPERTS + e) * bm
                for cond, row0 in ((pblk_ref[e] > 0, last), (n_blk - N_EXPERTS + e >= nu_ref[0], tail)):
                    @pl.when(cond)
                    def _():
                        if phase == 0:
                            zero_copy(row0).start()
                        else:
                            zero_copy(row0).wait()

    def row_copy(r, k):
        dst = dest_ref[r * TOP_K + k]
        return pltpu.make_async_copy(h_ref.at[pl.ds(r, 1)], xs_ref.at[pl.ds(dst, 1)], sem)

    def start(r, carry):
        for k in range(TOP_K):
            row_copy(r, k).start(priority=k % 2)
        return carry

    def wait(r, carry):
        for k in range(TOP_K):
            row_copy(r, k).wait()
        return carry

    lax.fori_loop(0, tm, start, 0)
    lax.fori_loop(0, tm, wait, 0)


def _dispatch(pstart, pblk, n_used, dest_flat, h2, n_rows):
    s, d = h2.shape
    tm = COMBINE_TILE
    grid_spec = pltpu.PrefetchScalarGridSpec(
        num_scalar_prefetch=3,
        grid=(s // tm,),
        in_specs=[pl.BlockSpec((tm * TOP_K,), lambda i, *_: (i,), memory_space=pltpu.SMEM),
                  pl.BlockSpec((tm, d), lambda i, *_: (i, 0))],
        out_specs=pl.BlockSpec(memory_space=pl.ANY),
        scratch_shapes=[pltpu.VMEM((EXPERT_BLOCK, d), F32), pltpu.SemaphoreType.DMA,
                        pltpu.SemaphoreType.DMA],
    )
    return pl.pallas_call(
        _dispatch_kernel,
        out_shape=jax.ShapeDtypeStruct((n_rows, d), F32),
        grid_spec=grid_spec,
        compiler_params=_cparams(("arbitrary",)),
        name="dispatch",
    )(pstart, pblk, n_used, dest_flat, h2)


def _experts_kernel(be_ref, nu_ref, xs_ref, wgu_ref, bgu_ref, wd_ref, bd_ref, y_ref,
                    wgu_bf, wd_bf):
    b = pl.program_id(0)
    d_exp = wd_ref.shape[1]
    prev = be_ref[jnp.maximum(b - 1, 0)]
    changed = (b == 0) | (be_ref[b] != prev)

    @pl.when((b < nu_ref[0]) & changed)
    def _():
        rows = 128

        def cast_gu(c, carry):
            r0 = pl.multiple_of(c * rows, rows)
            wgu_bf[pl.ds(r0, rows), :] = wgu_ref[0, pl.ds(r0, rows), :].astype(BF16)
            return carry

        def cast_d(c, carry):
            r0 = pl.multiple_of(c * rows, rows)
            wd_bf[pl.ds(r0, rows), :] = wd_ref[0, pl.ds(r0, rows), :].astype(BF16)
            return carry

        lax.fori_loop(0, wgu_ref.shape[1] // rows, cast_gu, 0)
        lax.fori_loop(0, wd_ref.shape[1] // rows, cast_d, 0)

    @pl.when(b < nu_ref[0])
    def _():
        xb = xs_ref[...].astype(BF16)
        hdn = jnp.dot(xb, wgu_bf[...], preferred_element_type=F32) + bgu_ref[0]
        x_glu = jnp.minimum(hdn[:, :d_exp], SWIGLU_LIMIT)
        x_lin = jnp.clip(hdn[:, d_exp:], -SWIGLU_LIMIT, SWIGLU_LIMIT)
        act = x_glu * jax.nn.sigmoid(SWIGLU_ALPHA * x_glu) * (x_lin + 1.0)
        y_ref[...] = jnp.dot(act.astype(BF16), wd_bf[...], preferred_element_type=F32) + bd_ref[0]

    @pl.when(b >= nu_ref[0])
    def _():
        y_ref[...] = jnp.zeros_like(y_ref)


def _experts(block_e, n_used, xs, w_gate_up, b_gate_up, w_down, b_down):
    n_rows, d = xs.shape
    bm = EXPERT_BLOCK
    n_blk = n_rows // bm
    ne, _, two_de = w_gate_up.shape
    de = w_down.shape[1]

    def blk(b, be, nu):
        return jnp.minimum(b, nu[0] - 1)

    grid_spec = pltpu.PrefetchScalarGridSpec(
        num_scalar_prefetch=2,
        grid=(n_blk,),
        in_specs=[pl.BlockSpec((bm, d), lambda b, be, nu: (blk(b, be, nu), 0)),
                  pl.BlockSpec((1, d, two_de), lambda b, be, nu: (be[blk(b, be, nu)], 0, 0)),
                  pl.BlockSpec((1, 1, two_de), lambda b, be, nu: (be[blk(b, be, nu)], 0, 0)),
                  pl.BlockSpec((1, de, d), lambda b, be, nu: (be[blk(b, be, nu)], 0, 0)),
                  pl.BlockSpec((1, 1, d), lambda b, be, nu: (be[blk(b, be, nu)], 0, 0))],
        out_specs=pl.BlockSpec((bm, d), lambda b, be, nu: (b, 0)),
        scratch_shapes=[pltpu.VMEM((d, two_de), BF16), pltpu.VMEM((de, d), BF16)],
    )
    return pl.pallas_call(
        _experts_kernel,
        out_shape=jax.ShapeDtypeStruct((n_rows, d), F32),
        grid_spec=grid_spec,
        compiler_params=_cparams(("arbitrary",)),
        name="experts",
    )(block_e, n_used, xs, w_gate_up, b_gate_up.reshape(ne, 1, two_de), w_down, b_down.reshape(ne, 1, d))


def _combine_kernel(dest_ref, y_ref, gate_ref, x1_ref, gt_ref, gpost_ref, o_ref, buf, sem):
    tm = x1_ref.shape[0]

    def row_copy(r, k):
        src = dest_ref[r * TOP_K + k]
        return pltpu.make_async_copy(y_ref.at[pl.ds(src, 1)], buf.at[k, pl.ds(r, 1)], sem)

    def start(r, carry):
        for k in range(TOP_K):
            row_copy(r, k).start(priority=k % 2)
        return carry

    def wait(r, carry):
        for k in range(TOP_K):
            row_copy(r, k).wait()
        return carry

    lax.fori_loop(0, tm, start, 0)
    lax.fori_loop(0, tm, wait, 0)

    gate = gate_ref[...]
    acc = gate[:, 0:1] * buf[0]
    for k in range(1, TOP_K):
        acc = acc + gate[:, k:k + 1] * buf[k]
    o_ref[...] = x1_ref[...] + gt_ref[...] * (_rms(acc) * gpost_ref[...])


def _combine(dest_flat, y, gate4, x1, gt, gpost):
    s, d = x1.shape
    tm = COMBINE_TILE
    return pl.pallas_call(
        _combine_kernel,
        out_shape=jax.ShapeDtypeStruct((s, d), F32),
        grid=(s // tm,),
        in_specs=[pl.BlockSpec((tm * TOP_K,), lambda i: (i,), memory_space=pltpu.SMEM),
                  pl.BlockSpec(memory_space=pl.ANY),
                  pl.BlockSpec((tm, TOP_K), lambda i: (i, 0)),
                  pl.BlockSpec((tm, d), lambda i: (i, 0)),
                  pl.BlockSpec((1, d), lambda i: (0, 0)),
                  pl.BlockSpec((1, d), lambda i: (0, 0))],
        out_specs=pl.BlockSpec((tm, d), lambda i: (i, 0)),
        scratch_shapes=[pltpu.VMEM((TOP_K, tm, d), F32), pltpu.SemaphoreType.DMA],
        compiler_params=_cparams(("arbitrary",)),
        name="combine",
    )(dest_flat, y, gate4, x1, gt, gpost)


def _layer(x2, mod, g_pre_mix, g_post_mix, w_in, b_forget, rel_bias, w_out,
           g_pre_ffn, g_post_ffn, w_router, b_router, w_gate_up, b_gate_up, w_down, b_down):
    s, d = x2.shape
    sh_m, sc_m, gt_m, sh_f, sc_f, gt_f = [mod[:, k * d:(k + 1) * d] for k in range(6)]
    n_qkv = 3 * (N_HEADS_FOX + N_HEADS_MOBA) * HEAD_DIM
    fox_w = N_HEADS_FOX * HEAD_DIM

    w_qkv = w_in[:, :n_qkv].astype(BF16)
    w_f = w_in[:, n_qkv:].T
    qkv, cum, sel, nrm, tr = _inproj(x2, g_pre_mix.reshape(1, d), sc_m, sh_m, w_qkv, w_f,
                                     b_forget.reshape(-1, 1))
    y_a = _fox(qkv, cum, nrm)
    y_b = _moba(qkv, tr, sel, rel_bias)

    w_out_bf = w_out.astype(BF16)
    x1, h2, idx4, rank4, gate4, cnt = _outproj(
        y_a, y_b, x2, w_out_bf[:fox_w], w_out_bf[fox_w:], g_post_mix.reshape(1, d), gt_m,
        g_pre_ffn.reshape(1, d), sc_f, sh_f, w_router, b_router.reshape(1, -1))

    bm = EXPERT_BLOCK
    counts = cnt[0].astype(I32)
    pblk = (counts + bm - 1) // bm
    pend_blk = jnp.cumsum(pblk)
    pstart = ((pend_blk - pblk) * bm).astype(I32)
    n_rows = s * TOP_K + N_EXPERTS * bm
    n_blk = n_rows // bm
    block_e = jnp.minimum(jnp.sum(pend_blk[None, :] <= jnp.arange(n_blk)[:, None], axis=1),
                          N_EXPERTS - 1).astype(I32)
    n_used = pend_blk[-1:].astype(I32)

    dest_flat = _dest(idx4, rank4, pstart).reshape(-1)
    xs = _dispatch(pstart, pblk.astype(I32), n_used, dest_flat, h2, n_rows)
    y = _experts(block_e, n_used, xs, w_gate_up, b_gate_up, w_down, b_down)
    return _combine(dest_flat, y, gate4, x1, gt_f, g_post_ffn.reshape(1, d))


def kernel(x, c, w_ada, b_ada, g_pre_mix, g_post_mix, w_in, b_forget, rel_bias, w_out, g_pre_ffn, g_post_ffn, w_router, b_router, w_gate_up, b_gate_up, w_down, b_down):
    bsz, s, d = x.shape
    depth = w_ada.shape[0]
    outs = []
    for bi in range(bsz):
        x2 = x[bi]
        for l in range(depth):
            mod = _adaln(c[bi:bi + 1], w_ada[l], b_ada[l])
            x2 = _layer(x2, mod, g_pre_mix[l], g_post_mix[l], w_in[l], b_forget[l], rel_bias, w_out[l],
                        g_pre_ffn[l], g_post_ffn[l], w_router[l], b_router[l], w_gate_up[l], b_gate_up[l],
                        w_down[l], b_down[l])
        outs.append(x2)
    return jnp.stack(outs)
```

```python
import functools
import math

import numpy as np
import jax
import jax.numpy as jnp
from jax import lax
from jax.experimental import pallas as pl
from jax.experimental.pallas import tpu as pltpu

F32 = jnp.float32
BF16 = jnp.bfloat16
I32 = jnp.int32

HEAD_DIM = 64
N_HEADS_FOX = 8
N_HEADS_MOBA = 8
PAIR = 2 * HEAD_DIM
MOBA_BLOCK = 256
MOBA_TOPK = 3
NUM_BUCKETS = 32
MAX_DISTANCE = 128
N_EXPERTS = 32
TOP_K = 4
SWIGLU_LIMIT = 7.0
SWIGLU_ALPHA = 1.702
RMS_EPS = 1e-6
NEG = -(2.0 ** 100)
M_INIT = -(2.0 ** 99)
LOG2E = math.log2(math.e)
SUM_ROWS = 16
EXP_UNDERFLOW = 90.0
VMEM_LIMIT = 56 * 1024 * 1024

ROW_TILE = 512
FOX_TILE = 256
EXPERT_BLOCK = 256
COMBINE_TILE = 256

NT_DIMS = (((1,), (1,)), ((), ()))


def _cparams(sem):
    return pltpu.CompilerParams(dimension_semantics=sem, vmem_limit_bytes=VMEM_LIMIT)


def _rms(x):
    return x * lax.rsqrt(jnp.mean(x * x, axis=-1, keepdims=True) + RMS_EPS)


def _adaln_kernel(c_ref, w_ref, b_ref, o_ref):
    c = c_ref[...]
    cond = c * jax.nn.sigmoid(c)
    o_ref[...] = jnp.dot(cond, w_ref[...], preferred_element_type=F32,
                         precision=lax.Precision.HIGHEST) + b_ref[...]


def _adaln(c, w_ada, b_ada):
    d = c.shape[-1]
    n = w_ada.shape[-1]
    c8 = jnp.broadcast_to(c.reshape(1, d), (8, d))
    out = pl.pallas_call(
        _adaln_kernel,
        out_shape=jax.ShapeDtypeStruct((8, n), F32),
        grid=(n // d,),
        in_specs=[pl.BlockSpec((8, d), lambda j: (0, 0)),
                  pl.BlockSpec((d, d), lambda j: (0, j)),
                  pl.BlockSpec((1, d), lambda j: (0, j))],
        out_specs=pl.BlockSpec((8, d), lambda j: (0, j)),
        compiler_params=_cparams(("arbitrary",)),
        name="adaln",
    )(c8, w_ada, b_ada.reshape(1, n))
    return out[0:1]


def _inproj_kernel(x_ref, g_ref, sc_ref, sh_ref, w_ref, wf_ref, bf_ref,
                   qkv_ref, cum_ref, sel_ref, nrm_ref, tr_ref, nr_ref, km_scr, carry_scr):
    i = pl.program_id(0)
    tm = x_ref.shape[0]
    nblk = km_scr.shape[0]

    @pl.when(i == 0)
    def _():
        km_scr[...] = jnp.zeros_like(km_scr)
        carry_scr[...] = jnp.zeros_like(carry_scr)

    x = x_ref[...]
    h = _rms(x) * g_ref[...] * (1.0 + sc_ref[...]) + sh_ref[...]
    hb = h.astype(BF16)

    width = N_HEADS_FOX * HEAD_DIM
    hsel = jnp.where(lax.broadcasted_iota(I32, (width, N_HEADS_FOX), 0) // HEAD_DIM
                     == lax.broadcasted_iota(I32, (width, N_HEADS_FOX), 1), 1.0, 0.0)
    qb = kb = None
    tr_slot = {3: 0, 5: 1, 0: 2, 2: 3}
    for c in range(6):
        pc = jnp.dot(hb, w_ref[:, c * width:(c + 1) * width], preferred_element_type=F32)
        if c == 0 or c == 3:
            pc_t = pc * (LOG2E * HEAD_DIM ** -0.5)
        if c in tr_slot:
            n = tr_slot[c]
            tr_ref[n * width:(n + 1) * width, :] = (pc_t if c in (0, 3) else pc).T.astype(BF16)
        if c == 1 or c == 4:
            qkv_ref[:, (c // 3) * width:(c // 3 + 1) * width] = pc.astype(BF16)
        if c < 2:
            sq = (pc * (HEAD_DIM ** -0.5) if c == 0 else pc).astype(BF16).astype(F32)
            n2 = jnp.dot((sq * sq).astype(BF16), hsel.astype(BF16), preferred_element_type=F32)
            nrm_ref[0, c:c + 1, :] = jnp.max(n2, axis=0, keepdims=True)
        if c == 3:
            qb = pc
        if c == 4:
            kb = pc

    ft = lax.dot_general(wf_ref[...], h, NT_DIMS, preferred_element_type=F32,
                         precision=lax.Precision.HIGHEST)
    z = ft + bf_ref[...]
    logf = -(jnp.maximum(-z, 0.0) + jnp.log1p(jnp.exp(-jnp.abs(z))))
    lane = lax.broadcasted_iota(I32, logf.shape, 1)
    cs = logf
    sh = 1
    while sh < tm:
        cs = cs + jnp.where(lane >= sh, pltpu.roll(cs, sh, axis=1), 0.0)
        sh *= 2
    base = jnp.zeros_like(cs)
    for b in range(1, tm // FOX_TILE):
        base = jnp.where(lane >= b * FOX_TILE, cs[:, b * FOX_TILE - 1:b * FOX_TILE], base)
    nr_ref[...] = (-LOG2E * (cs - base)).T
    cs = cs + carry_scr[...]
    cum_ref[...] = cs
    carry_scr[...] = cs[:, tm - 1:tm]

    nb_tile = tm // MOBA_BLOCK
    for b in range(nb_tile):
        kmean = jnp.sum(kb[b * MOBA_BLOCK:(b + 1) * MOBA_BLOCK], axis=0, keepdims=True) * (1.0 / MOBA_BLOCK)
        km_scr[pl.ds(i * nb_tile + b, 1), :] = kmean

    km = km_scr[...]
    lane_h = lax.broadcasted_iota(I32, km.shape, 1) // HEAD_DIM
    blk = lax.broadcasted_iota(I32, (nblk, tm), 0)
    col = lax.broadcasted_iota(I32, (nblk, tm), 1)
    own = i * nb_tile + col // MOBA_BLOCK
    for hd in range(N_HEADS_MOBA):
        kmh = jnp.where(lane_h == hd, km, 0.0)
        g = lax.dot_general(kmh, qb, NT_DIMS, preferred_element_type=F32,
                            precision=lax.Precision.HIGHEST)
        g = jnp.where(blk < own, g, -jnp.inf)
        sel = jnp.zeros(g.shape, dtype=jnp.bool_)
        for _ in range(MOBA_TOPK):
            m = jnp.max(g, axis=0, keepdims=True)
            first = jnp.min(jnp.where(g == m, blk, nblk), axis=0, keepdims=True)
            pick = (blk == first) & (m > -jnp.inf)
            sel = sel | pick
            g = jnp.where(pick, -jnp.inf, g)
        sel_ref[hd] = jnp.where(sel, 0.0, NEG)


def _inproj(x2, g, sc, sh, w_qkv, w_f, b_f):
    s, d = x2.shape
    tm = ROW_TILE
    nblk = s // MOBA_BLOCK
    n = w_qkv.shape[1]
    width = N_HEADS_FOX * HEAD_DIM
    n_rows_out = 2 * width
    n_tr = 4 * width
    assert tm % FOX_TILE == 0 and N_HEADS_FOX == N_HEADS_MOBA
    return pl.pallas_call(
        _inproj_kernel,
        out_shape=(jax.ShapeDtypeStruct((s, n_rows_out), BF16),
                   jax.ShapeDtypeStruct((N_HEADS_FOX, s), F32),
                   jax.ShapeDtypeStruct((N_HEADS_MOBA, nblk, s), F32),
                   jax.ShapeDtypeStruct((s // tm, 2, N_HEADS_FOX), F32),
                   jax.ShapeDtypeStruct((n_tr, s), BF16),
                   jax.ShapeDtypeStruct((s, N_HEADS_FOX), F32)),
        grid=(s // tm,),
        in_specs=[pl.BlockSpec((tm, d), lambda i: (i, 0)),
                  pl.BlockSpec((1, d), lambda i: (0, 0)),
                  pl.BlockSpec((1, d), lambda i: (0, 0)),
                  pl.BlockSpec((1, d), lambda i: (0, 0)),
                  pl.BlockSpec((d, n), lambda i: (0, 0)),
                  pl.BlockSpec((N_HEADS_FOX, d), lambda i: (0, 0)),
                  pl.BlockSpec((N_HEADS_FOX, 1), lambda i: (0, 0))],
        out_specs=(pl.BlockSpec((tm, n_rows_out), lambda i: (i, 0)),
                   pl.BlockSpec((N_HEADS_FOX, tm), lambda i: (0, i)),
                   pl.BlockSpec((N_HEADS_MOBA, nblk, tm), lambda i: (0, 0, i)),
                   pl.BlockSpec((1, 2, N_HEADS_FOX), lambda i: (i, 0, 0)),
                   pl.BlockSpec((n_tr, tm), lambda i: (0, i)),
                   pl.BlockSpec((tm, N_HEADS_FOX), lambda i: (i, 0))),
        scratch_shapes=[pltpu.VMEM((nblk, N_HEADS_MOBA * HEAD_DIM), F32),
                        pltpu.VMEM((N_HEADS_FOX, 1), F32)],
        compiler_params=_cparams(("arbitrary",)),
        name="inproj",
    )(x2, g, sc, sh, w_qkv, w_f, b_f)


def _fox_kernel(jlo_ref, cp_ref, qt_ref, k_ref, vt_ref, nr_ref, o_ref, m_ref, a_ref):
    p = pl.program_id(0)
    i = pl.program_id(1)
    j_first = jlo_ref[p * pl.num_programs(1) + i]
    t = qt_ref.shape[1]

    qt = qt_ref[...]
    top = lax.broadcasted_iota(I32, qt.shape, 0) < HEAD_DIM
    zq = jnp.zeros_like(qt)
    qth = (jnp.where(top, qt, zq), jnp.where(top, zq, qt))
    m_ref[...] = jnp.full(m_ref.shape, M_INIT, F32)
    a_ref[...] = jnp.zeros(a_ref.shape, F32)
    er = lax.broadcasted_iota(I32, (SUM_ROWS, 2 * t), 0)
    ec = lax.broadcasted_iota(I32, (SUM_ROWS, 2 * t), 1)
    ones_rows = jnp.where(((er == 0) & (ec < t)) | ((er == 1) & (ec >= t)), 1.0, 0.0).astype(BF16)
    arow = lax.broadcasted_iota(I32, a_ref.shape, 0)
    head0_rows = (arow < HEAD_DIM) | (arow == PAIR)

    def tile(j, diagonal):
        k0 = pl.multiple_of(j * t, t)
        kt = k_ref[pl.ds(k0, t), :]
        vt = vt_ref[:, pl.ds(k0, t)]
        ps, alphas = [], []
        for hd in range(2):
            h = 2 * p + hd
            s = jnp.dot(kt, qth[hd], preferred_element_type=F32) + nr_ref[0, pl.ds(k0, t), hd:hd + 1]
            if diagonal:
                r = lax.broadcasted_iota(I32, s.shape, 0)
                c = lax.broadcasted_iota(I32, s.shape, 1)
                s = jnp.where(r <= c, s, NEG)
            pair_shift = LOG2E * (cp_ref[h, i] - cp_ref[h, j])
            m_old = m_ref[hd]
            m_new = jnp.maximum(m_old, jnp.max(s, axis=0, keepdims=True) + pair_shift)
            alphas.append(jnp.exp2(m_old - m_new))
            m_ref[hd] = m_new
            ps.append(jnp.exp2(s - (m_new - pair_shift)).astype(BF16))
        zv = jnp.zeros_like(vt)
        vcat = jnp.concatenate([jnp.where(top, vt, zv), jnp.where(top, zv, vt)], axis=1)
        vcat = jnp.concatenate([vcat, ones_rows], axis=0)
        pcat = jnp.concatenate(ps, axis=0)
        a_ref[...] = (a_ref[...] * jnp.where(head0_rows, alphas[0], alphas[1])
                      + jnp.dot(vcat, pcat, preferred_element_type=F32))

    def body(j, carry):
        tile(j, False)
        return carry

    lax.fori_loop(j_first, i, body, 0)
    tile(i, True)
    out_t = a_ref[0:PAIR, :] / jnp.where(top, a_ref[PAIR:PAIR + 1, :], a_ref[PAIR + 1:PAIR + 2, :])
    o_ref[...] = out_t.T.astype(o_ref.dtype)


def _fox_first_tile(cum, nrm, t):
    cend = cum[:, t - 1::t]
    nt = cend.shape[1]
    cprev = jnp.concatenate([jnp.zeros((cend.shape[0], 1), F32), cend[:, :-1]], axis=1)
    rep = nt // nrm.shape[0]
    qn = jnp.repeat(jnp.sqrt(nrm[:, 0, :]).T, rep, axis=1)
    kn = jnp.repeat(jnp.sqrt(nrm[:, 1, :]).T, rep, axis=1)
    gap = (1.02 * qn[:, :, None] * (kn[:, None, :] + kn[:, :, None])
           + cprev[:, :, None] - cend[:, None, :])
    jj = jnp.arange(nt)[None, None, :]
    ii = jnp.arange(nt)[None, :, None]
    needed = (jj < ii) & jnp.logical_not(gap <= -EXP_UNDERFLOW)
    needed = needed[0::2] | needed[1::2]
    first = jnp.min(jnp.where(needed, jj, ii), axis=2)
    return first.reshape(-1).astype(I32)


def _fox(qkv, tr, nr, cum, nrm):
    s = qkv.shape[0]
    t = FOX_TILE
    npair = N_HEADS_FOX // 2
    jlo = _fox_first_tile(cum, nrm, t)
    cend = cum[:, t - 1::t]
    cprev = jnp.concatenate([jnp.zeros((cend.shape[0], 1), F32), cend[:, :-1]], axis=1)
    q_rows = 2 * (N_HEADS_MOBA // 2)
    grid_spec = pltpu.PrefetchScalarGridSpec(
        num_scalar_prefetch=1,
        grid=(npair, s // t),
        in_specs=[pl.BlockSpec(memory_space=pltpu.SMEM),
                  pl.BlockSpec((PAIR, t), lambda p, i, jl: (q_rows + p, i)),
                  pl.BlockSpec((s, PAIR), lambda p, i, jl: (0, p)),
                  pl.BlockSpec((PAIR, s), lambda p, i, jl: (q_rows + npair + p, 0)),
                  pl.BlockSpec((1, s, 2), lambda p, i, jl: (p, 0, 0))],
        out_specs=pl.BlockSpec((t, PAIR), lambda p, i, jl: (i, p)),
        scratch_shapes=[pltpu.VMEM((2, 1, t), F32), pltpu.VMEM((PAIR + SUM_ROWS, t), F32)],
    )
    nr_pairs = nr.reshape(s, npair, 2).transpose(1, 0, 2)
    return pl.pallas_call(
        _fox_kernel,
        out_shape=jax.ShapeDtypeStruct((s, npair * PAIR), BF16),
        grid_spec=grid_spec,
        compiler_params=_cparams(("arbitrary", "arbitrary")),
        name="fox",
    )(jlo, cprev, tr, qkv, tr, nr_pairs)


def _t5_bucket_np(dist):
    dist = np.maximum(dist, 0)
    max_exact = NUM_BUCKETS // 2
    d = np.maximum(dist, 1).astype(np.float32)
    large = max_exact + (np.log(d / np.float32(max_exact)) / np.float32(math.log(MAX_DISTANCE / max_exact))
                         * np.float32(NUM_BUCKETS - max_exact)).astype(np.int32)
    large = np.minimum(large, NUM_BUCKETS - 1)
    return np.where(dist < max_exact, dist, large).astype(np.int32)


def _moba_kernel(rb_ref, qt_ref, k_ref, vt_ref, sel_ref, bkt_ref, o_ref,
                 m_ref, a_ref, bias_scr, sa_ref, sb_ref, pa_ref, pb_ref, ala_ref, alb_ref):
    p = pl.program_id(0)
    g = pl.program_id(1)
    tq = qt_ref.shape[1]
    t = tq // 2

    @pl.when(g == 0)
    def _():
        r = lax.broadcasted_iota(I32, (t, t), 0)
        c = lax.broadcasted_iota(I32, (t, t), 1)
        zero = jnp.zeros((t, t), F32)
        for hd in range(2):
            h = 2 * p + hd
            far = rb_ref[(NUM_BUCKETS - 1) * N_HEADS_MOBA + h]
            tiles = []
            for w in range(2):
                bkt = bkt_ref[w]
                acc = jnp.zeros(bkt.shape, F32)
                for kk in range(NUM_BUCKETS):
                    acc = acc + jnp.where(bkt == kk, rb_ref[kk * N_HEADS_MOBA + h], 0.0)
                tiles.append((acc - far) * LOG2E)
            prev_t = tiles[0]
            own_t = jnp.where(r <= c, tiles[1], NEG)
            bias_scr[hd, 0] = jnp.concatenate([prev_t, zero], axis=1)
            bias_scr[hd, 1] = jnp.concatenate([own_t, prev_t], axis=1)
            bias_scr[hd, 2] = jnp.concatenate([zero, own_t], axis=1)

    qt = qt_ref[...]
    top = lax.broadcasted_iota(I32, qt.shape, 0) < HEAD_DIM
    zq = jnp.zeros_like(qt)
    qth = (jnp.where(top, qt, zq), jnp.where(top, zq, qt))
    m_ref[...] = jnp.full(m_ref.shape, M_INIT, F32)
    a_ref[...] = jnp.zeros(a_ref.shape, F32)

    er = lax.broadcasted_iota(I32, (SUM_ROWS, 2 * t), 0)
    ec = lax.broadcasted_iota(I32, (SUM_ROWS, 2 * t), 1)
    ones_rows = jnp.where(((er == 0) & (ec < t)) | ((er == 1) & (ec >= t)), 1.0, 0.0).astype(BF16)
    arow = lax.broadcasted_iota(I32, a_ref.shape, 0)
    head0_rows = (arow < HEAD_DIM) | (arow == PAIR)
    vtop = lax.broadcasted_iota(I32, (PAIR, t), 0) < HEAD_DIM
    in_a = lax.broadcasted_iota(I32, (1, tq), 1) < t

    n_far = jnp.maximum(2 * g - 1, 0)

    def produce(j, s_buf):
        k0 = pl.multiple_of(j * t, t)
        kt = k_ref[pl.ds(k0, t), :]
        for hd in range(2):
            s_buf[hd] = jnp.dot(kt, qth[hd], preferred_element_type=F32)

    def softmax(s_buf, p_buf, al_buf, selrows, w):
        for hd in range(2):
            s = s_buf[hd]
            if w is not None:
                s = s + bias_scr[hd, w]
            smax = jnp.max(s, axis=0, keepdims=True)
            m_old = m_ref[hd]
            m_new = jnp.maximum(m_old, smax + selrows[hd])
            shift = m_new - selrows[hd]
            al_buf[hd] = jnp.exp2(m_old - m_new)
            m_ref[hd] = m_new
            p_buf[hd * t:(hd + 1) * t, :] = jnp.exp2(s - shift).astype(BF16)

    def far_rows(j):
        return [jnp.where(j < n_far, sel_ref[hd, pl.ds(j, 1), :], NEG) for hd in range(2)]

    def accumulate(j, p_buf, al_buf):
        k0 = pl.multiple_of(j * t, t)
        vt = vt_ref[:, pl.ds(k0, t)]
        zv = jnp.zeros_like(vt)
        vcat = jnp.concatenate([jnp.where(vtop, vt, zv), jnp.where(vtop, zv, vt)], axis=1)
        vcat = jnp.concatenate([vcat, ones_rows], axis=0)
        a_ref[...] = (a_ref[...] * jnp.where(head0_rows, al_buf[0], al_buf[1])
                      + jnp.dot(vcat, p_buf[...], preferred_element_type=F32))

    pb_ref[...] = jnp.zeros_like(pb_ref)
    alb_ref[...] = jnp.ones_like(alb_ref)
    produce(0, sa_ref)

    def body(u, carry):
        j0 = 2 * u
        produce(j0 + 1, sb_ref)
        softmax(sa_ref, pa_ref, ala_ref, far_rows(j0), None)
        accumulate(jnp.maximum(j0 - 1, 0), pb_ref, alb_ref)
        produce(j0 + 2, sa_ref)
        accumulate(j0, pa_ref, ala_ref)
        softmax(sb_ref, pb_ref, alb_ref, far_rows(j0 + 1), None)
        return carry

    n_pairs = (n_far + 1) // 2
    lax.fori_loop(0, n_pairs, body, 0)

    j1 = jnp.maximum(2 * g - 1, 0)
    j2 = 2 * g
    j3 = 2 * g + 1
    rows1 = [jnp.where(g >= 1, sel_ref[hd, pl.ds(j1, 1), :], NEG) for hd in range(2)]
    rows2 = [jnp.where(in_a, 0.0, sel_ref[hd, pl.ds(j2, 1), :]) for hd in range(2)]
    rows3 = [jnp.where(in_a, NEG, 0.0)] * 2
    produce(j1, sa_ref)
    produce(j2, sb_ref)
    softmax(sa_ref, pa_ref, ala_ref, rows1, 0)
    accumulate(jnp.maximum(2 * n_pairs - 1, 0), pb_ref, alb_ref)
    produce(j3, sa_ref)
    softmax(sb_ref, pb_ref, alb_ref, rows2, 1)
    accumulate(j1, pa_ref, ala_ref)
    softmax(sa_ref, pa_ref, ala_ref, rows3, 2)
    accumulate(j2, pb_ref, alb_ref)
    accumulate(j3, pa_ref, ala_ref)
    out_t = a_ref[0:PAIR, :] / jnp.where(top, a_ref[PAIR:PAIR + 1, :], a_ref[PAIR + 1:PAIR + 2, :])
    o_ref[...] = out_t.T.astype(o_ref.dtype)


def _moba(qkv, tr, sel, rel_bias):
    s = qkv.shape[0]
    t = MOBA_BLOCK
    npair = N_HEADS_MOBA // 2
    nblk = s // t
    kcol = N_HEADS_FOX // 2
    a = np.arange(t)[None, :]
    b = np.arange(t)[:, None]
    bkt = jnp.asarray(np.stack([_t5_bucket_np(t + a - b), _t5_bucket_np(a - b)]))
    tq = 2 * t
    grid_spec = pltpu.PrefetchScalarGridSpec(
        num_scalar_prefetch=1,
        grid=(npair, s // tq),
        in_specs=[pl.BlockSpec((PAIR, tq), lambda p, i, rb: (p, i)),
                  pl.BlockSpec((s, PAIR), lambda p, i, rb: (0, kcol + p)),
                  pl.BlockSpec((PAIR, s), lambda p, i, rb: (npair + p, 0)),
                  pl.BlockSpec((2, nblk, tq), lambda p, i, rb: (p, 0, i)),
                  pl.BlockSpec((2, t, t), lambda p, i, rb: (0, 0, 0))],
        out_specs=pl.BlockSpec((tq, PAIR), lambda p, i, rb: (i, p)),
        scratch_shapes=[pltpu.VMEM((2, 1, tq), F32),
                        pltpu.VMEM((PAIR + SUM_ROWS, tq), F32), pltpu.VMEM((2, 3, t, tq), F32),
                        pltpu.VMEM((2, t, tq), F32), pltpu.VMEM((2, t, tq), F32),
                        pltpu.VMEM((2 * t, tq), BF16), pltpu.VMEM((2 * t, tq), BF16),
                        pltpu.VMEM((2, 1, tq), F32), pltpu.VMEM((2, 1, tq), F32)],
    )
    return pl.pallas_call(
        _moba_kernel,
        out_shape=jax.ShapeDtypeStruct((s, npair * PAIR), BF16),
        grid_spec=grid_spec,
        compiler_params=_cparams(("arbitrary", "arbitrary")),
        name="moba",
    )(rel_bias.reshape(-1), tr, qkv, tr, sel, bkt)


def _outproj_kernel(ya_ref, yb_ref, x_ref, wa_ref, wb_ref, gpost_ref, gt_ref, gpre_ref,
                    sc_ref, sh_ref, wr_ref, br_ref,
                    x1_ref, h2_ref, idx_ref, rank_ref, gate_ref, cnt_ref, carry_scr):
    i = pl.program_id(0)
    tm = x_ref.shape[0]

    @pl.when(i == 0)
    def _():
        carry_scr[...] = jnp.zeros_like(carry_scr)

    y = (jnp.dot(ya_ref[...], wa_ref[...], preferred_element_type=F32)
         + jnp.dot(yb_ref[...], wb_ref[...], preferred_element_type=F32))
    x1 = x_ref[...] + gt_ref[...] * (_rms(y) * gpost_ref[...])
    x1_ref[...] = x1
    h2 = _rms(x1) * gpre_ref[...] * (1.0 + sc_ref[...]) + sh_ref[...]
    h2_ref[...] = h2

    logits = jnp.dot(h2, wr_ref[...], preferred_element_type=F32,
                     precision=lax.Precision.HIGHEST) + br_ref[...]
    ne = logits.shape[1]
    lane = lax.broadcasted_iota(I32, logits.shape, 1)
    lane4 = lax.broadcasted_iota(I32, (tm, TOP_K), 1)
    g = logits
    mask = jnp.zeros(logits.shape, F32)
    vals, picks = [], []
    for _ in range(TOP_K):
        m = jnp.max(g, axis=1, keepdims=True)
        first = jnp.min(jnp.where(g == m, lane, ne), axis=1, keepdims=True)
        pick = lane == first
        mask = jnp.where(pick, 1.0, mask)
        g = jnp.where(pick, -jnp.inf, g)
        vals.append(m)
        picks.append(pick)
    ex = [jnp.exp(v - vals[0]) for v in vals]
    den = ex[0] + ex[1] + ex[2] + ex[3]
    gates = [e / den for e in ex]

    r = lax.broadcasted_iota(I32, (tm, tm), 0)
    c = lax.broadcasted_iota(I32, (tm, tm), 1)
    tril = jnp.where(c < r, 1.0, 0.0).astype(BF16)
    before = jnp.dot(tril, mask.astype(BF16), preferred_element_type=F32) + carry_scr[...]
    total = carry_scr[...] + jnp.sum(mask, axis=0, keepdims=True)
    carry_scr[...] = total
    cnt_ref[...] = jnp.broadcast_to(total, cnt_ref.shape)

    def pack4(cols):
        return jnp.where(lane4 == 0, cols[0],
                         jnp.where(lane4 == 1, cols[1], jnp.where(lane4 == 2, cols[2], cols[3])))

    ranks = [jnp.sum(jnp.where(pk, before, 0.0), axis=1, keepdims=True) for pk in picks]
    idxs = [jnp.sum(jnp.where(pk, lane, 0), axis=1, keepdims=True) for pk in picks]
    idx_ref[...] = pack4(idxs)
    rank_ref[...] = pack4(ranks).astype(I32)
    gate_ref[...] = pack4(gates)


def _outproj(mix_a, mix_b, x2, w_a, w_b, gpost, gt, gpre, sc, sh, w_router, b_router):
    s, d = x2.shape
    tm = ROW_TILE
    ne = w_router.shape[1]
    wa = mix_a.shape[1]
    row = lambda i: (i, 0)
    fix = lambda i: (0, 0)
    vec = pl.BlockSpec((1, d), fix)
    return pl.pallas_call(
        _outproj_kernel,
        out_shape=(jax.ShapeDtypeStruct((s, d), F32),
                   jax.ShapeDtypeStruct((s, d), F32),
                   jax.ShapeDtypeStruct((s, TOP_K), I32),
                   jax.ShapeDtypeStruct((s, TOP_K), I32),
                   jax.ShapeDtypeStruct((s, TOP_K), F32),
                   jax.ShapeDtypeStruct((8, ne), F32)),
        grid=(s // tm,),
        in_specs=[pl.BlockSpec((tm, wa), row), pl.BlockSpec((tm, wa), row), pl.BlockSpec((tm, d), row),
                  pl.BlockSpec((wa, d), fix), pl.BlockSpec((wa, d), fix),
                  vec, vec, vec, vec, vec,
                  pl.BlockSpec((d, ne), fix), pl.BlockSpec((1, ne), fix)],
        out_specs=(pl.BlockSpec((tm, d), row), pl.BlockSpec((tm, d), row),
                   pl.BlockSpec((tm, TOP_K), row), pl.BlockSpec((tm, TOP_K), row),
                   pl.BlockSpec((tm, TOP_K), row), pl.BlockSpec((8, ne), fix)),
        scratch_shapes=[pltpu.VMEM((1, ne), F32)],
        compiler_params=_cparams(("arbitrary",)),
        name="outproj",
    )(mix_a, mix_b, x2, w_a, w_b, gpost, gt, gpre, sc, sh, w_router, b_router)


def _dest_kernel(idx_ref, rank_ref, pstart_ref, o_ref):
    idx = idx_ref[...]
    tm = idx.shape[0]
    ne = pstart_ref.shape[1]
    lane = lax.broadcasted_iota(I32, (tm, ne), 1)
    lane4 = lax.broadcasted_iota(I32, idx.shape, 1)
    out = rank_ref[...]
    for k in range(TOP_K):
        start = jnp.sum(jnp.where(lane == idx[:, k:k + 1], pstart_ref[...], 0), axis=1, keepdims=True)
        out = out + jnp.where(lane4 == k, start, 0)
    o_ref[...] = out


def _dest(idx4, rank4, pstart):
    s = idx4.shape[0]
    tm = min(4 * ROW_TILE, s)
    row = lambda i: (i, 0)
    return pl.pallas_call(
        _dest_kernel,
        out_shape=jax.ShapeDtypeStruct((s, TOP_K), I32),
        grid=(s // tm,),
        in_specs=[pl.BlockSpec((tm, TOP_K), row), pl.BlockSpec((tm, TOP_K), row),
                  pl.BlockSpec((1, N_EXPERTS), lambda i: (0, 0))],
        out_specs=pl.BlockSpec((tm, TOP_K), row),
        compiler_params=_cparams(("arbitrary",)),
        name="dest",
    )(idx4, rank4, pstart.reshape(1, -1))


def _dispatch_kernel(pstart_ref, pblk_ref, nu_ref, dest_ref, h_ref, xs_ref,
                     zero_scr, sem, zsem):
    tm = h_ref.shape[0]
    bm = zero_scr.shape[0]
    n_blk = xs_ref.shape[0] // bm

    @pl.when(pl.program_id(0) == 0)
    def _():
        zero_scr[...] = jnp.zeros_like(zero_scr)

        def zero_copy(row0):
            return pltpu.make_async_copy(zero_scr, xs_ref.at[pl.ds(pl.multiple_of(row0, bm), bm)], zsem)

        for phase in range(2):
            for e in range(N_EXPERTS):
                last = pstart_ref[e] + (pblk_ref[e] - 1) * bm
                tail = (n_blk - N_EXPERTS + e) * bm
                for cond, row0 in ((pblk_ref[e] > 0, last), (n_blk - N_EXPERTS + e >= nu_ref[0], tail)):
                    @pl.when(cond)
                    def _():
                        if phase == 0:
                            zero_copy(row0).start()
                        else:
                            zero_copy(row0).wait()

    def row_copy(r, k):
        dst = dest_ref[r * TOP_K + k]
        return pltpu.make_async_copy(h_ref.at[pl.ds(r, 1)], xs_ref.at[pl.ds(dst, 1)], sem)

    def start(r, carry):
        for k in range(TOP_K):
            row_copy(r, k).start(priority=k % 2)
        return carry

    def wait(r, carry):
        for k in range(TOP_K):
            row_copy(r, k).wait()
        return carry

    lax.fori_loop(0, tm, start, 0)
    lax.fori_loop(0, tm, wait, 0)


def _dispatch(pstart, pblk, n_used, dest_flat, h2, n_rows):
    s, d = h2.shape
    tm = COMBINE_TILE
    grid_spec = pltpu.PrefetchScalarGridSpec(
        num_scalar_prefetch=3,
        grid=(s // tm,),
        in_specs=[pl.BlockSpec((tm * TOP_K,), lambda i, *_: (i,), memory_space=pltpu.SMEM),
                  pl.BlockSpec((tm, d), lambda i, *_: (i, 0))],
        out_specs=pl.BlockSpec(memory_space=pl.ANY),
        scratch_shapes=[pltpu.VMEM((EXPERT_BLOCK, d), F32), pltpu.SemaphoreType.DMA,
                        pltpu.SemaphoreType.DMA],
    )
    return pl.pallas_call(
        _dispatch_kernel,
        out_shape=jax.ShapeDtypeStruct((n_rows, d), F32),
        grid_spec=grid_spec,
        compiler_params=_cparams(("arbitrary",)),
        name="dispatch",
    )(pstart, pblk, n_used, dest_flat, h2)


def _experts_kernel(be_ref, nu_ref, xs_ref, wgu_ref, bgu_ref, wd_ref, bd_ref, y_ref,
                    wgu_bf, wd_bf):
    b = pl.program_id(0)
    d_exp = wd_ref.shape[1]
    prev = be_ref[jnp.maximum(b - 1, 0)]
    changed = (b == 0) | (be_ref[b] != prev)

    @pl.when((b < nu_ref[0]) & changed)
    def _():
        rows = 128

        def cast_gu(c, carry):
            r0 = pl.multiple_of(c * rows, rows)
            wgu_bf[pl.ds(r0, rows), :] = wgu_ref[0, pl.ds(r0, rows), :].astype(BF16)
            return carry

        def cast_d(c, carry):
            r0 = pl.multiple_of(c * rows, rows)
            wd_bf[pl.ds(r0, rows), :] = wd_ref[0, pl.ds(r0, rows), :].astype(BF16)
            return carry

        lax.fori_loop(0, wgu_ref.shape[1] // rows, cast_gu, 0)
        lax.fori_loop(0, wd_ref.shape[1] // rows, cast_d, 0)

    @pl.when(b < nu_ref[0])
    def _():
        xb = xs_ref[...].astype(BF16)
        hdn = jnp.dot(xb, wgu_bf[...], preferred_element_type=F32) + bgu_ref[0]
        x_glu = jnp.minimum(hdn[:, :d_exp], SWIGLU_LIMIT)
        x_lin = jnp.clip(hdn[:, d_exp:], -SWIGLU_LIMIT, SWIGLU_LIMIT)
        act = x_glu * jax.nn.sigmoid(SWIGLU_ALPHA * x_glu) * (x_lin + 1.0)
        y_ref[...] = jnp.dot(act.astype(BF16), wd_bf[...], preferred_element_type=F32) + bd_ref[0]

    @pl.when(b >= nu_ref[0])
    def _():
        y_ref[...] = jnp.zeros_like(y_ref)


def _experts(block_e, n_used, xs, w_gate_up, b_gate_up, w_down, b_down):
    n_rows, d = xs.shape
    bm = EXPERT_BLOCK
    n_blk = n_rows // bm
    ne, _, two_de = w_gate_up.shape
    de = w_down.shape[1]

    def blk(b, be, nu):
        return jnp.minimum(b, nu[0] - 1)

    grid_spec = pltpu.PrefetchScalarGridSpec(
        num_scalar_prefetch=2,
        grid=(n_blk,),
        in_specs=[pl.BlockSpec((bm, d), lambda b, be, nu: (blk(b, be, nu), 0)),
                  pl.BlockSpec((1, d, two_de), lambda b, be, nu: (be[blk(b, be, nu)], 0, 0)),
                  pl.BlockSpec((1, 1, two_de), lambda b, be, nu: (be[blk(b, be, nu)], 0, 0)),
                  pl.BlockSpec((1, de, d), lambda b, be, nu: (be[blk(b, be, nu)], 0, 0)),
                  pl.BlockSpec((1, 1, d), lambda b, be, nu: (be[blk(b, be, nu)], 0, 0))],
        out_specs=pl.BlockSpec((bm, d), lambda b, be, nu: (b, 0)),
        scratch_shapes=[pltpu.VMEM((d, two_de), BF16), pltpu.VMEM((de, d), BF16)],
    )
    return pl.pallas_call(
        _experts_kernel,
        out_shape=jax.ShapeDtypeStruct((n_rows, d), F32),
        grid_spec=grid_spec,
        compiler_params=_cparams(("arbitrary",)),
        name="experts",
    )(block_e, n_used, xs, w_gate_up, b_gate_up.reshape(ne, 1, two_de), w_down, b_down.reshape(ne, 1, d))


def _combine_kernel(dest_ref, y_ref, gate_ref, x1_ref, gt_ref, gpost_ref, o_ref, buf, sem):
    tm = x1_ref.shape[0]

    def row_copy(r, k):
        src = dest_ref[r * TOP_K + k]
        return pltpu.make_async_copy(y_ref.at[pl.ds(src, 1)], buf.at[k, pl.ds(r, 1)], sem)

    def start(r, carry):
        for k in range(TOP_K):
            row_copy(r, k).start(priority=k % 2)
        return carry

    def wait(r, carry):
        for k in range(TOP_K):
            row_copy(r, k).wait()
        return carry

    lax.fori_loop(0, tm, start, 0)
    lax.fori_loop(0, tm, wait, 0)

    gate = gate_ref[...]
    acc = gate[:, 0:1] * buf[0]
    for k in range(1, TOP_K):
        acc = acc + gate[:, k:k + 1] * buf[k]
    o_ref[...] = x1_ref[...] + gt_ref[...] * (_rms(acc) * gpost_ref[...])


def _combine(dest_flat, y, gate4, x1, gt, gpost):
    s, d = x1.shape
    tm = COMBINE_TILE
    return pl.pallas_call(
        _combine_kernel,
        out_shape=jax.ShapeDtypeStruct((s, d), F32),
        grid=(s // tm,),
        in_specs=[pl.BlockSpec((tm * TOP_K,), lambda i: (i,), memory_space=pltpu.SMEM),
                  pl.BlockSpec(memory_space=pl.ANY),
                  pl.BlockSpec((tm, TOP_K), lambda i: (i, 0)),
                  pl.BlockSpec((tm, d), lambda i: (i, 0)),
                  pl.BlockSpec((1, d), lambda i: (0, 0)),
                  pl.BlockSpec((1, d), lambda i: (0, 0))],
        out_specs=pl.BlockSpec((tm, d), lambda i: (i, 0)),
        scratch_shapes=[pltpu.VMEM((TOP_K, tm, d), F32), pltpu.SemaphoreType.DMA],
        compiler_params=_cparams(("arbitrary",)),
        name="combine",
    )(dest_flat, y, gate4, x1, gt, gpost)


def _layer(x2, mod, g_pre_mix, g_post_mix, w_in, b_forget, rel_bias, w_out,
           g_pre_ffn, g_post_ffn, w_router, b_router, w_gate_up, b_gate_up, w_down, b_down):
    s, d = x2.shape
    sh_m, sc_m, gt_m, sh_f, sc_f, gt_f = [mod[:, k * d:(k + 1) * d] for k in range(6)]
    n_qkv = 3 * (N_HEADS_FOX + N_HEADS_MOBA) * HEAD_DIM
    fox_w = N_HEADS_FOX * HEAD_DIM

    w_qkv = w_in[:, :n_qkv].astype(BF16)
    w_f = w_in[:, n_qkv:].T
    qkv, cum, sel, nrm, tr, nr = _inproj(x2, g_pre_mix.reshape(1, d), sc_m, sh_m, w_qkv, w_f,
                                         b_forget.reshape(-1, 1))
    y_a = _fox(qkv, tr, nr, cum, nrm)
    y_b = _moba(qkv, tr, sel, rel_bias)

    w_out_bf = w_out.astype(BF16)
    x1, h2, idx4, rank4, gate4, cnt = _outproj(
        y_a, y_b, x2, w_out_bf[:fox_w], w_out_bf[fox_w:], g_post_mix.reshape(1, d), gt_m,
        g_pre_ffn.reshape(1, d), sc_f, sh_f, w_router, b_router.reshape(1, -1))

    bm = EXPERT_BLOCK
    counts = cnt[0].astype(I32)
    pblk = (counts + bm - 1) // bm
    pend_blk = jnp.cumsum(pblk)
    pstart = ((pend_blk - pblk) * bm).astype(I32)
    n_rows = s * TOP_K + N_EXPERTS * bm
    n_blk = n_rows // bm
    block_e = jnp.minimum(jnp.sum(pend_blk[None, :] <= jnp.arange(n_blk)[:, None], axis=1),
                          N_EXPERTS - 1).astype(I32)
    n_used = pend_blk[-1:].astype(I32)

    dest_flat = _dest(idx4, rank4, pstart).reshape(-1)
    xs = _dispatch(pstart, pblk.astype(I32), n_used, dest_flat, h2, n_rows)
    y = _experts(block_e, n_used, xs, w_gate_up, b_gate_up, w_down, b_down)
    return _combine(dest_flat, y, gate4, x1, gt_f, g_post_ffn.reshape(1, d))


def kernel(x, c, w_ada, b_ada, g_pre_mix, g_post_mix, w_in, b_forget, rel_bias, w_out, g_pre_ffn, g_post_ffn, w_router, b_router, w_gate_up, b_gate_up, w_down, b_down):
    bsz, s, d = x.shape
    depth = w_ada.shape[0]
    outs = []
    for bi in range(bsz):
        x2 = x[bi]
        for l in range(depth):
            mod = _adaln(c[bi:bi + 1], w_ada[l], b_ada[l])
            x2 = _layer(x2, mod, g_pre_mix[l], g_post_mix[l], w_in[l], b_forget[l], rel_bias, w_out[l],
                        g_pre_ffn[l], g_post_ffn[l], w_router[l], b_router[l], w_gate_up[l], b_gate_up[l],
                        w_down[l], b_down[l])
        outs.append(x2)
    return jnp.stack(outs)
```

```python
import functools
import math

import numpy as np
import jax
import jax.numpy as jnp
from jax import lax
from jax.experimental import pallas as pl
from jax.experimental.pallas import tpu as pltpu

F32 = jnp.float32
BF16 = jnp.bfloat16
I32 = jnp.int32

HEAD_DIM = 64
N_HEADS_FOX = 8
N_HEADS_MOBA = 8
PAIR = 2 * HEAD_DIM
MOBA_BLOCK = 256
MOBA_TOPK = 3
NUM_BUCKETS = 32
MAX_DISTANCE = 128
N_EXPERTS = 32
TOP_K = 4
SWIGLU_LIMIT = 7.0
SWIGLU_ALPHA = 1.702
RMS_EPS = 1e-6
NEG = -(2.0 ** 100)
M_INIT = -(2.0 ** 99)
LOG2E = math.log2(math.e)
SUM_ROWS = 16
SUBLANES = 8
LANES = 128
EXP_UNDERFLOW = 90.0
VMEM_LIMIT = 56 * 1024 * 1024

ROW_TILE = 512
FOX_TILE = 256
EXPERT_BLOCK = 256
COMBINE_TILE = 256

NT_DIMS = (((1,), (1,)), ((), ()))


def _cparams(sem):
    return pltpu.CompilerParams(dimension_semantics=sem, vmem_limit_bytes=VMEM_LIMIT)


def _rms(x):
    return x * lax.rsqrt(jnp.mean(x * x, axis=-1, keepdims=True) + RMS_EPS)


def _adaln_kernel(c_ref, w_ref, b_ref, o_ref):
    c = c_ref[...]
    cond = c * jax.nn.sigmoid(c)
    o_ref[...] = jnp.dot(cond, w_ref[...], preferred_element_type=F32,
                         precision=lax.Precision.HIGHEST) + b_ref[...]


def _adaln(c, w_ada, b_ada):
    d = c.shape[-1]
    n = w_ada.shape[-1]
    c8 = jnp.broadcast_to(c.reshape(1, d), (8, d))
    out = pl.pallas_call(
        _adaln_kernel,
        out_shape=jax.ShapeDtypeStruct((8, n), F32),
        grid=(n // d,),
        in_specs=[pl.BlockSpec((8, d), lambda j: (0, 0)),
                  pl.BlockSpec((d, d), lambda j: (0, j)),
                  pl.BlockSpec((1, d), lambda j: (0, j))],
        out_specs=pl.BlockSpec((8, d), lambda j: (0, j)),
        compiler_params=_cparams(("arbitrary",)),
        name="adaln",
    )(c8, w_ada, b_ada.reshape(1, n))
    return out[0:1]


def _inproj_kernel(x_ref, g_ref, sc_ref, sh_ref, w_ref, wf_ref, bf_ref,
                   qkv_ref, cum_ref, sel_ref, nrm_ref, tr_ref, nr_ref, km_scr, carry_scr):
    i = pl.program_id(0)
    tm = x_ref.shape[0]
    nblk = km_scr.shape[0]

    @pl.when(i == 0)
    def _():
        km_scr[...] = jnp.zeros_like(km_scr)
        carry_scr[...] = jnp.zeros_like(carry_scr)

    x = x_ref[...]
    h = _rms(x) * g_ref[...] * (1.0 + sc_ref[...]) + sh_ref[...]
    hb = h.astype(BF16)

    width = N_HEADS_FOX * HEAD_DIM
    hsel = jnp.where(lax.broadcasted_iota(I32, (width, N_HEADS_FOX), 0) // HEAD_DIM
                     == lax.broadcasted_iota(I32, (width, N_HEADS_FOX), 1), 1.0, 0.0)
    qb = kb = None
    tr_slot = {3: 0, 5: 1, 0: 2, 2: 3}
    for c in range(6):
        pc = jnp.dot(hb, w_ref[:, c * width:(c + 1) * width], preferred_element_type=F32)
        if c == 0 or c == 3:
            pc_t = pc * (LOG2E * HEAD_DIM ** -0.5)
        if c in tr_slot:
            n = tr_slot[c]
            tr_ref[n * width:(n + 1) * width, :] = (pc_t if c in (0, 3) else pc).T.astype(BF16)
        if c == 1 or c == 4:
            qkv_ref[:, (c // 3) * width:(c // 3 + 1) * width] = pc.astype(BF16)
        if c < 2:
            sq = (pc * (HEAD_DIM ** -0.5) if c == 0 else pc).astype(BF16).astype(F32)
            n2 = jnp.dot((sq * sq).astype(BF16), hsel.astype(BF16), preferred_element_type=F32)
            nrm_ref[0, c:c + 1, :] = jnp.max(n2, axis=0, keepdims=True)
        if c == 3:
            qb = pc
        if c == 4:
            kb = pc

    ft = lax.dot_general(wf_ref[...], h, NT_DIMS, preferred_element_type=F32,
                         precision=lax.Precision.HIGHEST)
    z = ft + bf_ref[...]
    logf = -(jnp.maximum(-z, 0.0) + jnp.log1p(jnp.exp(-jnp.abs(z))))
    lane = lax.broadcasted_iota(I32, logf.shape, 1)
    cs = logf
    sh = 1
    while sh < tm:
        cs = cs + jnp.where(lane >= sh, pltpu.roll(cs, sh, axis=1), 0.0)
        sh *= 2
    base = jnp.zeros_like(cs)
    for b in range(1, tm // FOX_TILE):
        base = jnp.where(lane >= b * FOX_TILE, cs[:, b * FOX_TILE - 1:b * FOX_TILE], base)
    nr_ref[...] = (-LOG2E * (cs - base)).T
    cs = cs + carry_scr[...]
    cum_ref[...] = cs
    carry_scr[...] = cs[:, tm - 1:tm]

    nb_tile = tm // MOBA_BLOCK
    for b in range(nb_tile):
        kmean = jnp.sum(kb[b * MOBA_BLOCK:(b + 1) * MOBA_BLOCK], axis=0, keepdims=True) * (1.0 / MOBA_BLOCK)
        km_scr[pl.ds(i * nb_tile + b, 1), :] = kmean

    km = km_scr[...]
    lane_h = lax.broadcasted_iota(I32, km.shape, 1) // HEAD_DIM
    blk = lax.broadcasted_iota(I32, (nblk, tm), 0)
    col = lax.broadcasted_iota(I32, (nblk, tm), 1)
    own = i * nb_tile + col // MOBA_BLOCK
    for hd in range(N_HEADS_MOBA):
        kmh = jnp.where(lane_h == hd, km, 0.0)
        g = lax.dot_general(kmh, qb, NT_DIMS, preferred_element_type=F32,
                            precision=lax.Precision.HIGHEST)
        g = jnp.where(blk < own, g, -jnp.inf)
        sel = jnp.zeros(g.shape, dtype=jnp.bool_)
        for _ in range(MOBA_TOPK):
            m = jnp.max(g, axis=0, keepdims=True)
            first = jnp.min(jnp.where(g == m, blk, nblk), axis=0, keepdims=True)
            pick = (blk == first) & (m > -jnp.inf)
            sel = sel | pick
            g = jnp.where(pick, -jnp.inf, g)
        sel_ref[hd] = jnp.where(sel, 0.0, NEG)


def _inproj(x2, g, sc, sh, w_qkv, w_f, b_f):
    s, d = x2.shape
    tm = ROW_TILE
    nblk = s // MOBA_BLOCK
    n = w_qkv.shape[1]
    width = N_HEADS_FOX * HEAD_DIM
    n_rows_out = 2 * width
    n_tr = 4 * width
    assert tm % FOX_TILE == 0 and N_HEADS_FOX == N_HEADS_MOBA
    return pl.pallas_call(
        _inproj_kernel,
        out_shape=(jax.ShapeDtypeStruct((s, n_rows_out), BF16),
                   jax.ShapeDtypeStruct((N_HEADS_FOX, s), F32),
                   jax.ShapeDtypeStruct((N_HEADS_MOBA, nblk, s), F32),
                   jax.ShapeDtypeStruct((s // tm, 2, N_HEADS_FOX), F32),
                   jax.ShapeDtypeStruct((n_tr, s), BF16),
                   jax.ShapeDtypeStruct((s, N_HEADS_FOX), F32)),
        grid=(s // tm,),
        in_specs=[pl.BlockSpec((tm, d), lambda i: (i, 0)),
                  pl.BlockSpec((1, d), lambda i: (0, 0)),
                  pl.BlockSpec((1, d), lambda i: (0, 0)),
                  pl.BlockSpec((1, d), lambda i: (0, 0)),
                  pl.BlockSpec((d, n), lambda i: (0, 0)),
                  pl.BlockSpec((N_HEADS_FOX, d), lambda i: (0, 0)),
                  pl.BlockSpec((N_HEADS_FOX, 1), lambda i: (0, 0))],
        out_specs=(pl.BlockSpec((tm, n_rows_out), lambda i: (i, 0)),
                   pl.BlockSpec((N_HEADS_FOX, tm), lambda i: (0, i)),
                   pl.BlockSpec((N_HEADS_MOBA, nblk, tm), lambda i: (0, 0, i)),
                   pl.BlockSpec((1, 2, N_HEADS_FOX), lambda i: (i, 0, 0)),
                   pl.BlockSpec((n_tr, tm), lambda i: (0, i)),
                   pl.BlockSpec((tm, N_HEADS_FOX), lambda i: (i, 0))),
        scratch_shapes=[pltpu.VMEM((nblk, N_HEADS_MOBA * HEAD_DIM), F32),
                        pltpu.VMEM((N_HEADS_FOX, 1), F32)],
        compiler_params=_cparams(("arbitrary",)),
        name="inproj",
    )(x2, g, sc, sh, w_qkv, w_f, b_f)


def _fox_kernel(jlo_ref, cp_ref, qt_ref, k_ref, vt_ref, nr_ref, o_ref, m_ref, a_ref):
    p = pl.program_id(0)
    i = pl.program_id(1)
    j_first = jlo_ref[p * pl.num_programs(1) + i]
    t = qt_ref.shape[1]

    qt = qt_ref[...]
    top = lax.broadcasted_iota(I32, qt.shape, 0) < HEAD_DIM
    zq = jnp.zeros_like(qt)
    qth = (jnp.where(top, qt, zq), jnp.where(top, zq, qt))
    m_ref[...] = jnp.full(m_ref.shape, M_INIT, F32)
    a_ref[...] = jnp.zeros(a_ref.shape, F32)
    er = lax.broadcasted_iota(I32, (SUM_ROWS, 2 * t), 0)
    ec = lax.broadcasted_iota(I32, (SUM_ROWS, 2 * t), 1)
    ones_rows = jnp.where(((er == 0) & (ec < t)) | ((er == 1) & (ec >= t)), 1.0, 0.0).astype(BF16)
    arow = lax.broadcasted_iota(I32, a_ref.shape, 0)
    head0_rows = (arow < HEAD_DIM) | (arow == PAIR)

    def tile(j, diagonal):
        k0 = pl.multiple_of(j * t, t)
        kt = k_ref[pl.ds(k0, t), :]
        vt = vt_ref[:, pl.ds(k0, t)]
        ps, alphas = [], []
        for hd in range(2):
            h = 2 * p + hd
            s = jnp.dot(kt, qth[hd], preferred_element_type=F32) + nr_ref[0, pl.ds(k0, t), hd:hd + 1]
            if diagonal:
                r = lax.broadcasted_iota(I32, s.shape, 0)
                c = lax.broadcasted_iota(I32, s.shape, 1)
                s = jnp.where(r <= c, s, NEG)
            pair_shift = LOG2E * (cp_ref[h, i] - cp_ref[h, j])
            m_old = m_ref[hd]
            m_new = jnp.maximum(m_old, jnp.max(s, axis=0, keepdims=True) + pair_shift)
            alphas.append(jnp.exp2(m_old - m_new))
            m_ref[hd] = m_new
            ps.append(jnp.exp2(s - (m_new - pair_shift)).astype(BF16))
        zv = jnp.zeros_like(vt)
        vcat = jnp.concatenate([jnp.where(top, vt, zv), jnp.where(top, zv, vt)], axis=1)
        vcat = jnp.concatenate([vcat, ones_rows], axis=0)
        pcat = jnp.concatenate(ps, axis=0)
        a_ref[...] = (a_ref[...] * jnp.where(head0_rows, alphas[0], alphas[1])
                      + jnp.dot(vcat, pcat, preferred_element_type=F32))

    def body(j, carry):
        tile(j, False)
        return carry

    lax.fori_loop(j_first, i, body, 0)
    tile(i, True)
    out_t = a_ref[0:PAIR, :] / jnp.where(top, a_ref[PAIR:PAIR + 1, :], a_ref[PAIR + 1:PAIR + 2, :])
    o_ref[...] = out_t.T.astype(o_ref.dtype)


def _fox_first_tile(cum, nrm, t):
    cend = cum[:, t - 1::t]
    nt = cend.shape[1]
    cprev = jnp.concatenate([jnp.zeros((cend.shape[0], 1), F32), cend[:, :-1]], axis=1)
    rep = nt // nrm.shape[0]
    qn = jnp.repeat(jnp.sqrt(nrm[:, 0, :]).T, rep, axis=1)
    kn = jnp.repeat(jnp.sqrt(nrm[:, 1, :]).T, rep, axis=1)
    gap = (1.02 * qn[:, :, None] * (kn[:, None, :] + kn[:, :, None])
           + cprev[:, :, None] - cend[:, None, :])
    jj = jnp.arange(nt)[None, None, :]
    ii = jnp.arange(nt)[None, :, None]
    needed = (jj < ii) & jnp.logical_not(gap <= -EXP_UNDERFLOW)
    needed = needed[0::2] | needed[1::2]
    first = jnp.min(jnp.where(needed, jj, ii), axis=2)
    return first.reshape(-1).astype(I32)


def _fox(qkv, tr, nr, cum, nrm):
    s = qkv.shape[0]
    t = FOX_TILE
    npair = N_HEADS_FOX // 2
    jlo = _fox_first_tile(cum, nrm, t)
    cend = cum[:, t - 1::t]
    cprev = jnp.concatenate([jnp.zeros((cend.shape[0], 1), F32), cend[:, :-1]], axis=1)
    q_rows = 2 * (N_HEADS_MOBA // 2)
    grid_spec = pltpu.PrefetchScalarGridSpec(
        num_scalar_prefetch=1,
        grid=(npair, s // t),
        in_specs=[pl.BlockSpec(memory_space=pltpu.SMEM),
                  pl.BlockSpec((PAIR, t), lambda p, i, jl: (q_rows + p, i)),
                  pl.BlockSpec((s, PAIR), lambda p, i, jl: (0, p)),
                  pl.BlockSpec((PAIR, s), lambda p, i, jl: (q_rows + npair + p, 0)),
                  pl.BlockSpec((1, s, 2), lambda p, i, jl: (p, 0, 0))],
        out_specs=pl.BlockSpec((t, PAIR), lambda p, i, jl: (i, p)),
        scratch_shapes=[pltpu.VMEM((2, 1, t), F32), pltpu.VMEM((PAIR + SUM_ROWS, t), F32)],
    )
    nr_pairs = nr.reshape(s, npair, 2).transpose(1, 0, 2)
    return pl.pallas_call(
        _fox_kernel,
        out_shape=jax.ShapeDtypeStruct((s, npair * PAIR), BF16),
        grid_spec=grid_spec,
        compiler_params=_cparams(("arbitrary", "arbitrary")),
        name="fox",
    )(jlo, cprev, tr, qkv, tr, nr_pairs)


def _t5_bucket_np(dist):
    dist = np.maximum(dist, 0)
    max_exact = NUM_BUCKETS // 2
    d = np.maximum(dist, 1).astype(np.float32)
    large = max_exact + (np.log(d / np.float32(max_exact)) / np.float32(math.log(MAX_DISTANCE / max_exact))
                         * np.float32(NUM_BUCKETS - max_exact)).astype(np.int32)
    large = np.minimum(large, NUM_BUCKETS - 1)
    return np.where(dist < max_exact, dist, large).astype(np.int32)


def _moba_kernel(rb_ref, qt_ref, k_ref, vt_ref, sel_ref, bkt_ref, o_ref,
                 m_ref, a_ref, bias_scr, sa_ref, sb_ref, pa_ref, pb_ref, ala_ref, alb_ref):
    p = pl.program_id(0)
    g = pl.program_id(1)
    tq = qt_ref.shape[1]
    t = tq // 2

    @pl.when(g == 0)
    def _():
        r = lax.broadcasted_iota(I32, (t, t), 0)
        c = lax.broadcasted_iota(I32, (t, t), 1)
        zero = jnp.zeros((t, t), F32)
        for hd in range(2):
            h = 2 * p + hd
            far = rb_ref[(NUM_BUCKETS - 1) * N_HEADS_MOBA + h]
            tiles = []
            for w in range(2):
                bkt = bkt_ref[w]
                acc = jnp.zeros(bkt.shape, F32)
                for kk in range(NUM_BUCKETS):
                    acc = acc + jnp.where(bkt == kk, rb_ref[kk * N_HEADS_MOBA + h], 0.0)
                tiles.append((acc - far) * LOG2E)
            prev_t = tiles[0]
            own_t = jnp.where(r <= c, tiles[1], NEG)
            bias_scr[hd, 0] = jnp.concatenate([prev_t, zero], axis=1)
            bias_scr[hd, 1] = jnp.concatenate([own_t, prev_t], axis=1)
            bias_scr[hd, 2] = jnp.concatenate([zero, own_t], axis=1)

    qt = qt_ref[...]
    top = lax.broadcasted_iota(I32, qt.shape, 0) < HEAD_DIM
    zq = jnp.zeros_like(qt)
    qth = (jnp.where(top, qt, zq), jnp.where(top, zq, qt))
    m_ref[...] = jnp.full(m_ref.shape, M_INIT, F32)
    a_ref[...] = jnp.zeros(a_ref.shape, F32)

    er = lax.broadcasted_iota(I32, (SUM_ROWS, 2 * t), 0)
    ec = lax.broadcasted_iota(I32, (SUM_ROWS, 2 * t), 1)
    ones_rows = jnp.where(((er == 0) & (ec < t)) | ((er == 1) & (ec >= t)), 1.0, 0.0).astype(BF16)
    arow = lax.broadcasted_iota(I32, a_ref.shape, 0)
    head0_rows = (arow < HEAD_DIM) | (arow == PAIR)
    vtop = lax.broadcasted_iota(I32, (PAIR, t), 0) < HEAD_DIM
    in_a = lax.broadcasted_iota(I32, (1, tq), 1) < t

    n_far = jnp.maximum(2 * g - 1, 0)

    def produce(j, s_buf):
        k0 = pl.multiple_of(j * t, t)
        kt = k_ref[pl.ds(k0, t), :]
        for hd in range(2):
            s_buf[hd] = jnp.dot(kt, qth[hd], preferred_element_type=F32)

    def softmax(s_buf, p_buf, al_buf, selrows, w):
        for hd in range(2):
            s = s_buf[hd]
            if w is not None:
                s = s + bias_scr[hd, w]
            smax = jnp.max(s, axis=0, keepdims=True)
            m_old = m_ref[hd]
            m_new = jnp.maximum(m_old, smax + selrows[hd])
            shift = m_new - selrows[hd]
            al_buf[hd] = jnp.exp2(m_old - m_new)
            m_ref[hd] = m_new
            p_buf[hd * t:(hd + 1) * t, :] = jnp.exp2(s - shift).astype(BF16)

    def far_rows(j):
        return [jnp.where(j < n_far, sel_ref[hd, pl.ds(j, 1), :], NEG) for hd in range(2)]

    def accumulate(j, p_buf, al_buf):
        k0 = pl.multiple_of(j * t, t)
        vt = vt_ref[:, pl.ds(k0, t)]
        zv = jnp.zeros_like(vt)
        vcat = jnp.concatenate([jnp.where(vtop, vt, zv), jnp.where(vtop, zv, vt)], axis=1)
        vcat = jnp.concatenate([vcat, ones_rows], axis=0)
        a_ref[...] = (a_ref[...] * jnp.where(head0_rows, al_buf[0], al_buf[1])
                      + jnp.dot(vcat, p_buf[...], preferred_element_type=F32))

    pb_ref[...] = jnp.zeros_like(pb_ref)
    alb_ref[...] = jnp.ones_like(alb_ref)
    produce(0, sa_ref)

    def body(u, carry):
        j0 = 2 * u
        produce(j0 + 1, sb_ref)
        softmax(sa_ref, pa_ref, ala_ref, far_rows(j0), None)
        accumulate(jnp.maximum(j0 - 1, 0), pb_ref, alb_ref)
        produce(j0 + 2, sa_ref)
        accumulate(j0, pa_ref, ala_ref)
        softmax(sb_ref, pb_ref, alb_ref, far_rows(j0 + 1), None)
        return carry

    n_pairs = (n_far + 1) // 2
    lax.fori_loop(0, n_pairs, body, 0)

    j1 = jnp.maximum(2 * g - 1, 0)
    j2 = 2 * g
    j3 = 2 * g + 1
    rows1 = [jnp.where(g >= 1, sel_ref[hd, pl.ds(j1, 1), :], NEG) for hd in range(2)]
    rows2 = [jnp.where(in_a, 0.0, sel_ref[hd, pl.ds(j2, 1), :]) for hd in range(2)]
    rows3 = [jnp.where(in_a, NEG, 0.0)] * 2
    produce(j1, sa_ref)
    produce(j2, sb_ref)
    softmax(sa_ref, pa_ref, ala_ref, rows1, 0)
    accumulate(jnp.maximum(2 * n_pairs - 1, 0), pb_ref, alb_ref)
    produce(j3, sa_ref)
    softmax(sb_ref, pb_ref, alb_ref, rows2, 1)
    accumulate(j1, pa_ref, ala_ref)
    softmax(sa_ref, pa_ref, ala_ref, rows3, 2)
    accumulate(j2, pb_ref, alb_ref)
    accumulate(j3, pa_ref, ala_ref)
    out_t = a_ref[0:PAIR, :] / jnp.where(top, a_ref[PAIR:PAIR + 1, :], a_ref[PAIR + 1:PAIR + 2, :])
    o_ref[...] = out_t.T.astype(o_ref.dtype)


def _moba(qkv, tr, sel, rel_bias):
    s = qkv.shape[0]
    t = MOBA_BLOCK
    npair = N_HEADS_MOBA // 2
    nblk = s // t
    kcol = N_HEADS_FOX // 2
    a = np.arange(t)[None, :]
    b = np.arange(t)[:, None]
    bkt = jnp.asarray(np.stack([_t5_bucket_np(t + a - b), _t5_bucket_np(a - b)]))
    tq = 2 * t
    grid_spec = pltpu.PrefetchScalarGridSpec(
        num_scalar_prefetch=1,
        grid=(npair, s // tq),
        in_specs=[pl.BlockSpec((PAIR, tq), lambda p, i, rb: (p, i)),
                  pl.BlockSpec((s, PAIR), lambda p, i, rb: (0, kcol + p)),
                  pl.BlockSpec((PAIR, s), lambda p, i, rb: (npair + p, 0)),
                  pl.BlockSpec((2, nblk, tq), lambda p, i, rb: (p, 0, i)),
                  pl.BlockSpec((2, t, t), lambda p, i, rb: (0, 0, 0))],
        out_specs=pl.BlockSpec((tq, PAIR), lambda p, i, rb: (i, p)),
        scratch_shapes=[pltpu.VMEM((2, 1, tq), F32),
                        pltpu.VMEM((PAIR + SUM_ROWS, tq), F32), pltpu.VMEM((2, 3, t, tq), F32),
                        pltpu.VMEM((2, t, tq), F32), pltpu.VMEM((2, t, tq), F32),
                        pltpu.VMEM((2 * t, tq), BF16), pltpu.VMEM((2 * t, tq), BF16),
                        pltpu.VMEM((2, 1, tq), F32), pltpu.VMEM((2, 1, tq), F32)],
    )
    return pl.pallas_call(
        _moba_kernel,
        out_shape=jax.ShapeDtypeStruct((s, npair * PAIR), BF16),
        grid_spec=grid_spec,
        compiler_params=_cparams(("arbitrary", "arbitrary")),
        name="moba",
    )(rel_bias.reshape(-1), tr, qkv, tr, sel, bkt)


def _store_token_tiles(ref, val):
    n = val.shape[0]
    for c in range(SUBLANES):
        ref[pl.ds(c, n, stride=SUBLANES), :] = val[:, c * LANES:(c + 1) * LANES]


def _load_token_tiles(ref, n):
    return jnp.concatenate([ref[pl.ds(c, n, stride=SUBLANES), :] for c in range(SUBLANES)], axis=1)


def _tile_rows(r0, n):
    start = r0 * SUBLANES
    if not isinstance(start, int):
        start = pl.multiple_of(start, SUBLANES)
    return pl.ds(start, n * SUBLANES)


def _outproj_kernel(ya_ref, yb_ref, x_ref, wa_ref, wb_ref, gpost_ref, gt_ref, gpre_ref,
                    sc_ref, sh_ref, wr_ref, br_ref,
                    x1_ref, h2_ref, idx_ref, rank_ref, gate_ref, cnt_ref, carry_scr):
    i = pl.program_id(0)
    tm = x_ref.shape[0]

    @pl.when(i == 0)
    def _():
        carry_scr[...] = jnp.zeros_like(carry_scr)

    y = (jnp.dot(ya_ref[...], wa_ref[...], preferred_element_type=F32)
         + jnp.dot(yb_ref[...], wb_ref[...], preferred_element_type=F32))
    x1 = x_ref[...] + gt_ref[...] * (_rms(y) * gpost_ref[...])
    x1_ref[...] = x1
    h2 = _rms(x1) * gpre_ref[...] * (1.0 + sc_ref[...]) + sh_ref[...]
    _store_token_tiles(h2_ref, h2)

    logits =jnp.dot(h2, wr_ref[...], preferred_element_type=F32,
                     precision=lax.Precision.HIGHEST) + br_ref[...]
    ne = logits.shape[1]
    lane = lax.broadcasted_iota(I32, logits.shape, 1)
    lane4 = lax.broadcasted_iota(I32, (tm, TOP_K), 1)
    g = logits
    mask = jnp.zeros(logits.shape, F32)
    vals, picks = [], []
    for _ in range(TOP_K):
        m = jnp.max(g, axis=1, keepdims=True)
        first = jnp.min(jnp.where(g == m, lane, ne), axis=1, keepdims=True)
        pick = lane == first
        mask = jnp.where(pick, 1.0, mask)
        g = jnp.where(pick, -jnp.inf, g)
        vals.append(m)
        picks.append(pick)
    ex = [jnp.exp(v - vals[0]) for v in vals]
    den = ex[0] + ex[1] + ex[2] + ex[3]
    gates = [e / den for e in ex]

    r = lax.broadcasted_iota(I32, (tm, tm), 0)
    c = lax.broadcasted_iota(I32, (tm, tm), 1)
    tril = jnp.where(c < r, 1.0, 0.0).astype(BF16)
    before = jnp.dot(tril, mask.astype(BF16), preferred_element_type=F32) + carry_scr[...]
    total = carry_scr[...] + jnp.sum(mask, axis=0, keepdims=True)
    carry_scr[...] = total
    cnt_ref[...] = jnp.broadcast_to(total, cnt_ref.shape)

    def pack4(cols):
        return jnp.where(lane4 == 0, cols[0],
                         jnp.where(lane4 == 1, cols[1], jnp.where(lane4 == 2, cols[2], cols[3])))

    ranks = [jnp.sum(jnp.where(pk, before, 0.0), axis=1, keepdims=True) for pk in picks]
    idxs = [jnp.sum(jnp.where(pk, lane, 0), axis=1, keepdims=True) for pk in picks]
    idx_ref[...] = pack4(idxs)
    rank_ref[...] = pack4(ranks).astype(I32)
    gate_ref[...] = pack4(gates)


def _outproj(mix_a, mix_b, x2, w_a, w_b, gpost, gt, gpre, sc, sh, w_router, b_router):
    s, d = x2.shape
    tm = ROW_TILE
    ne = w_router.shape[1]
    wa = mix_a.shape[1]
    row = lambda i: (i, 0)
    fix = lambda i: (0, 0)
    vec = pl.BlockSpec((1, d), fix)
    return pl.pallas_call(
        _outproj_kernel,
        out_shape=(jax.ShapeDtypeStruct((s, d), F32),
                   jax.ShapeDtypeStruct((s * SUBLANES, LANES), F32),
                   jax.ShapeDtypeStruct((s, TOP_K), I32),
                   jax.ShapeDtypeStruct((s, TOP_K), I32),
                   jax.ShapeDtypeStruct((s, TOP_K), F32),
                   jax.ShapeDtypeStruct((8, ne), F32)),
        grid=(s // tm,),
        in_specs=[pl.BlockSpec((tm, wa), row), pl.BlockSpec((tm, wa), row), pl.BlockSpec((tm, d), row),
                  pl.BlockSpec((wa, d), fix), pl.BlockSpec((wa, d), fix),
                  vec, vec, vec, vec, vec,
                  pl.BlockSpec((d, ne), fix), pl.BlockSpec((1, ne), fix)],
        out_specs=(pl.BlockSpec((tm, d), row), pl.BlockSpec((tm * SUBLANES, LANES), row),
                   pl.BlockSpec((tm, TOP_K), row), pl.BlockSpec((tm, TOP_K), row),
                   pl.BlockSpec((tm, TOP_K), row), pl.BlockSpec((8, ne), fix)),
        scratch_shapes=[pltpu.VMEM((1, ne), F32)],
        compiler_params=_cparams(("arbitrary",)),
        name="outproj",
    )(mix_a, mix_b, x2, w_a, w_b, gpost, gt, gpre, sc, sh, w_router, b_router)


def _dest_kernel(idx_ref, rank_ref, pstart_ref, o_ref):
    idx = idx_ref[...]
    tm = idx.shape[0]
    ne = pstart_ref.shape[1]
    lane = lax.broadcasted_iota(I32, (tm, ne), 1)
    lane4 = lax.broadcasted_iota(I32, idx.shape, 1)
    out = rank_ref[...]
    for k in range(TOP_K):
        start = jnp.sum(jnp.where(lane == idx[:, k:k + 1], pstart_ref[...], 0), axis=1, keepdims=True)
        out = out + jnp.where(lane4 == k, start, 0)
    o_ref[...] = out


def _dest(idx4, rank4, pstart):
    s = idx4.shape[0]
    tm = min(4 * ROW_TILE, s)
    row = lambda i: (i, 0)
    return pl.pallas_call(
        _dest_kernel,
        out_shape=jax.ShapeDtypeStruct((s, TOP_K), I32),
        grid=(s // tm,),
        in_specs=[pl.BlockSpec((tm, TOP_K), row), pl.BlockSpec((tm, TOP_K), row),
                  pl.BlockSpec((1, N_EXPERTS), lambda i: (0, 0))],
        out_specs=pl.BlockSpec((tm, TOP_K), row),
        compiler_params=_cparams(("arbitrary",)),
        name="dest",
    )(idx4, rank4, pstart.reshape(1, -1))


def _dispatch_kernel(pstart_ref, pblk_ref, nu_ref, dest_ref, h_ref, xs_ref,
                     zero_scr, sem, zsem):
    tm = h_ref.shape[0] // SUBLANES
    bm = zero_scr.shape[0] // SUBLANES
    n_blk = xs_ref.shape[0] // (bm * SUBLANES)

    rows = _tile_rows

    @pl.when(pl.program_id(0) == 0)
    def _():
        zero_scr[...] = jnp.zeros_like(zero_scr)

        def zero_copy(row0):
            return pltpu.make_async_copy(zero_scr, xs_ref.at[rows(row0, bm)], zsem)

        for phase in range(2):
            for e in range(N_EXPERTS):
                last = pstart_ref[e] + (pblk_ref[e] - 1) * bm
                tail = (n_blk - N_EXPERTS + e) * bm
                for cond, row0 in ((pblk_ref[e] > 0, last), (n_blk - N_EXPERTS + e >= nu_ref[0], tail)):
                    @pl.when(cond)
                    def _():
                        if phase == 0:
                            zero_copy(row0).start()
                        else:
                            zero_copy(row0).wait()

    def row_copy(r, k):
        dst = dest_ref[r * TOP_K + k]
        return pltpu.make_async_copy(h_ref.at[rows(r, 1)], xs_ref.at[rows(dst, 1)], sem)

    def start(r, carry):
        for k in range(TOP_K):
            row_copy(r, k).start(priority=k % 2)
        return carry

    def wait(r, carry):
        for k in range(TOP_K):
            row_copy(r, k).wait()
        return carry

    lax.fori_loop(0, tm, start, 0)
    lax.fori_loop(0, tm, wait, 0)


def _dispatch(pstart, pblk, n_used, dest_flat, h2, n_rows):
    s = h2.shape[0] // SUBLANES
    tm = COMBINE_TILE
    grid_spec = pltpu.PrefetchScalarGridSpec(
        num_scalar_prefetch=3,
        grid=(s // tm,),
        in_specs=[pl.BlockSpec((tm * TOP_K,), lambda i, *_: (i,), memory_space=pltpu.SMEM),
                  pl.BlockSpec((tm * SUBLANES, LANES), lambda i, *_: (i, 0))],
        out_specs=pl.BlockSpec(memory_space=pl.ANY),
        scratch_shapes=[pltpu.VMEM((EXPERT_BLOCK * SUBLANES, LANES), F32), pltpu.SemaphoreType.DMA,
                        pltpu.SemaphoreType.DMA],
    )
    return pl.pallas_call(
        _dispatch_kernel,
        out_shape=jax.ShapeDtypeStruct((n_rows * SUBLANES, LANES), F32),
        grid_spec=grid_spec,
        compiler_params=_cparams(("arbitrary",)),
        name="dispatch",
    )(pstart, pblk, n_used, dest_flat, h2)


def _experts_kernel(be_ref, nu_ref, xs_ref, wgu_ref, bgu_ref, wd_ref, bd_ref, y_ref,
                    wgu_bf, wd_bf):
    b = pl.program_id(0)
    d_exp = wd_ref.shape[1]
    prev = be_ref[jnp.maximum(b - 1, 0)]
    changed = (b == 0) | (be_ref[b] != prev)

    @pl.when((b < nu_ref[0]) & changed)
    def _():
        rows = 128

        def cast_gu(c, carry):
            r0 = pl.multiple_of(c * rows, rows)
            wgu_bf[pl.ds(r0, rows), :] = wgu_ref[0, pl.ds(r0, rows), :].astype(BF16)
            return carry

        def cast_d(c, carry):
            r0 = pl.multiple_of(c * rows, rows)
            wd_bf[pl.ds(r0, rows), :] = wd_ref[0, pl.ds(r0, rows), :].astype(BF16)
            return carry

        lax.fori_loop(0, wgu_ref.shape[1] // rows, cast_gu, 0)
        lax.fori_loop(0, wd_ref.shape[1] // rows, cast_d, 0)

    @pl.when(b < nu_ref[0])
    def _():
        bm = xs_ref.shape[0] // SUBLANES
        xb = _load_token_tiles(xs_ref, bm).astype(BF16)
        hdn = jnp.dot(xb, wgu_bf[...], preferred_element_type=F32) + bgu_ref[0]
        x_glu = jnp.minimum(hdn[:, :d_exp], SWIGLU_LIMIT)
        x_lin = jnp.clip(hdn[:, d_exp:], -SWIGLU_LIMIT, SWIGLU_LIMIT)
        act = x_glu * jax.nn.sigmoid(SWIGLU_ALPHA * x_glu) * (x_lin + 1.0)
        _store_token_tiles(y_ref, jnp.dot(act.astype(BF16), wd_bf[...], preferred_element_type=F32)
                           + bd_ref[0])

    @pl.when(b >= nu_ref[0])
    def _():
        y_ref[...] = jnp.zeros_like(y_ref)


def _experts(block_e, n_used, xs, w_gate_up, b_gate_up, w_down, b_down):
    n_rows = xs.shape[0] // SUBLANES
    bm = EXPERT_BLOCK
    n_blk = n_rows // bm
    ne, d, two_de = w_gate_up.shape
    de = w_down.shape[1]
    assert d == SUBLANES * LANES

    def blk(b, be, nu):
        return jnp.minimum(b, nu[0] - 1)

    grid_spec = pltpu.PrefetchScalarGridSpec(
        num_scalar_prefetch=2,
        grid=(n_blk,),
        in_specs=[pl.BlockSpec((bm * SUBLANES, LANES), lambda b, be, nu: (blk(b, be, nu), 0)),
                  pl.BlockSpec((1, d, two_de), lambda b, be, nu: (be[blk(b, be, nu)], 0, 0)),
                  pl.BlockSpec((1, 1, two_de), lambda b, be, nu: (be[blk(b, be, nu)], 0, 0)),
                  pl.BlockSpec((1, de, d), lambda b, be, nu: (be[blk(b, be, nu)], 0, 0)),
                  pl.BlockSpec((1, 1, d), lambda b, be, nu: (be[blk(b, be, nu)], 0, 0))],
        out_specs=pl.BlockSpec((bm * SUBLANES, LANES), lambda b, be, nu: (b, 0)),
        scratch_shapes=[pltpu.VMEM((d, two_de), BF16), pltpu.VMEM((de, d), BF16)],
    )
    return pl.pallas_call(
        _experts_kernel,
        out_shape=jax.ShapeDtypeStruct((n_rows * SUBLANES, LANES), F32),
        grid_spec=grid_spec,
        compiler_params=_cparams(("arbitrary",)),
        name="experts",
    )(block_e, n_used, xs, w_gate_up, b_gate_up.reshape(ne, 1, two_de), w_down, b_down.reshape(ne, 1, d))


def _combine_kernel(dest_ref, y_ref, gate_ref, x1_ref, gt_ref, gpost_ref, o_ref, buf, sem):
    tm = x1_ref.shape[0]

    def row_copy(r, k):
        src = dest_ref[r * TOP_K + k]
        return pltpu.make_async_copy(y_ref.at[_tile_rows(src, 1)], buf.at[k, _tile_rows(r, 1)], sem)

    def start(r, carry):
        for k in range(TOP_K):
            row_copy(r, k).start(priority=k % 2)
        return carry

    def wait(r, carry):
        for k in range(TOP_K):
            row_copy(r, k).wait()
        return carry

    lax.fori_loop(0, tm, start, 0)
    lax.fori_loop(0, tm, wait, 0)

    gate = gate_ref[...]
    acc = gate[:, 0:1] * _load_token_tiles(buf.at[0], tm)
    for k in range(1, TOP_K):
        acc = acc + gate[:, k:k + 1] * _load_token_tiles(buf.at[k], tm)
    o_ref[...] = x1_ref[...] + gt_ref[...] * (_rms(acc) * gpost_ref[...])


def _combine(dest_flat, y, gate4, x1, gt, gpost):
    s, d = x1.shape
    tm = COMBINE_TILE
    return pl.pallas_call(
        _combine_kernel,
        out_shape=jax.ShapeDtypeStruct((s, d), F32),
        grid=(s // tm,),
        in_specs=[pl.BlockSpec((tm * TOP_K,), lambda i: (i,), memory_space=pltpu.SMEM),
                  pl.BlockSpec(memory_space=pl.ANY),
                  pl.BlockSpec((tm, TOP_K), lambda i: (i, 0)),
                  pl.BlockSpec((tm, d), lambda i: (i, 0)),
                  pl.BlockSpec((1, d), lambda i: (0, 0)),
                  pl.BlockSpec((1, d), lambda i: (0, 0))],
        out_specs=pl.BlockSpec((tm, d), lambda i: (i, 0)),
        scratch_shapes=[pltpu.VMEM((TOP_K, tm * SUBLANES, LANES), F32), pltpu.SemaphoreType.DMA],
        compiler_params=_cparams(("arbitrary",)),
        name="combine",
    )(dest_flat, y, gate4, x1, gt, gpost)


def _layer(x2, mod, g_pre_mix, g_post_mix, w_in, b_forget, rel_bias, w_out,
           g_pre_ffn, g_post_ffn, w_router, b_router, w_gate_up, b_gate_up, w_down, b_down):
    s, d = x2.shape
    sh_m, sc_m, gt_m, sh_f, sc_f, gt_f = [mod[:, k * d:(k + 1) * d] for k in range(6)]
    n_qkv = 3 * (N_HEADS_FOX + N_HEADS_MOBA) * HEAD_DIM
    fox_w = N_HEADS_FOX * HEAD_DIM

    w_qkv = w_in[:, :n_qkv].astype(BF16)
    w_f = w_in[:, n_qkv:].T
    qkv, cum, sel, nrm, tr, nr = _inproj(x2, g_pre_mix.reshape(1, d), sc_m, sh_m, w_qkv, w_f,
                                         b_forget.reshape(-1, 1))
    y_a = _fox(qkv, tr, nr, cum, nrm)
    y_b = _moba(qkv, tr, sel, rel_bias)

    w_out_bf = w_out.astype(BF16)
    x1, h2, idx4, rank4, gate4, cnt = _outproj(
        y_a, y_b, x2, w_out_bf[:fox_w], w_out_bf[fox_w:], g_post_mix.reshape(1, d), gt_m,
        g_pre_ffn.reshape(1, d), sc_f, sh_f, w_router, b_router.reshape(1, -1))

    bm = EXPERT_BLOCK
    counts = cnt[0].astype(I32)
    pblk = (counts + bm - 1) // bm
    pend_blk = jnp.cumsum(pblk)
    pstart = ((pend_blk - pblk) * bm).astype(I32)
    n_rows = s * TOP_K + N_EXPERTS * bm
    n_blk = n_rows // bm
    block_e = jnp.minimum(jnp.sum(pend_blk[None, :] <= jnp.arange(n_blk)[:, None], axis=1),
                          N_EXPERTS - 1).astype(I32)
    n_used = pend_blk[-1:].astype(I32)

    dest_flat = _dest(idx4, rank4, pstart).reshape(-1)
    xs = _dispatch(pstart, pblk.astype(I32), n_used, dest_flat, h2, n_rows)
    y = _experts(block_e, n_used, xs, w_gate_up, b_gate_up, w_down, b_down)
    return _combine(dest_flat, y, gate4, x1, gt_f, g_post_ffn.reshape(1, d))


def kernel(x, c, w_ada, b_ada, g_pre_mix, g_post_mix, w_in, b_forget, rel_bias, w_out, g_pre_ffn, g_post_ffn, w_router, b_router, w_gate_up, b_gate_up, w_down, b_down):
    bsz, s, d = x.shape
    depth = w_ada.shape[0]
    outs = []
    for bi in range(bsz):
        x2 = x[bi]
        for l in range(depth):
            mod = _adaln(c[bi:bi + 1], w_ada[l], b_ada[l])
            x2 = _layer(x2, mod, g_pre_mix[l], g_post_mix[l], w_in[l], b_forget[l], rel_bias, w_out[l],
                        g_pre_ffn[l], g_post_ffn[l], w_router[l], b_router[l], w_gate_up[l], b_gate_up[l],
                        w_down[l], b_down[l])
        outs.append(x2)
    return jnp.stack(outs)
```

```python
import functools
import math

import numpy as np
import jax
import jax.numpy as jnp
from jax import lax
from jax.experimental import pallas as pl
from jax.experimental.pallas import tpu as pltpu

F32 = jnp.float32
BF16 = jnp.bfloat16
I32 = jnp.int32

HEAD_DIM = 64
N_HEADS_FOX = 8
N_HEADS_MOBA = 8
PAIR = 2 * HEAD_DIM
MOBA_BLOCK = 256
MOBA_TOPK = 3
NUM_BUCKETS = 32
MAX_DISTANCE = 128
N_EXPERTS = 32
TOP_K = 4
SWIGLU_LIMIT = 7.0
SWIGLU_ALPHA = 1.702
RMS_EPS = 1e-6
NEG = -(2.0 ** 100)
M_INIT = -(2.0 ** 99)
LOG2E = math.log2(math.e)
SUM_ROWS = 16
SUBLANES = 8
LANES = 128
EXP_UNDERFLOW = 90.0
VMEM_LIMIT = 56 * 1024 * 1024

ROW_TILE = 512
FOX_TILE = 256
EXPERT_BLOCK = 256
COMBINE_TILE = 256

NT_DIMS = (((1,), (1,)), ((), ()))


def _cparams(sem):
    return pltpu.CompilerParams(dimension_semantics=sem, vmem_limit_bytes=VMEM_LIMIT)


def _rms(x):
    return x * lax.rsqrt(jnp.mean(x * x, axis=-1, keepdims=True) + RMS_EPS)


def _adaln_kernel(c_ref, w_ref, b_ref, o_ref):
    c = c_ref[...]
    cond = c * jax.nn.sigmoid(c)
    o_ref[...] = jnp.dot(cond, w_ref[...], preferred_element_type=F32,
                         precision=lax.Precision.HIGHEST) + b_ref[...]


def _adaln(c, w_ada, b_ada):
    d = c.shape[-1]
    n = w_ada.shape[-1]
    c8 = jnp.broadcast_to(c.reshape(1, d), (8, d))
    out = pl.pallas_call(
        _adaln_kernel,
        out_shape=jax.ShapeDtypeStruct((8, n), F32),
        grid=(n // d,),
        in_specs=[pl.BlockSpec((8, d), lambda j: (0, 0)),
                  pl.BlockSpec((d, d), lambda j: (0, j)),
                  pl.BlockSpec((1, d), lambda j: (0, j))],
        out_specs=pl.BlockSpec((8, d), lambda j: (0, j)),
        compiler_params=_cparams(("arbitrary",)),
        name="adaln",
    )(c8, w_ada, b_ada.reshape(1, n))
    return out[0:1]


def _inproj_kernel(x_ref, g_ref, sc_ref, sh_ref, w_ref, wf_ref, bf_ref,
                   qkv_ref, cum_ref, sel_ref, nrm_ref, tr_ref, nr_ref, km_scr, carry_scr):
    i = pl.program_id(0)
    tm = x_ref.shape[0]
    nblk = km_scr.shape[0]

    @pl.when(i == 0)
    def _():
        km_scr[...] = jnp.zeros_like(km_scr)
        carry_scr[...] = jnp.zeros_like(carry_scr)

    x = x_ref[...]
    h = _rms(x) * g_ref[...] * (1.0 + sc_ref[...]) + sh_ref[...]
    hb = h.astype(BF16)

    width = N_HEADS_FOX * HEAD_DIM
    hsel = jnp.where(lax.broadcasted_iota(I32, (width, N_HEADS_FOX), 0) // HEAD_DIM
                     == lax.broadcasted_iota(I32, (width, N_HEADS_FOX), 1), 1.0, 0.0)
    qb = kb = None
    tr_slot = {3: 0, 5: 1, 0: 2, 2: 3}
    for c in range(6):
        pc = jnp.dot(hb, w_ref[:, c * width:(c + 1) * width], preferred_element_type=F32)
        if c == 0 or c == 3:
            pc_t = pc * (LOG2E * HEAD_DIM ** -0.5)
        if c in tr_slot:
            n = tr_slot[c]
            tr_ref[n * width:(n + 1) * width, :] = (pc_t if c in (0, 3) else pc).T.astype(BF16)
        if c == 1 or c == 4:
            qkv_ref[:, (c // 3) * width:(c // 3 + 1) * width] = pc.astype(BF16)
        if c < 2:
            sq = (pc * (HEAD_DIM ** -0.5) if c == 0 else pc).astype(BF16).astype(F32)
            n2 = jnp.dot((sq * sq).astype(BF16), hsel.astype(BF16), preferred_element_type=F32)
            nrm_ref[0, c:c + 1, :] = jnp.max(n2, axis=0, keepdims=True)
        if c == 3:
            qb = pc
        if c == 4:
            kb = pc

    ft = lax.dot_general(wf_ref[...], h, NT_DIMS, preferred_element_type=F32,
                         precision=lax.Precision.HIGHEST)
    z = ft + bf_ref[...]
    logf = -(jnp.maximum(-z, 0.0) + jnp.log1p(jnp.exp(-jnp.abs(z))))
    lane = lax.broadcasted_iota(I32, logf.shape, 1)
    cs = logf
    sh = 1
    while sh < tm:
        cs = cs + jnp.where(lane >= sh, pltpu.roll(cs, sh, axis=1), 0.0)
        sh *= 2
    base = jnp.zeros_like(cs)
    for b in range(1, tm // FOX_TILE):
        base = jnp.where(lane >= b * FOX_TILE, cs[:, b * FOX_TILE - 1:b * FOX_TILE], base)
    nr_ref[...] = (-LOG2E * (cs - base)).T
    cs = cs + carry_scr[...]
    cum_ref[...] = cs
    carry_scr[...] = cs[:, tm - 1:tm]

    nb_tile = tm // MOBA_BLOCK
    for b in range(nb_tile):
        kmean = jnp.sum(kb[b * MOBA_BLOCK:(b + 1) * MOBA_BLOCK], axis=0, keepdims=True) * (1.0 / MOBA_BLOCK)
        km_scr[pl.ds(i * nb_tile + b, 1), :] = kmean

    km = km_scr[...]
    lane_h = lax.broadcasted_iota(I32, km.shape, 1) // HEAD_DIM
    blk = lax.broadcasted_iota(I32, (nblk, tm), 0)
    col = lax.broadcasted_iota(I32, (nblk, tm), 1)
    own = i * nb_tile + col // MOBA_BLOCK
    for hd in range(N_HEADS_MOBA):
        kmh = jnp.where(lane_h == hd, km, 0.0)
        g = lax.dot_general(kmh, qb, NT_DIMS, preferred_element_type=F32,
                            precision=lax.Precision.HIGHEST)
        g = jnp.where(blk < own, g, -jnp.inf)
        sel = jnp.zeros(g.shape, dtype=jnp.bool_)
        for _ in range(MOBA_TOPK):
            m = jnp.max(g, axis=0, keepdims=True)
            first = jnp.min(jnp.where(g == m, blk, nblk), axis=0, keepdims=True)
            pick = (blk == first) & (m > -jnp.inf)
            sel = sel | pick
            g = jnp.where(pick, -jnp.inf, g)
        sel_ref[hd] = jnp.where(sel, 0.0, NEG)


def _inproj(x2, g, sc, sh, w_qkv, w_f, b_f):
    s, d = x2.shape
    tm = ROW_TILE
    nblk = s // MOBA_BLOCK
    n = w_qkv.shape[1]
    width = N_HEADS_FOX * HEAD_DIM
    n_rows_out = 2 * width
    n_tr = 4 * width
    assert tm % FOX_TILE == 0 and N_HEADS_FOX == N_HEADS_MOBA
    return pl.pallas_call(
        _inproj_kernel,
        out_shape=(jax.ShapeDtypeStruct((s, n_rows_out), BF16),
                   jax.ShapeDtypeStruct((N_HEADS_FOX, s), F32),
                   jax.ShapeDtypeStruct((N_HEADS_MOBA, nblk, s), F32),
                   jax.ShapeDtypeStruct((s // tm, 2, N_HEADS_FOX), F32),
                   jax.ShapeDtypeStruct((n_tr, s), BF16),
                   jax.ShapeDtypeStruct((s, N_HEADS_FOX), F32)),
        grid=(s // tm,),
        in_specs=[pl.BlockSpec((tm, d), lambda i: (i, 0)),
                  pl.BlockSpec((1, d), lambda i: (0, 0)),
                  pl.BlockSpec((1, d), lambda i: (0, 0)),
                  pl.BlockSpec((1, d), lambda i: (0, 0)),
                  pl.BlockSpec((d, n), lambda i: (0, 0)),
                  pl.BlockSpec((N_HEADS_FOX, d), lambda i: (0, 0)),
                  pl.BlockSpec((N_HEADS_FOX, 1), lambda i: (0, 0))],
        out_specs=(pl.BlockSpec((tm, n_rows_out), lambda i: (i, 0)),
                   pl.BlockSpec((N_HEADS_FOX, tm), lambda i: (0, i)),
                   pl.BlockSpec((N_HEADS_MOBA, nblk, tm), lambda i: (0, 0, i)),
                   pl.BlockSpec((1, 2, N_HEADS_FOX), lambda i: (i, 0, 0)),
                   pl.BlockSpec((n_tr, tm), lambda i: (0, i)),
                   pl.BlockSpec((tm, N_HEADS_FOX), lambda i: (i, 0))),
        scratch_shapes=[pltpu.VMEM((nblk, N_HEADS_MOBA * HEAD_DIM), F32),
                        pltpu.VMEM((N_HEADS_FOX, 1), F32)],
        compiler_params=_cparams(("arbitrary",)),
        name="inproj",
    )(x2, g, sc, sh, w_qkv, w_f, b_f)


def _fox_kernel(jlo_ref, cp_ref, qt_ref, k_ref, vt_ref, nr_ref, o_ref, m_ref, a_ref):
    p = pl.program_id(0)
    i = pl.program_id(1)
    j_first = jlo_ref[p * pl.num_programs(1) + i]
    t = qt_ref.shape[1]

    qt = qt_ref[...]
    top = lax.broadcasted_iota(I32, qt.shape, 0) < HEAD_DIM
    zq = jnp.zeros_like(qt)
    qth = (jnp.where(top, qt, zq), jnp.where(top, zq, qt))
    m_ref[...] = jnp.full(m_ref.shape, M_INIT, F32)
    a_ref[...] = jnp.zeros(a_ref.shape, F32)
    er = lax.broadcasted_iota(I32, (SUM_ROWS, 2 * t), 0)
    ec = lax.broadcasted_iota(I32, (SUM_ROWS, 2 * t), 1)
    ones_rows = jnp.where(((er == 0) & (ec < t)) | ((er == 1) & (ec >= t)), 1.0, 0.0).astype(BF16)
    arow = lax.broadcasted_iota(I32, a_ref.shape, 0)
    head0_rows = (arow < HEAD_DIM) | (arow == PAIR)

    def tile(j, diagonal):
        k0 = pl.multiple_of(j * t, t)
        kt = k_ref[pl.ds(k0, t), :]
        vt = vt_ref[:, pl.ds(k0, t)]
        ps, alphas = [], []
        for hd in range(2):
            h = 2 * p + hd
            s = jnp.dot(kt, qth[hd], preferred_element_type=F32) + nr_ref[0, pl.ds(k0, t), hd:hd + 1]
            if diagonal:
                r = lax.broadcasted_iota(I32, s.shape, 0)
                c = lax.broadcasted_iota(I32, s.shape, 1)
                s = jnp.where(r <= c, s, NEG)
            pair_shift = LOG2E * (cp_ref[h, i] - cp_ref[h, j])
            m_old = m_ref[hd]
            m_new = jnp.maximum(m_old, jnp.max(s, axis=0, keepdims=True) + pair_shift)
            alphas.append(jnp.exp2(m_old - m_new))
            m_ref[hd] = m_new
            ps.append(jnp.exp2(s - (m_new - pair_shift)).astype(BF16))
        zv = jnp.zeros_like(vt)
        vcat = jnp.concatenate([jnp.where(top, vt, zv), jnp.where(top, zv, vt)], axis=1)
        vcat = jnp.concatenate([vcat, ones_rows], axis=0)
        pcat = jnp.concatenate(ps, axis=0)
        a_ref[...] = (a_ref[...] * jnp.where(head0_rows, alphas[0], alphas[1])
                      + jnp.dot(vcat, pcat, preferred_element_type=F32))

    def body(j, carry):
        tile(j, False)
        return carry

    lax.fori_loop(j_first, i, body, 0)
    tile(i, True)
    out_t = a_ref[0:PAIR, :] / jnp.where(top, a_ref[PAIR:PAIR + 1, :], a_ref[PAIR + 1:PAIR + 2, :])
    o_ref[...] = out_t.T.astype(o_ref.dtype)


def _fox_first_tile(cum, nrm, t):
    cend = cum[:, t - 1::t]
    nt = cend.shape[1]
    cprev = jnp.concatenate([jnp.zeros((cend.shape[0], 1), F32), cend[:, :-1]], axis=1)
    rep = nt // nrm.shape[0]
    qn = jnp.repeat(jnp.sqrt(nrm[:, 0, :]).T, rep, axis=1)
    kn = jnp.repeat(jnp.sqrt(nrm[:, 1, :]).T, rep, axis=1)
    gap = (1.02 * qn[:, :, None] * (kn[:, None, :] + kn[:, :, None])
           + cprev[:, :, None] - cend[:, None, :])
    jj = jnp.arange(nt)[None, None, :]
    ii = jnp.arange(nt)[None, :, None]
    needed = (jj < ii) & jnp.logical_not(gap <= -EXP_UNDERFLOW)
    needed = needed[0::2] | needed[1::2]
    first = jnp.min(jnp.where(needed, jj, ii), axis=2)
    return first.reshape(-1).astype(I32)


def _fox(qkv, tr, nr, cum, nrm):
    s = qkv.shape[0]
    t = FOX_TILE
    npair = N_HEADS_FOX // 2
    jlo = _fox_first_tile(cum, nrm, t)
    cend = cum[:, t - 1::t]
    cprev = jnp.concatenate([jnp.zeros((cend.shape[0], 1), F32), cend[:, :-1]], axis=1)
    q_rows = 2 * (N_HEADS_MOBA // 2)
    grid_spec = pltpu.PrefetchScalarGridSpec(
        num_scalar_prefetch=1,
        grid=(npair, s // t),
        in_specs=[pl.BlockSpec(memory_space=pltpu.SMEM),
                  pl.BlockSpec((PAIR, t), lambda p, i, jl: (q_rows + p, i)),
                  pl.BlockSpec((s, PAIR), lambda p, i, jl: (0, p)),
                  pl.BlockSpec((PAIR, s), lambda p, i, jl: (q_rows + npair + p, 0)),
                  pl.BlockSpec((1, s, 2), lambda p, i, jl: (p, 0, 0))],
        out_specs=pl.BlockSpec((t, PAIR), lambda p, i, jl: (i, p)),
        scratch_shapes=[pltpu.VMEM((2, 1, t), F32), pltpu.VMEM((PAIR + SUM_ROWS, t), F32)],
    )
    nr_pairs = nr.reshape(s, npair, 2).transpose(1, 0, 2)
    return pl.pallas_call(
        _fox_kernel,
        out_shape=jax.ShapeDtypeStruct((s, npair * PAIR), BF16),
        grid_spec=grid_spec,
        compiler_params=_cparams(("arbitrary", "arbitrary")),
        name="fox",
    )(jlo, cprev, tr, qkv, tr, nr_pairs)


def _t5_bucket_np(dist):
    dist = np.maximum(dist, 0)
    max_exact = NUM_BUCKETS // 2
    d = np.maximum(dist, 1).astype(np.float32)
    large = max_exact + (np.log(d / np.float32(max_exact)) / np.float32(math.log(MAX_DISTANCE / max_exact))
                         * np.float32(NUM_BUCKETS - max_exact)).astype(np.int32)
    large = np.minimum(large, NUM_BUCKETS - 1)
    return np.where(dist < max_exact, dist, large).astype(np.int32)


def _moba_kernel(rb_ref, qt_ref, k_ref, vt_ref, sel_ref, bkt_ref, o_ref,
                 m_ref, a_ref, bias_scr, sa_ref, sb_ref, pa_ref, pb_ref, ala_ref, alb_ref):
    p = pl.program_id(0)
    g = pl.program_id(1)
    tq = qt_ref.shape[1]
    t = tq // 2

    @pl.when(g == 0)
    def _():
        r = lax.broadcasted_iota(I32, (t, t), 0)
        c = lax.broadcasted_iota(I32, (t, t), 1)
        zero = jnp.zeros((t, t), F32)
        for hd in range(2):
            h = 2 * p + hd
            far = rb_ref[(NUM_BUCKETS - 1) * N_HEADS_MOBA + h]
            tiles = []
            for w in range(2):
                bkt = bkt_ref[w]
                acc = jnp.zeros(bkt.shape, F32)
                for kk in range(NUM_BUCKETS):
                    acc = acc + jnp.where(bkt == kk, rb_ref[kk * N_HEADS_MOBA + h], 0.0)
                tiles.append((acc - far) * LOG2E)
            prev_t = tiles[0]
            own_t = jnp.where(r <= c, tiles[1], NEG)
            bias_scr[hd, 0] = jnp.concatenate([prev_t, zero], axis=1)
            bias_scr[hd, 1] = jnp.concatenate([own_t, prev_t], axis=1)
            bias_scr[hd, 2] = jnp.concatenate([zero, own_t], axis=1)

    qt = qt_ref[...]
    top = lax.broadcasted_iota(I32, qt.shape, 0) < HEAD_DIM
    zq = jnp.zeros_like(qt)
    qth = (jnp.where(top, qt, zq), jnp.where(top, zq, qt))
    m_ref[...] = jnp.full(m_ref.shape, M_INIT, F32)
    a_ref[...] = jnp.zeros(a_ref.shape, F32)

    er = lax.broadcasted_iota(I32, (SUM_ROWS, 2 * t), 0)
    ec = lax.broadcasted_iota(I32, (SUM_ROWS, 2 * t), 1)
    ones_rows = jnp.where(((er == 0) & (ec < t)) | ((er == 1) & (ec >= t)), 1.0, 0.0).astype(BF16)
    arow = lax.broadcasted_iota(I32, a_ref.shape, 0)
    head0_rows = (arow < HEAD_DIM) | (arow == PAIR)
    vtop = lax.broadcasted_iota(I32, (PAIR, t), 0) < HEAD_DIM
    in_a = lax.broadcasted_iota(I32, (1, tq), 1) < t

    n_far = jnp.maximum(2 * g - 1, 0)

    def produce(j, s_buf):
        k0 = pl.multiple_of(j * t, t)
        kt = k_ref[pl.ds(k0, t), :]
        for hd in range(2):
            s_buf[hd] = jnp.dot(kt, qth[hd], preferred_element_type=F32)

    def softmax(s_buf, p_buf, al_buf, selrows, w):
        for hd in range(2):
            s = s_buf[hd]
            if w is not None:
                s = s + bias_scr[hd, w]
            smax = jnp.max(s, axis=0, keepdims=True)
            m_old = m_ref[hd]
            m_new = jnp.maximum(m_old, smax + selrows[hd])
            shift = m_new - selrows[hd]
            al_buf[hd] = jnp.exp2(m_old - m_new)
            m_ref[hd] = m_new
            p_buf[hd * t:(hd + 1) * t, :] = jnp.exp2(s - shift).astype(BF16)

    def far_rows(j):
        return [jnp.where(j < n_far, sel_ref[hd, pl.ds(j, 1), :], NEG) for hd in range(2)]

    def accumulate(j, p_buf, al_buf):
        k0 = pl.multiple_of(j * t, t)
        vt = vt_ref[:, pl.ds(k0, t)]
        zv = jnp.zeros_like(vt)
        vcat = jnp.concatenate([jnp.where(vtop, vt, zv), jnp.where(vtop, zv, vt)], axis=1)
        vcat = jnp.concatenate([vcat, ones_rows], axis=0)
        a_ref[...] = (a_ref[...] * jnp.where(head0_rows, al_buf[0], al_buf[1])
                      + jnp.dot(vcat, p_buf[...], preferred_element_type=F32))

    pb_ref[...] = jnp.zeros_like(pb_ref)
    alb_ref[...] = jnp.ones_like(alb_ref)
    produce(0, sa_ref)

    def body(u, carry):
        j0 = 2 * u
        produce(j0 + 1, sb_ref)
        softmax(sa_ref, pa_ref, ala_ref, far_rows(j0), None)
        accumulate(jnp.maximum(j0 - 1, 0), pb_ref, alb_ref)
        produce(j0 + 2, sa_ref)
        accumulate(j0, pa_ref, ala_ref)
        softmax(sb_ref, pb_ref, alb_ref, far_rows(j0 + 1), None)
        return carry

    n_pairs = (n_far + 1) // 2
    lax.fori_loop(0, n_pairs, body, 0)

    j1 = jnp.maximum(2 * g - 1, 0)
    j2 = 2 * g
    j3 = 2 * g + 1
    rows1 = [jnp.where(g >= 1, sel_ref[hd, pl.ds(j1, 1), :], NEG) for hd in range(2)]
    rows2 = [jnp.where(in_a, 0.0, sel_ref[hd, pl.ds(j2, 1), :]) for hd in range(2)]
    rows3 = [jnp.where(in_a, NEG, 0.0)] * 2
    produce(j1, sa_ref)
    produce(j2, sb_ref)
    softmax(sa_ref, pa_ref, ala_ref, rows1, 0)
    accumulate(jnp.maximum(2 * n_pairs - 1, 0), pb_ref, alb_ref)
    produce(j3, sa_ref)
    softmax(sb_ref, pb_ref, alb_ref, rows2, 1)
    accumulate(j1, pa_ref, ala_ref)
    softmax(sa_ref, pa_ref, ala_ref, rows3, 2)
    accumulate(j2, pb_ref, alb_ref)
    accumulate(j3, pa_ref, ala_ref)
    out_t = a_ref[0:PAIR, :] / jnp.where(top, a_ref[PAIR:PAIR + 1, :], a_ref[PAIR + 1:PAIR + 2, :])
    o_ref[...] = out_t.T.astype(o_ref.dtype)


def _moba(qkv, tr, sel, rel_bias):
    s = qkv.shape[0]
    t = MOBA_BLOCK
    npair = N_HEADS_MOBA // 2
    nblk = s // t
    kcol = N_HEADS_FOX // 2
    a = np.arange(t)[None, :]
    b = np.arange(t)[:, None]
    bkt = jnp.asarray(np.stack([_t5_bucket_np(t + a - b), _t5_bucket_np(a - b)]))
    tq = 2 * t
    grid_spec = pltpu.PrefetchScalarGridSpec(
        num_scalar_prefetch=1,
        grid=(npair, s // tq),
        in_specs=[pl.BlockSpec((PAIR, tq), lambda p, i, rb: (p, i)),
                  pl.BlockSpec((s, PAIR), lambda p, i, rb: (0, kcol + p)),
                  pl.BlockSpec((PAIR, s), lambda p, i, rb: (npair + p, 0)),
                  pl.BlockSpec((2, nblk, tq), lambda p, i, rb: (p, 0, i)),
                  pl.BlockSpec((2, t, t), lambda p, i, rb: (0, 0, 0))],
        out_specs=pl.BlockSpec((tq, PAIR), lambda p, i, rb: (i, p)),
        scratch_shapes=[pltpu.VMEM((2, 1, tq), F32),
                        pltpu.VMEM((PAIR + SUM_ROWS, tq), F32), pltpu.VMEM((2, 3, t, tq), F32),
                        pltpu.VMEM((2, t, tq), F32), pltpu.VMEM((2, t, tq), F32),
                        pltpu.VMEM((2 * t, tq), BF16), pltpu.VMEM((2 * t, tq), BF16),
                        pltpu.VMEM((2, 1, tq), F32), pltpu.VMEM((2, 1, tq), F32)],
    )
    return pl.pallas_call(
        _moba_kernel,
        out_shape=jax.ShapeDtypeStruct((s, npair * PAIR), BF16),
        grid_spec=grid_spec,
        compiler_params=_cparams(("arbitrary", "arbitrary")),
        name="moba",
    )(rel_bias.reshape(-1), tr, qkv, tr, sel, bkt)


def _store_token_tiles(ref, val):
    n = val.shape[0]
    for c in range(SUBLANES):
        ref[pl.ds(c, n, stride=SUBLANES), :] = val[:, c * LANES:(c + 1) * LANES]


def _load_token_tiles(ref, n):
    return jnp.concatenate([ref[pl.ds(c, n, stride=SUBLANES), :] for c in range(SUBLANES)], axis=1)


def _tile_rows(r0, n):
    start = r0 * SUBLANES
    if not isinstance(start, int):
        start = pl.multiple_of(start, SUBLANES)
    return pl.ds(start, n * SUBLANES)


def _outproj_kernel(ya_ref, yb_ref, x_ref, wa_ref, wb_ref, gpost_ref, gt_ref, gpre_ref,
                    sc_ref, sh_ref, wr_ref, br_ref,
                    x1_ref, h2_ref, idx_ref, rank_ref, gate_ref, cnt_ref, carry_scr):
    i = pl.program_id(0)
    tm = x_ref.shape[0]

    @pl.when(i == 0)
    def _():
        carry_scr[...] = jnp.zeros_like(carry_scr)

    y = (jnp.dot(ya_ref[...], wa_ref[...], preferred_element_type=F32)
         + jnp.dot(yb_ref[...], wb_ref[...], preferred_element_type=F32))
    x1 = x_ref[...] + gt_ref[...] * (_rms(y) * gpost_ref[...])
    x1_ref[...] = x1
    h2 = _rms(x1) * gpre_ref[...] * (1.0 + sc_ref[...]) + sh_ref[...]
    _store_token_tiles(h2_ref, h2)

    logits =jnp.dot(h2, wr_ref[...], preferred_element_type=F32,
                     precision=lax.Precision.HIGHEST) + br_ref[...]
    ne = logits.shape[1]
    lane = lax.broadcasted_iota(I32, logits.shape, 1)
    lane4 = lax.broadcasted_iota(I32, (tm, TOP_K), 1)
    g = logits
    mask = jnp.zeros(logits.shape, F32)
    vals, picks = [], []
    for _ in range(TOP_K):
        m = jnp.max(g, axis=1, keepdims=True)
        first = jnp.min(jnp.where(g == m, lane, ne), axis=1, keepdims=True)
        pick = lane == first
        mask = jnp.where(pick, 1.0, mask)
        g = jnp.where(pick, -jnp.inf, g)
        vals.append(m)
        picks.append(pick)
    ex = [jnp.exp(v - vals[0]) for v in vals]
    den = ex[0] + ex[1] + ex[2] + ex[3]
    gates = [e / den for e in ex]

    r = lax.broadcasted_iota(I32, (tm, tm), 0)
    c = lax.broadcasted_iota(I32, (tm, tm), 1)
    tril = jnp.where(c < r, 1.0, 0.0).astype(BF16)
    before = jnp.dot(tril, mask.astype(BF16), preferred_element_type=F32) + carry_scr[...]
    total = carry_scr[...] + jnp.sum(mask, axis=0, keepdims=True)
    carry_scr[...] = total
    cnt_ref[...] = jnp.broadcast_to(total, cnt_ref.shape)

    def pack4(cols):
        return jnp.where(lane4 == 0, cols[0],
                         jnp.where(lane4 == 1, cols[1], jnp.where(lane4 == 2, cols[2], cols[3])))

    ranks = [jnp.sum(jnp.where(pk, before, 0.0), axis=1, keepdims=True) for pk in picks]
    idxs = [jnp.sum(jnp.where(pk, lane, 0), axis=1, keepdims=True) for pk in picks]
    idx_ref[...] = pack4(idxs)
    rank_ref[...] = pack4(ranks).astype(I32)
    gate_ref[...] = pack4(gates)


def _outproj(mix_a, mix_b, x2, w_a, w_b, gpost, gt, gpre, sc, sh, w_router, b_router):
    s, d = x2.shape
    tm = ROW_TILE
    ne = w_router.shape[1]
    wa = mix_a.shape[1]
    row = lambda i: (i, 0)
    fix = lambda i: (0, 0)
    vec = pl.BlockSpec((1, d), fix)
    return pl.pallas_call(
        _outproj_kernel,
        out_shape=(jax.ShapeDtypeStruct((s, d), F32),
                   jax.ShapeDtypeStruct((s * SUBLANES, LANES), F32),
                   jax.ShapeDtypeStruct((s, TOP_K), I32),
                   jax.ShapeDtypeStruct((s, TOP_K), I32),
                   jax.ShapeDtypeStruct((s, TOP_K), F32),
                   jax.ShapeDtypeStruct((8, ne), F32)),
        grid=(s // tm,),
        in_specs=[pl.BlockSpec((tm, wa), row), pl.BlockSpec((tm, wa), row), pl.BlockSpec((tm, d), row),
                  pl.BlockSpec((wa, d), fix), pl.BlockSpec((wa, d), fix),
                  vec, vec, vec, vec, vec,
                  pl.BlockSpec((d, ne), fix), pl.BlockSpec((1, ne), fix)],
        out_specs=(pl.BlockSpec((tm, d), row), pl.BlockSpec((tm * SUBLANES, LANES), row),
                   pl.BlockSpec((tm, TOP_K), row), pl.BlockSpec((tm, TOP_K), row),
                   pl.BlockSpec((tm, TOP_K), row), pl.BlockSpec((8, ne), fix)),
        scratch_shapes=[pltpu.VMEM((1, ne), F32)],
        compiler_params=_cparams(("arbitrary",)),
        name="outproj",
    )(mix_a, mix_b, x2, w_a, w_b, gpost, gt, gpre, sc, sh, w_router, b_router)


def _dest_kernel(idx_ref, rank_ref, pstart_ref, o_ref):
    idx = idx_ref[...]
    tm = idx.shape[0]
    ne = pstart_ref.shape[1]
    lane = lax.broadcasted_iota(I32, (tm, ne), 1)
    lane4 = lax.broadcasted_iota(I32, idx.shape, 1)
    out = rank_ref[...]
    for k in range(TOP_K):
        start = jnp.sum(jnp.where(lane == idx[:, k:k + 1], pstart_ref[...], 0), axis=1, keepdims=True)
        out = out + jnp.where(lane4 == k, start, 0)
    o_ref[...] = out


def _dest(idx4, rank4, pstart):
    s = idx4.shape[0]
    tm = min(4 * ROW_TILE, s)
    row = lambda i: (i, 0)
    return pl.pallas_call(
        _dest_kernel,
        out_shape=jax.ShapeDtypeStruct((s, TOP_K), I32),
        grid=(s // tm,),
        in_specs=[pl.BlockSpec((tm, TOP_K), row), pl.BlockSpec((tm, TOP_K), row),
                  pl.BlockSpec((1, N_EXPERTS), lambda i: (0, 0))],
        out_specs=pl.BlockSpec((tm, TOP_K), row),
        compiler_params=_cparams(("arbitrary",)),
        name="dest",
    )(idx4, rank4, pstart.reshape(1, -1))


def _dispatch_kernel(pstart_ref, pblk_ref, nu_ref, dest_ref, h_ref, xs_ref,
                     zero_scr, sem, zsem):
    tm = h_ref.shape[0] // SUBLANES
    bm = zero_scr.shape[0] // SUBLANES
    n_blk = xs_ref.shape[0] // (bm * SUBLANES)

    rows = _tile_rows

    @pl.when(pl.program_id(0) == 0)
    def _():
        zero_scr[...] = jnp.zeros_like(zero_scr)

        def zero_copy(row0):
            return pltpu.make_async_copy(zero_scr, xs_ref.at[rows(row0, bm)], zsem)

        for phase in range(2):
            for e in range(N_EXPERTS):
                last = pstart_ref[e] + (pblk_ref[e] - 1) * bm
                tail = (n_blk - N_EXPERTS + e) * bm
                for cond, row0 in ((pblk_ref[e] > 0, last), (n_blk - N_EXPERTS + e >= nu_ref[0], tail)):
                    @pl.when(cond)
                    def _():
                        if phase == 0:
                            zero_copy(row0).start()
                        else:
                            zero_copy(row0).wait()

    def row_copy(r, k):
        dst = dest_ref[r * TOP_K + k]
        return pltpu.make_async_copy(h_ref.at[rows(r, 1)], xs_ref.at[rows(dst, 1)], sem)

    def start(r, carry):
        for k in range(TOP_K):
            row_copy(r, k).start(priority=k % 2)
        return carry

    lax.fori_loop(0, tm, start, 0, unroll=4)
    for k in range(TOP_K):
        pltpu.make_async_copy(h_ref, xs_ref.at[rows(0, tm)], sem).wait()


def _dispatch(pstart, pblk, n_used, dest_flat, h2, n_rows):
    s = h2.shape[0] // SUBLANES
    tm = COMBINE_TILE
    grid_spec = pltpu.PrefetchScalarGridSpec(
        num_scalar_prefetch=3,
        grid=(s // tm,),
        in_specs=[pl.BlockSpec((tm * TOP_K,), lambda i, *_: (i,), memory_space=pltpu.SMEM),
                  pl.BlockSpec((tm * SUBLANES, LANES), lambda i, *_: (i, 0))],
        out_specs=pl.BlockSpec(memory_space=pl.ANY),
        scratch_shapes=[pltpu.VMEM((EXPERT_BLOCK * SUBLANES, LANES), F32), pltpu.SemaphoreType.DMA,
                        pltpu.SemaphoreType.DMA],
    )
    return pl.pallas_call(
        _dispatch_kernel,
        out_shape=jax.ShapeDtypeStruct((n_rows * SUBLANES, LANES), F32),
        grid_spec=grid_spec,
        compiler_params=_cparams(("arbitrary",)),
        name="dispatch",
    )(pstart, pblk, n_used, dest_flat, h2)


def _experts_kernel(be_ref, nu_ref, xs_ref, wgu_ref, bgu_ref, wd_ref, bd_ref, y_ref,
                    wgu_bf, wd_bf):
    b = pl.program_id(0)
    d_exp = wd_ref.shape[1]
    prev = be_ref[jnp.maximum(b - 1, 0)]
    changed = (b == 0) | (be_ref[b] != prev)

    @pl.when((b < nu_ref[0]) & changed)
    def _():
        rows = 128

        def cast_gu(c, carry):
            r0 = pl.multiple_of(c * rows, rows)
            wgu_bf[pl.ds(r0, rows), :] = wgu_ref[0, pl.ds(r0, rows), :].astype(BF16)
            return carry

        def cast_d(c, carry):
            r0 = pl.multiple_of(c * rows, rows)
            wd_bf[pl.ds(r0, rows), :] = wd_ref[0, pl.ds(r0, rows), :].astype(BF16)
            return carry

        lax.fori_loop(0, wgu_ref.shape[1] // rows, cast_gu, 0)
        lax.fori_loop(0, wd_ref.shape[1] // rows, cast_d, 0)

    @pl.when(b < nu_ref[0])
    def _():
        bm = xs_ref.shape[0] // SUBLANES
        xb = _load_token_tiles(xs_ref, bm).astype(BF16)
        hdn = jnp.dot(xb, wgu_bf[...], preferred_element_type=F32) + bgu_ref[0]
        x_glu = jnp.minimum(hdn[:, :d_exp], SWIGLU_LIMIT)
        x_lin = jnp.clip(hdn[:, d_exp:], -SWIGLU_LIMIT, SWIGLU_LIMIT)
        act = x_glu * jax.nn.sigmoid(SWIGLU_ALPHA * x_glu) * (x_lin + 1.0)
        _store_token_tiles(y_ref, jnp.dot(act.astype(BF16), wd_bf[...], preferred_element_type=F32)
                           + bd_ref[0])

    @pl.when(b >= nu_ref[0])
    def _():
        y_ref[...] = jnp.zeros_like(y_ref)


def _experts(block_e, n_used, xs, w_gate_up, b_gate_up, w_down, b_down):
    n_rows = xs.shape[0] // SUBLANES
    bm = EXPERT_BLOCK
    n_blk = n_rows // bm
    ne, d, two_de = w_gate_up.shape
    de = w_down.shape[1]
    assert d == SUBLANES * LANES

    def blk(b, be, nu):
        return jnp.minimum(b, nu[0] - 1)

    grid_spec = pltpu.PrefetchScalarGridSpec(
        num_scalar_prefetch=2,
        grid=(n_blk,),
        in_specs=[pl.BlockSpec((bm * SUBLANES, LANES), lambda b, be, nu: (blk(b, be, nu), 0)),
                  pl.BlockSpec((1, d, two_de), lambda b, be, nu: (be[blk(b, be, nu)], 0, 0)),
                  pl.BlockSpec((1, 1, two_de), lambda b, be, nu: (be[blk(b, be, nu)], 0, 0)),
                  pl.BlockSpec((1, de, d), lambda b, be, nu: (be[blk(b, be, nu)], 0, 0)),
                  pl.BlockSpec((1, 1, d), lambda b, be, nu: (be[blk(b, be, nu)], 0, 0))],
        out_specs=pl.BlockSpec((bm * SUBLANES, LANES), lambda b, be, nu: (b, 0)),
        scratch_shapes=[pltpu.VMEM((d, two_de), BF16), pltpu.VMEM((de, d), BF16)],
    )
    return pl.pallas_call(
        _experts_kernel,
        out_shape=jax.ShapeDtypeStruct((n_rows * SUBLANES, LANES), F32),
        grid_spec=grid_spec,
        compiler_params=_cparams(("arbitrary",)),
        name="experts",
    )(block_e, n_used, xs, w_gate_up, b_gate_up.reshape(ne, 1, two_de), w_down, b_down.reshape(ne, 1, d))


def _combine_kernel(dest_ref, y_ref, gate_ref, x1_ref, gt_ref, gpost_ref, o_ref, buf, sem):
    tm = x1_ref.shape[0]

    def row_copy(r, k):
        src = dest_ref[r * TOP_K + k]
        return pltpu.make_async_copy(y_ref.at[_tile_rows(src, 1)], buf.at[k, _tile_rows(r, 1)], sem)

    def start(r, carry):
        for k in range(TOP_K):
            row_copy(r, k).start(priority=k % 2)
        return carry

    lax.fori_loop(0, tm, start, 0, unroll=4)
    for k in range(TOP_K):
        pltpu.make_async_copy(y_ref.at[_tile_rows(0, tm)], buf.at[k], sem).wait()

    gate = gate_ref[...]
    acc = gate[:, 0:1] * _load_token_tiles(buf.at[0], tm)
    for k in range(1, TOP_K):
        acc = acc + gate[:, k:k + 1] * _load_token_tiles(buf.at[k], tm)
    o_ref[...] = x1_ref[...] + gt_ref[...] * (_rms(acc) * gpost_ref[...])


def _combine(dest_flat, y, gate4, x1, gt, gpost):
    s, d = x1.shape
    tm = COMBINE_TILE
    return pl.pallas_call(
        _combine_kernel,
        out_shape=jax.ShapeDtypeStruct((s, d), F32),
        grid=(s // tm,),
        in_specs=[pl.BlockSpec((tm * TOP_K,), lambda i: (i,), memory_space=pltpu.SMEM),
                  pl.BlockSpec(memory_space=pl.ANY),
                  pl.BlockSpec((tm, TOP_K), lambda i: (i, 0)),
                  pl.BlockSpec((tm, d), lambda i: (i, 0)),
                  pl.BlockSpec((1, d), lambda i: (0, 0)),
                  pl.BlockSpec((1, d), lambda i: (0, 0))],
        out_specs=pl.BlockSpec((tm, d), lambda i: (i, 0)),
        scratch_shapes=[pltpu.VMEM((TOP_K, tm * SUBLANES, LANES), F32), pltpu.SemaphoreType.DMA],
        compiler_params=_cparams(("arbitrary",)),
        name="combine",
    )(dest_flat, y, gate4, x1, gt, gpost)


def _layer(x2, mod, g_pre_mix, g_post_mix, w_in, b_forget, rel_bias, w_out,
           g_pre_ffn, g_post_ffn, w_router, b_router, w_gate_up, b_gate_up, w_down, b_down):
    s, d = x2.shape
    sh_m, sc_m, gt_m, sh_f, sc_f, gt_f = [mod[:, k * d:(k + 1) * d] for k in range(6)]
    n_qkv = 3 * (N_HEADS_FOX + N_HEADS_MOBA) * HEAD_DIM
    fox_w = N_HEADS_FOX * HEAD_DIM

    w_qkv = w_in[:, :n_qkv].astype(BF16)
    w_f = w_in[:, n_qkv:].T
    qkv, cum, sel, nrm, tr, nr = _inproj(x2, g_pre_mix.reshape(1, d), sc_m, sh_m, w_qkv, w_f,
                                         b_forget.reshape(-1, 1))
    y_a = _fox(qkv, tr, nr, cum, nrm)
    y_b = _moba(qkv, tr, sel, rel_bias)

    w_out_bf = w_out.astype(BF16)
    x1, h2, idx4, rank4, gate4, cnt = _outproj(
        y_a, y_b, x2, w_out_bf[:fox_w], w_out_bf[fox_w:], g_post_mix.reshape(1, d), gt_m,
        g_pre_ffn.reshape(1, d), sc_f, sh_f, w_router, b_router.reshape(1, -1))

    bm = EXPERT_BLOCK
    counts = cnt[0].astype(I32)
    pblk = (counts + bm - 1) // bm
    pend_blk = jnp.cumsum(pblk)
    pstart = ((pend_blk - pblk) * bm).astype(I32)
    n_rows = s * TOP_K + N_EXPERTS * bm
    n_blk = n_rows // bm
    block_e = jnp.minimum(jnp.sum(pend_blk[None, :] <= jnp.arange(n_blk)[:, None], axis=1),
                          N_EXPERTS - 1).astype(I32)
    n_used = pend_blk[-1:].astype(I32)

    dest_flat = _dest(idx4, rank4, pstart).reshape(-1)
    xs = _dispatch(pstart, pblk.astype(I32), n_used, dest_flat, h2, n_rows)
    y = _experts(block_e, n_used, xs, w_gate_up, b_gate_up, w_down, b_down)
    return _combine(dest_flat, y, gate4, x1, gt_f, g_post_ffn.reshape(1, d))


def kernel(x, c, w_ada, b_ada, g_pre_mix, g_post_mix, w_in, b_forget, rel_bias, w_out, g_pre_ffn, g_post_ffn, w_router, b_router, w_gate_up, b_gate_up, w_down, b_down):
    bsz, s, d = x.shape
    depth = w_ada.shape[0]
    outs = []
    for bi in range(bsz):
        x2 = x[bi]
        for l in range(depth):
            mod = _adaln(c[bi:bi + 1], w_ada[l], b_ada[l])
            x2 = _layer(x2, mod, g_pre_mix[l], g_post_mix[l], w_in[l], b_forget[l], rel_bias, w_out[l],
                        g_pre_ffn[l], g_post_ffn[l], w_router[l], b_router[l], w_gate_up[l], b_gate_up[l],
                        w_down[l], b_down[l])
        outs.append(x2)
    return jnp.stack(outs)
```

```python
import functools
import math

import numpy as np
import jax
import jax.numpy as jnp
from jax import lax
from jax.experimental import pallas as pl
from jax.experimental.pallas import tpu as pltpu

F32 = jnp.float32
BF16 = jnp.bfloat16
I32 = jnp.int32

HEAD_DIM = 64
N_HEADS_FOX = 8
N_HEADS_MOBA = 8
PAIR = 2 * HEAD_DIM
MOBA_BLOCK = 256
MOBA_TOPK = 3
NUM_BUCKETS = 32
MAX_DISTANCE = 128
N_EXPERTS = 32
TOP_K = 4
SWIGLU_LIMIT = 7.0
SWIGLU_ALPHA = 1.702
RMS_EPS = 1e-6
NEG = -(2.0 ** 100)
M_INIT = -(2.0 ** 99)
LOG2E = math.log2(math.e)
SUM_ROWS = 16
SUBLANES = 8
LANES = 128
EXP_UNDERFLOW = 90.0
VMEM_LIMIT = 56 * 1024 * 1024

ROW_TILE = 512
FOX_TILE = 256
EXPERT_BLOCK = 512
COMBINE_TILE = 256

NT_DIMS = (((1,), (1,)), ((), ()))


def _cparams(sem):
    return pltpu.CompilerParams(dimension_semantics=sem, vmem_limit_bytes=VMEM_LIMIT)


def _rms(x):
    return x * lax.rsqrt(jnp.mean(x * x, axis=-1, keepdims=True) + RMS_EPS)


def _adaln_kernel(c_ref, w_ref, b_ref, o_ref):
    c = c_ref[...]
    cond = c * jax.nn.sigmoid(c)
    o_ref[...] = jnp.dot(cond, w_ref[...], preferred_element_type=F32,
                         precision=lax.Precision.HIGHEST) + b_ref[...]


def _adaln(c, w_ada, b_ada):
    d = c.shape[-1]
    n = w_ada.shape[-1]
    c8 = jnp.broadcast_to(c.reshape(1, d), (8, d))
    out = pl.pallas_call(
        _adaln_kernel,
        out_shape=jax.ShapeDtypeStruct((8, n), F32),
        grid=(n // d,),
        in_specs=[pl.BlockSpec((8, d), lambda j: (0, 0)),
                  pl.BlockSpec((d, d), lambda j: (0, j)),
                  pl.BlockSpec((1, d), lambda j: (0, j))],
        out_specs=pl.BlockSpec((8, d), lambda j: (0, j)),
        compiler_params=_cparams(("arbitrary",)),
        name="adaln",
    )(c8, w_ada, b_ada.reshape(1, n))
    return out[0:1]


def _inproj_kernel(x_ref, g_ref, sc_ref, sh_ref, w_ref, wf_ref, bf_ref,
                   qkv_ref, cum_ref, sel_ref, nrm_ref, tr_ref, nr_ref, km_scr, carry_scr):
    i = pl.program_id(0)
    tm = x_ref.shape[0]
    nblk = km_scr.shape[0]

    @pl.when(i == 0)
    def _():
        km_scr[...] = jnp.zeros_like(km_scr)
        carry_scr[...] = jnp.zeros_like(carry_scr)

    x = x_ref[...]
    h = _rms(x) * g_ref[...] * (1.0 + sc_ref[...]) + sh_ref[...]
    hb = h.astype(BF16)

    width = N_HEADS_FOX * HEAD_DIM
    hsel = jnp.where(lax.broadcasted_iota(I32, (width, N_HEADS_FOX), 0) // HEAD_DIM
                     == lax.broadcasted_iota(I32, (width, N_HEADS_FOX), 1), 1.0, 0.0)
    kb = None
    tr_slot = {3: 0, 5: 1, 0: 2, 2: 3}
    qbt = None
    for c in range(6):
        pc = jnp.dot(hb, w_ref[:, c * width:(c + 1) * width], preferred_element_type=F32)
        if c in tr_slot:
            n = tr_slot[c]
            pct = pc.T
            if c == 3:
                qbt = pct
            if c == 0 or c == 3:
                pct = pct * (LOG2E * HEAD_DIM ** -0.5)
            tr_ref[n * width:(n + 1) * width, :] = pct.astype(BF16)
        if c == 1 or c == 4:
            qkv_ref[:, (c // 3) * width:(c // 3 + 1) * width] = pc.astype(BF16)
        if c < 2:
            sq = (pc * (HEAD_DIM ** -0.5) if c == 0 else pc).astype(BF16).astype(F32)
            n2 = jnp.dot((sq * sq).astype(BF16), hsel.astype(BF16), preferred_element_type=F32)
            nrm_ref[0, c:c + 1, :] = jnp.max(n2, axis=0, keepdims=True)
        if c == 4:
            kb = pc

    ft = lax.dot_general(wf_ref[...].astype(BF16), hb, NT_DIMS, preferred_element_type=F32)
    z = ft + bf_ref[...]
    logf = -(jnp.maximum(-z, 0.0) + jnp.log1p(jnp.exp(-jnp.abs(z))))
    lane = lax.broadcasted_iota(I32, logf.shape, 1)
    cs = logf
    sh = 1
    while sh < tm:
        cs = cs + jnp.where(lane >= sh, pltpu.roll(cs, sh, axis=1), 0.0)
        sh *= 2
    base = jnp.zeros_like(cs)
    for b in range(1, tm // FOX_TILE):
        base = jnp.where(lane >= b * FOX_TILE, cs[:, b * FOX_TILE - 1:b * FOX_TILE], base)
    nr_ref[...] = (-LOG2E * (cs - base)).T
    cs = cs + carry_scr[...]
    cum_ref[...] = cs
    carry_scr[...] = cs[:, tm - 1:tm]

    nb_tile = tm // MOBA_BLOCK
    for b in range(nb_tile):
        kmean = jnp.sum(kb[b * MOBA_BLOCK:(b + 1) * MOBA_BLOCK], axis=0, keepdims=True) * (1.0 / MOBA_BLOCK)
        km_scr[pl.ds(i * nb_tile + b, 1), :] = kmean

    km = km_scr[...]
    blk = lax.broadcasted_iota(I32, (nblk, tm), 0)
    col = lax.broadcasted_iota(I32, (nblk, tm), 1)
    own = i * nb_tile + col // MOBA_BLOCK
    for hd in range(N_HEADS_MOBA):
        hs = slice(hd * HEAD_DIM, (hd + 1) * HEAD_DIM)
        g = jnp.dot(km[:, hs], qbt[hs, :], preferred_element_type=F32,
                    precision=lax.Precision.HIGHEST)
        g = jnp.where(blk < own, g, -jnp.inf)
        sel = jnp.zeros(g.shape, dtype=jnp.bool_)
        for _ in range(MOBA_TOPK):
            m = jnp.max(g, axis=0, keepdims=True)
            first = jnp.min(jnp.where(g == m, blk, nblk), axis=0, keepdims=True)
            pick = (blk == first) & (m > -jnp.inf)
            sel = sel | pick
            g = jnp.where(pick, -jnp.inf, g)
        sel_ref[hd] = jnp.where(sel, 0.0, NEG)


def _inproj(x2, g, sc, sh, w_qkv, w_f, b_f):
    s, d = x2.shape
    tm = ROW_TILE
    nblk = s // MOBA_BLOCK
    n = w_qkv.shape[1]
    width = N_HEADS_FOX * HEAD_DIM
    n_rows_out = 2 * width
    n_tr = 4 * width
    assert tm % FOX_TILE == 0 and N_HEADS_FOX == N_HEADS_MOBA
    return pl.pallas_call(
        _inproj_kernel,
        out_shape=(jax.ShapeDtypeStruct((s, n_rows_out), BF16),
                   jax.ShapeDtypeStruct((N_HEADS_FOX, s), F32),
                   jax.ShapeDtypeStruct((N_HEADS_MOBA, nblk, s), F32),
                   jax.ShapeDtypeStruct((s // tm, 2, N_HEADS_FOX), F32),
                   jax.ShapeDtypeStruct((n_tr, s), BF16),
                   jax.ShapeDtypeStruct((s, N_HEADS_FOX), F32)),
        grid=(s // tm,),
        in_specs=[pl.BlockSpec((tm, d), lambda i: (i, 0)),
                  pl.BlockSpec((1, d), lambda i: (0, 0)),
                  pl.BlockSpec((1, d), lambda i: (0, 0)),
                  pl.BlockSpec((1, d), lambda i: (0, 0)),
                  pl.BlockSpec((d, n), lambda i: (0, 0)),
                  pl.BlockSpec((N_HEADS_FOX, d), lambda i: (0, 0)),
                  pl.BlockSpec((N_HEADS_FOX, 1), lambda i: (0, 0))],
        out_specs=(pl.BlockSpec((tm, n_rows_out), lambda i: (i, 0)),
                   pl.BlockSpec((N_HEADS_FOX, tm), lambda i: (0, i)),
                   pl.BlockSpec((N_HEADS_MOBA, nblk, tm), lambda i: (0, 0, i)),
                   pl.BlockSpec((1, 2, N_HEADS_FOX), lambda i: (i, 0, 0)),
                   pl.BlockSpec((n_tr, tm), lambda i: (0, i)),
                   pl.BlockSpec((tm, N_HEADS_FOX), lambda i: (i, 0))),
        scratch_shapes=[pltpu.VMEM((nblk, N_HEADS_MOBA * HEAD_DIM), F32),
                        pltpu.VMEM((N_HEADS_FOX, 1), F32)],
        compiler_params=_cparams(("arbitrary",)),
        name="inproj",
    )(x2, g, sc, sh, w_qkv, w_f, b_f)


def _fox_kernel(jlo_ref, cp_ref, qt_ref, k_ref, vt_ref, nr_ref, o_ref, m_ref, a_ref):
    p = pl.program_id(0)
    i = pl.program_id(1)
    j_first = jlo_ref[p * pl.num_programs(1) + i]
    t = qt_ref.shape[1]

    qt = qt_ref[...]
    top = lax.broadcasted_iota(I32, qt.shape, 0) < HEAD_DIM
    zq = jnp.zeros_like(qt)
    qth = (jnp.where(top, qt, zq), jnp.where(top, zq, qt))
    m_ref[...] = jnp.full(m_ref.shape, M_INIT, F32)
    a_ref[...] = jnp.zeros(a_ref.shape, F32)
    er = lax.broadcasted_iota(I32, (SUM_ROWS, 2 * t), 0)
    ec = lax.broadcasted_iota(I32, (SUM_ROWS, 2 * t), 1)
    ones_rows = jnp.where(((er == 0) & (ec < t)) | ((er == 1) & (ec >= t)), 1.0, 0.0).astype(BF16)
    arow = lax.broadcasted_iota(I32, a_ref.shape, 0)
    head0_rows = (arow < HEAD_DIM) | (arow == PAIR)

    def tile(j, diagonal):
        k0 = pl.multiple_of(j * t, t)
        kt = k_ref[pl.ds(k0, t), :]
        vt = vt_ref[:, pl.ds(k0, t)]
        ps, alphas = [], []
        for hd in range(2):
            h = 2 * p + hd
            s = jnp.dot(kt, qth[hd], preferred_element_type=F32) + nr_ref[0, pl.ds(k0, t), hd:hd + 1]
            if diagonal:
                r = lax.broadcasted_iota(I32, s.shape, 0)
                c = lax.broadcasted_iota(I32, s.shape, 1)
                s = jnp.where(r <= c, s, NEG)
            pair_shift = LOG2E * (cp_ref[h, i] - cp_ref[h, j])
            m_old = m_ref[hd]
            m_new = jnp.maximum(m_old, jnp.max(s, axis=0, keepdims=True) + pair_shift)
            alphas.append(jnp.exp2(m_old - m_new))
            m_ref[hd] = m_new
            ps.append(jnp.exp2(s - (m_new - pair_shift)).astype(BF16))
        zv = jnp.zeros_like(vt)
        vcat = jnp.concatenate([jnp.where(top, vt, zv), jnp.where(top, zv, vt)], axis=1)
        vcat = jnp.concatenate([vcat, ones_rows], axis=0)
        pcat = jnp.concatenate(ps, axis=0)
        a_ref[...] = (a_ref[...] * jnp.where(head0_rows, alphas[0], alphas[1])
                      + jnp.dot(vcat, pcat, preferred_element_type=F32))

    def body(j, carry):
        tile(j, False)
        return carry

    lax.fori_loop(j_first, i, body, 0)
    tile(i, True)
    out_t = a_ref[0:PAIR, :] / jnp.where(top, a_ref[PAIR:PAIR + 1, :], a_ref[PAIR + 1:PAIR + 2, :])
    o_ref[...] = out_t.T.astype(o_ref.dtype)


def _fox_first_tile(cum, nrm, t):
    cend = cum[:, t - 1::t]
    nt = cend.shape[1]
    cprev = jnp.concatenate([jnp.zeros((cend.shape[0], 1), F32), cend[:, :-1]], axis=1)
    rep = nt // nrm.shape[0]
    qn = jnp.repeat(jnp.sqrt(nrm[:, 0, :]).T, rep, axis=1)
    kn = jnp.repeat(jnp.sqrt(nrm[:, 1, :]).T, rep, axis=1)
    gap = (1.02 * qn[:, :, None] * (kn[:, None, :] + kn[:, :, None])
           + cprev[:, :, None] - cend[:, None, :])
    jj = jnp.arange(nt)[None, None, :]
    ii = jnp.arange(nt)[None, :, None]
    needed = (jj < ii) & jnp.logical_not(gap <= -EXP_UNDERFLOW)
    needed = needed[0::2] | needed[1::2]
    first = jnp.min(jnp.where(needed, jj, ii), axis=2)
    return first.reshape(-1).astype(I32)


def _fox(qkv, tr, nr, cum, nrm):
    s = qkv.shape[0]
    t = FOX_TILE
    npair = N_HEADS_FOX // 2
    jlo = _fox_first_tile(cum, nrm, t)
    cend = cum[:, t - 1::t]
    cprev = jnp.concatenate([jnp.zeros((cend.shape[0], 1), F32), cend[:, :-1]], axis=1)
    q_rows = 2 * (N_HEADS_MOBA // 2)
    grid_spec = pltpu.PrefetchScalarGridSpec(
        num_scalar_prefetch=1,
        grid=(npair, s // t),
        in_specs=[pl.BlockSpec(memory_space=pltpu.SMEM),
                  pl.BlockSpec((PAIR, t), lambda p, i, jl: (q_rows + p, i)),
                  pl.BlockSpec((s, PAIR), lambda p, i, jl: (0, p)),
                  pl.BlockSpec((PAIR, s), lambda p, i, jl: (q_rows + npair + p, 0)),
                  pl.BlockSpec((1, s, 2), lambda p, i, jl: (p, 0, 0))],
        out_specs=pl.BlockSpec((t, PAIR), lambda p, i, jl: (i, p)),
        scratch_shapes=[pltpu.VMEM((2, 1, t), F32), pltpu.VMEM((PAIR + SUM_ROWS, t), F32)],
    )
    nr_pairs = nr.reshape(s, npair, 2).transpose(1, 0, 2)
    return pl.pallas_call(
        _fox_kernel,
        out_shape=jax.ShapeDtypeStruct((s, npair * PAIR), BF16),
        grid_spec=grid_spec,
        compiler_params=_cparams(("arbitrary", "arbitrary")),
        name="fox",
    )(jlo, cprev, tr, qkv, tr, nr_pairs)


def _t5_bucket_np(dist):
    dist = np.maximum(dist, 0)
    max_exact = NUM_BUCKETS // 2
    d = np.maximum(dist, 1).astype(np.float32)
    large = max_exact + (np.log(d / np.float32(max_exact)) / np.float32(math.log(MAX_DISTANCE / max_exact))
                         * np.float32(NUM_BUCKETS - max_exact)).astype(np.int32)
    large = np.minimum(large, NUM_BUCKETS - 1)
    return np.where(dist < max_exact, dist, large).astype(np.int32)


def _moba_kernel(rb_ref, qt_ref, k_ref, vt_ref, sel_ref, bkt_ref, o_ref,
                 m_ref, a_ref, bias_scr, sa_ref, sb_ref, pa_ref, pb_ref, ala_ref, alb_ref):
    p = pl.program_id(0)
    g = pl.program_id(1)
    tq = qt_ref.shape[1]
    t = tq // 2

    @pl.when(g == 0)
    def _():
        r = lax.broadcasted_iota(I32, (t, t), 0)
        c = lax.broadcasted_iota(I32, (t, t), 1)
        zero = jnp.zeros((t, t), F32)
        for hd in range(2):
            h = 2 * p + hd
            far = rb_ref[(NUM_BUCKETS - 1) * N_HEADS_MOBA + h]
            tiles = []
            for w in range(2):
                bkt = bkt_ref[w]
                acc = jnp.zeros(bkt.shape, F32)
                for kk in range(NUM_BUCKETS):
                    acc = acc + jnp.where(bkt == kk, rb_ref[kk * N_HEADS_MOBA + h], 0.0)
                tiles.append((acc - far) * LOG2E)
            prev_t = tiles[0]
            own_t = jnp.where(r <= c, tiles[1], NEG)
            bias_scr[hd, 0] = jnp.concatenate([prev_t, zero], axis=1)
            bias_scr[hd, 1] = jnp.concatenate([own_t, prev_t], axis=1)
            bias_scr[hd, 2] = jnp.concatenate([zero, own_t], axis=1)

    qt = qt_ref[...]
    top = lax.broadcasted_iota(I32, qt.shape, 0) < HEAD_DIM
    zq = jnp.zeros_like(qt)
    qth = (jnp.where(top, qt, zq), jnp.where(top, zq, qt))
    m_ref[...] = jnp.full(m_ref.shape, M_INIT, F32)
    a_ref[...] = jnp.zeros(a_ref.shape, F32)

    er = lax.broadcasted_iota(I32, (SUM_ROWS, 2 * t), 0)
    ec = lax.broadcasted_iota(I32, (SUM_ROWS, 2 * t), 1)
    ones_rows = jnp.where(((er == 0) & (ec < t)) | ((er == 1) & (ec >= t)), 1.0, 0.0).astype(BF16)
    arow = lax.broadcasted_iota(I32, a_ref.shape, 0)
    head0_rows = (arow < HEAD_DIM) | (arow == PAIR)
    vtop = lax.broadcasted_iota(I32, (PAIR, t), 0) < HEAD_DIM
    in_a = lax.broadcasted_iota(I32, (1, tq), 1) < t

    n_far = jnp.maximum(2 * g - 1, 0)

    def produce(j, s_buf):
        k0 = pl.multiple_of(j * t, t)
        kt = k_ref[pl.ds(k0, t), :]
        for hd in range(2):
            s_buf[hd] = jnp.dot(kt, qth[hd], preferred_element_type=F32)

    def softmax(s_buf, p_buf, al_buf, selrows, w):
        for hd in range(2):
            s = s_buf[hd]
            if w is not None:
                s = s + bias_scr[hd, w]
            smax = jnp.max(s, axis=0, keepdims=True)
            m_old = m_ref[hd]
            m_new = jnp.maximum(m_old, smax + selrows[hd])
            shift = m_new - selrows[hd]
            al_buf[hd] = jnp.exp2(m_old - m_new)
            m_ref[hd] = m_new
            p_buf[hd * t:(hd + 1) * t, :] = jnp.exp2(s - shift).astype(BF16)

    def far_rows(j):
        return [jnp.where(j < n_far, sel_ref[hd, pl.ds(j, 1), :], NEG) for hd in range(2)]

    def accumulate(j, p_buf, al_buf):
        k0 = pl.multiple_of(j * t, t)
        vt = vt_ref[:, pl.ds(k0, t)]
        zv = jnp.zeros_like(vt)
        vcat = jnp.concatenate([jnp.where(vtop, vt, zv), jnp.where(vtop, zv, vt)], axis=1)
        vcat = jnp.concatenate([vcat, ones_rows], axis=0)
        a_ref[...] = (a_ref[...] * jnp.where(head0_rows, al_buf[0], al_buf[1])
                      + jnp.dot(vcat, p_buf[...], preferred_element_type=F32))

    pb_ref[...] = jnp.zeros_like(pb_ref)
    alb_ref[...] = jnp.ones_like(alb_ref)
    produce(0, sa_ref)

    def body(u, carry):
        j0 = 2 * u
        produce(j0 + 1, sb_ref)
        softmax(sa_ref, pa_ref, ala_ref, far_rows(j0), None)
        accumulate(jnp.maximum(j0 - 1, 0), pb_ref, alb_ref)
        produce(j0 + 2, sa_ref)
        accumulate(j0, pa_ref, ala_ref)
        softmax(sb_ref, pb_ref, alb_ref, far_rows(j0 + 1), None)
        return carry

    n_pairs = (n_far + 1) // 2
    lax.fori_loop(0, n_pairs, body, 0)

    j1 = jnp.maximum(2 * g - 1, 0)
    j2 = 2 * g
    j3 = 2 * g + 1
    rows1 = [jnp.where(g >= 1, sel_ref[hd, pl.ds(j1, 1), :], NEG) for hd in range(2)]
    rows2 = [jnp.where(in_a, 0.0, sel_ref[hd, pl.ds(j2, 1), :]) for hd in range(2)]
    rows3 = [jnp.where(in_a, NEG, 0.0)] * 2
    produce(j1, sa_ref)
    produce(j2, sb_ref)
    softmax(sa_ref, pa_ref, ala_ref, rows1, 0)
    accumulate(jnp.maximum(2 * n_pairs - 1, 0), pb_ref, alb_ref)
    produce(j3, sa_ref)
    softmax(sb_ref, pb_ref, alb_ref, rows2, 1)
    accumulate(j1, pa_ref, ala_ref)
    softmax(sa_ref, pa_ref, ala_ref, rows3, 2)
    accumulate(j2, pb_ref, alb_ref)
    accumulate(j3, pa_ref, ala_ref)
    out_t = a_ref[0:PAIR, :] / jnp.where(top, a_ref[PAIR:PAIR + 1, :], a_ref[PAIR + 1:PAIR + 2, :])
    o_ref[...] = out_t.T.astype(o_ref.dtype)


def _moba(qkv, tr, sel, rel_bias):
    s = qkv.shape[0]
    t = MOBA_BLOCK
    npair = N_HEADS_MOBA // 2
    nblk = s // t
    kcol = N_HEADS_FOX // 2
    a = np.arange(t)[None, :]
    b = np.arange(t)[:, None]
    bkt = jnp.asarray(np.stack([_t5_bucket_np(t + a - b), _t5_bucket_np(a - b)]))
    tq = 2 * t
    grid_spec = pltpu.PrefetchScalarGridSpec(
        num_scalar_prefetch=1,
        grid=(npair, s // tq),
        in_specs=[pl.BlockSpec((PAIR, tq), lambda p, i, rb: (p, i)),
                  pl.BlockSpec((s, PAIR), lambda p, i, rb: (0, kcol + p)),
                  pl.BlockSpec((PAIR, s), lambda p, i, rb: (npair + p, 0)),
                  pl.BlockSpec((2, nblk, tq), lambda p, i, rb: (p, 0, i)),
                  pl.BlockSpec((2, t, t), lambda p, i, rb: (0, 0, 0))],
        out_specs=pl.BlockSpec((tq, PAIR), lambda p, i, rb: (i, p)),
        scratch_shapes=[pltpu.VMEM((2, 1, tq), F32),
                        pltpu.VMEM((PAIR + SUM_ROWS, tq), F32), pltpu.VMEM((2, 3, t, tq), F32),
                        pltpu.VMEM((2, t, tq), F32), pltpu.VMEM((2, t, tq), F32),
                        pltpu.VMEM((2 * t, tq), BF16), pltpu.VMEM((2 * t, tq), BF16),
                        pltpu.VMEM((2, 1, tq), F32), pltpu.VMEM((2, 1, tq), F32)],
    )
    return pl.pallas_call(
        _moba_kernel,
        out_shape=jax.ShapeDtypeStruct((s, npair * PAIR), BF16),
        grid_spec=grid_spec,
        compiler_params=_cparams(("arbitrary", "arbitrary")),
        name="moba",
    )(rel_bias.reshape(-1), tr, qkv, tr, sel, bkt)


def _store_token_tiles(ref, val):
    n = val.shape[0]
    for c in range(SUBLANES):
        ref[pl.ds(c, n, stride=SUBLANES), :] = val[:, c * LANES:(c + 1) * LANES]


def _load_token_tiles(ref, n):
    return jnp.concatenate([ref[pl.ds(c, n, stride=SUBLANES), :] for c in range(SUBLANES)], axis=1)


def _tile_rows(r0, n):
    start = r0 * SUBLANES
    if not isinstance(start, int):
        start = pl.multiple_of(start, SUBLANES)
    return pl.ds(start, n * SUBLANES)


def _outproj_kernel(ya_ref, yb_ref, x_ref, wa_ref, wb_ref, gpost_ref, gt_ref, gpre_ref,
                    sc_ref, sh_ref, wr_ref, br_ref,
                    x1_ref, h2_ref, idx_ref, rank_ref, gate_ref, cnt_ref, carry_scr):
    i = pl.program_id(0)
    tm = x_ref.shape[0]

    @pl.when(i == 0)
    def _():
        carry_scr[...] = jnp.zeros_like(carry_scr)

    y = (jnp.dot(ya_ref[...], wa_ref[...], preferred_element_type=F32)
         + jnp.dot(yb_ref[...], wb_ref[...], preferred_element_type=F32))
    x1 = x_ref[...] + gt_ref[...] * (_rms(y) * gpost_ref[...])
    x1_ref[...] = x1
    h2 = _rms(x1) * gpre_ref[...] * (1.0 + sc_ref[...]) + sh_ref[...]
    _store_token_tiles(h2_ref, h2)

    logits =jnp.dot(h2, wr_ref[...], preferred_element_type=F32,
                     precision=lax.Precision.HIGHEST) + br_ref[...]
    ne = logits.shape[1]
    lane = lax.broadcasted_iota(I32, logits.shape, 1)
    lane4 = lax.broadcasted_iota(I32, (tm, TOP_K), 1)
    g = logits
    mask = jnp.zeros(logits.shape, F32)
    vals, picks = [], []
    for _ in range(TOP_K):
        m = jnp.max(g, axis=1, keepdims=True)
        first = jnp.min(jnp.where(g == m, lane, ne), axis=1, keepdims=True)
        pick = lane == first
        mask = jnp.where(pick, 1.0, mask)
        g = jnp.where(pick, -jnp.inf, g)
        vals.append(m)
        picks.append(pick)
    ex = [jnp.exp(v - vals[0]) for v in vals]
    den = ex[0] + ex[1] + ex[2] + ex[3]
    gates = [e / den for e in ex]

    r = lax.broadcasted_iota(I32, (tm, tm), 0)
    c = lax.broadcasted_iota(I32, (tm, tm), 1)
    tril = jnp.where(c < r, 1.0, 0.0).astype(BF16)
    before = jnp.dot(tril, mask.astype(BF16), preferred_element_type=F32) + carry_scr[...]
    total = carry_scr[...] + jnp.sum(mask, axis=0, keepdims=True)
    carry_scr[...] = total
    cnt_ref[...] = jnp.broadcast_to(total, cnt_ref.shape)

    def pack4(cols):
        return jnp.where(lane4 == 0, cols[0],
                         jnp.where(lane4 == 1, cols[1], jnp.where(lane4 == 2, cols[2], cols[3])))

    ranks = [jnp.sum(jnp.where(pk, before, 0.0), axis=1, keepdims=True) for pk in picks]
    idxs = [jnp.sum(jnp.where(pk, lane, 0), axis=1, keepdims=True) for pk in picks]
    idx_ref[...] = pack4(idxs)
    rank_ref[...] = pack4(ranks).astype(I32)
    gate_ref[...] = pack4(gates)


def _outproj(mix_a, mix_b, x2, w_a, w_b, gpost, gt, gpre, sc, sh, w_router, b_router):
    s, d = x2.shape
    tm = ROW_TILE
    ne = w_router.shape[1]
    wa = mix_a.shape[1]
    row = lambda i: (i, 0)
    fix = lambda i: (0, 0)
    vec = pl.BlockSpec((1, d), fix)
    return pl.pallas_call(
        _outproj_kernel,
        out_shape=(jax.ShapeDtypeStruct((s, d), F32),
                   jax.ShapeDtypeStruct((s * SUBLANES, LANES), F32),
                   jax.ShapeDtypeStruct((s, TOP_K), I32),
                   jax.ShapeDtypeStruct((s, TOP_K), I32),
                   jax.ShapeDtypeStruct((s, TOP_K), F32),
                   jax.ShapeDtypeStruct((8, ne), F32)),
        grid=(s // tm,),
        in_specs=[pl.BlockSpec((tm, wa), row), pl.BlockSpec((tm, wa), row), pl.BlockSpec((tm, d), row),
                  pl.BlockSpec((wa, d), fix), pl.BlockSpec((wa, d), fix),
                  vec, vec, vec, vec, vec,
                  pl.BlockSpec((d, ne), fix), pl.BlockSpec((1, ne), fix)],
        out_specs=(pl.BlockSpec((tm, d), row), pl.BlockSpec((tm * SUBLANES, LANES), row),
                   pl.BlockSpec((tm, TOP_K), row), pl.BlockSpec((tm, TOP_K), row),
                   pl.BlockSpec((tm, TOP_K), row), pl.BlockSpec((8, ne), fix)),
        scratch_shapes=[pltpu.VMEM((1, ne), F32)],
        compiler_params=_cparams(("arbitrary",)),
        name="outproj",
    )(mix_a, mix_b, x2, w_a, w_b, gpost, gt, gpre, sc, sh, w_router, b_router)


def _dest_kernel(idx_ref, rank_ref, pstart_ref, o_ref):
    idx = idx_ref[...]
    tm = idx.shape[0]
    ne = pstart_ref.shape[1]
    lane = lax.broadcasted_iota(I32, (tm, ne), 1)
    lane4 = lax.broadcasted_iota(I32, idx.shape, 1)
    out = rank_ref[...]
    for k in range(TOP_K):
        start = jnp.sum(jnp.where(lane == idx[:, k:k + 1], pstart_ref[...], 0), axis=1, keepdims=True)
        out = out + jnp.where(lane4 == k, start, 0)
    o_ref[...] = out


def _dest(idx4, rank4, pstart):
    s = idx4.shape[0]
    tm = min(4 * ROW_TILE, s)
    row = lambda i: (i, 0)
    return pl.pallas_call(
        _dest_kernel,
        out_shape=jax.ShapeDtypeStruct((s, TOP_K), I32),
        grid=(s // tm,),
        in_specs=[pl.BlockSpec((tm, TOP_K), row), pl.BlockSpec((tm, TOP_K), row),
                  pl.BlockSpec((1, N_EXPERTS), lambda i: (0, 0))],
        out_specs=pl.BlockSpec((tm, TOP_K), row),
        compiler_params=_cparams(("arbitrary",)),
        name="dest",
    )(idx4, rank4, pstart.reshape(1, -1))


def _dispatch_kernel(pstart_ref, pblk_ref, nu_ref, dest_ref, h_ref, xs_ref,
                     zero_scr, sem, zsem):
    tm = h_ref.shape[0] // SUBLANES
    bm = zero_scr.shape[0] // SUBLANES
    n_blk = xs_ref.shape[0] // (bm * SUBLANES)

    rows = _tile_rows

    @pl.when(pl.program_id(0) == 0)
    def _():
        zero_scr[...] = jnp.zeros_like(zero_scr)

        def zero_copy(row0):
            return pltpu.make_async_copy(zero_scr, xs_ref.at[rows(row0, bm)], zsem)

        for phase in range(2):
            for e in range(N_EXPERTS):
                last = pstart_ref[e] + (pblk_ref[e] - 1) * bm
                tail = (n_blk - N_EXPERTS + e) * bm
                for cond, row0 in ((pblk_ref[e] > 0, last), (n_blk - N_EXPERTS + e >= nu_ref[0], tail)):
                    @pl.when(cond)
                    def _():
                        if phase == 0:
                            zero_copy(row0).start()
                        else:
                            zero_copy(row0).wait()

    def row_copy(r, k):
        dst = dest_ref[r * TOP_K + k]
        return pltpu.make_async_copy(h_ref.at[rows(r, 1)], xs_ref.at[rows(dst, 1)], sem)

    def start(r, carry):
        for k in range(TOP_K):
            row_copy(r, k).start(priority=k % 2)
        return carry

    lax.fori_loop(0, tm, start, 0, unroll=4)
    for k in range(TOP_K):
        pltpu.make_async_copy(h_ref, xs_ref.at[rows(0, tm)], sem).wait()


def _dispatch(pstart, pblk, n_used, dest_flat, h2, n_rows):
    s = h2.shape[0] // SUBLANES
    tm = COMBINE_TILE
    grid_spec = pltpu.PrefetchScalarGridSpec(
        num_scalar_prefetch=3,
        grid=(s // tm,),
        in_specs=[pl.BlockSpec((tm * TOP_K,), lambda i, *_: (i,), memory_space=pltpu.SMEM),
                  pl.BlockSpec((tm * SUBLANES, LANES), lambda i, *_: (i, 0))],
        out_specs=pl.BlockSpec(memory_space=pl.ANY),
        scratch_shapes=[pltpu.VMEM((EXPERT_BLOCK * SUBLANES, LANES), F32), pltpu.SemaphoreType.DMA,
                        pltpu.SemaphoreType.DMA],
    )
    return pl.pallas_call(
        _dispatch_kernel,
        out_shape=jax.ShapeDtypeStruct((n_rows * SUBLANES, LANES), F32),
        grid_spec=grid_spec,
        compiler_params=_cparams(("arbitrary",)),
        name="dispatch",
    )(pstart, pblk, n_used, dest_flat, h2)


def _experts_kernel(be_ref, nu_ref, xs_ref, wgu_ref, bgu_ref, wd_ref, bd_ref, y_ref,
                    wgu_bf, wd_bf):
    b = pl.program_id(0)
    d_exp = wd_ref.shape[1]
    prev = be_ref[jnp.maximum(b - 1, 0)]
    changed = (b == 0) | (be_ref[b] != prev)

    @pl.when((b < nu_ref[0]) & changed)
    def _():
        rows = 128

        def cast_gu(c, carry):
            r0 = pl.multiple_of(c * rows, rows)
            wgu_bf[pl.ds(r0, rows), :] = wgu_ref[0, pl.ds(r0, rows), :].astype(BF16)
            return carry

        def cast_d(c, carry):
            r0 = pl.multiple_of(c * rows, rows)
            wd_bf[pl.ds(r0, rows), :] = wd_ref[0, pl.ds(r0, rows), :].astype(BF16)
            return carry

        lax.fori_loop(0, wgu_ref.shape[1] // rows, cast_gu, 0)
        lax.fori_loop(0, wd_ref.shape[1] // rows, cast_d, 0)

    @pl.when(b < nu_ref[0])
    def _():
        bm = xs_ref.shape[0] // SUBLANES
        xb = _load_token_tiles(xs_ref, bm).astype(BF16)
        hdn = jnp.dot(xb, wgu_bf[...], preferred_element_type=F32) + bgu_ref[0]
        x_glu = jnp.minimum(hdn[:, :d_exp], SWIGLU_LIMIT)
        x_lin = jnp.clip(hdn[:, d_exp:], -SWIGLU_LIMIT, SWIGLU_LIMIT)
        act = x_glu * jax.nn.sigmoid(SWIGLU_ALPHA * x_glu) * (x_lin + 1.0)
        _store_token_tiles(y_ref, jnp.dot(act.astype(BF16), wd_bf[...], preferred_element_type=F32)
                           + bd_ref[0])

    @pl.when(b >= nu_ref[0])
    def _():
        y_ref[...] = jnp.zeros_like(y_ref)


def _experts(block_e, n_used, xs, w_gate_up, b_gate_up, w_down, b_down):
    n_rows = xs.shape[0] // SUBLANES
    bm = EXPERT_BLOCK
    n_blk = n_rows // bm
    ne, d, two_de = w_gate_up.shape
    de = w_down.shape[1]
    assert d == SUBLANES * LANES

    def blk(b, be, nu):
        return jnp.minimum(b, nu[0] - 1)

    grid_spec = pltpu.PrefetchScalarGridSpec(
        num_scalar_prefetch=2,
        grid=(n_blk,),
        in_specs=[pl.BlockSpec((bm * SUBLANES, LANES), lambda b, be, nu: (blk(b, be, nu), 0)),
                  pl.BlockSpec((1, d, two_de), lambda b, be, nu: (be[blk(b, be, nu)], 0, 0)),
                  pl.BlockSpec((1, 1, two_de), lambda b, be, nu: (be[blk(b, be, nu)], 0, 0)),
                  pl.BlockSpec((1, de, d), lambda b, be, nu: (be[blk(b, be, nu)], 0, 0)),
                  pl.BlockSpec((1, 1, d), lambda b, be, nu: (be[blk(b, be, nu)], 0, 0))],
        out_specs=pl.BlockSpec((bm * SUBLANES, LANES), lambda b, be, nu: (b, 0)),
        scratch_shapes=[pltpu.VMEM((d, two_de), BF16), pltpu.VMEM((de, d), BF16)],
    )
    return pl.pallas_call(
        _experts_kernel,
        out_shape=jax.ShapeDtypeStruct((n_rows * SUBLANES, LANES), F32),
        grid_spec=grid_spec,
        compiler_params=_cparams(("arbitrary",)),
        name="experts",
    )(block_e, n_used, xs, w_gate_up, b_gate_up.reshape(ne, 1, two_de), w_down, b_down.reshape(ne, 1, d))


def _combine_kernel(dest_ref, y_ref, gate_ref, x1_ref, gt_ref, gpost_ref, o_ref, buf, sem):
    tm = x1_ref.shape[0]

    def row_copy(r, k):
        src = dest_ref[r * TOP_K + k]
        return pltpu.make_async_copy(y_ref.at[_tile_rows(src, 1)], buf.at[k, _tile_rows(r, 1)], sem)

    def start(r, carry):
        for k in range(TOP_K):
            row_copy(r, k).start(priority=k % 2)
        return carry

    lax.fori_loop(0, tm, start, 0, unroll=4)
    for k in range(TOP_K):
        pltpu.make_async_copy(y_ref.at[_tile_rows(0, tm)], buf.at[k], sem).wait()

    gate = gate_ref[...]
    acc = gate[:, 0:1] * _load_token_tiles(buf.at[0], tm)
    for k in range(1, TOP_K):
        acc = acc + gate[:, k:k + 1] * _load_token_tiles(buf.at[k], tm)
    o_ref[...] = x1_ref[...] + gt_ref[...] * (_rms(acc) * gpost_ref[...])


def _combine(dest_flat, y, gate4, x1, gt, gpost):
    s, d = x1.shape
    tm = COMBINE_TILE
    return pl.pallas_call(
        _combine_kernel,
        out_shape=jax.ShapeDtypeStruct((s, d), F32),
        grid=(s // tm,),
        in_specs=[pl.BlockSpec((tm * TOP_K,), lambda i: (i,), memory_space=pltpu.SMEM),
                  pl.BlockSpec(memory_space=pl.ANY),
                  pl.BlockSpec((tm, TOP_K), lambda i: (i, 0)),
                  pl.BlockSpec((tm, d), lambda i: (i, 0)),
                  pl.BlockSpec((1, d), lambda i: (0, 0)),
                  pl.BlockSpec((1, d), lambda i: (0, 0))],
        out_specs=pl.BlockSpec((tm, d), lambda i: (i, 0)),
        scratch_shapes=[pltpu.VMEM((TOP_K, tm * SUBLANES, LANES), F32), pltpu.SemaphoreType.DMA],
        compiler_params=_cparams(("arbitrary",)),
        name="combine",
    )(dest_flat, y, gate4, x1, gt, gpost)


def _layer(x2, mod, g_pre_mix, g_post_mix, w_in, b_forget, rel_bias, w_out,
           g_pre_ffn, g_post_ffn, w_router, b_router, w_gate_up, b_gate_up, w_down, b_down):
    s, d = x2.shape
    sh_m, sc_m, gt_m, sh_f, sc_f, gt_f = [mod[:, k * d:(k + 1) * d] for k in range(6)]
    n_qkv = 3 * (N_HEADS_FOX + N_HEADS_MOBA) * HEAD_DIM
    fox_w = N_HEADS_FOX * HEAD_DIM

    w_qkv = w_in[:, :n_qkv].astype(BF16)
    w_f = w_in[:, n_qkv:].T
    qkv, cum, sel, nrm, tr, nr = _inproj(x2, g_pre_mix.reshape(1, d), sc_m, sh_m, w_qkv, w_f,
                                         b_forget.reshape(-1, 1))
    y_a = _fox(qkv, tr, nr, cum, nrm)
    y_b = _moba(qkv, tr, sel, rel_bias)

    w_out_bf = w_out.astype(BF16)
    x1, h2, idx4, rank4, gate4, cnt = _outproj(
        y_a, y_b, x2, w_out_bf[:fox_w], w_out_bf[fox_w:], g_post_mix.reshape(1, d), gt_m,
        g_pre_ffn.reshape(1, d), sc_f, sh_f, w_router, b_router.reshape(1, -1))

    bm = EXPERT_BLOCK
    counts = cnt[0].astype(I32)
    pblk = (counts + bm - 1) // bm
    pend_blk = jnp.cumsum(pblk)
    pstart = ((pend_blk - pblk) * bm).astype(I32)
    n_rows = s * TOP_K + N_EXPERTS * bm
    n_blk = n_rows // bm
    block_e = jnp.minimum(jnp.sum(pend_blk[None, :] <= jnp.arange(n_blk)[:, None], axis=1),
                          N_EXPERTS - 1).astype(I32)
    n_used = pend_blk[-1:].astype(I32)

    dest_flat = _dest(idx4, rank4, pstart).reshape(-1)
    xs = _dispatch(pstart, pblk.astype(I32), n_used, dest_flat, h2, n_rows)
    y = _experts(block_e, n_used, xs, w_gate_up, b_gate_up, w_down, b_down)
    return _combine(dest_flat, y, gate4, x1, gt_f, g_post_ffn.reshape(1, d))


def kernel(x, c, w_ada, b_ada, g_pre_mix, g_post_mix, w_in, b_forget, rel_bias, w_out, g_pre_ffn, g_post_ffn, w_router, b_router, w_gate_up, b_gate_up, w_down, b_down):
    bsz, s, d = x.shape
    depth = w_ada.shape[0]
    outs = []
    for bi in range(bsz):
        x2 = x[bi]
        for l in range(depth):
            mod = _adaln(c[bi:bi + 1], w_ada[l], b_ada[l])
            x2 = _layer(x2, mod, g_pre_mix[l], g_post_mix[l], w_in[l], b_forget[l], rel_bias, w_out[l],
                        g_pre_ffn[l], g_post_ffn[l], w_router[l], b_router[l], w_gate_up[l], b_gate_up[l],
                        w_down[l], b_down[l])
        outs.append(x2)
    return jnp.stack(outs)
```

```python
import functools
import math

import numpy as np
import jax
import jax.numpy as jnp
from jax import lax
from jax.experimental import pallas as pl
from jax.experimental.pallas import tpu as pltpu

F32 = jnp.float32
BF16 = jnp.bfloat16
I32 = jnp.int32

HEAD_DIM = 64
N_HEADS_FOX = 8
N_HEADS_MOBA = 8
PAIR = 2 * HEAD_DIM
MOBA_BLOCK = 256
MOBA_TOPK = 3
NUM_BUCKETS = 32
MAX_DISTANCE = 128
N_EXPERTS = 32
TOP_K = 4
SWIGLU_LIMIT = 7.0
SWIGLU_ALPHA = 1.702
RMS_EPS = 1e-6
NEG = -(2.0 ** 100)
M_INIT = -(2.0 ** 99)
LOG2E = math.log2(math.e)
SUM_ROWS = 16
SUBLANES = 8
LANES = 128
EXP_UNDERFLOW = 90.0
VMEM_LIMIT = 56 * 1024 * 1024

ROW_TILE = 512
FOX_TILE = 256
EXPERT_BLOCK = 512
COMBINE_TILE = 512

NT_DIMS = (((1,), (1,)), ((), ()))


def _cparams(sem):
    return pltpu.CompilerParams(dimension_semantics=sem, vmem_limit_bytes=VMEM_LIMIT)


def _rms(x):
    return x * lax.rsqrt(jnp.mean(x * x, axis=-1, keepdims=True) + RMS_EPS)


def _adaln_kernel(c_ref, w_ref, b_ref, o_ref):
    c = c_ref[...]
    cond = c * jax.nn.sigmoid(c)
    o_ref[...] = jnp.dot(cond, w_ref[...], preferred_element_type=F32,
                         precision=lax.Precision.HIGHEST) + b_ref[...]


def _adaln(c, w_ada, b_ada):
    d = c.shape[-1]
    n = w_ada.shape[-1]
    c8 = jnp.broadcast_to(c.reshape(1, d), (8, d))
    out = pl.pallas_call(
        _adaln_kernel,
        out_shape=jax.ShapeDtypeStruct((8, n), F32),
        grid=(n // d,),
        in_specs=[pl.BlockSpec((8, d), lambda j: (0, 0)),
                  pl.BlockSpec((d, d), lambda j: (0, j)),
                  pl.BlockSpec((1, d), lambda j: (0, j))],
        out_specs=pl.BlockSpec((8, d), lambda j: (0, j)),
        compiler_params=_cparams(("arbitrary",)),
        name="adaln",
    )(c8, w_ada, b_ada.reshape(1, n))
    return out[0:1]


def _inproj_kernel(x_ref, g_ref, sc_ref, sh_ref, w_ref, wf_ref, bf_ref,
                   qkv_ref, cum_ref, sel_ref, nrm_ref, tr_ref, nr_ref, km_scr, carry_scr):
    i = pl.program_id(0)
    tm = x_ref.shape[0]
    nblk = km_scr.shape[0]

    @pl.when(i == 0)
    def _():
        km_scr[...] = jnp.zeros_like(km_scr)
        carry_scr[...] = jnp.zeros_like(carry_scr)

    x = x_ref[...]
    h = _rms(x) * g_ref[...] * (1.0 + sc_ref[...]) + sh_ref[...]
    hb = h.astype(BF16)

    width = N_HEADS_FOX * HEAD_DIM
    hsel = jnp.where(lax.broadcasted_iota(I32, (width, N_HEADS_FOX), 0) // HEAD_DIM
                     == lax.broadcasted_iota(I32, (width, N_HEADS_FOX), 1), 1.0, 0.0)
    kb = None
    tr_slot = {3: 0, 5: 1, 0: 2, 2: 3}
    qbt = None
    for c in range(6):
        pc = jnp.dot(hb, w_ref[:, c * width:(c + 1) * width], preferred_element_type=F32)
        if c in tr_slot:
            n = tr_slot[c]
            pct = pc.T
            if c == 3:
                qbt = pct
            if c == 0 or c == 3:
                pct = pct * (LOG2E * HEAD_DIM ** -0.5)
            tr_ref[n * width:(n + 1) * width, :] = pct.astype(BF16)
        if c == 1 or c == 4:
            qkv_ref[:, (c // 3) * width:(c // 3 + 1) * width] = pc.astype(BF16)
        if c < 2:
            sq = (pc * (HEAD_DIM ** -0.5) if c == 0 else pc).astype(BF16).astype(F32)
            n2 = jnp.dot((sq * sq).astype(BF16), hsel.astype(BF16), preferred_element_type=F32)
            nrm_ref[0, c:c + 1, :] = jnp.max(n2, axis=0, keepdims=True)
        if c == 4:
            kb = pc

    ft = lax.dot_general(wf_ref[...].astype(BF16), hb, NT_DIMS, preferred_element_type=F32)
    z = ft + bf_ref[...]
    logf = -(jnp.maximum(-z, 0.0) + jnp.log1p(jnp.exp(-jnp.abs(z))))
    lane = lax.broadcasted_iota(I32, logf.shape, 1)
    cs = logf
    sh = 1
    while sh < tm:
        cs = cs + jnp.where(lane >= sh, pltpu.roll(cs, sh, axis=1), 0.0)
        sh *= 2
    base = jnp.zeros_like(cs)
    for b in range(1, tm // FOX_TILE):
        base = jnp.where(lane >= b * FOX_TILE, cs[:, b * FOX_TILE - 1:b * FOX_TILE], base)
    nr_ref[...] = (-LOG2E * (cs - base)).T
    cs = cs + carry_scr[...]
    cum_ref[...] = cs
    carry_scr[...] = cs[:, tm - 1:tm]

    nb_tile = tm // MOBA_BLOCK
    for b in range(nb_tile):
        kmean = jnp.sum(kb[b * MOBA_BLOCK:(b + 1) * MOBA_BLOCK], axis=0, keepdims=True) * (1.0 / MOBA_BLOCK)
        km_scr[pl.ds(i * nb_tile + b, 1), :] = kmean

    km = km_scr[...]
    blk = lax.broadcasted_iota(I32, (nblk, tm), 0)
    col = lax.broadcasted_iota(I32, (nblk, tm), 1)
    own = i * nb_tile + col // MOBA_BLOCK
    for hd in range(N_HEADS_MOBA):
        hs = slice(hd * HEAD_DIM, (hd + 1) * HEAD_DIM)
        g = jnp.dot(km[:, hs], qbt[hs, :], preferred_element_type=F32,
                    precision=lax.Precision.HIGHEST)
        g = jnp.where(blk < own, g, -jnp.inf)
        sel = jnp.zeros(g.shape, dtype=jnp.bool_)
        for _ in range(MOBA_TOPK):
            m = jnp.max(g, axis=0, keepdims=True)
            first = jnp.min(jnp.where(g == m, blk, nblk), axis=0, keepdims=True)
            pick = (blk == first) & (m > -jnp.inf)
            sel = sel | pick
            g = jnp.where(pick, -jnp.inf, g)
        sel_ref[hd] = jnp.where(sel, 0.0, NEG)


def _inproj(x2, g, sc, sh, w_qkv, w_f, b_f):
    s, d = x2.shape
    tm = ROW_TILE
    nblk = s // MOBA_BLOCK
    n = w_qkv.shape[1]
    width = N_HEADS_FOX * HEAD_DIM
    n_rows_out = 2 * width
    n_tr = 4 * width
    assert tm % FOX_TILE == 0 and N_HEADS_FOX == N_HEADS_MOBA
    return pl.pallas_call(
        _inproj_kernel,
        out_shape=(jax.ShapeDtypeStruct((s, n_rows_out), BF16),
                   jax.ShapeDtypeStruct((N_HEADS_FOX, s), F32),
                   jax.ShapeDtypeStruct((N_HEADS_MOBA, nblk, s), F32),
                   jax.ShapeDtypeStruct((s // tm, 2, N_HEADS_FOX), F32),
                   jax.ShapeDtypeStruct((n_tr, s), BF16),
                   jax.ShapeDtypeStruct((s, N_HEADS_FOX), F32)),
        grid=(s // tm,),
        in_specs=[pl.BlockSpec((tm, d), lambda i: (i, 0)),
                  pl.BlockSpec((1, d), lambda i: (0, 0)),
                  pl.BlockSpec((1, d), lambda i: (0, 0)),
                  pl.BlockSpec((1, d), lambda i: (0, 0)),
                  pl.BlockSpec((d, n), lambda i: (0, 0)),
                  pl.BlockSpec((N_HEADS_FOX, d), lambda i: (0, 0)),
                  pl.BlockSpec((N_HEADS_FOX, 1), lambda i: (0, 0))],
        out_specs=(pl.BlockSpec((tm, n_rows_out), lambda i: (i, 0)),
                   pl.BlockSpec((N_HEADS_FOX, tm), lambda i: (0, i)),
                   pl.BlockSpec((N_HEADS_MOBA, nblk, tm), lambda i: (0, 0, i)),
                   pl.BlockSpec((1, 2, N_HEADS_FOX), lambda i: (i, 0, 0)),
                   pl.BlockSpec((n_tr, tm), lambda i: (0, i)),
                   pl.BlockSpec((tm, N_HEADS_FOX), lambda i: (i, 0))),
        scratch_shapes=[pltpu.VMEM((nblk, N_HEADS_MOBA * HEAD_DIM), F32),
                        pltpu.VMEM((N_HEADS_FOX, 1), F32)],
        compiler_params=_cparams(("arbitrary",)),
        name="inproj",
    )(x2, g, sc, sh, w_qkv, w_f, b_f)


def _fox_kernel(jlo_ref, cp_ref, qt_ref, k_ref, vt_ref, nr_ref, o_ref, m_ref, a_ref):
    p = pl.program_id(0)
    i = pl.program_id(1)
    j_first = jlo_ref[p * pl.num_programs(1) + i]
    t = qt_ref.shape[1]

    qt = qt_ref[...]
    top = lax.broadcasted_iota(I32, qt.shape, 0) < HEAD_DIM
    zq = jnp.zeros_like(qt)
    qth = (jnp.where(top, qt, zq), jnp.where(top, zq, qt))
    m_ref[...] = jnp.full(m_ref.shape, M_INIT, F32)
    a_ref[...] = jnp.zeros(a_ref.shape, F32)
    er = lax.broadcasted_iota(I32, (SUM_ROWS, 2 * t), 0)
    ec = lax.broadcasted_iota(I32, (SUM_ROWS, 2 * t), 1)
    ones_rows = jnp.where(((er == 0) & (ec < t)) | ((er == 1) & (ec >= t)), 1.0, 0.0).astype(BF16)
    arow = lax.broadcasted_iota(I32, a_ref.shape, 0)
    head0_rows = (arow < HEAD_DIM) | (arow == PAIR)

    def scores(j, diagonal):
        k0 = pl.multiple_of(j * t, t)
        kt = k_ref[pl.ds(k0, t), :]
        out = []
        for hd in range(2):
            s = jnp.dot(kt, qth[hd], preferred_element_type=F32) + nr_ref[0, pl.ds(k0, t), hd:hd + 1]
            if diagonal:
                r = lax.broadcasted_iota(I32, s.shape, 0)
                c = lax.broadcasted_iota(I32, s.shape, 1)
                s = jnp.where(r <= c, s, NEG)
            out.append(s)
        return out

    def softmax(ss, j, valid):
        ps, alphas = [], []
        for hd in range(2):
            h = 2 * p + hd
            pair_shift = LOG2E * (cp_ref[h, i] - cp_ref[h, j])
            if valid is not None:
                pair_shift = jnp.where(valid, pair_shift, NEG)
            m_old = m_ref[hd]
            m_new = jnp.maximum(m_old, jnp.max(ss[hd], axis=0, keepdims=True) + pair_shift)
            alphas.append(jnp.exp2(m_old - m_new))
            m_ref[hd] = m_new
            ps.append(jnp.exp2(ss[hd] - (m_new - pair_shift)).astype(BF16))
        return ps, alphas

    def accumulate(j, ps, alphas):
        k0 = pl.multiple_of(j * t, t)
        vt = vt_ref[:, pl.ds(k0, t)]
        zv = jnp.zeros_like(vt)
        vcat = jnp.concatenate([jnp.where(top, vt, zv), jnp.where(top, zv, vt)], axis=1)
        vcat = jnp.concatenate([vcat, ones_rows], axis=0)
        pcat = jnp.concatenate(ps, axis=0)
        a_ref[...] = (a_ref[...] * jnp.where(head0_rows, alphas[0], alphas[1])
                      + jnp.dot(vcat, pcat, preferred_element_type=F32))

    def body(j, carry):
        accumulate(j, *softmax(scores(j, False), j, None))
        return carry

    lax.fori_loop(j_first, i - 1, body, 0)
    jp = jnp.maximum(i - 1, 0)
    ss_prev = scores(jp, False)
    ss_diag = scores(i, True)
    w_prev = softmax(ss_prev, jp, i >= 1)
    w_diag = softmax(ss_diag, i, None)
    accumulate(jp, *w_prev)
    accumulate(i, *w_diag)
    out_t =a_ref[0:PAIR, :] / jnp.where(top, a_ref[PAIR:PAIR + 1, :], a_ref[PAIR + 1:PAIR + 2, :])
    o_ref[...] = out_t.T.astype(o_ref.dtype)


def _fox_first_tile(cum, nrm, t):
    cend = cum[:, t - 1::t]
    nt = cend.shape[1]
    cprev = jnp.concatenate([jnp.zeros((cend.shape[0], 1), F32), cend[:, :-1]], axis=1)
    rep = nt // nrm.shape[0]
    qn = jnp.repeat(jnp.sqrt(nrm[:, 0, :]).T, rep, axis=1)
    kn = jnp.repeat(jnp.sqrt(nrm[:, 1, :]).T, rep, axis=1)
    gap = (1.02 * qn[:, :, None] * (kn[:, None, :] + kn[:, :, None])
           + cprev[:, :, None] - cend[:, None, :])
    jj = jnp.arange(nt)[None, None, :]
    ii = jnp.arange(nt)[None, :, None]
    needed = (jj < ii) & jnp.logical_not(gap <= -EXP_UNDERFLOW)
    needed = needed[0::2] | needed[1::2]
    first = jnp.min(jnp.where(needed, jj, ii), axis=2)
    return first.reshape(-1).astype(I32)


def _fox(qkv, tr, nr, cum, nrm):
    s = qkv.shape[0]
    t = FOX_TILE
    npair = N_HEADS_FOX // 2
    jlo = _fox_first_tile(cum, nrm, t)
    cend = cum[:, t - 1::t]
    cprev = jnp.concatenate([jnp.zeros((cend.shape[0], 1), F32), cend[:, :-1]], axis=1)
    q_rows = 2 * (N_HEADS_MOBA // 2)
    grid_spec = pltpu.PrefetchScalarGridSpec(
        num_scalar_prefetch=1,
        grid=(npair, s // t),
        in_specs=[pl.BlockSpec(memory_space=pltpu.SMEM),
                  pl.BlockSpec((PAIR, t), lambda p, i, jl: (q_rows + p, i)),
                  pl.BlockSpec((s, PAIR), lambda p, i, jl: (0, p)),
                  pl.BlockSpec((PAIR, s), lambda p, i, jl: (q_rows + npair + p, 0)),
                  pl.BlockSpec((1, s, 2), lambda p, i, jl: (p, 0, 0))],
        out_specs=pl.BlockSpec((t, PAIR), lambda p, i, jl: (i, p)),
        scratch_shapes=[pltpu.VMEM((2, 1, t), F32), pltpu.VMEM((PAIR + SUM_ROWS, t), F32)],
    )
    nr_pairs = nr.reshape(s, npair, 2).transpose(1, 0, 2)
    return pl.pallas_call(
        _fox_kernel,
        out_shape=jax.ShapeDtypeStruct((s, npair * PAIR), BF16),
        grid_spec=grid_spec,
        compiler_params=_cparams(("arbitrary", "arbitrary")),
        name="fox",
    )(jlo, cprev, tr, qkv, tr, nr_pairs)


def _t5_bucket_np(dist):
    dist = np.maximum(dist, 0)
    max_exact = NUM_BUCKETS // 2
    d = np.maximum(dist, 1).astype(np.float32)
    large = max_exact + (np.log(d / np.float32(max_exact)) / np.float32(math.log(MAX_DISTANCE / max_exact))
                         * np.float32(NUM_BUCKETS - max_exact)).astype(np.int32)
    large = np.minimum(large, NUM_BUCKETS - 1)
    return np.where(dist < max_exact, dist, large).astype(np.int32)


def _moba_kernel(rb_ref, qt_ref, k_ref, vt_ref, sel_ref, bkt_ref, o_ref,
                 m_ref, a_ref, bias_scr, sa_ref, sb_ref, pa_ref, pb_ref, ala_ref, alb_ref):
    p = pl.program_id(0)
    g = pl.program_id(1)
    tq = qt_ref.shape[1]
    t = tq // 2

    @pl.when(g == 0)
    def _():
        r = lax.broadcasted_iota(I32, (t, t), 0)
        c = lax.broadcasted_iota(I32, (t, t), 1)
        zero = jnp.zeros((t, t), F32)
        for hd in range(2):
            h = 2 * p + hd
            far = rb_ref[(NUM_BUCKETS - 1) * N_HEADS_MOBA + h]
            tiles = []
            for w in range(2):
                bkt = bkt_ref[w]
                acc = jnp.zeros(bkt.shape, F32)
                for kk in range(NUM_BUCKETS):
                    acc = acc + jnp.where(bkt == kk, rb_ref[kk * N_HEADS_MOBA + h], 0.0)
                tiles.append((acc - far) * LOG2E)
            prev_t = tiles[0]
            own_t = jnp.where(r <= c, tiles[1], NEG)
            bias_scr[hd, 0] = jnp.concatenate([prev_t, zero], axis=1)
            bias_scr[hd, 1] = jnp.concatenate([own_t, prev_t], axis=1)
            bias_scr[hd, 2] = jnp.concatenate([zero, own_t], axis=1)

    qt = qt_ref[...]
    top = lax.broadcasted_iota(I32, qt.shape, 0) < HEAD_DIM
    zq = jnp.zeros_like(qt)
    qth = (jnp.where(top, qt, zq), jnp.where(top, zq, qt))
    m_ref[...] = jnp.full(m_ref.shape, M_INIT, F32)
    a_ref[...] = jnp.zeros(a_ref.shape, F32)

    er = lax.broadcasted_iota(I32, (SUM_ROWS, 2 * t), 0)
    ec = lax.broadcasted_iota(I32, (SUM_ROWS, 2 * t), 1)
    ones_rows = jnp.where(((er == 0) & (ec < t)) | ((er == 1) & (ec >= t)), 1.0, 0.0).astype(BF16)
    arow = lax.broadcasted_iota(I32, a_ref.shape, 0)
    head0_rows = (arow < HEAD_DIM) | (arow == PAIR)
    vtop = lax.broadcasted_iota(I32, (PAIR, t), 0) < HEAD_DIM
    in_a = lax.broadcasted_iota(I32, (1, tq), 1) < t

    n_far = jnp.maximum(2 * g - 1, 0)

    def produce(j, s_buf):
        k0 = pl.multiple_of(j * t, t)
        kt = k_ref[pl.ds(k0, t), :]
        for hd in range(2):
            s_buf[hd] = jnp.dot(kt, qth[hd], preferred_element_type=F32)

    def softmax(s_buf, p_buf, al_buf, selrows, w):
        for hd in range(2):
            s = s_buf[hd]
            if w is not None:
                s = s + bias_scr[hd, w]
            smax = jnp.max(s, axis=0, keepdims=True)
            m_old = m_ref[hd]
            m_new = jnp.maximum(m_old, smax + selrows[hd])
            shift = m_new - selrows[hd]
            al_buf[hd] = jnp.exp2(m_old - m_new)
            m_ref[hd] = m_new
            p_buf[hd * t:(hd + 1) * t, :] = jnp.exp2(s - shift).astype(BF16)

    def far_rows(j):
        return [jnp.where(j < n_far, sel_ref[hd, pl.ds(j, 1), :], NEG) for hd in range(2)]

    def accumulate(j, p_buf, al_buf):
        k0 = pl.multiple_of(j * t, t)
        vt = vt_ref[:, pl.ds(k0, t)]
        zv = jnp.zeros_like(vt)
        vcat = jnp.concatenate([jnp.where(vtop, vt, zv), jnp.where(vtop, zv, vt)], axis=1)
        vcat = jnp.concatenate([vcat, ones_rows], axis=0)
        a_ref[...] = (a_ref[...] * jnp.where(head0_rows, al_buf[0], al_buf[1])
                      + jnp.dot(vcat, p_buf[...], preferred_element_type=F32))

    pb_ref[...] = jnp.zeros_like(pb_ref)
    alb_ref[...] = jnp.ones_like(alb_ref)
    produce(0, sa_ref)

    def body(u, carry):
        j0 = 2 * u
        produce(j0 + 1, sb_ref)
        softmax(sa_ref, pa_ref, ala_ref, far_rows(j0), None)
        accumulate(jnp.maximum(j0 - 1, 0), pb_ref, alb_ref)
        produce(j0 + 2, sa_ref)
        accumulate(j0, pa_ref, ala_ref)
        softmax(sb_ref, pb_ref, alb_ref, far_rows(j0 + 1), None)
        return carry

    n_pairs = (n_far + 1) // 2
    lax.fori_loop(0, n_pairs, body, 0)

    j1 = jnp.maximum(2 * g - 1, 0)
    j2 = 2 * g
    j3 = 2 * g + 1
    rows1 = [jnp.where(g >= 1, sel_ref[hd, pl.ds(j1, 1), :], NEG) for hd in range(2)]
    rows2 = [jnp.where(in_a, 0.0, sel_ref[hd, pl.ds(j2, 1), :]) for hd in range(2)]
    rows3 = [jnp.where(in_a, NEG, 0.0)] * 2
    produce(j1, sa_ref)
    produce(j2, sb_ref)
    softmax(sa_ref, pa_ref, ala_ref, rows1, 0)
    accumulate(jnp.maximum(2 * n_pairs - 1, 0), pb_ref, alb_ref)
    produce(j3, sa_ref)
    softmax(sb_ref, pb_ref, alb_ref, rows2, 1)
    accumulate(j1, pa_ref, ala_ref)
    softmax(sa_ref, pa_ref, ala_ref, rows3, 2)
    accumulate(j2, pb_ref, alb_ref)
    accumulate(j3, pa_ref, ala_ref)
    out_t = a_ref[0:PAIR, :] / jnp.where(top, a_ref[PAIR:PAIR + 1, :], a_ref[PAIR + 1:PAIR + 2, :])
    o_ref[...] = out_t.T.astype(o_ref.dtype)


def _moba(qkv, tr, sel, rel_bias):
    s = qkv.shape[0]
    t = MOBA_BLOCK
    npair = N_HEADS_MOBA // 2
    nblk = s // t
    kcol = N_HEADS_FOX // 2
    a = np.arange(t)[None, :]
    b = np.arange(t)[:, None]
    bkt = jnp.asarray(np.stack([_t5_bucket_np(t + a - b), _t5_bucket_np(a - b)]))
    tq = 2 * t
    grid_spec = pltpu.PrefetchScalarGridSpec(
        num_scalar_prefetch=1,
        grid=(npair, s // tq),
        in_specs=[pl.BlockSpec((PAIR, tq), lambda p, i, rb: (p, i)),
                  pl.BlockSpec((s, PAIR), lambda p, i, rb: (0, kcol + p)),
                  pl.BlockSpec((PAIR, s), lambda p, i, rb: (npair + p, 0)),
                  pl.BlockSpec((2, nblk, tq), lambda p, i, rb: (p, 0, i)),
                  pl.BlockSpec((2, t, t), lambda p, i, rb: (0, 0, 0))],
        out_specs=pl.BlockSpec((tq, PAIR), lambda p, i, rb: (i, p)),
        scratch_shapes=[pltpu.VMEM((2, 1, tq), F32),
                        pltpu.VMEM((PAIR + SUM_ROWS, tq), F32), pltpu.VMEM((2, 3, t, tq), F32),
                        pltpu.VMEM((2, t, tq), F32), pltpu.VMEM((2, t, tq), F32),
                        pltpu.VMEM((2 * t, tq), BF16), pltpu.VMEM((2 * t, tq), BF16),
                        pltpu.VMEM((2, 1, tq), F32), pltpu.VMEM((2, 1, tq), F32)],
    )
    return pl.pallas_call(
        _moba_kernel,
        out_shape=jax.ShapeDtypeStruct((s, npair * PAIR), BF16),
        grid_spec=grid_spec,
        compiler_params=_cparams(("arbitrary", "arbitrary")),
        name="moba",
    )(rel_bias.reshape(-1), tr, qkv, tr, sel, bkt)


def _store_token_tiles(ref, val):
    n = val.shape[0]
    for c in range(SUBLANES):
        ref[pl.ds(c, n, stride=SUBLANES), :] = val[:, c * LANES:(c + 1) * LANES]


def _load_token_tiles(ref, n):
    return jnp.concatenate([ref[pl.ds(c, n, stride=SUBLANES), :] for c in range(SUBLANES)], axis=1)


def _tile_rows(r0, n):
    start = r0 * SUBLANES
    if not isinstance(start, int):
        start = pl.multiple_of(start, SUBLANES)
    return pl.ds(start, n * SUBLANES)


def _outproj_kernel(ya_ref, yb_ref, x_ref, wa_ref, wb_ref, gpost_ref, gt_ref, gpre_ref,
                    sc_ref, sh_ref, wr_ref, br_ref,
                    x1_ref, h2_ref, idx_ref, rank_ref, gate_ref, cnt_ref, carry_scr):
    i = pl.program_id(0)
    tm = x_ref.shape[0]

    @pl.when(i == 0)
    def _():
        carry_scr[...] = jnp.zeros_like(carry_scr)

    y = (jnp.dot(ya_ref[...], wa_ref[...], preferred_element_type=F32)
         + jnp.dot(yb_ref[...], wb_ref[...], preferred_element_type=F32))
    x1 = x_ref[...] + gt_ref[...] * (_rms(y) * gpost_ref[...])
    x1_ref[...] = x1
    h2 = _rms(x1) * gpre_ref[...] * (1.0 + sc_ref[...]) + sh_ref[...]
    _store_token_tiles(h2_ref, h2)

    logits =jnp.dot(h2, wr_ref[...], preferred_element_type=F32,
                     precision=lax.Precision.HIGHEST) + br_ref[...]
    ne = logits.shape[1]
    lane = lax.broadcasted_iota(I32, logits.shape, 1)
    lane4 = lax.broadcasted_iota(I32, (tm, TOP_K), 1)
    g = logits
    mask = jnp.zeros(logits.shape, F32)
    vals, picks = [], []
    for _ in range(TOP_K):
        m = jnp.max(g, axis=1, keepdims=True)
        first = jnp.min(jnp.where(g == m, lane, ne), axis=1, keepdims=True)
        pick = lane == first
        mask = jnp.where(pick, 1.0, mask)
        g = jnp.where(pick, -jnp.inf, g)
        vals.append(m)
        picks.append(pick)
    ex = [jnp.exp(v - vals[0]) for v in vals]
    den = ex[0] + ex[1] + ex[2] + ex[3]
    gates = [e / den for e in ex]

    r = lax.broadcasted_iota(I32, (tm, tm), 0)
    c = lax.broadcasted_iota(I32, (tm, tm), 1)
    tril = jnp.where(c < r, 1.0, 0.0).astype(BF16)
    before = jnp.dot(tril, mask.astype(BF16), preferred_element_type=F32) + carry_scr[...]
    total = carry_scr[...] + jnp.sum(mask, axis=0, keepdims=True)
    carry_scr[...] = total
    cnt_ref[...] = jnp.broadcast_to(total, cnt_ref.shape)

    def pack4(cols):
        return jnp.where(lane4 == 0, cols[0],
                         jnp.where(lane4 == 1, cols[1], jnp.where(lane4 == 2, cols[2], cols[3])))

    ranks = [jnp.sum(jnp.where(pk, before, 0.0), axis=1, keepdims=True) for pk in picks]
    idxs = [jnp.sum(jnp.where(pk, lane, 0), axis=1, keepdims=True) for pk in picks]
    idx_ref[...] = pack4(idxs)
    rank_ref[...] = pack4(ranks).astype(I32)
    gate_ref[...] = pack4(gates)


def _outproj(mix_a, mix_b, x2, w_a, w_b, gpost, gt, gpre, sc, sh, w_router, b_router):
    s, d = x2.shape
    tm = ROW_TILE
    ne = w_router.shape[1]
    wa = mix_a.shape[1]
    row = lambda i: (i, 0)
    fix = lambda i: (0, 0)
    vec = pl.BlockSpec((1, d), fix)
    return pl.pallas_call(
        _outproj_kernel,
        out_shape=(jax.ShapeDtypeStruct((s, d), F32),
                   jax.ShapeDtypeStruct((s * SUBLANES, LANES), F32),
                   jax.ShapeDtypeStruct((s, TOP_K), I32),
                   jax.ShapeDtypeStruct((s, TOP_K), I32),
                   jax.ShapeDtypeStruct((s, TOP_K), F32),
                   jax.ShapeDtypeStruct((8, ne), F32)),
        grid=(s // tm,),
        in_specs=[pl.BlockSpec((tm, wa), row), pl.BlockSpec((tm, wa), row), pl.BlockSpec((tm, d), row),
                  pl.BlockSpec((wa, d), fix), pl.BlockSpec((wa, d), fix),
                  vec, vec, vec, vec, vec,
                  pl.BlockSpec((d, ne), fix), pl.BlockSpec((1, ne), fix)],
        out_specs=(pl.BlockSpec((tm, d), row), pl.BlockSpec((tm * SUBLANES, LANES), row),
                   pl.BlockSpec((tm, TOP_K), row), pl.BlockSpec((tm, TOP_K), row),
                   pl.BlockSpec((tm, TOP_K), row), pl.BlockSpec((8, ne), fix)),
        scratch_shapes=[pltpu.VMEM((1, ne), F32)],
        compiler_params=_cparams(("arbitrary",)),
        name="outproj",
    )(mix_a, mix_b, x2, w_a, w_b, gpost, gt, gpre, sc, sh, w_router, b_router)


def _dest_kernel(idx_ref, rank_ref, pstart_ref, o_ref):
    idx = idx_ref[...]
    tm = idx.shape[0]
    ne = pstart_ref.shape[1]
    lane = lax.broadcasted_iota(I32, (tm, ne), 1)
    lane4 = lax.broadcasted_iota(I32, idx.shape, 1)
    out = rank_ref[...]
    for k in range(TOP_K):
        start = jnp.sum(jnp.where(lane == idx[:, k:k + 1], pstart_ref[...], 0), axis=1, keepdims=True)
        out = out + jnp.where(lane4 == k, start, 0)
    o_ref[...] = out


def _dest(idx4, rank4, pstart):
    s = idx4.shape[0]
    tm = min(4 * ROW_TILE, s)
    row = lambda i: (i, 0)
    return pl.pallas_call(
        _dest_kernel,
        out_shape=jax.ShapeDtypeStruct((s, TOP_K), I32),
        grid=(s // tm,),
        in_specs=[pl.BlockSpec((tm, TOP_K), row), pl.BlockSpec((tm, TOP_K), row),
                  pl.BlockSpec((1, N_EXPERTS), lambda i: (0, 0))],
        out_specs=pl.BlockSpec((tm, TOP_K), row),
        compiler_params=_cparams(("arbitrary",)),
        name="dest",
    )(idx4, rank4, pstart.reshape(1, -1))


def _dispatch_kernel(pstart_ref, pblk_ref, nu_ref, dest_ref, h_ref, xs_ref,
                     zero_scr, sem, zsem):
    tm = h_ref.shape[0] // SUBLANES
    bm = zero_scr.shape[0] // SUBLANES
    n_blk = xs_ref.shape[0] // (bm * SUBLANES)

    rows = _tile_rows

    @pl.when(pl.program_id(0) == 0)
    def _():
        zero_scr[...] = jnp.zeros_like(zero_scr)

        def zero_copy(row0):
            return pltpu.make_async_copy(zero_scr, xs_ref.at[rows(row0, bm)], zsem)

        for phase in range(2):
            for e in range(N_EXPERTS):
                last = pstart_ref[e] + (pblk_ref[e] - 1) * bm
                tail = (n_blk - N_EXPERTS + e) * bm
                for cond, row0 in ((pblk_ref[e] > 0, last), (n_blk - N_EXPERTS + e >= nu_ref[0], tail)):
                    @pl.when(cond)
                    def _():
                        if phase == 0:
                            zero_copy(row0).start()
                        else:
                            zero_copy(row0).wait()

    def row_copy(r, k):
        dst = dest_ref[r * TOP_K + k]
        return pltpu.make_async_copy(h_ref.at[rows(r, 1)], xs_ref.at[rows(dst, 1)], sem)

    def start(r, carry):
        for k in range(TOP_K):
            row_copy(r, k).start(priority=k % 2)
        return carry

    lax.fori_loop(0, tm, start, 0, unroll=4)
    for k in range(TOP_K):
        pltpu.make_async_copy(h_ref, xs_ref.at[rows(0, tm)], sem).wait()


def _dispatch(pstart, pblk, n_used, dest_flat, h2, n_rows):
    s = h2.shape[0] // SUBLANES
    tm = COMBINE_TILE
    grid_spec = pltpu.PrefetchScalarGridSpec(
        num_scalar_prefetch=3,
        grid=(s // tm,),
        in_specs=[pl.BlockSpec((tm * TOP_K,), lambda i, *_: (i,), memory_space=pltpu.SMEM),
                  pl.BlockSpec((tm * SUBLANES, LANES), lambda i, *_: (i, 0))],
        out_specs=pl.BlockSpec(memory_space=pl.ANY),
        scratch_shapes=[pltpu.VMEM((EXPERT_BLOCK * SUBLANES, LANES), F32), pltpu.SemaphoreType.DMA,
                        pltpu.SemaphoreType.DMA],
    )
    return pl.pallas_call(
        _dispatch_kernel,
        out_shape=jax.ShapeDtypeStruct((n_rows * SUBLANES, LANES), F32),
        grid_spec=grid_spec,
        compiler_params=_cparams(("arbitrary",)),
        name="dispatch",
    )(pstart, pblk, n_used, dest_flat, h2)


def _experts_kernel(be_ref, nu_ref, xs_ref, wgu_ref, bgu_ref, wd_ref, bd_ref, y_ref,
                    wgu_bf, wd_bf):
    b = pl.program_id(0)
    d_exp = wd_ref.shape[1]
    prev = be_ref[jnp.maximum(b - 1, 0)]
    changed = (b == 0) | (be_ref[b] != prev)

    @pl.when((b < nu_ref[0]) & changed)
    def _():
        rows = 128

        def cast_gu(c, carry):
            r0 = pl.multiple_of(c * rows, rows)
            wgu_bf[pl.ds(r0, rows), :] = wgu_ref[0, pl.ds(r0, rows), :].astype(BF16)
            return carry

        def cast_d(c, carry):
            r0 = pl.multiple_of(c * rows, rows)
            wd_bf[pl.ds(r0, rows), :] = wd_ref[0, pl.ds(r0, rows), :].astype(BF16)
            return carry

        lax.fori_loop(0, wgu_ref.shape[1] // rows, cast_gu, 0)
        lax.fori_loop(0, wd_ref.shape[1] // rows, cast_d, 0)

    @pl.when(b < nu_ref[0])
    def _():
        bm = xs_ref.shape[0] // SUBLANES
        xb = _load_token_tiles(xs_ref, bm).astype(BF16)
        hdn = jnp.dot(xb, wgu_bf[...], preferred_element_type=F32) + bgu_ref[0]
        x_glu = jnp.minimum(hdn[:, :d_exp], SWIGLU_LIMIT)
        x_lin = jnp.clip(hdn[:, d_exp:], -SWIGLU_LIMIT, SWIGLU_LIMIT)
        act = x_glu * jax.nn.sigmoid(SWIGLU_ALPHA * x_glu) * (x_lin + 1.0)
        _store_token_tiles(y_ref, jnp.dot(act.astype(BF16), wd_bf[...], preferred_element_type=F32)
                           + bd_ref[0])

    @pl.when(b >= nu_ref[0])
    def _():
        y_ref[...] = jnp.zeros_like(y_ref)


def _experts(block_e, n_used, xs, w_gate_up, b_gate_up, w_down, b_down):
    n_rows = xs.shape[0] // SUBLANES
    bm = EXPERT_BLOCK
    n_blk = n_rows // bm
    ne, d, two_de = w_gate_up.shape
    de = w_down.shape[1]
    assert d == SUBLANES * LANES

    def blk(b, be, nu):
        return jnp.minimum(b, nu[0] - 1)

    grid_spec = pltpu.PrefetchScalarGridSpec(
        num_scalar_prefetch=2,
        grid=(n_blk,),
        in_specs=[pl.BlockSpec((bm * SUBLANES, LANES), lambda b, be, nu: (blk(b, be, nu), 0)),
                  pl.BlockSpec((1, d, two_de), lambda b, be, nu: (be[blk(b, be, nu)], 0, 0)),
                  pl.BlockSpec((1, 1, two_de), lambda b, be, nu: (be[blk(b, be, nu)], 0, 0)),
                  pl.BlockSpec((1, de, d), lambda b, be, nu: (be[blk(b, be, nu)], 0, 0)),
                  pl.BlockSpec((1, 1, d), lambda b, be, nu: (be[blk(b, be, nu)], 0, 0))],
        out_specs=pl.BlockSpec((bm * SUBLANES, LANES), lambda b, be, nu: (b, 0)),
        scratch_shapes=[pltpu.VMEM((d, two_de), BF16), pltpu.VMEM((de, d), BF16)],
    )
    return pl.pallas_call(
        _experts_kernel,
        out_shape=jax.ShapeDtypeStruct((n_rows * SUBLANES, LANES), F32),
        grid_spec=grid_spec,
        compiler_params=_cparams(("arbitrary",)),
        name="experts",
    )(block_e, n_used, xs, w_gate_up, b_gate_up.reshape(ne, 1, two_de), w_down, b_down.reshape(ne, 1, d))


def _combine_kernel(dest_ref, y_ref, gate_ref, x1_ref, gt_ref, gpost_ref, o_ref, buf, sem):
    tm = x1_ref.shape[0]

    def row_copy(r, k):
        src = dest_ref[r * TOP_K + k]
        return pltpu.make_async_copy(y_ref.at[_tile_rows(src, 1)], buf.at[k, _tile_rows(r, 1)], sem)

    def start(r, carry):
        for k in range(TOP_K):
            row_copy(r, k).start(priority=k % 2)
        return carry

    lax.fori_loop(0, tm, start, 0, unroll=4)
    for k in range(TOP_K):
        pltpu.make_async_copy(y_ref.at[_tile_rows(0, tm)], buf.at[k], sem).wait()

    gate = gate_ref[...]
    acc = gate[:, 0:1] * _load_token_tiles(buf.at[0], tm)
    for k in range(1, TOP_K):
        acc = acc + gate[:, k:k + 1] * _load_token_tiles(buf.at[k], tm)
    o_ref[...] = x1_ref[...] + gt_ref[...] * (_rms(acc) * gpost_ref[...])


def _combine(dest_flat, y, gate4, x1, gt, gpost):
    s, d = x1.shape
    tm = COMBINE_TILE
    return pl.pallas_call(
        _combine_kernel,
        out_shape=jax.ShapeDtypeStruct((s, d), F32),
        grid=(s // tm,),
        in_specs=[pl.BlockSpec((tm * TOP_K,), lambda i: (i,), memory_space=pltpu.SMEM),
                  pl.BlockSpec(memory_space=pl.ANY),
                  pl.BlockSpec((tm, TOP_K), lambda i: (i, 0)),
                  pl.BlockSpec((tm, d), lambda i: (i, 0)),
                  pl.BlockSpec((1, d), lambda i: (0, 0)),
                  pl.BlockSpec((1, d), lambda i: (0, 0))],
        out_specs=pl.BlockSpec((tm, d), lambda i: (i, 0)),
        scratch_shapes=[pltpu.VMEM((TOP_K, tm * SUBLANES, LANES), F32), pltpu.SemaphoreType.DMA],
        compiler_params=_cparams(("arbitrary",)),
        name="combine",
    )(dest_flat, y, gate4, x1, gt, gpost)


def _layer(x2, mod, g_pre_mix, g_post_mix, w_in, b_forget, rel_bias, w_out,
           g_pre_ffn, g_post_ffn, w_router, b_router, w_gate_up, b_gate_up, w_down, b_down):
    s, d = x2.shape
    sh_m, sc_m, gt_m, sh_f, sc_f, gt_f = [mod[:, k * d:(k + 1) * d] for k in range(6)]
    n_qkv = 3 * (N_HEADS_FOX + N_HEADS_MOBA) * HEAD_DIM
    fox_w = N_HEADS_FOX * HEAD_DIM

    w_qkv = w_in[:, :n_qkv].astype(BF16)
    w_f = w_in[:, n_qkv:].T
    qkv, cum, sel, nrm, tr, nr = _inproj(x2, g_pre_mix.reshape(1, d), sc_m, sh_m, w_qkv, w_f,
                                         b_forget.reshape(-1, 1))
    y_a = _fox(qkv, tr, nr, cum, nrm)
    y_b = _moba(qkv, tr, sel, rel_bias)

    w_out_bf = w_out.astype(BF16)
    x1, h2, idx4, rank4, gate4, cnt = _outproj(
        y_a, y_b, x2, w_out_bf[:fox_w], w_out_bf[fox_w:], g_post_mix.reshape(1, d), gt_m,
        g_pre_ffn.reshape(1, d), sc_f, sh_f, w_router, b_router.reshape(1, -1))

    bm = EXPERT_BLOCK
    counts = cnt[0].astype(I32)
    pblk = (counts + bm - 1) // bm
    pend_blk = jnp.cumsum(pblk)
    pstart = ((pend_blk - pblk) * bm).astype(I32)
    n_rows = s * TOP_K + N_EXPERTS * bm
    n_blk = n_rows // bm
    block_e = jnp.minimum(jnp.sum(pend_blk[None, :] <= jnp.arange(n_blk)[:, None], axis=1),
                          N_EXPERTS - 1).astype(I32)
    n_used = pend_blk[-1:].astype(I32)

    dest_flat = _dest(idx4, rank4, pstart).reshape(-1)
    xs = _dispatch(pstart, pblk.astype(I32), n_used, dest_flat, h2, n_rows)
    y = _experts(block_e, n_used, xs, w_gate_up, b_gate_up, w_down, b_down)
    return _combine(dest_flat, y, gate4, x1, gt_f, g_post_ffn.reshape(1, d))


def kernel(x, c, w_ada, b_ada, g_pre_mix, g_post_mix, w_in, b_forget, rel_bias, w_out, g_pre_ffn, g_post_ffn, w_router, b_router, w_gate_up, b_gate_up, w_down, b_down):
    bsz, s, d = x.shape
    depth = w_ada.shape[0]
    outs = []
    for bi in range(bsz):
        x2 = x[bi]
        for l in range(depth):
            mod = _adaln(c[bi:bi + 1], w_ada[l], b_ada[l])
            x2 = _layer(x2, mod, g_pre_mix[l], g_post_mix[l], w_in[l], b_forget[l], rel_bias, w_out[l],
                        g_pre_ffn[l], g_post_ffn[l], w_router[l], b_router[l], w_gate_up[l], b_gate_up[l],
                        w_down[l], b_down[l])
        outs.append(x2)
    return jnp.stack(outs)
```

```python
import functools
import math

import numpy as np
import jax
import jax.numpy as jnp
from jax import lax
from jax.experimental import pallas as pl
from jax.experimental.pallas import tpu as pltpu

F32 = jnp.float32
BF16 = jnp.bfloat16
I32 = jnp.int32

HEAD_DIM = 64
N_HEADS_FOX = 8
N_HEADS_MOBA = 8
PAIR = 2 * HEAD_DIM
MOBA_BLOCK = 256
MOBA_TOPK = 3
NUM_BUCKETS = 32
MAX_DISTANCE = 128
N_EXPERTS = 32
TOP_K = 4
SWIGLU_LIMIT = 7.0
SWIGLU_ALPHA = 1.702
RMS_EPS = 1e-6
NEG = -(2.0 ** 100)
M_INIT = -(2.0 ** 99)
LOG2E = math.log2(math.e)
SUM_ROWS = 16
SUBLANES = 8
LANES = 128
EXP_UNDERFLOW = 90.0
VMEM_LIMIT = 56 * 1024 * 1024

ROW_TILE = 512
FOX_TILE = 256
EXPERT_BLOCK = 512
COMBINE_TILE = 512

NT_DIMS = (((1,), (1,)), ((), ()))


def _cparams(sem):
    return pltpu.CompilerParams(dimension_semantics=sem, vmem_limit_bytes=VMEM_LIMIT)


def _rms(x):
    return x * lax.rsqrt(jnp.mean(x * x, axis=-1, keepdims=True) + RMS_EPS)


def _adaln_kernel(c_ref, w_ref, b_ref, o_ref):
    c = c_ref[...]
    cond = c * jax.nn.sigmoid(c)
    o_ref[...] = jnp.dot(cond, w_ref[...], preferred_element_type=F32,
                         precision=lax.Precision.HIGHEST) + b_ref[...]


def _adaln(c, w_ada, b_ada):
    d = c.shape[-1]
    n = w_ada.shape[-1]
    c8 = jnp.broadcast_to(c.reshape(1, d), (8, d))
    out = pl.pallas_call(
        _adaln_kernel,
        out_shape=jax.ShapeDtypeStruct((8, n), F32),
        grid=(n // d,),
        in_specs=[pl.BlockSpec((8, d), lambda j: (0, 0)),
                  pl.BlockSpec((d, d), lambda j: (0, j)),
                  pl.BlockSpec((1, d), lambda j: (0, j))],
        out_specs=pl.BlockSpec((8, d), lambda j: (0, j)),
        compiler_params=_cparams(("arbitrary",)),
        name="adaln",
    )(c8, w_ada, b_ada.reshape(1, n))
    return out[0:1]


def _inproj_kernel(x_ref, g_ref, sc_ref, sh_ref, w_ref, wf_ref, bf_ref,
                   qkv_ref, cum_ref, sel_ref, nrm_ref, tr_ref, nr_ref, km_scr, carry_scr):
    i = pl.program_id(0)
    tm = x_ref.shape[0]
    nblk = km_scr.shape[0]

    @pl.when(i == 0)
    def _():
        km_scr[...] = jnp.zeros_like(km_scr)
        carry_scr[...] = jnp.zeros_like(carry_scr)

    x = x_ref[...]
    h = _rms(x) * g_ref[...] * (1.0 + sc_ref[...]) + sh_ref[...]
    hb = h.astype(BF16)

    width = N_HEADS_FOX * HEAD_DIM
    hsel = jnp.where(lax.broadcasted_iota(I32, (width, N_HEADS_FOX), 0) // HEAD_DIM
                     == lax.broadcasted_iota(I32, (width, N_HEADS_FOX), 1), 1.0, 0.0)
    kb = None
    tr_slot = {3: 0, 5: 1, 0: 2, 2: 3}
    qbt = None
    for c in range(6):
        pc = jnp.dot(hb, w_ref[:, c * width:(c + 1) * width], preferred_element_type=F32)
        if c in tr_slot:
            n = tr_slot[c]
            pct = pc.T
            if c == 3:
                qbt = pct
            if c == 0 or c == 3:
                pct = pct * (LOG2E * HEAD_DIM ** -0.5)
            tr_ref[n * width:(n + 1) * width, :] = pct.astype(BF16)
        if c == 1 or c == 4:
            qkv_ref[:, (c // 3) * width:(c // 3 + 1) * width] = pc.astype(BF16)
        if c < 2:
            sq = (pc * (HEAD_DIM ** -0.5) if c == 0 else pc).astype(BF16).astype(F32)
            n2 = jnp.dot((sq * sq).astype(BF16), hsel.astype(BF16), preferred_element_type=F32)
            nrm_ref[0, c:c + 1, :] = jnp.max(n2, axis=0, keepdims=True)
        if c == 4:
            kb = pc

    ft = lax.dot_general(wf_ref[...].astype(BF16), hb, NT_DIMS, preferred_element_type=F32)
    z = ft + bf_ref[...]
    logf = -(jnp.maximum(-z, 0.0) + jnp.log1p(jnp.exp(-jnp.abs(z))))
    lane = lax.broadcasted_iota(I32, logf.shape, 1)
    cs = logf
    sh = 1
    while sh < tm:
        cs = cs + jnp.where(lane >= sh, pltpu.roll(cs, sh, axis=1), 0.0)
        sh *= 2
    base = jnp.zeros_like(cs)
    for b in range(1, tm // FOX_TILE):
        base = jnp.where(lane >= b * FOX_TILE, cs[:, b * FOX_TILE - 1:b * FOX_TILE], base)
    nr_t = (-LOG2E * (cs - base)).T
    for pp in range(nr_ref.shape[0]):
        nr_ref[pp] = nr_t[:, 2 * pp:2 * pp + 2]
    cs = cs + carry_scr[...]
    cum_ref[...] = cs
    carry_scr[...] = cs[:, tm - 1:tm]

    nb_tile = tm // MOBA_BLOCK
    for b in range(nb_tile):
        kmean = jnp.sum(kb[b * MOBA_BLOCK:(b + 1) * MOBA_BLOCK], axis=0, keepdims=True) * (1.0 / MOBA_BLOCK)
        km_scr[pl.ds(i * nb_tile + b, 1), :] = kmean

    km = km_scr[...]
    blk = lax.broadcasted_iota(I32, (nblk, tm), 0)
    col = lax.broadcasted_iota(I32, (nblk, tm), 1)
    own = i * nb_tile + col // MOBA_BLOCK
    for hd in range(N_HEADS_MOBA):
        hs = slice(hd * HEAD_DIM, (hd + 1) * HEAD_DIM)
        g = jnp.dot(km[:, hs], qbt[hs, :], preferred_element_type=F32,
                    precision=lax.Precision.HIGHEST)
        g = jnp.where(blk < own, g, -jnp.inf)
        sel = jnp.zeros(g.shape, dtype=jnp.bool_)
        for _ in range(MOBA_TOPK):
            m = jnp.max(g, axis=0, keepdims=True)
            first = jnp.min(jnp.where(g == m, blk, nblk), axis=0, keepdims=True)
            pick = (blk == first) & (m > -jnp.inf)
            sel = sel | pick
            g = jnp.where(pick, -jnp.inf, g)
        sel_ref[hd] = jnp.where(sel, 0.0, NEG)


def _inproj(x2, g, sc, sh, w_qkv, w_f, b_f):
    s, d = x2.shape
    tm = ROW_TILE
    nblk = s // MOBA_BLOCK
    n = w_qkv.shape[1]
    width = N_HEADS_FOX * HEAD_DIM
    n_rows_out = 2 * width
    n_tr = 4 * width
    assert tm % FOX_TILE == 0 and N_HEADS_FOX == N_HEADS_MOBA
    return pl.pallas_call(
        _inproj_kernel,
        out_shape=(jax.ShapeDtypeStruct((s, n_rows_out), BF16),
                   jax.ShapeDtypeStruct((N_HEADS_FOX, s), F32),
                   jax.ShapeDtypeStruct((N_HEADS_MOBA, nblk, s), F32),
                   jax.ShapeDtypeStruct((s // tm, 2, N_HEADS_FOX), F32),
                   jax.ShapeDtypeStruct((n_tr, s), BF16),
                   jax.ShapeDtypeStruct((N_HEADS_FOX // 2, s, 2), F32)),
        grid=(s // tm,),
        in_specs=[pl.BlockSpec((tm, d), lambda i: (i, 0)),
                  pl.BlockSpec((1, d), lambda i: (0, 0)),
                  pl.BlockSpec((1, d), lambda i: (0, 0)),
                  pl.BlockSpec((1, d), lambda i: (0, 0)),
                  pl.BlockSpec((d, n), lambda i: (0, 0)),
                  pl.BlockSpec((N_HEADS_FOX, d), lambda i: (0, 0)),
                  pl.BlockSpec((N_HEADS_FOX, 1), lambda i: (0, 0))],
        out_specs=(pl.BlockSpec((tm, n_rows_out), lambda i: (i, 0)),
                   pl.BlockSpec((N_HEADS_FOX, tm), lambda i: (0, i)),
                   pl.BlockSpec((N_HEADS_MOBA, nblk, tm), lambda i: (0, 0, i)),
                   pl.BlockSpec((1, 2, N_HEADS_FOX), lambda i: (i, 0, 0)),
                   pl.BlockSpec((n_tr, tm), lambda i: (0, i)),
                   pl.BlockSpec((N_HEADS_FOX // 2, tm, 2), lambda i: (0, i, 0))),
        scratch_shapes=[pltpu.VMEM((nblk, N_HEADS_MOBA * HEAD_DIM), F32),
                        pltpu.VMEM((N_HEADS_FOX, 1), F32)],
        compiler_params=_cparams(("arbitrary",)),
        name="inproj",
    )(x2, g, sc, sh, w_qkv, w_f, b_f)


def _fox_kernel(jlo_ref, cp_ref, qt_ref, k_ref, vt_ref, nr_ref, o_ref, m_ref, a_ref):
    p = pl.program_id(0)
    i = pl.program_id(1)
    j_first = jlo_ref[p * pl.num_programs(1) + i]
    t = qt_ref.shape[1]

    qt = qt_ref[...]
    top = lax.broadcasted_iota(I32, qt.shape, 0) < HEAD_DIM
    zq = jnp.zeros_like(qt)
    qth = (jnp.where(top, qt, zq), jnp.where(top, zq, qt))
    m_ref[...] = jnp.full(m_ref.shape, M_INIT, F32)
    a_ref[...] = jnp.zeros(a_ref.shape, F32)
    er = lax.broadcasted_iota(I32, (SUM_ROWS, 2 * t), 0)
    ec = lax.broadcasted_iota(I32, (SUM_ROWS, 2 * t), 1)
    ones_rows = jnp.where(((er == 0) & (ec < t)) | ((er == 1) & (ec >= t)), 1.0, 0.0).astype(BF16)
    arow = lax.broadcasted_iota(I32, a_ref.shape, 0)
    head0_rows = (arow < HEAD_DIM) | (arow == PAIR)

    def scores(j, diagonal):
        k0 = pl.multiple_of(j * t, t)
        kt = k_ref[pl.ds(k0, t), :]
        out = []
        for hd in range(2):
            s = jnp.dot(kt, qth[hd], preferred_element_type=F32) + nr_ref[0, pl.ds(k0, t), hd:hd + 1]
            if diagonal:
                r = lax.broadcasted_iota(I32, s.shape, 0)
                c = lax.broadcasted_iota(I32, s.shape, 1)
                s = jnp.where(r <= c, s, NEG)
            out.append(s)
        return out

    def softmax(ss, j, valid):
        ps, alphas = [], []
        for hd in range(2):
            h = 2 * p + hd
            pair_shift = LOG2E * (cp_ref[h, i] - cp_ref[h, j])
            if valid is not None:
                pair_shift = jnp.where(valid, pair_shift, NEG)
            m_old = m_ref[hd]
            m_new = jnp.maximum(m_old, jnp.max(ss[hd], axis=0, keepdims=True) + pair_shift)
            alphas.append(jnp.exp2(m_old - m_new))
            m_ref[hd] = m_new
            ps.append(jnp.exp2(ss[hd] - (m_new - pair_shift)).astype(BF16))
        return ps, alphas

    def accumulate(j, ps, alphas):
        k0 = pl.multiple_of(j * t, t)
        vt = vt_ref[:, pl.ds(k0, t)]
        zv = jnp.zeros_like(vt)
        vcat = jnp.concatenate([jnp.where(top, vt, zv), jnp.where(top, zv, vt)], axis=1)
        vcat = jnp.concatenate([vcat, ones_rows], axis=0)
        pcat = jnp.concatenate(ps, axis=0)
        a_ref[...] = (a_ref[...] * jnp.where(head0_rows, alphas[0], alphas[1])
                      + jnp.dot(vcat, pcat, preferred_element_type=F32))

    def body(j, carry):
        accumulate(j, *softmax(scores(j, False), j, None))
        return carry

    lax.fori_loop(j_first, i - 1, body, 0)
    jp = jnp.maximum(i - 1, 0)
    ss_prev = scores(jp, False)
    ss_diag = scores(i, True)
    w_prev = softmax(ss_prev, jp, i >= 1)
    w_diag = softmax(ss_diag, i, None)
    accumulate(jp, *w_prev)
    accumulate(i, *w_diag)
    out_t =a_ref[0:PAIR, :] / jnp.where(top, a_ref[PAIR:PAIR + 1, :], a_ref[PAIR + 1:PAIR + 2, :])
    o_ref[...] = out_t.T.astype(o_ref.dtype)


def _fox_first_tile(cum, nrm, t):
    cend = cum[:, t - 1::t]
    nt = cend.shape[1]
    cprev = jnp.concatenate([jnp.zeros((cend.shape[0], 1), F32), cend[:, :-1]], axis=1)
    rep = nt // nrm.shape[0]
    qn = jnp.repeat(jnp.sqrt(nrm[:, 0, :]).T, rep, axis=1)
    kn = jnp.repeat(jnp.sqrt(nrm[:, 1, :]).T, rep, axis=1)
    gap = (1.02 * qn[:, :, None] * (kn[:, None, :] + kn[:, :, None])
           + cprev[:, :, None] - cend[:, None, :])
    jj = jnp.arange(nt)[None, None, :]
    ii = jnp.arange(nt)[None, :, None]
    needed = (jj < ii) & jnp.logical_not(gap <= -EXP_UNDERFLOW)
    needed = needed[0::2] | needed[1::2]
    first = jnp.min(jnp.where(needed, jj, ii), axis=2)
    return first.reshape(-1).astype(I32)


def _fox(qkv, tr, nr, cum, nrm):
    s = qkv.shape[0]
    t = FOX_TILE
    npair = N_HEADS_FOX // 2
    jlo = _fox_first_tile(cum, nrm, t)
    cend = cum[:, t - 1::t]
    cprev = jnp.concatenate([jnp.zeros((cend.shape[0], 1), F32), cend[:, :-1]], axis=1)
    q_rows = 2 * (N_HEADS_MOBA // 2)
    grid_spec = pltpu.PrefetchScalarGridSpec(
        num_scalar_prefetch=1,
        grid=(npair, s // t),
        in_specs=[pl.BlockSpec(memory_space=pltpu.SMEM),
                  pl.BlockSpec((PAIR, t), lambda p, i, jl: (q_rows + p, i)),
                  pl.BlockSpec((s, PAIR), lambda p, i, jl: (0, p)),
                  pl.BlockSpec((PAIR, s), lambda p, i, jl: (q_rows + npair + p, 0)),
                  pl.BlockSpec((1, s, 2), lambda p, i, jl: (p, 0, 0))],
        out_specs=pl.BlockSpec((t, PAIR), lambda p, i, jl: (i, p)),
        scratch_shapes=[pltpu.VMEM((2, 1, t), F32), pltpu.VMEM((PAIR + SUM_ROWS, t), F32)],
    )
    return pl.pallas_call(
        _fox_kernel,
        out_shape=jax.ShapeDtypeStruct((s, npair * PAIR), BF16),
        grid_spec=grid_spec,
        compiler_params=_cparams(("arbitrary", "arbitrary")),
        name="fox",
    )(jlo, cprev, tr, qkv, tr, nr)


def _t5_bucket_np(dist):
    dist = np.maximum(dist, 0)
    max_exact = NUM_BUCKETS // 2
    d = np.maximum(dist, 1).astype(np.float32)
    large = max_exact + (np.log(d / np.float32(max_exact)) / np.float32(math.log(MAX_DISTANCE / max_exact))
                         * np.float32(NUM_BUCKETS - max_exact)).astype(np.int32)
    large = np.minimum(large, NUM_BUCKETS - 1)
    return np.where(dist < max_exact, dist, large).astype(np.int32)


def _moba_kernel(rb_ref, qt_ref, k_ref, vt_ref, sel_ref, bkt_ref, o_ref,
                 m_ref, a_ref, bias_scr, sa_ref, sb_ref, pa_ref, pb_ref, ala_ref, alb_ref):
    p = pl.program_id(0)
    g = pl.program_id(1)
    tq = qt_ref.shape[1]
    t = tq // 2

    @pl.when(g == 0)
    def _():
        r = lax.broadcasted_iota(I32, (t, t), 0)
        c = lax.broadcasted_iota(I32, (t, t), 1)
        zero = jnp.zeros((t, t), F32)
        for hd in range(2):
            h = 2 * p + hd
            far = rb_ref[(NUM_BUCKETS - 1) * N_HEADS_MOBA + h]
            tiles = []
            for w in range(2):
                bkt = bkt_ref[w]
                acc = jnp.zeros(bkt.shape, F32)
                for kk in range(NUM_BUCKETS):
                    acc = acc + jnp.where(bkt == kk, rb_ref[kk * N_HEADS_MOBA + h], 0.0)
                tiles.append((acc - far) * LOG2E)
            prev_t = tiles[0]
            own_t = jnp.where(r <= c, tiles[1], NEG)
            bias_scr[hd, 0] = jnp.concatenate([prev_t, zero], axis=1)
            bias_scr[hd, 1] = jnp.concatenate([own_t, prev_t], axis=1)
            bias_scr[hd, 2] = jnp.concatenate([zero, own_t], axis=1)

    qt = qt_ref[...]
    top = lax.broadcasted_iota(I32, qt.shape, 0) < HEAD_DIM
    zq = jnp.zeros_like(qt)
    qth = (jnp.where(top, qt, zq), jnp.where(top, zq, qt))
    m_ref[...] = jnp.full(m_ref.shape, M_INIT, F32)
    a_ref[...] = jnp.zeros(a_ref.shape, F32)

    er = lax.broadcasted_iota(I32, (SUM_ROWS, 2 * t), 0)
    ec = lax.broadcasted_iota(I32, (SUM_ROWS, 2 * t), 1)
    ones_rows = jnp.where(((er == 0) & (ec < t)) | ((er == 1) & (ec >= t)), 1.0, 0.0).astype(BF16)
    arow = lax.broadcasted_iota(I32, a_ref.shape, 0)
    head0_rows = (arow < HEAD_DIM) | (arow == PAIR)
    vtop = lax.broadcasted_iota(I32, (PAIR, t), 0) < HEAD_DIM
    in_a = lax.broadcasted_iota(I32, (1, tq), 1) < t

    n_far = jnp.maximum(2 * g - 1, 0)

    def produce(j, s_buf):
        k0 = pl.multiple_of(j * t, t)
        kt = k_ref[pl.ds(k0, t), :]
        for hd in range(2):
            s_buf[hd] = jnp.dot(kt, qth[hd], preferred_element_type=F32)

    def softmax(s_buf, p_buf, al_buf, selrows, w):
        for hd in range(2):
            s = s_buf[hd]
            if w is not None:
                s = s + bias_scr[hd, w]
            smax = jnp.max(s, axis=0, keepdims=True)
            m_old = m_ref[hd]
            m_new = jnp.maximum(m_old, smax + selrows[hd])
            shift = m_new - selrows[hd]
            al_buf[hd] = jnp.exp2(m_old - m_new)
            m_ref[hd] = m_new
            p_buf[hd * t:(hd + 1) * t, :] = jnp.exp2(s - shift).astype(BF16)

    def far_rows(j):
        return [jnp.where(j < n_far, sel_ref[hd, pl.ds(j, 1), :], NEG) for hd in range(2)]

    def accumulate(j, p_buf, al_buf):
        k0 = pl.multiple_of(j * t, t)
        vt = vt_ref[:, pl.ds(k0, t)]
        zv = jnp.zeros_like(vt)
        vcat = jnp.concatenate([jnp.where(vtop, vt, zv), jnp.where(vtop, zv, vt)], axis=1)
        vcat = jnp.concatenate([vcat, ones_rows], axis=0)
        a_ref[...] = (a_ref[...] * jnp.where(head0_rows, al_buf[0], al_buf[1])
                      + jnp.dot(vcat, p_buf[...], preferred_element_type=F32))

    pb_ref[...] = jnp.zeros_like(pb_ref)
    alb_ref[...] = jnp.ones_like(alb_ref)
    produce(0, sa_ref)

    def body(u, carry):
        j0 = 2 * u
        produce(j0 + 1, sb_ref)
        softmax(sa_ref, pa_ref, ala_ref, far_rows(j0), None)
        accumulate(jnp.maximum(j0 - 1, 0), pb_ref, alb_ref)
        produce(j0 + 2, sa_ref)
        accumulate(j0, pa_ref, ala_ref)
        softmax(sb_ref, pb_ref, alb_ref, far_rows(j0 + 1), None)
        return carry

    n_pairs = (n_far + 1) // 2
    lax.fori_loop(0, n_pairs, body, 0)

    j1 = jnp.maximum(2 * g - 1, 0)
    j2 = 2 * g
    j3 = 2 * g + 1
    rows1 = [jnp.where(g >= 1, sel_ref[hd, pl.ds(j1, 1), :], NEG) for hd in range(2)]
    rows2 = [jnp.where(in_a, 0.0, sel_ref[hd, pl.ds(j2, 1), :]) for hd in range(2)]
    rows3 = [jnp.where(in_a, NEG, 0.0)] * 2
    produce(j1, sa_ref)
    produce(j2, sb_ref)
    softmax(sa_ref, pa_ref, ala_ref, rows1, 0)
    accumulate(jnp.maximum(2 * n_pairs - 1, 0), pb_ref, alb_ref)
    produce(j3, sa_ref)
    softmax(sb_ref, pb_ref, alb_ref, rows2, 1)
    accumulate(j1, pa_ref, ala_ref)
    softmax(sa_ref, pa_ref, ala_ref, rows3, 2)
    accumulate(j2, pb_ref, alb_ref)
    accumulate(j3, pa_ref, ala_ref)
    out_t = a_ref[0:PAIR, :] / jnp.where(top, a_ref[PAIR:PAIR + 1, :], a_ref[PAIR + 1:PAIR + 2, :])
    o_ref[...] = out_t.T.astype(o_ref.dtype)


def _moba(qkv, tr, sel, rel_bias):
    s = qkv.shape[0]
    t = MOBA_BLOCK
    npair = N_HEADS_MOBA // 2
    nblk = s // t
    kcol = N_HEADS_FOX // 2
    a = np.arange(t)[None, :]
    b = np.arange(t)[:, None]
    bkt = jnp.asarray(np.stack([_t5_bucket_np(t + a - b), _t5_bucket_np(a - b)]))
    tq = 2 * t
    grid_spec = pltpu.PrefetchScalarGridSpec(
        num_scalar_prefetch=1,
        grid=(npair, s // tq),
        in_specs=[pl.BlockSpec((PAIR, tq), lambda p, i, rb: (p, i)),
                  pl.BlockSpec((s, PAIR), lambda p, i, rb: (0, kcol + p)),
                  pl.BlockSpec((PAIR, s), lambda p, i, rb: (npair + p, 0)),
                  pl.BlockSpec((2, nblk, tq), lambda p, i, rb: (p, 0, i)),
                  pl.BlockSpec((2, t, t), lambda p, i, rb: (0, 0, 0))],
        out_specs=pl.BlockSpec((tq, PAIR), lambda p, i, rb: (i, p)),
        scratch_shapes=[pltpu.VMEM((2, 1, tq), F32),
                        pltpu.VMEM((PAIR + SUM_ROWS, tq), F32), pltpu.VMEM((2, 3, t, tq), F32),
                        pltpu.VMEM((2, t, tq), F32), pltpu.VMEM((2, t, tq), F32),
                        pltpu.VMEM((2 * t, tq), BF16), pltpu.VMEM((2 * t, tq), BF16),
                        pltpu.VMEM((2, 1, tq), F32), pltpu.VMEM((2, 1, tq), F32)],
    )
    return pl.pallas_call(
        _moba_kernel,
        out_shape=jax.ShapeDtypeStruct((s, npair * PAIR), BF16),
        grid_spec=grid_spec,
        compiler_params=_cparams(("arbitrary", "arbitrary")),
        name="moba",
    )(rel_bias.reshape(-1), tr, qkv, tr, sel, bkt)


def _store_token_tiles(ref, val):
    n = val.shape[0]
    for c in range(SUBLANES):
        ref[pl.ds(c, n, stride=SUBLANES), :] = val[:, c * LANES:(c + 1) * LANES]


def _load_token_tiles(ref, n):
    return jnp.concatenate([ref[pl.ds(c, n, stride=SUBLANES), :] for c in range(SUBLANES)], axis=1)


def _tile_rows(r0, n):
    start = r0 * SUBLANES
    if not isinstance(start, int):
        start = pl.multiple_of(start, SUBLANES)
    return pl.ds(start, n * SUBLANES)


def _outproj_kernel(ya_ref, yb_ref, x_ref, wa_ref, wb_ref, gpost_ref, gt_ref, gpre_ref,
                    sc_ref, sh_ref, wr_ref, br_ref,
                    x1_ref, h2_ref, idx_ref, rank_ref, gate_ref, cnt_ref, carry_scr):
    i = pl.program_id(0)
    tm = x_ref.shape[0]

    @pl.when(i == 0)
    def _():
        carry_scr[...] = jnp.zeros_like(carry_scr)

    y = (jnp.dot(ya_ref[...], wa_ref[...], preferred_element_type=F32)
         + jnp.dot(yb_ref[...], wb_ref[...], preferred_element_type=F32))
    x1 = x_ref[...] + gt_ref[...] * (_rms(y) * gpost_ref[...])
    x1_ref[...] = x1
    h2 = _rms(x1) * gpre_ref[...] * (1.0 + sc_ref[...]) + sh_ref[...]
    _store_token_tiles(h2_ref, h2)

    h_hi = h2.astype(BF16)
    h_lo = (h2 - h_hi.astype(F32)).astype(BF16)
    logits = (jnp.dot(h_hi, wr_ref[0], preferred_element_type=F32)
              + jnp.dot(h_hi, wr_ref[1], preferred_element_type=F32)
              + jnp.dot(h_lo, wr_ref[0], preferred_element_type=F32)) + br_ref[...]
    ne = logits.shape[1]
    lane = lax.broadcasted_iota(I32, logits.shape, 1)
    lane4 = lax.broadcasted_iota(I32, (tm, TOP_K), 1)
    g = logits
    mask = jnp.zeros(logits.shape, F32)
    vals, picks = [], []
    for _ in range(TOP_K):
        m = jnp.max(g, axis=1, keepdims=True)
        first = jnp.min(jnp.where(g == m, lane, ne), axis=1, keepdims=True)
        pick = lane == first
        mask = jnp.where(pick, 1.0, mask)
        g = jnp.where(pick, -jnp.inf, g)
        vals.append(m)
        picks.append(pick)
    ex = [jnp.exp(v - vals[0]) for v in vals]
    den = ex[0] + ex[1] + ex[2] + ex[3]
    gates = [e / den for e in ex]

    r = lax.broadcasted_iota(I32, (tm, tm), 0)
    c = lax.broadcasted_iota(I32, (tm, tm), 1)
    tril = jnp.where(c < r, 1.0, 0.0).astype(BF16)
    before = jnp.dot(tril, mask.astype(BF16), preferred_element_type=F32) + carry_scr[...]
    total = carry_scr[...] + jnp.sum(mask, axis=0, keepdims=True)
    carry_scr[...] = total
    cnt_ref[...] = jnp.broadcast_to(total, cnt_ref.shape)

    def pack4(cols):
        return jnp.where(lane4 == 0, cols[0],
                         jnp.where(lane4 == 1, cols[1], jnp.where(lane4 == 2, cols[2], cols[3])))

    ranks = [jnp.sum(jnp.where(pk, before, 0.0), axis=1, keepdims=True) for pk in picks]
    idxs = [jnp.sum(jnp.where(pk, lane, 0), axis=1, keepdims=True) for pk in picks]
    idx_ref[...] = pack4(idxs)
    rank_ref[...] = pack4(ranks).astype(I32)
    gate_ref[...] = pack4(gates)


def _outproj(mix_a, mix_b, x2, w_a, w_b, gpost, gt, gpre, sc, sh, w_router, b_router):
    s, d = x2.shape
    tm = ROW_TILE
    ne = w_router.shape[1]
    wa = mix_a.shape[1]
    w_hi = w_router.astype(BF16)
    w_router_hl = jnp.stack([w_hi, (w_router - w_hi.astype(F32)).astype(BF16)])
    row = lambda i: (i, 0)
    fix = lambda i: (0, 0)
    vec = pl.BlockSpec((1, d), fix)
    return pl.pallas_call(
        _outproj_kernel,
        out_shape=(jax.ShapeDtypeStruct((s, d), F32),
                   jax.ShapeDtypeStruct((s * SUBLANES, LANES), F32),
                   jax.ShapeDtypeStruct((s, TOP_K), I32),
                   jax.ShapeDtypeStruct((s, TOP_K), I32),
                   jax.ShapeDtypeStruct((s, TOP_K), F32),
                   jax.ShapeDtypeStruct((8, ne), F32)),
        grid=(s // tm,),
        in_specs=[pl.BlockSpec((tm, wa), row), pl.BlockSpec((tm, wa), row), pl.BlockSpec((tm, d), row),
                  pl.BlockSpec((wa, d), fix), pl.BlockSpec((wa, d), fix),
                  vec, vec, vec, vec, vec,
                  pl.BlockSpec((2, d, ne), lambda i: (0, 0, 0)), pl.BlockSpec((1, ne), fix)],
        out_specs=(pl.BlockSpec((tm, d), row), pl.BlockSpec((tm * SUBLANES, LANES), row),
                   pl.BlockSpec((tm, TOP_K), row), pl.BlockSpec((tm, TOP_K), row),
                   pl.BlockSpec((tm, TOP_K), row), pl.BlockSpec((8, ne), fix)),
        scratch_shapes=[pltpu.VMEM((1, ne), F32)],
        compiler_params=_cparams(("arbitrary",)),
        name="outproj",
    )(mix_a, mix_b, x2, w_a, w_b, gpost, gt, gpre, sc, sh, w_router_hl, b_router)


def _dest_kernel(idx_ref, rank_ref, pstart_ref, o_ref):
    idx = idx_ref[...]
    tm = idx.shape[0]
    ne = pstart_ref.shape[1]
    lane = lax.broadcasted_iota(I32, (tm, ne), 1)
    lane4 = lax.broadcasted_iota(I32, idx.shape, 1)
    out = rank_ref[...]
    for k in range(TOP_K):
        start = jnp.sum(jnp.where(lane == idx[:, k:k + 1], pstart_ref[...], 0), axis=1, keepdims=True)
        out = out + jnp.where(lane4 == k, start, 0)
    o_ref[...] = out


def _dest(idx4, rank4, pstart):
    s = idx4.shape[0]
    tm = min(4 * ROW_TILE, s)
    row = lambda i: (i, 0)
    return pl.pallas_call(
        _dest_kernel,
        out_shape=jax.ShapeDtypeStruct((s, TOP_K), I32),
        grid=(s // tm,),
        in_specs=[pl.BlockSpec((tm, TOP_K), row), pl.BlockSpec((tm, TOP_K), row),
                  pl.BlockSpec((1, N_EXPERTS), lambda i: (0, 0))],
        out_specs=pl.BlockSpec((tm, TOP_K), row),
        compiler_params=_cparams(("arbitrary",)),
        name="dest",
    )(idx4, rank4, pstart.reshape(1, -1))


def _dispatch_kernel(pstart_ref, pblk_ref, nu_ref, dest_ref, h_ref, xs_ref,
                     zero_scr, sem, zsem):
    tm = h_ref.shape[0] // SUBLANES
    bm = zero_scr.shape[0] // SUBLANES
    n_blk = xs_ref.shape[0] // (bm * SUBLANES)

    rows = _tile_rows

    @pl.when(pl.program_id(0) == 0)
    def _():
        zero_scr[...] = jnp.zeros_like(zero_scr)

        def zero_copy(row0):
            return pltpu.make_async_copy(zero_scr, xs_ref.at[rows(row0, bm)], zsem)

        for phase in range(2):
            for e in range(N_EXPERTS):
                last = pstart_ref[e] + (pblk_ref[e] - 1) * bm
                tail = (n_blk - N_EXPERTS + e) * bm
                for cond, row0 in ((pblk_ref[e] > 0, last), (n_blk - N_EXPERTS + e >= nu_ref[0], tail)):
                    @pl.when(cond)
                    def _():
                        if phase == 0:
                            zero_copy(row0).start()
                        else:
                            zero_copy(row0).wait()

    def row_copy(r, k):
        dst = dest_ref[r * TOP_K + k]
        return pltpu.make_async_copy(h_ref.at[rows(r, 1)], xs_ref.at[rows(dst, 1)], sem)

    def start(r, carry):
        for k in range(TOP_K):
            row_copy(r, k).start(priority=k % 2)
        return carry

    lax.fori_loop(0, tm, start, 0, unroll=4)
    for k in range(TOP_K):
        pltpu.make_async_copy(h_ref, xs_ref.at[rows(0, tm)], sem).wait()


def _dispatch(pstart, pblk, n_used, dest_flat, h2, n_rows):
    s = h2.shape[0] // SUBLANES
    tm = COMBINE_TILE
    grid_spec = pltpu.PrefetchScalarGridSpec(
        num_scalar_prefetch=3,
        grid=(s // tm,),
        in_specs=[pl.BlockSpec((tm * TOP_K,), lambda i, *_: (i,), memory_space=pltpu.SMEM),
                  pl.BlockSpec((tm * SUBLANES, LANES), lambda i, *_: (i, 0))],
        out_specs=pl.BlockSpec(memory_space=pl.ANY),
        scratch_shapes=[pltpu.VMEM((EXPERT_BLOCK * SUBLANES, LANES), F32), pltpu.SemaphoreType.DMA,
                        pltpu.SemaphoreType.DMA],
    )
    return pl.pallas_call(
        _dispatch_kernel,
        out_shape=jax.ShapeDtypeStruct((n_rows * SUBLANES, LANES), F32),
        grid_spec=grid_spec,
        compiler_params=_cparams(("arbitrary",)),
        name="dispatch",
    )(pstart, pblk, n_used, dest_flat, h2)


def _experts_kernel(be_ref, nu_ref, xs_ref, wgu_ref, bgu_ref, wd_ref, bd_ref, y_ref,
                    wgu_bf, wd_bf):
    b = pl.program_id(0)
    d_exp = wd_ref.shape[1]
    prev = be_ref[jnp.maximum(b - 1, 0)]
    changed = (b == 0) | (be_ref[b] != prev)

    @pl.when((b < nu_ref[0]) & changed)
    def _():
        rows = 128

        def cast_gu(c, carry):
            r0 = pl.multiple_of(c * rows, rows)
            wgu_bf[pl.ds(r0, rows), :] = wgu_ref[0, pl.ds(r0, rows), :].astype(BF16)
            return carry

        def cast_d(c, carry):
            r0 = pl.multiple_of(c * rows, rows)
            wd_bf[pl.ds(r0, rows), :] = wd_ref[0, pl.ds(r0, rows), :].astype(BF16)
            return carry

        lax.fori_loop(0, wgu_ref.shape[1] // rows, cast_gu, 0)
        lax.fori_loop(0, wd_ref.shape[1] // rows, cast_d, 0)

    @pl.when(b < nu_ref[0])
    def _():
        bm = xs_ref.shape[0] // SUBLANES
        xb = _load_token_tiles(xs_ref, bm).astype(BF16)
        hdn = jnp.dot(xb, wgu_bf[...], preferred_element_type=F32) + bgu_ref[0]
        x_glu = jnp.minimum(hdn[:, :d_exp], SWIGLU_LIMIT)
        x_lin = jnp.clip(hdn[:, d_exp:], -SWIGLU_LIMIT, SWIGLU_LIMIT)
        act = x_glu * jax.nn.sigmoid(SWIGLU_ALPHA * x_glu) * (x_lin + 1.0)
        _store_token_tiles(y_ref, jnp.dot(act.astype(BF16), wd_bf[...], preferred_element_type=F32)
                           + bd_ref[0])

    @pl.when(b >= nu_ref[0])
    def _():
        y_ref[...] = jnp.zeros_like(y_ref)


def _experts(block_e, n_used, xs, w_gate_up, b_gate_up, w_down, b_down):
    n_rows = xs.shape[0] // SUBLANES
    bm = EXPERT_BLOCK
    n_blk = n_rows // bm
    ne, d, two_de = w_gate_up.shape
    de = w_down.shape[1]
    assert d == SUBLANES * LANES

    def blk(b, be, nu):
        return jnp.minimum(b, nu[0] - 1)

    grid_spec = pltpu.PrefetchScalarGridSpec(
        num_scalar_prefetch=2,
        grid=(n_blk,),
        in_specs=[pl.BlockSpec((bm * SUBLANES, LANES), lambda b, be, nu: (blk(b, be, nu), 0)),
                  pl.BlockSpec((1, d, two_de), lambda b, be, nu: (be[blk(b, be, nu)], 0, 0)),
                  pl.BlockSpec((1, 1, two_de), lambda b, be, nu: (be[blk(b, be, nu)], 0, 0)),
                  pl.BlockSpec((1, de, d), lambda b, be, nu: (be[blk(b, be, nu)], 0, 0)),
                  pl.BlockSpec((1, 1, d), lambda b, be, nu: (be[blk(b, be, nu)], 0, 0))],
        out_specs=pl.BlockSpec((bm * SUBLANES, LANES), lambda b, be, nu: (b, 0)),
        scratch_shapes=[pltpu.VMEM((d, two_de), BF16), pltpu.VMEM((de, d), BF16)],
    )
    return pl.pallas_call(
        _experts_kernel,
        out_shape=jax.ShapeDtypeStruct((n_rows * SUBLANES, LANES), F32),
        grid_spec=grid_spec,
        compiler_params=_cparams(("arbitrary",)),
        name="experts",
    )(block_e, n_used, xs, w_gate_up, b_gate_up.reshape(ne, 1, two_de), w_down, b_down.reshape(ne, 1, d))


def _combine_kernel(dest_ref, y_ref, gate_ref, x1_ref, gt_ref, gpost_ref, o_ref, buf, sem):
    tm = x1_ref.shape[0]

    def row_copy(r, k):
        src = dest_ref[r * TOP_K + k]
        return pltpu.make_async_copy(y_ref.at[_tile_rows(src, 1)], buf.at[k, _tile_rows(r, 1)], sem)

    def start(r, carry):
        for k in range(TOP_K):
            row_copy(r, k).start(priority=k % 2)
        return carry

    lax.fori_loop(0, tm, start, 0, unroll=4)
    for k in range(TOP_K):
        pltpu.make_async_copy(y_ref.at[_tile_rows(0, tm)], buf.at[k], sem).wait()

    gate = gate_ref[...]
    acc = gate[:, 0:1] * _load_token_tiles(buf.at[0], tm)
    for k in range(1, TOP_K):
        acc = acc + gate[:, k:k + 1] * _load_token_tiles(buf.at[k], tm)
    o_ref[...] = x1_ref[...] + gt_ref[...] * (_rms(acc) * gpost_ref[...])


def _combine(dest_flat, y, gate4, x1, gt, gpost):
    s, d = x1.shape
    tm = COMBINE_TILE
    return pl.pallas_call(
        _combine_kernel,
        out_shape=jax.ShapeDtypeStruct((s, d), F32),
        grid=(s // tm,),
        in_specs=[pl.BlockSpec((tm * TOP_K,), lambda i: (i,), memory_space=pltpu.SMEM),
                  pl.BlockSpec(memory_space=pl.ANY),
                  pl.BlockSpec((tm, TOP_K), lambda i: (i, 0)),
                  pl.BlockSpec((tm, d), lambda i: (i, 0)),
                  pl.BlockSpec((1, d), lambda i: (0, 0)),
                  pl.BlockSpec((1, d), lambda i: (0, 0))],
        out_specs=pl.BlockSpec((tm, d), lambda i: (i, 0)),
        scratch_shapes=[pltpu.VMEM((TOP_K, tm * SUBLANES, LANES), F32), pltpu.SemaphoreType.DMA],
        compiler_params=_cparams(("arbitrary",)),
        name="combine",
    )(dest_flat, y, gate4, x1, gt, gpost)


def _layer(x2, mod, g_pre_mix, g_post_mix, w_in, b_forget, rel_bias, w_out,
           g_pre_ffn, g_post_ffn, w_router, b_router, w_gate_up, b_gate_up, w_down, b_down):
    s, d = x2.shape
    sh_m, sc_m, gt_m, sh_f, sc_f, gt_f = [mod[:, k * d:(k + 1) * d] for k in range(6)]
    n_qkv = 3 * (N_HEADS_FOX + N_HEADS_MOBA) * HEAD_DIM
    fox_w = N_HEADS_FOX * HEAD_DIM

    w_qkv = w_in[:, :n_qkv].astype(BF16)
    w_f = w_in[:, n_qkv:].T
    qkv, cum, sel, nrm, tr, nr = _inproj(x2, g_pre_mix.reshape(1, d), sc_m, sh_m, w_qkv, w_f,
                                         b_forget.reshape(-1, 1))
    y_a = _fox(qkv, tr, nr, cum, nrm)
    y_b = _moba(qkv, tr, sel, rel_bias)

    w_out_bf = w_out.astype(BF16)
    x1, h2, idx4, rank4, gate4, cnt = _outproj(
        y_a, y_b, x2, w_out_bf[:fox_w], w_out_bf[fox_w:], g_post_mix.reshape(1, d), gt_m,
        g_pre_ffn.reshape(1, d), sc_f, sh_f, w_router, b_router.reshape(1, -1))

    bm = EXPERT_BLOCK
    counts = cnt[0].astype(I32)
    pblk = (counts + bm - 1) // bm
    pend_blk = jnp.cumsum(pblk)
    pstart = ((pend_blk - pblk) * bm).astype(I32)
    n_rows = s * TOP_K + N_EXPERTS * bm
    n_blk = n_rows // bm
    block_e = jnp.minimum(jnp.sum(pend_blk[None, :] <= jnp.arange(n_blk)[:, None], axis=1),
                          N_EXPERTS - 1).astype(I32)
    n_used = pend_blk[-1:].astype(I32)

    dest_flat = _dest(idx4, rank4, pstart).reshape(-1)
    xs = _dispatch(pstart, pblk.astype(I32), n_used, dest_flat, h2, n_rows)
    y = _experts(block_e, n_used, xs, w_gate_up, b_gate_up, w_down, b_down)
    return _combine(dest_flat, y, gate4, x1, gt_f, g_post_ffn.reshape(1, d))


def kernel(x, c, w_ada, b_ada, g_pre_mix, g_post_mix, w_in, b_forget, rel_bias, w_out, g_pre_ffn, g_post_ffn, w_router, b_router, w_gate_up, b_gate_up, w_down, b_down):
    bsz, s, d = x.shape
    depth = w_ada.shape[0]
    outs = []
    for bi in range(bsz):
        x2 = x[bi]
        for l in range(depth):
            mod = _adaln(c[bi:bi + 1], w_ada[l], b_ada[l])
            x2 = _layer(x2, mod, g_pre_mix[l], g_post_mix[l], w_in[l], b_forget[l], rel_bias, w_out[l],
                        g_pre_ffn[l], g_post_ffn[l], w_router[l], b_router[l], w_gate_up[l], b_gate_up[l],
                        w_down[l], b_down[l])
        outs.append(x2)
    return outs[0].reshape(1, s, d) if bsz == 1 else jnp.stack(outs)
```

```python
import functools
import math

import numpy as np
import jax
import jax.numpy as jnp
from jax import lax
from jax.experimental import pallas as pl
from jax.experimental.pallas import tpu as pltpu

F32 = jnp.float32
BF16 = jnp.bfloat16
I32 = jnp.int32

HEAD_DIM = 64
N_HEADS_FOX = 8
N_HEADS_MOBA = 8
PAIR = 2 * HEAD_DIM
MOBA_BLOCK = 256
MOBA_TOPK = 3
NUM_BUCKETS = 32
MAX_DISTANCE = 128
N_EXPERTS = 32
TOP_K = 4
SWIGLU_LIMIT = 7.0
SWIGLU_ALPHA = 1.702
RMS_EPS = 1e-6
NEG = -(2.0 ** 100)
M_INIT = -(2.0 ** 99)
LOG2E = math.log2(math.e)
SUM_ROWS = 16
SUBLANES = 8
LANES = 128
EXP_UNDERFLOW = 90.0
VMEM_LIMIT = 56 * 1024 * 1024

ROW_TILE = 512
FOX_TILE = 256
EXPERT_BLOCK = 512
COMBINE_TILE = 512
MOBA_UNROLL = 4

NT_DIMS = (((1,), (1,)), ((), ()))


def _cparams(sem):
    return pltpu.CompilerParams(dimension_semantics=sem, vmem_limit_bytes=VMEM_LIMIT)


def _rms(x):
    return x * lax.rsqrt(jnp.mean(x * x, axis=-1, keepdims=True) + RMS_EPS)


def _adaln_kernel(c_ref, w_ref, b_ref, o_ref):
    c = c_ref[...]
    cond = c * jax.nn.sigmoid(c)
    o_ref[...] = jnp.dot(cond, w_ref[...], preferred_element_type=F32,
                         precision=lax.Precision.HIGHEST) + b_ref[...]


def _adaln(c, w_ada, b_ada):
    d = c.shape[-1]
    n = w_ada.shape[-1]
    c8 = jnp.broadcast_to(c.reshape(1, d), (8, d))
    out = pl.pallas_call(
        _adaln_kernel,
        out_shape=jax.ShapeDtypeStruct((8, n), F32),
        grid=(n // d,),
        in_specs=[pl.BlockSpec((8, d), lambda j: (0, 0)),
                  pl.BlockSpec((d, d), lambda j: (0, j)),
                  pl.BlockSpec((1, d), lambda j: (0, j))],
        out_specs=pl.BlockSpec((8, d), lambda j: (0, j)),
        compiler_params=_cparams(("arbitrary",)),
        name="adaln",
    )(c8, w_ada, b_ada.reshape(1, n))
    return out[0:1]


def _inproj_kernel(x_ref, g_ref, sc_ref, sh_ref, w_ref, wf_ref, bf_ref,
                   qkv_ref, cum_ref, sel_ref, nrm_ref, tr_ref, nr_ref, km_scr, carry_scr):
    i = pl.program_id(0)
    tm = x_ref.shape[0]
    nblk = km_scr.shape[0]

    @pl.when(i == 0)
    def _():
        km_scr[...] = jnp.zeros_like(km_scr)
        carry_scr[...] = jnp.zeros_like(carry_scr)

    x = x_ref[...]
    h = _rms(x) * g_ref[...] * (1.0 + sc_ref[...]) + sh_ref[...]
    hb = h.astype(BF16)

    width = N_HEADS_FOX * HEAD_DIM
    hsel = jnp.where(lax.broadcasted_iota(I32, (width, N_HEADS_FOX), 0) // HEAD_DIM
                     == lax.broadcasted_iota(I32, (width, N_HEADS_FOX), 1), 1.0, 0.0)
    kb = None
    tr_slot = {3: 0, 5: 1, 0: 2, 2: 3}
    qbt = None
    for c in range(6):
        pc = jnp.dot(hb, w_ref[:, c * width:(c + 1) * width], preferred_element_type=F32)
        if c in tr_slot:
            n = tr_slot[c]
            pct = pc.T
            if c == 3:
                qbt = pct
            if c == 0 or c == 3:
                pct = pct * (LOG2E * HEAD_DIM ** -0.5)
            tr_ref[n * width:(n + 1) * width, :] = pct.astype(BF16)
        if c == 1 or c == 4:
            qkv_ref[:, (c // 3) * width:(c // 3 + 1) * width] = pc.astype(BF16)
        if c < 2:
            sq = (pc * (HEAD_DIM ** -0.5) if c == 0 else pc).astype(BF16).astype(F32)
            n2 = jnp.dot((sq * sq).astype(BF16), hsel.astype(BF16), preferred_element_type=F32)
            nrm_ref[0, c:c + 1, :] = jnp.max(n2, axis=0, keepdims=True)
        if c == 4:
            kb = pc

    ft = lax.dot_general(wf_ref[...].astype(BF16), hb, NT_DIMS, preferred_element_type=F32)
    z = ft + bf_ref[...]
    logf = -(jnp.maximum(-z, 0.0) + jnp.log1p(jnp.exp(-jnp.abs(z))))
    lane = lax.broadcasted_iota(I32, logf.shape, 1)
    cs = logf
    sh = 1
    while sh < tm:
        cs = cs + jnp.where(lane >= sh, pltpu.roll(cs, sh, axis=1), 0.0)
        sh *= 2
    base = jnp.zeros_like(cs)
    for b in range(1, tm // FOX_TILE):
        base = jnp.where(lane >= b * FOX_TILE, cs[:, b * FOX_TILE - 1:b * FOX_TILE], base)
    nr_t = (-LOG2E * (cs - base)).T
    for pp in range(nr_ref.shape[0]):
        nr_ref[pp] = nr_t[:, 2 * pp:2 * pp + 2]
    cs = cs + carry_scr[...]
    cum_ref[...] = cs
    carry_scr[...] = cs[:, tm - 1:tm]

    nb_tile = tm // MOBA_BLOCK
    for b in range(nb_tile):
        kmean = jnp.sum(kb[b * MOBA_BLOCK:(b + 1) * MOBA_BLOCK], axis=0, keepdims=True) * (1.0 / MOBA_BLOCK)
        km_scr[pl.ds(i * nb_tile + b, 1), :] = kmean

    km = km_scr[...]
    blk = lax.broadcasted_iota(I32, (nblk, tm), 0)
    col = lax.broadcasted_iota(I32, (nblk, tm), 1)
    own = i * nb_tile + col // MOBA_BLOCK
    for hd in range(N_HEADS_MOBA):
        hs = slice(hd * HEAD_DIM, (hd + 1) * HEAD_DIM)
        g = jnp.dot(km[:, hs], qbt[hs, :], preferred_element_type=F32,
                    precision=lax.Precision.HIGHEST)
        g = jnp.where(blk < own, g, -jnp.inf)
        sel = jnp.zeros(g.shape, dtype=jnp.bool_)
        for _ in range(MOBA_TOPK):
            m = jnp.max(g, axis=0, keepdims=True)
            first = jnp.min(jnp.where(g == m, blk, nblk), axis=0, keepdims=True)
            pick = (blk == first) & (m > -jnp.inf)
            sel = sel | pick
            g = jnp.where(pick, -jnp.inf, g)
        sel_ref[hd] = jnp.where(sel, 0.0, NEG)


def _inproj(x2, g, sc, sh, w_qkv, w_f, b_f):
    s, d = x2.shape
    tm = ROW_TILE
    nblk = s // MOBA_BLOCK
    n = w_qkv.shape[1]
    width = N_HEADS_FOX * HEAD_DIM
    n_rows_out = 2 * width
    n_tr = 4 * width
    assert tm % FOX_TILE == 0 and N_HEADS_FOX == N_HEADS_MOBA
    return pl.pallas_call(
        _inproj_kernel,
        out_shape=(jax.ShapeDtypeStruct((s, n_rows_out), BF16),
                   jax.ShapeDtypeStruct((N_HEADS_FOX, s), F32),
                   jax.ShapeDtypeStruct((N_HEADS_MOBA, nblk, s), F32),
                   jax.ShapeDtypeStruct((s // tm, 2, N_HEADS_FOX), F32),
                   jax.ShapeDtypeStruct((n_tr, s), BF16),
                   jax.ShapeDtypeStruct((N_HEADS_FOX // 2, s, 2), F32)),
        grid=(s // tm,),
        in_specs=[pl.BlockSpec((tm, d), lambda i: (i, 0)),
                  pl.BlockSpec((1, d), lambda i: (0, 0)),
                  pl.BlockSpec((1, d), lambda i: (0, 0)),
                  pl.BlockSpec((1, d), lambda i: (0, 0)),
                  pl.BlockSpec((d, n), lambda i: (0, 0)),
                  pl.BlockSpec((N_HEADS_FOX, d), lambda i: (0, 0)),
                  pl.BlockSpec((N_HEADS_FOX, 1), lambda i: (0, 0))],
        out_specs=(pl.BlockSpec((tm, n_rows_out), lambda i: (i, 0)),
                   pl.BlockSpec((N_HEADS_FOX, tm), lambda i: (0, i)),
                   pl.BlockSpec((N_HEADS_MOBA, nblk, tm), lambda i: (0, 0, i)),
                   pl.BlockSpec((1, 2, N_HEADS_FOX), lambda i: (i, 0, 0)),
                   pl.BlockSpec((n_tr, tm), lambda i: (0, i)),
                   pl.BlockSpec((N_HEADS_FOX // 2, tm, 2), lambda i: (0, i, 0))),
        scratch_shapes=[pltpu.VMEM((nblk, N_HEADS_MOBA * HEAD_DIM), F32),
                        pltpu.VMEM((N_HEADS_FOX, 1), F32)],
        compiler_params=_cparams(("arbitrary",)),
        name="inproj",
    )(x2, g, sc, sh, w_qkv, w_f, b_f)


def _fox_kernel(jlo_ref, cp_ref, qt_ref, k_ref, vt_ref, nr_ref, o_ref, m_ref, a_ref):
    p = pl.program_id(0)
    i = pl.program_id(1)
    j_first = jlo_ref[p * pl.num_programs(1) + i]
    t = qt_ref.shape[1]

    qt = qt_ref[...]
    top = lax.broadcasted_iota(I32, qt.shape, 0) < HEAD_DIM
    zq = jnp.zeros_like(qt)
    qth = (jnp.where(top, qt, zq), jnp.where(top, zq, qt))
    m_ref[...] = jnp.full(m_ref.shape, M_INIT, F32)
    a_ref[...] = jnp.zeros(a_ref.shape, F32)
    er = lax.broadcasted_iota(I32, (SUM_ROWS, 2 * t), 0)
    ec = lax.broadcasted_iota(I32, (SUM_ROWS, 2 * t), 1)
    ones_rows = jnp.where(((er == 0) & (ec < t)) | ((er == 1) & (ec >= t)), 1.0, 0.0).astype(BF16)
    arow = lax.broadcasted_iota(I32, a_ref.shape, 0)
    head0_rows = (arow < HEAD_DIM) | (arow == PAIR)

    def scores(j, diagonal):
        k0 = pl.multiple_of(j * t, t)
        kt = k_ref[pl.ds(k0, t), :]
        out = []
        for hd in range(2):
            s = jnp.dot(kt, qth[hd], preferred_element_type=F32) + nr_ref[0, pl.ds(k0, t), hd:hd + 1]
            if diagonal:
                r = lax.broadcasted_iota(I32, s.shape, 0)
                c = lax.broadcasted_iota(I32, s.shape, 1)
                s = jnp.where(r <= c, s, NEG)
            out.append(s)
        return out

    def softmax(ss, j, valid):
        ps, alphas = [], []
        for hd in range(2):
            h = 2 * p + hd
            pair_shift = LOG2E * (cp_ref[h, i] - cp_ref[h, j])
            if valid is not None:
                pair_shift = jnp.where(valid, pair_shift, NEG)
            m_old = m_ref[hd]
            m_new = jnp.maximum(m_old, jnp.max(ss[hd], axis=0, keepdims=True) + pair_shift)
            alphas.append(jnp.exp2(m_old - m_new))
            m_ref[hd] = m_new
            ps.append(jnp.exp2(ss[hd] - (m_new - pair_shift)).astype(BF16))
        return ps, alphas

    def accumulate(j, ps, alphas):
        k0 = pl.multiple_of(j * t, t)
        vt = vt_ref[:, pl.ds(k0, t)]
        zv = jnp.zeros_like(vt)
        vcat = jnp.concatenate([jnp.where(top, vt, zv), jnp.where(top, zv, vt)], axis=1)
        vcat = jnp.concatenate([vcat, ones_rows], axis=0)
        pcat = jnp.concatenate(ps, axis=0)
        a_ref[...] = (a_ref[...] * jnp.where(head0_rows, alphas[0], alphas[1])
                      + jnp.dot(vcat, pcat, preferred_element_type=F32))

    def body(j, carry):
        accumulate(j, *softmax(scores(j, False), j, None))
        return carry

    lax.fori_loop(j_first, i - 1, body, 0)
    jp = jnp.maximum(i - 1, 0)
    ss_prev = scores(jp, False)
    ss_diag = scores(i, True)
    w_prev = softmax(ss_prev, jp, i >= 1)
    w_diag = softmax(ss_diag, i, None)
    accumulate(jp, *w_prev)
    accumulate(i, *w_diag)
    out_t =a_ref[0:PAIR, :] / jnp.where(top, a_ref[PAIR:PAIR + 1, :], a_ref[PAIR + 1:PAIR + 2, :])
    o_ref[...] = out_t.T.astype(o_ref.dtype)


def _fox_first_tile(cum, nrm, t):
    cend = cum[:, t - 1::t]
    nt = cend.shape[1]
    cprev = jnp.concatenate([jnp.zeros((cend.shape[0], 1), F32), cend[:, :-1]], axis=1)
    rep = nt // nrm.shape[0]
    qn = jnp.repeat(jnp.sqrt(nrm[:, 0, :]).T, rep, axis=1)
    kn = jnp.repeat(jnp.sqrt(nrm[:, 1, :]).T, rep, axis=1)
    gap = (1.02 * qn[:, :, None] * (kn[:, None, :] + kn[:, :, None])
           + cprev[:, :, None] - cend[:, None, :])
    jj = jnp.arange(nt)[None, None, :]
    ii = jnp.arange(nt)[None, :, None]
    needed = (jj < ii) & jnp.logical_not(gap <= -EXP_UNDERFLOW)
    needed = needed[0::2] | needed[1::2]
    first = jnp.min(jnp.where(needed, jj, ii), axis=2)
    return first.reshape(-1).astype(I32)


def _fox(qkv, tr, nr, cum, nrm):
    s = qkv.shape[0]
    t = FOX_TILE
    npair = N_HEADS_FOX // 2
    jlo = _fox_first_tile(cum, nrm, t)
    cend = cum[:, t - 1::t]
    cprev = jnp.concatenate([jnp.zeros((cend.shape[0], 1), F32), cend[:, :-1]], axis=1)
    q_rows = 2 * (N_HEADS_MOBA // 2)
    grid_spec = pltpu.PrefetchScalarGridSpec(
        num_scalar_prefetch=1,
        grid=(npair, s // t),
        in_specs=[pl.BlockSpec(memory_space=pltpu.SMEM),
                  pl.BlockSpec((PAIR, t), lambda p, i, jl: (q_rows + p, i)),
                  pl.BlockSpec((s, PAIR), lambda p, i, jl: (0, p)),
                  pl.BlockSpec((PAIR, s), lambda p, i, jl: (q_rows + npair + p, 0)),
                  pl.BlockSpec((1, s, 2), lambda p, i, jl: (p, 0, 0))],
        out_specs=pl.BlockSpec((t, PAIR), lambda p, i, jl: (i, p)),
        scratch_shapes=[pltpu.VMEM((2, 1, t), F32), pltpu.VMEM((PAIR + SUM_ROWS, t), F32)],
    )
    return pl.pallas_call(
        _fox_kernel,
        out_shape=jax.ShapeDtypeStruct((s, npair * PAIR), BF16),
        grid_spec=grid_spec,
        compiler_params=_cparams(("arbitrary", "arbitrary")),
        name="fox",
    )(jlo, cprev, tr, qkv, tr, nr)


def _t5_bucket_np(dist):
    dist = np.maximum(dist, 0)
    max_exact = NUM_BUCKETS // 2
    d = np.maximum(dist, 1).astype(np.float32)
    large = max_exact + (np.log(d / np.float32(max_exact)) / np.float32(math.log(MAX_DISTANCE / max_exact))
                         * np.float32(NUM_BUCKETS - max_exact)).astype(np.int32)
    large = np.minimum(large, NUM_BUCKETS - 1)
    return np.where(dist < max_exact, dist, large).astype(np.int32)


def _moba_kernel(rb_ref, qt_ref, k_ref, vt_ref, sel_ref, bkt_ref, o_ref,
                 m_ref, a_ref, bias_scr, *bufs):
    s_bufs = bufs[0:MOBA_UNROLL]
    p_bufs = bufs[MOBA_UNROLL:2 * MOBA_UNROLL]
    al_bufs = bufs[2 * MOBA_UNROLL:3 * MOBA_UNROLL]
    p = pl.program_id(0)
    g = pl.program_id(1)
    tq = qt_ref.shape[1]
    t = tq // 2

    @pl.when(g == 0)
    def _():
        r = lax.broadcasted_iota(I32, (t, t), 0)
        c = lax.broadcasted_iota(I32, (t, t), 1)
        zero = jnp.zeros((t, t), F32)
        for hd in range(2):
            h = 2 * p + hd
            far = rb_ref[(NUM_BUCKETS - 1) * N_HEADS_MOBA + h]
            tiles = []
            for w in range(2):
                bkt = bkt_ref[w]
                acc = jnp.zeros(bkt.shape, F32)
                for kk in range(NUM_BUCKETS):
                    acc = acc + jnp.where(bkt == kk, rb_ref[kk * N_HEADS_MOBA + h], 0.0)
                tiles.append((acc - far) * LOG2E)
            prev_t = tiles[0]
            own_t = jnp.where(r <= c, tiles[1], NEG)
            bias_scr[hd, 0] = jnp.concatenate([prev_t, zero], axis=1)
            bias_scr[hd, 1] = jnp.concatenate([own_t, prev_t], axis=1)
            bias_scr[hd, 2] = jnp.concatenate([zero, own_t], axis=1)

    qt = qt_ref[...]
    top = lax.broadcasted_iota(I32, qt.shape, 0) < HEAD_DIM
    zq = jnp.zeros_like(qt)
    qth = (jnp.where(top, qt, zq), jnp.where(top, zq, qt))
    m_ref[...] = jnp.full(m_ref.shape, M_INIT, F32)
    a_ref[...] = jnp.zeros(a_ref.shape, F32)

    er = lax.broadcasted_iota(I32, (SUM_ROWS, 2 * t), 0)
    ec = lax.broadcasted_iota(I32, (SUM_ROWS, 2 * t), 1)
    ones_rows = jnp.where(((er == 0) & (ec < t)) | ((er == 1) & (ec >= t)), 1.0, 0.0).astype(BF16)
    arow = lax.broadcasted_iota(I32, a_ref.shape, 0)
    head0_rows = (arow < HEAD_DIM) | (arow == PAIR)
    vtop = lax.broadcasted_iota(I32, (PAIR, t), 0) < HEAD_DIM
    in_a = lax.broadcasted_iota(I32, (1, tq), 1) < t

    n_far = jnp.maximum(2 * g - 1, 0)

    n_key_tiles = k_ref.shape[0] // t

    def produce(j, s_buf):
        j = jnp.minimum(j, n_key_tiles - 1)
        k0 = pl.multiple_of(j * t, t)
        kt = k_ref[pl.ds(k0, t), :]
        for hd in range(2):
            s_buf[hd] = jnp.dot(kt, qth[hd], preferred_element_type=F32)

    def softmax(s_buf, p_buf, al_buf, selrows, w):
        for hd in range(2):
            s = s_buf[hd]
            if w is not None:
                s = s + bias_scr[hd, w]
            smax = jnp.max(s, axis=0, keepdims=True)
            m_old = m_ref[hd]
            m_new = jnp.maximum(m_old, smax + selrows[hd])
            shift = m_new - selrows[hd]
            al_buf[hd] = jnp.exp2(m_old - m_new)
            m_ref[hd] = m_new
            p_buf[hd * t:(hd + 1) * t, :] = jnp.exp2(s - shift).astype(BF16)

    def far_rows(j):
        return [jnp.where(j < n_far, sel_ref[hd, pl.ds(j, 1), :], NEG) for hd in range(2)]

    def accumulate(j, p_buf, al_buf):
        k0 = pl.multiple_of(j * t, t)
        vt = vt_ref[:, pl.ds(k0, t)]
        zv = jnp.zeros_like(vt)
        vcat = jnp.concatenate([jnp.where(vtop, vt, zv), jnp.where(vtop, zv, vt)], axis=1)
        vcat = jnp.concatenate([vcat, ones_rows], axis=0)
        a_ref[...] = (a_ref[...] * jnp.where(head0_rows, al_buf[0], al_buf[1])
                      + jnp.dot(vcat, p_buf[...], preferred_element_type=F32))

    un = MOBA_UNROLL
    p_bufs[un - 1][...] = jnp.zeros_like(p_bufs[un - 1])
    al_bufs[un - 1][...] = jnp.ones_like(al_bufs[un - 1])
    produce(0, s_bufs[0])

    def body(u, carry):
        j0 = un * u
        for n in range(un):
            produce(j0 + n + 1, s_bufs[(n + 1) % un])
            softmax(s_bufs[n], p_bufs[n], al_bufs[n], far_rows(j0 + n), None)
            accumulate(jnp.maximum(j0 + n - 1, 0), p_bufs[(n - 1) % un], al_bufs[(n - 1) % un])
        return carry

    n_trips = (n_far + un - 1) // un
    lax.fori_loop(0, n_trips, body, 0)

    j1 = jnp.maximum(2 * g - 1, 0)
    j2 = 2 * g
    j3 = 2 * g + 1
    rows1 = [jnp.where(g >= 1, sel_ref[hd, pl.ds(j1, 1), :], NEG) for hd in range(2)]
    rows2 = [jnp.where(in_a, 0.0, sel_ref[hd, pl.ds(j2, 1), :]) for hd in range(2)]
    rows3 = [jnp.where(in_a, NEG, 0.0)] * 2
    produce(j1, s_bufs[0])
    produce(j2, s_bufs[1])
    softmax(s_bufs[0], p_bufs[0], al_bufs[0], rows1, 0)
    accumulate(jnp.maximum(un * n_trips - 1, 0), p_bufs[un - 1], al_bufs[un - 1])
    produce(j3, s_bufs[0])
    softmax(s_bufs[1], p_bufs[1], al_bufs[1], rows2, 1)
    accumulate(j1, p_bufs[0], al_bufs[0])
    softmax(s_bufs[0], p_bufs[2], al_bufs[2], rows3, 2)
    accumulate(j2, p_bufs[1], al_bufs[1])
    accumulate(j3, p_bufs[2], al_bufs[2])
    out_t = a_ref[0:PAIR, :] / jnp.where(top, a_ref[PAIR:PAIR + 1, :], a_ref[PAIR + 1:PAIR + 2, :])
    o_ref[...] = out_t.T.astype(o_ref.dtype)


def _moba(qkv, tr, sel, rel_bias):
    s = qkv.shape[0]
    t = MOBA_BLOCK
    npair = N_HEADS_MOBA // 2
    nblk = s // t
    kcol = N_HEADS_FOX // 2
    a = np.arange(t)[None, :]
    b = np.arange(t)[:, None]
    bkt = jnp.asarray(np.stack([_t5_bucket_np(t + a - b), _t5_bucket_np(a - b)]))
    tq = 2 * t
    grid_spec = pltpu.PrefetchScalarGridSpec(
        num_scalar_prefetch=1,
        grid=(npair, s // tq),
        in_specs=[pl.BlockSpec((PAIR, tq), lambda p, i, rb: (p, i)),
                  pl.BlockSpec((s, PAIR), lambda p, i, rb: (0, kcol + p)),
                  pl.BlockSpec((PAIR, s), lambda p, i, rb: (npair + p, 0)),
                  pl.BlockSpec((2, nblk, tq), lambda p, i, rb: (p, 0, i)),
                  pl.BlockSpec((2, t, t), lambda p, i, rb: (0, 0, 0))],
        out_specs=pl.BlockSpec((tq, PAIR), lambda p, i, rb: (i, p)),
        scratch_shapes=[pltpu.VMEM((2, 1, tq), F32),
                        pltpu.VMEM((PAIR + SUM_ROWS, tq), F32), pltpu.VMEM((2, 3, t, tq), F32)]
        + [pltpu.VMEM((2, t, tq), F32)] * MOBA_UNROLL
        + [pltpu.VMEM((2 * t, tq), BF16)] * MOBA_UNROLL
        + [pltpu.VMEM((2, 1, tq), F32)] * MOBA_UNROLL,
    )
    return pl.pallas_call(
        _moba_kernel,
        out_shape=jax.ShapeDtypeStruct((s, npair * PAIR), BF16),
        grid_spec=grid_spec,
        compiler_params=_cparams(("arbitrary", "arbitrary")),
        name="moba",
    )(rel_bias.reshape(-1), tr, qkv, tr, sel, bkt)


def _store_token_tiles(ref, val):
    n = val.shape[0]
    for c in range(SUBLANES):
        ref[pl.ds(c, n, stride=SUBLANES), :] = val[:, c * LANES:(c + 1) * LANES]


def _load_token_tiles(ref, n):
    return jnp.concatenate([ref[pl.ds(c, n, stride=SUBLANES), :] for c in range(SUBLANES)], axis=1)


def _tile_rows(r0, n):
    start = r0 * SUBLANES
    if not isinstance(start, int):
        start = pl.multiple_of(start, SUBLANES)
    return pl.ds(start, n * SUBLANES)


def _outproj_kernel(ya_ref, yb_ref, x_ref, wa_ref, wb_ref, gpost_ref, gt_ref, gpre_ref,
                    sc_ref, sh_ref, wr_ref, br_ref,
                    x1_ref, h2_ref, idx_ref, rank_ref, gate_ref, cnt_ref, carry_scr):
    i = pl.program_id(0)
    tm = x_ref.shape[0]

    @pl.when(i == 0)
    def _():
        carry_scr[...] = jnp.zeros_like(carry_scr)

    y = (jnp.dot(ya_ref[...], wa_ref[...], preferred_element_type=F32)
         + jnp.dot(yb_ref[...], wb_ref[...], preferred_element_type=F32))
    x1 = x_ref[...] + gt_ref[...] * (_rms(y) * gpost_ref[...])
    x1_ref[...] = x1
    h2 = _rms(x1) * gpre_ref[...] * (1.0 + sc_ref[...]) + sh_ref[...]
    _store_token_tiles(h2_ref, h2)

    h_hi = h2.astype(BF16)
    h_lo = (h2 - h_hi.astype(F32)).astype(BF16)
    logits = (jnp.dot(h_hi, wr_ref[0], preferred_element_type=F32)
              + jnp.dot(h_hi, wr_ref[1], preferred_element_type=F32)
              + jnp.dot(h_lo, wr_ref[0], preferred_element_type=F32)) + br_ref[...]
    ne = logits.shape[1]
    lane = lax.broadcasted_iota(I32, logits.shape, 1)
    lane4 = lax.broadcasted_iota(I32, (tm, TOP_K), 1)
    g = logits
    mask = jnp.zeros(logits.shape, F32)
    vals, picks = [], []
    for _ in range(TOP_K):
        m = jnp.max(g, axis=1, keepdims=True)
        first = jnp.min(jnp.where(g == m, lane, ne), axis=1, keepdims=True)
        pick = lane == first
        mask = jnp.where(pick, 1.0, mask)
        g = jnp.where(pick, -jnp.inf, g)
        vals.append(m)
        picks.append(pick)
    ex = [jnp.exp(v - vals[0]) for v in vals]
    den = ex[0] + ex[1] + ex[2] + ex[3]
    gates = [e / den for e in ex]

    r = lax.broadcasted_iota(I32, (tm, tm), 0)
    c = lax.broadcasted_iota(I32, (tm, tm), 1)
    tril = jnp.where(c < r, 1.0, 0.0).astype(BF16)
    before = jnp.dot(tril, mask.astype(BF16), preferred_element_type=F32) + carry_scr[...]
    total = carry_scr[...] + jnp.sum(mask, axis=0, keepdims=True)
    carry_scr[...] = total
    cnt_ref[...] = jnp.broadcast_to(total, cnt_ref.shape)

    def pack4(cols):
        return jnp.where(lane4 == 0, cols[0],
                         jnp.where(lane4 == 1, cols[1], jnp.where(lane4 == 2, cols[2], cols[3])))

    ranks = [jnp.sum(jnp.where(pk, before, 0.0), axis=1, keepdims=True) for pk in picks]
    idxs = [jnp.sum(jnp.where(pk, lane, 0), axis=1, keepdims=True) for pk in picks]
    idx_ref[...] = pack4(idxs)
    rank_ref[...] = pack4(ranks).astype(I32)
    gate_ref[...] = pack4(gates)


def _outproj(mix_a, mix_b, x2, w_a, w_b, gpost, gt, gpre, sc, sh, w_router, b_router):
    s, d = x2.shape
    tm = ROW_TILE
    ne = w_router.shape[1]
    wa = mix_a.shape[1]
    w_hi = w_router.astype(BF16)
    w_router_hl = jnp.stack([w_hi, (w_router - w_hi.astype(F32)).astype(BF16)])
    row = lambda i: (i, 0)
    fix = lambda i: (0, 0)
    vec = pl.BlockSpec((1, d), fix)
    return pl.pallas_call(
        _outproj_kernel,
        out_shape=(jax.ShapeDtypeStruct((s, d), F32),
                   jax.ShapeDtypeStruct((s * SUBLANES, LANES), F32),
                   jax.ShapeDtypeStruct((s, TOP_K), I32),
                   jax.ShapeDtypeStruct((s, TOP_K), I32),
                   jax.ShapeDtypeStruct((s, TOP_K), F32),
                   jax.ShapeDtypeStruct((8, ne), F32)),
        grid=(s // tm,),
        in_specs=[pl.BlockSpec((tm, wa), row), pl.BlockSpec((tm, wa), row), pl.BlockSpec((tm, d), row),
                  pl.BlockSpec((wa, d), fix), pl.BlockSpec((wa, d), fix),
                  vec, vec, vec, vec, vec,
                  pl.BlockSpec((2, d, ne), lambda i: (0, 0, 0)), pl.BlockSpec((1, ne), fix)],
        out_specs=(pl.BlockSpec((tm, d), row), pl.BlockSpec((tm * SUBLANES, LANES), row),
                   pl.BlockSpec((tm, TOP_K), row), pl.BlockSpec((tm, TOP_K), row),
                   pl.BlockSpec((tm, TOP_K), row), pl.BlockSpec((8, ne), fix)),
        scratch_shapes=[pltpu.VMEM((1, ne), F32)],
        compiler_params=_cparams(("arbitrary",)),
        name="outproj",
    )(mix_a, mix_b, x2, w_a, w_b, gpost, gt, gpre, sc, sh, w_router_hl, b_router)


def _dest_kernel(idx_ref, rank_ref, pstart_ref, o_ref):
    idx = idx_ref[...]
    tm = idx.shape[0]
    ne = pstart_ref.shape[1]
    lane = lax.broadcasted_iota(I32, (tm, ne), 1)
    lane4 = lax.broadcasted_iota(I32, idx.shape, 1)
    out = rank_ref[...]
    for k in range(TOP_K):
        start = jnp.sum(jnp.where(lane == idx[:, k:k + 1], pstart_ref[...], 0), axis=1, keepdims=True)
        out = out + jnp.where(lane4 == k, start, 0)
    o_ref[...] = out


def _dest(idx4, rank4, pstart):
    s = idx4.shape[0]
    tm = min(4 * ROW_TILE, s)
    row = lambda i: (i, 0)
    return pl.pallas_call(
        _dest_kernel,
        out_shape=jax.ShapeDtypeStruct((s, TOP_K), I32),
        grid=(s // tm,),
        in_specs=[pl.BlockSpec((tm, TOP_K), row), pl.BlockSpec((tm, TOP_K), row),
                  pl.BlockSpec((1, N_EXPERTS), lambda i: (0, 0))],
        out_specs=pl.BlockSpec((tm, TOP_K), row),
        compiler_params=_cparams(("arbitrary",)),
        name="dest",
    )(idx4, rank4, pstart.reshape(1, -1))


def _dispatch_kernel(pstart_ref, pblk_ref, nu_ref, dest_ref, h_ref, xs_ref,
                     zero_scr, sem, zsem):
    tm = h_ref.shape[0] // SUBLANES
    bm = zero_scr.shape[0] // SUBLANES
    n_blk = xs_ref.shape[0] // (bm * SUBLANES)

    rows = _tile_rows

    @pl.when(pl.program_id(0) == 0)
    def _():
        zero_scr[...] = jnp.zeros_like(zero_scr)

        def zero_copy(row0):
            return pltpu.make_async_copy(zero_scr, xs_ref.at[rows(row0, bm)], zsem)

        for phase in range(2):
            for e in range(N_EXPERTS):
                last = pstart_ref[e] + (pblk_ref[e] - 1) * bm
                tail = (n_blk - N_EXPERTS + e) * bm
                for cond, row0 in ((pblk_ref[e] > 0, last), (n_blk - N_EXPERTS + e >= nu_ref[0], tail)):
                    @pl.when(cond)
                    def _():
                        if phase == 0:
                            zero_copy(row0).start()
                        else:
                            zero_copy(row0).wait()

    def row_copy(r, k):
        dst = dest_ref[r * TOP_K + k]
        return pltpu.make_async_copy(h_ref.at[rows(r, 1)], xs_ref.at[rows(dst, 1)], sem)

    def start(r, carry):
        for k in range(TOP_K):
            row_copy(r, k).start(priority=k % 2)
        return carry

    lax.fori_loop(0, tm, start, 0, unroll=4)
    for k in range(TOP_K):
        pltpu.make_async_copy(h_ref, xs_ref.at[rows(0, tm)], sem).wait()


def _dispatch(pstart, pblk, n_used, dest_flat, h2, n_rows):
    s = h2.shape[0] // SUBLANES
    tm = COMBINE_TILE
    grid_spec = pltpu.PrefetchScalarGridSpec(
        num_scalar_prefetch=3,
        grid=(s // tm,),
        in_specs=[pl.BlockSpec((tm * TOP_K,), lambda i, *_: (i,), memory_space=pltpu.SMEM),
                  pl.BlockSpec((tm * SUBLANES, LANES), lambda i, *_: (i, 0))],
        out_specs=pl.BlockSpec(memory_space=pl.ANY),
        scratch_shapes=[pltpu.VMEM((EXPERT_BLOCK * SUBLANES, LANES), F32), pltpu.SemaphoreType.DMA,
                        pltpu.SemaphoreType.DMA],
    )
    return pl.pallas_call(
        _dispatch_kernel,
        out_shape=jax.ShapeDtypeStruct((n_rows * SUBLANES, LANES), F32),
        grid_spec=grid_spec,
        compiler_params=_cparams(("arbitrary",)),
        name="dispatch",
    )(pstart, pblk, n_used, dest_flat, h2)


def _experts_kernel(be_ref, nu_ref, xs_ref, wgu_ref, bgu_ref, wd_ref, bd_ref, y_ref,
                    wgu_bf, wd_bf):
    b = pl.program_id(0)
    d_exp = wd_ref.shape[1]
    prev = be_ref[jnp.maximum(b - 1, 0)]
    changed = (b == 0) | (be_ref[b] != prev)

    @pl.when((b < nu_ref[0]) & changed)
    def _():
        rows = 128

        def cast_gu(c, carry):
            r0 = pl.multiple_of(c * rows, rows)
            wgu_bf[pl.ds(r0, rows), :] = wgu_ref[0, pl.ds(r0, rows), :].astype(BF16)
            return carry

        def cast_d(c, carry):
            r0 = pl.multiple_of(c * rows, rows)
            wd_bf[pl.ds(r0, rows), :] = wd_ref[0, pl.ds(r0, rows), :].astype(BF16)
            return carry

        lax.fori_loop(0, wgu_ref.shape[1] // rows, cast_gu, 0)
        lax.fori_loop(0, wd_ref.shape[1] // rows, cast_d, 0)

    @pl.when(b < nu_ref[0])
    def _():
        bm = xs_ref.shape[0] // SUBLANES
        xb = _load_token_tiles(xs_ref, bm).astype(BF16)
        hdn = jnp.dot(xb, wgu_bf[...], preferred_element_type=F32) + bgu_ref[0]
        x_glu = jnp.minimum(hdn[:, :d_exp], SWIGLU_LIMIT)
        x_lin = jnp.clip(hdn[:, d_exp:], -SWIGLU_LIMIT, SWIGLU_LIMIT)
        act = x_glu * jax.nn.sigmoid(SWIGLU_ALPHA * x_glu) * (x_lin + 1.0)
        _store_token_tiles(y_ref, jnp.dot(act.astype(BF16), wd_bf[...], preferred_element_type=F32)
                           + bd_ref[0])

    @pl.when(b >= nu_ref[0])
    def _():
        y_ref[...] = jnp.zeros_like(y_ref)


def _experts(block_e, n_used, xs, w_gate_up, b_gate_up, w_down, b_down):
    n_rows = xs.shape[0] // SUBLANES
    bm = EXPERT_BLOCK
    n_blk = n_rows // bm
    ne, d, two_de = w_gate_up.shape
    de = w_down.shape[1]
    assert d == SUBLANES * LANES

    def blk(b, be, nu):
        return jnp.minimum(b, nu[0] - 1)

    grid_spec = pltpu.PrefetchScalarGridSpec(
        num_scalar_prefetch=2,
        grid=(n_blk,),
        in_specs=[pl.BlockSpec((bm * SUBLANES, LANES), lambda b, be, nu: (blk(b, be, nu), 0)),
                  pl.BlockSpec((1, d, two_de), lambda b, be, nu: (be[blk(b, be, nu)], 0, 0)),
                  pl.BlockSpec((1, 1, two_de), lambda b, be, nu: (be[blk(b, be, nu)], 0, 0)),
                  pl.BlockSpec((1, de, d), lambda b, be, nu: (be[blk(b, be, nu)], 0, 0)),
                  pl.BlockSpec((1, 1, d), lambda b, be, nu: (be[blk(b, be, nu)], 0, 0))],
        out_specs=pl.BlockSpec((bm * SUBLANES, LANES), lambda b, be, nu: (b, 0)),
        scratch_shapes=[pltpu.VMEM((d, two_de), BF16), pltpu.VMEM((de, d), BF16)],
    )
    return pl.pallas_call(
        _experts_kernel,
        out_shape=jax.ShapeDtypeStruct((n_rows * SUBLANES, LANES), F32),
        grid_spec=grid_spec,
        compiler_params=_cparams(("arbitrary",)),
        name="experts",
    )(block_e, n_used, xs, w_gate_up, b_gate_up.reshape(ne, 1, two_de), w_down, b_down.reshape(ne, 1, d))


def _combine_kernel(dest_ref, y_ref, gate_ref, x1_ref, gt_ref, gpost_ref, o_ref, buf, sem):
    tm = x1_ref.shape[0]

    def row_copy(r, k):
        src = dest_ref[r * TOP_K + k]
        return pltpu.make_async_copy(y_ref.at[_tile_rows(src, 1)], buf.at[k, _tile_rows(r, 1)], sem)

    def start(r, carry):
        for k in range(TOP_K):
            row_copy(r, k).start(priority=k % 2)
        return carry

    lax.fori_loop(0, tm, start, 0, unroll=4)
    for k in range(TOP_K):
        pltpu.make_async_copy(y_ref.at[_tile_rows(0, tm)], buf.at[k], sem).wait()

    gate = gate_ref[...]
    acc = gate[:, 0:1] * _load_token_tiles(buf.at[0], tm)
    for k in range(1, TOP_K):
        acc = acc + gate[:, k:k + 1] * _load_token_tiles(buf.at[k], tm)
    o_ref[...] = x1_ref[...] + gt_ref[...] * (_rms(acc) * gpost_ref[...])


def _combine(dest_flat, y, gate4, x1, gt, gpost):
    s, d = x1.shape
    tm = COMBINE_TILE
    return pl.pallas_call(
        _combine_kernel,
        out_shape=jax.ShapeDtypeStruct((s, d), F32),
        grid=(s // tm,),
        in_specs=[pl.BlockSpec((tm * TOP_K,), lambda i: (i,), memory_space=pltpu.SMEM),
                  pl.BlockSpec(memory_space=pl.ANY),
                  pl.BlockSpec((tm, TOP_K), lambda i: (i, 0)),
                  pl.BlockSpec((tm, d), lambda i: (i, 0)),
                  pl.BlockSpec((1, d), lambda i: (0, 0)),
                  pl.BlockSpec((1, d), lambda i: (0, 0))],
        out_specs=pl.BlockSpec((tm, d), lambda i: (i, 0)),
        scratch_shapes=[pltpu.VMEM((TOP_K, tm * SUBLANES, LANES), F32), pltpu.SemaphoreType.DMA],
        compiler_params=_cparams(("arbitrary",)),
        name="combine",
    )(dest_flat, y, gate4, x1, gt, gpost)


def _layer(x2, mod, g_pre_mix, g_post_mix, w_in, b_forget, rel_bias, w_out,
           g_pre_ffn, g_post_ffn, w_router, b_router, w_gate_up, b_gate_up, w_down, b_down):
    s, d = x2.shape
    sh_m, sc_m, gt_m, sh_f, sc_f, gt_f = [mod[:, k * d:(k + 1) * d] for k in range(6)]
    n_qkv = 3 * (N_HEADS_FOX + N_HEADS_MOBA) * HEAD_DIM
    fox_w = N_HEADS_FOX * HEAD_DIM

    w_qkv = w_in[:, :n_qkv].astype(BF16)
    w_f = w_in[:, n_qkv:].T
    qkv, cum, sel, nrm, tr, nr = _inproj(x2, g_pre_mix.reshape(1, d), sc_m, sh_m, w_qkv, w_f,
                                         b_forget.reshape(-1, 1))
    y_a = _fox(qkv, tr, nr, cum, nrm)
    y_b = _moba(qkv, tr, sel, rel_bias)

    w_out_bf = w_out.astype(BF16)
    x1, h2, idx4, rank4, gate4, cnt = _outproj(
        y_a, y_b, x2, w_out_bf[:fox_w], w_out_bf[fox_w:], g_post_mix.reshape(1, d), gt_m,
        g_pre_ffn.reshape(1, d), sc_f, sh_f, w_router, b_router.reshape(1, -1))

    bm = EXPERT_BLOCK
    counts = cnt[0].astype(I32)
    pblk = (counts + bm - 1) // bm
    pend_blk = jnp.cumsum(pblk)
    pstart = ((pend_blk - pblk) * bm).astype(I32)
    n_rows = s * TOP_K + N_EXPERTS * bm
    n_blk = n_rows // bm
    block_e = jnp.minimum(jnp.sum(pend_blk[None, :] <= jnp.arange(n_blk)[:, None], axis=1),
                          N_EXPERTS - 1).astype(I32)
    n_used = pend_blk[-1:].astype(I32)

    dest_flat = _dest(idx4, rank4, pstart).reshape(-1)
    xs = _dispatch(pstart, pblk.astype(I32), n_used, dest_flat, h2, n_rows)
    y = _experts(block_e, n_used, xs, w_gate_up, b_gate_up, w_down, b_down)
    return _combine(dest_flat, y, gate4, x1, gt_f, g_post_ffn.reshape(1, d))


def kernel(x, c, w_ada, b_ada, g_pre_mix, g_post_mix, w_in, b_forget, rel_bias, w_out, g_pre_ffn, g_post_ffn, w_router, b_router, w_gate_up, b_gate_up, w_down, b_down):
    bsz, s, d = x.shape
    depth = w_ada.shape[0]
    outs = []
    for bi in range(bsz):
        x2 = x[bi]
        for l in range(depth):
            mod = _adaln(c[bi:bi + 1], w_ada[l], b_ada[l])
            x2 = _layer(x2, mod, g_pre_mix[l], g_post_mix[l], w_in[l], b_forget[l], rel_bias, w_out[l],
                        g_pre_ffn[l], g_post_ffn[l], w_router[l], b_router[l], w_gate_up[l], b_gate_up[l],
                        w_down[l], b_down[l])
        outs.append(x2)
    return outs[0].reshape(1, s, d) if bsz == 1 else jnp.stack(outs)
```

```python
import functools
import math

import numpy as np
import jax
import jax.numpy as jnp
from jax import lax
from jax.experimental import pallas as pl
from jax.experimental.pallas import tpu as pltpu

F32 = jnp.float32
BF16 = jnp.bfloat16
I32 = jnp.int32

HEAD_DIM = 64
N_HEADS_FOX = 8
N_HEADS_MOBA = 8
PAIR = 2 * HEAD_DIM
MOBA_BLOCK = 256
MOBA_TOPK = 3
NUM_BUCKETS = 32
MAX_DISTANCE = 128
N_EXPERTS = 32
TOP_K = 4
SWIGLU_LIMIT = 7.0
SWIGLU_ALPHA = 1.702
RMS_EPS = 1e-6
NEG = -(2.0 ** 100)
M_INIT = -(2.0 ** 99)
LOG2E = math.log2(math.e)
SUM_ROWS = 16
SUBLANES = 8
LANES = 128
EXP_UNDERFLOW = 90.0
VMEM_LIMIT = 56 * 1024 * 1024

ROW_TILE = 512
FOX_TILE = 256
EXPERT_BLOCK = 512
COMBINE_TILE = 512
MOBA_LAG = 1
MOBA_UNROLL = 4

NT_DIMS = (((1,), (1,)), ((), ()))


def _cparams(sem):
    return pltpu.CompilerParams(dimension_semantics=sem, vmem_limit_bytes=VMEM_LIMIT)


def _rms(x):
    return x * lax.rsqrt(jnp.mean(x * x, axis=-1, keepdims=True) + RMS_EPS)


def _adaln_kernel(c_ref, w_ref, b_ref, o_ref):
    c = c_ref[...]
    cond = c * jax.nn.sigmoid(c)
    o_ref[...] = jnp.dot(cond, w_ref[...], preferred_element_type=F32,
                         precision=lax.Precision.HIGHEST) + b_ref[...]


def _adaln(c, w_ada, b_ada):
    d = c.shape[-1]
    n = w_ada.shape[-1]
    c8 = jnp.broadcast_to(c.reshape(1, d), (8, d))
    out = pl.pallas_call(
        _adaln_kernel,
        out_shape=jax.ShapeDtypeStruct((8, n), F32),
        grid=(n // d,),
        in_specs=[pl.BlockSpec((8, d), lambda j: (0, 0)),
                  pl.BlockSpec((d, d), lambda j: (0, j)),
                  pl.BlockSpec((1, d), lambda j: (0, j))],
        out_specs=pl.BlockSpec((8, d), lambda j: (0, j)),
        compiler_params=_cparams(("arbitrary",)),
        name="adaln",
    )(c8, w_ada, b_ada.reshape(1, n))
    return out[0:1]


def _inproj_kernel(x_ref, g_ref, sc_ref, sh_ref, w_ref, wf_ref, bf_ref,
                   qkv_ref, cum_ref, sel_ref, nrm_ref, tr_ref, nr_ref, km_scr, carry_scr):
    i = pl.program_id(0)
    tm = x_ref.shape[0]
    nblk = km_scr.shape[0]

    @pl.when(i == 0)
    def _():
        km_scr[...] = jnp.zeros_like(km_scr)
        carry_scr[...] = jnp.zeros_like(carry_scr)

    x = x_ref[...]
    h = _rms(x) * g_ref[...] * (1.0 + sc_ref[...]) + sh_ref[...]
    hb = h.astype(BF16)

    width = N_HEADS_FOX * HEAD_DIM
    hsel = jnp.where(lax.broadcasted_iota(I32, (width, N_HEADS_FOX), 0) // HEAD_DIM
                     == lax.broadcasted_iota(I32, (width, N_HEADS_FOX), 1), 1.0, 0.0)
    kb = None
    tr_slot = {3: 0, 5: 1, 0: 2, 2: 3}
    qbt = None
    for c in range(6):
        pc = jnp.dot(hb, w_ref[:, c * width:(c + 1) * width], preferred_element_type=F32)
        if c in tr_slot:
            n = tr_slot[c]
            pct = pc.T
            if c == 3:
                qbt = pct
            if c == 0 or c == 3:
                pct = pct * (LOG2E * HEAD_DIM ** -0.5)
            tr_ref[n * width:(n + 1) * width, :] = pct.astype(BF16)
        if c == 1 or c == 4:
            qkv_ref[:, (c // 3) * width:(c // 3 + 1) * width] = pc.astype(BF16)
        if c < 2:
            sq = (pc * (HEAD_DIM ** -0.5) if c == 0 else pc).astype(BF16).astype(F32)
            n2 = jnp.dot((sq * sq).astype(BF16), hsel.astype(BF16), preferred_element_type=F32)
            nrm_ref[0, c:c + 1, :] = jnp.max(n2, axis=0, keepdims=True)
        if c == 4:
            kb = pc

    ft = lax.dot_general(wf_ref[...].astype(BF16), hb, NT_DIMS, preferred_element_type=F32)
    z = ft + bf_ref[...]
    logf = -(jnp.maximum(-z, 0.0) + jnp.log1p(jnp.exp(-jnp.abs(z))))
    lane = lax.broadcasted_iota(I32, logf.shape, 1)
    cs = logf
    sh = 1
    while sh < tm:
        cs = cs + jnp.where(lane >= sh, pltpu.roll(cs, sh, axis=1), 0.0)
        sh *= 2
    base = jnp.zeros_like(cs)
    for b in range(1, tm // FOX_TILE):
        base = jnp.where(lane >= b * FOX_TILE, cs[:, b * FOX_TILE - 1:b * FOX_TILE], base)
    nr_t = (-LOG2E * (cs - base)).T
    for pp in range(nr_ref.shape[0]):
        nr_ref[pp] = nr_t[:, 2 * pp:2 * pp + 2]
    cs = cs + carry_scr[...]
    cum_ref[...] = cs
    carry_scr[...] = cs[:, tm - 1:tm]

    nb_tile = tm // MOBA_BLOCK
    for b in range(nb_tile):
        kmean = jnp.sum(kb[b * MOBA_BLOCK:(b + 1) * MOBA_BLOCK], axis=0, keepdims=True) * (1.0 / MOBA_BLOCK)
        km_scr[pl.ds(i * nb_tile + b, 1), :] = kmean

    km = km_scr[...]
    blk = lax.broadcasted_iota(I32, (nblk, tm), 0)
    col = lax.broadcasted_iota(I32, (nblk, tm), 1)
    own = i * nb_tile + col // MOBA_BLOCK
    for hd in range(N_HEADS_MOBA):
        hs = slice(hd * HEAD_DIM, (hd + 1) * HEAD_DIM)
        g = jnp.dot(km[:, hs], qbt[hs, :], preferred_element_type=F32,
                    precision=lax.Precision.HIGHEST)
        g = jnp.where(blk < own, g, -jnp.inf)
        sel = jnp.zeros(g.shape, dtype=jnp.bool_)
        for _ in range(MOBA_TOPK):
            m = jnp.max(g, axis=0, keepdims=True)
            first = jnp.min(jnp.where(g == m, blk, nblk), axis=0, keepdims=True)
            pick = (blk == first) & (m > -jnp.inf)
            sel = sel | pick
            g = jnp.where(pick, -jnp.inf, g)
        sel_ref[hd] = jnp.where(sel, 0.0, NEG)


def _inproj(x2, g, sc, sh, w_qkv, w_f, b_f):
    s, d = x2.shape
    tm = ROW_TILE
    nblk = s // MOBA_BLOCK
    n = w_qkv.shape[1]
    width = N_HEADS_FOX * HEAD_DIM
    n_rows_out = 2 * width
    n_tr = 4 * width
    assert tm % FOX_TILE == 0 and N_HEADS_FOX == N_HEADS_MOBA
    return pl.pallas_call(
        _inproj_kernel,
        out_shape=(jax.ShapeDtypeStruct((s, n_rows_out), BF16),
                   jax.ShapeDtypeStruct((N_HEADS_FOX, s), F32),
                   jax.ShapeDtypeStruct((N_HEADS_MOBA, nblk, s), F32),
                   jax.ShapeDtypeStruct((s // tm, 2, N_HEADS_FOX), F32),
                   jax.ShapeDtypeStruct((n_tr, s), BF16),
                   jax.ShapeDtypeStruct((N_HEADS_FOX // 2, s, 2), F32)),
        grid=(s // tm,),
        in_specs=[pl.BlockSpec((tm, d), lambda i: (i, 0)),
                  pl.BlockSpec((1, d), lambda i: (0, 0)),
                  pl.BlockSpec((1, d), lambda i: (0, 0)),
                  pl.BlockSpec((1, d), lambda i: (0, 0)),
                  pl.BlockSpec((d, n), lambda i: (0, 0)),
                  pl.BlockSpec((N_HEADS_FOX, d), lambda i: (0, 0)),
                  pl.BlockSpec((N_HEADS_FOX, 1), lambda i: (0, 0))],
        out_specs=(pl.BlockSpec((tm, n_rows_out), lambda i: (i, 0)),
                   pl.BlockSpec((N_HEADS_FOX, tm), lambda i: (0, i)),
                   pl.BlockSpec((N_HEADS_MOBA, nblk, tm), lambda i: (0, 0, i)),
                   pl.BlockSpec((1, 2, N_HEADS_FOX), lambda i: (i, 0, 0)),
                   pl.BlockSpec((n_tr, tm), lambda i: (0, i)),
                   pl.BlockSpec((N_HEADS_FOX // 2, tm, 2), lambda i: (0, i, 0))),
        scratch_shapes=[pltpu.VMEM((nblk, N_HEADS_MOBA * HEAD_DIM), F32),
                        pltpu.VMEM((N_HEADS_FOX, 1), F32)],
        compiler_params=_cparams(("arbitrary",)),
        name="inproj",
    )(x2, g, sc, sh, w_qkv, w_f, b_f)


def _fox_kernel(jlo_ref, cp_ref, qt_ref, k_ref, vt_ref, nr_ref, o_ref, m_ref, a_ref):
    p = pl.program_id(0)
    i = pl.program_id(1)
    j_first = jlo_ref[p * pl.num_programs(1) + i]
    t = qt_ref.shape[1]

    qt = qt_ref[...]
    top = lax.broadcasted_iota(I32, qt.shape, 0) < HEAD_DIM
    zq = jnp.zeros_like(qt)
    qth = (jnp.where(top, qt, zq), jnp.where(top, zq, qt))
    m_ref[...] = jnp.full(m_ref.shape, M_INIT, F32)
    a_ref[...] = jnp.zeros(a_ref.shape, F32)
    er = lax.broadcasted_iota(I32, (SUM_ROWS, 2 * t), 0)
    ec = lax.broadcasted_iota(I32, (SUM_ROWS, 2 * t), 1)
    ones_rows = jnp.where(((er == 0) & (ec < t)) | ((er == 1) & (ec >= t)), 1.0, 0.0).astype(BF16)
    arow = lax.broadcasted_iota(I32, a_ref.shape, 0)
    head0_rows = (arow < HEAD_DIM) | (arow == PAIR)

    def scores(j, diagonal):
        k0 = pl.multiple_of(j * t, t)
        kt = k_ref[pl.ds(k0, t), :]
        out = []
        for hd in range(2):
            s = jnp.dot(kt, qth[hd], preferred_element_type=F32) + nr_ref[0, pl.ds(k0, t), hd:hd + 1]
            if diagonal:
                r = lax.broadcasted_iota(I32, s.shape, 0)
                c = lax.broadcasted_iota(I32, s.shape, 1)
                s = jnp.where(r <= c, s, NEG)
            out.append(s)
        return out

    def softmax(ss, j, valid):
        ps, alphas = [], []
        for hd in range(2):
            h = 2 * p + hd
            pair_shift = LOG2E * (cp_ref[h, i] - cp_ref[h, j])
            if valid is not None:
                pair_shift = jnp.where(valid, pair_shift, NEG)
            m_old = m_ref[hd]
            m_new = jnp.maximum(m_old, jnp.max(ss[hd], axis=0, keepdims=True) + pair_shift)
            alphas.append(jnp.exp2(m_old - m_new))
            m_ref[hd] = m_new
            ps.append(jnp.exp2(ss[hd] - (m_new - pair_shift)).astype(BF16))
        return ps, alphas

    def accumulate(j, ps, alphas):
        k0 = pl.multiple_of(j * t, t)
        vt = vt_ref[:, pl.ds(k0, t)]
        zv = jnp.zeros_like(vt)
        vcat = jnp.concatenate([jnp.where(top, vt, zv), jnp.where(top, zv, vt)], axis=1)
        vcat = jnp.concatenate([vcat, ones_rows], axis=0)
        pcat = jnp.concatenate(ps, axis=0)
        a_ref[...] = (a_ref[...] * jnp.where(head0_rows, alphas[0], alphas[1])
                      + jnp.dot(vcat, pcat, preferred_element_type=F32))

    def body(j, carry):
        accumulate(j, *softmax(scores(j, False), j, None))
        return carry

    lax.fori_loop(j_first, i - 1, body, 0)
    jp = jnp.maximum(i - 1, 0)
    ss_prev = scores(jp, False)
    ss_diag = scores(i, True)
    w_prev = softmax(ss_prev, jp, i >= 1)
    w_diag = softmax(ss_diag, i, None)
    accumulate(jp, *w_prev)
    accumulate(i, *w_diag)
    out_t =a_ref[0:PAIR, :] / jnp.where(top, a_ref[PAIR:PAIR + 1, :], a_ref[PAIR + 1:PAIR + 2, :])
    o_ref[...] = out_t.T.astype(o_ref.dtype)


def _fox_first_tile(cum, nrm, t):
    cend = cum[:, t - 1::t]
    nt = cend.shape[1]
    cprev = jnp.concatenate([jnp.zeros((cend.shape[0], 1), F32), cend[:, :-1]], axis=1)
    rep = nt // nrm.shape[0]
    qn = jnp.repeat(jnp.sqrt(nrm[:, 0, :]).T, rep, axis=1)
    kn = jnp.repeat(jnp.sqrt(nrm[:, 1, :]).T, rep, axis=1)
    gap = (1.02 * qn[:, :, None] * (kn[:, None, :] + kn[:, :, None])
           + cprev[:, :, None] - cend[:, None, :])
    jj = jnp.arange(nt)[None, None, :]
    ii = jnp.arange(nt)[None, :, None]
    needed = (jj < ii) & jnp.logical_not(gap <= -EXP_UNDERFLOW)
    needed = needed[0::2] | needed[1::2]
    first = jnp.min(jnp.where(needed, jj, ii), axis=2)
    return first.reshape(-1).astype(I32)


def _fox(qkv, tr, nr, cum, nrm):
    s = qkv.shape[0]
    t = FOX_TILE
    npair = N_HEADS_FOX // 2
    jlo = _fox_first_tile(cum, nrm, t)
    cend = cum[:, t - 1::t]
    cprev = jnp.concatenate([jnp.zeros((cend.shape[0], 1), F32), cend[:, :-1]], axis=1)
    q_rows = 2 * (N_HEADS_MOBA // 2)
    grid_spec = pltpu.PrefetchScalarGridSpec(
        num_scalar_prefetch=1,
        grid=(npair, s // t),
        in_specs=[pl.BlockSpec(memory_space=pltpu.SMEM),
                  pl.BlockSpec((PAIR, t), lambda p, i, jl: (q_rows + p, i)),
                  pl.BlockSpec((s, PAIR), lambda p, i, jl: (0, p)),
                  pl.BlockSpec((PAIR, s), lambda p, i, jl: (q_rows + npair + p, 0)),
                  pl.BlockSpec((1, s, 2), lambda p, i, jl: (p, 0, 0))],
        out_specs=pl.BlockSpec((t, PAIR), lambda p, i, jl: (i, p)),
        scratch_shapes=[pltpu.VMEM((2, 1, t), F32), pltpu.VMEM((PAIR + SUM_ROWS, t), F32)],
    )
    return pl.pallas_call(
        _fox_kernel,
        out_shape=jax.ShapeDtypeStruct((s, npair * PAIR), BF16),
        grid_spec=grid_spec,
        compiler_params=_cparams(("arbitrary", "arbitrary")),
        name="fox",
    )(jlo, cprev, tr, qkv, tr, nr)


def _t5_bucket_np(dist):
    dist = np.maximum(dist, 0)
    max_exact = NUM_BUCKETS // 2
    d = np.maximum(dist, 1).astype(np.float32)
    large = max_exact + (np.log(d / np.float32(max_exact)) / np.float32(math.log(MAX_DISTANCE / max_exact))
                         * np.float32(NUM_BUCKETS - max_exact)).astype(np.int32)
    large = np.minimum(large, NUM_BUCKETS - 1)
    return np.where(dist < max_exact, dist, large).astype(np.int32)


def _moba_kernel(rb_ref, qt_ref, k_ref, vt_ref, sel_ref, bkt_ref, o_ref,
                 m_ref, a_ref, bias_scr, *bufs):
    s_bufs = bufs[0:MOBA_UNROLL]
    p_bufs = bufs[MOBA_UNROLL:2 * MOBA_UNROLL]
    al_bufs = bufs[2 * MOBA_UNROLL:3 * MOBA_UNROLL]
    p = pl.program_id(0)
    g = pl.program_id(1)
    tq = qt_ref.shape[1]
    t = tq // 2

    @pl.when(g == 0)
    def _():
        r = lax.broadcasted_iota(I32, (t, t), 0)
        c = lax.broadcasted_iota(I32, (t, t), 1)
        zero = jnp.zeros((t, t), F32)
        for hd in range(2):
            h = 2 * p + hd
            far = rb_ref[(NUM_BUCKETS - 1) * N_HEADS_MOBA + h]
            tiles = []
            for w in range(2):
                bkt = bkt_ref[w]
                acc = jnp.zeros(bkt.shape, F32)
                for kk in range(NUM_BUCKETS):
                    acc = acc + jnp.where(bkt == kk, rb_ref[kk * N_HEADS_MOBA + h], 0.0)
                tiles.append((acc - far) * LOG2E)
            prev_t = tiles[0]
            own_t = jnp.where(r <= c, tiles[1], NEG)
            bias_scr[hd, 0] = jnp.concatenate([prev_t, zero], axis=1)
            bias_scr[hd, 1] = jnp.concatenate([own_t, prev_t], axis=1)
            bias_scr[hd, 2] = jnp.concatenate([zero, own_t], axis=1)

    qt = qt_ref[...]
    top = lax.broadcasted_iota(I32, qt.shape, 0) < HEAD_DIM
    zq = jnp.zeros_like(qt)
    qth = (jnp.where(top, qt, zq), jnp.where(top, zq, qt))
    m_ref[...] = jnp.full(m_ref.shape, M_INIT, F32)
    a_ref[...] = jnp.zeros(a_ref.shape, F32)

    er = lax.broadcasted_iota(I32, (SUM_ROWS, 2 * t), 0)
    ec = lax.broadcasted_iota(I32, (SUM_ROWS, 2 * t), 1)
    ones_rows = jnp.where(((er == 0) & (ec < t)) | ((er == 1) & (ec >= t)), 1.0, 0.0).astype(BF16)
    arow = lax.broadcasted_iota(I32, a_ref.shape, 0)
    head0_rows = (arow < HEAD_DIM) | (arow == PAIR)
    vtop = lax.broadcasted_iota(I32, (PAIR, t), 0) < HEAD_DIM
    in_a = lax.broadcasted_iota(I32, (1, tq), 1) < t

    n_far = jnp.maximum(2 * g - 1, 0)

    n_key_tiles = k_ref.shape[0] // t

    def produce(j, s_buf):
        j = jnp.minimum(j, n_key_tiles - 1)
        k0 = pl.multiple_of(j * t, t)
        kt = k_ref[pl.ds(k0, t), :]
        for hd in range(2):
            s_buf[hd] = jnp.dot(kt, qth[hd], preferred_element_type=F32)

    def softmax(s_buf, p_buf, al_buf, selrows, w):
        for hd in range(2):
            s = s_buf[hd]
            if w is not None:
                s = s + bias_scr[hd, w]
            smax = jnp.max(s, axis=0, keepdims=True)
            m_old = m_ref[hd]
            m_new = jnp.maximum(m_old, smax + selrows[hd])
            shift = m_new - selrows[hd]
            al_buf[hd] = jnp.exp2(m_old - m_new)
            m_ref[hd] = m_new
            p_buf[hd * t:(hd + 1) * t, :] = jnp.exp2(s - shift).astype(BF16)

    def far_rows(j):
        return [jnp.where(j < n_far, sel_ref[hd, pl.ds(j, 1), :], NEG) for hd in range(2)]

    def accumulate(j, p_buf, al_buf):
        k0 = pl.multiple_of(j * t, t)
        vt = vt_ref[:, pl.ds(k0, t)]
        zv = jnp.zeros_like(vt)
        vcat = jnp.concatenate([jnp.where(vtop, vt, zv), jnp.where(vtop, zv, vt)], axis=1)
        vcat = jnp.concatenate([vcat, ones_rows], axis=0)
        a_ref[...] = (a_ref[...] * jnp.where(head0_rows, al_buf[0], al_buf[1])
                      + jnp.dot(vcat, p_buf[...], preferred_element_type=F32))

    un = MOBA_UNROLL
    lag = MOBA_LAG
    for n in range(un - lag, un):
        p_bufs[n][...] = jnp.zeros_like(p_bufs[n])
        al_bufs[n][...] = jnp.ones_like(al_bufs[n])
    produce(0, s_bufs[0])

    def body(u, carry):
        j0 = un * u
        for n in range(un):
            accumulate(jnp.maximum(j0 + n - lag, 0), p_bufs[(n - lag) % un], al_bufs[(n - lag) % un])
            softmax(s_bufs[n], p_bufs[n], al_bufs[n], far_rows(j0 + n), None)
            produce(j0 + n + 1, s_bufs[(n + 1) % un])
        return carry

    n_trips = (n_far + un - 1) // un
    lax.fori_loop(0, n_trips, body, 0)

    j1 = jnp.maximum(2 * g - 1, 0)
    j2 = 2 * g
    j3 = 2 * g + 1
    rows1 = [jnp.where(g >= 1, sel_ref[hd, pl.ds(j1, 1), :], NEG) for hd in range(2)]
    rows2 = [jnp.where(in_a, 0.0, sel_ref[hd, pl.ds(j2, 1), :]) for hd in range(2)]
    rows3 = [jnp.where(in_a, NEG, 0.0)] * 2
    produce(j1, s_bufs[0])
    produce(j2, s_bufs[1])
    for n in range(un - lag, un):
        accumulate(jnp.maximum(un * n_trips - un + n, 0), p_bufs[n], al_bufs[n])
    softmax(s_bufs[0], p_bufs[0], al_bufs[0], rows1, 0)
    produce(j3, s_bufs[0])
    softmax(s_bufs[1], p_bufs[1], al_bufs[1], rows2, 1)
    accumulate(j1, p_bufs[0], al_bufs[0])
    softmax(s_bufs[0], p_bufs[2], al_bufs[2], rows3, 2)
    accumulate(j2, p_bufs[1], al_bufs[1])
    accumulate(j3, p_bufs[2], al_bufs[2])
    out_t = a_ref[0:PAIR, :] / jnp.where(top, a_ref[PAIR:PAIR + 1, :], a_ref[PAIR + 1:PAIR + 2, :])
    o_ref[...] = out_t.T.astype(o_ref.dtype)


def _moba(qkv, tr, sel, rel_bias):
    s = qkv.shape[0]
    t = MOBA_BLOCK
    npair = N_HEADS_MOBA // 2
    nblk = s // t
    kcol = N_HEADS_FOX // 2
    a = np.arange(t)[None, :]
    b = np.arange(t)[:, None]
    bkt = jnp.asarray(np.stack([_t5_bucket_np(t + a - b), _t5_bucket_np(a - b)]))
    tq = 2 * t
    grid_spec = pltpu.PrefetchScalarGridSpec(
        num_scalar_prefetch=1,
        grid=(npair, s // tq),
        in_specs=[pl.BlockSpec((PAIR, tq), lambda p, i, rb: (p, i)),
                  pl.BlockSpec((s, PAIR), lambda p, i, rb: (0, kcol + p)),
                  pl.BlockSpec((PAIR, s), lambda p, i, rb: (npair + p, 0)),
                  pl.BlockSpec((2, nblk, tq), lambda p, i, rb: (p, 0, i)),
                  pl.BlockSpec((2, t, t), lambda p, i, rb: (0, 0, 0))],
        out_specs=pl.BlockSpec((tq, PAIR), lambda p, i, rb: (i, p)),
        scratch_shapes=[pltpu.VMEM((2, 1, tq), F32),
                        pltpu.VMEM((PAIR + SUM_ROWS, tq), F32), pltpu.VMEM((2, 3, t, tq), F32)]
        + [pltpu.VMEM((2, t, tq), F32)] * MOBA_UNROLL
        + [pltpu.VMEM((2 * t, tq), BF16)] * MOBA_UNROLL
        + [pltpu.VMEM((2, 1, tq), F32)] * MOBA_UNROLL,
    )
    return pl.pallas_call(
        _moba_kernel,
        out_shape=jax.ShapeDtypeStruct((s, npair * PAIR), BF16),
        grid_spec=grid_spec,
        compiler_params=_cparams(("arbitrary", "arbitrary")),
        name="moba",
    )(rel_bias.reshape(-1), tr, qkv, tr, sel, bkt)


def _store_token_tiles(ref, val):
    n = val.shape[0]
    for c in range(SUBLANES):
        ref[pl.ds(c, n, stride=SUBLANES), :] = val[:, c * LANES:(c + 1) * LANES]


def _load_token_tiles(ref, n):
    return jnp.concatenate([ref[pl.ds(c, n, stride=SUBLANES), :] for c in range(SUBLANES)], axis=1)


def _tile_rows(r0, n):
    start = r0 * SUBLANES
    if not isinstance(start, int):
        start = pl.multiple_of(start, SUBLANES)
    return pl.ds(start, n * SUBLANES)


def _outproj_kernel(ya_ref, yb_ref, x_ref, wa_ref, wb_ref, gpost_ref, gt_ref, gpre_ref,
                    sc_ref, sh_ref, wr_ref, br_ref,
                    x1_ref, h2_ref, idx_ref, rank_ref, gate_ref, cnt_ref, carry_scr):
    i = pl.program_id(0)
    tm = x_ref.shape[0]

    @pl.when(i == 0)
    def _():
        carry_scr[...] = jnp.zeros_like(carry_scr)

    y = (jnp.dot(ya_ref[...], wa_ref[...], preferred_element_type=F32)
         + jnp.dot(yb_ref[...], wb_ref[...], preferred_element_type=F32))
    x1 = x_ref[...] + gt_ref[...] * (_rms(y) * gpost_ref[...])
    x1_ref[...] = x1
    h2 = _rms(x1) * gpre_ref[...] * (1.0 + sc_ref[...]) + sh_ref[...]
    _store_token_tiles(h2_ref, h2)

    h_hi = h2.astype(BF16)
    h_lo = (h2 - h_hi.astype(F32)).astype(BF16)
    logits = (jnp.dot(h_hi, wr_ref[0], preferred_element_type=F32)
              + jnp.dot(h_hi, wr_ref[1], preferred_element_type=F32)
              + jnp.dot(h_lo, wr_ref[0], preferred_element_type=F32)) + br_ref[...]
    ne = logits.shape[1]
    lane = lax.broadcasted_iota(I32, logits.shape, 1)
    lane4 = lax.broadcasted_iota(I32, (tm, TOP_K), 1)
    g = logits
    mask = jnp.zeros(logits.shape, F32)
    vals, picks = [], []
    for _ in range(TOP_K):
        m = jnp.max(g, axis=1, keepdims=True)
        first = jnp.min(jnp.where(g == m, lane, ne), axis=1, keepdims=True)
        pick = lane == first
        mask = jnp.where(pick, 1.0, mask)
        g = jnp.where(pick, -jnp.inf, g)
        vals.append(m)
        picks.append(pick)
    ex = [jnp.exp(v - vals[0]) for v in vals]
    den = ex[0] + ex[1] + ex[2] + ex[3]
    gates = [e / den for e in ex]

    r = lax.broadcasted_iota(I32, (tm, tm), 0)
    c = lax.broadcasted_iota(I32, (tm, tm), 1)
    tril = jnp.where(c < r, 1.0, 0.0).astype(BF16)
    before = jnp.dot(tril, mask.astype(BF16), preferred_element_type=F32) + carry_scr[...]
    total = carry_scr[...] + jnp.sum(mask, axis=0, keepdims=True)
    carry_scr[...] = total
    cnt_ref[...] = jnp.broadcast_to(total, cnt_ref.shape)

    def pack4(cols):
        return jnp.where(lane4 == 0, cols[0],
                         jnp.where(lane4 == 1, cols[1], jnp.where(lane4 == 2, cols[2], cols[3])))

    ranks = [jnp.sum(jnp.where(pk, before, 0.0), axis=1, keepdims=True) for pk in picks]
    idxs = [jnp.sum(jnp.where(pk, lane, 0), axis=1, keepdims=True) for pk in picks]
    idx_ref[...] = pack4(idxs)
    rank_ref[...] = pack4(ranks).astype(I32)
    gate_ref[...] = pack4(gates)


def _outproj(mix_a, mix_b, x2, w_a, w_b, gpost, gt, gpre, sc, sh, w_router, b_router):
    s, d = x2.shape
    tm = ROW_TILE
    ne = w_router.shape[1]
    wa = mix_a.shape[1]
    w_hi = w_router.astype(BF16)
    w_router_hl = jnp.stack([w_hi, (w_router - w_hi.astype(F32)).astype(BF16)])
    row = lambda i: (i, 0)
    fix = lambda i: (0, 0)
    vec = pl.BlockSpec((1, d), fix)
    return pl.pallas_call(
        _outproj_kernel,
        out_shape=(jax.ShapeDtypeStruct((s, d), F32),
                   jax.ShapeDtypeStruct((s * SUBLANES, LANES), F32),
                   jax.ShapeDtypeStruct((s, TOP_K), I32),
                   jax.ShapeDtypeStruct((s, TOP_K), I32),
                   jax.ShapeDtypeStruct((s, TOP_K), F32),
                   jax.ShapeDtypeStruct((8, ne), F32)),
        grid=(s // tm,),
        in_specs=[pl.BlockSpec((tm, wa), row), pl.BlockSpec((tm, wa), row), pl.BlockSpec((tm, d), row),
                  pl.BlockSpec((wa, d), fix), pl.BlockSpec((wa, d), fix),
                  vec, vec, vec, vec, vec,
                  pl.BlockSpec((2, d, ne), lambda i: (0, 0, 0)), pl.BlockSpec((1, ne), fix)],
        out_specs=(pl.BlockSpec((tm, d), row), pl.BlockSpec((tm * SUBLANES, LANES), row),
                   pl.BlockSpec((tm, TOP_K), row), pl.BlockSpec((tm, TOP_K), row),
                   pl.BlockSpec((tm, TOP_K), row), pl.BlockSpec((8, ne), fix)),
        scratch_shapes=[pltpu.VMEM((1, ne), F32)],
        compiler_params=_cparams(("arbitrary",)),
        name="outproj",
    )(mix_a, mix_b, x2, w_a, w_b, gpost, gt, gpre, sc, sh, w_router_hl, b_router)


def _dest_kernel(idx_ref, rank_ref, pstart_ref, o_ref):
    idx = idx_ref[...]
    tm = idx.shape[0]
    ne = pstart_ref.shape[1]
    lane = lax.broadcasted_iota(I32, (tm, ne), 1)
    lane4 = lax.broadcasted_iota(I32, idx.shape, 1)
    out = rank_ref[...]
    for k in range(TOP_K):
        start = jnp.sum(jnp.where(lane == idx[:, k:k + 1], pstart_ref[...], 0), axis=1, keepdims=True)
        out = out + jnp.where(lane4 == k, start, 0)
    o_ref[...] = out


def _dest(idx4, rank4, pstart):
    s = idx4.shape[0]
    tm = min(4 * ROW_TILE, s)
    row = lambda i: (i, 0)
    return pl.pallas_call(
        _dest_kernel,
        out_shape=jax.ShapeDtypeStruct((s, TOP_K), I32),
        grid=(s // tm,),
        in_specs=[pl.BlockSpec((tm, TOP_K), row), pl.BlockSpec((tm, TOP_K), row),
                  pl.BlockSpec((1, N_EXPERTS), lambda i: (0, 0))],
        out_specs=pl.BlockSpec((tm, TOP_K), row),
        compiler_params=_cparams(("arbitrary",)),
        name="dest",
    )(idx4, rank4, pstart.reshape(1, -1))


def _dispatch_kernel(pstart_ref, pblk_ref, nu_ref, dest_ref, h_ref, xs_ref,
                     zero_scr, sem, zsem):
    tm = h_ref.shape[0] // SUBLANES
    bm = zero_scr.shape[0] // SUBLANES
    n_blk = xs_ref.shape[0] // (bm * SUBLANES)

    rows = _tile_rows

    @pl.when(pl.program_id(0) == 0)
    def _():
        zero_scr[...] = jnp.zeros_like(zero_scr)

        def zero_copy(row0):
            return pltpu.make_async_copy(zero_scr, xs_ref.at[rows(row0, bm)], zsem)

        for phase in range(2):
            for e in range(N_EXPERTS):
                last = pstart_ref[e] + (pblk_ref[e] - 1) * bm
                tail = (n_blk - N_EXPERTS + e) * bm
                for cond, row0 in ((pblk_ref[e] > 0, last), (n_blk - N_EXPERTS + e >= nu_ref[0], tail)):
                    @pl.when(cond)
                    def _():
                        if phase == 0:
                            zero_copy(row0).start()
                        else:
                            zero_copy(row0).wait()

    def row_copy(r, k):
        dst = dest_ref[r * TOP_K + k]
        return pltpu.make_async_copy(h_ref.at[rows(r, 1)], xs_ref.at[rows(dst, 1)], sem)

    def start(r, carry):
        for k in range(TOP_K):
            row_copy(r, k).start(priority=k % 2)
        return carry

    lax.fori_loop(0, tm, start, 0, unroll=4)
    for k in range(TOP_K):
        pltpu.make_async_copy(h_ref, xs_ref.at[rows(0, tm)], sem).wait()


def _dispatch(pstart, pblk, n_used, dest_flat, h2, n_rows):
    s = h2.shape[0] // SUBLANES
    tm = COMBINE_TILE
    grid_spec = pltpu.PrefetchScalarGridSpec(
        num_scalar_prefetch=3,
        grid=(s // tm,),
        in_specs=[pl.BlockSpec((tm * TOP_K,), lambda i, *_: (i,), memory_space=pltpu.SMEM),
                  pl.BlockSpec((tm * SUBLANES, LANES), lambda i, *_: (i, 0))],
        out_specs=pl.BlockSpec(memory_space=pl.ANY),
        scratch_shapes=[pltpu.VMEM((EXPERT_BLOCK * SUBLANES, LANES), F32), pltpu.SemaphoreType.DMA,
                        pltpu.SemaphoreType.DMA],
    )
    return pl.pallas_call(
        _dispatch_kernel,
        out_shape=jax.ShapeDtypeStruct((n_rows * SUBLANES, LANES), F32),
        grid_spec=grid_spec,
        compiler_params=_cparams(("arbitrary",)),
        name="dispatch",
    )(pstart, pblk, n_used, dest_flat, h2)


def _experts_kernel(be_ref, nu_ref, xs_ref, wgu_ref, bgu_ref, wd_ref, bd_ref, y_ref,
                    wgu_bf, wd_bf):
    b = pl.program_id(0)
    d_exp = wd_ref.shape[1]
    prev = be_ref[jnp.maximum(b - 1, 0)]
    changed = (b == 0) | (be_ref[b] != prev)

    @pl.when((b < nu_ref[0]) & changed)
    def _():
        rows = 128

        def cast_gu(c, carry):
            r0 = pl.multiple_of(c * rows, rows)
            wgu_bf[pl.ds(r0, rows), :] = wgu_ref[0, pl.ds(r0, rows), :].astype(BF16)
            return carry

        def cast_d(c, carry):
            r0 = pl.multiple_of(c * rows, rows)
            wd_bf[pl.ds(r0, rows), :] = wd_ref[0, pl.ds(r0, rows), :].astype(BF16)
            return carry

        lax.fori_loop(0, wgu_ref.shape[1] // rows, cast_gu, 0)
        lax.fori_loop(0, wd_ref.shape[1] // rows, cast_d, 0)

    @pl.when(b < nu_ref[0])
    def _():
        bm = xs_ref.shape[0] // SUBLANES
        xb = _load_token_tiles(xs_ref, bm).astype(BF16)
        hdn = jnp.dot(xb, wgu_bf[...], preferred_element_type=F32) + bgu_ref[0]
        x_glu = jnp.minimum(hdn[:, :d_exp], SWIGLU_LIMIT)
        x_lin = jnp.clip(hdn[:, d_exp:], -SWIGLU_LIMIT, SWIGLU_LIMIT)
        act = x_glu * jax.nn.sigmoid(SWIGLU_ALPHA * x_glu) * (x_lin + 1.0)
        _store_token_tiles(y_ref, jnp.dot(act.astype(BF16), wd_bf[...], preferred_element_type=F32)
                           + bd_ref[0])

    @pl.when(b >= nu_ref[0])
    def _():
        y_ref[...] = jnp.zeros_like(y_ref)


def _experts(block_e, n_used, xs, w_gate_up, b_gate_up, w_down, b_down):
    n_rows = xs.shape[0] // SUBLANES
    bm = EXPERT_BLOCK
    n_blk = n_rows // bm
    ne, d, two_de = w_gate_up.shape
    de = w_down.shape[1]
    assert d == SUBLANES * LANES

    def blk(b, be, nu):
        return jnp.minimum(b, nu[0] - 1)

    grid_spec = pltpu.PrefetchScalarGridSpec(
        num_scalar_prefetch=2,
        grid=(n_blk,),
        in_specs=[pl.BlockSpec((bm * SUBLANES, LANES), lambda b, be, nu: (blk(b, be, nu), 0)),
                  pl.BlockSpec((1, d, two_de), lambda b, be, nu: (be[blk(b, be, nu)], 0, 0)),
                  pl.BlockSpec((1, 1, two_de), lambda b, be, nu: (be[blk(b, be, nu)], 0, 0)),
                  pl.BlockSpec((1, de, d), lambda b, be, nu: (be[blk(b, be, nu)], 0, 0)),
                  pl.BlockSpec((1, 1, d), lambda b, be, nu: (be[blk(b, be, nu)], 0, 0))],
        out_specs=pl.BlockSpec((bm * SUBLANES, LANES), lambda b, be, nu: (b, 0)),
        scratch_shapes=[pltpu.VMEM((d, two_de), BF16), pltpu.VMEM((de, d), BF16)],
    )
    return pl.pallas_call(
        _experts_kernel,
        out_shape=jax.ShapeDtypeStruct((n_rows * SUBLANES, LANES), F32),
        grid_spec=grid_spec,
        compiler_params=_cparams(("arbitrary",)),
        name="experts",
    )(block_e, n_used, xs, w_gate_up, b_gate_up.reshape(ne, 1, two_de), w_down, b_down.reshape(ne, 1, d))


def _combine_kernel(dest_ref, y_ref, gate_ref, x1_ref, gt_ref, gpost_ref, o_ref, buf, sem):
    tm = x1_ref.shape[0]

    def row_copy(r, k):
        src = dest_ref[r * TOP_K + k]
        return pltpu.make_async_copy(y_ref.at[_tile_rows(src, 1)], buf.at[k, _tile_rows(r, 1)], sem)

    def start(r, carry):
        for k in range(TOP_K):
            row_copy(r, k).start(priority=k % 2)
        return carry

    lax.fori_loop(0, tm, start, 0, unroll=4)
    for k in range(TOP_K):
        pltpu.make_async_copy(y_ref.at[_tile_rows(0, tm)], buf.at[k], sem).wait()

    gate = gate_ref[...]
    acc = gate[:, 0:1] * _load_token_tiles(buf.at[0], tm)
    for k in range(1, TOP_K):
        acc = acc + gate[:, k:k + 1] * _load_token_tiles(buf.at[k], tm)
    o_ref[...] = x1_ref[...] + gt_ref[...] * (_rms(acc) * gpost_ref[...])


def _combine(dest_flat, y, gate4, x1, gt, gpost):
    s, d = x1.shape
    tm = COMBINE_TILE
    return pl.pallas_call(
        _combine_kernel,
        out_shape=jax.ShapeDtypeStruct((s, d), F32),
        grid=(s // tm,),
        in_specs=[pl.BlockSpec((tm * TOP_K,), lambda i: (i,), memory_space=pltpu.SMEM),
                  pl.BlockSpec(memory_space=pl.ANY),
                  pl.BlockSpec((tm, TOP_K), lambda i: (i, 0)),
                  pl.BlockSpec((tm, d), lambda i: (i, 0)),
                  pl.BlockSpec((1, d), lambda i: (0, 0)),
                  pl.BlockSpec((1, d), lambda i: (0, 0))],
        out_specs=pl.BlockSpec((tm, d), lambda i: (i, 0)),
        scratch_shapes=[pltpu.VMEM((TOP_K, tm * SUBLANES, LANES), F32), pltpu.SemaphoreType.DMA],
        compiler_params=_cparams(("arbitrary",)),
        name="combine",
    )(dest_flat, y, gate4, x1, gt, gpost)


def _layer(x2, mod, g_pre_mix, g_post_mix, w_in, b_forget, rel_bias, w_out,
           g_pre_ffn, g_post_ffn, w_router, b_router, w_gate_up, b_gate_up, w_down, b_down):
    s, d = x2.shape
    sh_m, sc_m, gt_m, sh_f, sc_f, gt_f = [mod[:, k * d:(k + 1) * d] for k in range(6)]
    n_qkv = 3 * (N_HEADS_FOX + N_HEADS_MOBA) * HEAD_DIM
    fox_w = N_HEADS_FOX * HEAD_DIM

    w_qkv = w_in[:, :n_qkv].astype(BF16)
    w_f = w_in[:, n_qkv:].T
    qkv, cum, sel, nrm, tr, nr = _inproj(x2, g_pre_mix.reshape(1, d), sc_m, sh_m, w_qkv, w_f,
                                         b_forget.reshape(-1, 1))
    y_a = _fox(qkv, tr, nr, cum, nrm)
    y_b = _moba(qkv, tr, sel, rel_bias)

    w_out_bf = w_out.astype(BF16)
    x1, h2, idx4, rank4, gate4, cnt = _outproj(
        y_a, y_b, x2, w_out_bf[:fox_w], w_out_bf[fox_w:], g_post_mix.reshape(1, d), gt_m,
        g_pre_ffn.reshape(1, d), sc_f, sh_f, w_router, b_router.reshape(1, -1))

    bm = EXPERT_BLOCK
    counts = cnt[0].astype(I32)
    pblk = (counts + bm - 1) // bm
    pend_blk = jnp.cumsum(pblk)
    pstart = ((pend_blk - pblk) * bm).astype(I32)
    n_rows = s * TOP_K + N_EXPERTS * bm
    n_blk = n_rows // bm
    block_e = jnp.minimum(jnp.sum(pend_blk[None, :] <= jnp.arange(n_blk)[:, None], axis=1),
                          N_EXPERTS - 1).astype(I32)
    n_used = pend_blk[-1:].astype(I32)

    dest_flat = _dest(idx4, rank4, pstart).reshape(-1)
    xs = _dispatch(pstart, pblk.astype(I32), n_used, dest_flat, h2, n_rows)
    y = _experts(block_e, n_used, xs, w_gate_up, b_gate_up, w_down, b_down)
    return _combine(dest_flat, y, gate4, x1, gt_f, g_post_ffn.reshape(1, d))


def kernel(x, c, w_ada, b_ada, g_pre_mix, g_post_mix, w_in, b_forget, rel_bias, w_out, g_pre_ffn, g_post_ffn, w_router, b_router, w_gate_up, b_gate_up, w_down, b_down):
    bsz, s, d = x.shape
    depth = w_ada.shape[0]
    outs = []
    for bi in range(bsz):
        x2 = x[bi]
        for l in range(depth):
            mod = _adaln(c[bi:bi + 1], w_ada[l], b_ada[l])
            x2 = _layer(x2, mod, g_pre_mix[l], g_post_mix[l], w_in[l], b_forget[l], rel_bias, w_out[l],
                        g_pre_ffn[l], g_post_ffn[l], w_router[l], b_router[l], w_gate_up[l], b_gate_up[l],
                        w_down[l], b_down[l])
        outs.append(x2)
    return outs[0].reshape(1, s, d) if bsz == 1 else jnp.stack(outs)
```

```python
import math

import numpy as np
import jax
import jax.numpy as jnp
from jax import lax
from jax.experimental import pallas as pl
from jax.experimental.pallas import tpu as pltpu

F32 = jnp.float32
BF16 = jnp.bfloat16
I32 = jnp.int32

HEAD_DIM = 64
N_HEADS_FOX = 8
N_HEADS_MOBA = 8
PAIR = 2 * HEAD_DIM
MOBA_BLOCK = 256
MOBA_TOPK = 3
NUM_BUCKETS = 32
MAX_DISTANCE = 128
N_EXPERTS = 32
TOP_K = 4
SWIGLU_LIMIT = 7.0
SWIGLU_ALPHA = 1.702
RMS_EPS = 1e-6
NEG = -(2.0 ** 100)
M_INIT = -(2.0 ** 99)
LOG2E = math.log2(math.e)
SUM_ROWS = 16
SUBLANES = 8
LANES = 128
EXP_UNDERFLOW = 90.0
VMEM_LIMIT = 56 * 1024 * 1024

ROW_TILE = 512
FOX_TILE = 256
EXPERT_BLOCK = 512
COMBINE_TILE = 1024
MOBA_LAG = 1
MOBA_UNROLL = 4

NT_DIMS = (((1,), (1,)), ((), ()))


def _cparams(sem):
    return pltpu.CompilerParams(dimension_semantics=sem, vmem_limit_bytes=VMEM_LIMIT)


def _rms(x):
    return x * lax.rsqrt(jnp.mean(x * x, axis=-1, keepdims=True) + RMS_EPS)


def _adaln_kernel(c_ref, w_ref, b_ref, o_ref):
    c = c_ref[...]
    cond = c * jax.nn.sigmoid(c)
    o_ref[...] = jnp.dot(cond, w_ref[...], preferred_element_type=F32,
                         precision=lax.Precision.HIGHEST) + b_ref[...]


def _adaln(c, w_ada, b_ada):
    d = c.shape[-1]
    n = w_ada.shape[-1]
    c8 = jnp.broadcast_to(c.reshape(1, d), (8, d))
    out = pl.pallas_call(
        _adaln_kernel,
        out_shape=jax.ShapeDtypeStruct((8, n), F32),
        grid=(n // d,),
        in_specs=[pl.BlockSpec((8, d), lambda j: (0, 0)),
                  pl.BlockSpec((d, d), lambda j: (0, j)),
                  pl.BlockSpec((1, d), lambda j: (0, j))],
        out_specs=pl.BlockSpec((8, d), lambda j: (0, j)),
        compiler_params=_cparams(("arbitrary",)),
        name="adaln",
    )(c8, w_ada, b_ada.reshape(1, n))
    return out[0:1]


def _inproj_kernel(x_ref, g_ref, sc_ref, sh_ref, w_ref, wf_ref, bf_ref,
                   qkv_ref, cum_ref, sel_ref, nrm_ref, tr_ref, nr_ref, km_scr, carry_scr):
    i = pl.program_id(0)
    tm = x_ref.shape[0]
    nblk = km_scr.shape[0]

    @pl.when(i == 0)
    def _():
        km_scr[...] = jnp.zeros_like(km_scr)
        carry_scr[...] = jnp.zeros_like(carry_scr)

    x = x_ref[...]
    h = _rms(x) * g_ref[...] * (1.0 + sc_ref[...]) + sh_ref[...]
    hb = h.astype(BF16)

    width = N_HEADS_FOX * HEAD_DIM
    hsel = jnp.where(lax.broadcasted_iota(I32, (width, N_HEADS_FOX), 0) // HEAD_DIM
                     == lax.broadcasted_iota(I32, (width, N_HEADS_FOX), 1), 1.0, 0.0)
    kb = None
    tr_slot = {3: 0, 5: 1, 0: 2, 2: 3}
    qbt = None
    for c in range(6):
        pc = jnp.dot(hb, w_ref[:, c * width:(c + 1) * width], preferred_element_type=F32)
        if c in tr_slot:
            n = tr_slot[c]
            pct = pc.T
            if c == 3:
                qbt = pct
            if c == 0 or c == 3:
                pct = pct * (LOG2E * HEAD_DIM ** -0.5)
            tr_ref[n * width:(n + 1) * width, :] = pct.astype(BF16)
        if c == 1 or c == 4:
            qkv_ref[:, (c // 3) * width:(c // 3 + 1) * width] = pc.astype(BF16)
        if c < 2:
            sq = (pc * (HEAD_DIM ** -0.5) if c == 0 else pc).astype(BF16).astype(F32)
            n2 = jnp.dot((sq * sq).astype(BF16), hsel.astype(BF16), preferred_element_type=F32)
            nrm_ref[0, c:c + 1, :] = jnp.max(n2, axis=0, keepdims=True)
        if c == 4:
            kb = pc

    ft = lax.dot_general(wf_ref[...].astype(BF16), hb, NT_DIMS, preferred_element_type=F32)
    z = ft + bf_ref[...]
    logf = -(jnp.maximum(-z, 0.0) + jnp.log1p(jnp.exp(-jnp.abs(z))))
    lane = lax.broadcasted_iota(I32, logf.shape, 1)
    cs = logf
    sh = 1
    while sh < tm:
        cs = cs + jnp.where(lane >= sh, pltpu.roll(cs, sh, axis=1), 0.0)
        sh *= 2
    base = jnp.zeros_like(cs)
    for b in range(1, tm // FOX_TILE):
        base = jnp.where(lane >= b * FOX_TILE, cs[:, b * FOX_TILE - 1:b * FOX_TILE], base)
    nr_t = (-LOG2E * (cs - base)).T
    for pp in range(nr_ref.shape[0]):
        nr_ref[pp] = nr_t[:, 2 * pp:2 * pp + 2]
    cs = cs + carry_scr[...]
    cum_ref[...] = cs
    carry_scr[...] = cs[:, tm - 1:tm]

    nb_tile = tm // MOBA_BLOCK
    for b in range(nb_tile):
        kmean = jnp.sum(kb[b * MOBA_BLOCK:(b + 1) * MOBA_BLOCK], axis=0, keepdims=True) * (1.0 / MOBA_BLOCK)
        km_scr[pl.ds(i * nb_tile + b, 1), :] = kmean

    km = km_scr[...]
    blk = lax.broadcasted_iota(I32, (nblk, tm), 0)
    col = lax.broadcasted_iota(I32, (nblk, tm), 1)
    own = i * nb_tile + col // MOBA_BLOCK
    for hd in range(N_HEADS_MOBA):
        hs = slice(hd * HEAD_DIM, (hd + 1) * HEAD_DIM)
        g = jnp.dot(km[:, hs], qbt[hs, :], preferred_element_type=F32,
                    precision=lax.Precision.HIGHEST)
        g = jnp.where(blk < own, g, -jnp.inf)
        sel = jnp.zeros(g.shape, dtype=jnp.bool_)
        for _ in range(MOBA_TOPK):
            m = jnp.max(g, axis=0, keepdims=True)
            first = jnp.min(jnp.where(g == m, blk, nblk), axis=0, keepdims=True)
            pick = (blk == first) & (m > -jnp.inf)
            sel = sel | pick
            g = jnp.where(pick, -jnp.inf, g)
        sel_ref[hd] = jnp.where(sel, 0.0, NEG)


def _inproj(x2, g, sc, sh, w_qkv, w_f, b_f):
    s, d = x2.shape
    tm = ROW_TILE
    nblk = s // MOBA_BLOCK
    n = w_qkv.shape[1]
    width = N_HEADS_FOX * HEAD_DIM
    n_rows_out = 2 * width
    n_tr = 4 * width
    assert tm % FOX_TILE == 0 and N_HEADS_FOX == N_HEADS_MOBA
    return pl.pallas_call(
        _inproj_kernel,
        out_shape=(jax.ShapeDtypeStruct((s, n_rows_out), BF16),
                   jax.ShapeDtypeStruct((N_HEADS_FOX, s), F32),
                   jax.ShapeDtypeStruct((N_HEADS_MOBA, nblk, s), F32),
                   jax.ShapeDtypeStruct((s // tm, 2, N_HEADS_FOX), F32),
                   jax.ShapeDtypeStruct((n_tr, s), BF16),
                   jax.ShapeDtypeStruct((N_HEADS_FOX // 2, s, 2), F32)),
        grid=(s // tm,),
        in_specs=[pl.BlockSpec((tm, d), lambda i: (i, 0)),
                  pl.BlockSpec((1, d), lambda i: (0, 0)),
                  pl.BlockSpec((1, d), lambda i: (0, 0)),
                  pl.BlockSpec((1, d), lambda i: (0, 0)),
                  pl.BlockSpec((d, n), lambda i: (0, 0)),
                  pl.BlockSpec((N_HEADS_FOX, d), lambda i: (0, 0)),
                  pl.BlockSpec((N_HEADS_FOX, 1), lambda i: (0, 0))],
        out_specs=(pl.BlockSpec((tm, n_rows_out), lambda i: (i, 0)),
                   pl.BlockSpec((N_HEADS_FOX, tm), lambda i: (0, i)),
                   pl.BlockSpec((N_HEADS_MOBA, nblk, tm), lambda i: (0, 0, i)),
                   pl.BlockSpec((1, 2, N_HEADS_FOX), lambda i: (i, 0, 0)),
                   pl.BlockSpec((n_tr, tm), lambda i: (0, i)),
                   pl.BlockSpec((N_HEADS_FOX // 2, tm, 2), lambda i: (0, i, 0))),
        scratch_shapes=[pltpu.VMEM((nblk, N_HEADS_MOBA * HEAD_DIM), F32),
                        pltpu.VMEM((N_HEADS_FOX, 1), F32)],
        compiler_params=_cparams(("arbitrary",)),
        name="inproj",
    )(x2, g, sc, sh, w_qkv, w_f, b_f)


def _fox_kernel(jlo_ref, cp_ref, qt_ref, k_ref, vt_ref, nr_ref, o_ref, m_ref, a_ref):
    p = pl.program_id(0)
    i = pl.program_id(1)
    j_first = jlo_ref[p * pl.num_programs(1) + i]
    t = qt_ref.shape[1]

    qt = qt_ref[...]
    top = lax.broadcasted_iota(I32, qt.shape, 0) < HEAD_DIM
    zq = jnp.zeros_like(qt)
    qth = (jnp.where(top, qt, zq), jnp.where(top, zq, qt))
    m_ref[...] = jnp.full(m_ref.shape, M_INIT, F32)
    a_ref[...] = jnp.zeros(a_ref.shape, F32)
    er = lax.broadcasted_iota(I32, (SUM_ROWS, 2 * t), 0)
    ec = lax.broadcasted_iota(I32, (SUM_ROWS, 2 * t), 1)
    ones_rows = jnp.where(((er == 0) & (ec < t)) | ((er == 1) & (ec >= t)), 1.0, 0.0).astype(BF16)
    arow = lax.broadcasted_iota(I32, a_ref.shape, 0)
    head0_rows = (arow < HEAD_DIM) | (arow == PAIR)

    def scores(j, diagonal):
        k0 = pl.multiple_of(j * t, t)
        kt = k_ref[pl.ds(k0, t), :]
        out = []
        for hd in range(2):
            s = jnp.dot(kt, qth[hd], preferred_element_type=F32) + nr_ref[0, pl.ds(k0, t), hd:hd + 1]
            if diagonal:
                r = lax.broadcasted_iota(I32, s.shape, 0)
                c = lax.broadcasted_iota(I32, s.shape, 1)
                s = jnp.where(r <= c, s, NEG)
            out.append(s)
        return out

    def softmax(ss, j, valid):
        ps, alphas = [], []
        for hd in range(2):
            h = 2 * p + hd
            pair_shift = LOG2E * (cp_ref[h, i] - cp_ref[h, j])
            if valid is not None:
                pair_shift = jnp.where(valid, pair_shift, NEG)
            m_old = m_ref[hd]
            m_new = jnp.maximum(m_old, jnp.max(ss[hd], axis=0, keepdims=True) + pair_shift)
            alphas.append(jnp.exp2(m_old - m_new))
            m_ref[hd] = m_new
            ps.append(jnp.exp2(ss[hd] - (m_new - pair_shift)).astype(BF16))
        return ps, alphas

    def accumulate(j, ps, alphas):
        k0 = pl.multiple_of(j * t, t)
        vt = vt_ref[:, pl.ds(k0, t)]
        zv = jnp.zeros_like(vt)
        vcat = jnp.concatenate([jnp.where(top, vt, zv), jnp.where(top, zv, vt)], axis=1)
        vcat = jnp.concatenate([vcat, ones_rows], axis=0)
        pcat = jnp.concatenate(ps, axis=0)
        a_ref[...] = (a_ref[...] * jnp.where(head0_rows, alphas[0], alphas[1])
                      + jnp.dot(vcat, pcat, preferred_element_type=F32))

    def body(j, carry):
        accumulate(j, *softmax(scores(j, False), j, None))
        return carry

    lax.fori_loop(j_first, i - 1, body, 0)
    jp = jnp.maximum(i - 1, 0)
    ss_prev = scores(jp, False)
    ss_diag = scores(i, True)
    w_prev = softmax(ss_prev, jp, i >= 1)
    w_diag = softmax(ss_diag, i, None)
    accumulate(jp, *w_prev)
    accumulate(i, *w_diag)
    out_t =a_ref[0:PAIR, :] / jnp.where(top, a_ref[PAIR:PAIR + 1, :], a_ref[PAIR + 1:PAIR + 2, :])
    o_ref[...] = out_t.T.astype(o_ref.dtype)


def _fox_first_tile(cum, nrm, t):
    cend = cum[:, t - 1::t]
    nt = cend.shape[1]
    cprev = jnp.concatenate([jnp.zeros((cend.shape[0], 1), F32), cend[:, :-1]], axis=1)
    rep = nt // nrm.shape[0]
    qn = jnp.repeat(jnp.sqrt(nrm[:, 0, :]).T, rep, axis=1)
    kn = jnp.repeat(jnp.sqrt(nrm[:, 1, :]).T, rep, axis=1)
    gap = (1.02 * qn[:, :, None] * (kn[:, None, :] + kn[:, :, None])
           + cprev[:, :, None] - cend[:, None, :])
    jj = jnp.arange(nt)[None, None, :]
    ii = jnp.arange(nt)[None, :, None]
    needed = (jj < ii) & jnp.logical_not(gap <= -EXP_UNDERFLOW)
    needed = needed[0::2] | needed[1::2]
    first = jnp.min(jnp.where(needed, jj, ii), axis=2)
    return first.reshape(-1).astype(I32)


def _fox(qkv, tr, nr, cum, nrm):
    s = qkv.shape[0]
    t = FOX_TILE
    npair = N_HEADS_FOX // 2
    jlo = _fox_first_tile(cum, nrm, t)
    cend = cum[:, t - 1::t]
    cprev = jnp.concatenate([jnp.zeros((cend.shape[0], 1), F32), cend[:, :-1]], axis=1)
    q_rows = 2 * (N_HEADS_MOBA // 2)
    grid_spec = pltpu.PrefetchScalarGridSpec(
        num_scalar_prefetch=1,
        grid=(npair, s // t),
        in_specs=[pl.BlockSpec(memory_space=pltpu.SMEM),
                  pl.BlockSpec((PAIR, t), lambda p, i, jl: (q_rows + p, i)),
                  pl.BlockSpec((s, PAIR), lambda p, i, jl: (0, p)),
                  pl.BlockSpec((PAIR, s), lambda p, i, jl: (q_rows + npair + p, 0)),
                  pl.BlockSpec((1, s, 2), lambda p, i, jl: (p, 0, 0))],
        out_specs=pl.BlockSpec((t, PAIR), lambda p, i, jl: (i, p)),
        scratch_shapes=[pltpu.VMEM((2, 1, t), F32), pltpu.VMEM((PAIR + SUM_ROWS, t), F32)],
    )
    return pl.pallas_call(
        _fox_kernel,
        out_shape=jax.ShapeDtypeStruct((s, npair * PAIR), BF16),
        grid_spec=grid_spec,
        compiler_params=_cparams(("arbitrary", "arbitrary")),
        name="fox",
    )(jlo, cprev, tr, qkv, tr, nr)


def _t5_bucket_np(dist):
    dist = np.maximum(dist, 0)
    max_exact = NUM_BUCKETS // 2
    d = np.maximum(dist, 1).astype(np.float32)
    large = max_exact + (np.log(d / np.float32(max_exact)) / np.float32(math.log(MAX_DISTANCE / max_exact))
                         * np.float32(NUM_BUCKETS - max_exact)).astype(np.int32)
    large = np.minimum(large, NUM_BUCKETS - 1)
    return np.where(dist < max_exact, dist, large).astype(np.int32)


def _moba_kernel(rb_ref, qt_ref, k_ref, vt_ref, sel_ref, bkt_ref, o_ref,
                 m_ref, a_ref, bias_scr, *bufs):
    s_bufs = bufs[0:MOBA_UNROLL]
    p_bufs = bufs[MOBA_UNROLL:2 * MOBA_UNROLL]
    al_bufs = bufs[2 * MOBA_UNROLL:3 * MOBA_UNROLL]
    p = pl.program_id(0)
    g = pl.program_id(1)
    tq = qt_ref.shape[1]
    t = tq // 2

    @pl.when(g == 0)
    def _():
        r = lax.broadcasted_iota(I32, (t, t), 0)
        c = lax.broadcasted_iota(I32, (t, t), 1)
        zero = jnp.zeros((t, t), F32)
        for hd in range(2):
            h = 2 * p + hd
            far = rb_ref[(NUM_BUCKETS - 1) * N_HEADS_MOBA + h]
            tiles = []
            for w in range(2):
                bkt = bkt_ref[w]
                acc = jnp.zeros(bkt.shape, F32)
                for kk in range(NUM_BUCKETS):
                    acc = acc + jnp.where(bkt == kk, rb_ref[kk * N_HEADS_MOBA + h], 0.0)
                tiles.append((acc - far) * LOG2E)
            prev_t = tiles[0]
            own_t = jnp.where(r <= c, tiles[1], NEG)
            bias_scr[hd, 0] = jnp.concatenate([prev_t, zero], axis=1)
            bias_scr[hd, 1] = jnp.concatenate([own_t, prev_t], axis=1)
            bias_scr[hd, 2] = jnp.concatenate([zero, own_t], axis=1)

    qt = qt_ref[...]
    top = lax.broadcasted_iota(I32, qt.shape, 0) < HEAD_DIM
    zq = jnp.zeros_like(qt)
    qth = (jnp.where(top, qt, zq), jnp.where(top, zq, qt))
    m_ref[...] = jnp.full(m_ref.shape, M_INIT, F32)
    a_ref[...] = jnp.zeros(a_ref.shape, F32)

    er = lax.broadcasted_iota(I32, (SUM_ROWS, 2 * t), 0)
    ec = lax.broadcasted_iota(I32, (SUM_ROWS, 2 * t), 1)
    ones_rows = jnp.where(((er == 0) & (ec < t)) | ((er == 1) & (ec >= t)), 1.0, 0.0).astype(BF16)
    arow = lax.broadcasted_iota(I32, a_ref.shape, 0)
    head0_rows = (arow < HEAD_DIM) | (arow == PAIR)
    vtop = lax.broadcasted_iota(I32, (PAIR, t), 0) < HEAD_DIM
    in_a = lax.broadcasted_iota(I32, (1, tq), 1) < t

    n_far = jnp.maximum(2 * g - 1, 0)

    n_key_tiles = k_ref.shape[0] // t

    def produce(j, s_buf):
        j = jnp.minimum(j, n_key_tiles - 1)
        k0 = pl.multiple_of(j * t, t)
        kt = k_ref[pl.ds(k0, t), :]
        for hd in range(2):
            s_buf[hd] = jnp.dot(kt, qth[hd], preferred_element_type=F32)

    def softmax(s_buf, p_buf, al_buf, selrows, w):
        for hd in range(2):
            s = s_buf[hd]
            if w is not None:
                s = s + bias_scr[hd, w]
            smax = jnp.max(s, axis=0, keepdims=True)
            m_old = m_ref[hd]
            m_new = jnp.maximum(m_old, smax + selrows[hd])
            shift = m_new - selrows[hd]
            al_buf[hd] = jnp.exp2(m_old - m_new)
            m_ref[hd] = m_new
            p_buf[hd * t:(hd + 1) * t, :] = jnp.exp2(s - shift).astype(BF16)

    def far_rows(j):
        return [jnp.where(j < n_far, sel_ref[hd, pl.ds(j, 1), :], NEG) for hd in range(2)]

    def accumulate(j, p_buf, al_buf):
        k0 = pl.multiple_of(j * t, t)
        vt = vt_ref[:, pl.ds(k0, t)]
        zv = jnp.zeros_like(vt)
        vcat = jnp.concatenate([jnp.where(vtop, vt, zv), jnp.where(vtop, zv, vt)], axis=1)
        vcat = jnp.concatenate([vcat, ones_rows], axis=0)
        a_ref[...] = (a_ref[...] * jnp.where(head0_rows, al_buf[0], al_buf[1])
                      + jnp.dot(vcat, p_buf[...], preferred_element_type=F32))

    un = MOBA_UNROLL
    lag = MOBA_LAG
    for n in range(un - lag, un):
        p_bufs[n][...] = jnp.zeros_like(p_bufs[n])
        al_bufs[n][...] = jnp.ones_like(al_bufs[n])
    produce(0, s_bufs[0])

    def body(u, carry):
        j0 = un * u
        for n in range(un):
            accumulate(jnp.maximum(j0 + n - lag, 0), p_bufs[(n - lag) % un], al_bufs[(n - lag) % un])
            softmax(s_bufs[n], p_bufs[n], al_bufs[n], far_rows(j0 + n), None)
            produce(j0 + n + 1, s_bufs[(n + 1) % un])
        return carry

    n_trips = (n_far + un - 1) // un
    lax.fori_loop(0, n_trips, body, 0)

    j1 = jnp.maximum(2 * g - 1, 0)
    j2 = 2 * g
    j3 = 2 * g + 1
    rows1 = [jnp.where(g >= 1, sel_ref[hd, pl.ds(j1, 1), :], NEG) for hd in range(2)]
    rows2 = [jnp.where(in_a, 0.0, sel_ref[hd, pl.ds(j2, 1), :]) for hd in range(2)]
    rows3 = [jnp.where(in_a, NEG, 0.0)] * 2
    produce(j1, s_bufs[0])
    produce(j2, s_bufs[1])
    for n in range(un - lag, un):
        accumulate(jnp.maximum(un * n_trips - un + n, 0), p_bufs[n], al_bufs[n])
    softmax(s_bufs[0], p_bufs[0], al_bufs[0], rows1, 0)
    produce(j3, s_bufs[0])
    softmax(s_bufs[1], p_bufs[1], al_bufs[1], rows2, 1)
    accumulate(j1, p_bufs[0], al_bufs[0])
    softmax(s_bufs[0], p_bufs[2], al_bufs[2], rows3, 2)
    accumulate(j2, p_bufs[1], al_bufs[1])
    accumulate(j3, p_bufs[2], al_bufs[2])
    out_t = a_ref[0:PAIR, :] / jnp.where(top, a_ref[PAIR:PAIR + 1, :], a_ref[PAIR + 1:PAIR + 2, :])
    o_ref[...] = out_t.T.astype(o_ref.dtype)


def _moba(qkv, tr, sel, rel_bias):
    s = qkv.shape[0]
    t = MOBA_BLOCK
    npair = N_HEADS_MOBA // 2
    nblk = s // t
    kcol = N_HEADS_FOX // 2
    a = np.arange(t)[None, :]
    b = np.arange(t)[:, None]
    bkt = jnp.asarray(np.stack([_t5_bucket_np(t + a - b), _t5_bucket_np(a - b)]))
    tq = 2 * t
    grid_spec = pltpu.PrefetchScalarGridSpec(
        num_scalar_prefetch=1,
        grid=(npair, s // tq),
        in_specs=[pl.BlockSpec((PAIR, tq), lambda p, i, rb: (p, i)),
                  pl.BlockSpec((s, PAIR), lambda p, i, rb: (0, kcol + p)),
                  pl.BlockSpec((PAIR, s), lambda p, i, rb: (npair + p, 0)),
                  pl.BlockSpec((2, nblk, tq), lambda p, i, rb: (p, 0, i)),
                  pl.BlockSpec((2, t, t), lambda p, i, rb: (0, 0, 0))],
        out_specs=pl.BlockSpec((tq, PAIR), lambda p, i, rb: (i, p)),
        scratch_shapes=[pltpu.VMEM((2, 1, tq), F32),
                        pltpu.VMEM((PAIR + SUM_ROWS, tq), F32), pltpu.VMEM((2, 3, t, tq), F32)]
        + [pltpu.VMEM((2, t, tq), F32)] * MOBA_UNROLL
        + [pltpu.VMEM((2 * t, tq), BF16)] * MOBA_UNROLL
        + [pltpu.VMEM((2, 1, tq), F32)] * MOBA_UNROLL,
    )
    return pl.pallas_call(
        _moba_kernel,
        out_shape=jax.ShapeDtypeStruct((s, npair * PAIR), BF16),
        grid_spec=grid_spec,
        compiler_params=_cparams(("arbitrary", "arbitrary")),
        name="moba",
    )(rel_bias.reshape(-1), tr, qkv, tr, sel, bkt)


def _store_token_tiles(ref, val):
    n = val.shape[0]
    for c in range(SUBLANES):
        ref[pl.ds(c, n, stride=SUBLANES), :] = val[:, c * LANES:(c + 1) * LANES]


def _load_token_tiles(ref, n):
    return jnp.concatenate([ref[pl.ds(c, n, stride=SUBLANES), :] for c in range(SUBLANES)], axis=1)


def _tile_rows(r0, n):
    start = r0 * SUBLANES
    if not isinstance(start, int):
        start = pl.multiple_of(start, SUBLANES)
    return pl.ds(start, n * SUBLANES)


def _outproj_kernel(ya_ref, yb_ref, x_ref, wa_ref, wb_ref, gpost_ref, gt_ref, gpre_ref,
                    sc_ref, sh_ref, wr_ref, br_ref,
                    x1_ref, h2_ref, idx_ref, rank_ref, gate_ref, cnt_ref, carry_scr):
    i = pl.program_id(0)
    tm = x_ref.shape[0]

    @pl.when(i == 0)
    def _():
        carry_scr[...] = jnp.zeros_like(carry_scr)

    y = (jnp.dot(ya_ref[...], wa_ref[...], preferred_element_type=F32)
         + jnp.dot(yb_ref[...], wb_ref[...], preferred_element_type=F32))
    x1 = x_ref[...] + gt_ref[...] * (_rms(y) * gpost_ref[...])
    x1_ref[...] = x1
    h2 = _rms(x1) * gpre_ref[...] * (1.0 + sc_ref[...]) + sh_ref[...]
    _store_token_tiles(h2_ref, h2)

    h_hi = h2.astype(BF16)
    h_lo = (h2 - h_hi.astype(F32)).astype(BF16)
    logits = (jnp.dot(h_hi, wr_ref[0], preferred_element_type=F32)
              + jnp.dot(h_hi, wr_ref[1], preferred_element_type=F32)
              + jnp.dot(h_lo, wr_ref[0], preferred_element_type=F32)) + br_ref[...]
    ne = logits.shape[1]
    lane = lax.broadcasted_iota(I32, logits.shape, 1)
    lane4 = lax.broadcasted_iota(I32, (tm, TOP_K), 1)
    g = logits
    mask = jnp.zeros(logits.shape, F32)
    vals, picks = [], []
    for _ in range(TOP_K):
        m = jnp.max(g, axis=1, keepdims=True)
        first = jnp.min(jnp.where(g == m, lane, ne), axis=1, keepdims=True)
        pick = lane == first
        mask = jnp.where(pick, 1.0, mask)
        g = jnp.where(pick, -jnp.inf, g)
        vals.append(m)
        picks.append(pick)
    ex = [jnp.exp(v - vals[0]) for v in vals]
    den = ex[0] + ex[1] + ex[2] + ex[3]
    gates = [e / den for e in ex]

    r = lax.broadcasted_iota(I32, (tm, tm), 0)
    c = lax.broadcasted_iota(I32, (tm, tm), 1)
    tril = jnp.where(c < r, 1.0, 0.0).astype(BF16)
    before = jnp.dot(tril, mask.astype(BF16), preferred_element_type=F32) + carry_scr[...]
    total = carry_scr[...] + jnp.sum(mask, axis=0, keepdims=True)
    carry_scr[...] = total
    cnt_ref[...] = jnp.broadcast_to(total, cnt_ref.shape)

    def pack4(cols):
        return jnp.where(lane4 == 0, cols[0],
                         jnp.where(lane4 == 1, cols[1], jnp.where(lane4 == 2, cols[2], cols[3])))

    ranks = [jnp.sum(jnp.where(pk, before, 0.0), axis=1, keepdims=True) for pk in picks]
    idxs = [jnp.sum(jnp.where(pk, lane, 0), axis=1, keepdims=True) for pk in picks]
    idx_ref[...] = pack4(idxs)
    rank_ref[...] = pack4(ranks).astype(I32)
    gate_ref[...] = pack4(gates)


def _outproj(mix_a, mix_b, x2, w_a, w_b, gpost, gt, gpre, sc, sh, w_router, b_router):
    s, d = x2.shape
    tm = ROW_TILE
    ne = w_router.shape[1]
    wa = mix_a.shape[1]
    w_hi = w_router.astype(BF16)
    w_router_hl = jnp.stack([w_hi, (w_router - w_hi.astype(F32)).astype(BF16)])
    row = lambda i: (i, 0)
    fix = lambda i: (0, 0)
    vec = pl.BlockSpec((1, d), fix)
    return pl.pallas_call(
        _outproj_kernel,
        out_shape=(jax.ShapeDtypeStruct((s, d), F32),
                   jax.ShapeDtypeStruct((s * SUBLANES, LANES), F32),
                   jax.ShapeDtypeStruct((s, TOP_K), I32),
                   jax.ShapeDtypeStruct((s, TOP_K), I32),
                   jax.ShapeDtypeStruct((s, TOP_K), F32),
                   jax.ShapeDtypeStruct((8, ne), F32)),
        grid=(s // tm,),
        in_specs=[pl.BlockSpec((tm, wa), row), pl.BlockSpec((tm, wa), row), pl.BlockSpec((tm, d), row),
                  pl.BlockSpec((wa, d), fix), pl.BlockSpec((wa, d), fix),
                  vec, vec, vec, vec, vec,
                  pl.BlockSpec((2, d, ne), lambda i: (0, 0, 0)), pl.BlockSpec((1, ne), fix)],
        out_specs=(pl.BlockSpec((tm, d), row), pl.BlockSpec((tm * SUBLANES, LANES), row),
                   pl.BlockSpec((tm, TOP_K), row), pl.BlockSpec((tm, TOP_K), row),
                   pl.BlockSpec((tm, TOP_K), row), pl.BlockSpec((8, ne), fix)),
        scratch_shapes=[pltpu.VMEM((1, ne), F32)],
        compiler_params=_cparams(("arbitrary",)),
        name="outproj",
    )(mix_a, mix_b, x2, w_a, w_b, gpost, gt, gpre, sc, sh, w_router_hl, b_router)


def _dest_kernel(idx_ref, rank_ref, pstart_ref, o_ref):
    idx = idx_ref[...]
    tm = idx.shape[0]
    ne = pstart_ref.shape[1]
    lane = lax.broadcasted_iota(I32, (tm, ne), 1)
    lane4 = lax.broadcasted_iota(I32, idx.shape, 1)
    out = rank_ref[...]
    for k in range(TOP_K):
        start = jnp.sum(jnp.where(lane == idx[:, k:k + 1], pstart_ref[...], 0), axis=1, keepdims=True)
        out = out + jnp.where(lane4 == k, start, 0)
    o_ref[...] = out


def _dest(idx4, rank4, pstart):
    s = idx4.shape[0]
    tm = min(4 * ROW_TILE, s)
    row = lambda i: (i, 0)
    return pl.pallas_call(
        _dest_kernel,
        out_shape=jax.ShapeDtypeStruct((s, TOP_K), I32),
        grid=(s // tm,),
        in_specs=[pl.BlockSpec((tm, TOP_K), row), pl.BlockSpec((tm, TOP_K), row),
                  pl.BlockSpec((1, N_EXPERTS), lambda i: (0, 0))],
        out_specs=pl.BlockSpec((tm, TOP_K), row),
        compiler_params=_cparams(("arbitrary",)),
        name="dest",
    )(idx4, rank4, pstart.reshape(1, -1))


def _dispatch_kernel(pstart_ref, pblk_ref, nu_ref, dest_ref, h_ref, xs_ref,
                     zero_scr, sem, zsem):
    tm = h_ref.shape[0] // SUBLANES
    bm = zero_scr.shape[0] // SUBLANES
    n_blk = xs_ref.shape[0] // (bm * SUBLANES)

    rows = _tile_rows

    @pl.when(pl.program_id(0) == 0)
    def _():
        zero_scr[...] = jnp.zeros_like(zero_scr)

        def zero_copy(row0):
            return pltpu.make_async_copy(zero_scr, xs_ref.at[rows(row0, bm)], zsem)

        for phase in range(2):
            for e in range(N_EXPERTS):
                last = pstart_ref[e] + (pblk_ref[e] - 1) * bm
                tail = (n_blk - N_EXPERTS + e) * bm
                for cond, row0 in ((pblk_ref[e] > 0, last), (n_blk - N_EXPERTS + e >= nu_ref[0], tail)):
                    @pl.when(cond)
                    def _():
                        if phase == 0:
                            zero_copy(row0).start()
                        else:
                            zero_copy(row0).wait()

    def row_copy(r, k):
        dst = dest_ref[r * TOP_K + k]
        return pltpu.make_async_copy(h_ref.at[rows(r, 1)], xs_ref.at[rows(dst, 1)], sem)

    def start(r, carry):
        for k in range(TOP_K):
            row_copy(r, k).start(priority=k % 2)
        return carry

    lax.fori_loop(0, tm, start, 0, unroll=4)
    for k in range(TOP_K):
        pltpu.make_async_copy(h_ref, xs_ref.at[rows(0, tm)], sem).wait()


def _dispatch(pstart, pblk, n_used, dest_flat, h2, n_rows):
    s = h2.shape[0] // SUBLANES
    tm = COMBINE_TILE
    grid_spec = pltpu.PrefetchScalarGridSpec(
        num_scalar_prefetch=3,
        grid=(s // tm,),
        in_specs=[pl.BlockSpec((tm * TOP_K,), lambda i, *_: (i,), memory_space=pltpu.SMEM),
                  pl.BlockSpec((tm * SUBLANES, LANES), lambda i, *_: (i, 0))],
        out_specs=pl.BlockSpec(memory_space=pl.ANY),
        scratch_shapes=[pltpu.VMEM((EXPERT_BLOCK * SUBLANES, LANES), F32), pltpu.SemaphoreType.DMA,
                        pltpu.SemaphoreType.DMA],
    )
    return pl.pallas_call(
        _dispatch_kernel,
        out_shape=jax.ShapeDtypeStruct((n_rows * SUBLANES, LANES), F32),
        grid_spec=grid_spec,
        compiler_params=_cparams(("arbitrary",)),
        name="dispatch",
    )(pstart, pblk, n_used, dest_flat, h2)


def _experts_kernel(be_ref, nu_ref, xs_ref, wgu_ref, bgu_ref, wd_ref, bd_ref, y_ref,
                    wgu_bf, wd_bf):
    b = pl.program_id(0)
    d_exp = wd_ref.shape[1]
    prev = be_ref[jnp.maximum(b - 1, 0)]
    changed = (b == 0) | (be_ref[b] != prev)

    @pl.when((b < nu_ref[0]) & changed)
    def _():
        rows = LANES

        def cast_gu(c, carry):
            r0 = pl.multiple_of(c * rows, rows)
            wgu_bf[pl.ds(r0, rows), :] = wgu_ref[0, pl.ds(r0, rows), :].astype(BF16)
            return carry

        def cast_d(c, carry):
            r0 = pl.multiple_of(c * rows, rows)
            wd_bf[pl.ds(r0, rows), :] = wd_ref[0, pl.ds(r0, rows), :].astype(BF16)
            return carry

        lax.fori_loop(0, wgu_ref.shape[1] // rows, cast_gu, 0)
        lax.fori_loop(0, wd_ref.shape[1] // rows, cast_d, 0)

    @pl.when(b < nu_ref[0])
    def _():
        bm = xs_ref.shape[0] // SUBLANES
        xb = _load_token_tiles(xs_ref, bm).astype(BF16)
        hdn = jnp.dot(xb, wgu_bf[...], preferred_element_type=F32) + bgu_ref[0]
        x_glu = jnp.minimum(hdn[:, :d_exp], SWIGLU_LIMIT)
        x_lin = jnp.clip(hdn[:, d_exp:], -SWIGLU_LIMIT, SWIGLU_LIMIT)
        act = x_glu * jax.nn.sigmoid(SWIGLU_ALPHA * x_glu) * (x_lin + 1.0)
        _store_token_tiles(y_ref, jnp.dot(act.astype(BF16), wd_bf[...], preferred_element_type=F32)
                           + bd_ref[0])

    @pl.when(b >= nu_ref[0])
    def _():
        y_ref[...] = jnp.zeros_like(y_ref)


def _experts(block_e, n_used, xs, w_gate_up, b_gate_up, w_down, b_down):
    n_rows = xs.shape[0] // SUBLANES
    bm = EXPERT_BLOCK
    n_blk = n_rows // bm
    ne, d, two_de = w_gate_up.shape
    de = w_down.shape[1]
    assert d == SUBLANES * LANES

    def blk(b, be, nu):
        return jnp.minimum(b, nu[0] - 1)

    grid_spec = pltpu.PrefetchScalarGridSpec(
        num_scalar_prefetch=2,
        grid=(n_blk,),
        in_specs=[pl.BlockSpec((bm * SUBLANES, LANES), lambda b, be, nu: (blk(b, be, nu), 0)),
                  pl.BlockSpec((1, d, two_de), lambda b, be, nu: (be[blk(b, be, nu)], 0, 0)),
                  pl.BlockSpec((1, 1, two_de), lambda b, be, nu: (be[blk(b, be, nu)], 0, 0)),
                  pl.BlockSpec((1, de, d), lambda b, be, nu: (be[blk(b, be, nu)], 0, 0)),
                  pl.BlockSpec((1, 1, d), lambda b, be, nu: (be[blk(b, be, nu)], 0, 0))],
        out_specs=pl.BlockSpec((bm * SUBLANES, LANES), lambda b, be, nu: (b, 0)),
        scratch_shapes=[pltpu.VMEM((d, two_de), BF16), pltpu.VMEM((de, d), BF16)],
    )
    return pl.pallas_call(
        _experts_kernel,
        out_shape=jax.ShapeDtypeStruct((n_rows * SUBLANES, LANES), F32),
        grid_spec=grid_spec,
        compiler_params=_cparams(("arbitrary",)),
        name="experts",
    )(block_e, n_used, xs, w_gate_up, b_gate_up.reshape(ne, 1, two_de), w_down, b_down.reshape(ne, 1, d))


def _combine_kernel(dest_ref, y_ref, gate_ref, x1_ref, gt_ref, gpost_ref, o_ref, buf, sem):
    tm = x1_ref.shape[0]

    def row_copy(r, k):
        src = dest_ref[r * TOP_K + k]
        return pltpu.make_async_copy(y_ref.at[_tile_rows(src, 1)], buf.at[k, _tile_rows(r, 1)], sem)

    def start(r, carry):
        for k in range(TOP_K):
            row_copy(r, k).start(priority=k % 2)
        return carry

    lax.fori_loop(0, tm, start, 0, unroll=4)
    for k in range(TOP_K):
        pltpu.make_async_copy(y_ref.at[_tile_rows(0, tm)], buf.at[k], sem).wait()

    gate = gate_ref[...]
    acc = gate[:, 0:1] * _load_token_tiles(buf.at[0], tm)
    for k in range(1, TOP_K):
        acc = acc + gate[:, k:k + 1] * _load_token_tiles(buf.at[k], tm)
    o_ref[...] = x1_ref[...] + gt_ref[...] * (_rms(acc) * gpost_ref[...])


def _combine(dest_flat, y, gate4, x1, gt, gpost):
    s, d = x1.shape
    tm = COMBINE_TILE
    return pl.pallas_call(
        _combine_kernel,
        out_shape=jax.ShapeDtypeStruct((s, d), F32),
        grid=(s // tm,),
        in_specs=[pl.BlockSpec((tm * TOP_K,), lambda i: (i,), memory_space=pltpu.SMEM),
                  pl.BlockSpec(memory_space=pl.ANY),
                  pl.BlockSpec((tm, TOP_K), lambda i: (i, 0)),
                  pl.BlockSpec((tm, d), lambda i: (i, 0)),
                  pl.BlockSpec((1, d), lambda i: (0, 0)),
                  pl.BlockSpec((1, d), lambda i: (0, 0))],
        out_specs=pl.BlockSpec((tm, d), lambda i: (i, 0)),
        scratch_shapes=[pltpu.VMEM((TOP_K, tm * SUBLANES, LANES), F32), pltpu.SemaphoreType.DMA],
        compiler_params=_cparams(("arbitrary",)),
        name="combine",
    )(dest_flat, y, gate4, x1, gt, gpost)


def _layer(x2, mod, g_pre_mix, g_post_mix, w_in, b_forget, rel_bias, w_out,
           g_pre_ffn, g_post_ffn, w_router, b_router, w_gate_up, b_gate_up, w_down, b_down):
    s, d = x2.shape
    sh_m, sc_m, gt_m, sh_f, sc_f, gt_f = [mod[:, k * d:(k + 1) * d] for k in range(6)]
    n_qkv = 3 * (N_HEADS_FOX + N_HEADS_MOBA) * HEAD_DIM
    fox_w = N_HEADS_FOX * HEAD_DIM

    w_qkv = w_in[:, :n_qkv].astype(BF16)
    w_f = w_in[:, n_qkv:].T
    qkv, cum, sel, nrm, tr, nr = _inproj(x2, g_pre_mix.reshape(1, d), sc_m, sh_m, w_qkv, w_f,
                                         b_forget.reshape(-1, 1))
    y_a = _fox(qkv, tr, nr, cum, nrm)
    y_b = _moba(qkv, tr, sel, rel_bias)

    w_out_bf = w_out.astype(BF16)
    x1, h2, idx4, rank4, gate4, cnt = _outproj(
        y_a, y_b, x2, w_out_bf[:fox_w], w_out_bf[fox_w:], g_post_mix.reshape(1, d), gt_m,
        g_pre_ffn.reshape(1, d), sc_f, sh_f, w_router, b_router.reshape(1, -1))

    bm = EXPERT_BLOCK
    counts = cnt[0].astype(I32)
    pblk = (counts + bm - 1) // bm
    pend_blk = jnp.cumsum(pblk)
    pstart = ((pend_blk - pblk) * bm).astype(I32)
    n_rows = s * TOP_K + N_EXPERTS * bm
    n_blk = n_rows // bm
    block_e = jnp.minimum(jnp.sum(pend_blk[None, :] <= jnp.arange(n_blk)[:, None], axis=1),
                          N_EXPERTS - 1).astype(I32)
    n_used = pend_blk[-1:].astype(I32)

    dest_flat = _dest(idx4, rank4, pstart).reshape(-1)
    xs = _dispatch(pstart, pblk.astype(I32), n_used, dest_flat, h2, n_rows)
    y = _experts(block_e, n_used, xs, w_gate_up, b_gate_up, w_down, b_down)
    return _combine(dest_flat, y, gate4, x1, gt_f, g_post_ffn.reshape(1, d))


def kernel(x, c, w_ada, b_ada, g_pre_mix, g_post_mix, w_in, b_forget, rel_bias, w_out, g_pre_ffn, g_post_ffn, w_router, b_router, w_gate_up, b_gate_up, w_down, b_down):
    bsz, s, d = x.shape
    depth = w_ada.shape[0]
    outs = []
    for bi in range(bsz):
        x2 = x[bi]
        for l in range(depth):
            mod = _adaln(c[bi:bi + 1], w_ada[l], b_ada[l])
            x2 = _layer(x2, mod, g_pre_mix[l], g_post_mix[l], w_in[l], b_forget[l], rel_bias, w_out[l],
                        g_pre_ffn[l], g_post_ffn[l], w_router[l], b_router[l], w_gate_up[l], b_gate_up[l],
                        w_down[l], b_down[l])
        outs.append(x2)
    return outs[0].reshape(1, s, d) if bsz == 1 else jnp.stack(outs)
```

```python
import math

import numpy as np
import jax
import jax.numpy as jnp
from jax import lax
from jax.experimental import pallas as pl
from jax.experimental.pallas import tpu as pltpu

F32 = jnp.float32
BF16 = jnp.bfloat16
I32 = jnp.int32

HEAD_DIM = 64
N_HEADS_FOX = 8
N_HEADS_MOBA = 8
PAIR = 2 * HEAD_DIM
MOBA_BLOCK = 256
MOBA_TOPK = 3
NUM_BUCKETS = 32
MAX_DISTANCE = 128
N_EXPERTS = 32
TOP_K = 4
SWIGLU_LIMIT = 7.0
SWIGLU_ALPHA = 1.702
RMS_EPS = 1e-6
NEG = -(2.0 ** 100)
M_INIT = -(2.0 ** 99)
LOG2E = math.log2(math.e)
SUM_ROWS = 16
SUBLANES = 8
LANES = 128
EXP_UNDERFLOW = 90.0
VMEM_LIMIT = 56 * 1024 * 1024

ROW_TILE = 512
FOX_TILE = 256
EXPERT_BLOCK = 512
COMBINE_TILE = 1024
MOBA_LAG = 1
MOBA_UNROLL = 4

NT_DIMS = (((1,), (1,)), ((), ()))


def _cparams(sem):
    return pltpu.CompilerParams(dimension_semantics=sem, vmem_limit_bytes=VMEM_LIMIT)


def _rms(x):
    return x * lax.rsqrt(jnp.mean(x * x, axis=-1, keepdims=True) + RMS_EPS)


def _adaln_kernel(c_ref, w_ref, b_ref, o_ref):
    c = c_ref[...]
    cond = c * jax.nn.sigmoid(c)
    o_ref[...] = jnp.dot(cond, w_ref[...], preferred_element_type=F32,
                         precision=lax.Precision.HIGHEST) + b_ref[...]


def _adaln(c, w_ada, b_ada):
    d = c.shape[-1]
    n = w_ada.shape[-1]
    c8 = jnp.broadcast_to(c.reshape(1, d), (8, d))
    out = pl.pallas_call(
        _adaln_kernel,
        out_shape=jax.ShapeDtypeStruct((8, n), F32),
        grid=(n // d,),
        in_specs=[pl.BlockSpec((8, d), lambda j: (0, 0)),
                  pl.BlockSpec((d, d), lambda j: (0, j)),
                  pl.BlockSpec((1, d), lambda j: (0, j))],
        out_specs=pl.BlockSpec((8, d), lambda j: (0, j)),
        compiler_params=_cparams(("arbitrary",)),
        name="adaln",
    )(c8, w_ada, b_ada.reshape(1, n))
    return out[0:1]


def _inproj_kernel(x_ref, g_ref, sc_ref, sh_ref, w_ref, wf_ref, bf_ref,
                   qkv_ref, cum_ref, sel_ref, nrm_ref, tr_ref, nr_ref, km_scr, carry_scr):
    i = pl.program_id(0)
    tm = x_ref.shape[0]
    nblk = km_scr.shape[0]

    @pl.when(i == 0)
    def _():
        km_scr[...] = jnp.zeros_like(km_scr)
        carry_scr[...] = jnp.zeros_like(carry_scr)

    x = x_ref[...]
    h = _rms(x) * g_ref[...] * (1.0 + sc_ref[...]) + sh_ref[...]
    hb = h.astype(BF16)

    width = N_HEADS_FOX * HEAD_DIM
    hsel = jnp.where(lax.broadcasted_iota(I32, (width, N_HEADS_FOX), 0) // HEAD_DIM
                     == lax.broadcasted_iota(I32, (width, N_HEADS_FOX), 1), 1.0, 0.0)
    kb = None
    tr_slot = {3: 0, 5: 1, 0: 2, 2: 3}
    qbt = None
    for c in range(6):
        pc = jnp.dot(hb, w_ref[:, c * width:(c + 1) * width], preferred_element_type=F32)
        if c in tr_slot:
            n = tr_slot[c]
            pct = pc.T
            if c == 3:
                qbt = pct
            if c == 0 or c == 3:
                pct = pct * (LOG2E * HEAD_DIM ** -0.5)
            tr_ref[n * width:(n + 1) * width, :] = pct.astype(BF16)
        if c == 1 or c == 4:
            qkv_ref[:, (c // 3) * width:(c // 3 + 1) * width] = pc.astype(BF16)
        if c < 2:
            sq = (pc * (HEAD_DIM ** -0.5) if c == 0 else pc).astype(BF16).astype(F32)
            n2 = jnp.dot((sq * sq).astype(BF16), hsel.astype(BF16), preferred_element_type=F32)
            nrm_ref[0, c:c + 1, :] = jnp.max(n2, axis=0, keepdims=True)
        if c == 4:
            kb = pc

    ft = lax.dot_general(wf_ref[...].astype(BF16), hb, NT_DIMS, preferred_element_type=F32)
    z = ft + bf_ref[...]
    logf = -(jnp.maximum(-z, 0.0) + jnp.log1p(jnp.exp(-jnp.abs(z))))
    lane = lax.broadcasted_iota(I32, logf.shape, 1)
    cs = logf
    sh = 1
    while sh < tm:
        cs = cs + jnp.where(lane >= sh, pltpu.roll(cs, sh, axis=1), 0.0)
        sh *= 2
    base = jnp.zeros_like(cs)
    for b in range(1, tm // FOX_TILE):
        base = jnp.where(lane >= b * FOX_TILE, cs[:, b * FOX_TILE - 1:b * FOX_TILE], base)
    nr_t = (-LOG2E * (cs - base)).T
    for pp in range(nr_ref.shape[0]):
        nr_ref[pp] = nr_t[:, 2 * pp:2 * pp + 2]
    cs = cs + carry_scr[...]
    cum_ref[...] = cs
    carry_scr[...] = cs[:, tm - 1:tm]

    nb_tile = tm // MOBA_BLOCK
    for b in range(nb_tile):
        kmean = jnp.sum(kb[b * MOBA_BLOCK:(b + 1) * MOBA_BLOCK], axis=0, keepdims=True) * (1.0 / MOBA_BLOCK)
        km_scr[pl.ds(i * nb_tile + b, 1), :] = kmean

    km = km_scr[...]
    blk = lax.broadcasted_iota(I32, (nblk, tm), 0)
    col = lax.broadcasted_iota(I32, (nblk, tm), 1)
    own = i * nb_tile + col // MOBA_BLOCK
    for hd in range(N_HEADS_MOBA):
        hs = slice(hd * HEAD_DIM, (hd + 1) * HEAD_DIM)
        g = jnp.dot(km[:, hs], qbt[hs, :], preferred_element_type=F32,
                    precision=lax.Precision.HIGHEST)
        g = jnp.where(blk < own, g, -jnp.inf)
        sel = jnp.zeros(g.shape, dtype=jnp.bool_)
        for _ in range(MOBA_TOPK):
            m = jnp.max(g, axis=0, keepdims=True)
            first = jnp.min(jnp.where(g == m, blk, nblk), axis=0, keepdims=True)
            pick = (blk == first) & (m > -jnp.inf)
            sel = sel | pick
            g = jnp.where(pick, -jnp.inf, g)
        sel_ref[hd] = jnp.where(sel, 0.0, NEG)


def _inproj(x2, g, sc, sh, w_qkv, w_f, b_f):
    s, d = x2.shape
    tm = ROW_TILE
    nblk = s // MOBA_BLOCK
    n = w_qkv.shape[1]
    width = N_HEADS_FOX * HEAD_DIM
    n_rows_out = 2 * width
    n_tr = 4 * width
    assert tm % FOX_TILE == 0 and N_HEADS_FOX == N_HEADS_MOBA
    return pl.pallas_call(
        _inproj_kernel,
        out_shape=(jax.ShapeDtypeStruct((s, n_rows_out), BF16),
                   jax.ShapeDtypeStruct((N_HEADS_FOX, s), F32),
                   jax.ShapeDtypeStruct((N_HEADS_MOBA, nblk, s), F32),
                   jax.ShapeDtypeStruct((s // tm, 2, N_HEADS_FOX), F32),
                   jax.ShapeDtypeStruct((n_tr, s), BF16),
                   jax.ShapeDtypeStruct((N_HEADS_FOX // 2, s, 2), F32)),
        grid=(s // tm,),
        in_specs=[pl.BlockSpec((tm, d), lambda i: (i, 0)),
                  pl.BlockSpec((1, d), lambda i: (0, 0)),
                  pl.BlockSpec((1, d), lambda i: (0, 0)),
                  pl.BlockSpec((1, d), lambda i: (0, 0)),
                  pl.BlockSpec((d, n), lambda i: (0, 0)),
                  pl.BlockSpec((N_HEADS_FOX, d), lambda i: (0, 0)),
                  pl.BlockSpec((N_HEADS_FOX, 1), lambda i: (0, 0))],
        out_specs=(pl.BlockSpec((tm, n_rows_out), lambda i: (i, 0)),
                   pl.BlockSpec((N_HEADS_FOX, tm), lambda i: (0, i)),
                   pl.BlockSpec((N_HEADS_MOBA, nblk, tm), lambda i: (0, 0, i)),
                   pl.BlockSpec((1, 2, N_HEADS_FOX), lambda i: (i, 0, 0)),
                   pl.BlockSpec((n_tr, tm), lambda i: (0, i)),
                   pl.BlockSpec((N_HEADS_FOX // 2, tm, 2), lambda i: (0, i, 0))),
        scratch_shapes=[pltpu.VMEM((nblk, N_HEADS_MOBA * HEAD_DIM), F32),
                        pltpu.VMEM((N_HEADS_FOX, 1), F32)],
        compiler_params=_cparams(("arbitrary",)),
        name="inproj",
    )(x2, g, sc, sh, w_qkv, w_f, b_f)


def _fox_kernel(jlo_ref, cp_ref, qt_ref, k_ref, vt_ref, nr_ref, o_ref, m_ref, a_ref):
    p = pl.program_id(0)
    g = pl.program_id(1)
    j_first = jlo_ref[p * pl.num_programs(1) + g]
    tq = qt_ref.shape[1]
    t = tq // 2

    qt = qt_ref[...]
    top = lax.broadcasted_iota(I32, qt.shape, 0) < HEAD_DIM
    zq = jnp.zeros_like(qt)
    qth = (jnp.where(top, qt, zq), jnp.where(top, zq, qt))
    m_ref[...] = jnp.full(m_ref.shape, M_INIT, F32)
    a_ref[...] = jnp.zeros(a_ref.shape, F32)
    er = lax.broadcasted_iota(I32, (SUM_ROWS, 2 * t), 0)
    ec = lax.broadcasted_iota(I32, (SUM_ROWS, 2 * t), 1)
    ones_rows = jnp.where(((er == 0) & (ec < t)) | ((er == 1) & (ec >= t)), 1.0, 0.0).astype(BF16)
    arow = lax.broadcasted_iota(I32, a_ref.shape, 0)
    head0_rows = (arow < HEAD_DIM) | (arow == PAIR)
    vtop = lax.broadcasted_iota(I32, (PAIR, t), 0) < HEAD_DIM
    in_a = lax.broadcasted_iota(I32, (1, tq), 1) < t
    krow = lax.broadcasted_iota(I32, (t, tq), 0)
    qcol = lax.broadcasted_iota(I32, (t, tq), 1)
    future_a = (qcol < t) & (krow > qcol)
    future_b = (qcol >= t) & (krow > qcol - t)

    def scores(j, future):
        k0 = pl.multiple_of(j * t, t)
        kt = k_ref[pl.ds(k0, t), :]
        out = []
        for hd in range(2):
            s = jnp.dot(kt, qth[hd], preferred_element_type=F32) + nr_ref[0, pl.ds(k0, t), hd:hd + 1]
            out.append(s if future is None else jnp.where(future, NEG, s))
        return out

    def shifts(j, a_on, b_on):
        rows = []
        for hd in range(2):
            h = 2 * p + hd
            sa = LOG2E * (cp_ref[h, 2 * g] - cp_ref[h, j])
            sb = LOG2E * (cp_ref[h, 2 * g + 1] - cp_ref[h, j])
            rows.append(jnp.where(in_a, jnp.where(a_on, sa, NEG), jnp.where(b_on, sb, NEG)))
        return rows

    def softmax(ss, shift_rows):
        ps, alphas = [], []
        for hd in range(2):
            m_old = m_ref[hd]
            m_new = jnp.maximum(m_old, jnp.max(ss[hd], axis=0, keepdims=True) + shift_rows[hd])
            alphas.append(jnp.exp2(m_old - m_new))
            m_ref[hd] = m_new
            ps.append(jnp.exp2(ss[hd] - (m_new - shift_rows[hd])).astype(BF16))
        return ps, alphas

    def accumulate(j, ps, alphas):
        k0 = pl.multiple_of(j * t, t)
        vt = vt_ref[:, pl.ds(k0, t)]
        zv = jnp.zeros_like(vt)
        vcat = jnp.concatenate([jnp.where(vtop, vt, zv), jnp.where(vtop, zv, vt)], axis=1)
        vcat = jnp.concatenate([vcat, ones_rows], axis=0)
        pcat = jnp.concatenate(ps, axis=0)
        a_ref[...] = (a_ref[...] * jnp.where(head0_rows, alphas[0], alphas[1])
                      + jnp.dot(vcat, pcat, preferred_element_type=F32))

    def body(j, carry):
        accumulate(j, *softmax(scores(j, None), shifts(j, True, True)))
        return carry

    lax.fori_loop(j_first, 2 * g - 1, body, 0)
    j1 = jnp.maximum(2 * g - 1, 0)
    j2 = 2 * g
    j3 = 2 * g + 1
    ss1 = scores(j1, None)
    ss2 = scores(j2, future_a)
    ss3 = scores(j3, future_b)
    w1 = softmax(ss1, shifts(j1, g >= 1, g >= 1))
    w2 = softmax(ss2, shifts(j2, True, True))
    accumulate(j1, *w1)
    w3 = softmax(ss3, shifts(j3, False, True))
    accumulate(j2, *w2)
    accumulate(j3, *w3)
    out_t =a_ref[0:PAIR, :] / jnp.where(top, a_ref[PAIR:PAIR + 1, :], a_ref[PAIR + 1:PAIR + 2, :])
    o_ref[...] = out_t.T.astype(o_ref.dtype)


def _fox_first_tile(cum, nrm, t):
    cend = cum[:, t - 1::t]
    nt = cend.shape[1]
    cprev = jnp.concatenate([jnp.zeros((cend.shape[0], 1), F32), cend[:, :-1]], axis=1)
    rep = nt // nrm.shape[0]
    qn = jnp.repeat(jnp.sqrt(nrm[:, 0, :]).T, rep, axis=1)
    kn = jnp.repeat(jnp.sqrt(nrm[:, 1, :]).T, rep, axis=1)
    gap = (1.02 * qn[:, :, None] * (kn[:, None, :] + kn[:, :, None])
           + cprev[:, :, None] - cend[:, None, :])
    jj = jnp.arange(nt)[None, None, :]
    ii = jnp.arange(nt)[None, :, None]
    needed = (jj < ii) & jnp.logical_not(gap <= -EXP_UNDERFLOW)
    needed = needed[0::2] | needed[1::2]
    first = jnp.min(jnp.where(needed, jj, ii), axis=2)
    return first.reshape(-1).astype(I32)


def _fox(qkv, tr, nr, cum, nrm):
    s = qkv.shape[0]
    t = FOX_TILE
    npair = N_HEADS_FOX // 2
    tq = 2 * t
    jlo = _fox_first_tile(cum, nrm, t).reshape(npair, s // tq, 2).min(axis=2).reshape(-1)
    cend = cum[:, t - 1::t]
    cprev = jnp.concatenate([jnp.zeros((cend.shape[0], 1), F32), cend[:, :-1]], axis=1)
    q_rows = 2 * (N_HEADS_MOBA // 2)
    grid_spec = pltpu.PrefetchScalarGridSpec(
        num_scalar_prefetch=1,
        grid=(npair, s // tq),
        in_specs=[pl.BlockSpec(memory_space=pltpu.SMEM),
                  pl.BlockSpec((PAIR, tq), lambda p, i, jl: (q_rows + p, i)),
                  pl.BlockSpec((s, PAIR), lambda p, i, jl: (0, p)),
                  pl.BlockSpec((PAIR, s), lambda p, i, jl: (q_rows + npair + p, 0)),
                  pl.BlockSpec((1, s, 2), lambda p, i, jl: (p, 0, 0))],
        out_specs=pl.BlockSpec((tq, PAIR), lambda p, i, jl: (i, p)),
        scratch_shapes=[pltpu.VMEM((2, 1, tq), F32), pltpu.VMEM((PAIR + SUM_ROWS, tq), F32)],
    )
    return pl.pallas_call(
        _fox_kernel,
        out_shape=jax.ShapeDtypeStruct((s, npair * PAIR), BF16),
        grid_spec=grid_spec,
        compiler_params=_cparams(("arbitrary", "arbitrary")),
        name="fox",
    )(jlo, cprev, tr, qkv, tr, nr)


def _t5_bucket_np(dist):
    dist = np.maximum(dist, 0)
    max_exact = NUM_BUCKETS // 2
    d = np.maximum(dist, 1).astype(np.float32)
    large = max_exact + (np.log(d / np.float32(max_exact)) / np.float32(math.log(MAX_DISTANCE / max_exact))
                         * np.float32(NUM_BUCKETS - max_exact)).astype(np.int32)
    large = np.minimum(large, NUM_BUCKETS - 1)
    return np.where(dist < max_exact, dist, large).astype(np.int32)


def _moba_kernel(rb_ref, qt_ref, k_ref, vt_ref, sel_ref, bkt_ref, o_ref,
                 m_ref, a_ref, bias_scr, *bufs):
    s_bufs = bufs[0:MOBA_UNROLL]
    p_bufs = bufs[MOBA_UNROLL:2 * MOBA_UNROLL]
    al_bufs = bufs[2 * MOBA_UNROLL:3 * MOBA_UNROLL]
    p = pl.program_id(0)
    g = pl.program_id(1)
    tq = qt_ref.shape[1]
    t = tq // 2

    @pl.when(g == 0)
    def _():
        r = lax.broadcasted_iota(I32, (t, t), 0)
        c = lax.broadcasted_iota(I32, (t, t), 1)
        zero = jnp.zeros((t, t), F32)
        for hd in range(2):
            h = 2 * p + hd
            far = rb_ref[(NUM_BUCKETS - 1) * N_HEADS_MOBA + h]
            tiles = []
            for w in range(2):
                bkt = bkt_ref[w]
                acc = jnp.zeros(bkt.shape, F32)
                for kk in range(NUM_BUCKETS):
                    acc = acc + jnp.where(bkt == kk, rb_ref[kk * N_HEADS_MOBA + h], 0.0)
                tiles.append((acc - far) * LOG2E)
            prev_t = tiles[0]
            own_t = jnp.where(r <= c, tiles[1], NEG)
            bias_scr[hd, 0] = jnp.concatenate([prev_t, zero], axis=1)
            bias_scr[hd, 1] = jnp.concatenate([own_t, prev_t], axis=1)
            bias_scr[hd, 2] = jnp.concatenate([zero, own_t], axis=1)

    qt = qt_ref[...]
    top = lax.broadcasted_iota(I32, qt.shape, 0) < HEAD_DIM
    zq = jnp.zeros_like(qt)
    qth = (jnp.where(top, qt, zq), jnp.where(top, zq, qt))
    m_ref[...] = jnp.full(m_ref.shape, M_INIT, F32)
    a_ref[...] = jnp.zeros(a_ref.shape, F32)

    er = lax.broadcasted_iota(I32, (SUM_ROWS, 2 * t), 0)
    ec = lax.broadcasted_iota(I32, (SUM_ROWS, 2 * t), 1)
    ones_rows = jnp.where(((er == 0) & (ec < t)) | ((er == 1) & (ec >= t)), 1.0, 0.0).astype(BF16)
    arow = lax.broadcasted_iota(I32, a_ref.shape, 0)
    head0_rows = (arow < HEAD_DIM) | (arow == PAIR)
    vtop = lax.broadcasted_iota(I32, (PAIR, t), 0) < HEAD_DIM
    in_a = lax.broadcasted_iota(I32, (1, tq), 1) < t

    left = lax.rem(2 * g + 3, 4)
    n_far = jnp.maximum(2 * g - 1 - left, 0)

    n_key_tiles = k_ref.shape[0] // t

    def produce(j, s_buf):
        j = jnp.minimum(j, n_key_tiles - 1)
        k0 = pl.multiple_of(j * t, t)
        kt = k_ref[pl.ds(k0, t), :]
        for hd in range(2):
            s_buf[hd] = jnp.dot(kt, qth[hd], preferred_element_type=F32)

    def softmax(s_buf, p_buf, al_buf, selrows, w):
        for hd in range(2):
            s = s_buf[hd]
            if w is not None:
                s = s + bias_scr[hd, w]
            smax = jnp.max(s, axis=0, keepdims=True)
            m_old = m_ref[hd]
            m_new = jnp.maximum(m_old, smax + selrows[hd])
            shift = m_new - selrows[hd]
            al_buf[hd] = jnp.exp2(m_old - m_new)
            m_ref[hd] = m_new
            p_buf[hd * t:(hd + 1) * t, :] = jnp.exp2(s - shift).astype(BF16)

    def far_rows(j):
        return [jnp.where(j < n_far, sel_ref[hd, pl.ds(j, 1), :], NEG) for hd in range(2)]

    def accumulate(j, p_buf, al_buf):
        k0 = pl.multiple_of(j * t, t)
        vt = vt_ref[:, pl.ds(k0, t)]
        zv = jnp.zeros_like(vt)
        vcat = jnp.concatenate([jnp.where(vtop, vt, zv), jnp.where(vtop, zv, vt)], axis=1)
        vcat = jnp.concatenate([vcat, ones_rows], axis=0)
        a_ref[...] = (a_ref[...] * jnp.where(head0_rows, al_buf[0], al_buf[1])
                      + jnp.dot(vcat, p_buf[...], preferred_element_type=F32))

    un = MOBA_UNROLL
    lag = MOBA_LAG
    assert un == 4 and lag == 1
    for n in range(un - lag, un):
        p_bufs[n][...] = jnp.zeros_like(p_bufs[n])
        al_bufs[n][...] = jnp.ones_like(al_bufs[n])
    produce(0, s_bufs[0])

    def body(u, carry):
        j0 = un * u
        for n in range(un):
            accumulate(jnp.maximum(j0 + n - lag, 0), p_bufs[(n - lag) % un], al_bufs[(n - lag) % un])
            softmax(s_bufs[n], p_bufs[n], al_bufs[n], far_rows(j0 + n), None)
            produce(j0 + n + 1, s_bufs[(n + 1) % un])
        return carry

    n_trips = (n_far + un - 1) // un
    lax.fori_loop(0, n_trips, body, 0)

    def near_block(n_left):
        tiles = []
        for e in range(n_left):
            j = 2 * g - 1 - n_left + e
            ok = j >= 0
            jc = jnp.maximum(j, 0)
            tiles.append((jc, [jnp.where(ok, sel_ref[hd, pl.ds(jc, 1), :], NEG) for hd in range(2)], None))
        j1 = jnp.maximum(2 * g - 1, 0)
        tiles.append((j1, [jnp.where(g >= 1, sel_ref[hd, pl.ds(j1, 1), :], NEG) for hd in range(2)], 0))
        tiles.append((2 * g, [jnp.where(in_a, 0.0, sel_ref[hd, pl.ds(2 * g, 1), :]) for hd in range(2)], 1))
        tiles.append((2 * g + 1, [jnp.where(in_a, NEG, 0.0)] * 2, 2))
        prev_tile = jnp.maximum(un * n_trips - 1, 0)
        produce(tiles[0][0], s_bufs[0])
        for w, (j, rows, bias) in enumerate(tiles):
            accumulate(prev_tile, p_bufs[(w - 1) % un], al_bufs[(w - 1) % un])
            softmax(s_bufs[w % un], p_bufs[w % un], al_bufs[w % un], rows, bias)
            if w + 1 < len(tiles):
                produce(tiles[w + 1][0], s_bufs[(w + 1) % un])
            prev_tile = j
        last = (len(tiles) - 1) % un
        accumulate(prev_tile, p_bufs[last], al_bufs[last])

    for n_left in (1, 3):
        @pl.when(left == n_left)
        def _():
            near_block(n_left)

    out_t = a_ref[0:PAIR, :] / jnp.where(top, a_ref[PAIR:PAIR + 1, :], a_ref[PAIR + 1:PAIR + 2, :])
    o_ref[...] = out_t.T.astype(o_ref.dtype)


def _moba(qkv, tr, sel, rel_bias):
    s = qkv.shape[0]
    t = MOBA_BLOCK
    npair = N_HEADS_MOBA // 2
    nblk = s // t
    kcol = N_HEADS_FOX // 2
    a = np.arange(t)[None, :]
    b = np.arange(t)[:, None]
    bkt = jnp.asarray(np.stack([_t5_bucket_np(t + a - b), _t5_bucket_np(a - b)]))
    tq = 2 * t
    grid_spec = pltpu.PrefetchScalarGridSpec(
        num_scalar_prefetch=1,
        grid=(npair, s // tq),
        in_specs=[pl.BlockSpec((PAIR, tq), lambda p, i, rb: (p, i)),
                  pl.BlockSpec((s, PAIR), lambda p, i, rb: (0, kcol + p)),
                  pl.BlockSpec((PAIR, s), lambda p, i, rb: (npair + p, 0)),
                  pl.BlockSpec((2, nblk, tq), lambda p, i, rb: (p, 0, i)),
                  pl.BlockSpec((2, t, t), lambda p, i, rb: (0, 0, 0))],
        out_specs=pl.BlockSpec((tq, PAIR), lambda p, i, rb: (i, p)),
        scratch_shapes=[pltpu.VMEM((2, 1, tq), F32),
                        pltpu.VMEM((PAIR + SUM_ROWS, tq), F32), pltpu.VMEM((2, 3, t, tq), F32)]
        + [pltpu.VMEM((2, t, tq), F32)] * MOBA_UNROLL
        + [pltpu.VMEM((2 * t, tq), BF16)] * MOBA_UNROLL
        + [pltpu.VMEM((2, 1, tq), F32)] * MOBA_UNROLL,
    )
    return pl.pallas_call(
        _moba_kernel,
        out_shape=jax.ShapeDtypeStruct((s, npair * PAIR), BF16),
        grid_spec=grid_spec,
        compiler_params=_cparams(("arbitrary", "arbitrary")),
        name="moba",
    )(rel_bias.reshape(-1), tr, qkv, tr, sel, bkt)


def _store_token_tiles(ref, val):
    n = val.shape[0]
    for c in range(SUBLANES):
        ref[pl.ds(c, n, stride=SUBLANES), :] = val[:, c * LANES:(c + 1) * LANES]


def _load_token_tiles(ref, n):
    return jnp.concatenate([ref[pl.ds(c, n, stride=SUBLANES), :] for c in range(SUBLANES)], axis=1)


def _tile_rows(r0, n):
    start = r0 * SUBLANES
    if not isinstance(start, int):
        start = pl.multiple_of(start, SUBLANES)
    return pl.ds(start, n * SUBLANES)


def _outproj_kernel(ya_ref, yb_ref, x_ref, wa_ref, wb_ref, gpost_ref, gt_ref, gpre_ref,
                    sc_ref, sh_ref, wr_ref, br_ref,
                    x1_ref, h2_ref, route_ref, gate_ref, cnt_ref, carry_scr):
    i = pl.program_id(0)
    tm = x_ref.shape[0]

    @pl.when(i == 0)
    def _():
        carry_scr[...] = jnp.zeros_like(carry_scr)

    y = (jnp.dot(ya_ref[...], wa_ref[...], preferred_element_type=F32)
         + jnp.dot(yb_ref[...], wb_ref[...], preferred_element_type=F32))
    x1 = x_ref[...] + gt_ref[...] * (_rms(y) * gpost_ref[...])
    x1_ref[...] = x1
    h2 = _rms(x1) * gpre_ref[...] * (1.0 + sc_ref[...]) + sh_ref[...]
    _store_token_tiles(h2_ref, h2)

    h_hi = h2.astype(BF16)
    h_lo = (h2 - h_hi.astype(F32)).astype(BF16)
    logits = (jnp.dot(h_hi, wr_ref[0], preferred_element_type=F32)
              + jnp.dot(h_hi, wr_ref[1], preferred_element_type=F32)
              + jnp.dot(h_lo, wr_ref[0], preferred_element_type=F32)) + br_ref[...]
    ne = logits.shape[1]
    lane = lax.broadcasted_iota(I32, logits.shape, 1)
    lane4 = lax.broadcasted_iota(I32, (tm, TOP_K), 1)
    g = logits
    mask = jnp.zeros(logits.shape, F32)
    vals, picks = [], []
    for _ in range(TOP_K):
        m = jnp.max(g, axis=1, keepdims=True)
        first = jnp.min(jnp.where(g == m, lane, ne), axis=1, keepdims=True)
        pick = lane == first
        mask = jnp.where(pick, 1.0, mask)
        g = jnp.where(pick, -jnp.inf, g)
        vals.append(m)
        picks.append(pick)
    ex = [jnp.exp(v - vals[0]) for v in vals]
    den = ex[0] + ex[1] + ex[2] + ex[3]
    gates = [e / den for e in ex]

    r = lax.broadcasted_iota(I32, (tm, tm), 0)
    c = lax.broadcasted_iota(I32, (tm, tm), 1)
    tril = jnp.where(c < r, 1.0, 0.0).astype(BF16)
    before = jnp.dot(tril, mask.astype(BF16), preferred_element_type=F32) + carry_scr[...]
    total = carry_scr[...] + jnp.sum(mask, axis=0, keepdims=True)
    carry_scr[...] = total
    cnt_ref[...] = jnp.broadcast_to(total, cnt_ref.shape)

    def pack4(cols):
        return jnp.where(lane4 == 0, cols[0],
                         jnp.where(lane4 == 1, cols[1], jnp.where(lane4 == 2, cols[2], cols[3])))

    wide = lax.broadcasted_iota(I32, (tm, LANES), 1)
    route = jnp.zeros((tm, LANES), F32)
    for k, pk in enumerate(picks):
        idx_k = jnp.sum(jnp.where(pk, lane, 0), axis=1, keepdims=True).astype(F32)
        rank_k = jnp.sum(jnp.where(pk, before, 0.0), axis=1, keepdims=True)
        route = jnp.where(wide == k, idx_k, jnp.where(wide == TOP_K + k, rank_k, route))
    route_ref[...] = route.T[0:2 * TOP_K, :].astype(I32)
    gate_ref[...] = pack4(gates)


def _outproj(mix_a, mix_b, x2, w_a, w_b, gpost, gt, gpre, sc, sh, w_router, b_router):
    s, d = x2.shape
    tm = ROW_TILE
    ne = w_router.shape[1]
    wa = mix_a.shape[1]
    w_hi = w_router.astype(BF16)
    w_router_hl = jnp.stack([w_hi, (w_router - w_hi.astype(F32)).astype(BF16)])
    row = lambda i: (i, 0)
    fix = lambda i: (0, 0)
    vec = pl.BlockSpec((1, d), fix)
    return pl.pallas_call(
        _outproj_kernel,
        out_shape=(jax.ShapeDtypeStruct((s, d), F32),
                   jax.ShapeDtypeStruct((s * SUBLANES, LANES), F32),
                   jax.ShapeDtypeStruct((2 * TOP_K, s), I32),
                   jax.ShapeDtypeStruct((s, TOP_K), F32),
                   jax.ShapeDtypeStruct((8, ne), F32)),
        grid=(s // tm,),
        in_specs=[pl.BlockSpec((tm, wa), row), pl.BlockSpec((tm, wa), row), pl.BlockSpec((tm, d), row),
                  pl.BlockSpec((wa, d), fix), pl.BlockSpec((wa, d), fix),
                  vec, vec, vec, vec, vec,
                  pl.BlockSpec((2, d, ne), lambda i: (0, 0, 0)), pl.BlockSpec((1, ne), fix)],
        out_specs=(pl.BlockSpec((tm, d), row), pl.BlockSpec((tm * SUBLANES, LANES), row),
                   pl.BlockSpec((2 * TOP_K, tm), lambda i: (0, i)),
                   pl.BlockSpec((tm, TOP_K), row), pl.BlockSpec((8, ne), fix)),
        scratch_shapes=[pltpu.VMEM((1, ne), F32)],
        compiler_params=_cparams(("arbitrary",)),
        name="outproj",
    )(mix_a, mix_b, x2, w_a, w_b, gpost, gt, gpre, sc, sh, w_router_hl, b_router)


def _dest_kernel(route_ref, pstart_ref, o_ref):
    route = route_ref[...]
    tm = route.shape[1]
    ne = pstart_ref.shape[0]
    expert = lax.broadcasted_iota(I32, (ne, tm), 0)
    for k in range(TOP_K):
        start = jnp.sum(jnp.where(expert == route[k:k + 1, :], pstart_ref[...], 0), axis=0, keepdims=True)
        o_ref[k:k + 1, :] = start + route[TOP_K + k:TOP_K + k + 1, :]


def _dest(route, pstart):
    s = route.shape[1]
    tm = min(4 * ROW_TILE, s)
    return pl.pallas_call(
        _dest_kernel,
        out_shape=jax.ShapeDtypeStruct((TOP_K, s), I32),
        grid=(s // tm,),
        in_specs=[pl.BlockSpec((2 * TOP_K, tm), lambda i: (0, i)),
                  pl.BlockSpec((N_EXPERTS, 1), lambda i: (0, 0))],
        out_specs=pl.BlockSpec((TOP_K, tm), lambda i: (0, i)),
        compiler_params=_cparams(("arbitrary",)),
        name="dest",
    )(route, pstart.reshape(-1, 1))


def _dispatch_kernel(pstart_ref, pblk_ref, nu_ref, dest_ref, h_ref, xs_ref,
                     zero_scr, sem, zsem):
    tm = h_ref.shape[0] // SUBLANES
    bm = zero_scr.shape[0] // SUBLANES
    n_blk = xs_ref.shape[0] // (bm * SUBLANES)

    rows = _tile_rows

    @pl.when(pl.program_id(0) == 0)
    def _():
        zero_scr[...] = jnp.zeros_like(zero_scr)

        def zero_copy(row0):
            return pltpu.make_async_copy(zero_scr, xs_ref.at[rows(row0, bm)], zsem)

        for phase in range(2):
            for e in range(N_EXPERTS):
                last = pstart_ref[e] + (pblk_ref[e] - 1) * bm
                tail = (n_blk - N_EXPERTS + e) * bm
                for cond, row0 in ((pblk_ref[e] > 0, last), (n_blk - N_EXPERTS + e >= nu_ref[0], tail)):
                    @pl.when(cond)
                    def _():
                        if phase == 0:
                            zero_copy(row0).start()
                        else:
                            zero_copy(row0).wait()

    def row_copy(r, k):
        dst = dest_ref[k, r]
        return pltpu.make_async_copy(h_ref.at[rows(r, 1)], xs_ref.at[rows(dst, 1)], sem)

    def start(r, carry):
        for k in range(TOP_K):
            row_copy(r, k).start(priority=k % 2)
        return carry

    lax.fori_loop(0, tm, start, 0, unroll=4)
    for k in range(TOP_K):
        pltpu.make_async_copy(h_ref, xs_ref.at[rows(0, tm)], sem).wait()


def _dispatch(pstart, pblk, n_used, dest, h2, n_rows):
    s = h2.shape[0] // SUBLANES
    tm = COMBINE_TILE
    grid_spec = pltpu.PrefetchScalarGridSpec(
        num_scalar_prefetch=3,
        grid=(s // tm,),
        in_specs=[pl.BlockSpec((TOP_K, tm), lambda i, *_: (0, i), memory_space=pltpu.SMEM),
                  pl.BlockSpec((tm * SUBLANES, LANES), lambda i, *_: (i, 0))],
        out_specs=pl.BlockSpec(memory_space=pl.ANY),
        scratch_shapes=[pltpu.VMEM((EXPERT_BLOCK * SUBLANES, LANES), F32), pltpu.SemaphoreType.DMA,
                        pltpu.SemaphoreType.DMA],
    )
    return pl.pallas_call(
        _dispatch_kernel,
        out_shape=jax.ShapeDtypeStruct((n_rows * SUBLANES, LANES), F32),
        grid_spec=grid_spec,
        compiler_params=_cparams(("arbitrary",)),
        name="dispatch",
    )(pstart, pblk, n_used, dest, h2)


def _experts_kernel(be_ref, nu_ref, xs_ref, wgu_ref, bgu_ref, wd_ref, bd_ref, y_ref,
                    wgu_bf, wd_bf):
    b = pl.program_id(0)
    d_exp = wd_ref.shape[1]
    prev = be_ref[jnp.maximum(b - 1, 0)]
    changed = (b == 0) | (be_ref[b] != prev)

    @pl.when((b < nu_ref[0]) & changed)
    def _():
        rows = LANES

        def cast_gu(c, carry):
            r0 = pl.multiple_of(c * rows, rows)
            wgu_bf[pl.ds(r0, rows), :] = wgu_ref[0, pl.ds(r0, rows), :].astype(BF16)
            return carry

        def cast_d(c, carry):
            r0 = pl.multiple_of(c * rows, rows)
            wd_bf[pl.ds(r0, rows), :] = wd_ref[0, pl.ds(r0, rows), :].astype(BF16)
            return carry

        lax.fori_loop(0, wgu_ref.shape[1] // rows, cast_gu, 0)
        lax.fori_loop(0, wd_ref.shape[1] // rows, cast_d, 0)

    @pl.when(b < nu_ref[0])
    def _():
        bm = xs_ref.shape[0] // SUBLANES
        xb = _load_token_tiles(xs_ref, bm).astype(BF16)
        hdn = jnp.dot(xb, wgu_bf[...], preferred_element_type=F32) + bgu_ref[0]
        x_glu = jnp.minimum(hdn[:, :d_exp], SWIGLU_LIMIT)
        x_lin = jnp.clip(hdn[:, d_exp:], -SWIGLU_LIMIT, SWIGLU_LIMIT)
        act = x_glu * jax.nn.sigmoid(SWIGLU_ALPHA * x_glu) * (x_lin + 1.0)
        _store_token_tiles(y_ref, jnp.dot(act.astype(BF16), wd_bf[...], preferred_element_type=F32)
                           + bd_ref[0])

    @pl.when(b >= nu_ref[0])
    def _():
        y_ref[...] = jnp.zeros_like(y_ref)


def _experts(block_e, n_used, xs, w_gate_up, b_gate_up, w_down, b_down):
    n_rows = xs.shape[0] // SUBLANES
    bm = EXPERT_BLOCK
    n_blk = n_rows // bm
    ne, d, two_de = w_gate_up.shape
    de = w_down.shape[1]
    assert d == SUBLANES * LANES

    def blk(b, be, nu):
        return jnp.minimum(b, nu[0] - 1)

    grid_spec = pltpu.PrefetchScalarGridSpec(
        num_scalar_prefetch=2,
        grid=(n_blk,),
        in_specs=[pl.BlockSpec((bm * SUBLANES, LANES), lambda b, be, nu: (blk(b, be, nu), 0)),
                  pl.BlockSpec((1, d, two_de), lambda b, be, nu: (be[blk(b, be, nu)], 0, 0)),
                  pl.BlockSpec((1, 1, two_de), lambda b, be, nu: (be[blk(b, be, nu)], 0, 0)),
                  pl.BlockSpec((1, de, d), lambda b, be, nu: (be[blk(b, be, nu)], 0, 0)),
                  pl.BlockSpec((1, 1, d), lambda b, be, nu: (be[blk(b, be, nu)], 0, 0))],
        out_specs=pl.BlockSpec((bm * SUBLANES, LANES), lambda b, be, nu: (b, 0)),
        scratch_shapes=[pltpu.VMEM((d, two_de), BF16), pltpu.VMEM((de, d), BF16)],
    )
    return pl.pallas_call(
        _experts_kernel,
        out_shape=jax.ShapeDtypeStruct((n_rows * SUBLANES, LANES), F32),
        grid_spec=grid_spec,
        compiler_params=_cparams(("arbitrary",)),
        name="experts",
    )(block_e, n_used, xs, w_gate_up, b_gate_up.reshape(ne, 1, two_de), w_down, b_down.reshape(ne, 1, d))


def _combine_kernel(dest_ref, y_ref, gate_ref, x1_ref, gt_ref, gpost_ref, o_ref, buf, sem):
    tm = x1_ref.shape[0]

    def row_copy(r, k):
        src = dest_ref[k, r]
        return pltpu.make_async_copy(y_ref.at[_tile_rows(src, 1)], buf.at[k, _tile_rows(r, 1)], sem)

    def start(r, carry):
        for k in range(TOP_K):
            row_copy(r, k).start(priority=k % 2)
        return carry

    lax.fori_loop(0, tm, start, 0, unroll=4)
    for k in range(TOP_K):
        pltpu.make_async_copy(y_ref.at[_tile_rows(0, tm)], buf.at[k], sem).wait()

    gate = gate_ref[...]
    acc = gate[:, 0:1] * _load_token_tiles(buf.at[0], tm)
    for k in range(1, TOP_K):
        acc = acc + gate[:, k:k + 1] * _load_token_tiles(buf.at[k], tm)
    o_ref[...] = x1_ref[...] + gt_ref[...] * (_rms(acc) * gpost_ref[...])


def _combine(dest, y, gate4, x1, gt, gpost):
    s, d = x1.shape
    tm = COMBINE_TILE
    return pl.pallas_call(
        _combine_kernel,
        out_shape=jax.ShapeDtypeStruct((s, d), F32),
        grid=(s // tm,),
        in_specs=[pl.BlockSpec((TOP_K, tm), lambda i: (0, i), memory_space=pltpu.SMEM),
                  pl.BlockSpec(memory_space=pl.ANY),
                  pl.BlockSpec((tm, TOP_K), lambda i: (i, 0)),
                  pl.BlockSpec((tm, d), lambda i: (i, 0)),
                  pl.BlockSpec((1, d), lambda i: (0, 0)),
                  pl.BlockSpec((1, d), lambda i: (0, 0))],
        out_specs=pl.BlockSpec((tm, d), lambda i: (i, 0)),
        scratch_shapes=[pltpu.VMEM((TOP_K, tm * SUBLANES, LANES), F32), pltpu.SemaphoreType.DMA],
        compiler_params=_cparams(("arbitrary",)),
        name="combine",
    )(dest, y, gate4, x1, gt, gpost)


def _layer(x2, mod, g_pre_mix, g_post_mix, w_in, b_forget, rel_bias, w_out,
           g_pre_ffn, g_post_ffn, w_router, b_router, w_gate_up, b_gate_up, w_down, b_down):
    s, d = x2.shape
    sh_m, sc_m, gt_m, sh_f, sc_f, gt_f = [mod[:, k * d:(k + 1) * d] for k in range(6)]
    n_qkv = 3 * (N_HEADS_FOX + N_HEADS_MOBA) * HEAD_DIM
    fox_w = N_HEADS_FOX * HEAD_DIM

    w_qkv = w_in[:, :n_qkv].astype(BF16)
    w_f = w_in[:, n_qkv:].T
    qkv, cum, sel, nrm, tr, nr = _inproj(x2, g_pre_mix.reshape(1, d), sc_m, sh_m, w_qkv, w_f,
                                         b_forget.reshape(-1, 1))
    y_a = _fox(qkv, tr, nr, cum, nrm)
    y_b = _moba(qkv, tr, sel, rel_bias)

    w_out_bf = w_out.astype(BF16)
    x1, h2, route, gate4, cnt = _outproj(
        y_a, y_b, x2, w_out_bf[:fox_w], w_out_bf[fox_w:], g_post_mix.reshape(1, d), gt_m,
        g_pre_ffn.reshape(1, d), sc_f, sh_f, w_router, b_router.reshape(1, -1))

    bm = EXPERT_BLOCK
    counts = cnt[0].astype(I32)
    pblk = (counts + bm - 1) // bm
    pend_blk = jnp.cumsum(pblk)
    pstart = ((pend_blk - pblk) * bm).astype(I32)
    n_rows = s * TOP_K + N_EXPERTS * bm
    n_blk = n_rows // bm
    block_e = jnp.minimum(jnp.sum(pend_blk[None, :] <= jnp.arange(n_blk)[:, None], axis=1),
                          N_EXPERTS - 1).astype(I32)
    n_used = pend_blk[-1:].astype(I32)

    dest = _dest(route, pstart)
    xs = _dispatch(pstart, pblk.astype(I32), n_used, dest, h2, n_rows)
    y = _experts(block_e, n_used, xs, w_gate_up, b_gate_up, w_down, b_down)
    return _combine(dest, y, gate4, x1, gt_f, g_post_ffn.reshape(1, d))


def kernel(x, c, w_ada, b_ada, g_pre_mix, g_post_mix, w_in, b_forget, rel_bias, w_out, g_pre_ffn, g_post_ffn, w_router, b_router, w_gate_up, b_gate_up, w_down, b_down):
    bsz, s, d = x.shape
    depth = w_ada.shape[0]
    outs = []
    for bi in range(bsz):
        x2 = x[bi]
        for l in range(depth):
            mod = _adaln(c[bi:bi + 1], w_ada[l], b_ada[l])
            x2 = _layer(x2, mod, g_pre_mix[l], g_post_mix[l], w_in[l], b_forget[l], rel_bias, w_out[l],
                        g_pre_ffn[l], g_post_ffn[l], w_router[l], b_router[l], w_gate_up[l], b_gate_up[l],
                        w_down[l], b_down[l])
        outs.append(x2)
    return outs[0].reshape(1, s, d) if bsz == 1 else jnp.stack(outs)
```

```python
import math

import numpy as np
import jax
import jax.numpy as jnp
from jax import lax
from jax.experimental import pallas as pl
from jax.experimental.pallas import tpu as pltpu

F32 = jnp.float32
BF16 = jnp.bfloat16
I32 = jnp.int32

HEAD_DIM = 64
N_HEADS_FOX = 8
N_HEADS_MOBA = 8
PAIR = 2 * HEAD_DIM
MOBA_BLOCK = 256
MOBA_TOPK = 3
NUM_BUCKETS = 32
MAX_DISTANCE = 128
N_EXPERTS = 32
TOP_K = 4
SWIGLU_LIMIT = 7.0
SWIGLU_ALPHA = 1.702
RMS_EPS = 1e-6
NEG = -(2.0 ** 100)
M_INIT = -(2.0 ** 99)
LOG2E = math.log2(math.e)
SUM_ROWS = 16
SUBLANES = 8
LANES = 128
EXP_UNDERFLOW = 90.0
VMEM_LIMIT = 56 * 1024 * 1024

ROW_TILE = 512
FOX_TILE = 256
EXPERT_BLOCK = 512
COMBINE_TILE = 1024
MOBA_LAG = 1
MOBA_UNROLL = 4

NT_DIMS = (((1,), (1,)), ((), ()))


def _cparams(sem):
    return pltpu.CompilerParams(dimension_semantics=sem, vmem_limit_bytes=VMEM_LIMIT)


def _rms(x):
    return x * lax.rsqrt(jnp.mean(x * x, axis=-1, keepdims=True) + RMS_EPS)


def _adaln_kernel(c_ref, w_ref, b_ref, o_ref):
    c = c_ref[...]
    cond = c * jax.nn.sigmoid(c)
    o_ref[...] = jnp.dot(cond, w_ref[...], preferred_element_type=F32,
                         precision=lax.Precision.HIGHEST) + b_ref[...]


def _adaln(c, w_ada, b_ada):
    d = c.shape[-1]
    n = w_ada.shape[-1]
    c8 = jnp.broadcast_to(c.reshape(1, d), (8, d))
    out = pl.pallas_call(
        _adaln_kernel,
        out_shape=jax.ShapeDtypeStruct((8, n), F32),
        grid=(n // d,),
        in_specs=[pl.BlockSpec((8, d), lambda j: (0, 0)),
                  pl.BlockSpec((d, d), lambda j: (0, j)),
                  pl.BlockSpec((1, d), lambda j: (0, j))],
        out_specs=pl.BlockSpec((8, d), lambda j: (0, j)),
        compiler_params=_cparams(("arbitrary",)),
        name="adaln",
    )(c8, w_ada, b_ada.reshape(1, n))
    return out[0:1]


def _inproj_kernel(x_ref, g_ref, sc_ref, sh_ref, w_ref, wf_ref, bf_ref,
                   qkv_ref, cum_ref, sel_ref, nrm_ref, tr_ref, nr_ref, km_scr, carry_scr):
    i = pl.program_id(0)
    tm = x_ref.shape[0]
    nblk = km_scr.shape[0]

    @pl.when(i == 0)
    def _():
        km_scr[...] = jnp.zeros_like(km_scr)
        carry_scr[...] = jnp.zeros_like(carry_scr)

    x = x_ref[...]
    h = _rms(x) * g_ref[...] * (1.0 + sc_ref[...]) + sh_ref[...]
    hb = h.astype(BF16)

    width = N_HEADS_FOX * HEAD_DIM
    hsel = jnp.where(lax.broadcasted_iota(I32, (width, N_HEADS_FOX), 0) // HEAD_DIM
                     == lax.broadcasted_iota(I32, (width, N_HEADS_FOX), 1), 1.0, 0.0)
    kb = None
    tr_slot = {3: 0, 5: 1, 0: 2, 2: 3}
    qbt = None
    for c in range(6):
        pc = jnp.dot(hb, w_ref[:, c * width:(c + 1) * width], preferred_element_type=F32)
        if c in tr_slot:
            n = tr_slot[c]
            pct = pc.T
            if c == 3:
                qbt = pct
            if c == 0 or c == 3:
                pct = pct * (LOG2E * HEAD_DIM ** -0.5)
            tr_ref[n * width:(n + 1) * width, :] = pct.astype(BF16)
        if c == 1 or c == 4:
            qkv_ref[:, (c // 3) * width:(c // 3 + 1) * width] = pc.astype(BF16)
        if c < 2:
            sq = (pc * (HEAD_DIM ** -0.5) if c == 0 else pc).astype(BF16).astype(F32)
            n2 = jnp.dot((sq * sq).astype(BF16), hsel.astype(BF16), preferred_element_type=F32)
            nrm_ref[0, c:c + 1, :] = jnp.max(n2, axis=0, keepdims=True)
        if c == 4:
            kb = pc

    ft = lax.dot_general(wf_ref[...].astype(BF16), hb, NT_DIMS, preferred_element_type=F32)
    z = ft + bf_ref[...]
    logf = -(jnp.maximum(-z, 0.0) + jnp.log1p(jnp.exp(-jnp.abs(z))))
    lane = lax.broadcasted_iota(I32, logf.shape, 1)
    cs = logf
    sh = 1
    while sh < tm:
        cs = cs + jnp.where(lane >= sh, pltpu.roll(cs, sh, axis=1), 0.0)
        sh *= 2
    base = jnp.zeros_like(cs)
    for b in range(1, tm // FOX_TILE):
        base = jnp.where(lane >= b * FOX_TILE, cs[:, b * FOX_TILE - 1:b * FOX_TILE], base)
    nr_t = (-LOG2E * (cs - base)).T
    for pp in range(nr_ref.shape[0]):
        nr_ref[pp] = nr_t[:, 2 * pp:2 * pp + 2]
    cs = cs + carry_scr[...]
    cum_ref[...] = cs
    carry_scr[...] = cs[:, tm - 1:tm]

    nb_tile = tm // MOBA_BLOCK
    for b in range(nb_tile):
        kmean = jnp.sum(kb[b * MOBA_BLOCK:(b + 1) * MOBA_BLOCK], axis=0, keepdims=True) * (1.0 / MOBA_BLOCK)
        km_scr[pl.ds(i * nb_tile + b, 1), :] = kmean

    km = km_scr[...]
    blk = lax.broadcasted_iota(I32, (nblk, tm), 0)
    col = lax.broadcasted_iota(I32, (nblk, tm), 1)
    own = i * nb_tile + col // MOBA_BLOCK
    for hd in range(N_HEADS_MOBA):
        hs = slice(hd * HEAD_DIM, (hd + 1) * HEAD_DIM)
        g = jnp.dot(km[:, hs], qbt[hs, :], preferred_element_type=F32,
                    precision=lax.Precision.HIGHEST)
        g = jnp.where(blk < own, g, -jnp.inf)
        sel = jnp.zeros(g.shape, dtype=jnp.bool_)
        for _ in range(MOBA_TOPK):
            m = jnp.max(g, axis=0, keepdims=True)
            first = jnp.min(jnp.where(g == m, blk, nblk), axis=0, keepdims=True)
            pick = (blk == first) & (m > -jnp.inf)
            sel = sel | pick
            g = jnp.where(pick, -jnp.inf, g)
        sel_ref[hd] = jnp.where(sel, 0.0, NEG)


def _inproj(x2, g, sc, sh, w_qkv, w_f, b_f):
    s, d = x2.shape
    tm = ROW_TILE
    nblk = s // MOBA_BLOCK
    n = w_qkv.shape[1]
    width = N_HEADS_FOX * HEAD_DIM
    n_rows_out = 2 * width
    n_tr = 4 * width
    assert tm % FOX_TILE == 0 and N_HEADS_FOX == N_HEADS_MOBA
    return pl.pallas_call(
        _inproj_kernel,
        out_shape=(jax.ShapeDtypeStruct((s, n_rows_out), BF16),
                   jax.ShapeDtypeStruct((N_HEADS_FOX, s), F32),
                   jax.ShapeDtypeStruct((N_HEADS_MOBA, nblk, s), F32),
                   jax.ShapeDtypeStruct((s // tm, 2, N_HEADS_FOX), F32),
                   jax.ShapeDtypeStruct((n_tr, s), BF16),
                   jax.ShapeDtypeStruct((N_HEADS_FOX // 2, s, 2), F32)),
        grid=(s // tm,),
        in_specs=[pl.BlockSpec((tm, d), lambda i: (i, 0)),
                  pl.BlockSpec((1, d), lambda i: (0, 0)),
                  pl.BlockSpec((1, d), lambda i: (0, 0)),
                  pl.BlockSpec((1, d), lambda i: (0, 0)),
                  pl.BlockSpec((d, n), lambda i: (0, 0)),
                  pl.BlockSpec((N_HEADS_FOX, d), lambda i: (0, 0)),
                  pl.BlockSpec((N_HEADS_FOX, 1), lambda i: (0, 0))],
        out_specs=(pl.BlockSpec((tm, n_rows_out), lambda i: (i, 0)),
                   pl.BlockSpec((N_HEADS_FOX, tm), lambda i: (0, i)),
                   pl.BlockSpec((N_HEADS_MOBA, nblk, tm), lambda i: (0, 0, i)),
                   pl.BlockSpec((1, 2, N_HEADS_FOX), lambda i: (i, 0, 0)),
                   pl.BlockSpec((n_tr, tm), lambda i: (0, i)),
                   pl.BlockSpec((N_HEADS_FOX // 2, tm, 2), lambda i: (0, i, 0))),
        scratch_shapes=[pltpu.VMEM((nblk, N_HEADS_MOBA * HEAD_DIM), F32),
                        pltpu.VMEM((N_HEADS_FOX, 1), F32)],
        compiler_params=_cparams(("arbitrary",)),
        name="inproj",
    )(x2, g, sc, sh, w_qkv, w_f, b_f)


def _fox_kernel(jlo_ref, cp_ref, qt_ref, k_ref, vt_ref, nr_ref, o_ref, m_ref, a_ref):
    p = pl.program_id(0)
    g = pl.program_id(1)
    j_first = jlo_ref[p * pl.num_programs(1) + g]
    tq = qt_ref.shape[1]
    t = tq // 2

    qt = qt_ref[...]
    top = lax.broadcasted_iota(I32, qt.shape, 0) < HEAD_DIM
    zq = jnp.zeros_like(qt)
    qth = (jnp.where(top, qt, zq), jnp.where(top, zq, qt))
    m_ref[...] = jnp.full(m_ref.shape, M_INIT, F32)
    a_ref[...] = jnp.zeros(a_ref.shape, F32)
    er = lax.broadcasted_iota(I32, (SUM_ROWS, 2 * t), 0)
    ec = lax.broadcasted_iota(I32, (SUM_ROWS, 2 * t), 1)
    ones_rows = jnp.where(((er == 0) & (ec < t)) | ((er == 1) & (ec >= t)), 1.0, 0.0).astype(BF16)
    arow = lax.broadcasted_iota(I32, a_ref.shape, 0)
    head0_rows = (arow < HEAD_DIM) | (arow == PAIR)
    vtop = lax.broadcasted_iota(I32, (PAIR, t), 0) < HEAD_DIM
    in_a = lax.broadcasted_iota(I32, (1, tq), 1) < t
    krow = lax.broadcasted_iota(I32, (t, tq), 0)
    qcol = lax.broadcasted_iota(I32, (t, tq), 1)
    future_a = (qcol < t) & (krow > qcol)
    future_b = (qcol >= t) & (krow > qcol - t)

    def scores(j, future):
        k0 = pl.multiple_of(j * t, t)
        kt = k_ref[pl.ds(k0, t), :]
        out = []
        for hd in range(2):
            s = jnp.dot(kt, qth[hd], preferred_element_type=F32) + nr_ref[0, pl.ds(k0, t), hd:hd + 1]
            out.append(s if future is None else jnp.where(future, NEG, s))
        return out

    def shifts(j, a_on, b_on):
        rows = []
        for hd in range(2):
            h = 2 * p + hd
            sa = LOG2E * (cp_ref[h, 2 * g] - cp_ref[h, j])
            sb = LOG2E * (cp_ref[h, 2 * g + 1] - cp_ref[h, j])
            rows.append(jnp.where(in_a, jnp.where(a_on, sa, NEG), jnp.where(b_on, sb, NEG)))
        return rows

    def softmax(ss, shift_rows):
        ps, alphas = [], []
        for hd in range(2):
            m_old = m_ref[hd]
            m_new = jnp.maximum(m_old, jnp.max(ss[hd], axis=0, keepdims=True) + shift_rows[hd])
            alphas.append(jnp.exp2(m_old - m_new))
            m_ref[hd] = m_new
            ps.append(jnp.exp2(ss[hd] - (m_new - shift_rows[hd])).astype(BF16))
        return ps, alphas

    def accumulate(j, ps, alphas):
        k0 = pl.multiple_of(j * t, t)
        vt = vt_ref[:, pl.ds(k0, t)]
        zv = jnp.zeros_like(vt)
        vcat = jnp.concatenate([jnp.where(vtop, vt, zv), jnp.where(vtop, zv, vt)], axis=1)
        vcat = jnp.concatenate([vcat, ones_rows], axis=0)
        pcat = jnp.concatenate(ps, axis=0)
        a_ref[...] = (a_ref[...] * jnp.where(head0_rows, alphas[0], alphas[1])
                      + jnp.dot(vcat, pcat, preferred_element_type=F32))

    def body(j, carry):
        accumulate(j, *softmax(scores(j, None), shifts(j, True, True)))
        return carry

    lax.fori_loop(j_first, 2 * g - 1, body, 0)
    j1 = jnp.maximum(2 * g - 1, 0)
    j2 = 2 * g
    j3 = 2 * g + 1
    ss1 = scores(j1, None)
    ss2 = scores(j2, future_a)
    ss3 = scores(j3, future_b)
    w1 = softmax(ss1, shifts(j1, g >= 1, g >= 1))
    w2 = softmax(ss2, shifts(j2, True, True))
    accumulate(j1, *w1)
    w3 = softmax(ss3, shifts(j3, False, True))
    accumulate(j2, *w2)
    accumulate(j3, *w3)
    out_t =a_ref[0:PAIR, :] / jnp.where(top, a_ref[PAIR:PAIR + 1, :], a_ref[PAIR + 1:PAIR + 2, :])
    o_ref[...] = out_t.T.astype(o_ref.dtype)


def _fox_first_tile(cum, nrm, t):
    cend = cum[:, t - 1::t]
    nt = cend.shape[1]
    cprev = jnp.concatenate([jnp.zeros((cend.shape[0], 1), F32), cend[:, :-1]], axis=1)
    rep = nt // nrm.shape[0]
    qn = jnp.repeat(jnp.sqrt(nrm[:, 0, :]).T, rep, axis=1)
    kn = jnp.repeat(jnp.sqrt(nrm[:, 1, :]).T, rep, axis=1)
    gap = (1.02 * qn[:, :, None] * (kn[:, None, :] + kn[:, :, None])
           + cprev[:, :, None] - cend[:, None, :])
    jj = jnp.arange(nt)[None, None, :]
    ii = jnp.arange(nt)[None, :, None]
    needed = (jj < ii) & jnp.logical_not(gap <= -EXP_UNDERFLOW)
    needed = needed[0::2] | needed[1::2]
    first = jnp.min(jnp.where(needed, jj, ii), axis=2)
    return first.reshape(-1).astype(I32)


def _fox(qkv, tr, nr, cum, nrm):
    s = qkv.shape[0]
    t = FOX_TILE
    npair = N_HEADS_FOX // 2
    tq = 2 * t
    jlo = _fox_first_tile(cum, nrm, t).reshape(npair, s // tq, 2).min(axis=2).reshape(-1)
    cend = cum[:, t - 1::t]
    cprev = jnp.concatenate([jnp.zeros((cend.shape[0], 1), F32), cend[:, :-1]], axis=1)
    q_rows = 2 * (N_HEADS_MOBA // 2)
    grid_spec = pltpu.PrefetchScalarGridSpec(
        num_scalar_prefetch=1,
        grid=(npair, s // tq),
        in_specs=[pl.BlockSpec(memory_space=pltpu.SMEM),
                  pl.BlockSpec((PAIR, tq), lambda p, i, jl: (q_rows + p, i)),
                  pl.BlockSpec((s, PAIR), lambda p, i, jl: (0, p)),
                  pl.BlockSpec((PAIR, s), lambda p, i, jl: (q_rows + npair + p, 0)),
                  pl.BlockSpec((1, s, 2), lambda p, i, jl: (p, 0, 0))],
        out_specs=pl.BlockSpec((tq, PAIR), lambda p, i, jl: (i, p)),
        scratch_shapes=[pltpu.VMEM((2, 1, tq), F32), pltpu.VMEM((PAIR + SUM_ROWS, tq), F32)],
    )
    return pl.pallas_call(
        _fox_kernel,
        out_shape=jax.ShapeDtypeStruct((s, npair * PAIR), BF16),
        grid_spec=grid_spec,
        compiler_params=_cparams(("arbitrary", "arbitrary")),
        name="fox",
    )(jlo, cprev, tr, qkv, tr, nr)


def _t5_bucket_np(dist):
    dist = np.maximum(dist, 0)
    max_exact = NUM_BUCKETS // 2
    d = np.maximum(dist, 1).astype(np.float32)
    large = max_exact + (np.log(d / np.float32(max_exact)) / np.float32(math.log(MAX_DISTANCE / max_exact))
                         * np.float32(NUM_BUCKETS - max_exact)).astype(np.int32)
    large = np.minimum(large, NUM_BUCKETS - 1)
    return np.where(dist < max_exact, dist, large).astype(np.int32)


def _moba_kernel(rb_ref, qt_ref, k_ref, vt_ref, sel_ref, bkt_ref, o_ref,
                 m_ref, a_ref, bias_scr, *bufs):
    s_bufs = bufs[0:MOBA_UNROLL]
    p_bufs = bufs[MOBA_UNROLL:2 * MOBA_UNROLL]
    al_bufs = bufs[2 * MOBA_UNROLL:3 * MOBA_UNROLL]
    p = pl.program_id(0)
    g = pl.program_id(1)
    tq = qt_ref.shape[1]
    t = tq // 2

    @pl.when(g == 0)
    def _():
        r = lax.broadcasted_iota(I32, (t, t), 0)
        c = lax.broadcasted_iota(I32, (t, t), 1)
        zero = jnp.zeros((t, t), F32)
        for hd in range(2):
            h = 2 * p + hd
            far = rb_ref[(NUM_BUCKETS - 1) * N_HEADS_MOBA + h]
            tiles = []
            for w in range(2):
                bkt = bkt_ref[w]
                acc = jnp.zeros(bkt.shape, F32)
                for kk in range(NUM_BUCKETS):
                    acc = acc + jnp.where(bkt == kk, rb_ref[kk * N_HEADS_MOBA + h], 0.0)
                tiles.append((acc - far) * LOG2E)
            prev_t = tiles[0]
            own_t = jnp.where(r <= c, tiles[1], NEG)
            bias_scr[hd, 0] = jnp.concatenate([zero, zero], axis=1)
            bias_scr[hd, 1] = jnp.concatenate([prev_t, zero], axis=1)
            bias_scr[hd, 2] = jnp.concatenate([own_t, prev_t], axis=1)
            bias_scr[hd, 3] = jnp.concatenate([zero, own_t], axis=1)

    qt = qt_ref[...]
    top = lax.broadcasted_iota(I32, qt.shape, 0) < HEAD_DIM
    zq = jnp.zeros_like(qt)
    qth = (jnp.where(top, qt, zq), jnp.where(top, zq, qt))
    m_ref[...] = jnp.full(m_ref.shape, M_INIT, F32)
    a_ref[...] = jnp.zeros(a_ref.shape, F32)

    er = lax.broadcasted_iota(I32, (SUM_ROWS, 2 * t), 0)
    ec = lax.broadcasted_iota(I32, (SUM_ROWS, 2 * t), 1)
    ones_rows = jnp.where(((er == 0) & (ec < t)) | ((er == 1) & (ec >= t)), 1.0, 0.0).astype(BF16)
    arow = lax.broadcasted_iota(I32, a_ref.shape, 0)
    head0_rows = (arow < HEAD_DIM) | (arow == PAIR)
    vtop = lax.broadcasted_iota(I32, (PAIR, t), 0) < HEAD_DIM
    in_a = lax.broadcasted_iota(I32, (1, tq), 1) < t

    n_tiles = 2 * g + 2
    n_key_tiles = k_ref.shape[0] // t

    def produce(j, s_buf):
        j = jnp.minimum(j, n_key_tiles - 1)
        k0 = pl.multiple_of(j * t, t)
        kt = k_ref[pl.ds(k0, t), :]
        for hd in range(2):
            s_buf[hd] = jnp.dot(kt, qth[hd], preferred_element_type=F32)

    def softmax(j, s_buf, p_buf, al_buf):
        d_a = 2 * g - j
        d_b = d_a + 1
        bias_idx = jnp.clip(2 - d_a, 0, 3)
        jc = jnp.minimum(j, n_key_tiles - 1)
        for hd in range(2):
            sel = sel_ref[hd, pl.ds(jc, 1), :]
            row_a = jnp.where(d_a >= 1, sel, jnp.where(d_a == 0, 0.0, NEG))
            row_b = jnp.where(d_b >= 1, sel, jnp.where(d_b == 0, 0.0, NEG))
            row = jnp.where(in_a, row_a, row_b)
            s = s_buf[hd] + bias_scr[hd, bias_idx]
            smax = jnp.max(s, axis=0, keepdims=True)
            m_old = m_ref[hd]
            m_new = jnp.maximum(m_old, smax + row)
            shift = m_new - row
            al_buf[hd] = jnp.exp2(m_old - m_new)
            m_ref[hd] = m_new
            p_buf[hd * t:(hd + 1) * t, :] = jnp.exp2(s - shift).astype(BF16)

    def accumulate(j, p_buf, al_buf):
        j = jnp.clip(j, 0, n_key_tiles - 1)
        k0 = pl.multiple_of(j * t, t)
        vt = vt_ref[:, pl.ds(k0, t)]
        zv = jnp.zeros_like(vt)
        vcat = jnp.concatenate([jnp.where(vtop, vt, zv), jnp.where(vtop, zv, vt)], axis=1)
        vcat = jnp.concatenate([vcat, ones_rows], axis=0)
        a_ref[...] = (a_ref[...] * jnp.where(head0_rows, al_buf[0], al_buf[1])
                      + jnp.dot(vcat, p_buf[...], preferred_element_type=F32))

    un = MOBA_UNROLL
    lag = MOBA_LAG
    for n in range(un - lag, un):
        p_bufs[n][...] = jnp.zeros_like(p_bufs[n])
        al_bufs[n][...] = jnp.ones_like(al_bufs[n])
    produce(0, s_bufs[0])

    def body(u, carry):
        j0 = un * u
        for n in range(un):
            accumulate(j0 + n - lag, p_bufs[(n - lag) % un], al_bufs[(n - lag) % un])
            softmax(j0 + n, s_bufs[n], p_bufs[n], al_bufs[n])
            produce(j0 + n + 1, s_bufs[(n + 1) % un])
        return carry

    n_trips = (n_tiles + un - 1) // un
    lax.fori_loop(0, n_trips, body, 0)
    for n in range(un - lag, un):
        accumulate(un * n_trips - un + n, p_bufs[n], al_bufs[n])
    out_t = a_ref[0:PAIR, :] / jnp.where(top, a_ref[PAIR:PAIR + 1, :], a_ref[PAIR + 1:PAIR + 2, :])
    o_ref[...] = out_t.T.astype(o_ref.dtype)


def _moba(qkv, tr, sel, rel_bias):
    s = qkv.shape[0]
    t = MOBA_BLOCK
    npair = N_HEADS_MOBA // 2
    nblk = s // t
    kcol = N_HEADS_FOX // 2
    a = np.arange(t)[None, :]
    b = np.arange(t)[:, None]
    bkt = jnp.asarray(np.stack([_t5_bucket_np(t + a - b), _t5_bucket_np(a - b)]))
    tq = 2 * t
    grid_spec = pltpu.PrefetchScalarGridSpec(
        num_scalar_prefetch=1,
        grid=(npair, s // tq),
        in_specs=[pl.BlockSpec((PAIR, tq), lambda p, i, rb: (p, i)),
                  pl.BlockSpec((s, PAIR), lambda p, i, rb: (0, kcol + p)),
                  pl.BlockSpec((PAIR, s), lambda p, i, rb: (npair + p, 0)),
                  pl.BlockSpec((2, nblk, tq), lambda p, i, rb: (p, 0, i)),
                  pl.BlockSpec((2, t, t), lambda p, i, rb: (0, 0, 0))],
        out_specs=pl.BlockSpec((tq, PAIR), lambda p, i, rb: (i, p)),
        scratch_shapes=[pltpu.VMEM((2, 1, tq), F32),
                        pltpu.VMEM((PAIR + SUM_ROWS, tq), F32), pltpu.VMEM((2, 4, t, tq), F32)]
        + [pltpu.VMEM((2, t, tq), F32)] * MOBA_UNROLL
        + [pltpu.VMEM((2 * t, tq), BF16)] * MOBA_UNROLL
        + [pltpu.VMEM((2, 1, tq), F32)] * MOBA_UNROLL,
    )
    return pl.pallas_call(
        _moba_kernel,
        out_shape=jax.ShapeDtypeStruct((s, npair * PAIR), BF16),
        grid_spec=grid_spec,
        compiler_params=_cparams(("arbitrary", "arbitrary")),
        name="moba",
    )(rel_bias.reshape(-1), tr, qkv, tr, sel, bkt)


def _store_token_tiles(ref, val):
    n = val.shape[0]
    for c in range(SUBLANES):
        ref[pl.ds(c, n, stride=SUBLANES), :] = val[:, c * LANES:(c + 1) * LANES]


def _load_token_tiles(ref, n):
    return jnp.concatenate([ref[pl.ds(c, n, stride=SUBLANES), :] for c in range(SUBLANES)], axis=1)


def _tile_rows(r0, n):
    start = r0 * SUBLANES
    if not isinstance(start, int):
        start = pl.multiple_of(start, SUBLANES)
    return pl.ds(start, n * SUBLANES)


def _outproj_kernel(ya_ref, yb_ref, x_ref, wa_ref, wb_ref, gpost_ref, gt_ref, gpre_ref,
                    sc_ref, sh_ref, wr_ref, br_ref,
                    x1_ref, h2_ref, route_ref, gate_ref, cnt_ref, carry_scr):
    i = pl.program_id(0)
    tm = x_ref.shape[0]

    @pl.when(i == 0)
    def _():
        carry_scr[...] = jnp.zeros_like(carry_scr)

    y = (jnp.dot(ya_ref[...], wa_ref[...], preferred_element_type=F32)
         + jnp.dot(yb_ref[...], wb_ref[...], preferred_element_type=F32))
    x1 = x_ref[...] + gt_ref[...] * (_rms(y) * gpost_ref[...])
    x1_ref[...] = x1
    h2 = _rms(x1) * gpre_ref[...] * (1.0 + sc_ref[...]) + sh_ref[...]
    _store_token_tiles(h2_ref, h2)

    h_hi = h2.astype(BF16)
    h_lo = (h2 - h_hi.astype(F32)).astype(BF16)
    logits = (jnp.dot(h_hi, wr_ref[0], preferred_element_type=F32)
              + jnp.dot(h_hi, wr_ref[1], preferred_element_type=F32)
              + jnp.dot(h_lo, wr_ref[0], preferred_element_type=F32)) + br_ref[...]
    ne = logits.shape[1]
    lane = lax.broadcasted_iota(I32, logits.shape, 1)
    lane4 = lax.broadcasted_iota(I32, (tm, TOP_K), 1)
    g = logits
    mask = jnp.zeros(logits.shape, F32)
    vals, picks = [], []
    for _ in range(TOP_K):
        m = jnp.max(g, axis=1, keepdims=True)
        first = jnp.min(jnp.where(g == m, lane, ne), axis=1, keepdims=True)
        pick = lane == first
        mask = jnp.where(pick, 1.0, mask)
        g = jnp.where(pick, -jnp.inf, g)
        vals.append(m)
        picks.append(pick)
    ex = [jnp.exp(v - vals[0]) for v in vals]
    den = ex[0] + ex[1] + ex[2] + ex[3]
    gates = [e / den for e in ex]

    r = lax.broadcasted_iota(I32, (tm, tm), 0)
    c = lax.broadcasted_iota(I32, (tm, tm), 1)
    tril = jnp.where(c < r, 1.0, 0.0).astype(BF16)
    before = jnp.dot(tril, mask.astype(BF16), preferred_element_type=F32) + carry_scr[...]
    total = carry_scr[...] + jnp.sum(mask, axis=0, keepdims=True)
    carry_scr[...] = total
    cnt_ref[...] = jnp.broadcast_to(total, cnt_ref.shape)

    def pack4(cols):
        return jnp.where(lane4 == 0, cols[0],
                         jnp.where(lane4 == 1, cols[1], jnp.where(lane4 == 2, cols[2], cols[3])))

    wide = lax.broadcasted_iota(I32, (tm, LANES), 1)
    route = jnp.zeros((tm, LANES), F32)
    for k, pk in enumerate(picks):
        idx_k = jnp.sum(jnp.where(pk, lane, 0), axis=1, keepdims=True).astype(F32)
        rank_k = jnp.sum(jnp.where(pk, before, 0.0), axis=1, keepdims=True)
        route = jnp.where(wide == k, idx_k, jnp.where(wide == TOP_K + k, rank_k, route))
    route_ref[...] = route.T[0:2 * TOP_K, :].astype(I32)
    gate_ref[...] = pack4(gates)


def _outproj(mix_a, mix_b, x2, w_a, w_b, gpost, gt, gpre, sc, sh, w_router, b_router):
    s, d = x2.shape
    tm = ROW_TILE
    ne = w_router.shape[1]
    wa = mix_a.shape[1]
    w_hi = w_router.astype(BF16)
    w_router_hl = jnp.stack([w_hi, (w_router - w_hi.astype(F32)).astype(BF16)])
    row = lambda i: (i, 0)
    fix = lambda i: (0, 0)
    vec = pl.BlockSpec((1, d), fix)
    return pl.pallas_call(
        _outproj_kernel,
        out_shape=(jax.ShapeDtypeStruct((s, d), F32),
                   jax.ShapeDtypeStruct((s * SUBLANES, LANES), F32),
                   jax.ShapeDtypeStruct((2 * TOP_K, s), I32),
                   jax.ShapeDtypeStruct((s, TOP_K), F32),
                   jax.ShapeDtypeStruct((8, ne), F32)),
        grid=(s // tm,),
        in_specs=[pl.BlockSpec((tm, wa), row), pl.BlockSpec((tm, wa), row), pl.BlockSpec((tm, d), row),
                  pl.BlockSpec((wa, d), fix), pl.BlockSpec((wa, d), fix),
                  vec, vec, vec, vec, vec,
                  pl.BlockSpec((2, d, ne), lambda i: (0, 0, 0)), pl.BlockSpec((1, ne), fix)],
        out_specs=(pl.BlockSpec((tm, d), row), pl.BlockSpec((tm * SUBLANES, LANES), row),
                   pl.BlockSpec((2 * TOP_K, tm), lambda i: (0, i)),
                   pl.BlockSpec((tm, TOP_K), row), pl.BlockSpec((8, ne), fix)),
        scratch_shapes=[pltpu.VMEM((1, ne), F32)],
        compiler_params=_cparams(("arbitrary",)),
        name="outproj",
    )(mix_a, mix_b, x2, w_a, w_b, gpost, gt, gpre, sc, sh, w_router_hl, b_router)


def _dest_kernel(route_ref, pstart_ref, o_ref):
    route = route_ref[...]
    tm = route.shape[1]
    ne = pstart_ref.shape[0]
    expert = lax.broadcasted_iota(I32, (ne, tm), 0)
    for k in range(TOP_K):
        start = jnp.sum(jnp.where(expert == route[k:k + 1, :], pstart_ref[...], 0), axis=0, keepdims=True)
        o_ref[k:k + 1, :] = start + route[TOP_K + k:TOP_K + k + 1, :]


def _dest(route, pstart):
    s = route.shape[1]
    tm = min(4 * ROW_TILE, s)
    return pl.pallas_call(
        _dest_kernel,
        out_shape=jax.ShapeDtypeStruct((TOP_K, s), I32),
        grid=(s // tm,),
        in_specs=[pl.BlockSpec((2 * TOP_K, tm), lambda i: (0, i)),
                  pl.BlockSpec((N_EXPERTS, 1), lambda i: (0, 0))],
        out_specs=pl.BlockSpec((TOP_K, tm), lambda i: (0, i)),
        compiler_params=_cparams(("arbitrary",)),
        name="dest",
    )(route, pstart.reshape(-1, 1))


def _dispatch_kernel(pstart_ref, pblk_ref, nu_ref, dest_ref, h_ref, xs_ref,
                     zero_scr, sem, zsem):
    tm = h_ref.shape[0] // SUBLANES
    bm = zero_scr.shape[0] // SUBLANES
    n_blk = xs_ref.shape[0] // (bm * SUBLANES)

    rows = _tile_rows

    @pl.when(pl.program_id(0) == 0)
    def _():
        zero_scr[...] = jnp.zeros_like(zero_scr)

        def zero_copy(row0):
            return pltpu.make_async_copy(zero_scr, xs_ref.at[rows(row0, bm)], zsem)

        for phase in range(2):
            for e in range(N_EXPERTS):
                last = pstart_ref[e] + (pblk_ref[e] - 1) * bm
                tail = (n_blk - N_EXPERTS + e) * bm
                for cond, row0 in ((pblk_ref[e] > 0, last), (n_blk - N_EXPERTS + e >= nu_ref[0], tail)):
                    @pl.when(cond)
                    def _():
                        if phase == 0:
                            zero_copy(row0).start()
                        else:
                            zero_copy(row0).wait()

    def row_copy(r, k):
        dst = dest_ref[k, r]
        return pltpu.make_async_copy(h_ref.at[rows(r, 1)], xs_ref.at[rows(dst, 1)], sem)

    def start(r, carry):
        for k in range(TOP_K):
            row_copy(r, k).start(priority=k % 2)
        return carry

    lax.fori_loop(0, tm, start, 0, unroll=4)
    for k in range(TOP_K):
        pltpu.make_async_copy(h_ref, xs_ref.at[rows(0, tm)], sem).wait()


def _dispatch(pstart, pblk, n_used, dest, h2, n_rows):
    s = h2.shape[0] // SUBLANES
    tm = COMBINE_TILE
    grid_spec = pltpu.PrefetchScalarGridSpec(
        num_scalar_prefetch=3,
        grid=(s // tm,),
        in_specs=[pl.BlockSpec((TOP_K, tm), lambda i, *_: (0, i), memory_space=pltpu.SMEM),
                  pl.BlockSpec((tm * SUBLANES, LANES), lambda i, *_: (i, 0))],
        out_specs=pl.BlockSpec(memory_space=pl.ANY),
        scratch_shapes=[pltpu.VMEM((EXPERT_BLOCK * SUBLANES, LANES), F32), pltpu.SemaphoreType.DMA,
                        pltpu.SemaphoreType.DMA],
    )
    return pl.pallas_call(
        _dispatch_kernel,
        out_shape=jax.ShapeDtypeStruct((n_rows * SUBLANES, LANES), F32),
        grid_spec=grid_spec,
        compiler_params=_cparams(("arbitrary",)),
        name="dispatch",
    )(pstart, pblk, n_used, dest, h2)


def _experts_kernel(be_ref, nu_ref, xs_ref, wgu_ref, bgu_ref, wd_ref, bd_ref, y_ref,
                    wgu_bf, wd_bf):
    b = pl.program_id(0)
    d_exp = wd_ref.shape[1]
    prev = be_ref[jnp.maximum(b - 1, 0)]
    changed = (b == 0) | (be_ref[b] != prev)

    @pl.when((b < nu_ref[0]) & changed)
    def _():
        rows = LANES

        def cast_gu(c, carry):
            r0 = pl.multiple_of(c * rows, rows)
            wgu_bf[pl.ds(r0, rows), :] = wgu_ref[0, pl.ds(r0, rows), :].astype(BF16)
            return carry

        def cast_d(c, carry):
            r0 = pl.multiple_of(c * rows, rows)
            wd_bf[pl.ds(r0, rows), :] = wd_ref[0, pl.ds(r0, rows), :].astype(BF16)
            return carry

        lax.fori_loop(0, wgu_ref.shape[1] // rows, cast_gu, 0)
        lax.fori_loop(0, wd_ref.shape[1] // rows, cast_d, 0)

    @pl.when(b < nu_ref[0])
    def _():
        bm = xs_ref.shape[0] // SUBLANES
        xb = _load_token_tiles(xs_ref, bm).astype(BF16)
        hdn = jnp.dot(xb, wgu_bf[...], preferred_element_type=F32) + bgu_ref[0]
        x_glu = jnp.minimum(hdn[:, :d_exp], SWIGLU_LIMIT)
        x_lin = jnp.clip(hdn[:, d_exp:], -SWIGLU_LIMIT, SWIGLU_LIMIT)
        act = x_glu * jax.nn.sigmoid(SWIGLU_ALPHA * x_glu) * (x_lin + 1.0)
        _store_token_tiles(y_ref, jnp.dot(act.astype(BF16), wd_bf[...], preferred_element_type=F32)
                           + bd_ref[0])

    @pl.when(b >= nu_ref[0])
    def _():
        y_ref[...] = jnp.zeros_like(y_ref)


def _experts(block_e, n_used, xs, w_gate_up, b_gate_up, w_down, b_down):
    n_rows = xs.shape[0] // SUBLANES
    bm = EXPERT_BLOCK
    n_blk = n_rows // bm
    ne, d, two_de = w_gate_up.shape
    de = w_down.shape[1]
    assert d == SUBLANES * LANES

    def blk(b, be, nu):
        return jnp.minimum(b, nu[0] - 1)

    grid_spec = pltpu.PrefetchScalarGridSpec(
        num_scalar_prefetch=2,
        grid=(n_blk,),
        in_specs=[pl.BlockSpec((bm * SUBLANES, LANES), lambda b, be, nu: (blk(b, be, nu), 0)),
                  pl.BlockSpec((1, d, two_de), lambda b, be, nu: (be[blk(b, be, nu)], 0, 0)),
                  pl.BlockSpec((1, 1, two_de), lambda b, be, nu: (be[blk(b, be, nu)], 0, 0)),
                  pl.BlockSpec((1, de, d), lambda b, be, nu: (be[blk(b, be, nu)], 0, 0)),
                  pl.BlockSpec((1, 1, d), lambda b, be, nu: (be[blk(b, be, nu)], 0, 0))],
        out_specs=pl.BlockSpec((bm * SUBLANES, LANES), lambda b, be, nu: (b, 0)),
        scratch_shapes=[pltpu.VMEM((d, two_de), BF16), pltpu.VMEM((de, d), BF16)],
    )
    return pl.pallas_call(
        _experts_kernel,
        out_shape=jax.ShapeDtypeStruct((n_rows * SUBLANES, LANES), F32),
        grid_spec=grid_spec,
        compiler_params=_cparams(("arbitrary",)),
        name="experts",
    )(block_e, n_used, xs, w_gate_up, b_gate_up.reshape(ne, 1, two_de), w_down, b_down.reshape(ne, 1, d))


def _combine_kernel(dest_ref, y_ref, gate_ref, x1_ref, gt_ref, gpost_ref, o_ref, buf, sem):
    tm = x1_ref.shape[0]

    def row_copy(r, k):
        src = dest_ref[k, r]
        return pltpu.make_async_copy(y_ref.at[_tile_rows(src, 1)], buf.at[k, _tile_rows(r, 1)], sem)

    def start(r, carry):
        for k in range(TOP_K):
            row_copy(r, k).start(priority=k % 2)
        return carry

    lax.fori_loop(0, tm, start, 0, unroll=4)
    for k in range(TOP_K):
        pltpu.make_async_copy(y_ref.at[_tile_rows(0, tm)], buf.at[k], sem).wait()

    gate = gate_ref[...]
    acc = gate[:, 0:1] * _load_token_tiles(buf.at[0], tm)
    for k in range(1, TOP_K):
        acc = acc + gate[:, k:k + 1] * _load_token_tiles(buf.at[k], tm)
    o_ref[...] = x1_ref[...] + gt_ref[...] * (_rms(acc) * gpost_ref[...])


def _combine(dest, y, gate4, x1, gt, gpost):
    s, d = x1.shape
    tm = COMBINE_TILE
    return pl.pallas_call(
        _combine_kernel,
        out_shape=jax.ShapeDtypeStruct((s, d), F32),
        grid=(s // tm,),
        in_specs=[pl.BlockSpec((TOP_K, tm), lambda i: (0, i), memory_space=pltpu.SMEM),
                  pl.BlockSpec(memory_space=pl.ANY),
                  pl.BlockSpec((tm, TOP_K), lambda i: (i, 0)),
                  pl.BlockSpec((tm, d), lambda i: (i, 0)),
                  pl.BlockSpec((1, d), lambda i: (0, 0)),
                  pl.BlockSpec((1, d), lambda i: (0, 0))],
        out_specs=pl.BlockSpec((tm, d), lambda i: (i, 0)),
        scratch_shapes=[pltpu.VMEM((TOP_K, tm * SUBLANES, LANES), F32), pltpu.SemaphoreType.DMA],
        compiler_params=_cparams(("arbitrary",)),
        name="combine",
    )(dest, y, gate4, x1, gt, gpost)


def _layer(x2, mod, g_pre_mix, g_post_mix, w_in, b_forget, rel_bias, w_out,
           g_pre_ffn, g_post_ffn, w_router, b_router, w_gate_up, b_gate_up, w_down, b_down):
    s, d = x2.shape
    sh_m, sc_m, gt_m, sh_f, sc_f, gt_f = [mod[:, k * d:(k + 1) * d] for k in range(6)]
    n_qkv = 3 * (N_HEADS_FOX + N_HEADS_MOBA) * HEAD_DIM
    fox_w = N_HEADS_FOX * HEAD_DIM

    w_qkv = w_in[:, :n_qkv].astype(BF16)
    w_f = w_in[:, n_qkv:].T
    qkv, cum, sel, nrm, tr, nr = _inproj(x2, g_pre_mix.reshape(1, d), sc_m, sh_m, w_qkv, w_f,
                                         b_forget.reshape(-1, 1))
    y_a = _fox(qkv, tr, nr, cum, nrm)
    y_b = _moba(qkv, tr, sel, rel_bias)

    w_out_bf = w_out.astype(BF16)
    x1, h2, route, gate4, cnt = _outproj(
        y_a, y_b, x2, w_out_bf[:fox_w], w_out_bf[fox_w:], g_post_mix.reshape(1, d), gt_m,
        g_pre_ffn.reshape(1, d), sc_f, sh_f, w_router, b_router.reshape(1, -1))

    bm = EXPERT_BLOCK
    counts = cnt[0].astype(I32)
    pblk = (counts + bm - 1) // bm
    pend_blk = jnp.cumsum(pblk)
    pstart = ((pend_blk - pblk) * bm).astype(I32)
    n_rows = s * TOP_K + N_EXPERTS * bm
    n_blk = n_rows // bm
    block_e = jnp.minimum(jnp.sum(pend_blk[None, :] <= jnp.arange(n_blk)[:, None], axis=1),
                          N_EXPERTS - 1).astype(I32)
    n_used = pend_blk[-1:].astype(I32)

    dest = _dest(route, pstart)
    xs = _dispatch(pstart, pblk.astype(I32), n_used, dest, h2, n_rows)
    y = _experts(block_e, n_used, xs, w_gate_up, b_gate_up, w_down, b_down)
    return _combine(dest, y, gate4, x1, gt_f, g_post_ffn.reshape(1, d))


def kernel(x, c, w_ada, b_ada, g_pre_mix, g_post_mix, w_in, b_forget, rel_bias, w_out, g_pre_ffn, g_post_ffn, w_router, b_router, w_gate_up, b_gate_up, w_down, b_down):
    bsz, s, d = x.shape
    depth = w_ada.shape[0]
    outs = []
    for bi in range(bsz):
        x2 = x[bi]
        for l in range(depth):
            mod = _adaln(c[bi:bi + 1], w_ada[l], b_ada[l])
            x2 = _layer(x2, mod, g_pre_mix[l], g_post_mix[l], w_in[l], b_forget[l], rel_bias, w_out[l],
                        g_pre_ffn[l], g_post_ffn[l], w_router[l], b_router[l], w_gate_up[l], b_gate_up[l],
                        w_down[l], b_down[l])
        outs.append(x2)
    return outs[0].reshape(1, s, d) if bsz == 1 else jnp.stack(outs)
```

```python
import math

import numpy as np
import jax
import jax.numpy as jnp
from jax import lax
from jax.experimental import pallas as pl
from jax.experimental.pallas import tpu as pltpu

F32 = jnp.float32
BF16 = jnp.bfloat16
I32 = jnp.int32

HEAD_DIM = 64
N_HEADS_FOX = 8
N_HEADS_MOBA = 8
PAIR = 2 * HEAD_DIM
MOBA_BLOCK = 256
MOBA_TOPK = 3
NUM_BUCKETS = 32
MAX_DISTANCE = 128
N_EXPERTS = 32
TOP_K = 4
SWIGLU_LIMIT = 7.0
SWIGLU_ALPHA = 1.702
RMS_EPS = 1e-6
NEG = -(2.0 ** 100)
M_INIT = -(2.0 ** 99)
LOG2E = math.log2(math.e)
SUM_ROWS = 16
SUBLANES = 8
LANES = 128
EXP_UNDERFLOW = 90.0
VMEM_LIMIT = 56 * 1024 * 1024

ROW_TILE = 512
FOX_TILE = 256
EXPERT_BLOCK = 512
COMBINE_TILE = 1024
MOBA_LAG = 1
MOBA_UNROLL = 4

NT_DIMS = (((1,), (1,)), ((), ()))


def _cparams(sem):
    return pltpu.CompilerParams(dimension_semantics=sem, vmem_limit_bytes=VMEM_LIMIT)


def _rms(x):
    return x * lax.rsqrt(jnp.mean(x * x, axis=-1, keepdims=True) + RMS_EPS)


def _adaln_kernel(c_ref, w_ref, b_ref, o_ref):
    c = c_ref[...]
    cond = c * jax.nn.sigmoid(c)
    o_ref[...] = jnp.dot(cond, w_ref[...], preferred_element_type=F32,
                         precision=lax.Precision.HIGHEST) + b_ref[...]


def _adaln(c, w_ada, b_ada):
    d = c.shape[-1]
    n = w_ada.shape[-1]
    c8 = jnp.broadcast_to(c.reshape(1, d), (8, d))
    out = pl.pallas_call(
        _adaln_kernel,
        out_shape=jax.ShapeDtypeStruct((8, n), F32),
        grid=(n // d,),
        in_specs=[pl.BlockSpec((8, d), lambda j: (0, 0)),
                  pl.BlockSpec((d, d), lambda j: (0, j)),
                  pl.BlockSpec((1, d), lambda j: (0, j))],
        out_specs=pl.BlockSpec((8, d), lambda j: (0, j)),
        compiler_params=_cparams(("arbitrary",)),
        name="adaln",
    )(c8, w_ada, b_ada.reshape(1, n))
    return out[0:1]


def _inproj_kernel(x_ref, g_ref, sc_ref, sh_ref, w_ref, wf_ref, bf_ref,
                   qkv_ref, cum_ref, sel_ref, nrm_ref, tr_ref, nr_ref, km_scr, carry_scr):
    i = pl.program_id(0)
    tm = x_ref.shape[0]
    nblk = km_scr.shape[0]

    @pl.when(i == 0)
    def _():
        km_scr[...] = jnp.zeros_like(km_scr)
        carry_scr[...] = jnp.zeros_like(carry_scr)

    x = x_ref[...]
    h = _rms(x) * g_ref[...] * (1.0 + sc_ref[...]) + sh_ref[...]
    hb = h.astype(BF16)

    width = N_HEADS_FOX * HEAD_DIM
    hsel = jnp.where(lax.broadcasted_iota(I32, (width, N_HEADS_FOX), 0) // HEAD_DIM
                     == lax.broadcasted_iota(I32, (width, N_HEADS_FOX), 1), 1.0, 0.0)
    kb = None
    tr_slot = {3: 0, 5: 1, 0: 2, 2: 3}
    qbt = None
    for c in range(6):
        pc = jnp.dot(hb, w_ref[:, c * width:(c + 1) * width], preferred_element_type=F32)
        if c in tr_slot:
            n = tr_slot[c]
            pct = pc.T
            if c == 3:
                qbt = pct
            if c == 0 or c == 3:
                pct = pct * (LOG2E * HEAD_DIM ** -0.5)
            tr_ref[n * width:(n + 1) * width, :] = pct.astype(BF16)
        if c == 1 or c == 4:
            qkv_ref[:, (c // 3) * width:(c // 3 + 1) * width] = pc.astype(BF16)
        if c < 2:
            sq = (pc * (HEAD_DIM ** -0.5) if c == 0 else pc).astype(BF16).astype(F32)
            n2 = jnp.dot((sq * sq).astype(BF16), hsel.astype(BF16), preferred_element_type=F32)
            nrm_ref[0, c:c + 1, :] = jnp.max(n2, axis=0, keepdims=True)
        if c == 4:
            kb = pc

    ft = lax.dot_general(wf_ref[...].astype(BF16), hb, NT_DIMS, preferred_element_type=F32)
    z = ft + bf_ref[...]
    logf = -(jnp.maximum(-z, 0.0) + jnp.log1p(jnp.exp(-jnp.abs(z))))
    lane = lax.broadcasted_iota(I32, logf.shape, 1)
    cs = logf
    sh = 1
    while sh < tm:
        cs = cs + jnp.where(lane >= sh, pltpu.roll(cs, sh, axis=1), 0.0)
        sh *= 2
    base = jnp.zeros_like(cs)
    for b in range(1, tm // FOX_TILE):
        base = jnp.where(lane >= b * FOX_TILE, cs[:, b * FOX_TILE - 1:b * FOX_TILE], base)
    nr_t = (-LOG2E * (cs - base)).T
    for pp in range(nr_ref.shape[0]):
        nr_ref[pp] = nr_t[:, 2 * pp:2 * pp + 2]
    cs = cs + carry_scr[...]
    cum_ref[...] = cs
    carry_scr[...] = cs[:, tm - 1:tm]

    nb_tile = tm // MOBA_BLOCK
    for b in range(nb_tile):
        kmean = jnp.sum(kb[b * MOBA_BLOCK:(b + 1) * MOBA_BLOCK], axis=0, keepdims=True) * (1.0 / MOBA_BLOCK)
        km_scr[pl.ds(i * nb_tile + b, 1), :] = kmean

    km = km_scr[...]
    blk = lax.broadcasted_iota(I32, (nblk, tm), 0)
    col = lax.broadcasted_iota(I32, (nblk, tm), 1)
    own = i * nb_tile + col // MOBA_BLOCK
    for hd in range(N_HEADS_MOBA):
        hs = slice(hd * HEAD_DIM, (hd + 1) * HEAD_DIM)
        g = jnp.dot(km[:, hs], qbt[hs, :], preferred_element_type=F32,
                    precision=lax.Precision.HIGHEST)
        g = jnp.where(blk < own, g, -jnp.inf)
        sel = jnp.zeros(g.shape, dtype=jnp.bool_)
        for _ in range(MOBA_TOPK):
            m = jnp.max(g, axis=0, keepdims=True)
            first = jnp.min(jnp.where(g == m, blk, nblk), axis=0, keepdims=True)
            pick = (blk == first) & (m > -jnp.inf)
            sel = sel | pick
            g = jnp.where(pick, -jnp.inf, g)
        sel_ref[hd] = jnp.where(sel, 0.0, NEG)


def _inproj(x2, g, sc, sh, w_qkv, w_f, b_f):
    s, d = x2.shape
    tm = ROW_TILE
    nblk = s // MOBA_BLOCK
    n = w_qkv.shape[1]
    width = N_HEADS_FOX * HEAD_DIM
    n_rows_out = 2 * width
    n_tr = 4 * width
    assert tm % FOX_TILE == 0 and N_HEADS_FOX == N_HEADS_MOBA
    return pl.pallas_call(
        _inproj_kernel,
        out_shape=(jax.ShapeDtypeStruct((s, n_rows_out), BF16),
                   jax.ShapeDtypeStruct((N_HEADS_FOX, s), F32),
                   jax.ShapeDtypeStruct((N_HEADS_MOBA, nblk, s), F32),
                   jax.ShapeDtypeStruct((s // tm, 2, N_HEADS_FOX), F32),
                   jax.ShapeDtypeStruct((n_tr, s), BF16),
                   jax.ShapeDtypeStruct((N_HEADS_FOX // 2, s, 2), F32)),
        grid=(s // tm,),
        in_specs=[pl.BlockSpec((tm, d), lambda i: (i, 0)),
                  pl.BlockSpec((1, d), lambda i: (0, 0)),
                  pl.BlockSpec((1, d), lambda i: (0, 0)),
                  pl.BlockSpec((1, d), lambda i: (0, 0)),
                  pl.BlockSpec((d, n), lambda i: (0, 0)),
                  pl.BlockSpec((N_HEADS_FOX, d), lambda i: (0, 0)),
                  pl.BlockSpec((N_HEADS_FOX, 1), lambda i: (0, 0))],
        out_specs=(pl.BlockSpec((tm, n_rows_out), lambda i: (i, 0)),
                   pl.BlockSpec((N_HEADS_FOX, tm), lambda i: (0, i)),
                   pl.BlockSpec((N_HEADS_MOBA, nblk, tm), lambda i: (0, 0, i)),
                   pl.BlockSpec((1, 2, N_HEADS_FOX), lambda i: (i, 0, 0)),
                   pl.BlockSpec((n_tr, tm), lambda i: (0, i)),
                   pl.BlockSpec((N_HEADS_FOX // 2, tm, 2), lambda i: (0, i, 0))),
        scratch_shapes=[pltpu.VMEM((nblk, N_HEADS_MOBA * HEAD_DIM), F32),
                        pltpu.VMEM((N_HEADS_FOX, 1), F32)],
        compiler_params=_cparams(("arbitrary",)),
        name="inproj",
    )(x2, g, sc, sh, w_qkv, w_f, b_f)


def _fox_kernel(jlo_ref, cp_ref, qt_ref, k_ref, vt_ref, nr_ref, o_ref, m_ref, a_ref):
    p = pl.program_id(0)
    g = pl.program_id(1)
    j_first = jlo_ref[p * pl.num_programs(1) + g]
    tq = qt_ref.shape[1]
    t = tq // 2

    qt = qt_ref[...]
    top = lax.broadcasted_iota(I32, qt.shape, 0) < HEAD_DIM
    zq = jnp.zeros_like(qt)
    qth = (jnp.where(top, qt, zq), jnp.where(top, zq, qt))
    m_ref[...] = jnp.full(m_ref.shape, M_INIT, F32)
    a_ref[...] = jnp.zeros(a_ref.shape, F32)
    er = lax.broadcasted_iota(I32, (SUM_ROWS, 2 * t), 0)
    ec = lax.broadcasted_iota(I32, (SUM_ROWS, 2 * t), 1)
    ones_rows = jnp.where(((er == 0) & (ec < t)) | ((er == 1) & (ec >= t)), 1.0, 0.0).astype(BF16)
    arow = lax.broadcasted_iota(I32, a_ref.shape, 0)
    head0_rows = (arow < HEAD_DIM) | (arow == PAIR)
    vtop = lax.broadcasted_iota(I32, (PAIR, t), 0) < HEAD_DIM
    in_a = lax.broadcasted_iota(I32, (1, tq), 1) < t
    krow = lax.broadcasted_iota(I32, (t, tq), 0)
    qcol = lax.broadcasted_iota(I32, (t, tq), 1)
    future_a = (qcol < t) & (krow > qcol)
    future_b = (qcol >= t) & (krow > qcol - t)

    def scores(j, future):
        k0 = pl.multiple_of(j * t, t)
        kt = k_ref[pl.ds(k0, t), :]
        out = []
        for hd in range(2):
            s = jnp.dot(kt, qth[hd], preferred_element_type=F32) + nr_ref[0, pl.ds(k0, t), hd:hd + 1]
            out.append(s if future is None else jnp.where(future, NEG, s))
        return out

    def shifts(j, a_on, b_on):
        rows = []
        for hd in range(2):
            h = 2 * p + hd
            sa = LOG2E * (cp_ref[h, 2 * g] - cp_ref[h, j])
            sb = LOG2E * (cp_ref[h, 2 * g + 1] - cp_ref[h, j])
            rows.append(jnp.where(in_a, jnp.where(a_on, sa, NEG), jnp.where(b_on, sb, NEG)))
        return rows

    def softmax(ss, shift_rows):
        ps, alphas = [], []
        for hd in range(2):
            m_old = m_ref[hd]
            m_new = jnp.maximum(m_old, jnp.max(ss[hd], axis=0, keepdims=True) + shift_rows[hd])
            alphas.append(jnp.exp2(m_old - m_new))
            m_ref[hd] = m_new
            ps.append(jnp.exp2(ss[hd] - (m_new - shift_rows[hd])).astype(BF16))
        return ps, alphas

    def accumulate(j, ps, alphas):
        k0 = pl.multiple_of(j * t, t)
        vt = vt_ref[:, pl.ds(k0, t)]
        zv = jnp.zeros_like(vt)
        vcat = jnp.concatenate([jnp.where(vtop, vt, zv), jnp.where(vtop, zv, vt)], axis=1)
        vcat = jnp.concatenate([vcat, ones_rows], axis=0)
        pcat = jnp.concatenate(ps, axis=0)
        a_ref[...] = (a_ref[...] * jnp.where(head0_rows, alphas[0], alphas[1])
                      + jnp.dot(vcat, pcat, preferred_element_type=F32))

    def body(j, carry):
        accumulate(j, *softmax(scores(j, None), shifts(j, True, True)))
        return carry

    lax.fori_loop(j_first, 2 * g - 1, body, 0)
    j1 = jnp.maximum(2 * g - 1, 0)
    j2 = 2 * g
    j3 = 2 * g + 1
    ss1 = scores(j1, None)
    ss2 = scores(j2, future_a)
    ss3 = scores(j3, future_b)
    w1 = softmax(ss1, shifts(j1, g >= 1, g >= 1))
    w2 = softmax(ss2, shifts(j2, True, True))
    accumulate(j1, *w1)
    w3 = softmax(ss3, shifts(j3, False, True))
    accumulate(j2, *w2)
    accumulate(j3, *w3)
    out_t =a_ref[0:PAIR, :] / jnp.where(top, a_ref[PAIR:PAIR + 1, :], a_ref[PAIR + 1:PAIR + 2, :])
    o_ref[...] = out_t.T.astype(o_ref.dtype)


def _fox_first_tile(cum, nrm, t):
    cend = cum[:, t - 1::t]
    nt = cend.shape[1]
    cprev = jnp.concatenate([jnp.zeros((cend.shape[0], 1), F32), cend[:, :-1]], axis=1)
    rep = nt // nrm.shape[0]
    qn = jnp.repeat(jnp.sqrt(nrm[:, 0, :]).T, rep, axis=1)
    kn = jnp.repeat(jnp.sqrt(nrm[:, 1, :]).T, rep, axis=1)
    gap = (1.02 * qn[:, :, None] * (kn[:, None, :] + kn[:, :, None])
           + cprev[:, :, None] - cend[:, None, :])
    jj = jnp.arange(nt)[None, None, :]
    ii = jnp.arange(nt)[None, :, None]
    needed = (jj < ii) & jnp.logical_not(gap <= -EXP_UNDERFLOW)
    needed = needed[0::2] | needed[1::2]
    first = jnp.min(jnp.where(needed, jj, ii), axis=2)
    return first.reshape(-1).astype(I32)


def _fox(qkv, tr, nr, cum, nrm):
    s = qkv.shape[0]
    t = FOX_TILE
    npair = N_HEADS_FOX // 2
    tq = 2 * t
    jlo = _fox_first_tile(cum, nrm, t).reshape(npair, s // tq, 2).min(axis=2).reshape(-1)
    cend = cum[:, t - 1::t]
    cprev = jnp.concatenate([jnp.zeros((cend.shape[0], 1), F32), cend[:, :-1]], axis=1)
    q_rows = 2 * (N_HEADS_MOBA // 2)
    grid_spec = pltpu.PrefetchScalarGridSpec(
        num_scalar_prefetch=1,
        grid=(npair, s // tq),
        in_specs=[pl.BlockSpec(memory_space=pltpu.SMEM),
                  pl.BlockSpec((PAIR, tq), lambda p, i, jl: (q_rows + p, i)),
                  pl.BlockSpec((s, PAIR), lambda p, i, jl: (0, p)),
                  pl.BlockSpec((PAIR, s), lambda p, i, jl: (q_rows + npair + p, 0)),
                  pl.BlockSpec((1, s, 2), lambda p, i, jl: (p, 0, 0))],
        out_specs=pl.BlockSpec((tq, PAIR), lambda p, i, jl: (i, p)),
        scratch_shapes=[pltpu.VMEM((2, 1, tq), F32), pltpu.VMEM((PAIR + SUM_ROWS, tq), F32)],
    )
    return pl.pallas_call(
        _fox_kernel,
        out_shape=jax.ShapeDtypeStruct((s, npair * PAIR), BF16),
        grid_spec=grid_spec,
        compiler_params=_cparams(("arbitrary", "arbitrary")),
        name="fox",
    )(jlo, cprev, tr, qkv, tr, nr)


def _t5_bucket_np(dist):
    dist = np.maximum(dist, 0)
    max_exact = NUM_BUCKETS // 2
    d = np.maximum(dist, 1).astype(np.float32)
    large = max_exact + (np.log(d / np.float32(max_exact)) / np.float32(math.log(MAX_DISTANCE / max_exact))
                         * np.float32(NUM_BUCKETS - max_exact)).astype(np.int32)
    large = np.minimum(large, NUM_BUCKETS - 1)
    return np.where(dist < max_exact, dist, large).astype(np.int32)


def _moba_kernel(rb_ref, qt_ref, k_ref, vt_ref, sel_ref, bkt_ref, o_ref,
                 m_ref, a_ref, bias_scr, *bufs):
    s_bufs = bufs[0:MOBA_UNROLL]
    p_bufs = bufs[MOBA_UNROLL:2 * MOBA_UNROLL]
    al_bufs = bufs[2 * MOBA_UNROLL:3 * MOBA_UNROLL]
    p = pl.program_id(0)
    g = pl.program_id(1)
    tq = qt_ref.shape[1]
    t = tq // 2

    @pl.when(g == 0)
    def _():
        r = lax.broadcasted_iota(I32, (t, t), 0)
        c = lax.broadcasted_iota(I32, (t, t), 1)
        zero = jnp.zeros((t, t), F32)
        for hd in range(2):
            h = 2 * p + hd
            far = rb_ref[(NUM_BUCKETS - 1) * N_HEADS_MOBA + h]
            tiles = []
            for w in range(2):
                bkt = bkt_ref[w]
                acc = jnp.zeros(bkt.shape, F32)
                for kk in range(NUM_BUCKETS):
                    acc = acc + jnp.where(bkt == kk, rb_ref[kk * N_HEADS_MOBA + h], 0.0)
                tiles.append((acc - far) * LOG2E)
            prev_t = tiles[0]
            own_t = jnp.where(r <= c, tiles[1], NEG)
            bias_scr[hd, 0] = jnp.concatenate([prev_t, zero], axis=1)
            bias_scr[hd, 1] = jnp.concatenate([own_t, prev_t], axis=1)
            bias_scr[hd, 2] = jnp.concatenate([zero, own_t], axis=1)

    qt = qt_ref[...]
    top = lax.broadcasted_iota(I32, qt.shape, 0) < HEAD_DIM
    zq = jnp.zeros_like(qt)
    qth = (jnp.where(top, qt, zq), jnp.where(top, zq, qt))
    m_ref[...] = jnp.full(m_ref.shape, M_INIT, F32)
    a_ref[...] = jnp.zeros(a_ref.shape, F32)

    er = lax.broadcasted_iota(I32, (SUM_ROWS, 2 * t), 0)
    ec = lax.broadcasted_iota(I32, (SUM_ROWS, 2 * t), 1)
    ones_rows = jnp.where(((er == 0) & (ec < t)) | ((er == 1) & (ec >= t)), 1.0, 0.0).astype(BF16)
    arow = lax.broadcasted_iota(I32, a_ref.shape, 0)
    head0_rows = (arow < HEAD_DIM) | (arow == PAIR)
    vtop = lax.broadcasted_iota(I32, (PAIR, t), 0) < HEAD_DIM
    in_a = lax.broadcasted_iota(I32, (1, tq), 1) < t

    n_far = jnp.maximum(2 * g - 1, 0)

    n_key_tiles = k_ref.shape[0] // t

    def produce(j, s_buf):
        j = jnp.minimum(j, n_key_tiles - 1)
        k0 = pl.multiple_of(j * t, t)
        kt = k_ref[pl.ds(k0, t), :]
        for hd in range(2):
            s_buf[hd] = jnp.dot(kt, qth[hd], preferred_element_type=F32)

    def softmax(s_buf, p_buf, al_buf, selrows, w):
        for hd in range(2):
            s = s_buf[hd]
            if w is not None:
                s = s + bias_scr[hd, w]
            smax = jnp.max(s, axis=0, keepdims=True)
            m_old = m_ref[hd]
            m_new = jnp.maximum(m_old, smax + selrows[hd])
            shift = m_new - selrows[hd]
            al_buf[hd] = jnp.exp2(m_old - m_new)
            m_ref[hd] = m_new
            p_buf[hd * t:(hd + 1) * t, :] = jnp.exp2(s - shift).astype(BF16)

    def far_rows(j):
        return [jnp.where(j < n_far, sel_ref[hd, pl.ds(j, 1), :], NEG) for hd in range(2)]

    def accumulate(j, p_buf, al_buf):
        k0 = pl.multiple_of(j * t, t)
        vt = vt_ref[:, pl.ds(k0, t)]
        zv = jnp.zeros_like(vt)
        vcat = jnp.concatenate([jnp.where(vtop, vt, zv), jnp.where(vtop, zv, vt)], axis=1)
        vcat = jnp.concatenate([vcat, ones_rows], axis=0)
        a_ref[...] = (a_ref[...] * jnp.where(head0_rows, al_buf[0], al_buf[1])
                      + jnp.dot(vcat, p_buf[...], preferred_element_type=F32))

    un = MOBA_UNROLL
    lag = MOBA_LAG
    for n in range(un - lag, un):
        p_bufs[n][...] = jnp.zeros_like(p_bufs[n])
        al_bufs[n][...] = jnp.ones_like(al_bufs[n])
    produce(0, s_bufs[0])

    def body(u, carry):
        j0 = un * u
        for n in range(un):
            accumulate(jnp.maximum(j0 + n - lag, 0), p_bufs[(n - lag) % un], al_bufs[(n - lag) % un])
            softmax(s_bufs[n], p_bufs[n], al_bufs[n], far_rows(j0 + n), None)
            produce(j0 + n + 1, s_bufs[(n + 1) % un])
        return carry

    n_trips = (n_far + un - 1) // un
    lax.fori_loop(0, n_trips, body, 0)

    j1 = jnp.maximum(2 * g - 1, 0)
    j2 = 2 * g
    j3 = 2 * g + 1
    rows1 = [jnp.where(g >= 1, sel_ref[hd, pl.ds(j1, 1), :], NEG) for hd in range(2)]
    rows2 = [jnp.where(in_a, 0.0, sel_ref[hd, pl.ds(j2, 1), :]) for hd in range(2)]
    rows3 = [jnp.where(in_a, NEG, 0.0)] * 2
    produce(j1, s_bufs[0])
    produce(j2, s_bufs[1])
    for n in range(un - lag, un):
        accumulate(jnp.maximum(un * n_trips - un + n, 0), p_bufs[n], al_bufs[n])
    softmax(s_bufs[0], p_bufs[0], al_bufs[0], rows1, 0)
    produce(j3, s_bufs[0])
    softmax(s_bufs[1], p_bufs[1], al_bufs[1], rows2, 1)
    accumulate(j1, p_bufs[0], al_bufs[0])
    softmax(s_bufs[0], p_bufs[2], al_bufs[2], rows3, 2)
    accumulate(j2, p_bufs[1], al_bufs[1])
    accumulate(j3, p_bufs[2], al_bufs[2])
    out_t = a_ref[0:PAIR, :] / jnp.where(top, a_ref[PAIR:PAIR + 1, :], a_ref[PAIR + 1:PAIR + 2, :])
    o_ref[...] = out_t.T.astype(o_ref.dtype)


def _moba(qkv, tr, sel, rel_bias):
    s = qkv.shape[0]
    t = MOBA_BLOCK
    npair = N_HEADS_MOBA // 2
    nblk = s // t
    kcol = N_HEADS_FOX // 2
    a = np.arange(t)[None, :]
    b = np.arange(t)[:, None]
    bkt = jnp.asarray(np.stack([_t5_bucket_np(t + a - b), _t5_bucket_np(a - b)]))
    tq = 2 * t
    grid_spec = pltpu.PrefetchScalarGridSpec(
        num_scalar_prefetch=1,
        grid=(npair, s // tq),
        in_specs=[pl.BlockSpec((PAIR, tq), lambda p, i, rb: (p, i)),
                  pl.BlockSpec((s, PAIR), lambda p, i, rb: (0, kcol + p)),
                  pl.BlockSpec((PAIR, s), lambda p, i, rb: (npair + p, 0)),
                  pl.BlockSpec((2, nblk, tq), lambda p, i, rb: (p, 0, i)),
                  pl.BlockSpec((2, t, t), lambda p, i, rb: (0, 0, 0))],
        out_specs=pl.BlockSpec((tq, PAIR), lambda p, i, rb: (i, p)),
        scratch_shapes=[pltpu.VMEM((2, 1, tq), F32),
                        pltpu.VMEM((PAIR + SUM_ROWS, tq), F32), pltpu.VMEM((2, 3, t, tq), F32)]
        + [pltpu.VMEM((2, t, tq), F32)] * MOBA_UNROLL
        + [pltpu.VMEM((2 * t, tq), BF16)] * MOBA_UNROLL
        + [pltpu.VMEM((2, 1, tq), F32)] * MOBA_UNROLL,
    )
    return pl.pallas_call(
        _moba_kernel,
        out_shape=jax.ShapeDtypeStruct((s, npair * PAIR), BF16),
        grid_spec=grid_spec,
        compiler_params=_cparams(("arbitrary", "arbitrary")),
        name="moba",
    )(rel_bias.reshape(-1), tr, qkv, tr, sel, bkt)


def _store_token_tiles(ref, val):
    n = val.shape[0]
    for c in range(SUBLANES):
        ref[pl.ds(c, n, stride=SUBLANES), :] = val[:, c * LANES:(c + 1) * LANES]


def _load_token_tiles(ref, n):
    return jnp.concatenate([ref[pl.ds(c, n, stride=SUBLANES), :] for c in range(SUBLANES)], axis=1)


def _tile_rows(r0, n):
    start = r0 * SUBLANES
    if not isinstance(start, int):
        start = pl.multiple_of(start, SUBLANES)
    return pl.ds(start, n * SUBLANES)


def _outproj_kernel(ya_ref, yb_ref, x_ref, wa_ref, wb_ref, gpost_ref, gt_ref, gpre_ref,
                    sc_ref, sh_ref, wr_ref, br_ref,
                    x1_ref, h2_ref, route_ref, gate_ref, cnt_ref, carry_scr):
    i = pl.program_id(0)
    tm = x_ref.shape[0]

    @pl.when(i == 0)
    def _():
        carry_scr[...] = jnp.zeros_like(carry_scr)

    y = (jnp.dot(ya_ref[...], wa_ref[...], preferred_element_type=F32)
         + jnp.dot(yb_ref[...], wb_ref[...], preferred_element_type=F32))
    x1 = x_ref[...] + gt_ref[...] * (_rms(y) * gpost_ref[...])
    x1_ref[...] = x1
    h2 = _rms(x1) * gpre_ref[...] * (1.0 + sc_ref[...]) + sh_ref[...]
    _store_token_tiles(h2_ref, h2)

    h_hi = h2.astype(BF16)
    h_lo = (h2 - h_hi.astype(F32)).astype(BF16)
    logits = (jnp.dot(h_hi, wr_ref[0], preferred_element_type=F32)
              + jnp.dot(h_hi, wr_ref[1], preferred_element_type=F32)
              + jnp.dot(h_lo, wr_ref[0], preferred_element_type=F32)) + br_ref[...]
    ne = logits.shape[1]
    lane = lax.broadcasted_iota(I32, logits.shape, 1)
    lane4 = lax.broadcasted_iota(I32, (tm, TOP_K), 1)
    g = logits
    mask = jnp.zeros(logits.shape, F32)
    vals, picks = [], []
    for _ in range(TOP_K):
        m = jnp.max(g, axis=1, keepdims=True)
        first = jnp.min(jnp.where(g == m, lane, ne), axis=1, keepdims=True)
        pick = lane == first
        mask = jnp.where(pick, 1.0, mask)
        g = jnp.where(pick, -jnp.inf, g)
        vals.append(m)
        picks.append(pick)
    ex = [jnp.exp(v - vals[0]) for v in vals]
    den = ex[0] + ex[1] + ex[2] + ex[3]
    gates = [e / den for e in ex]

    r = lax.broadcasted_iota(I32, (tm, tm), 0)
    c = lax.broadcasted_iota(I32, (tm, tm), 1)
    tril = jnp.where(c < r, 1.0, 0.0).astype(BF16)
    before = jnp.dot(tril, mask.astype(BF16), preferred_element_type=F32) + carry_scr[...]
    total = carry_scr[...] + jnp.sum(mask, axis=0, keepdims=True)
    carry_scr[...] = total
    cnt_ref[...] = jnp.broadcast_to(total, cnt_ref.shape)

    def pack4(cols):
        return jnp.where(lane4 == 0, cols[0],
                         jnp.where(lane4 == 1, cols[1], jnp.where(lane4 == 2, cols[2], cols[3])))

    wide = lax.broadcasted_iota(I32, (tm, LANES), 1)
    route = jnp.zeros((tm, LANES), F32)
    for k, pk in enumerate(picks):
        idx_k = jnp.sum(jnp.where(pk, lane, 0), axis=1, keepdims=True).astype(F32)
        rank_k = jnp.sum(jnp.where(pk, before, 0.0), axis=1, keepdims=True)
        route = jnp.where(wide == k, idx_k, jnp.where(wide == TOP_K + k, rank_k, route))
    route_ref[...] = route.T[0:2 * TOP_K, :].astype(I32)
    gate_ref[...] = pack4(gates)


def _outproj(mix_a, mix_b, x2, w_a, w_b, gpost, gt, gpre, sc, sh, w_router, b_router):
    s, d = x2.shape
    tm = ROW_TILE
    ne = w_router.shape[1]
    wa = mix_a.shape[1]
    w_hi = w_router.astype(BF16)
    w_router_hl = jnp.stack([w_hi, (w_router - w_hi.astype(F32)).astype(BF16)])
    row = lambda i: (i, 0)
    fix = lambda i: (0, 0)
    vec = pl.BlockSpec((1, d), fix)
    return pl.pallas_call(
        _outproj_kernel,
        out_shape=(jax.ShapeDtypeStruct((s, d), F32),
                   jax.ShapeDtypeStruct((s * SUBLANES, LANES), F32),
                   jax.ShapeDtypeStruct((2 * TOP_K, s), I32),
                   jax.ShapeDtypeStruct((s, TOP_K), F32),
                   jax.ShapeDtypeStruct((8, ne), F32)),
        grid=(s // tm,),
        in_specs=[pl.BlockSpec((tm, wa), row), pl.BlockSpec((tm, wa), row), pl.BlockSpec((tm, d), row),
                  pl.BlockSpec((wa, d), fix), pl.BlockSpec((wa, d), fix),
                  vec, vec, vec, vec, vec,
                  pl.BlockSpec((2, d, ne), lambda i: (0, 0, 0)), pl.BlockSpec((1, ne), fix)],
        out_specs=(pl.BlockSpec((tm, d), row), pl.BlockSpec((tm * SUBLANES, LANES), row),
                   pl.BlockSpec((2 * TOP_K, tm), lambda i: (0, i)),
                   pl.BlockSpec((tm, TOP_K), row), pl.BlockSpec((8, ne), fix)),
        scratch_shapes=[pltpu.VMEM((1, ne), F32)],
        compiler_params=_cparams(("arbitrary",)),
        name="outproj",
    )(mix_a, mix_b, x2, w_a, w_b, gpost, gt, gpre, sc, sh, w_router_hl, b_router)


def _dest_kernel(route_ref, pstart_ref, o_ref):
    route = route_ref[...]
    tm = route.shape[1]
    ne = pstart_ref.shape[0]
    expert = lax.broadcasted_iota(I32, (ne, tm), 0)
    for k in range(TOP_K):
        start = jnp.sum(jnp.where(expert == route[k:k + 1, :], pstart_ref[...], 0), axis=0, keepdims=True)
        o_ref[k:k + 1, :] = start + route[TOP_K + k:TOP_K + k + 1, :]


def _dest(route, pstart):
    s = route.shape[1]
    tm = min(4 * ROW_TILE, s)
    return pl.pallas_call(
        _dest_kernel,
        out_shape=jax.ShapeDtypeStruct((TOP_K, s), I32),
        grid=(s // tm,),
        in_specs=[pl.BlockSpec((2 * TOP_K, tm), lambda i: (0, i)),
                  pl.BlockSpec((N_EXPERTS, 1), lambda i: (0, 0))],
        out_specs=pl.BlockSpec((TOP_K, tm), lambda i: (0, i)),
        compiler_params=_cparams(("arbitrary",)),
        name="dest",
    )(route, pstart.reshape(-1, 1))


def _dispatch_kernel(pstart_ref, pblk_ref, nu_ref, dest_ref, h_ref, xs_ref,
                     zero_scr, sem, zsem):
    tm = h_ref.shape[0] // SUBLANES
    bm = zero_scr.shape[0] // SUBLANES
    n_blk = xs_ref.shape[0] // (bm * SUBLANES)

    rows = _tile_rows

    @pl.when(pl.program_id(0) == 0)
    def _():
        zero_scr[...] = jnp.zeros_like(zero_scr)

        def zero_copy(row0):
            return pltpu.make_async_copy(zero_scr, xs_ref.at[rows(row0, bm)], zsem)

        for phase in range(2):
            for e in range(N_EXPERTS):
                last = pstart_ref[e] + (pblk_ref[e] - 1) * bm
                tail = (n_blk - N_EXPERTS + e) * bm
                for cond, row0 in ((pblk_ref[e] > 0, last), (n_blk - N_EXPERTS + e >= nu_ref[0], tail)):
                    @pl.when(cond)
                    def _():
                        if phase == 0:
                            zero_copy(row0).start()
                        else:
                            zero_copy(row0).wait()

    def row_copy(r, k):
        dst = dest_ref[k, r]
        return pltpu.make_async_copy(h_ref.at[rows(r, 1)], xs_ref.at[rows(dst, 1)], sem)

    def start(r, carry):
        for k in range(TOP_K):
            row_copy(r, k).start(priority=k % 2)
        return carry

    lax.fori_loop(0, tm, start, 0, unroll=4)
    for k in range(TOP_K):
        pltpu.make_async_copy(h_ref, xs_ref.at[rows(0, tm)], sem).wait()


def _dispatch(pstart, pblk, n_used, dest, h2, n_rows):
    s = h2.shape[0] // SUBLANES
    tm = COMBINE_TILE
    grid_spec = pltpu.PrefetchScalarGridSpec(
        num_scalar_prefetch=3,
        grid=(s // tm,),
        in_specs=[pl.BlockSpec((TOP_K, tm), lambda i, *_: (0, i), memory_space=pltpu.SMEM),
                  pl.BlockSpec((tm * SUBLANES, LANES), lambda i, *_: (i, 0))],
        out_specs=pl.BlockSpec(memory_space=pl.ANY),
        scratch_shapes=[pltpu.VMEM((EXPERT_BLOCK * SUBLANES, LANES), F32), pltpu.SemaphoreType.DMA,
                        pltpu.SemaphoreType.DMA],
    )
    return pl.pallas_call(
        _dispatch_kernel,
        out_shape=jax.ShapeDtypeStruct((n_rows * SUBLANES, LANES), F32),
        grid_spec=grid_spec,
        compiler_params=_cparams(("arbitrary",)),
        name="dispatch",
    )(pstart, pblk, n_used, dest, h2)


def _experts_kernel(be_ref, nu_ref, xs_ref, wgu_ref, bgu_ref, wd_ref, bd_ref, y_ref,
                    wgu_bf, wd_bf):
    b = pl.program_id(0)
    d_exp = wd_ref.shape[1]
    prev = be_ref[jnp.maximum(b - 1, 0)]
    changed = (b == 0) | (be_ref[b] != prev)

    @pl.when((b < nu_ref[0]) & changed)
    def _():
        rows = LANES

        def cast_gu(c, carry):
            r0 = pl.multiple_of(c * rows, rows)
            wgu_bf[pl.ds(r0, rows), :] = wgu_ref[0, pl.ds(r0, rows), :].astype(BF16)
            return carry

        def cast_d(c, carry):
            r0 = pl.multiple_of(c * rows, rows)
            wd_bf[pl.ds(r0, rows), :] = wd_ref[0, pl.ds(r0, rows), :].astype(BF16)
            return carry

        lax.fori_loop(0, wgu_ref.shape[1] // rows, cast_gu, 0)
        lax.fori_loop(0, wd_ref.shape[1] // rows, cast_d, 0)

    @pl.when(b < nu_ref[0])
    def _():
        bm = xs_ref.shape[0] // SUBLANES
        xb = _load_token_tiles(xs_ref, bm).astype(BF16)
        hdn = jnp.dot(xb, wgu_bf[...], preferred_element_type=F32) + bgu_ref[0]
        x_glu = jnp.minimum(hdn[:, :d_exp], SWIGLU_LIMIT)
        x_lin = jnp.clip(hdn[:, d_exp:], -SWIGLU_LIMIT, SWIGLU_LIMIT)
        act = x_glu * jax.nn.sigmoid(SWIGLU_ALPHA * x_glu) * (x_lin + 1.0)
        _store_token_tiles(y_ref, jnp.dot(act.astype(BF16), wd_bf[...], preferred_element_type=F32)
                           + bd_ref[0])

    @pl.when(b >= nu_ref[0])
    def _():
        y_ref[...] = jnp.zeros_like(y_ref)


def _experts(block_e, n_used, xs, w_gate_up, b_gate_up, w_down, b_down):
    n_rows = xs.shape[0] // SUBLANES
    bm = EXPERT_BLOCK
    n_blk = n_rows // bm
    ne, d, two_de = w_gate_up.shape
    de = w_down.shape[1]
    assert d == SUBLANES * LANES

    def blk(b, be, nu):
        return jnp.minimum(b, nu[0] - 1)

    grid_spec = pltpu.PrefetchScalarGridSpec(
        num_scalar_prefetch=2,
        grid=(n_blk,),
        in_specs=[pl.BlockSpec((bm * SUBLANES, LANES), lambda b, be, nu: (blk(b, be, nu), 0)),
                  pl.BlockSpec((1, d, two_de), lambda b, be, nu: (be[blk(b, be, nu)], 0, 0)),
                  pl.BlockSpec((1, 1, two_de), lambda b, be, nu: (be[blk(b, be, nu)], 0, 0)),
                  pl.BlockSpec((1, de, d), lambda b, be, nu: (be[blk(b, be, nu)], 0, 0)),
                  pl.BlockSpec((1, 1, d), lambda b, be, nu: (be[blk(b, be, nu)], 0, 0))],
        out_specs=pl.BlockSpec((bm * SUBLANES, LANES), lambda b, be, nu: (b, 0)),
        scratch_shapes=[pltpu.VMEM((d, two_de), BF16), pltpu.VMEM((de, d), BF16)],
    )
    return pl.pallas_call(
        _experts_kernel,
        out_shape=jax.ShapeDtypeStruct((n_rows * SUBLANES, LANES), F32),
        grid_spec=grid_spec,
        compiler_params=_cparams(("arbitrary",)),
        name="experts",
    )(block_e, n_used, xs, w_gate_up, b_gate_up.reshape(ne, 1, two_de), w_down, b_down.reshape(ne, 1, d))


def _combine_kernel(dest_ref, dest_next_ref, y_ref, gate_ref, x1_ref, gt_ref, gpost_ref, o_ref, buf, sem):
    i = pl.program_id(0)
    tm = x1_ref.shape[0]
    cur = i % 2

    def gather(d_ref, slot):
        def start(r, carry):
            for k in range(TOP_K):
                pltpu.make_async_copy(y_ref.at[_tile_rows(d_ref[k, r], 1)],
                                      buf.at[slot, k, _tile_rows(r, 1)], sem.at[slot]).start(priority=k % 2)
            return carry

        lax.fori_loop(0, tm, start, 0, unroll=4)

    @pl.when(i == 0)
    def _():
        gather(dest_ref, 0)

    @pl.when(i + 1 < pl.num_programs(0))
    def _():
        gather(dest_next_ref, 1 - cur)

    for k in range(TOP_K):
        pltpu.make_async_copy(y_ref.at[_tile_rows(0, tm)], buf.at[cur, k], sem.at[cur]).wait()

    gate = gate_ref[...]
    acc = gate[:, 0:1] * _load_token_tiles(buf.at[cur, 0], tm)
    for k in range(1, TOP_K):
        acc = acc + gate[:, k:k + 1] * _load_token_tiles(buf.at[cur, k], tm)
    o_ref[...] = x1_ref[...] + gt_ref[...] * (_rms(acc) * gpost_ref[...])


def _combine(dest, y, gate4, x1, gt, gpost):
    s, d = x1.shape
    tm = COMBINE_TILE // 2
    n = s // tm
    return pl.pallas_call(
        _combine_kernel,
        out_shape=jax.ShapeDtypeStruct((s, d), F32),
        grid=(n,),
        in_specs=[pl.BlockSpec((TOP_K, tm), lambda i: (0, i), memory_space=pltpu.SMEM),
                  pl.BlockSpec((TOP_K, tm), lambda i: (0, jnp.minimum(i + 1, n - 1)), memory_space=pltpu.SMEM),
                  pl.BlockSpec(memory_space=pl.ANY),
                  pl.BlockSpec((tm, TOP_K), lambda i: (i, 0)),
                  pl.BlockSpec((tm, d), lambda i: (i, 0)),
                  pl.BlockSpec((1, d), lambda i: (0, 0)),
                  pl.BlockSpec((1, d), lambda i: (0, 0))],
        out_specs=pl.BlockSpec((tm, d), lambda i: (i, 0)),
        scratch_shapes=[pltpu.VMEM((2, TOP_K, tm * SUBLANES, LANES), F32), pltpu.SemaphoreType.DMA((2,))],
        compiler_params=_cparams(("arbitrary",)),
        name="combine",
    )(dest, dest, y, gate4, x1, gt, gpost)


def _layer(x2, mod, g_pre_mix, g_post_mix, w_in, b_forget, rel_bias, w_out,
           g_pre_ffn, g_post_ffn, w_router, b_router, w_gate_up, b_gate_up, w_down, b_down):
    s, d = x2.shape
    sh_m, sc_m, gt_m, sh_f, sc_f, gt_f = [mod[:, k * d:(k + 1) * d] for k in range(6)]
    n_qkv = 3 * (N_HEADS_FOX + N_HEADS_MOBA) * HEAD_DIM
    fox_w = N_HEADS_FOX * HEAD_DIM

    w_qkv = w_in[:, :n_qkv].astype(BF16)
    w_f = w_in[:, n_qkv:].T
    qkv, cum, sel, nrm, tr, nr = _inproj(x2, g_pre_mix.reshape(1, d), sc_m, sh_m, w_qkv, w_f,
                                         b_forget.reshape(-1, 1))
    y_a = _fox(qkv, tr, nr, cum, nrm)
    y_b = _moba(qkv, tr, sel, rel_bias)

    w_out_bf = w_out.astype(BF16)
    x1, h2, route, gate4, cnt = _outproj(
        y_a, y_b, x2, w_out_bf[:fox_w], w_out_bf[fox_w:], g_post_mix.reshape(1, d), gt_m,
        g_pre_ffn.reshape(1, d), sc_f, sh_f, w_router, b_router.reshape(1, -1))

    bm = EXPERT_BLOCK
    counts = cnt[0].astype(I32)
    pblk = (counts + bm - 1) // bm
    pend_blk = jnp.cumsum(pblk)
    pstart = ((pend_blk - pblk) * bm).astype(I32)
    n_rows = s * TOP_K + N_EXPERTS * bm
    n_blk = n_rows // bm
    block_e = jnp.minimum(jnp.sum(pend_blk[None, :] <= jnp.arange(n_blk)[:, None], axis=1),
                          N_EXPERTS - 1).astype(I32)
    n_used = pend_blk[-1:].astype(I32)

    dest = _dest(route, pstart)
    xs = _dispatch(pstart, pblk.astype(I32), n_used, dest, h2, n_rows)
    y = _experts(block_e, n_used, xs, w_gate_up, b_gate_up, w_down, b_down)
    return _combine(dest, y, gate4, x1, gt_f, g_post_ffn.reshape(1, d))


def kernel(x, c, w_ada, b_ada, g_pre_mix, g_post_mix, w_in, b_forget, rel_bias, w_out, g_pre_ffn, g_post_ffn, w_router, b_router, w_gate_up, b_gate_up, w_down, b_down):
    bsz, s, d = x.shape
    depth = w_ada.shape[0]
    outs = []
    for bi in range(bsz):
        x2 = x[bi]
        for l in range(depth):
            mod = _adaln(c[bi:bi + 1], w_ada[l], b_ada[l])
            x2 = _layer(x2, mod, g_pre_mix[l], g_post_mix[l], w_in[l], b_forget[l], rel_bias, w_out[l],
                        g_pre_ffn[l], g_post_ffn[l], w_router[l], b_router[l], w_gate_up[l], b_gate_up[l],
                        w_down[l], b_down[l])
        outs.append(x2)
    return outs[0].reshape(1, s, d) if bsz == 1 else jnp.stack(outs)
```

```python
import math

import numpy as np
import jax
import jax.numpy as jnp
from jax import lax
from jax.experimental import pallas as pl
from jax.experimental.pallas import tpu as pltpu

F32 = jnp.float32
BF16 = jnp.bfloat16
I32 = jnp.int32

HEAD_DIM = 64
N_HEADS_FOX = 8
N_HEADS_MOBA = 8
PAIR = 2 * HEAD_DIM
MOBA_BLOCK = 256
MOBA_TOPK = 3
NUM_BUCKETS = 32
MAX_DISTANCE = 128
N_EXPERTS = 32
TOP_K = 4
SWIGLU_LIMIT = 7.0
SWIGLU_ALPHA = 1.702
RMS_EPS = 1e-6
NEG = -(2.0 ** 100)
M_INIT = -(2.0 ** 99)
LOG2E = math.log2(math.e)
SUM_ROWS = 16
SUBLANES = 8
LANES = 128
EXP_UNDERFLOW = 90.0
VMEM_LIMIT = 56 * 1024 * 1024

ROW_TILE = 512
FOX_TILE = 256
EXPERT_BLOCK = 512
EXPERT_CHUNK = 256
COMBINE_TILE = 1024
MOBA_LAG = 1
MOBA_UNROLL = 4

NT_DIMS = (((1,), (1,)), ((), ()))


def _cparams(sem):
    return pltpu.CompilerParams(dimension_semantics=sem, vmem_limit_bytes=VMEM_LIMIT)


def _rms(x):
    return x * lax.rsqrt(jnp.mean(x * x, axis=-1, keepdims=True) + RMS_EPS)


def _adaln_kernel(c_ref, w_ref, b_ref, o_ref):
    c = c_ref[...]
    cond = c * jax.nn.sigmoid(c)
    o_ref[...] = jnp.dot(cond, w_ref[...], preferred_element_type=F32,
                         precision=lax.Precision.HIGHEST) + b_ref[...]


def _adaln(c, w_ada, b_ada):
    d = c.shape[-1]
    n = w_ada.shape[-1]
    c8 = jnp.broadcast_to(c.reshape(1, d), (8, d))
    out = pl.pallas_call(
        _adaln_kernel,
        out_shape=jax.ShapeDtypeStruct((8, n), F32),
        grid=(n // d,),
        in_specs=[pl.BlockSpec((8, d), lambda j: (0, 0)),
                  pl.BlockSpec((d, d), lambda j: (0, j)),
                  pl.BlockSpec((1, d), lambda j: (0, j))],
        out_specs=pl.BlockSpec((8, d), lambda j: (0, j)),
        compiler_params=_cparams(("arbitrary",)),
        name="adaln",
    )(c8, w_ada, b_ada.reshape(1, n))
    return out[0:1]


def _inproj_kernel(x_ref, g_ref, sc_ref, sh_ref, w_ref, wf_ref, bf_ref,
                   qkv_ref, cum_ref, sel_ref, nrm_ref, tr_ref, nr_ref, km_scr, carry_scr):
    i = pl.program_id(0)
    tm = x_ref.shape[0]
    nblk = km_scr.shape[0]

    @pl.when(i == 0)
    def _():
        km_scr[...] = jnp.zeros_like(km_scr)
        carry_scr[...] = jnp.zeros_like(carry_scr)

    x = x_ref[...]
    h = _rms(x) * g_ref[...] * (1.0 + sc_ref[...]) + sh_ref[...]
    hb = h.astype(BF16)

    width = N_HEADS_FOX * HEAD_DIM
    hsel = jnp.where(lax.broadcasted_iota(I32, (width, N_HEADS_FOX), 0) // HEAD_DIM
                     == lax.broadcasted_iota(I32, (width, N_HEADS_FOX), 1), 1.0, 0.0)
    kb = None
    tr_slot = {3: 0, 5: 1, 0: 2, 2: 3}
    qbt = None
    for c in range(6):
        pc = jnp.dot(hb, w_ref[:, c * width:(c + 1) * width], preferred_element_type=F32)
        if c in tr_slot:
            n = tr_slot[c]
            pct = pc.T
            if c == 3:
                qbt = pct
            if c == 0 or c == 3:
                pct = pct * (LOG2E * HEAD_DIM ** -0.5)
            tr_ref[n * width:(n + 1) * width, :] = pct.astype(BF16)
        if c == 1 or c == 4:
            qkv_ref[:, (c // 3) * width:(c // 3 + 1) * width] = pc.astype(BF16)
        if c < 2:
            sq = (pc * (HEAD_DIM ** -0.5) if c == 0 else pc).astype(BF16).astype(F32)
            n2 = jnp.dot((sq * sq).astype(BF16), hsel.astype(BF16), preferred_element_type=F32)
            nrm_ref[0, c:c + 1, :] = jnp.max(n2, axis=0, keepdims=True)
        if c == 4:
            kb = pc

    ft = lax.dot_general(wf_ref[...].astype(BF16), hb, NT_DIMS, preferred_element_type=F32)
    z = ft + bf_ref[...]
    logf = -(jnp.maximum(-z, 0.0) + jnp.log1p(jnp.exp(-jnp.abs(z))))
    lane = lax.broadcasted_iota(I32, logf.shape, 1)
    cs = logf
    sh = 1
    while sh < tm:
        cs = cs + jnp.where(lane >= sh, pltpu.roll(cs, sh, axis=1), 0.0)
        sh *= 2
    base = jnp.zeros_like(cs)
    for b in range(1, tm // FOX_TILE):
        base = jnp.where(lane >= b * FOX_TILE, cs[:, b * FOX_TILE - 1:b * FOX_TILE], base)
    nr_t = (-LOG2E * (cs - base)).T
    for pp in range(nr_ref.shape[0]):
        nr_ref[pp] = nr_t[:, 2 * pp:2 * pp + 2]
    cs = cs + carry_scr[...]
    cum_ref[...] = cs
    carry_scr[...] = cs[:, tm - 1:tm]

    nb_tile = tm // MOBA_BLOCK
    for b in range(nb_tile):
        kmean = jnp.sum(kb[b * MOBA_BLOCK:(b + 1) * MOBA_BLOCK], axis=0, keepdims=True) * (1.0 / MOBA_BLOCK)
        km_scr[pl.ds(i * nb_tile + b, 1), :] = kmean

    km = km_scr[...]
    blk = lax.broadcasted_iota(I32, (nblk, tm), 0)
    col = lax.broadcasted_iota(I32, (nblk, tm), 1)
    own = i * nb_tile + col // MOBA_BLOCK
    for hd in range(N_HEADS_MOBA):
        hs = slice(hd * HEAD_DIM, (hd + 1) * HEAD_DIM)
        g = jnp.dot(km[:, hs], qbt[hs, :], preferred_element_type=F32,
                    precision=lax.Precision.HIGHEST)
        g = jnp.where(blk < own, g, -jnp.inf)
        sel = jnp.zeros(g.shape, dtype=jnp.bool_)
        for _ in range(MOBA_TOPK):
            m = jnp.max(g, axis=0, keepdims=True)
            first = jnp.min(jnp.where(g == m, blk, nblk), axis=0, keepdims=True)
            pick = (blk == first) & (m > -jnp.inf)
            sel = sel | pick
            g = jnp.where(pick, -jnp.inf, g)
        sel_ref[hd] = jnp.where(sel, 0.0, NEG)


def _inproj(x2, g, sc, sh, w_qkv, w_f, b_f):
    s, d = x2.shape
    tm = ROW_TILE
    nblk = s // MOBA_BLOCK
    n = w_qkv.shape[1]
    width = N_HEADS_FOX * HEAD_DIM
    n_rows_out = 2 * width
    n_tr = 4 * width
    assert tm % FOX_TILE == 0 and N_HEADS_FOX == N_HEADS_MOBA
    return pl.pallas_call(
        _inproj_kernel,
        out_shape=(jax.ShapeDtypeStruct((s, n_rows_out), BF16),
                   jax.ShapeDtypeStruct((N_HEADS_FOX, s), F32),
                   jax.ShapeDtypeStruct((N_HEADS_MOBA, nblk, s), F32),
                   jax.ShapeDtypeStruct((s // tm, 2, N_HEADS_FOX), F32),
                   jax.ShapeDtypeStruct((n_tr, s), BF16),
                   jax.ShapeDtypeStruct((N_HEADS_FOX // 2, s, 2), F32)),
        grid=(s // tm,),
        in_specs=[pl.BlockSpec((tm, d), lambda i: (i, 0)),
                  pl.BlockSpec((1, d), lambda i: (0, 0)),
                  pl.BlockSpec((1, d), lambda i: (0, 0)),
                  pl.BlockSpec((1, d), lambda i: (0, 0)),
                  pl.BlockSpec((d, n), lambda i: (0, 0)),
                  pl.BlockSpec((N_HEADS_FOX, d), lambda i: (0, 0)),
                  pl.BlockSpec((N_HEADS_FOX, 1), lambda i: (0, 0))],
        out_specs=(pl.BlockSpec((tm, n_rows_out), lambda i: (i, 0)),
                   pl.BlockSpec((N_HEADS_FOX, tm), lambda i: (0, i)),
                   pl.BlockSpec((N_HEADS_MOBA, nblk, tm), lambda i: (0, 0, i)),
                   pl.BlockSpec((1, 2, N_HEADS_FOX), lambda i: (i, 0, 0)),
                   pl.BlockSpec((n_tr, tm), lambda i: (0, i)),
                   pl.BlockSpec((N_HEADS_FOX // 2, tm, 2), lambda i: (0, i, 0))),
        scratch_shapes=[pltpu.VMEM((nblk, N_HEADS_MOBA * HEAD_DIM), F32),
                        pltpu.VMEM((N_HEADS_FOX, 1), F32)],
        compiler_params=_cparams(("arbitrary",)),
        name="inproj",
    )(x2, g, sc, sh, w_qkv, w_f, b_f)


def _fox_kernel(jlo_ref, cp_ref, qt_ref, k_ref, vt_ref, nr_ref, o_ref, m_ref, a_ref):
    p = pl.program_id(0)
    g = pl.program_id(1)
    j_first = jlo_ref[p * pl.num_programs(1) + g]
    tq = qt_ref.shape[1]
    t = tq // 2

    qt = qt_ref[...]
    top = lax.broadcasted_iota(I32, qt.shape, 0) < HEAD_DIM
    zq = jnp.zeros_like(qt)
    qth = (jnp.where(top, qt, zq), jnp.where(top, zq, qt))
    m_ref[...] = jnp.full(m_ref.shape, M_INIT, F32)
    a_ref[...] = jnp.zeros(a_ref.shape, F32)
    er = lax.broadcasted_iota(I32, (SUM_ROWS, 2 * t), 0)
    ec = lax.broadcasted_iota(I32, (SUM_ROWS, 2 * t), 1)
    ones_rows = jnp.where(((er == 0) & (ec < t)) | ((er == 1) & (ec >= t)), 1.0, 0.0).astype(BF16)
    arow = lax.broadcasted_iota(I32, a_ref.shape, 0)
    head0_rows = (arow < HEAD_DIM) | (arow == PAIR)
    vtop = lax.broadcasted_iota(I32, (PAIR, t), 0) < HEAD_DIM
    in_a = lax.broadcasted_iota(I32, (1, tq), 1) < t
    krow = lax.broadcasted_iota(I32, (t, tq), 0)
    qcol = lax.broadcasted_iota(I32, (t, tq), 1)
    future_a = (qcol < t) & (krow > qcol)
    future_b = (qcol >= t) & (krow > qcol - t)

    def scores(j, future):
        k0 = pl.multiple_of(j * t, t)
        kt = k_ref[pl.ds(k0, t), :]
        out = []
        for hd in range(2):
            s = jnp.dot(kt, qth[hd], preferred_element_type=F32) + nr_ref[0, pl.ds(k0, t), hd:hd + 1]
            out.append(s if future is None else jnp.where(future, NEG, s))
        return out

    def shifts(j, a_on, b_on):
        rows = []
        for hd in range(2):
            h = 2 * p + hd
            sa = LOG2E * (cp_ref[h, 2 * g] - cp_ref[h, j])
            sb = LOG2E * (cp_ref[h, 2 * g + 1] - cp_ref[h, j])
            rows.append(jnp.where(in_a, jnp.where(a_on, sa, NEG), jnp.where(b_on, sb, NEG)))
        return rows

    def softmax(ss, shift_rows):
        ps, alphas = [], []
        for hd in range(2):
            m_old = m_ref[hd]
            m_new = jnp.maximum(m_old, jnp.max(ss[hd], axis=0, keepdims=True) + shift_rows[hd])
            alphas.append(jnp.exp2(m_old - m_new))
            m_ref[hd] = m_new
            ps.append(jnp.exp2(ss[hd] - (m_new - shift_rows[hd])).astype(BF16))
        return ps, alphas

    def accumulate(j, ps, alphas):
        k0 = pl.multiple_of(j * t, t)
        vt = vt_ref[:, pl.ds(k0, t)]
        zv = jnp.zeros_like(vt)
        vcat = jnp.concatenate([jnp.where(vtop, vt, zv), jnp.where(vtop, zv, vt)], axis=1)
        vcat = jnp.concatenate([vcat, ones_rows], axis=0)
        pcat = jnp.concatenate(ps, axis=0)
        a_ref[...] = (a_ref[...] * jnp.where(head0_rows, alphas[0], alphas[1])
                      + jnp.dot(vcat, pcat, preferred_element_type=F32))

    def body(j, carry):
        accumulate(j, *softmax(scores(j, None), shifts(j, True, True)))
        return carry

    lax.fori_loop(j_first, 2 * g - 1, body, 0)
    j1 = jnp.maximum(2 * g - 1, 0)
    j2 = 2 * g
    j3 = 2 * g + 1
    ss1 = scores(j1, None)
    ss2 = scores(j2, future_a)
    ss3 = scores(j3, future_b)
    w1 = softmax(ss1, shifts(j1, g >= 1, g >= 1))
    w2 = softmax(ss2, shifts(j2, True, True))
    accumulate(j1, *w1)
    w3 = softmax(ss3, shifts(j3, False, True))
    accumulate(j2, *w2)
    accumulate(j3, *w3)
    out_t =a_ref[0:PAIR, :] / jnp.where(top, a_ref[PAIR:PAIR + 1, :], a_ref[PAIR + 1:PAIR + 2, :])
    o_ref[...] = out_t.T.astype(o_ref.dtype)


def _fox_first_tile(cum, nrm, t):
    cend = cum[:, t - 1::t]
    nt = cend.shape[1]
    cprev = jnp.concatenate([jnp.zeros((cend.shape[0], 1), F32), cend[:, :-1]], axis=1)
    rep = nt // nrm.shape[0]
    qn = jnp.repeat(jnp.sqrt(nrm[:, 0, :]).T, rep, axis=1)
    kn = jnp.repeat(jnp.sqrt(nrm[:, 1, :]).T, rep, axis=1)
    gap = (1.02 * qn[:, :, None] * (kn[:, None, :] + kn[:, :, None])
           + cprev[:, :, None] - cend[:, None, :])
    jj = jnp.arange(nt)[None, None, :]
    ii = jnp.arange(nt)[None, :, None]
    needed = (jj < ii) & jnp.logical_not(gap <= -EXP_UNDERFLOW)
    needed = needed[0::2] | needed[1::2]
    first = jnp.min(jnp.where(needed, jj, ii), axis=2)
    return first.reshape(-1).astype(I32)


def _fox(qkv, tr, nr, cum, nrm):
    s = qkv.shape[0]
    t = FOX_TILE
    npair = N_HEADS_FOX // 2
    tq = 2 * t
    jlo = _fox_first_tile(cum, nrm, t).reshape(npair, s // tq, 2).min(axis=2).reshape(-1)
    cend = cum[:, t - 1::t]
    cprev = jnp.concatenate([jnp.zeros((cend.shape[0], 1), F32), cend[:, :-1]], axis=1)
    q_rows = 2 * (N_HEADS_MOBA // 2)
    grid_spec = pltpu.PrefetchScalarGridSpec(
        num_scalar_prefetch=1,
        grid=(npair, s // tq),
        in_specs=[pl.BlockSpec(memory_space=pltpu.SMEM),
                  pl.BlockSpec((PAIR, tq), lambda p, i, jl: (q_rows + p, i)),
                  pl.BlockSpec((s, PAIR), lambda p, i, jl: (0, p)),
                  pl.BlockSpec((PAIR, s), lambda p, i, jl: (q_rows + npair + p, 0)),
                  pl.BlockSpec((1, s, 2), lambda p, i, jl: (p, 0, 0))],
        out_specs=pl.BlockSpec((tq, PAIR), lambda p, i, jl: (i, p)),
        scratch_shapes=[pltpu.VMEM((2, 1, tq), F32), pltpu.VMEM((PAIR + SUM_ROWS, tq), F32)],
    )
    return pl.pallas_call(
        _fox_kernel,
        out_shape=jax.ShapeDtypeStruct((s, npair * PAIR), BF16),
        grid_spec=grid_spec,
        compiler_params=_cparams(("arbitrary", "arbitrary")),
        name="fox",
    )(jlo, cprev, tr, qkv, tr, nr)


def _t5_bucket_np(dist):
    dist = np.maximum(dist, 0)
    max_exact = NUM_BUCKETS // 2
    d = np.maximum(dist, 1).astype(np.float32)
    large = max_exact + (np.log(d / np.float32(max_exact)) / np.float32(math.log(MAX_DISTANCE / max_exact))
                         * np.float32(NUM_BUCKETS - max_exact)).astype(np.int32)
    large = np.minimum(large, NUM_BUCKETS - 1)
    return np.where(dist < max_exact, dist, large).astype(np.int32)


def _moba_kernel(rb_ref, qt_ref, k_ref, vt_ref, sel_ref, bkt_ref, o_ref,
                 m_ref, a_ref, bias_scr, *bufs):
    s_bufs = bufs[0:MOBA_UNROLL]
    p_bufs = bufs[MOBA_UNROLL:2 * MOBA_UNROLL]
    al_bufs = bufs[2 * MOBA_UNROLL:3 * MOBA_UNROLL]
    p = pl.program_id(0)
    g = pl.program_id(1)
    tq = qt_ref.shape[1]
    t = tq // 2

    @pl.when(g == 0)
    def _():
        r = lax.broadcasted_iota(I32, (t, t), 0)
        c = lax.broadcasted_iota(I32, (t, t), 1)
        zero = jnp.zeros((t, t), F32)
        for hd in range(2):
            h = 2 * p + hd
            far = rb_ref[(NUM_BUCKETS - 1) * N_HEADS_MOBA + h]
            tiles = []
            for w in range(2):
                bkt = bkt_ref[w]
                acc = jnp.zeros(bkt.shape, F32)
                for kk in range(NUM_BUCKETS):
                    acc = acc + jnp.where(bkt == kk, rb_ref[kk * N_HEADS_MOBA + h], 0.0)
                tiles.append((acc - far) * LOG2E)
            prev_t = tiles[0]
            own_t = jnp.where(r <= c, tiles[1], NEG)
            bias_scr[hd, 0] = jnp.concatenate([prev_t, zero], axis=1)
            bias_scr[hd, 1] = jnp.concatenate([own_t, prev_t], axis=1)
            bias_scr[hd, 2] = jnp.concatenate([zero, own_t], axis=1)

    qt = qt_ref[...]
    top = lax.broadcasted_iota(I32, qt.shape, 0) < HEAD_DIM
    zq = jnp.zeros_like(qt)
    qth = (jnp.where(top, qt, zq), jnp.where(top, zq, qt))
    m_ref[...] = jnp.full(m_ref.shape, M_INIT, F32)
    a_ref[...] = jnp.zeros(a_ref.shape, F32)

    er = lax.broadcasted_iota(I32, (SUM_ROWS, 2 * t), 0)
    ec = lax.broadcasted_iota(I32, (SUM_ROWS, 2 * t), 1)
    ones_rows = jnp.where(((er == 0) & (ec < t)) | ((er == 1) & (ec >= t)), 1.0, 0.0).astype(BF16)
    arow = lax.broadcasted_iota(I32, a_ref.shape, 0)
    head0_rows = (arow < HEAD_DIM) | (arow == PAIR)
    vtop = lax.broadcasted_iota(I32, (PAIR, t), 0) < HEAD_DIM
    in_a = lax.broadcasted_iota(I32, (1, tq), 1) < t

    n_far = jnp.maximum(2 * g - 1, 0)

    n_key_tiles = k_ref.shape[0] // t

    def produce(j, s_buf):
        j = jnp.minimum(j, n_key_tiles - 1)
        k0 = pl.multiple_of(j * t, t)
        kt = k_ref[pl.ds(k0, t), :]
        for hd in range(2):
            s_buf[hd] = jnp.dot(kt, qth[hd], preferred_element_type=F32)

    def softmax(s_buf, p_buf, al_buf, selrows, w):
        for hd in range(2):
            s = s_buf[hd]
            if w is not None:
                s = s + bias_scr[hd, w]
            smax = jnp.max(s, axis=0, keepdims=True)
            m_old = m_ref[hd]
            m_new = jnp.maximum(m_old, smax + selrows[hd])
            shift = m_new - selrows[hd]
            al_buf[hd] = jnp.exp2(m_old - m_new)
            m_ref[hd] = m_new
            p_buf[hd * t:(hd + 1) * t, :] = jnp.exp2(s - shift).astype(BF16)

    def far_rows(j):
        return [jnp.where(j < n_far, sel_ref[hd, pl.ds(j, 1), :], NEG) for hd in range(2)]

    def accumulate(j, p_buf, al_buf):
        k0 = pl.multiple_of(j * t, t)
        vt = vt_ref[:, pl.ds(k0, t)]
        zv = jnp.zeros_like(vt)
        vcat = jnp.concatenate([jnp.where(vtop, vt, zv), jnp.where(vtop, zv, vt)], axis=1)
        vcat = jnp.concatenate([vcat, ones_rows], axis=0)
        a_ref[...] = (a_ref[...] * jnp.where(head0_rows, al_buf[0], al_buf[1])
                      + jnp.dot(vcat, p_buf[...], preferred_element_type=F32))

    un = MOBA_UNROLL
    lag = MOBA_LAG
    for n in range(un - lag, un):
        p_bufs[n][...] = jnp.zeros_like(p_bufs[n])
        al_bufs[n][...] = jnp.ones_like(al_bufs[n])
    produce(0, s_bufs[0])

    def body(u, carry):
        j0 = un * u
        for n in range(un):
            accumulate(jnp.maximum(j0 + n - lag, 0), p_bufs[(n - lag) % un], al_bufs[(n - lag) % un])
            softmax(s_bufs[n], p_bufs[n], al_bufs[n], far_rows(j0 + n), None)
            produce(j0 + n + 1, s_bufs[(n + 1) % un])
        return carry

    n_trips = (n_far + un - 1) // un
    lax.fori_loop(0, n_trips, body, 0)

    j1 = jnp.maximum(2 * g - 1, 0)
    j2 = 2 * g
    j3 = 2 * g + 1
    rows1 = [jnp.where(g >= 1, sel_ref[hd, pl.ds(j1, 1), :], NEG) for hd in range(2)]
    rows2 = [jnp.where(in_a, 0.0, sel_ref[hd, pl.ds(j2, 1), :]) for hd in range(2)]
    rows3 = [jnp.where(in_a, NEG, 0.0)] * 2
    produce(j1, s_bufs[0])
    produce(j2, s_bufs[1])
    for n in range(un - lag, un):
        accumulate(jnp.maximum(un * n_trips - un + n, 0), p_bufs[n], al_bufs[n])
    softmax(s_bufs[0], p_bufs[0], al_bufs[0], rows1, 0)
    produce(j3, s_bufs[0])
    softmax(s_bufs[1], p_bufs[1], al_bufs[1], rows2, 1)
    accumulate(j1, p_bufs[0], al_bufs[0])
    softmax(s_bufs[0], p_bufs[2], al_bufs[2], rows3, 2)
    accumulate(j2, p_bufs[1], al_bufs[1])
    accumulate(j3, p_bufs[2], al_bufs[2])
    out_t = a_ref[0:PAIR, :] / jnp.where(top, a_ref[PAIR:PAIR + 1, :], a_ref[PAIR + 1:PAIR + 2, :])
    o_ref[...] = out_t.T.astype(o_ref.dtype)


def _moba(qkv, tr, sel, rel_bias):
    s = qkv.shape[0]
    t = MOBA_BLOCK
    npair = N_HEADS_MOBA // 2
    nblk = s // t
    kcol = N_HEADS_FOX // 2
    a = np.arange(t)[None, :]
    b = np.arange(t)[:, None]
    bkt = jnp.asarray(np.stack([_t5_bucket_np(t + a - b), _t5_bucket_np(a - b)]))
    tq = 2 * t
    grid_spec = pltpu.PrefetchScalarGridSpec(
        num_scalar_prefetch=1,
        grid=(npair, s // tq),
        in_specs=[pl.BlockSpec((PAIR, tq), lambda p, i, rb: (p, i)),
                  pl.BlockSpec((s, PAIR), lambda p, i, rb: (0, kcol + p)),
                  pl.BlockSpec((PAIR, s), lambda p, i, rb: (npair + p, 0)),
                  pl.BlockSpec((2, nblk, tq), lambda p, i, rb: (p, 0, i)),
                  pl.BlockSpec((2, t, t), lambda p, i, rb: (0, 0, 0))],
        out_specs=pl.BlockSpec((tq, PAIR), lambda p, i, rb: (i, p)),
        scratch_shapes=[pltpu.VMEM((2, 1, tq), F32),
                        pltpu.VMEM((PAIR + SUM_ROWS, tq), F32), pltpu.VMEM((2, 3, t, tq), F32)]
        + [pltpu.VMEM((2, t, tq), F32)] * MOBA_UNROLL
        + [pltpu.VMEM((2 * t, tq), BF16)] * MOBA_UNROLL
        + [pltpu.VMEM((2, 1, tq), F32)] * MOBA_UNROLL,
    )
    return pl.pallas_call(
        _moba_kernel,
        out_shape=jax.ShapeDtypeStruct((s, npair * PAIR), BF16),
        grid_spec=grid_spec,
        compiler_params=_cparams(("arbitrary", "arbitrary")),
        name="moba",
    )(rel_bias.reshape(-1), tr, qkv, tr, sel, bkt)


def _store_token_tiles(ref, val):
    n = val.shape[0]
    for c in range(SUBLANES):
        ref[pl.ds(c, n, stride=SUBLANES), :] = val[:, c * LANES:(c + 1) * LANES]


def _load_token_tiles(ref, n):
    return jnp.concatenate([ref[pl.ds(c, n, stride=SUBLANES), :] for c in range(SUBLANES)], axis=1)


def _tile_rows(r0, n):
    start = r0 * SUBLANES
    if not isinstance(start, int):
        start = pl.multiple_of(start, SUBLANES)
    return pl.ds(start, n * SUBLANES)


def _outproj_kernel(ya_ref, yb_ref, x_ref, wa_ref, wb_ref, gpost_ref, gt_ref, gpre_ref,
                    sc_ref, sh_ref, wr_ref, br_ref,
                    x1_ref, h2_ref, route_ref, gate_ref, cnt_ref, carry_scr):
    i = pl.program_id(0)
    tm = x_ref.shape[0]

    @pl.when(i == 0)
    def _():
        carry_scr[...] = jnp.zeros_like(carry_scr)

    y = (jnp.dot(ya_ref[...], wa_ref[...], preferred_element_type=F32)
         + jnp.dot(yb_ref[...], wb_ref[...], preferred_element_type=F32))
    x1 = x_ref[...] + gt_ref[...] * (_rms(y) * gpost_ref[...])
    x1_ref[...] = x1
    h2 = _rms(x1) * gpre_ref[...] * (1.0 + sc_ref[...]) + sh_ref[...]
    _store_token_tiles(h2_ref, h2)

    h_hi = h2.astype(BF16)
    h_lo = (h2 - h_hi.astype(F32)).astype(BF16)
    logits = (jnp.dot(h_hi, wr_ref[0], preferred_element_type=F32)
              + jnp.dot(h_hi, wr_ref[1], preferred_element_type=F32)
              + jnp.dot(h_lo, wr_ref[0], preferred_element_type=F32)) + br_ref[...]
    ne = logits.shape[1]
    lane = lax.broadcasted_iota(I32, logits.shape, 1)
    lane4 = lax.broadcasted_iota(I32, (tm, TOP_K), 1)
    g = logits
    mask = jnp.zeros(logits.shape, F32)
    vals, picks = [], []
    for _ in range(TOP_K):
        m = jnp.max(g, axis=1, keepdims=True)
        first = jnp.min(jnp.where(g == m, lane, ne), axis=1, keepdims=True)
        pick = lane == first
        mask = jnp.where(pick, 1.0, mask)
        g = jnp.where(pick, -jnp.inf, g)
        vals.append(m)
        picks.append(pick)
    ex = [jnp.exp(v - vals[0]) for v in vals]
    den = ex[0] + ex[1] + ex[2] + ex[3]
    gates = [e / den for e in ex]

    r = lax.broadcasted_iota(I32, (tm, tm), 0)
    c = lax.broadcasted_iota(I32, (tm, tm), 1)
    tril = jnp.where(c < r, 1.0, 0.0).astype(BF16)
    before = jnp.dot(tril, mask.astype(BF16), preferred_element_type=F32) + carry_scr[...]
    total = carry_scr[...] + jnp.sum(mask, axis=0, keepdims=True)
    carry_scr[...] = total
    cnt_ref[...] = jnp.broadcast_to(total, cnt_ref.shape)

    def pack4(cols):
        return jnp.where(lane4 == 0, cols[0],
                         jnp.where(lane4 == 1, cols[1], jnp.where(lane4 == 2, cols[2], cols[3])))

    wide = lax.broadcasted_iota(I32, (tm, LANES), 1)
    route = jnp.zeros((tm, LANES), F32)
    for k, pk in enumerate(picks):
        idx_k = jnp.sum(jnp.where(pk, lane, 0), axis=1, keepdims=True).astype(F32)
        rank_k = jnp.sum(jnp.where(pk, before, 0.0), axis=1, keepdims=True)
        route = jnp.where(wide == k, idx_k, jnp.where(wide == TOP_K + k, rank_k, route))
    route_ref[...] = route.T[0:2 * TOP_K, :].astype(I32)
    gate_ref[...] = pack4(gates)


def _outproj(mix_a, mix_b, x2, w_a, w_b, gpost, gt, gpre, sc, sh, w_router, b_router):
    s, d = x2.shape
    tm = ROW_TILE
    ne = w_router.shape[1]
    wa = mix_a.shape[1]
    w_hi = w_router.astype(BF16)
    w_router_hl = jnp.stack([w_hi, (w_router - w_hi.astype(F32)).astype(BF16)])
    row = lambda i: (i, 0)
    fix = lambda i: (0, 0)
    vec = pl.BlockSpec((1, d), fix)
    return pl.pallas_call(
        _outproj_kernel,
        out_shape=(jax.ShapeDtypeStruct((s, d), F32),
                   jax.ShapeDtypeStruct((s * SUBLANES, LANES), F32),
                   jax.ShapeDtypeStruct((2 * TOP_K, s), I32),
                   jax.ShapeDtypeStruct((s, TOP_K), F32),
                   jax.ShapeDtypeStruct((8, ne), F32)),
        grid=(s // tm,),
        in_specs=[pl.BlockSpec((tm, wa), row), pl.BlockSpec((tm, wa), row), pl.BlockSpec((tm, d), row),
                  pl.BlockSpec((wa, d), fix), pl.BlockSpec((wa, d), fix),
                  vec, vec, vec, vec, vec,
                  pl.BlockSpec((2, d, ne), lambda i: (0, 0, 0)), pl.BlockSpec((1, ne), fix)],
        out_specs=(pl.BlockSpec((tm, d), row), pl.BlockSpec((tm * SUBLANES, LANES), row),
                   pl.BlockSpec((2 * TOP_K, tm), lambda i: (0, i)),
                   pl.BlockSpec((tm, TOP_K), row), pl.BlockSpec((8, ne), fix)),
        scratch_shapes=[pltpu.VMEM((1, ne), F32)],
        compiler_params=_cparams(("arbitrary",)),
        name="outproj",
    )(mix_a, mix_b, x2, w_a, w_b, gpost, gt, gpre, sc, sh, w_router_hl, b_router)


def _dest_kernel(route_ref, pstart_ref, o_ref):
    route = route_ref[...]
    tm = route.shape[1]
    ne = pstart_ref.shape[0]
    expert = lax.broadcasted_iota(I32, (ne, tm), 0)
    for k in range(TOP_K):
        start = jnp.sum(jnp.where(expert == route[k:k + 1, :], pstart_ref[...], 0), axis=0, keepdims=True)
        o_ref[k:k + 1, :] = start + route[TOP_K + k:TOP_K + k + 1, :]


def _dest(route, pstart):
    s = route.shape[1]
    tm = min(4 * ROW_TILE, s)
    return pl.pallas_call(
        _dest_kernel,
        out_shape=jax.ShapeDtypeStruct((TOP_K, s), I32),
        grid=(s // tm,),
        in_specs=[pl.BlockSpec((2 * TOP_K, tm), lambda i: (0, i)),
                  pl.BlockSpec((N_EXPERTS, 1), lambda i: (0, 0))],
        out_specs=pl.BlockSpec((TOP_K, tm), lambda i: (0, i)),
        compiler_params=_cparams(("arbitrary",)),
        name="dest",
    )(route, pstart.reshape(-1, 1))


def _dispatch_kernel(pstart_ref, pblk_ref, nu_ref, dest_ref, h_ref, xs_ref,
                     zero_scr, sem, zsem):
    tm = h_ref.shape[0] // SUBLANES
    bm = zero_scr.shape[0] // SUBLANES
    n_blk = xs_ref.shape[0] // (bm * SUBLANES)

    rows = _tile_rows

    @pl.when(pl.program_id(0) == 0)
    def _():
        zero_scr[...] = jnp.zeros_like(zero_scr)

        def zero_copy(row0):
            return pltpu.make_async_copy(zero_scr, xs_ref.at[rows(row0, bm)], zsem)

        for phase in range(2):
            for e in range(N_EXPERTS):
                last = pstart_ref[e] + (pblk_ref[e] - 1) * bm
                tail = (n_blk - N_EXPERTS + e) * bm
                for cond, row0 in ((pblk_ref[e] > 0, last), (n_blk - N_EXPERTS + e >= nu_ref[0], tail)):
                    @pl.when(cond)
                    def _():
                        if phase == 0:
                            zero_copy(row0).start()
                        else:
                            zero_copy(row0).wait()

    def row_copy(r, k):
        dst = dest_ref[k, r]
        return pltpu.make_async_copy(h_ref.at[rows(r, 1)], xs_ref.at[rows(dst, 1)], sem)

    def start(r, carry):
        for k in range(TOP_K):
            row_copy(r, k).start(priority=k % 2)
        return carry

    lax.fori_loop(0, tm, start, 0, unroll=4)
    for k in range(TOP_K):
        pltpu.make_async_copy(h_ref, xs_ref.at[rows(0, tm)], sem).wait()


def _dispatch(pstart, pblk, n_used, dest, h2, n_rows):
    s = h2.shape[0] // SUBLANES
    tm = COMBINE_TILE
    grid_spec = pltpu.PrefetchScalarGridSpec(
        num_scalar_prefetch=3,
        grid=(s // tm,),
        in_specs=[pl.BlockSpec((TOP_K, tm), lambda i, *_: (0, i), memory_space=pltpu.SMEM),
                  pl.BlockSpec((tm * SUBLANES, LANES), lambda i, *_: (i, 0))],
        out_specs=pl.BlockSpec(memory_space=pl.ANY),
        scratch_shapes=[pltpu.VMEM((EXPERT_BLOCK * SUBLANES, LANES), F32), pltpu.SemaphoreType.DMA,
                        pltpu.SemaphoreType.DMA],
    )
    return pl.pallas_call(
        _dispatch_kernel,
        out_shape=jax.ShapeDtypeStruct((n_rows * SUBLANES, LANES), F32),
        grid_spec=grid_spec,
        compiler_params=_cparams(("arbitrary",)),
        name="dispatch",
    )(pstart, pblk, n_used, dest, h2)


def _experts_kernel(be_ref, nu_ref, xs_ref, wgu_ref, bgu_ref, wd_ref, bd_ref, y_ref,
                    wgu_bf, wd_bf):
    b = pl.program_id(0)
    d_exp = wd_ref.shape[1]
    prev = be_ref[jnp.maximum(b - 1, 0)]
    changed = (b == 0) | (be_ref[b] != prev)

    @pl.when((b < nu_ref[0]) & changed)
    def _():
        rows = LANES

        def cast_gu(c, carry):
            r0 = pl.multiple_of(c * rows, rows)
            wgu_bf[pl.ds(r0, rows), :] = wgu_ref[0, pl.ds(r0, rows), :].astype(BF16)
            return carry

        def cast_d(c, carry):
            r0 = pl.multiple_of(c * rows, rows)
            wd_bf[pl.ds(r0, rows), :] = wd_ref[0, pl.ds(r0, rows), :].astype(BF16)
            return carry

        lax.fori_loop(0, wgu_ref.shape[1] // rows, cast_gu, 0)
        lax.fori_loop(0, wd_ref.shape[1] // rows, cast_d, 0)

    @pl.when(b < nu_ref[0])
    def _():
        bm = xs_ref.shape[0] // SUBLANES
        xb = _load_token_tiles(xs_ref, bm).astype(BF16)
        y = bd_ref[0]
        for c0 in range(0, d_exp, EXPERT_CHUNK):
            c1 = c0 + EXPERT_CHUNK
            x_glu = jnp.dot(xb, wgu_bf[:, c0:c1], preferred_element_type=F32) + bgu_ref[0, :, c0:c1]
            x_lin = (jnp.dot(xb, wgu_bf[:, d_exp + c0:d_exp + c1], preferred_element_type=F32)
                     + bgu_ref[0, :, d_exp + c0:d_exp + c1])
            x_glu = jnp.minimum(x_glu, SWIGLU_LIMIT)
            x_lin = jnp.clip(x_lin, -SWIGLU_LIMIT, SWIGLU_LIMIT)
            act = x_glu * jax.nn.sigmoid(SWIGLU_ALPHA * x_glu) * (x_lin + 1.0)
            y = y + jnp.dot(act.astype(BF16), wd_bf[c0:c1, :], preferred_element_type=F32)
        _store_token_tiles(y_ref, y)

    @pl.when(b >= nu_ref[0])
    def _():
        y_ref[...] = jnp.zeros_like(y_ref)


def _experts(block_e, n_used, xs, w_gate_up, b_gate_up, w_down, b_down):
    n_rows = xs.shape[0] // SUBLANES
    bm = EXPERT_BLOCK
    n_blk = n_rows // bm
    ne, d, two_de = w_gate_up.shape
    de = w_down.shape[1]
    assert d == SUBLANES * LANES

    def blk(b, be, nu):
        return jnp.minimum(b, nu[0] - 1)

    grid_spec = pltpu.PrefetchScalarGridSpec(
        num_scalar_prefetch=2,
        grid=(n_blk,),
        in_specs=[pl.BlockSpec((bm * SUBLANES, LANES), lambda b, be, nu: (blk(b, be, nu), 0)),
                  pl.BlockSpec((1, d, two_de), lambda b, be, nu: (be[blk(b, be, nu)], 0, 0)),
                  pl.BlockSpec((1, 1, two_de), lambda b, be, nu: (be[blk(b, be, nu)], 0, 0)),
                  pl.BlockSpec((1, de, d), lambda b, be, nu: (be[blk(b, be, nu)], 0, 0)),
                  pl.BlockSpec((1, 1, d), lambda b, be, nu: (be[blk(b, be, nu)], 0, 0))],
        out_specs=pl.BlockSpec((bm * SUBLANES, LANES), lambda b, be, nu: (b, 0)),
        scratch_shapes=[pltpu.VMEM((d, two_de), BF16), pltpu.VMEM((de, d), BF16)],
    )
    return pl.pallas_call(
        _experts_kernel,
        out_shape=jax.ShapeDtypeStruct((n_rows * SUBLANES, LANES), F32),
        grid_spec=grid_spec,
        compiler_params=_cparams(("arbitrary",)),
        name="experts",
    )(block_e, n_used, xs, w_gate_up, b_gate_up.reshape(ne, 1, two_de), w_down, b_down.reshape(ne, 1, d))


def _combine_kernel(dest_ref, dest_next_ref, y_ref, gate_ref, x1_ref, gt_ref, gpost_ref, o_ref, buf, sem):
    i = pl.program_id(0)
    tm = x1_ref.shape[0]
    cur = i % 2

    def gather(d_ref, slot):
        def start(r, carry):
            for k in range(TOP_K):
                pltpu.make_async_copy(y_ref.at[_tile_rows(d_ref[k, r], 1)],
                                      buf.at[slot, k, _tile_rows(r, 1)], sem.at[slot]).start(priority=k % 2)
            return carry

        lax.fori_loop(0, tm, start, 0, unroll=4)

    @pl.when(i == 0)
    def _():
        gather(dest_ref, 0)

    @pl.when(i + 1 < pl.num_programs(0))
    def _():
        gather(dest_next_ref, 1 - cur)

    for k in range(TOP_K):
        pltpu.make_async_copy(y_ref.at[_tile_rows(0, tm)], buf.at[cur, k], sem.at[cur]).wait()

    gate = gate_ref[...]
    acc = gate[:, 0:1] * _load_token_tiles(buf.at[cur, 0], tm)
    for k in range(1, TOP_K):
        acc = acc + gate[:, k:k + 1] * _load_token_tiles(buf.at[cur, k], tm)
    o_ref[...] = x1_ref[...] + gt_ref[...] * (_rms(acc) * gpost_ref[...])


def _combine(dest, y, gate4, x1, gt, gpost):
    s, d = x1.shape
    tm = COMBINE_TILE // 2
    n = s // tm
    return pl.pallas_call(
        _combine_kernel,
        out_shape=jax.ShapeDtypeStruct((s, d), F32),
        grid=(n,),
        in_specs=[pl.BlockSpec((TOP_K, tm), lambda i: (0, i), memory_space=pltpu.SMEM),
                  pl.BlockSpec((TOP_K, tm), lambda i: (0, jnp.minimum(i + 1, n - 1)), memory_space=pltpu.SMEM),
                  pl.BlockSpec(memory_space=pl.ANY),
                  pl.BlockSpec((tm, TOP_K), lambda i: (i, 0)),
                  pl.BlockSpec((tm, d), lambda i: (i, 0)),
                  pl.BlockSpec((1, d), lambda i: (0, 0)),
                  pl.BlockSpec((1, d), lambda i: (0, 0))],
        out_specs=pl.BlockSpec((tm, d), lambda i: (i, 0)),
        scratch_shapes=[pltpu.VMEM((2, TOP_K, tm * SUBLANES, LANES), F32), pltpu.SemaphoreType.DMA((2,))],
        compiler_params=_cparams(("arbitrary",)),
        name="combine",
    )(dest, dest, y, gate4, x1, gt, gpost)


def _layer(x2, mod, g_pre_mix, g_post_mix, w_in, b_forget, rel_bias, w_out,
           g_pre_ffn, g_post_ffn, w_router, b_router, w_gate_up, b_gate_up, w_down, b_down):
    s, d = x2.shape
    sh_m, sc_m, gt_m, sh_f, sc_f, gt_f = [mod[:, k * d:(k + 1) * d] for k in range(6)]
    n_qkv = 3 * (N_HEADS_FOX + N_HEADS_MOBA) * HEAD_DIM
    fox_w = N_HEADS_FOX * HEAD_DIM

    w_qkv = w_in[:, :n_qkv].astype(BF16)
    w_f = w_in[:, n_qkv:].T
    qkv, cum, sel, nrm, tr, nr = _inproj(x2, g_pre_mix.reshape(1, d), sc_m, sh_m, w_qkv, w_f,
                                         b_forget.reshape(-1, 1))
    y_a = _fox(qkv, tr, nr, cum, nrm)
    y_b = _moba(qkv, tr, sel, rel_bias)

    w_out_bf = w_out.astype(BF16)
    x1, h2, route, gate4, cnt = _outproj(
        y_a, y_b, x2, w_out_bf[:fox_w], w_out_bf[fox_w:], g_post_mix.reshape(1, d), gt_m,
        g_pre_ffn.reshape(1, d), sc_f, sh_f, w_router, b_router.reshape(1, -1))

    bm = EXPERT_BLOCK
    counts = cnt[0].astype(I32)
    pblk = (counts + bm - 1) // bm
    pend_blk = jnp.cumsum(pblk)
    pstart = ((pend_blk - pblk) * bm).astype(I32)
    n_rows = s * TOP_K + N_EXPERTS * bm
    n_blk = n_rows // bm
    block_e = jnp.minimum(jnp.sum(pend_blk[None, :] <= jnp.arange(n_blk)[:, None], axis=1),
                          N_EXPERTS - 1).astype(I32)
    n_used = pend_blk[-1:].astype(I32)

    dest = _dest(route, pstart)
    xs = _dispatch(pstart, pblk.astype(I32), n_used, dest, h2, n_rows)
    y = _experts(block_e, n_used, xs, w_gate_up, b_gate_up, w_down, b_down)
    return _combine(dest, y, gate4, x1, gt_f, g_post_ffn.reshape(1, d))


def kernel(x, c, w_ada, b_ada, g_pre_mix, g_post_mix, w_in, b_forget, rel_bias, w_out, g_pre_ffn, g_post_ffn, w_router, b_router, w_gate_up, b_gate_up, w_down, b_down):
    bsz, s, d = x.shape
    depth = w_ada.shape[0]
    outs = []
    for bi in range(bsz):
        x2 = x[bi]
        for l in range(depth):
            mod = _adaln(c[bi:bi + 1], w_ada[l], b_ada[l])
            x2 = _layer(x2, mod, g_pre_mix[l], g_post_mix[l], w_in[l], b_forget[l], rel_bias, w_out[l],
                        g_pre_ffn[l], g_post_ffn[l], w_router[l], b_router[l], w_gate_up[l], b_gate_up[l],
                        w_down[l], b_down[l])
        outs.append(x2)
    return outs[0].reshape(1, s, d) if bsz == 1 else jnp.stack(outs)
```

```python
import math

import numpy as np
import jax
import jax.numpy as jnp
from jax import lax
from jax.experimental import pallas as pl
from jax.experimental.pallas import tpu as pltpu

F32 = jnp.float32
BF16 = jnp.bfloat16
I32 = jnp.int32

HEAD_DIM = 64
N_HEADS_FOX = 8
N_HEADS_MOBA = 8
PAIR = 2 * HEAD_DIM
MOBA_BLOCK = 256
MOBA_TOPK = 3
NUM_BUCKETS = 32
MAX_DISTANCE = 128
N_EXPERTS = 32
TOP_K = 4
SWIGLU_LIMIT = 7.0
SWIGLU_ALPHA = 1.702
RMS_EPS = 1e-6
NEG = -(2.0 ** 100)
M_INIT = -(2.0 ** 99)
LOG2E = math.log2(math.e)
SUM_ROWS = 16
SUBLANES = 8
LANES = 128
EXP_UNDERFLOW = 90.0
VMEM_LIMIT = 56 * 1024 * 1024

ROW_TILE = 512
FOX_TILE = 256
EXPERT_BLOCK = 512
COMBINE_TILE = 1024
MOBA_LAG = 1
MOBA_UNROLL = 4

NT_DIMS = (((1,), (1,)), ((), ()))


def _cparams(sem):
    return pltpu.CompilerParams(dimension_semantics=sem, vmem_limit_bytes=VMEM_LIMIT)


def _rms(x):
    return x * lax.rsqrt(jnp.mean(x * x, axis=-1, keepdims=True) + RMS_EPS)


def _adaln_kernel(c_ref, w_ref, b_ref, o_ref):
    c = c_ref[...]
    cond = c * jax.nn.sigmoid(c)
    o_ref[...] = jnp.dot(cond, w_ref[...], preferred_element_type=F32,
                         precision=lax.Precision.HIGHEST) + b_ref[...]


def _adaln(c, w_ada, b_ada):
    d = c.shape[-1]
    n = w_ada.shape[-1]
    c8 = jnp.broadcast_to(c.reshape(1, d), (8, d))
    out = pl.pallas_call(
        _adaln_kernel,
        out_shape=jax.ShapeDtypeStruct((8, n), F32),
        grid=(n // d,),
        in_specs=[pl.BlockSpec((8, d), lambda j: (0, 0)),
                  pl.BlockSpec((d, d), lambda j: (0, j)),
                  pl.BlockSpec((1, d), lambda j: (0, j))],
        out_specs=pl.BlockSpec((8, d), lambda j: (0, j)),
        compiler_params=_cparams(("arbitrary",)),
        name="adaln",
    )(c8, w_ada, b_ada.reshape(1, n))
    return out[0:1]


def _inproj_kernel(x_ref, g_ref, sc_ref, sh_ref, w_ref, wf_ref, bf_ref,
                   qkv_ref, cum_ref, sel_ref, nrm_ref, tr_ref, nr_ref, km_scr, carry_scr):
    i = pl.program_id(0)
    tm = x_ref.shape[0]
    nblk = km_scr.shape[0]

    @pl.when(i == 0)
    def _():
        km_scr[...] = jnp.zeros_like(km_scr)
        carry_scr[...] = jnp.zeros_like(carry_scr)

    x = x_ref[...]
    h = _rms(x) * g_ref[...] * (1.0 + sc_ref[...]) + sh_ref[...]
    hb = h.astype(BF16)

    width = N_HEADS_FOX * HEAD_DIM
    hsel = jnp.where(lax.broadcasted_iota(I32, (width, N_HEADS_FOX), 0) // HEAD_DIM
                     == lax.broadcasted_iota(I32, (width, N_HEADS_FOX), 1), 1.0, 0.0)
    kb = None
    tr_slot = {3: 0, 5: 1, 0: 2, 2: 3}
    qbt = None
    for c in range(6):
        pc = jnp.dot(hb, w_ref[:, c * width:(c + 1) * width], preferred_element_type=F32)
        if c in tr_slot:
            n = tr_slot[c]
            pct = pc.T
            if c == 3:
                qbt = pct
            if c == 0 or c == 3:
                pct = pct * (LOG2E * HEAD_DIM ** -0.5)
            tr_ref[n * width:(n + 1) * width, :] = pct.astype(BF16)
        if c == 1 or c == 4:
            qkv_ref[:, (c // 3) * width:(c // 3 + 1) * width] = pc.astype(BF16)
        if c < 2:
            sq = (pc * (HEAD_DIM ** -0.5) if c == 0 else pc).astype(BF16).astype(F32)
            n2 = jnp.dot((sq * sq).astype(BF16), hsel.astype(BF16), preferred_element_type=F32)
            nrm_ref[0, c:c + 1, :] = jnp.max(n2, axis=0, keepdims=True)
        if c == 4:
            kb = pc

    ft = lax.dot_general(wf_ref[...].astype(BF16), hb, NT_DIMS, preferred_element_type=F32)
    z = ft + bf_ref[...]
    logf = -(jnp.maximum(-z, 0.0) + jnp.log1p(jnp.exp(-jnp.abs(z))))
    lane = lax.broadcasted_iota(I32, logf.shape, 1)
    cs = logf
    sh = 1
    while sh < tm:
        cs = cs + jnp.where(lane >= sh, pltpu.roll(cs, sh, axis=1), 0.0)
        sh *= 2
    base = jnp.zeros_like(cs)
    for b in range(1, tm // FOX_TILE):
        base = jnp.where(lane >= b * FOX_TILE, cs[:, b * FOX_TILE - 1:b * FOX_TILE], base)
    nr_t = (-LOG2E * (cs - base)).T
    for pp in range(nr_ref.shape[0]):
        nr_ref[pp] = nr_t[:, 2 * pp:2 * pp + 2]
    cs = cs + carry_scr[...]
    cum_ref[...] = cs
    carry_scr[...] = cs[:, tm - 1:tm]

    nb_tile = tm // MOBA_BLOCK
    for b in range(nb_tile):
        kmean = jnp.sum(kb[b * MOBA_BLOCK:(b + 1) * MOBA_BLOCK], axis=0, keepdims=True) * (1.0 / MOBA_BLOCK)
        km_scr[pl.ds(i * nb_tile + b, 1), :] = kmean

    km = km_scr[...]
    blk = lax.broadcasted_iota(I32, (nblk, tm), 0)
    col = lax.broadcasted_iota(I32, (nblk, tm), 1)
    own = i * nb_tile + col // MOBA_BLOCK
    for hd in range(N_HEADS_MOBA):
        hs = slice(hd * HEAD_DIM, (hd + 1) * HEAD_DIM)
        g = jnp.dot(km[:, hs], qbt[hs, :], preferred_element_type=F32,
                    precision=lax.Precision.HIGHEST)
        g = jnp.where(blk < own, g, -jnp.inf)
        sel = jnp.zeros(g.shape, dtype=jnp.bool_)
        for _ in range(MOBA_TOPK):
            m = jnp.max(g, axis=0, keepdims=True)
            first = jnp.min(jnp.where(g == m, blk, nblk), axis=0, keepdims=True)
            pick = (blk == first) & (m > -jnp.inf)
            sel = sel | pick
            g = jnp.where(pick, -jnp.inf, g)
        sel_ref[hd] = jnp.where(sel, 0.0, NEG)


def _inproj(x2, g, sc, sh, w_qkv, w_f, b_f):
    s, d = x2.shape
    tm = ROW_TILE
    nblk = s // MOBA_BLOCK
    n = w_qkv.shape[1]
    width = N_HEADS_FOX * HEAD_DIM
    n_rows_out = 2 * width
    n_tr = 4 * width
    assert tm % FOX_TILE == 0 and N_HEADS_FOX == N_HEADS_MOBA
    return pl.pallas_call(
        _inproj_kernel,
        out_shape=(jax.ShapeDtypeStruct((s, n_rows_out), BF16),
                   jax.ShapeDtypeStruct((N_HEADS_FOX, s), F32),
                   jax.ShapeDtypeStruct((N_HEADS_MOBA, nblk, s), F32),
                   jax.ShapeDtypeStruct((s // tm, 2, N_HEADS_FOX), F32),
                   jax.ShapeDtypeStruct((n_tr, s), BF16),
                   jax.ShapeDtypeStruct((N_HEADS_FOX // 2, s, 2), F32)),
        grid=(s // tm,),
        in_specs=[pl.BlockSpec((tm, d), lambda i: (i, 0)),
                  pl.BlockSpec((1, d), lambda i: (0, 0)),
                  pl.BlockSpec((1, d), lambda i: (0, 0)),
                  pl.BlockSpec((1, d), lambda i: (0, 0)),
                  pl.BlockSpec((d, n), lambda i: (0, 0)),
                  pl.BlockSpec((N_HEADS_FOX, d), lambda i: (0, 0)),
                  pl.BlockSpec((N_HEADS_FOX, 1), lambda i: (0, 0))],
        out_specs=(pl.BlockSpec((tm, n_rows_out), lambda i: (i, 0)),
                   pl.BlockSpec((N_HEADS_FOX, tm), lambda i: (0, i)),
                   pl.BlockSpec((N_HEADS_MOBA, nblk, tm), lambda i: (0, 0, i)),
                   pl.BlockSpec((1, 2, N_HEADS_FOX), lambda i: (i, 0, 0)),
                   pl.BlockSpec((n_tr, tm), lambda i: (0, i)),
                   pl.BlockSpec((N_HEADS_FOX // 2, tm, 2), lambda i: (0, i, 0))),
        scratch_shapes=[pltpu.VMEM((nblk, N_HEADS_MOBA * HEAD_DIM), F32),
                        pltpu.VMEM((N_HEADS_FOX, 1), F32)],
        compiler_params=_cparams(("arbitrary",)),
        name="inproj",
    )(x2, g, sc, sh, w_qkv, w_f, b_f)


def _fox_kernel(jlo_ref, cp_ref, qt_ref, k_ref, vt_ref, nr_ref, o_ref, m_ref, a_ref):
    p = pl.program_id(0)
    g = pl.program_id(1)
    j_first = jlo_ref[p * pl.num_programs(1) + g]
    tq = qt_ref.shape[1]
    t = tq // 2

    qt = qt_ref[...]
    top = lax.broadcasted_iota(I32, qt.shape, 0) < HEAD_DIM
    zq = jnp.zeros_like(qt)
    qth = (jnp.where(top, qt, zq), jnp.where(top, zq, qt))
    m_ref[...] = jnp.full(m_ref.shape, M_INIT, F32)
    a_ref[...] = jnp.zeros(a_ref.shape, F32)
    er = lax.broadcasted_iota(I32, (SUM_ROWS, 2 * t), 0)
    ec = lax.broadcasted_iota(I32, (SUM_ROWS, 2 * t), 1)
    ones_rows = jnp.where(((er == 0) & (ec < t)) | ((er == 1) & (ec >= t)), 1.0, 0.0).astype(BF16)
    arow = lax.broadcasted_iota(I32, a_ref.shape, 0)
    head0_rows = (arow < HEAD_DIM) | (arow == PAIR)
    vtop = lax.broadcasted_iota(I32, (PAIR, t), 0) < HEAD_DIM
    in_a = lax.broadcasted_iota(I32, (1, tq), 1) < t
    krow = lax.broadcasted_iota(I32, (t, tq), 0)
    qcol = lax.broadcasted_iota(I32, (t, tq), 1)
    future_a = (qcol < t) & (krow > qcol)
    future_b = (qcol >= t) & (krow > qcol - t)

    def scores(j, future):
        k0 = pl.multiple_of(j * t, t)
        kt = k_ref[pl.ds(k0, t), :]
        out = []
        for hd in range(2):
            s = jnp.dot(kt, qth[hd], preferred_element_type=F32) + nr_ref[0, pl.ds(k0, t), hd:hd + 1]
            out.append(s if future is None else jnp.where(future, NEG, s))
        return out

    def shifts(j, a_on, b_on):
        rows = []
        for hd in range(2):
            h = 2 * p + hd
            sa = LOG2E * (cp_ref[h, 2 * g] - cp_ref[h, j])
            sb = LOG2E * (cp_ref[h, 2 * g + 1] - cp_ref[h, j])
            rows.append(jnp.where(in_a, jnp.where(a_on, sa, NEG), jnp.where(b_on, sb, NEG)))
        return rows

    def softmax(ss, shift_rows):
        ps, alphas = [], []
        for hd in range(2):
            m_old = m_ref[hd]
            m_new = jnp.maximum(m_old, jnp.max(ss[hd], axis=0, keepdims=True) + shift_rows[hd])
            alphas.append(jnp.exp2(m_old - m_new))
            m_ref[hd] = m_new
            ps.append(jnp.exp2(ss[hd] - (m_new - shift_rows[hd])).astype(BF16))
        return ps, alphas

    def accumulate(j, ps, alphas):
        k0 = pl.multiple_of(j * t, t)
        vt = vt_ref[:, pl.ds(k0, t)]
        zv = jnp.zeros_like(vt)
        vcat = jnp.concatenate([jnp.where(vtop, vt, zv), jnp.where(vtop, zv, vt)], axis=1)
        vcat = jnp.concatenate([vcat, ones_rows], axis=0)
        pcat = jnp.concatenate(ps, axis=0)
        a_ref[...] = (a_ref[...] * jnp.where(head0_rows, alphas[0], alphas[1])
                      + jnp.dot(vcat, pcat, preferred_element_type=F32))

    def body(j, carry):
        accumulate(j, *softmax(scores(j, None), shifts(j, True, True)))
        return carry

    lax.fori_loop(j_first, 2 * g - 1, body, 0)
    j1 = jnp.maximum(2 * g - 1, 0)
    j2 = 2 * g
    j3 = 2 * g + 1
    ss1 = scores(j1, None)
    ss2 = scores(j2, future_a)
    ss3 = scores(j3, future_b)
    w1 = softmax(ss1, shifts(j1, g >= 1, g >= 1))
    w2 = softmax(ss2, shifts(j2, True, True))
    accumulate(j1, *w1)
    w3 = softmax(ss3, shifts(j3, False, True))
    accumulate(j2, *w2)
    accumulate(j3, *w3)
    out_t =a_ref[0:PAIR, :] / jnp.where(top, a_ref[PAIR:PAIR + 1, :], a_ref[PAIR + 1:PAIR + 2, :])
    o_ref[...] = out_t.T.astype(o_ref.dtype)


def _fox_first_tile(cum, nrm, t):
    cend = cum[:, t - 1::t]
    nt = cend.shape[1]
    cprev = jnp.concatenate([jnp.zeros((cend.shape[0], 1), F32), cend[:, :-1]], axis=1)
    rep = nt // nrm.shape[0]
    qn = jnp.repeat(jnp.sqrt(nrm[:, 0, :]).T, rep, axis=1)
    kn = jnp.repeat(jnp.sqrt(nrm[:, 1, :]).T, rep, axis=1)
    gap = (1.02 * qn[:, :, None] * (kn[:, None, :] + kn[:, :, None])
           + cprev[:, :, None] - cend[:, None, :])
    jj = jnp.arange(nt)[None, None, :]
    ii = jnp.arange(nt)[None, :, None]
    needed = (jj < ii) & jnp.logical_not(gap <= -EXP_UNDERFLOW)
    needed = needed[0::2] | needed[1::2]
    first = jnp.min(jnp.where(needed, jj, ii), axis=2)
    return first.reshape(-1).astype(I32)


def _fox(qkv, tr, nr, cum, nrm):
    s = qkv.shape[0]
    t = FOX_TILE
    npair = N_HEADS_FOX // 2
    tq = 2 * t
    jlo = _fox_first_tile(cum, nrm, t).reshape(npair, s // tq, 2).min(axis=2).reshape(-1)
    cend = cum[:, t - 1::t]
    cprev = jnp.concatenate([jnp.zeros((cend.shape[0], 1), F32), cend[:, :-1]], axis=1)
    q_rows = 2 * (N_HEADS_MOBA // 2)
    grid_spec = pltpu.PrefetchScalarGridSpec(
        num_scalar_prefetch=1,
        grid=(npair, s // tq),
        in_specs=[pl.BlockSpec(memory_space=pltpu.SMEM),
                  pl.BlockSpec((PAIR, tq), lambda p, i, jl: (q_rows + p, i)),
                  pl.BlockSpec((s, PAIR), lambda p, i, jl: (0, p)),
                  pl.BlockSpec((PAIR, s), lambda p, i, jl: (q_rows + npair + p, 0)),
                  pl.BlockSpec((1, s, 2), lambda p, i, jl: (p, 0, 0))],
        out_specs=pl.BlockSpec((tq, PAIR), lambda p, i, jl: (i, p)),
        scratch_shapes=[pltpu.VMEM((2, 1, tq), F32), pltpu.VMEM((PAIR + SUM_ROWS, tq), F32)],
    )
    return pl.pallas_call(
        _fox_kernel,
        out_shape=jax.ShapeDtypeStruct((s, npair * PAIR), BF16),
        grid_spec=grid_spec,
        compiler_params=_cparams(("arbitrary", "arbitrary")),
        name="fox",
    )(jlo, cprev, tr, qkv, tr, nr)


def _t5_bucket_np(dist):
    dist = np.maximum(dist, 0)
    max_exact = NUM_BUCKETS // 2
    d = np.maximum(dist, 1).astype(np.float32)
    large = max_exact + (np.log(d / np.float32(max_exact)) / np.float32(math.log(MAX_DISTANCE / max_exact))
                         * np.float32(NUM_BUCKETS - max_exact)).astype(np.int32)
    large = np.minimum(large, NUM_BUCKETS - 1)
    return np.where(dist < max_exact, dist, large).astype(np.int32)


def _moba_kernel(rb_ref, qt_ref, k_ref, vt_ref, sel_ref, bkt_ref, o_ref,
                 m_ref, a_ref, bias_scr, *bufs):
    s_bufs = bufs[0:MOBA_UNROLL]
    p_bufs = bufs[MOBA_UNROLL:2 * MOBA_UNROLL]
    al_bufs = bufs[2 * MOBA_UNROLL:3 * MOBA_UNROLL]
    p = pl.program_id(0)
    g = pl.program_id(1)
    tq = qt_ref.shape[1]
    t = tq // 2

    @pl.when(g == 0)
    def _():
        r = lax.broadcasted_iota(I32, (t, t), 0)
        c = lax.broadcasted_iota(I32, (t, t), 1)
        zero = jnp.zeros((t, t), F32)
        for hd in range(2):
            h = 2 * p + hd
            far = rb_ref[(NUM_BUCKETS - 1) * N_HEADS_MOBA + h]
            tiles = []
            for w in range(2):
                bkt = bkt_ref[w]
                acc = jnp.zeros(bkt.shape, F32)
                for kk in range(NUM_BUCKETS):
                    acc = acc + jnp.where(bkt == kk, rb_ref[kk * N_HEADS_MOBA + h], 0.0)
                tiles.append((acc - far) * LOG2E)
            prev_t = tiles[0]
            own_t = jnp.where(r <= c, tiles[1], NEG)
            bias_scr[hd, 0] = jnp.concatenate([prev_t, zero], axis=1)
            bias_scr[hd, 1] = jnp.concatenate([own_t, prev_t], axis=1)
            bias_scr[hd, 2] = jnp.concatenate([zero, own_t], axis=1)

    qt = qt_ref[...]
    top = lax.broadcasted_iota(I32, qt.shape, 0) < HEAD_DIM
    zq = jnp.zeros_like(qt)
    qth = (jnp.where(top, qt, zq), jnp.where(top, zq, qt))
    m_ref[...] = jnp.full(m_ref.shape, M_INIT, F32)
    a_ref[...] = jnp.zeros(a_ref.shape, F32)

    ones_rows = jnp.where(lax.broadcasted_iota(I32, (SUM_ROWS, t), 0) == 0, 1.0, 0.0).astype(BF16)
    in_a = lax.broadcasted_iota(I32, (1, tq), 1) < t

    n_far = jnp.maximum(2 * g - 1, 0)

    n_key_tiles = k_ref.shape[0] // t

    def produce(j, s_buf):
        j = jnp.minimum(j, n_key_tiles - 1)
        k0 = pl.multiple_of(j * t, t)
        kt = k_ref[pl.ds(k0, t), :]
        for hd in range(2):
            s_buf[hd] = jnp.dot(kt, qth[hd], preferred_element_type=F32)

    def softmax(s_buf, p_buf, al_buf, selrows, w):
        for hd in range(2):
            s = s_buf[hd]
            if w is not None:
                s = s + bias_scr[hd, w]
            smax = jnp.max(s, axis=0, keepdims=True)
            m_old = m_ref[hd]
            m_new = jnp.maximum(m_old, smax + selrows[hd])
            shift = m_new - selrows[hd]
            al_buf[hd] = jnp.exp2(m_old - m_new)
            m_ref[hd] = m_new
            p_buf[hd * t:(hd + 1) * t, :] = jnp.exp2(s - shift).astype(BF16)

    def far_rows(j):
        return [jnp.where(j < n_far, sel_ref[hd, pl.ds(j, 1), :], NEG) for hd in range(2)]

    def accumulate(j, p_buf, al_buf):
        k0 = pl.multiple_of(j * t, t)
        for hd in range(2):
            vh = vt_ref[hd * HEAD_DIM:(hd + 1) * HEAD_DIM, pl.ds(k0, t)]
            vcat = jnp.concatenate([vh, ones_rows], axis=0)
            a_ref[hd] = (a_ref[hd] * al_buf[hd]
                         + jnp.dot(vcat, p_buf[hd * t:(hd + 1) * t, :], preferred_element_type=F32))

    un = MOBA_UNROLL
    lag = MOBA_LAG
    for n in range(un - lag, un):
        p_bufs[n][...] = jnp.zeros_like(p_bufs[n])
        al_bufs[n][...] = jnp.ones_like(al_bufs[n])
    produce(0, s_bufs[0])

    def body(u, carry):
        j0 = un * u
        for n in range(un):
            accumulate(jnp.maximum(j0 + n - lag, 0), p_bufs[(n - lag) % un], al_bufs[(n - lag) % un])
            softmax(s_bufs[n], p_bufs[n], al_bufs[n], far_rows(j0 + n), None)
            produce(j0 + n + 1, s_bufs[(n + 1) % un])
        return carry

    n_trips = (n_far + un - 1) // un
    lax.fori_loop(0, n_trips, body, 0)

    j1 = jnp.maximum(2 * g - 1, 0)
    j2 = 2 * g
    j3 = 2 * g + 1
    rows1 = [jnp.where(g >= 1, sel_ref[hd, pl.ds(j1, 1), :], NEG) for hd in range(2)]
    rows2 = [jnp.where(in_a, 0.0, sel_ref[hd, pl.ds(j2, 1), :]) for hd in range(2)]
    rows3 = [jnp.where(in_a, NEG, 0.0)] * 2
    produce(j1, s_bufs[0])
    produce(j2, s_bufs[1])
    for n in range(un - lag, un):
        accumulate(jnp.maximum(un * n_trips - un + n, 0), p_bufs[n], al_bufs[n])
    softmax(s_bufs[0], p_bufs[0], al_bufs[0], rows1, 0)
    produce(j3, s_bufs[0])
    softmax(s_bufs[1], p_bufs[1], al_bufs[1], rows2, 1)
    accumulate(j1, p_bufs[0], al_bufs[0])
    softmax(s_bufs[0], p_bufs[2], al_bufs[2], rows3, 2)
    accumulate(j2, p_bufs[1], al_bufs[1])
    accumulate(j3, p_bufs[2], al_bufs[2])
    out_t = jnp.concatenate([a_ref[hd, 0:HEAD_DIM, :] / a_ref[hd, HEAD_DIM:HEAD_DIM + 1, :]
                             for hd in range(2)], axis=0)
    o_ref[...] = out_t.T.astype(o_ref.dtype)


def _moba(qkv, tr, sel, rel_bias):
    s = qkv.shape[0]
    t = MOBA_BLOCK
    npair = N_HEADS_MOBA // 2
    nblk = s // t
    kcol = N_HEADS_FOX // 2
    a = np.arange(t)[None, :]
    b = np.arange(t)[:, None]
    bkt = jnp.asarray(np.stack([_t5_bucket_np(t + a - b), _t5_bucket_np(a - b)]))
    tq = 2 * t
    grid_spec = pltpu.PrefetchScalarGridSpec(
        num_scalar_prefetch=1,
        grid=(npair, s // tq),
        in_specs=[pl.BlockSpec((PAIR, tq), lambda p, i, rb: (p, i)),
                  pl.BlockSpec((s, PAIR), lambda p, i, rb: (0, kcol + p)),
                  pl.BlockSpec((PAIR, s), lambda p, i, rb: (npair + p, 0)),
                  pl.BlockSpec((2, nblk, tq), lambda p, i, rb: (p, 0, i)),
                  pl.BlockSpec((2, t, t), lambda p, i, rb: (0, 0, 0))],
        out_specs=pl.BlockSpec((tq, PAIR), lambda p, i, rb: (i, p)),
        scratch_shapes=[pltpu.VMEM((2, 1, tq), F32),
                        pltpu.VMEM((2, HEAD_DIM + SUM_ROWS, tq), F32), pltpu.VMEM((2, 3, t, tq), F32)]
        + [pltpu.VMEM((2, t, tq), F32)] * MOBA_UNROLL
        + [pltpu.VMEM((2 * t, tq), BF16)] * MOBA_UNROLL
        + [pltpu.VMEM((2, 1, tq), F32)] * MOBA_UNROLL,
    )
    return pl.pallas_call(
        _moba_kernel,
        out_shape=jax.ShapeDtypeStruct((s, npair * PAIR), BF16),
        grid_spec=grid_spec,
        compiler_params=_cparams(("arbitrary", "arbitrary")),
        name="moba",
    )(rel_bias.reshape(-1), tr, qkv, tr, sel, bkt)


def _store_token_tiles(ref, val):
    n = val.shape[0]
    for c in range(SUBLANES):
        ref[pl.ds(c, n, stride=SUBLANES), :] = val[:, c * LANES:(c + 1) * LANES]


def _load_token_tiles(ref, n):
    return jnp.concatenate([ref[pl.ds(c, n, stride=SUBLANES), :] for c in range(SUBLANES)], axis=1)


def _tile_rows(r0, n):
    start = r0 * SUBLANES
    if not isinstance(start, int):
        start = pl.multiple_of(start, SUBLANES)
    return pl.ds(start, n * SUBLANES)


def _outproj_kernel(ya_ref, yb_ref, x_ref, wa_ref, wb_ref, gpost_ref, gt_ref, gpre_ref,
                    sc_ref, sh_ref, wr_ref, br_ref,
                    x1_ref, h2_ref, route_ref, gate_ref, cnt_ref, carry_scr):
    i = pl.program_id(0)
    tm = x_ref.shape[0]

    @pl.when(i == 0)
    def _():
        carry_scr[...] = jnp.zeros_like(carry_scr)

    y = (jnp.dot(ya_ref[...], wa_ref[...], preferred_element_type=F32)
         + jnp.dot(yb_ref[...], wb_ref[...], preferred_element_type=F32))
    x1 = x_ref[...] + gt_ref[...] * (_rms(y) * gpost_ref[...])
    x1_ref[...] = x1
    h2 = _rms(x1) * gpre_ref[...] * (1.0 + sc_ref[...]) + sh_ref[...]
    _store_token_tiles(h2_ref, h2)

    h_hi = h2.astype(BF16)
    h_lo = (h2 - h_hi.astype(F32)).astype(BF16)
    logits = (jnp.dot(h_hi, wr_ref[0], preferred_element_type=F32)
              + jnp.dot(h_hi, wr_ref[1], preferred_element_type=F32)
              + jnp.dot(h_lo, wr_ref[0], preferred_element_type=F32)) + br_ref[...]
    ne = logits.shape[1]
    lane = lax.broadcasted_iota(I32, logits.shape, 1)
    lane4 = lax.broadcasted_iota(I32, (tm, TOP_K), 1)
    g = logits
    mask = jnp.zeros(logits.shape, F32)
    vals, picks = [], []
    for _ in range(TOP_K):
        m = jnp.max(g, axis=1, keepdims=True)
        first = jnp.min(jnp.where(g == m, lane, ne), axis=1, keepdims=True)
        pick = lane == first
        mask = jnp.where(pick, 1.0, mask)
        g = jnp.where(pick, -jnp.inf, g)
        vals.append(m)
        picks.append(pick)
    ex = [jnp.exp(v - vals[0]) for v in vals]
    den = ex[0] + ex[1] + ex[2] + ex[3]
    gates = [e / den for e in ex]

    r = lax.broadcasted_iota(I32, (tm, tm), 0)
    c = lax.broadcasted_iota(I32, (tm, tm), 1)
    tril = jnp.where(c < r, 1.0, 0.0).astype(BF16)
    before = jnp.dot(tril, mask.astype(BF16), preferred_element_type=F32) + carry_scr[...]
    total = carry_scr[...] + jnp.sum(mask, axis=0, keepdims=True)
    carry_scr[...] = total
    cnt_ref[...] = jnp.broadcast_to(total, cnt_ref.shape)

    def pack4(cols):
        return jnp.where(lane4 == 0, cols[0],
                         jnp.where(lane4 == 1, cols[1], jnp.where(lane4 == 2, cols[2], cols[3])))

    wide = lax.broadcasted_iota(I32, (tm, LANES), 1)
    route = jnp.zeros((tm, LANES), F32)
    for k, pk in enumerate(picks):
        idx_k = jnp.sum(jnp.where(pk, lane, 0), axis=1, keepdims=True).astype(F32)
        rank_k = jnp.sum(jnp.where(pk, before, 0.0), axis=1, keepdims=True)
        route = jnp.where(wide == k, idx_k, jnp.where(wide == TOP_K + k, rank_k, route))
    route_ref[...] = route.T[0:2 * TOP_K, :].astype(I32)
    gate_ref[...] = pack4(gates)


def _outproj(mix_a, mix_b, x2, w_a, w_b, gpost, gt, gpre, sc, sh, w_router, b_router):
    s, d = x2.shape
    tm = ROW_TILE
    ne = w_router.shape[1]
    wa = mix_a.shape[1]
    w_hi = w_router.astype(BF16)
    w_router_hl = jnp.stack([w_hi, (w_router - w_hi.astype(F32)).astype(BF16)])
    row = lambda i: (i, 0)
    fix = lambda i: (0, 0)
    vec = pl.BlockSpec((1, d), fix)
    return pl.pallas_call(
        _outproj_kernel,
        out_shape=(jax.ShapeDtypeStruct((s, d), F32),
                   jax.ShapeDtypeStruct((s * SUBLANES, LANES), F32),
                   jax.ShapeDtypeStruct((2 * TOP_K, s), I32),
                   jax.ShapeDtypeStruct((s, TOP_K), F32),
                   jax.ShapeDtypeStruct((8, ne), F32)),
        grid=(s // tm,),
        in_specs=[pl.BlockSpec((tm, wa), row), pl.BlockSpec((tm, wa), row), pl.BlockSpec((tm, d), row),
                  pl.BlockSpec((wa, d), fix), pl.BlockSpec((wa, d), fix),
                  vec, vec, vec, vec, vec,
                  pl.BlockSpec((2, d, ne), lambda i: (0, 0, 0)), pl.BlockSpec((1, ne), fix)],
        out_specs=(pl.BlockSpec((tm, d), row), pl.BlockSpec((tm * SUBLANES, LANES), row),
                   pl.BlockSpec((2 * TOP_K, tm), lambda i: (0, i)),
                   pl.BlockSpec((tm, TOP_K), row), pl.BlockSpec((8, ne), fix)),
        scratch_shapes=[pltpu.VMEM((1, ne), F32)],
        compiler_params=_cparams(("arbitrary",)),
        name="outproj",
    )(mix_a, mix_b, x2, w_a, w_b, gpost, gt, gpre, sc, sh, w_router_hl, b_router)


def _dest_kernel(route_ref, pstart_ref, o_ref):
    route = route_ref[...]
    tm = route.shape[1]
    ne = pstart_ref.shape[0]
    expert = lax.broadcasted_iota(I32, (ne, tm), 0)
    for k in range(TOP_K):
        start = jnp.sum(jnp.where(expert == route[k:k + 1, :], pstart_ref[...], 0), axis=0, keepdims=True)
        o_ref[k:k + 1, :] = start + route[TOP_K + k:TOP_K + k + 1, :]


def _dest(route, pstart):
    s = route.shape[1]
    tm = min(4 * ROW_TILE, s)
    return pl.pallas_call(
        _dest_kernel,
        out_shape=jax.ShapeDtypeStruct((TOP_K, s), I32),
        grid=(s // tm,),
        in_specs=[pl.BlockSpec((2 * TOP_K, tm), lambda i: (0, i)),
                  pl.BlockSpec((N_EXPERTS, 1), lambda i: (0, 0))],
        out_specs=pl.BlockSpec((TOP_K, tm), lambda i: (0, i)),
        compiler_params=_cparams(("arbitrary",)),
        name="dest",
    )(route, pstart.reshape(-1, 1))


def _dispatch_kernel(pstart_ref, pblk_ref, nu_ref, dest_ref, h_ref, xs_ref,
                     zero_scr, sem, zsem):
    tm = h_ref.shape[0] // SUBLANES
    bm = zero_scr.shape[0] // SUBLANES
    n_blk = xs_ref.shape[0] // (bm * SUBLANES)

    rows = _tile_rows

    @pl.when(pl.program_id(0) == 0)
    def _():
        zero_scr[...] = jnp.zeros_like(zero_scr)

        def zero_copy(row0):
            return pltpu.make_async_copy(zero_scr, xs_ref.at[rows(row0, bm)], zsem)

        for phase in range(2):
            for e in range(N_EXPERTS):
                last = pstart_ref[e] + (pblk_ref[e] - 1) * bm
                tail = (n_blk - N_EXPERTS + e) * bm
                for cond, row0 in ((pblk_ref[e] > 0, last), (n_blk - N_EXPERTS + e >= nu_ref[0], tail)):
                    @pl.when(cond)
                    def _():
                        if phase == 0:
                            zero_copy(row0).start()
                        else:
                            zero_copy(row0).wait()

    def row_copy(r, k):
        dst = dest_ref[k, r]
        return pltpu.make_async_copy(h_ref.at[rows(r, 1)], xs_ref.at[rows(dst, 1)], sem)

    def start(r, carry):
        for k in range(TOP_K):
            row_copy(r, k).start(priority=k % 2)
        return carry

    lax.fori_loop(0, tm, start, 0, unroll=4)
    for k in range(TOP_K):
        pltpu.make_async_copy(h_ref, xs_ref.at[rows(0, tm)], sem).wait()


def _dispatch(pstart, pblk, n_used, dest, h2, n_rows):
    s = h2.shape[0] // SUBLANES
    tm = COMBINE_TILE
    grid_spec = pltpu.PrefetchScalarGridSpec(
        num_scalar_prefetch=3,
        grid=(s // tm,),
        in_specs=[pl.BlockSpec((TOP_K, tm), lambda i, *_: (0, i), memory_space=pltpu.SMEM),
                  pl.BlockSpec((tm * SUBLANES, LANES), lambda i, *_: (i, 0))],
        out_specs=pl.BlockSpec(memory_space=pl.ANY),
        scratch_shapes=[pltpu.VMEM((EXPERT_BLOCK * SUBLANES, LANES), F32), pltpu.SemaphoreType.DMA,
                        pltpu.SemaphoreType.DMA],
    )
    return pl.pallas_call(
        _dispatch_kernel,
        out_shape=jax.ShapeDtypeStruct((n_rows * SUBLANES, LANES), F32),
        grid_spec=grid_spec,
        compiler_params=_cparams(("arbitrary",)),
        name="dispatch",
    )(pstart, pblk, n_used, dest, h2)


def _experts_kernel(be_ref, nu_ref, xs_ref, wgu_ref, bgu_ref, wd_ref, bd_ref, y_ref,
                    wgu_bf, wd_bf):
    b = pl.program_id(0)
    d_exp = wd_ref.shape[1]
    prev = be_ref[jnp.maximum(b - 1, 0)]
    changed = (b == 0) | (be_ref[b] != prev)

    @pl.when((b < nu_ref[0]) & changed)
    def _():
        rows = LANES

        def cast_gu(c, carry):
            r0 = pl.multiple_of(c * rows, rows)
            wgu_bf[pl.ds(r0, rows), :] = wgu_ref[0, pl.ds(r0, rows), :].astype(BF16)
            return carry

        def cast_d(c, carry):
            r0 = pl.multiple_of(c * rows, rows)
            wd_bf[pl.ds(r0, rows), :] = wd_ref[0, pl.ds(r0, rows), :].astype(BF16)
            return carry

        lax.fori_loop(0, wgu_ref.shape[1] // rows, cast_gu, 0)
        lax.fori_loop(0, wd_ref.shape[1] // rows, cast_d, 0)

    @pl.when(b < nu_ref[0])
    def _():
        bm = xs_ref.shape[0] // SUBLANES
        xb = _load_token_tiles(xs_ref, bm).astype(BF16)
        hdn = jnp.dot(xb, wgu_bf[...], preferred_element_type=F32) + bgu_ref[0]
        x_glu = jnp.minimum(hdn[:, :d_exp], SWIGLU_LIMIT)
        x_lin = jnp.clip(hdn[:, d_exp:], -SWIGLU_LIMIT, SWIGLU_LIMIT)
        act = x_glu * jax.nn.sigmoid(SWIGLU_ALPHA * x_glu) * (x_lin + 1.0)
        _store_token_tiles(y_ref, jnp.dot(act.astype(BF16), wd_bf[...], preferred_element_type=F32)
                           + bd_ref[0])

    @pl.when(b >= nu_ref[0])
    def _():
        y_ref[...] = jnp.zeros_like(y_ref)


def _experts(block_e, n_used, xs, w_gate_up, b_gate_up, w_down, b_down):
    n_rows = xs.shape[0] // SUBLANES
    bm = EXPERT_BLOCK
    n_blk = n_rows // bm
    ne, d, two_de = w_gate_up.shape
    de = w_down.shape[1]
    assert d == SUBLANES * LANES

    def blk(b, be, nu):
        return jnp.minimum(b, nu[0] - 1)

    grid_spec = pltpu.PrefetchScalarGridSpec(
        num_scalar_prefetch=2,
        grid=(n_blk,),
        in_specs=[pl.BlockSpec((bm * SUBLANES, LANES), lambda b, be, nu: (blk(b, be, nu), 0)),
                  pl.BlockSpec((1, d, two_de), lambda b, be, nu: (be[blk(b, be, nu)], 0, 0)),
                  pl.BlockSpec((1, 1, two_de), lambda b, be, nu: (be[blk(b, be, nu)], 0, 0)),
                  pl.BlockSpec((1, de, d), lambda b, be, nu: (be[blk(b, be, nu)], 0, 0)),
                  pl.BlockSpec((1, 1, d), lambda b, be, nu: (be[blk(b, be, nu)], 0, 0))],
        out_specs=pl.BlockSpec((bm * SUBLANES, LANES), lambda b, be, nu: (b, 0)),
        scratch_shapes=[pltpu.VMEM((d, two_de), BF16), pltpu.VMEM((de, d), BF16)],
    )
    return pl.pallas_call(
        _experts_kernel,
        out_shape=jax.ShapeDtypeStruct((n_rows * SUBLANES, LANES), F32),
        grid_spec=grid_spec,
        compiler_params=_cparams(("arbitrary",)),
        name="experts",
    )(block_e, n_used, xs, w_gate_up, b_gate_up.reshape(ne, 1, two_de), w_down, b_down.reshape(ne, 1, d))


def _combine_kernel(dest_ref, dest_next_ref, y_ref, gate_ref, x1_ref, gt_ref, gpost_ref, o_ref, buf, sem):
    i = pl.program_id(0)
    tm = x1_ref.shape[0]
    cur = i % 2

    def gather(d_ref, slot):
        def start(r, carry):
            for k in range(TOP_K):
                pltpu.make_async_copy(y_ref.at[_tile_rows(d_ref[k, r], 1)],
                                      buf.at[slot, k, _tile_rows(r, 1)], sem.at[slot]).start(priority=k % 2)
            return carry

        lax.fori_loop(0, tm, start, 0, unroll=4)

    @pl.when(i == 0)
    def _():
        gather(dest_ref, 0)

    @pl.when(i + 1 < pl.num_programs(0))
    def _():
        gather(dest_next_ref, 1 - cur)

    for k in range(TOP_K):
        pltpu.make_async_copy(y_ref.at[_tile_rows(0, tm)], buf.at[cur, k], sem.at[cur]).wait()

    gate = gate_ref[...]
    acc = gate[:, 0:1] * _load_token_tiles(buf.at[cur, 0], tm)
    for k in range(1, TOP_K):
        acc = acc + gate[:, k:k + 1] * _load_token_tiles(buf.at[cur, k], tm)
    o_ref[...] = x1_ref[...] + gt_ref[...] * (_rms(acc) * gpost_ref[...])


def _combine(dest, y, gate4, x1, gt, gpost):
    s, d = x1.shape
    tm = COMBINE_TILE // 2
    n = s // tm
    return pl.pallas_call(
        _combine_kernel,
        out_shape=jax.ShapeDtypeStruct((s, d), F32),
        grid=(n,),
        in_specs=[pl.BlockSpec((TOP_K, tm), lambda i: (0, i), memory_space=pltpu.SMEM),
                  pl.BlockSpec((TOP_K, tm), lambda i: (0, jnp.minimum(i + 1, n - 1)), memory_space=pltpu.SMEM),
                  pl.BlockSpec(memory_space=pl.ANY),
                  pl.BlockSpec((tm, TOP_K), lambda i: (i, 0)),
                  pl.BlockSpec((tm, d), lambda i: (i, 0)),
                  pl.BlockSpec((1, d), lambda i: (0, 0)),
                  pl.BlockSpec((1, d), lambda i: (0, 0))],
        out_specs=pl.BlockSpec((tm, d), lambda i: (i, 0)),
        scratch_shapes=[pltpu.VMEM((2, TOP_K, tm * SUBLANES, LANES), F32), pltpu.SemaphoreType.DMA((2,))],
        compiler_params=_cparams(("arbitrary",)),
        name="combine",
    )(dest, dest, y, gate4, x1, gt, gpost)


def _layer(x2, mod, g_pre_mix, g_post_mix, w_in, b_forget, rel_bias, w_out,
           g_pre_ffn, g_post_ffn, w_router, b_router, w_gate_up, b_gate_up, w_down, b_down):
    s, d = x2.shape
    sh_m, sc_m, gt_m, sh_f, sc_f, gt_f = [mod[:, k * d:(k + 1) * d] for k in range(6)]
    n_qkv = 3 * (N_HEADS_FOX + N_HEADS_MOBA) * HEAD_DIM
    fox_w = N_HEADS_FOX * HEAD_DIM

    w_qkv = w_in[:, :n_qkv].astype(BF16)
    w_f = w_in[:, n_qkv:].T
    qkv, cum, sel, nrm, tr, nr = _inproj(x2, g_pre_mix.reshape(1, d), sc_m, sh_m, w_qkv, w_f,
                                         b_forget.reshape(-1, 1))
    y_a = _fox(qkv, tr, nr, cum, nrm)
    y_b = _moba(qkv, tr, sel, rel_bias)

    w_out_bf = w_out.astype(BF16)
    x1, h2, route, gate4, cnt = _outproj(
        y_a, y_b, x2, w_out_bf[:fox_w], w_out_bf[fox_w:], g_post_mix.reshape(1, d), gt_m,
        g_pre_ffn.reshape(1, d), sc_f, sh_f, w_router, b_router.reshape(1, -1))

    bm = EXPERT_BLOCK
    counts = cnt[0].astype(I32)
    pblk = (counts + bm - 1) // bm
    pend_blk = jnp.cumsum(pblk)
    pstart = ((pend_blk - pblk) * bm).astype(I32)
    n_rows = s * TOP_K + N_EXPERTS * bm
    n_blk = n_rows // bm
    block_e = jnp.minimum(jnp.sum(pend_blk[None, :] <= jnp.arange(n_blk)[:, None], axis=1),
                          N_EXPERTS - 1).astype(I32)
    n_used = pend_blk[-1:].astype(I32)

    dest = _dest(route, pstart)
    xs = _dispatch(pstart, pblk.astype(I32), n_used, dest, h2, n_rows)
    y = _experts(block_e, n_used, xs, w_gate_up, b_gate_up, w_down, b_down)
    return _combine(dest, y, gate4, x1, gt_f, g_post_ffn.reshape(1, d))


def kernel(x, c, w_ada, b_ada, g_pre_mix, g_post_mix, w_in, b_forget, rel_bias, w_out, g_pre_ffn, g_post_ffn, w_router, b_router, w_gate_up, b_gate_up, w_down, b_down):
    bsz, s, d = x.shape
    depth = w_ada.shape[0]
    outs = []
    for bi in range(bsz):
        x2 = x[bi]
        for l in range(depth):
            mod = _adaln(c[bi:bi + 1], w_ada[l], b_ada[l])
            x2 = _layer(x2, mod, g_pre_mix[l], g_post_mix[l], w_in[l], b_forget[l], rel_bias, w_out[l],
                        g_pre_ffn[l], g_post_ffn[l], w_router[l], b_router[l], w_gate_up[l], b_gate_up[l],
                        w_down[l], b_down[l])
        outs.append(x2)
    return outs[0].reshape(1, s, d) if bsz == 1 else jnp.stack(outs)
```

```python
import math

import numpy as np
import jax
import jax.numpy as jnp
from jax import lax
from jax.experimental import pallas as pl
from jax.experimental.pallas import tpu as pltpu

F32 = jnp.float32
BF16 = jnp.bfloat16
I32 = jnp.int32

HEAD_DIM = 64
N_HEADS_FOX = 8
N_HEADS_MOBA = 8
PAIR = 2 * HEAD_DIM
MOBA_BLOCK = 256
MOBA_TOPK = 3
NUM_BUCKETS = 32
MAX_DISTANCE = 128
N_EXPERTS = 32
TOP_K = 4
SWIGLU_LIMIT = 7.0
SWIGLU_ALPHA = 1.702
RMS_EPS = 1e-6
NEG = -(2.0 ** 100)
M_INIT = -(2.0 ** 99)
LOG2E = math.log2(math.e)
SUM_ROWS = 16
SUBLANES = 8
LANES = 128
EXP_UNDERFLOW = 90.0
VMEM_LIMIT = 56 * 1024 * 1024

ROW_TILE = 512
FOX_TILE = 256
EXPERT_BLOCK = 512
COMBINE_TILE = 1024
MOBA_LAG = 1
MOBA_UNROLL = 4

NT_DIMS = (((1,), (1,)), ((), ()))


def _cparams(sem):
    return pltpu.CompilerParams(dimension_semantics=sem, vmem_limit_bytes=VMEM_LIMIT)


def _rms(x):
    return x * lax.rsqrt(jnp.mean(x * x, axis=-1, keepdims=True) + RMS_EPS)


def _adaln_kernel(c_ref, w_ref, b_ref, o_ref):
    c = c_ref[...]
    cond = c * jax.nn.sigmoid(c)
    o_ref[...] = jnp.dot(cond, w_ref[...], preferred_element_type=F32,
                         precision=lax.Precision.HIGHEST) + b_ref[...]


def _adaln(c, w_ada, b_ada):
    d = c.shape[-1]
    n = w_ada.shape[-1]
    c8 = jnp.broadcast_to(c.reshape(1, d), (8, d))
    out = pl.pallas_call(
        _adaln_kernel,
        out_shape=jax.ShapeDtypeStruct((8, n), F32),
        grid=(n // d,),
        in_specs=[pl.BlockSpec((8, d), lambda j: (0, 0)),
                  pl.BlockSpec((d, d), lambda j: (0, j)),
                  pl.BlockSpec((1, d), lambda j: (0, j))],
        out_specs=pl.BlockSpec((8, d), lambda j: (0, j)),
        compiler_params=_cparams(("arbitrary",)),
        name="adaln",
    )(c8, w_ada, b_ada.reshape(1, n))
    return out[0:1]


def _inproj_kernel(x_ref, g_ref, sc_ref, sh_ref, w_ref, wf_ref, bf_ref,
                   qkv_ref, cum_ref, sel_ref, nrm_ref, tr_ref, nr_ref, km_scr, carry_scr):
    i = pl.program_id(0)
    tm = x_ref.shape[0]
    nblk = km_scr.shape[0]

    @pl.when(i == 0)
    def _():
        km_scr[...] = jnp.zeros_like(km_scr)
        carry_scr[...] = jnp.zeros_like(carry_scr)

    x = x_ref[...]
    h = _rms(x) * g_ref[...] * (1.0 + sc_ref[...]) + sh_ref[...]
    hb = h.astype(BF16)

    width = N_HEADS_FOX * HEAD_DIM
    hsel = jnp.where(lax.broadcasted_iota(I32, (width, N_HEADS_FOX), 0) // HEAD_DIM
                     == lax.broadcasted_iota(I32, (width, N_HEADS_FOX), 1), 1.0, 0.0)
    kb = None
    tr_slot = {3: 0, 5: 1, 0: 2, 2: 3}
    qbt = None
    for c in range(6):
        pc = jnp.dot(hb, w_ref[:, c * width:(c + 1) * width], preferred_element_type=F32)
        if c in tr_slot:
            n = tr_slot[c]
            pct = pc.T
            if c == 3:
                qbt = pct
            if c == 0 or c == 3:
                pct = pct * (LOG2E * HEAD_DIM ** -0.5)
            tr_ref[n * width:(n + 1) * width, :] = pct.astype(BF16)
        if c == 1 or c == 4:
            qkv_ref[:, (c // 3) * width:(c // 3 + 1) * width] = pc.astype(BF16)
        if c < 2:
            sq = (pc * (HEAD_DIM ** -0.5) if c == 0 else pc).astype(BF16).astype(F32)
            n2 = jnp.dot((sq * sq).astype(BF16), hsel.astype(BF16), preferred_element_type=F32)
            nrm_ref[0, c:c + 1, :] = jnp.max(n2, axis=0, keepdims=True)
        if c == 4:
            kb = pc

    ft = lax.dot_general(wf_ref[...].astype(BF16), hb, NT_DIMS, preferred_element_type=F32)
    z = ft + bf_ref[...]
    logf = -(jnp.maximum(-z, 0.0) + jnp.log1p(jnp.exp(-jnp.abs(z))))
    lane = lax.broadcasted_iota(I32, logf.shape, 1)
    cs = logf
    sh = 1
    while sh < tm:
        cs = cs + jnp.where(lane >= sh, pltpu.roll(cs, sh, axis=1), 0.0)
        sh *= 2
    base = jnp.zeros_like(cs)
    for b in range(1, tm // FOX_TILE):
        base = jnp.where(lane >= b * FOX_TILE, cs[:, b * FOX_TILE - 1:b * FOX_TILE], base)
    nr_t = (-LOG2E * (cs - base)).T
    for pp in range(nr_ref.shape[0]):
        nr_ref[pp] = nr_t[:, 2 * pp:2 * pp + 2]
    cs = cs + carry_scr[...]
    cum_ref[...] = cs
    carry_scr[...] = cs[:, tm - 1:tm]

    nb_tile = tm // MOBA_BLOCK
    for b in range(nb_tile):
        kmean = jnp.sum(kb[b * MOBA_BLOCK:(b + 1) * MOBA_BLOCK], axis=0, keepdims=True) * (1.0 / MOBA_BLOCK)
        km_scr[pl.ds(i * nb_tile + b, 1), :] = kmean

    km = km_scr[...]
    blk = lax.broadcasted_iota(I32, (nblk, tm), 0)
    col = lax.broadcasted_iota(I32, (nblk, tm), 1)
    own = i * nb_tile + col // MOBA_BLOCK
    for hd in range(N_HEADS_MOBA):
        hs = slice(hd * HEAD_DIM, (hd + 1) * HEAD_DIM)
        g = jnp.dot(km[:, hs], qbt[hs, :], preferred_element_type=F32,
                    precision=lax.Precision.HIGHEST)
        g = jnp.where(blk < own, g, -jnp.inf)
        sel = jnp.zeros(g.shape, dtype=jnp.bool_)
        for _ in range(MOBA_TOPK):
            m = jnp.max(g, axis=0, keepdims=True)
            first = jnp.min(jnp.where(g == m, blk, nblk), axis=0, keepdims=True)
            pick = (blk == first) & (m > -jnp.inf)
            sel = sel | pick
            g = jnp.where(pick, -jnp.inf, g)
        sel_ref[hd] = jnp.where(sel, 0.0, NEG)


def _inproj(x2, g, sc, sh, w_qkv, w_f, b_f):
    s, d = x2.shape
    tm = ROW_TILE
    nblk = s // MOBA_BLOCK
    n = w_qkv.shape[1]
    width = N_HEADS_FOX * HEAD_DIM
    n_rows_out = 2 * width
    n_tr = 4 * width
    assert tm % FOX_TILE == 0 and N_HEADS_FOX == N_HEADS_MOBA
    return pl.pallas_call(
        _inproj_kernel,
        out_shape=(jax.ShapeDtypeStruct((s, n_rows_out), BF16),
                   jax.ShapeDtypeStruct((N_HEADS_FOX, s), F32),
                   jax.ShapeDtypeStruct((N_HEADS_MOBA, nblk, s), F32),
                   jax.ShapeDtypeStruct((s // tm, 2, N_HEADS_FOX), F32),
                   jax.ShapeDtypeStruct((n_tr, s), BF16),
                   jax.ShapeDtypeStruct((N_HEADS_FOX // 2, s, 2), F32)),
        grid=(s // tm,),
        in_specs=[pl.BlockSpec((tm, d), lambda i: (i, 0)),
                  pl.BlockSpec((1, d), lambda i: (0, 0)),
                  pl.BlockSpec((1, d), lambda i: (0, 0)),
                  pl.BlockSpec((1, d), lambda i: (0, 0)),
                  pl.BlockSpec((d, n), lambda i: (0, 0)),
                  pl.BlockSpec((N_HEADS_FOX, d), lambda i: (0, 0)),
                  pl.BlockSpec((N_HEADS_FOX, 1), lambda i: (0, 0))],
        out_specs=(pl.BlockSpec((tm, n_rows_out), lambda i: (i, 0)),
                   pl.BlockSpec((N_HEADS_FOX, tm), lambda i: (0, i)),
                   pl.BlockSpec((N_HEADS_MOBA, nblk, tm), lambda i: (0, 0, i)),
                   pl.BlockSpec((1, 2, N_HEADS_FOX), lambda i: (i, 0, 0)),
                   pl.BlockSpec((n_tr, tm), lambda i: (0, i)),
                   pl.BlockSpec((N_HEADS_FOX // 2, tm, 2), lambda i: (0, i, 0))),
        scratch_shapes=[pltpu.VMEM((nblk, N_HEADS_MOBA * HEAD_DIM), F32),
                        pltpu.VMEM((N_HEADS_FOX, 1), F32)],
        compiler_params=_cparams(("arbitrary",)),
        name="inproj",
    )(x2, g, sc, sh, w_qkv, w_f, b_f)


def _fox_kernel(jlo_ref, cp_ref, qt_ref, k_ref, vt_ref, nr_ref, o_ref, m_ref, a_ref):
    p = pl.program_id(0)
    g = pl.program_id(1)
    j_first = jlo_ref[p * pl.num_programs(1) + g]
    tq = qt_ref.shape[1]
    t = tq // 2

    m_ref[...] = jnp.full(m_ref.shape, M_INIT, F32)
    a_ref[...] = jnp.zeros(a_ref.shape, F32)
    ones_rows = jnp.where(lax.broadcasted_iota(I32, (SUM_ROWS, t), 0) == 0, 1.0, 0.0).astype(BF16)
    in_a = lax.broadcasted_iota(I32, (1, tq), 1) < t
    krow = lax.broadcasted_iota(I32, (t, tq), 0)
    qcol = lax.broadcasted_iota(I32, (t, tq), 1)
    future_a = (qcol < t) & (krow > qcol)
    future_b = (qcol >= t) & (krow > qcol - t)

    def scores(j, future):
        k0 = pl.multiple_of(j * t, t)
        out = []
        for hd in range(2):
            kh = k_ref[pl.ds(k0, t), hd * HEAD_DIM:(hd + 1) * HEAD_DIM]
            qh = qt_ref[hd * HEAD_DIM:(hd + 1) * HEAD_DIM, :]
            s = jnp.dot(kh, qh, preferred_element_type=F32) + nr_ref[0, pl.ds(k0, t), hd:hd + 1]
            out.append(s if future is None else jnp.where(future, NEG, s))
        return out

    def shifts(j, a_on, b_on):
        rows = []
        for hd in range(2):
            h = 2 * p + hd
            sa = LOG2E * (cp_ref[h, 2 * g] - cp_ref[h, j])
            sb = LOG2E * (cp_ref[h, 2 * g + 1] - cp_ref[h, j])
            rows.append(jnp.where(in_a, jnp.where(a_on, sa, NEG), jnp.where(b_on, sb, NEG)))
        return rows

    def softmax(ss, shift_rows):
        ps, alphas = [], []
        for hd in range(2):
            m_old = m_ref[hd]
            m_new = jnp.maximum(m_old, jnp.max(ss[hd], axis=0, keepdims=True) + shift_rows[hd])
            alphas.append(jnp.exp2(m_old - m_new))
            m_ref[hd] = m_new
            ps.append(jnp.exp2(ss[hd] - (m_new - shift_rows[hd])).astype(BF16))
        return ps, alphas

    def accumulate(j, ps, alphas):
        k0 = pl.multiple_of(j * t, t)
        for hd in range(2):
            vh = vt_ref[hd * HEAD_DIM:(hd + 1) * HEAD_DIM, pl.ds(k0, t)]
            vcat = jnp.concatenate([vh, ones_rows], axis=0)
            a_ref[hd] = a_ref[hd] * alphas[hd] + jnp.dot(vcat, ps[hd], preferred_element_type=F32)

    def body(j, carry):
        accumulate(j, *softmax(scores(j, None), shifts(j, True, True)))
        return carry

    lax.fori_loop(j_first, 2 * g - 1, body, 0)
    j1 = jnp.maximum(2 * g - 1, 0)
    j2 = 2 * g
    j3 = 2 * g + 1
    ss1 = scores(j1, None)
    ss2 = scores(j2, future_a)
    ss3 = scores(j3, future_b)
    w1 = softmax(ss1, shifts(j1, g >= 1, g >= 1))
    w2 = softmax(ss2, shifts(j2, True, True))
    accumulate(j1, *w1)
    w3 = softmax(ss3, shifts(j3, False, True))
    accumulate(j2, *w2)
    accumulate(j3, *w3)
    out_t = jnp.concatenate([a_ref[hd, 0:HEAD_DIM, :] / a_ref[hd, HEAD_DIM:HEAD_DIM + 1, :]
                             for hd in range(2)], axis=0)
    o_ref[...] = out_t.T.astype(o_ref.dtype)


def _fox_first_tile(cum, nrm, t):
    cend = cum[:, t - 1::t]
    nt = cend.shape[1]
    cprev = jnp.concatenate([jnp.zeros((cend.shape[0], 1), F32), cend[:, :-1]], axis=1)
    rep = nt // nrm.shape[0]
    qn = jnp.repeat(jnp.sqrt(nrm[:, 0, :]).T, rep, axis=1)
    kn = jnp.repeat(jnp.sqrt(nrm[:, 1, :]).T, rep, axis=1)
    gap = (1.02 * qn[:, :, None] * (kn[:, None, :] + kn[:, :, None])
           + cprev[:, :, None] - cend[:, None, :])
    jj = jnp.arange(nt)[None, None, :]
    ii = jnp.arange(nt)[None, :, None]
    needed = (jj < ii) & jnp.logical_not(gap <= -EXP_UNDERFLOW)
    needed = needed[0::2] | needed[1::2]
    first = jnp.min(jnp.where(needed, jj, ii), axis=2)
    return first.reshape(-1).astype(I32)


def _fox(qkv, tr, nr, cum, nrm):
    s = qkv.shape[0]
    t = FOX_TILE
    npair = N_HEADS_FOX // 2
    tq = 2 * t
    jlo = _fox_first_tile(cum, nrm, t).reshape(npair, s // tq, 2).min(axis=2).reshape(-1)
    cend = cum[:, t - 1::t]
    cprev = jnp.concatenate([jnp.zeros((cend.shape[0], 1), F32), cend[:, :-1]], axis=1)
    q_rows = 2 * (N_HEADS_MOBA // 2)
    grid_spec = pltpu.PrefetchScalarGridSpec(
        num_scalar_prefetch=1,
        grid=(npair, s // tq),
        in_specs=[pl.BlockSpec(memory_space=pltpu.SMEM),
                  pl.BlockSpec((PAIR, tq), lambda p, i, jl: (q_rows + p, i)),
                  pl.BlockSpec((s, PAIR), lambda p, i, jl: (0, p)),
                  pl.BlockSpec((PAIR, s), lambda p, i, jl: (q_rows + npair + p, 0)),
                  pl.BlockSpec((1, s, 2), lambda p, i, jl: (p, 0, 0))],
        out_specs=pl.BlockSpec((tq, PAIR), lambda p, i, jl: (i, p)),
        scratch_shapes=[pltpu.VMEM((2, 1, tq), F32), pltpu.VMEM((2, HEAD_DIM + SUM_ROWS, tq), F32)],
    )
    return pl.pallas_call(
        _fox_kernel,
        out_shape=jax.ShapeDtypeStruct((s, npair * PAIR), BF16),
        grid_spec=grid_spec,
        compiler_params=_cparams(("arbitrary", "arbitrary")),
        name="fox",
    )(jlo, cprev, tr, qkv, tr, nr)


def _t5_bucket_np(dist):
    dist = np.maximum(dist, 0)
    max_exact = NUM_BUCKETS // 2
    d = np.maximum(dist, 1).astype(np.float32)
    large = max_exact + (np.log(d / np.float32(max_exact)) / np.float32(math.log(MAX_DISTANCE / max_exact))
                         * np.float32(NUM_BUCKETS - max_exact)).astype(np.int32)
    large = np.minimum(large, NUM_BUCKETS - 1)
    return np.where(dist < max_exact, dist, large).astype(np.int32)


def _moba_kernel(rb_ref, qt_ref, k_ref, vt_ref, sel_ref, bkt_ref, o_ref,
                 m_ref, a_ref, bias_scr, *bufs):
    s_bufs = bufs[0:MOBA_UNROLL]
    p_bufs = bufs[MOBA_UNROLL:2 * MOBA_UNROLL]
    al_bufs = bufs[2 * MOBA_UNROLL:3 * MOBA_UNROLL]
    p = pl.program_id(0)
    g = pl.program_id(1)
    tq = qt_ref.shape[1]
    t = tq // 2

    @pl.when(g == 0)
    def _():
        r = lax.broadcasted_iota(I32, (t, t), 0)
        c = lax.broadcasted_iota(I32, (t, t), 1)
        zero = jnp.zeros((t, t), F32)
        for hd in range(2):
            h = 2 * p + hd
            far = rb_ref[(NUM_BUCKETS - 1) * N_HEADS_MOBA + h]
            tiles = []
            for w in range(2):
                bkt = bkt_ref[w]
                acc = jnp.zeros(bkt.shape, F32)
                for kk in range(NUM_BUCKETS):
                    acc = acc + jnp.where(bkt == kk, rb_ref[kk * N_HEADS_MOBA + h], 0.0)
                tiles.append((acc - far) * LOG2E)
            prev_t = tiles[0]
            own_t = jnp.where(r <= c, tiles[1], NEG)
            bias_scr[hd, 0] = jnp.concatenate([prev_t, zero], axis=1)
            bias_scr[hd, 1] = jnp.concatenate([own_t, prev_t], axis=1)
            bias_scr[hd, 2] = jnp.concatenate([zero, own_t], axis=1)

    m_ref[...] = jnp.full(m_ref.shape, M_INIT, F32)
    a_ref[...] = jnp.zeros(a_ref.shape, F32)

    ones_rows = jnp.where(lax.broadcasted_iota(I32, (SUM_ROWS, t), 0) == 0, 1.0, 0.0).astype(BF16)
    in_a = lax.broadcasted_iota(I32, (1, tq), 1) < t

    n_far = jnp.maximum(2 * g - 1, 0)

    n_key_tiles = k_ref.shape[0] // t

    def produce(j, s_buf):
        j = jnp.minimum(j, n_key_tiles - 1)
        k0 = pl.multiple_of(j * t, t)
        for hd in range(2):
            kh = k_ref[pl.ds(k0, t), hd * HEAD_DIM:(hd + 1) * HEAD_DIM]
            qh = qt_ref[hd * HEAD_DIM:(hd + 1) * HEAD_DIM, :]
            s_buf[hd] = jnp.dot(kh, qh, preferred_element_type=F32)

    def softmax(s_buf, p_buf, al_buf, selrows, w):
        for hd in range(2):
            s = s_buf[hd]
            if w is not None:
                s = s + bias_scr[hd, w]
            smax = jnp.max(s, axis=0, keepdims=True)
            m_old = m_ref[hd]
            m_new = jnp.maximum(m_old, smax + selrows[hd])
            shift = m_new - selrows[hd]
            al_buf[hd] = jnp.exp2(m_old - m_new)
            m_ref[hd] = m_new
            p_buf[hd * t:(hd + 1) * t, :] = jnp.exp2(s - shift).astype(BF16)

    def far_rows(j):
        return [jnp.where(j < n_far, sel_ref[hd, pl.ds(j, 1), :], NEG) for hd in range(2)]

    def accumulate(j, p_buf, al_buf):
        k0 = pl.multiple_of(j * t, t)
        for hd in range(2):
            vh = vt_ref[hd * HEAD_DIM:(hd + 1) * HEAD_DIM, pl.ds(k0, t)]
            vcat = jnp.concatenate([vh, ones_rows], axis=0)
            a_ref[hd] = (a_ref[hd] * al_buf[hd]
                         + jnp.dot(vcat, p_buf[hd * t:(hd + 1) * t, :], preferred_element_type=F32))

    un = MOBA_UNROLL
    lag = MOBA_LAG
    for n in range(un - lag, un):
        p_bufs[n][...] = jnp.zeros_like(p_bufs[n])
        al_bufs[n][...] = jnp.ones_like(al_bufs[n])
    produce(0, s_bufs[0])

    def body(u, carry):
        j0 = un * u
        for n in range(un):
            accumulate(jnp.maximum(j0 + n - lag, 0), p_bufs[(n - lag) % un], al_bufs[(n - lag) % un])
            softmax(s_bufs[n], p_bufs[n], al_bufs[n], far_rows(j0 + n), None)
            produce(j0 + n + 1, s_bufs[(n + 1) % un])
        return carry

    n_trips = (n_far + un - 1) // un
    lax.fori_loop(0, n_trips, body, 0)

    j1 = jnp.maximum(2 * g - 1, 0)
    j2 = 2 * g
    j3 = 2 * g + 1
    rows1 = [jnp.where(g >= 1, sel_ref[hd, pl.ds(j1, 1), :], NEG) for hd in range(2)]
    rows2 = [jnp.where(in_a, 0.0, sel_ref[hd, pl.ds(j2, 1), :]) for hd in range(2)]
    rows3 = [jnp.where(in_a, NEG, 0.0)] * 2
    produce(j1, s_bufs[0])
    produce(j2, s_bufs[1])
    for n in range(un - lag, un):
        accumulate(jnp.maximum(un * n_trips - un + n, 0), p_bufs[n], al_bufs[n])
    softmax(s_bufs[0], p_bufs[0], al_bufs[0], rows1, 0)
    produce(j3, s_bufs[0])
    softmax(s_bufs[1], p_bufs[1], al_bufs[1], rows2, 1)
    accumulate(j1, p_bufs[0], al_bufs[0])
    softmax(s_bufs[0], p_bufs[2], al_bufs[2], rows3, 2)
    accumulate(j2, p_bufs[1], al_bufs[1])
    accumulate(j3, p_bufs[2], al_bufs[2])
    out_t = jnp.concatenate([a_ref[hd, 0:HEAD_DIM, :] / a_ref[hd, HEAD_DIM:HEAD_DIM + 1, :]
                             for hd in range(2)], axis=0)
    o_ref[...] = out_t.T.astype(o_ref.dtype)


def _moba(qkv, tr, sel, rel_bias):
    s = qkv.shape[0]
    t = MOBA_BLOCK
    npair = N_HEADS_MOBA // 2
    nblk = s // t
    kcol = N_HEADS_FOX // 2
    a = np.arange(t)[None, :]
    b = np.arange(t)[:, None]
    bkt = jnp.asarray(np.stack([_t5_bucket_np(t + a - b), _t5_bucket_np(a - b)]))
    tq = 2 * t
    grid_spec = pltpu.PrefetchScalarGridSpec(
        num_scalar_prefetch=1,
        grid=(npair, s // tq),
        in_specs=[pl.BlockSpec((PAIR, tq), lambda p, i, rb: (p, i)),
                  pl.BlockSpec((s, PAIR), lambda p, i, rb: (0, kcol + p)),
                  pl.BlockSpec((PAIR, s), lambda p, i, rb: (npair + p, 0)),
                  pl.BlockSpec((2, nblk, tq), lambda p, i, rb: (p, 0, i)),
                  pl.BlockSpec((2, t, t), lambda p, i, rb: (0, 0, 0))],
        out_specs=pl.BlockSpec((tq, PAIR), lambda p, i, rb: (i, p)),
        scratch_shapes=[pltpu.VMEM((2, 1, tq), F32),
                        pltpu.VMEM((2, HEAD_DIM + SUM_ROWS, tq), F32), pltpu.VMEM((2, 3, t, tq), F32)]
        + [pltpu.VMEM((2, t, tq), F32)] * MOBA_UNROLL
        + [pltpu.VMEM((2 * t, tq), BF16)] * MOBA_UNROLL
        + [pltpu.VMEM((2, 1, tq), F32)] * MOBA_UNROLL,
    )
    return pl.pallas_call(
        _moba_kernel,
        out_shape=jax.ShapeDtypeStruct((s, npair * PAIR), BF16),
        grid_spec=grid_spec,
        compiler_params=_cparams(("arbitrary", "arbitrary")),
        name="moba",
    )(rel_bias.reshape(-1), tr, qkv, tr, sel, bkt)


def _store_token_tiles(ref, val):
    n = val.shape[0]
    for c in range(SUBLANES):
        ref[pl.ds(c, n, stride=SUBLANES), :] = val[:, c * LANES:(c + 1) * LANES]


def _load_token_tiles(ref, n):
    return jnp.concatenate([ref[pl.ds(c, n, stride=SUBLANES), :] for c in range(SUBLANES)], axis=1)


def _tile_rows(r0, n):
    start = r0 * SUBLANES
    if not isinstance(start, int):
        start = pl.multiple_of(start, SUBLANES)
    return pl.ds(start, n * SUBLANES)


def _outproj_kernel(ya_ref, yb_ref, x_ref, wa_ref, wb_ref, gpost_ref, gt_ref, gpre_ref,
                    sc_ref, sh_ref, wr_ref, br_ref,
                    x1_ref, h2_ref, route_ref, gate_ref, cnt_ref, carry_scr):
    i = pl.program_id(0)
    tm = x_ref.shape[0]

    @pl.when(i == 0)
    def _():
        carry_scr[...] = jnp.zeros_like(carry_scr)

    y = (jnp.dot(ya_ref[...], wa_ref[...], preferred_element_type=F32)
         + jnp.dot(yb_ref[...], wb_ref[...], preferred_element_type=F32))
    x1 = x_ref[...] + gt_ref[...] * (_rms(y) * gpost_ref[...])
    x1_ref[...] = x1
    h2 = _rms(x1) * gpre_ref[...] * (1.0 + sc_ref[...]) + sh_ref[...]
    _store_token_tiles(h2_ref, h2)

    h_hi = h2.astype(BF16)
    h_lo = (h2 - h_hi.astype(F32)).astype(BF16)
    logits = (jnp.dot(h_hi, wr_ref[0], preferred_element_type=F32)
              + jnp.dot(h_hi, wr_ref[1], preferred_element_type=F32)
              + jnp.dot(h_lo, wr_ref[0], preferred_element_type=F32)) + br_ref[...]
    ne = logits.shape[1]
    lane = lax.broadcasted_iota(I32, logits.shape, 1)
    lane4 = lax.broadcasted_iota(I32, (tm, TOP_K), 1)
    g = logits
    mask = jnp.zeros(logits.shape, F32)
    vals, picks = [], []
    for _ in range(TOP_K):
        m = jnp.max(g, axis=1, keepdims=True)
        first = jnp.min(jnp.where(g == m, lane, ne), axis=1, keepdims=True)
        pick = lane == first
        mask = jnp.where(pick, 1.0, mask)
        g = jnp.where(pick, -jnp.inf, g)
        vals.append(m)
        picks.append(pick)
    ex = [jnp.exp(v - vals[0]) for v in vals]
    den = ex[0] + ex[1] + ex[2] + ex[3]
    gates = [e / den for e in ex]

    r = lax.broadcasted_iota(I32, (tm, tm), 0)
    c = lax.broadcasted_iota(I32, (tm, tm), 1)
    tril = jnp.where(c < r, 1.0, 0.0).astype(BF16)
    before = jnp.dot(tril, mask.astype(BF16), preferred_element_type=F32) + carry_scr[...]
    total = carry_scr[...] + jnp.sum(mask, axis=0, keepdims=True)
    carry_scr[...] = total
    cnt_ref[...] = jnp.broadcast_to(total, cnt_ref.shape)

    def pack4(cols):
        return jnp.where(lane4 == 0, cols[0],
                         jnp.where(lane4 == 1, cols[1], jnp.where(lane4 == 2, cols[2], cols[3])))

    wide = lax.broadcasted_iota(I32, (tm, LANES), 1)
    route = jnp.zeros((tm, LANES), F32)
    for k, pk in enumerate(picks):
        idx_k = jnp.sum(jnp.where(pk, lane, 0), axis=1, keepdims=True).astype(F32)
        rank_k = jnp.sum(jnp.where(pk, before, 0.0), axis=1, keepdims=True)
        route = jnp.where(wide == k, idx_k, jnp.where(wide == TOP_K + k, rank_k, route))
    route_ref[...] = route.T[0:2 * TOP_K, :].astype(I32)
    gate_ref[...] = pack4(gates)


def _outproj(mix_a, mix_b, x2, w_a, w_b, gpost, gt, gpre, sc, sh, w_router, b_router):
    s, d = x2.shape
    tm = ROW_TILE
    ne = w_router.shape[1]
    wa = mix_a.shape[1]
    w_hi = w_router.astype(BF16)
    w_router_hl = jnp.stack([w_hi, (w_router - w_hi.astype(F32)).astype(BF16)])
    row = lambda i: (i, 0)
    fix = lambda i: (0, 0)
    vec = pl.BlockSpec((1, d), fix)
    return pl.pallas_call(
        _outproj_kernel,
        out_shape=(jax.ShapeDtypeStruct((s, d), F32),
                   jax.ShapeDtypeStruct((s * SUBLANES, LANES), F32),
                   jax.ShapeDtypeStruct((2 * TOP_K, s), I32),
                   jax.ShapeDtypeStruct((s, TOP_K), F32),
                   jax.ShapeDtypeStruct((8, ne), F32)),
        grid=(s // tm,),
        in_specs=[pl.BlockSpec((tm, wa), row), pl.BlockSpec((tm, wa), row), pl.BlockSpec((tm, d), row),
                  pl.BlockSpec((wa, d), fix), pl.BlockSpec((wa, d), fix),
                  vec, vec, vec, vec, vec,
                  pl.BlockSpec((2, d, ne), lambda i: (0, 0, 0)), pl.BlockSpec((1, ne), fix)],
        out_specs=(pl.BlockSpec((tm, d), row), pl.BlockSpec((tm * SUBLANES, LANES), row),
                   pl.BlockSpec((2 * TOP_K, tm), lambda i: (0, i)),
                   pl.BlockSpec((tm, TOP_K), row), pl.BlockSpec((8, ne), fix)),
        scratch_shapes=[pltpu.VMEM((1, ne), F32)],
        compiler_params=_cparams(("arbitrary",)),
        name="outproj",
    )(mix_a, mix_b, x2, w_a, w_b, gpost, gt, gpre, sc, sh, w_router_hl, b_router)


def _dest_kernel(route_ref, pstart_ref, o_ref):
    route = route_ref[...]
    tm = route.shape[1]
    ne = pstart_ref.shape[0]
    expert = lax.broadcasted_iota(I32, (ne, tm), 0)
    for k in range(TOP_K):
        start = jnp.sum(jnp.where(expert == route[k:k + 1, :], pstart_ref[...], 0), axis=0, keepdims=True)
        o_ref[k:k + 1, :] = start + route[TOP_K + k:TOP_K + k + 1, :]


def _dest(route, pstart):
    s = route.shape[1]
    tm = min(4 * ROW_TILE, s)
    return pl.pallas_call(
        _dest_kernel,
        out_shape=jax.ShapeDtypeStruct((TOP_K, s), I32),
        grid=(s // tm,),
        in_specs=[pl.BlockSpec((2 * TOP_K, tm), lambda i: (0, i)),
                  pl.BlockSpec((N_EXPERTS, 1), lambda i: (0, 0))],
        out_specs=pl.BlockSpec((TOP_K, tm), lambda i: (0, i)),
        compiler_params=_cparams(("arbitrary",)),
        name="dest",
    )(route, pstart.reshape(-1, 1))


def _dispatch_kernel(pstart_ref, pblk_ref, nu_ref, dest_ref, h_ref, xs_ref,
                     zero_scr, sem, zsem):
    tm = h_ref.shape[0] // SUBLANES
    bm = zero_scr.shape[0] // SUBLANES
    n_blk = xs_ref.shape[0] // (bm * SUBLANES)

    rows = _tile_rows

    @pl.when(pl.program_id(0) == 0)
    def _():
        zero_scr[...] = jnp.zeros_like(zero_scr)

        def zero_copy(row0):
            return pltpu.make_async_copy(zero_scr, xs_ref.at[rows(row0, bm)], zsem)

        for phase in range(2):
            for e in range(N_EXPERTS):
                last = pstart_ref[e] + (pblk_ref[e] - 1) * bm
                tail = (n_blk - N_EXPERTS + e) * bm
                for cond, row0 in ((pblk_ref[e] > 0, last), (n_blk - N_EXPERTS + e >= nu_ref[0], tail)):
                    @pl.when(cond)
                    def _():
                        if phase == 0:
                            zero_copy(row0).start()
                        else:
                            zero_copy(row0).wait()

    def row_copy(r, k):
        dst = dest_ref[k, r]
        return pltpu.make_async_copy(h_ref.at[rows(r, 1)], xs_ref.at[rows(dst, 1)], sem)

    def start(r, carry):
        for k in range(TOP_K):
            row_copy(r, k).start(priority=k % 2)
        return carry

    lax.fori_loop(0, tm, start, 0, unroll=4)
    for k in range(TOP_K):
        pltpu.make_async_copy(h_ref, xs_ref.at[rows(0, tm)], sem).wait()


def _dispatch(pstart, pblk, n_used, dest, h2, n_rows):
    s = h2.shape[0] // SUBLANES
    tm = COMBINE_TILE
    grid_spec = pltpu.PrefetchScalarGridSpec(
        num_scalar_prefetch=3,
        grid=(s // tm,),
        in_specs=[pl.BlockSpec((TOP_K, tm), lambda i, *_: (0, i), memory_space=pltpu.SMEM),
                  pl.BlockSpec((tm * SUBLANES, LANES), lambda i, *_: (i, 0))],
        out_specs=pl.BlockSpec(memory_space=pl.ANY),
        scratch_shapes=[pltpu.VMEM((EXPERT_BLOCK * SUBLANES, LANES), F32), pltpu.SemaphoreType.DMA,
                        pltpu.SemaphoreType.DMA],
    )
    return pl.pallas_call(
        _dispatch_kernel,
        out_shape=jax.ShapeDtypeStruct((n_rows * SUBLANES, LANES), F32),
        grid_spec=grid_spec,
        compiler_params=_cparams(("arbitrary",)),
        name="dispatch",
    )(pstart, pblk, n_used, dest, h2)


def _experts_kernel(be_ref, nu_ref, xs_ref, wgu_ref, bgu_ref, wd_ref, bd_ref, y_ref,
                    wgu_bf, wd_bf):
    b = pl.program_id(0)
    d_exp = wd_ref.shape[1]
    prev = be_ref[jnp.maximum(b - 1, 0)]
    changed = (b == 0) | (be_ref[b] != prev)

    @pl.when((b < nu_ref[0]) & changed)
    def _():
        rows = LANES

        def cast_gu(c, carry):
            r0 = pl.multiple_of(c * rows, rows)
            wgu_bf[pl.ds(r0, rows), :] = wgu_ref[0, pl.ds(r0, rows), :].astype(BF16)
            return carry

        def cast_d(c, carry):
            r0 = pl.multiple_of(c * rows, rows)
            wd_bf[pl.ds(r0, rows), :] = wd_ref[0, pl.ds(r0, rows), :].astype(BF16)
            return carry

        lax.fori_loop(0, wgu_ref.shape[1] // rows, cast_gu, 0)
        lax.fori_loop(0, wd_ref.shape[1] // rows, cast_d, 0)

    @pl.when(b < nu_ref[0])
    def _():
        bm = xs_ref.shape[0] // SUBLANES
        xb = _load_token_tiles(xs_ref, bm).astype(BF16)
        hdn = jnp.dot(xb, wgu_bf[...], preferred_element_type=F32) + bgu_ref[0]
        x_glu = jnp.minimum(hdn[:, :d_exp], SWIGLU_LIMIT)
        x_lin = jnp.clip(hdn[:, d_exp:], -SWIGLU_LIMIT, SWIGLU_LIMIT)
        act = x_glu * jax.nn.sigmoid(SWIGLU_ALPHA * x_glu) * (x_lin + 1.0)
        _store_token_tiles(y_ref, jnp.dot(act.astype(BF16), wd_bf[...], preferred_element_type=F32)
                           + bd_ref[0])

    @pl.when(b >= nu_ref[0])
    def _():
        y_ref[...] = jnp.zeros_like(y_ref)


def _experts(block_e, n_used, xs, w_gate_up, b_gate_up, w_down, b_down):
    n_rows = xs.shape[0] // SUBLANES
    bm = EXPERT_BLOCK
    n_blk = n_rows // bm
    ne, d, two_de = w_gate_up.shape
    de = w_down.shape[1]
    assert d == SUBLANES * LANES

    def blk(b, be, nu):
        return jnp.minimum(b, nu[0] - 1)

    grid_spec = pltpu.PrefetchScalarGridSpec(
        num_scalar_prefetch=2,
        grid=(n_blk,),
        in_specs=[pl.BlockSpec((bm * SUBLANES, LANES), lambda b, be, nu: (blk(b, be, nu), 0)),
                  pl.BlockSpec((1, d, two_de), lambda b, be, nu: (be[blk(b, be, nu)], 0, 0)),
                  pl.BlockSpec((1, 1, two_de), lambda b, be, nu: (be[blk(b, be, nu)], 0, 0)),
                  pl.BlockSpec((1, de, d), lambda b, be, nu: (be[blk(b, be, nu)], 0, 0)),
                  pl.BlockSpec((1, 1, d), lambda b, be, nu: (be[blk(b, be, nu)], 0, 0))],
        out_specs=pl.BlockSpec((bm * SUBLANES, LANES), lambda b, be, nu: (b, 0)),
        scratch_shapes=[pltpu.VMEM((d, two_de), BF16), pltpu.VMEM((de, d), BF16)],
    )
    return pl.pallas_call(
        _experts_kernel,
        out_shape=jax.ShapeDtypeStruct((n_rows * SUBLANES, LANES), F32),
        grid_spec=grid_spec,
        compiler_params=_cparams(("arbitrary",)),
        name="experts",
    )(block_e, n_used, xs, w_gate_up, b_gate_up.reshape(ne, 1, two_de), w_down, b_down.reshape(ne, 1, d))


def _combine_kernel(dest_ref, dest_next_ref, y_ref, gate_ref, x1_ref, gt_ref, gpost_ref, o_ref, buf, sem):
    i = pl.program_id(0)
    tm = x1_ref.shape[0]
    cur = i % 2

    def gather(d_ref, slot):
        def start(r, carry):
            for k in range(TOP_K):
                pltpu.make_async_copy(y_ref.at[_tile_rows(d_ref[k, r], 1)],
                                      buf.at[slot, k, _tile_rows(r, 1)], sem.at[slot]).start(priority=k % 2)
            return carry

        lax.fori_loop(0, tm, start, 0, unroll=4)

    @pl.when(i == 0)
    def _():
        gather(dest_ref, 0)

    @pl.when(i + 1 < pl.num_programs(0))
    def _():
        gather(dest_next_ref, 1 - cur)

    for k in range(TOP_K):
        pltpu.make_async_copy(y_ref.at[_tile_rows(0, tm)], buf.at[cur, k], sem.at[cur]).wait()

    gate = gate_ref[...]
    acc = gate[:, 0:1] * _load_token_tiles(buf.at[cur, 0], tm)
    for k in range(1, TOP_K):
        acc = acc + gate[:, k:k + 1] * _load_token_tiles(buf.at[cur, k], tm)
    o_ref[...] = x1_ref[...] + gt_ref[...] * (_rms(acc) * gpost_ref[...])


def _combine(dest, y, gate4, x1, gt, gpost):
    s, d = x1.shape
    tm = COMBINE_TILE // 2
    n = s // tm
    return pl.pallas_call(
        _combine_kernel,
        out_shape=jax.ShapeDtypeStruct((s, d), F32),
        grid=(n,),
        in_specs=[pl.BlockSpec((TOP_K, tm), lambda i: (0, i), memory_space=pltpu.SMEM),
                  pl.BlockSpec((TOP_K, tm), lambda i: (0, jnp.minimum(i + 1, n - 1)), memory_space=pltpu.SMEM),
                  pl.BlockSpec(memory_space=pl.ANY),
                  pl.BlockSpec((tm, TOP_K), lambda i: (i, 0)),
                  pl.BlockSpec((tm, d), lambda i: (i, 0)),
                  pl.BlockSpec((1, d), lambda i: (0, 0)),
                  pl.BlockSpec((1, d), lambda i: (0, 0))],
        out_specs=pl.BlockSpec((tm, d), lambda i: (i, 0)),
        scratch_shapes=[pltpu.VMEM((2, TOP_K, tm * SUBLANES, LANES), F32), pltpu.SemaphoreType.DMA((2,))],
        compiler_params=_cparams(("arbitrary",)),
        name="combine",
    )(dest, dest, y, gate4, x1, gt, gpost)


def _layer(x2, mod, g_pre_mix, g_post_mix, w_in, b_forget, rel_bias, w_out,
           g_pre_ffn, g_post_ffn, w_router, b_router, w_gate_up, b_gate_up, w_down, b_down):
    s, d = x2.shape
    sh_m, sc_m, gt_m, sh_f, sc_f, gt_f = [mod[:, k * d:(k + 1) * d] for k in range(6)]
    n_qkv = 3 * (N_HEADS_FOX + N_HEADS_MOBA) * HEAD_DIM
    fox_w = N_HEADS_FOX * HEAD_DIM

    w_qkv = w_in[:, :n_qkv].astype(BF16)
    w_f = w_in[:, n_qkv:].T
    qkv, cum, sel, nrm, tr, nr = _inproj(x2, g_pre_mix.reshape(1, d), sc_m, sh_m, w_qkv, w_f,
                                         b_forget.reshape(-1, 1))
    y_a = _fox(qkv, tr, nr, cum, nrm)
    y_b = _moba(qkv, tr, sel, rel_bias)

    w_out_bf = w_out.astype(BF16)
    x1, h2, route, gate4, cnt = _outproj(
        y_a, y_b, x2, w_out_bf[:fox_w], w_out_bf[fox_w:], g_post_mix.reshape(1, d), gt_m,
        g_pre_ffn.reshape(1, d), sc_f, sh_f, w_router, b_router.reshape(1, -1))

    bm = EXPERT_BLOCK
    counts = cnt[0].astype(I32)
    pblk = (counts + bm - 1) // bm
    pend_blk = jnp.cumsum(pblk)
    pstart = ((pend_blk - pblk) * bm).astype(I32)
    n_rows = s * TOP_K + N_EXPERTS * bm
    n_blk = n_rows // bm
    block_e = jnp.minimum(jnp.sum(pend_blk[None, :] <= jnp.arange(n_blk)[:, None], axis=1),
                          N_EXPERTS - 1).astype(I32)
    n_used = pend_blk[-1:].astype(I32)

    dest = _dest(route, pstart)
    xs = _dispatch(pstart, pblk.astype(I32), n_used, dest, h2, n_rows)
    y = _experts(block_e, n_used, xs, w_gate_up, b_gate_up, w_down, b_down)
    return _combine(dest, y, gate4, x1, gt_f, g_post_ffn.reshape(1, d))


def kernel(x, c, w_ada, b_ada, g_pre_mix, g_post_mix, w_in, b_forget, rel_bias, w_out, g_pre_ffn, g_post_ffn, w_router, b_router, w_gate_up, b_gate_up, w_down, b_down):
    bsz, s, d = x.shape
    depth = w_ada.shape[0]
    outs = []
    for bi in range(bsz):
        x2 = x[bi]
        for l in range(depth):
            mod = _adaln(c[bi:bi + 1], w_ada[l], b_ada[l])
            x2 = _layer(x2, mod, g_pre_mix[l], g_post_mix[l], w_in[l], b_forget[l], rel_bias, w_out[l],
                        g_pre_ffn[l], g_post_ffn[l], w_router[l], b_router[l], w_gate_up[l], b_gate_up[l],
                        w_down[l], b_down[l])
        outs.append(x2)
    return outs[0].reshape(1, s, d) if bsz == 1 else jnp.stack(outs)
```

```python
import math

import numpy as np
import jax
import jax.numpy as jnp
from jax import lax
from jax.experimental import pallas as pl
from jax.experimental.pallas import tpu as pltpu

F32 = jnp.float32
BF16 = jnp.bfloat16
I32 = jnp.int32

HEAD_DIM = 64
N_HEADS_FOX = 8
N_HEADS_MOBA = 8
PAIR = 2 * HEAD_DIM
MOBA_BLOCK = 256
MOBA_TOPK = 3
NUM_BUCKETS = 32
MAX_DISTANCE = 128
N_EXPERTS = 32
TOP_K = 4
SWIGLU_LIMIT = 7.0
SWIGLU_ALPHA = 1.702
RMS_EPS = 1e-6
NEG = -(2.0 ** 100)
M_INIT = -(2.0 ** 99)
LOG2E = math.log2(math.e)
SUM_ROWS = 16
SUBLANES = 8
LANES = 128
EXP_UNDERFLOW = 90.0
VMEM_LIMIT = 56 * 1024 * 1024

ROW_TILE = 512
FOX_TILE = 256
EXPERT_BLOCK = 512
COMBINE_TILE = 1024
MOBA_LAG = 1
MOBA_UNROLL = 4

NT_DIMS = (((1,), (1,)), ((), ()))


def _cparams(sem):
    return pltpu.CompilerParams(dimension_semantics=sem, vmem_limit_bytes=VMEM_LIMIT)


def _rms(x):
    return x * lax.rsqrt(jnp.mean(x * x, axis=-1, keepdims=True) + RMS_EPS)


def _adaln_kernel(c_ref, w_ref, b_ref, o_ref):
    c = c_ref[...]
    cond = c * jax.nn.sigmoid(c)
    o_ref[...] = jnp.dot(cond, w_ref[...], preferred_element_type=F32,
                         precision=lax.Precision.HIGHEST) + b_ref[...]


def _adaln(c, w_ada, b_ada):
    d = c.shape[-1]
    n = w_ada.shape[-1]
    c8 = jnp.broadcast_to(c.reshape(1, d), (8, d))
    out = pl.pallas_call(
        _adaln_kernel,
        out_shape=jax.ShapeDtypeStruct((8, n), F32),
        grid=(n // d,),
        in_specs=[pl.BlockSpec((8, d), lambda j: (0, 0)),
                  pl.BlockSpec((d, d), lambda j: (0, j)),
                  pl.BlockSpec((1, d), lambda j: (0, j))],
        out_specs=pl.BlockSpec((8, d), lambda j: (0, j)),
        compiler_params=_cparams(("arbitrary",)),
        name="adaln",
    )(c8, w_ada, b_ada.reshape(1, n))
    return out[0:1]


def _inproj_kernel(x_ref, g_ref, sc_ref, sh_ref, w_ref, wf_ref, bf_ref,
                   qkv_ref, cum_ref, sel_ref, nrm_ref, tr_ref, nr_ref, km_scr, carry_scr):
    i = pl.program_id(0)
    tm = x_ref.shape[0]
    nblk = km_scr.shape[0]

    @pl.when(i == 0)
    def _():
        km_scr[...] = jnp.zeros_like(km_scr)
        carry_scr[...] = jnp.zeros_like(carry_scr)

    x = x_ref[...]
    h = _rms(x) * g_ref[...] * (1.0 + sc_ref[...]) + sh_ref[...]
    hb = h.astype(BF16)

    width = N_HEADS_FOX * HEAD_DIM
    hsel = jnp.where(lax.broadcasted_iota(I32, (width, N_HEADS_FOX), 0) // HEAD_DIM
                     == lax.broadcasted_iota(I32, (width, N_HEADS_FOX), 1), 1.0, 0.0)
    kb = None
    tr_slot = {3: 0, 5: 1, 0: 2, 2: 3}
    qbt = None
    for c in range(6):
        pc = jnp.dot(hb, w_ref[:, c * width:(c + 1) * width], preferred_element_type=F32)
        if c in tr_slot:
            n = tr_slot[c]
            pct = pc.T
            if c == 3:
                qbt = pct
            if c == 0 or c == 3:
                pct = pct * (LOG2E * HEAD_DIM ** -0.5)
            tr_ref[n * width:(n + 1) * width, :] = pct.astype(BF16)
        if c == 1 or c == 4:
            qkv_ref[:, (c // 3) * width:(c // 3 + 1) * width] = pc.astype(BF16)
        if c < 2:
            sq = (pc * (HEAD_DIM ** -0.5) if c == 0 else pc).astype(BF16).astype(F32)
            n2 = jnp.dot((sq * sq).astype(BF16), hsel.astype(BF16), preferred_element_type=F32)
            nrm_ref[0, c:c + 1, :] = jnp.max(n2, axis=0, keepdims=True)
        if c == 4:
            kb = pc

    ft = lax.dot_general(wf_ref[...].astype(BF16), hb, NT_DIMS, preferred_element_type=F32)
    z = ft + bf_ref[...]
    logf = -(jnp.maximum(-z, 0.0) + jnp.log1p(jnp.exp(-jnp.abs(z))))
    lane = lax.broadcasted_iota(I32, logf.shape, 1)
    cs = logf
    sh = 1
    while sh < tm:
        cs = cs + jnp.where(lane >= sh, pltpu.roll(cs, sh, axis=1), 0.0)
        sh *= 2
    base = jnp.zeros_like(cs)
    for b in range(1, tm // FOX_TILE):
        base = jnp.where(lane >= b * FOX_TILE, cs[:, b * FOX_TILE - 1:b * FOX_TILE], base)
    nr_t = (-LOG2E * (cs - base)).T
    for pp in range(nr_ref.shape[0]):
        nr_ref[pp] = nr_t[:, 2 * pp:2 * pp + 2]
    cs = cs + carry_scr[...]
    cum_ref[...] = cs
    carry_scr[...] = cs[:, tm - 1:tm]

    nb_tile = tm // MOBA_BLOCK
    for b in range(nb_tile):
        kmean = jnp.sum(kb[b * MOBA_BLOCK:(b + 1) * MOBA_BLOCK], axis=0, keepdims=True) * (1.0 / MOBA_BLOCK)
        km_scr[pl.ds(i * nb_tile + b, 1), :] = kmean

    km = km_scr[...]
    blk = lax.broadcasted_iota(I32, (nblk, tm), 0)
    col = lax.broadcasted_iota(I32, (nblk, tm), 1)
    own = i * nb_tile + col // MOBA_BLOCK
    for hd in range(N_HEADS_MOBA):
        hs = slice(hd * HEAD_DIM, (hd + 1) * HEAD_DIM)
        g = jnp.dot(km[:, hs], qbt[hs, :], preferred_element_type=F32,
                    precision=lax.Precision.HIGHEST)
        g = jnp.where(blk < own, g, -jnp.inf)
        sel = jnp.zeros(g.shape, dtype=jnp.bool_)
        for _ in range(MOBA_TOPK):
            m = jnp.max(g, axis=0, keepdims=True)
            first = jnp.min(jnp.where(g == m, blk, nblk), axis=0, keepdims=True)
            pick = (blk == first) & (m > -jnp.inf)
            sel = sel | pick
            g = jnp.where(pick, -jnp.inf, g)
        sel_ref[hd] = jnp.where(sel, 0.0, NEG)


def _inproj(x2, g, sc, sh, w_qkv, w_f, b_f):
    s, d = x2.shape
    tm = ROW_TILE
    nblk = s // MOBA_BLOCK
    n = w_qkv.shape[1]
    width = N_HEADS_FOX * HEAD_DIM
    n_rows_out = 2 * width
    n_tr = 4 * width
    assert tm % FOX_TILE == 0 and N_HEADS_FOX == N_HEADS_MOBA
    return pl.pallas_call(
        _inproj_kernel,
        out_shape=(jax.ShapeDtypeStruct((s, n_rows_out), BF16),
                   jax.ShapeDtypeStruct((N_HEADS_FOX, s), F32),
                   jax.ShapeDtypeStruct((N_HEADS_MOBA, nblk, s), F32),
                   jax.ShapeDtypeStruct((s // tm, 2, N_HEADS_FOX), F32),
                   jax.ShapeDtypeStruct((n_tr, s), BF16),
                   jax.ShapeDtypeStruct((N_HEADS_FOX // 2, s, 2), F32)),
        grid=(s // tm,),
        in_specs=[pl.BlockSpec((tm, d), lambda i: (i, 0)),
                  pl.BlockSpec((1, d), lambda i: (0, 0)),
                  pl.BlockSpec((1, d), lambda i: (0, 0)),
                  pl.BlockSpec((1, d), lambda i: (0, 0)),
                  pl.BlockSpec((d, n), lambda i: (0, 0)),
                  pl.BlockSpec((N_HEADS_FOX, d), lambda i: (0, 0)),
                  pl.BlockSpec((N_HEADS_FOX, 1), lambda i: (0, 0))],
        out_specs=(pl.BlockSpec((tm, n_rows_out), lambda i: (i, 0)),
                   pl.BlockSpec((N_HEADS_FOX, tm), lambda i: (0, i)),
                   pl.BlockSpec((N_HEADS_MOBA, nblk, tm), lambda i: (0, 0, i)),
                   pl.BlockSpec((1, 2, N_HEADS_FOX), lambda i: (i, 0, 0)),
                   pl.BlockSpec((n_tr, tm), lambda i: (0, i)),
                   pl.BlockSpec((N_HEADS_FOX // 2, tm, 2), lambda i: (0, i, 0))),
        scratch_shapes=[pltpu.VMEM((nblk, N_HEADS_MOBA * HEAD_DIM), F32),
                        pltpu.VMEM((N_HEADS_FOX, 1), F32)],
        compiler_params=_cparams(("arbitrary",)),
        name="inproj",
    )(x2, g, sc, sh, w_qkv, w_f, b_f)


def _fox_kernel(jlo_ref, cp_ref, qt_ref, k_ref, vt_ref, nr_ref, o_ref, m_ref, a_ref):
    p = pl.program_id(0)
    g = pl.program_id(1)
    j_first = jlo_ref[p * pl.num_programs(1) + g]
    tq = qt_ref.shape[1]
    t = tq // 2

    m_ref[...] = jnp.full(m_ref.shape, M_INIT, F32)
    a_ref[...] = jnp.zeros(a_ref.shape, F32)
    ones_rows = jnp.where(lax.broadcasted_iota(I32, (SUM_ROWS, t), 0) == 0, 1.0, 0.0).astype(BF16)
    in_a = lax.broadcasted_iota(I32, (1, tq), 1) < t
    krow = lax.broadcasted_iota(I32, (t, tq), 0)
    qcol = lax.broadcasted_iota(I32, (t, tq), 1)
    future_a = (qcol < t) & (krow > qcol)
    future_b = (qcol >= t) & (krow > qcol - t)

    def scores(j, future):
        k0 = pl.multiple_of(j * t, t)
        out = []
        for hd in range(2):
            kh = k_ref[pl.ds(k0, t), hd * HEAD_DIM:(hd + 1) * HEAD_DIM]
            qh = qt_ref[hd * HEAD_DIM:(hd + 1) * HEAD_DIM, :]
            s = jnp.dot(kh, qh, preferred_element_type=F32) + nr_ref[0, pl.ds(k0, t), hd:hd + 1]
            out.append(s if future is None else jnp.where(future, NEG, s))
        return out

    def shifts(j, a_on, b_on):
        rows = []
        for hd in range(2):
            h = 2 * p + hd
            sa = LOG2E * (cp_ref[h, 2 * g] - cp_ref[h, j])
            sb = LOG2E * (cp_ref[h, 2 * g + 1] - cp_ref[h, j])
            rows.append(jnp.where(in_a, jnp.where(a_on, sa, NEG), jnp.where(b_on, sb, NEG)))
        return rows

    def softmax(ss, shift_rows):
        ps, alphas = [], []
        for hd in range(2):
            m_old = m_ref[hd]
            m_new = jnp.maximum(m_old, jnp.max(ss[hd], axis=0, keepdims=True) + shift_rows[hd])
            alphas.append(jnp.exp2(m_old - m_new))
            m_ref[hd] = m_new
            ps.append(jnp.exp2(ss[hd] - (m_new - shift_rows[hd])).astype(BF16))
        return ps, alphas

    def accumulate(j, ps, alphas):
        k0 = pl.multiple_of(j * t, t)
        for hd in range(2):
            vh = vt_ref[hd * HEAD_DIM:(hd + 1) * HEAD_DIM, pl.ds(k0, t)]
            vcat = jnp.concatenate([vh, ones_rows], axis=0)
            a_ref[hd] = a_ref[hd] * alphas[hd] + jnp.dot(vcat, ps[hd], preferred_element_type=F32)

    def body(j, carry):
        accumulate(j, *softmax(scores(j, None), shifts(j, True, True)))
        return carry

    lax.fori_loop(j_first, 2 * g - 1, body, 0)
    j1 = jnp.maximum(2 * g - 1, 0)
    j2 = 2 * g
    j3 = 2 * g + 1
    ss1 = scores(j1, None)
    ss2 = scores(j2, future_a)
    ss3 = scores(j3, future_b)
    w1 = softmax(ss1, shifts(j1, g >= 1, g >= 1))
    w2 = softmax(ss2, shifts(j2, True, True))
    accumulate(j1, *w1)
    w3 = softmax(ss3, shifts(j3, False, True))
    accumulate(j2, *w2)
    accumulate(j3, *w3)
    out_t = jnp.concatenate([a_ref[hd, 0:HEAD_DIM, :] / a_ref[hd, HEAD_DIM:HEAD_DIM + 1, :]
                             for hd in range(2)], axis=0)
    o_ref[...] = out_t.T.astype(o_ref.dtype)


def _fox_first_tile(cum, nrm, t):
    cend = cum[:, t - 1::t]
    nt = cend.shape[1]
    cprev = jnp.concatenate([jnp.zeros((cend.shape[0], 1), F32), cend[:, :-1]], axis=1)
    rep = nt // nrm.shape[0]
    qn = jnp.repeat(jnp.sqrt(nrm[:, 0, :]).T, rep, axis=1)
    kn = jnp.repeat(jnp.sqrt(nrm[:, 1, :]).T, rep, axis=1)
    gap = (1.02 * qn[:, :, None] * (kn[:, None, :] + kn[:, :, None])
           + cprev[:, :, None] - cend[:, None, :])
    jj = jnp.arange(nt)[None, None, :]
    ii = jnp.arange(nt)[None, :, None]
    needed = (jj < ii) & jnp.logical_not(gap <= -EXP_UNDERFLOW)
    needed = needed[0::2] | needed[1::2]
    first = jnp.min(jnp.where(needed, jj, ii), axis=2)
    return first.reshape(-1).astype(I32)


def _fox(qkv, tr, nr, cum, nrm):
    s = qkv.shape[0]
    t = FOX_TILE
    npair = N_HEADS_FOX // 2
    tq = 2 * t
    jlo = _fox_first_tile(cum, nrm, t).reshape(npair, s // tq, 2).min(axis=2).reshape(-1)
    cend = cum[:, t - 1::t]
    cprev = jnp.concatenate([jnp.zeros((cend.shape[0], 1), F32), cend[:, :-1]], axis=1)
    q_rows = 2 * (N_HEADS_MOBA // 2)
    grid_spec = pltpu.PrefetchScalarGridSpec(
        num_scalar_prefetch=1,
        grid=(npair, s // tq),
        in_specs=[pl.BlockSpec(memory_space=pltpu.SMEM),
                  pl.BlockSpec((PAIR, tq), lambda p, i, jl: (q_rows + p, i)),
                  pl.BlockSpec((s, PAIR), lambda p, i, jl: (0, p)),
                  pl.BlockSpec((PAIR, s), lambda p, i, jl: (q_rows + npair + p, 0)),
                  pl.BlockSpec((1, s, 2), lambda p, i, jl: (p, 0, 0))],
        out_specs=pl.BlockSpec((tq, PAIR), lambda p, i, jl: (i, p)),
        scratch_shapes=[pltpu.VMEM((2, 1, tq), F32), pltpu.VMEM((2, HEAD_DIM + SUM_ROWS, tq), F32)],
    )
    return pl.pallas_call(
        _fox_kernel,
        out_shape=jax.ShapeDtypeStruct((s, npair * PAIR), BF16),
        grid_spec=grid_spec,
        compiler_params=_cparams(("arbitrary", "arbitrary")),
        name="fox",
    )(jlo, cprev, tr, qkv, tr, nr)


def _t5_bucket_np(dist):
    dist = np.maximum(dist, 0)
    max_exact = NUM_BUCKETS // 2
    d = np.maximum(dist, 1).astype(np.float32)
    large = max_exact + (np.log(d / np.float32(max_exact)) / np.float32(math.log(MAX_DISTANCE / max_exact))
                         * np.float32(NUM_BUCKETS - max_exact)).astype(np.int32)
    large = np.minimum(large, NUM_BUCKETS - 1)
    return np.where(dist < max_exact, dist, large).astype(np.int32)


def _moba_kernel(rb_ref, qt_ref, k_ref, vt_ref, sel_ref, bkt_ref, o_ref,
                 m_ref, a_ref, bias_scr, *bufs):
    s_bufs = bufs[0:MOBA_UNROLL]
    p_bufs = bufs[MOBA_UNROLL:2 * MOBA_UNROLL]
    al_bufs = bufs[2 * MOBA_UNROLL:3 * MOBA_UNROLL]
    p = pl.program_id(0)
    g = pl.program_id(1)
    tq = qt_ref.shape[1]
    t = tq // 2

    @pl.when(g == 0)
    def _():
        r = lax.broadcasted_iota(I32, (t, t), 0)
        c = lax.broadcasted_iota(I32, (t, t), 1)
        zero = jnp.zeros((t, t), F32)
        for hd in range(2):
            h = 2 * p + hd
            far = rb_ref[(NUM_BUCKETS - 1) * N_HEADS_MOBA + h]
            tiles = []
            for w in range(2):
                bkt = bkt_ref[w]
                acc = jnp.zeros(bkt.shape, F32)
                for kk in range(NUM_BUCKETS):
                    acc = acc + jnp.where(bkt == kk, rb_ref[kk * N_HEADS_MOBA + h], 0.0)
                tiles.append((acc - far) * LOG2E)
            prev_t = tiles[0]
            own_t = jnp.where(r <= c, tiles[1], NEG)
            bias_scr[hd, 0] = jnp.concatenate([prev_t, zero], axis=1)
            bias_scr[hd, 1] = jnp.concatenate([own_t, prev_t], axis=1)
            bias_scr[hd, 2] = jnp.concatenate([zero, own_t], axis=1)

    qt = qt_ref[...]
    top = lax.broadcasted_iota(I32, qt.shape, 0) < HEAD_DIM
    zq = jnp.zeros_like(qt)
    qth = (jnp.where(top, qt, zq), jnp.where(top, zq, qt))
    m_ref[...] = jnp.full(m_ref.shape, M_INIT, F32)
    a_ref[...] = jnp.zeros(a_ref.shape, F32)

    ones_rows = jnp.where(lax.broadcasted_iota(I32, (SUM_ROWS, t), 0) == 0, 1.0, 0.0).astype(BF16)
    in_a = lax.broadcasted_iota(I32, (1, tq), 1) < t

    n_far = jnp.maximum(2 * g - 1, 0)

    n_key_tiles = k_ref.shape[0] // t

    def produce(j, s_buf):
        j = jnp.minimum(j, n_key_tiles - 1)
        k0 = pl.multiple_of(j * t, t)
        kt = k_ref[pl.ds(k0, t), :]
        for hd in range(2):
            s_buf[hd] = jnp.dot(kt, qth[hd], preferred_element_type=F32)

    def softmax(s_buf, p_buf, al_buf, selrows, w):
        for hd in range(2):
            s = s_buf[hd]
            if w is not None:
                s = s + bias_scr[hd, w]
            smax = jnp.max(s, axis=0, keepdims=True)
            m_old = m_ref[hd]
            m_new = jnp.maximum(m_old, smax + selrows[hd])
            shift = m_new - selrows[hd]
            al_buf[hd] = jnp.exp2(m_old - m_new)
            m_ref[hd] = m_new
            p_buf[hd * t:(hd + 1) * t, :] = jnp.exp2(s - shift).astype(BF16)

    def far_rows(j):
        return [jnp.where(j < n_far, sel_ref[hd, pl.ds(j, 1), :], NEG) for hd in range(2)]

    def accumulate(j, p_buf, al_buf):
        k0 = pl.multiple_of(j * t, t)
        for hd in range(2):
            vh = vt_ref[hd * HEAD_DIM:(hd + 1) * HEAD_DIM, pl.ds(k0, t)]
            vcat = jnp.concatenate([vh, ones_rows], axis=0)
            a_ref[hd] = (a_ref[hd] * al_buf[hd]
                         + jnp.dot(vcat, p_buf[hd * t:(hd + 1) * t, :], preferred_element_type=F32))

    un = MOBA_UNROLL
    lag = MOBA_LAG
    for n in range(un - lag, un):
        p_bufs[n][...] = jnp.zeros_like(p_bufs[n])
        al_bufs[n][...] = jnp.ones_like(al_bufs[n])
    produce(0, s_bufs[0])

    def body(u, carry):
        j0 = un * u
        for n in range(un):
            accumulate(jnp.maximum(j0 + n - lag, 0), p_bufs[(n - lag) % un], al_bufs[(n - lag) % un])
            softmax(s_bufs[n], p_bufs[n], al_bufs[n], far_rows(j0 + n), None)
            produce(j0 + n + 1, s_bufs[(n + 1) % un])
        return carry

    n_trips = (n_far + un - 1) // un
    lax.fori_loop(0, n_trips, body, 0)

    j1 = jnp.maximum(2 * g - 1, 0)
    j2 = 2 * g
    j3 = 2 * g + 1
    rows1 = [jnp.where(g >= 1, sel_ref[hd, pl.ds(j1, 1), :], NEG) for hd in range(2)]
    rows2 = [jnp.where(in_a, 0.0, sel_ref[hd, pl.ds(j2, 1), :]) for hd in range(2)]
    rows3 = [jnp.where(in_a, NEG, 0.0)] * 2
    produce(j1, s_bufs[0])
    produce(j2, s_bufs[1])
    for n in range(un - lag, un):
        accumulate(jnp.maximum(un * n_trips - un + n, 0), p_bufs[n], al_bufs[n])
    softmax(s_bufs[0], p_bufs[0], al_bufs[0], rows1, 0)
    produce(j3, s_bufs[0])
    softmax(s_bufs[1], p_bufs[1], al_bufs[1], rows2, 1)
    accumulate(j1, p_bufs[0], al_bufs[0])
    softmax(s_bufs[0], p_bufs[2], al_bufs[2], rows3, 2)
    accumulate(j2, p_bufs[1], al_bufs[1])
    accumulate(j3, p_bufs[2], al_bufs[2])
    out_t = jnp.concatenate([a_ref[hd, 0:HEAD_DIM, :] / a_ref[hd, HEAD_DIM:HEAD_DIM + 1, :]
                             for hd in range(2)], axis=0)
    o_ref[...] = out_t.T.astype(o_ref.dtype)


def _moba(qkv, tr, sel, rel_bias):
    s = qkv.shape[0]
    t = MOBA_BLOCK
    npair = N_HEADS_MOBA // 2
    nblk = s // t
    kcol = N_HEADS_FOX // 2
    a = np.arange(t)[None, :]
    b = np.arange(t)[:, None]
    bkt = jnp.asarray(np.stack([_t5_bucket_np(t + a - b), _t5_bucket_np(a - b)]))
    tq = 2 * t
    grid_spec = pltpu.PrefetchScalarGridSpec(
        num_scalar_prefetch=1,
        grid=(npair, s // tq),
        in_specs=[pl.BlockSpec((PAIR, tq), lambda p, i, rb: (p, i)),
                  pl.BlockSpec((s, PAIR), lambda p, i, rb: (0, kcol + p)),
                  pl.BlockSpec((PAIR, s), lambda p, i, rb: (npair + p, 0)),
                  pl.BlockSpec((2, nblk, tq), lambda p, i, rb: (p, 0, i)),
                  pl.BlockSpec((2, t, t), lambda p, i, rb: (0, 0, 0))],
        out_specs=pl.BlockSpec((tq, PAIR), lambda p, i, rb: (i, p)),
        scratch_shapes=[pltpu.VMEM((2, 1, tq), F32),
                        pltpu.VMEM((2, HEAD_DIM + SUM_ROWS, tq), F32), pltpu.VMEM((2, 3, t, tq), F32)]
        + [pltpu.VMEM((2, t, tq), F32)] * MOBA_UNROLL
        + [pltpu.VMEM((2 * t, tq), BF16)] * MOBA_UNROLL
        + [pltpu.VMEM((2, 1, tq), F32)] * MOBA_UNROLL,
    )
    return pl.pallas_call(
        _moba_kernel,
        out_shape=jax.ShapeDtypeStruct((s, npair * PAIR), BF16),
        grid_spec=grid_spec,
        compiler_params=_cparams(("arbitrary", "arbitrary")),
        name="moba",
    )(rel_bias.reshape(-1), tr, qkv, tr, sel, bkt)


def _store_token_tiles(ref, val):
    n = val.shape[0]
    for c in range(SUBLANES):
        ref[pl.ds(c, n, stride=SUBLANES), :] = val[:, c * LANES:(c + 1) * LANES]


def _load_token_tiles(ref, n):
    return jnp.concatenate([ref[pl.ds(c, n, stride=SUBLANES), :] for c in range(SUBLANES)], axis=1)


def _tile_rows(r0, n):
    start = r0 * SUBLANES
    if not isinstance(start, int):
        start = pl.multiple_of(start, SUBLANES)
    return pl.ds(start, n * SUBLANES)


def _outproj_kernel(ya_ref, yb_ref, x_ref, wa_ref, wb_ref, gpost_ref, gt_ref, gpre_ref,
                    sc_ref, sh_ref, wr_ref, br_ref,
                    x1_ref, h2_ref, route_ref, gate_ref, cnt_ref, carry_scr):
    i = pl.program_id(0)
    tm = x_ref.shape[0]

    @pl.when(i == 0)
    def _():
        carry_scr[...] = jnp.zeros_like(carry_scr)

    y = (jnp.dot(ya_ref[...], wa_ref[...], preferred_element_type=F32)
         + jnp.dot(yb_ref[...], wb_ref[...], preferred_element_type=F32))
    x1 = x_ref[...] + gt_ref[...] * (_rms(y) * gpost_ref[...])
    x1_ref[...] = x1
    h2 = _rms(x1) * gpre_ref[...] * (1.0 + sc_ref[...]) + sh_ref[...]
    _store_token_tiles(h2_ref, h2)

    h_hi = h2.astype(BF16)
    h_lo = (h2 - h_hi.astype(F32)).astype(BF16)
    logits = (jnp.dot(h_hi, wr_ref[0], preferred_element_type=F32)
              + jnp.dot(h_hi, wr_ref[1], preferred_element_type=F32)
              + jnp.dot(h_lo, wr_ref[0], preferred_element_type=F32)) + br_ref[...]
    ne = logits.shape[1]
    lane = lax.broadcasted_iota(I32, logits.shape, 1)
    lane4 = lax.broadcasted_iota(I32, (tm, TOP_K), 1)
    g = logits
    mask = jnp.zeros(logits.shape, F32)
    vals, picks = [], []
    for _ in range(TOP_K):
        m = jnp.max(g, axis=1, keepdims=True)
        first = jnp.min(jnp.where(g == m, lane, ne), axis=1, keepdims=True)
        pick = lane == first
        mask = jnp.where(pick, 1.0, mask)
        g = jnp.where(pick, -jnp.inf, g)
        vals.append(m)
        picks.append(pick)
    ex = [jnp.exp(v - vals[0]) for v in vals]
    den = ex[0] + ex[1] + ex[2] + ex[3]
    gates = [e / den for e in ex]

    r = lax.broadcasted_iota(I32, (tm, tm), 0)
    c = lax.broadcasted_iota(I32, (tm, tm), 1)
    tril = jnp.where(c < r, 1.0, 0.0).astype(BF16)
    before = jnp.dot(tril, mask.astype(BF16), preferred_element_type=F32) + carry_scr[...]
    total = carry_scr[...] + jnp.sum(mask, axis=0, keepdims=True)
    carry_scr[...] = total
    cnt_ref[...] = jnp.broadcast_to(total, cnt_ref.shape)

    def pack4(cols):
        return jnp.where(lane4 == 0, cols[0],
                         jnp.where(lane4 == 1, cols[1], jnp.where(lane4 == 2, cols[2], cols[3])))

    wide = lax.broadcasted_iota(I32, (tm, LANES), 1)
    route = jnp.zeros((tm, LANES), F32)
    for k, pk in enumerate(picks):
        idx_k = jnp.sum(jnp.where(pk, lane, 0), axis=1, keepdims=True).astype(F32)
        rank_k = jnp.sum(jnp.where(pk, before, 0.0), axis=1, keepdims=True)
        route = jnp.where(wide == k, idx_k, jnp.where(wide == TOP_K + k, rank_k, route))
    route_ref[...] = route.T[0:2 * TOP_K, :].astype(I32)
    gate_ref[...] = pack4(gates)


def _outproj(mix_a, mix_b, x2, w_a, w_b, gpost, gt, gpre, sc, sh, w_router, b_router):
    s, d = x2.shape
    tm = ROW_TILE
    ne = w_router.shape[1]
    wa = mix_a.shape[1]
    w_hi = w_router.astype(BF16)
    w_router_hl = jnp.stack([w_hi, (w_router - w_hi.astype(F32)).astype(BF16)])
    row = lambda i: (i, 0)
    fix = lambda i: (0, 0)
    vec = pl.BlockSpec((1, d), fix)
    return pl.pallas_call(
        _outproj_kernel,
        out_shape=(jax.ShapeDtypeStruct((s, d), F32),
                   jax.ShapeDtypeStruct((s * SUBLANES, LANES), F32),
                   jax.ShapeDtypeStruct((2 * TOP_K, s), I32),
                   jax.ShapeDtypeStruct((s, TOP_K), F32),
                   jax.ShapeDtypeStruct((8, ne), F32)),
        grid=(s // tm,),
        in_specs=[pl.BlockSpec((tm, wa), row), pl.BlockSpec((tm, wa), row), pl.BlockSpec((tm, d), row),
                  pl.BlockSpec((wa, d), fix), pl.BlockSpec((wa, d), fix),
                  vec, vec, vec, vec, vec,
                  pl.BlockSpec((2, d, ne), lambda i: (0, 0, 0)), pl.BlockSpec((1, ne), fix)],
        out_specs=(pl.BlockSpec((tm, d), row), pl.BlockSpec((tm * SUBLANES, LANES), row),
                   pl.BlockSpec((2 * TOP_K, tm), lambda i: (0, i)),
                   pl.BlockSpec((tm, TOP_K), row), pl.BlockSpec((8, ne), fix)),
        scratch_shapes=[pltpu.VMEM((1, ne), F32)],
        compiler_params=_cparams(("arbitrary",)),
        name="outproj",
    )(mix_a, mix_b, x2, w_a, w_b, gpost, gt, gpre, sc, sh, w_router_hl, b_router)


def _dest_kernel(route_ref, pstart_ref, o_ref):
    route = route_ref[...]
    tm = route.shape[1]
    ne = pstart_ref.shape[0]
    expert = lax.broadcasted_iota(I32, (ne, tm), 0)
    for k in range(TOP_K):
        start = jnp.sum(jnp.where(expert == route[k:k + 1, :], pstart_ref[...], 0), axis=0, keepdims=True)
        o_ref[k:k + 1, :] = start + route[TOP_K + k:TOP_K + k + 1, :]


def _dest(route, pstart):
    s = route.shape[1]
    tm = min(4 * ROW_TILE, s)
    return pl.pallas_call(
        _dest_kernel,
        out_shape=jax.ShapeDtypeStruct((TOP_K, s), I32),
        grid=(s // tm,),
        in_specs=[pl.BlockSpec((2 * TOP_K, tm), lambda i: (0, i)),
                  pl.BlockSpec((N_EXPERTS, 1), lambda i: (0, 0))],
        out_specs=pl.BlockSpec((TOP_K, tm), lambda i: (0, i)),
        compiler_params=_cparams(("arbitrary",)),
        name="dest",
    )(route, pstart.reshape(-1, 1))


def _dispatch_kernel(pstart_ref, pblk_ref, nu_ref, dest_ref, h_ref, xs_ref,
                     zero_scr, sem, zsem):
    tm = h_ref.shape[0] // SUBLANES
    bm = zero_scr.shape[0] // SUBLANES
    n_blk = xs_ref.shape[0] // (bm * SUBLANES)

    rows = _tile_rows

    @pl.when(pl.program_id(0) == 0)
    def _():
        zero_scr[...] = jnp.zeros_like(zero_scr)

        def zero_copy(row0):
            return pltpu.make_async_copy(zero_scr, xs_ref.at[rows(row0, bm)], zsem)

        for phase in range(2):
            for e in range(N_EXPERTS):
                last = pstart_ref[e] + (pblk_ref[e] - 1) * bm
                tail = (n_blk - N_EXPERTS + e) * bm
                for cond, row0 in ((pblk_ref[e] > 0, last), (n_blk - N_EXPERTS + e >= nu_ref[0], tail)):
                    @pl.when(cond)
                    def _():
                        if phase == 0:
                            zero_copy(row0).start()
                        else:
                            zero_copy(row0).wait()

    def row_copy(r, k):
        dst = dest_ref[k, r]
        return pltpu.make_async_copy(h_ref.at[rows(r, 1)], xs_ref.at[rows(dst, 1)], sem)

    def start(r, carry):
        for k in range(TOP_K):
            row_copy(r, k).start(priority=k % 2)
        return carry

    lax.fori_loop(0, tm, start, 0, unroll=4)
    for k in range(TOP_K):
        pltpu.make_async_copy(h_ref, xs_ref.at[rows(0, tm)], sem).wait()


def _dispatch(pstart, pblk, n_used, dest, h2, n_rows):
    s = h2.shape[0] // SUBLANES
    tm = COMBINE_TILE
    grid_spec = pltpu.PrefetchScalarGridSpec(
        num_scalar_prefetch=3,
        grid=(s // tm,),
        in_specs=[pl.BlockSpec((TOP_K, tm), lambda i, *_: (0, i), memory_space=pltpu.SMEM),
                  pl.BlockSpec((tm * SUBLANES, LANES), lambda i, *_: (i, 0))],
        out_specs=pl.BlockSpec(memory_space=pl.ANY),
        scratch_shapes=[pltpu.VMEM((EXPERT_BLOCK * SUBLANES, LANES), F32), pltpu.SemaphoreType.DMA,
                        pltpu.SemaphoreType.DMA],
    )
    return pl.pallas_call(
        _dispatch_kernel,
        out_shape=jax.ShapeDtypeStruct((n_rows * SUBLANES, LANES), F32),
        grid_spec=grid_spec,
        compiler_params=_cparams(("arbitrary",)),
        name="dispatch",
    )(pstart, pblk, n_used, dest, h2)


def _experts_kernel(be_ref, nu_ref, xs_ref, wgu_ref, bgu_ref, wd_ref, bd_ref, y_ref,
                    wgu_bf, wd_bf):
    b = pl.program_id(0)
    d_exp = wd_ref.shape[1]
    prev = be_ref[jnp.maximum(b - 1, 0)]
    changed = (b == 0) | (be_ref[b] != prev)

    @pl.when((b < nu_ref[0]) & changed)
    def _():
        rows = LANES

        def cast_gu(c, carry):
            r0 = pl.multiple_of(c * rows, rows)
            wgu_bf[pl.ds(r0, rows), :] = wgu_ref[0, pl.ds(r0, rows), :].astype(BF16)
            return carry

        def cast_d(c, carry):
            r0 = pl.multiple_of(c * rows, rows)
            wd_bf[pl.ds(r0, rows), :] = wd_ref[0, pl.ds(r0, rows), :].astype(BF16)
            return carry

        lax.fori_loop(0, wgu_ref.shape[1] // rows, cast_gu, 0)
        lax.fori_loop(0, wd_ref.shape[1] // rows, cast_d, 0)

    @pl.when(b < nu_ref[0])
    def _():
        bm = xs_ref.shape[0] // SUBLANES
        xb = _load_token_tiles(xs_ref, bm).astype(BF16)
        hdn = jnp.dot(xb, wgu_bf[...], preferred_element_type=F32) + bgu_ref[0]
        x_glu = jnp.minimum(hdn[:, :d_exp], SWIGLU_LIMIT)
        x_lin = jnp.clip(hdn[:, d_exp:], -SWIGLU_LIMIT, SWIGLU_LIMIT)
        act = x_glu * jax.nn.sigmoid(SWIGLU_ALPHA * x_glu) * (x_lin + 1.0)
        _store_token_tiles(y_ref, jnp.dot(act.astype(BF16), wd_bf[...], preferred_element_type=F32)
                           + bd_ref[0])

    @pl.when(b >= nu_ref[0])
    def _():
        y_ref[...] = jnp.zeros_like(y_ref)


def _experts(block_e, n_used, xs, w_gate_up, b_gate_up, w_down, b_down):
    n_rows = xs.shape[0] // SUBLANES
    bm = EXPERT_BLOCK
    n_blk = n_rows // bm
    ne, d, two_de = w_gate_up.shape
    de = w_down.shape[1]
    assert d == SUBLANES * LANES

    def blk(b, be, nu):
        return jnp.minimum(b, nu[0] - 1)

    grid_spec = pltpu.PrefetchScalarGridSpec(
        num_scalar_prefetch=2,
        grid=(n_blk,),
        in_specs=[pl.BlockSpec((bm * SUBLANES, LANES), lambda b, be, nu: (blk(b, be, nu), 0)),
                  pl.BlockSpec((1, d, two_de), lambda b, be, nu: (be[blk(b, be, nu)], 0, 0)),
                  pl.BlockSpec((1, 1, two_de), lambda b, be, nu: (be[blk(b, be, nu)], 0, 0)),
                  pl.BlockSpec((1, de, d), lambda b, be, nu: (be[blk(b, be, nu)], 0, 0)),
                  pl.BlockSpec((1, 1, d), lambda b, be, nu: (be[blk(b, be, nu)], 0, 0))],
        out_specs=pl.BlockSpec((bm * SUBLANES, LANES), lambda b, be, nu: (b, 0)),
        scratch_shapes=[pltpu.VMEM((d, two_de), BF16), pltpu.VMEM((de, d), BF16)],
    )
    return pl.pallas_call(
        _experts_kernel,
        out_shape=jax.ShapeDtypeStruct((n_rows * SUBLANES, LANES), F32),
        grid_spec=grid_spec,
        compiler_params=_cparams(("arbitrary",)),
        name="experts",
    )(block_e, n_used, xs, w_gate_up, b_gate_up.reshape(ne, 1, two_de), w_down, b_down.reshape(ne, 1, d))


def _combine_kernel(dest_ref, dest_next_ref, y_ref, gate_ref, x1_ref, gt_ref, gpost_ref, o_ref, buf, sem):
    i = pl.program_id(0)
    tm = x1_ref.shape[0]
    cur = i % 2

    def gather(d_ref, slot):
        def start(r, carry):
            for k in range(TOP_K):
                pltpu.make_async_copy(y_ref.at[_tile_rows(d_ref[k, r], 1)],
                                      buf.at[slot, k, _tile_rows(r, 1)], sem.at[slot]).start(priority=k % 2)
            return carry

        lax.fori_loop(0, tm, start, 0, unroll=4)

    @pl.when(i == 0)
    def _():
        gather(dest_ref, 0)

    @pl.when(i + 1 < pl.num_programs(0))
    def _():
        gather(dest_next_ref, 1 - cur)

    for k in range(TOP_K):
        pltpu.make_async_copy(y_ref.at[_tile_rows(0, tm)], buf.at[cur, k], sem.at[cur]).wait()

    gate = gate_ref[...]
    acc = gate[:, 0:1] * _load_token_tiles(buf.at[cur, 0], tm)
    for k in range(1, TOP_K):
        acc = acc + gate[:, k:k + 1] * _load_token_tiles(buf.at[cur, k], tm)
    o_ref[...] = x1_ref[...] + gt_ref[...] * (_rms(acc) * gpost_ref[...])


def _combine(dest, y, gate4, x1, gt, gpost):
    s, d = x1.shape
    tm = COMBINE_TILE // 2
    n = s // tm
    return pl.pallas_call(
        _combine_kernel,
        out_shape=jax.ShapeDtypeStruct((s, d), F32),
        grid=(n,),
        in_specs=[pl.BlockSpec((TOP_K, tm), lambda i: (0, i), memory_space=pltpu.SMEM),
                  pl.BlockSpec((TOP_K, tm), lambda i: (0, jnp.minimum(i + 1, n - 1)), memory_space=pltpu.SMEM),
                  pl.BlockSpec(memory_space=pl.ANY),
                  pl.BlockSpec((tm, TOP_K), lambda i: (i, 0)),
                  pl.BlockSpec((tm, d), lambda i: (i, 0)),
                  pl.BlockSpec((1, d), lambda i: (0, 0)),
                  pl.BlockSpec((1, d), lambda i: (0, 0))],
        out_specs=pl.BlockSpec((tm, d), lambda i: (i, 0)),
        scratch_shapes=[pltpu.VMEM((2, TOP_K, tm * SUBLANES, LANES), F32), pltpu.SemaphoreType.DMA((2,))],
        compiler_params=_cparams(("arbitrary",)),
        name="combine",
    )(dest, dest, y, gate4, x1, gt, gpost)


def _layer(x2, mod, g_pre_mix, g_post_mix, w_in, b_forget, rel_bias, w_out,
           g_pre_ffn, g_post_ffn, w_router, b_router, w_gate_up, b_gate_up, w_down, b_down):
    s, d = x2.shape
    sh_m, sc_m, gt_m, sh_f, sc_f, gt_f = [mod[:, k * d:(k + 1) * d] for k in range(6)]
    n_qkv = 3 * (N_HEADS_FOX + N_HEADS_MOBA) * HEAD_DIM
    fox_w = N_HEADS_FOX * HEAD_DIM

    w_qkv = w_in[:, :n_qkv].astype(BF16)
    w_f = w_in[:, n_qkv:].T
    qkv, cum, sel, nrm, tr, nr = _inproj(x2, g_pre_mix.reshape(1, d), sc_m, sh_m, w_qkv, w_f,
                                         b_forget.reshape(-1, 1))
    y_a = _fox(qkv, tr, nr, cum, nrm)
    y_b = _moba(qkv, tr, sel, rel_bias)

    w_out_bf = w_out.astype(BF16)
    x1, h2, route, gate4, cnt = _outproj(
        y_a, y_b, x2, w_out_bf[:fox_w], w_out_bf[fox_w:], g_post_mix.reshape(1, d), gt_m,
        g_pre_ffn.reshape(1, d), sc_f, sh_f, w_router, b_router.reshape(1, -1))

    bm = EXPERT_BLOCK
    counts = cnt[0].astype(I32)
    pblk = (counts + bm - 1) // bm
    pend_blk = jnp.cumsum(pblk)
    pstart = ((pend_blk - pblk) * bm).astype(I32)
    n_rows = s * TOP_K + N_EXPERTS * bm
    n_blk = n_rows // bm
    block_e = jnp.minimum(jnp.sum(pend_blk[None, :] <= jnp.arange(n_blk)[:, None], axis=1),
                          N_EXPERTS - 1).astype(I32)
    n_used = pend_blk[-1:].astype(I32)

    dest = _dest(route, pstart)
    xs = _dispatch(pstart, pblk.astype(I32), n_used, dest, h2, n_rows)
    y = _experts(block_e, n_used, xs, w_gate_up, b_gate_up, w_down, b_down)
    return _combine(dest, y, gate4, x1, gt_f, g_post_ffn.reshape(1, d))


def kernel(x, c, w_ada, b_ada, g_pre_mix, g_post_mix, w_in, b_forget, rel_bias, w_out, g_pre_ffn, g_post_ffn, w_router, b_router, w_gate_up, b_gate_up, w_down, b_down):
    bsz, s, d = x.shape
    depth = w_ada.shape[0]
    outs = []
    for bi in range(bsz):
        x2 = x[bi]
        for l in range(depth):
            mod = _adaln(c[bi:bi + 1], w_ada[l], b_ada[l])
            x2 = _layer(x2, mod, g_pre_mix[l], g_post_mix[l], w_in[l], b_forget[l], rel_bias, w_out[l],
                        g_pre_ffn[l], g_post_ffn[l], w_router[l], b_router[l], w_gate_up[l], b_gate_up[l],
                        w_down[l], b_down[l])
        outs.append(x2)
    return outs[0].reshape(1, s, d) if bsz == 1 else jnp.stack(outs)
```

```python
import math

import numpy as np
import jax
import jax.numpy as jnp
from jax import lax
from jax.experimental import pallas as pl
from jax.experimental.pallas import tpu as pltpu

F32 = jnp.float32
BF16 = jnp.bfloat16
I32 = jnp.int32

HEAD_DIM = 64
N_HEADS_FOX = 8
N_HEADS_MOBA = 8
PAIR = 2 * HEAD_DIM
MOBA_BLOCK = 256
MOBA_TOPK = 3
NUM_BUCKETS = 32
MAX_DISTANCE = 128
N_EXPERTS = 32
TOP_K = 4
SWIGLU_LIMIT = 7.0
SWIGLU_ALPHA = 1.702
RMS_EPS = 1e-6
NEG = -(2.0 ** 100)
M_INIT = -(2.0 ** 99)
LOG2E = math.log2(math.e)
SUM_ROWS = 16
SUBLANES = 8
LANES = 128
EXP_UNDERFLOW = 90.0
VMEM_LIMIT = 56 * 1024 * 1024

ROW_TILE = 512
FOX_TILE = 256
EXPERT_BLOCK = 512
COMBINE_TILE = 1024
MOBA_LAG = 1
MOBA_UNROLL = 4

NT_DIMS = (((1,), (1,)), ((), ()))


def _cparams(sem):
    return pltpu.CompilerParams(dimension_semantics=sem, vmem_limit_bytes=VMEM_LIMIT)


def _rms(x):
    return x * lax.rsqrt(jnp.mean(x * x, axis=-1, keepdims=True) + RMS_EPS)


def _adaln_kernel(c_ref, w_ref, b_ref, o_ref):
    c = c_ref[...]
    cond = c * jax.nn.sigmoid(c)
    o_ref[...] = jnp.dot(cond, w_ref[...], preferred_element_type=F32,
                         precision=lax.Precision.HIGHEST) + b_ref[...]


def _adaln(c, w_ada, b_ada):
    d = c.shape[-1]
    n = w_ada.shape[-1]
    c8 = jnp.broadcast_to(c.reshape(1, d), (8, d))
    out = pl.pallas_call(
        _adaln_kernel,
        out_shape=jax.ShapeDtypeStruct((8, n), F32),
        grid=(n // d,),
        in_specs=[pl.BlockSpec((8, d), lambda j: (0, 0)),
                  pl.BlockSpec((d, d), lambda j: (0, j)),
                  pl.BlockSpec((1, d), lambda j: (0, j))],
        out_specs=pl.BlockSpec((8, d), lambda j: (0, j)),
        compiler_params=_cparams(("arbitrary",)),
        name="adaln",
    )(c8, w_ada, b_ada.reshape(1, n))
    return out[0:1]


def _inproj_kernel(x_ref, g_ref, sc_ref, sh_ref, w_ref, wf_ref, bf_ref,
                   qkv_ref, cum_ref, sel_ref, nrm_ref, tr_ref, nr_ref, km_scr, carry_scr):
    i = pl.program_id(0)
    tm = x_ref.shape[0]
    nblk = km_scr.shape[0]

    @pl.when(i == 0)
    def _():
        km_scr[...] = jnp.zeros_like(km_scr)
        carry_scr[...] = jnp.zeros_like(carry_scr)

    x = x_ref[...]
    h = _rms(x) * g_ref[...] * (1.0 + sc_ref[...]) + sh_ref[...]
    hb = h.astype(BF16)

    width = N_HEADS_FOX * HEAD_DIM
    hsel = jnp.where(lax.broadcasted_iota(I32, (width, N_HEADS_FOX), 0) // HEAD_DIM
                     == lax.broadcasted_iota(I32, (width, N_HEADS_FOX), 1), 1.0, 0.0)
    kb = None
    tr_slot = {3: 0, 5: 1, 0: 2, 2: 3}
    qbt = None
    for c in range(6):
        pc = jnp.dot(hb, w_ref[:, c * width:(c + 1) * width], preferred_element_type=F32)
        if c in tr_slot:
            n = tr_slot[c]
            pct = pc.T
            if c == 3:
                qbt = pct
            if c == 0 or c == 3:
                pct = pct * (LOG2E * HEAD_DIM ** -0.5)
            tr_ref[n * width:(n + 1) * width, :] = pct.astype(BF16)
        if c == 1 or c == 4:
            qkv_ref[:, (c // 3) * width:(c // 3 + 1) * width] = pc.astype(BF16)
        if c < 2:
            sq = (pc * (HEAD_DIM ** -0.5) if c == 0 else pc).astype(BF16).astype(F32)
            n2 = jnp.dot((sq * sq).astype(BF16), hsel.astype(BF16), preferred_element_type=F32)
            nrm_ref[0, c:c + 1, :] = jnp.max(n2, axis=0, keepdims=True)
        if c == 4:
            kb = pc

    ft = lax.dot_general(wf_ref[...].astype(BF16), hb, NT_DIMS, preferred_element_type=F32)
    z = ft + bf_ref[...]
    logf = -(jnp.maximum(-z, 0.0) + jnp.log1p(jnp.exp(-jnp.abs(z))))
    lane = lax.broadcasted_iota(I32, logf.shape, 1)
    cs = logf
    sh = 1
    while sh < tm:
        cs = cs + jnp.where(lane >= sh, pltpu.roll(cs, sh, axis=1), 0.0)
        sh *= 2
    base = jnp.zeros_like(cs)
    for b in range(1, tm // FOX_TILE):
        base = jnp.where(lane >= b * FOX_TILE, cs[:, b * FOX_TILE - 1:b * FOX_TILE], base)
    nr_t = (-LOG2E * (cs - base)).T
    for pp in range(nr_ref.shape[0]):
        nr_ref[pp] = nr_t[:, 2 * pp:2 * pp + 2]
    cs = cs + carry_scr[...]
    cum_ref[...] = cs
    carry_scr[...] = cs[:, tm - 1:tm]

    nb_tile = tm // MOBA_BLOCK
    for b in range(nb_tile):
        kmean = jnp.sum(kb[b * MOBA_BLOCK:(b + 1) * MOBA_BLOCK], axis=0, keepdims=True) * (1.0 / MOBA_BLOCK)
        km_scr[pl.ds(i * nb_tile + b, 1), :] = kmean

    km = km_scr[...]
    blk = lax.broadcasted_iota(I32, (nblk, tm), 0)
    col = lax.broadcasted_iota(I32, (nblk, tm), 1)
    own = i * nb_tile + col // MOBA_BLOCK
    for hd in range(N_HEADS_MOBA):
        hs = slice(hd * HEAD_DIM, (hd + 1) * HEAD_DIM)
        g = jnp.dot(km[:, hs], qbt[hs, :], preferred_element_type=F32,
                    precision=lax.Precision.HIGHEST)
        g = jnp.where(blk < own, g, -jnp.inf)
        sel = jnp.zeros(g.shape, dtype=jnp.bool_)
        for _ in range(MOBA_TOPK):
            m = jnp.max(g, axis=0, keepdims=True)
            first = jnp.min(jnp.where(g == m, blk, nblk), axis=0, keepdims=True)
            pick = (blk == first) & (m > -jnp.inf)
            sel = sel | pick
            g = jnp.where(pick, -jnp.inf, g)
        sel_ref[hd] = jnp.where(sel, 0.0, NEG)


def _inproj(x2, g, sc, sh, w_qkv, w_f, b_f):
    s, d = x2.shape
    tm = ROW_TILE
    nblk = s // MOBA_BLOCK
    n = w_qkv.shape[1]
    width = N_HEADS_FOX * HEAD_DIM
    n_rows_out = 2 * width
    n_tr = 4 * width
    assert tm % FOX_TILE == 0 and N_HEADS_FOX == N_HEADS_MOBA
    return pl.pallas_call(
        _inproj_kernel,
        out_shape=(jax.ShapeDtypeStruct((s, n_rows_out), BF16),
                   jax.ShapeDtypeStruct((N_HEADS_FOX, s), F32),
                   jax.ShapeDtypeStruct((N_HEADS_MOBA, nblk, s), F32),
                   jax.ShapeDtypeStruct((s // tm, 2, N_HEADS_FOX), F32),
                   jax.ShapeDtypeStruct((n_tr, s), BF16),
                   jax.ShapeDtypeStruct((N_HEADS_FOX // 2, s, 2), F32)),
        grid=(s // tm,),
        in_specs=[pl.BlockSpec((tm, d), lambda i: (i, 0)),
                  pl.BlockSpec((1, d), lambda i: (0, 0)),
                  pl.BlockSpec((1, d), lambda i: (0, 0)),
                  pl.BlockSpec((1, d), lambda i: (0, 0)),
                  pl.BlockSpec((d, n), lambda i: (0, 0)),
                  pl.BlockSpec((N_HEADS_FOX, d), lambda i: (0, 0)),
                  pl.BlockSpec((N_HEADS_FOX, 1), lambda i: (0, 0))],
        out_specs=(pl.BlockSpec((tm, n_rows_out), lambda i: (i, 0)),
                   pl.BlockSpec((N_HEADS_FOX, tm), lambda i: (0, i)),
                   pl.BlockSpec((N_HEADS_MOBA, nblk, tm), lambda i: (0, 0, i)),
                   pl.BlockSpec((1, 2, N_HEADS_FOX), lambda i: (i, 0, 0)),
                   pl.BlockSpec((n_tr, tm), lambda i: (0, i)),
                   pl.BlockSpec((N_HEADS_FOX // 2, tm, 2), lambda i: (0, i, 0))),
        scratch_shapes=[pltpu.VMEM((nblk, N_HEADS_MOBA * HEAD_DIM), F32),
                        pltpu.VMEM((N_HEADS_FOX, 1), F32)],
        compiler_params=_cparams(("arbitrary",)),
        name="inproj",
    )(x2, g, sc, sh, w_qkv, w_f, b_f)


def _fox_kernel(jlo_ref, cp_ref, qt_ref, k_ref, vt_ref, nr_ref, o_ref, m_ref, a_ref):
    p = pl.program_id(0)
    g = pl.program_id(1)
    j_first = jlo_ref[p * pl.num_programs(1) + g]
    tq = qt_ref.shape[1]
    t = tq // 2

    m_ref[...] = jnp.full(m_ref.shape, M_INIT, F32)
    a_ref[...] = jnp.zeros(a_ref.shape, F32)
    ones_rows = jnp.where(lax.broadcasted_iota(I32, (SUM_ROWS, t), 0) == 0, 1.0, 0.0).astype(BF16)
    in_a = lax.broadcasted_iota(I32, (1, tq), 1) < t
    krow = lax.broadcasted_iota(I32, (t, tq), 0)
    qcol = lax.broadcasted_iota(I32, (t, tq), 1)
    future_a = (qcol < t) & (krow > qcol)
    future_b = (qcol >= t) & (krow > qcol - t)

    def scores(j, future):
        k0 = pl.multiple_of(j * t, t)
        out = []
        for hd in range(2):
            kh = k_ref[pl.ds(k0, t), hd * HEAD_DIM:(hd + 1) * HEAD_DIM]
            qh = qt_ref[hd * HEAD_DIM:(hd + 1) * HEAD_DIM, :]
            s = jnp.dot(kh, qh, preferred_element_type=F32) + nr_ref[0, pl.ds(k0, t), hd:hd + 1]
            out.append(s if future is None else jnp.where(future, NEG, s))
        return out

    def shifts(j, a_on, b_on):
        rows = []
        for hd in range(2):
            h = 2 * p + hd
            sa = LOG2E * (cp_ref[h, 2 * g] - cp_ref[h, j])
            sb = LOG2E * (cp_ref[h, 2 * g + 1] - cp_ref[h, j])
            rows.append(jnp.where(in_a, jnp.where(a_on, sa, NEG), jnp.where(b_on, sb, NEG)))
        return rows

    def softmax(ss, shift_rows):
        ps, alphas = [], []
        for hd in range(2):
            m_old = m_ref[hd]
            m_new = jnp.maximum(m_old, jnp.max(ss[hd], axis=0, keepdims=True) + shift_rows[hd])
            alphas.append(jnp.exp2(m_old - m_new))
            m_ref[hd] = m_new
            ps.append(jnp.exp2(ss[hd] - (m_new - shift_rows[hd])).astype(BF16))
        return ps, alphas

    def accumulate(j, ps, alphas):
        k0 = pl.multiple_of(j * t, t)
        for hd in range(2):
            vh = vt_ref[hd * HEAD_DIM:(hd + 1) * HEAD_DIM, pl.ds(k0, t)]
            vcat = jnp.concatenate([vh, ones_rows], axis=0)
            a_ref[hd] = a_ref[hd] * alphas[hd] + jnp.dot(vcat, ps[hd], preferred_element_type=F32)

    def body(j, carry):
        accumulate(j, *softmax(scores(j, None), shifts(j, True, True)))
        return carry

    lax.fori_loop(j_first, 2 * g - 1, body, 0)
    j1 = jnp.maximum(2 * g - 1, 0)
    j2 = 2 * g
    j3 = 2 * g + 1
    ss1 = scores(j1, None)
    ss2 = scores(j2, future_a)
    ss3 = scores(j3, future_b)
    w1 = softmax(ss1, shifts(j1, g >= 1, g >= 1))
    w2 = softmax(ss2, shifts(j2, True, True))
    accumulate(j1, *w1)
    w3 = softmax(ss3, shifts(j3, False, True))
    accumulate(j2, *w2)
    accumulate(j3, *w3)
    out_t = jnp.concatenate([a_ref[hd, 0:HEAD_DIM, :] / a_ref[hd, HEAD_DIM:HEAD_DIM + 1, :]
                             for hd in range(2)], axis=0)
    o_ref[...] = out_t.T.astype(o_ref.dtype)


def _fox_first_tile(cum, nrm, t):
    cend = cum[:, t - 1::t]
    nt = cend.shape[1]
    cprev = jnp.concatenate([jnp.zeros((cend.shape[0], 1), F32), cend[:, :-1]], axis=1)
    rep = nt // nrm.shape[0]
    qn = jnp.repeat(jnp.sqrt(nrm[:, 0, :]).T, rep, axis=1)
    kn = jnp.repeat(jnp.sqrt(nrm[:, 1, :]).T, rep, axis=1)
    gap = (1.02 * qn[:, :, None] * (kn[:, None, :] + kn[:, :, None])
           + cprev[:, :, None] - cend[:, None, :])
    jj = jnp.arange(nt)[None, None, :]
    ii = jnp.arange(nt)[None, :, None]
    needed = (jj < ii) & jnp.logical_not(gap <= -EXP_UNDERFLOW)
    needed = needed[0::2] | needed[1::2]
    first = jnp.min(jnp.where(needed, jj, ii), axis=2)
    return first.reshape(-1).astype(I32)


def _fox(qkv, tr, nr, cum, nrm):
    s = qkv.shape[0]
    t = FOX_TILE
    npair = N_HEADS_FOX // 2
    tq = 2 * t
    jlo = _fox_first_tile(cum, nrm, t).reshape(npair, s // tq, 2).min(axis=2).reshape(-1)
    cend = cum[:, t - 1::t]
    cprev = jnp.concatenate([jnp.zeros((cend.shape[0], 1), F32), cend[:, :-1]], axis=1)
    q_rows = 2 * (N_HEADS_MOBA // 2)
    grid_spec = pltpu.PrefetchScalarGridSpec(
        num_scalar_prefetch=1,
        grid=(npair, s // tq),
        in_specs=[pl.BlockSpec(memory_space=pltpu.SMEM),
                  pl.BlockSpec((PAIR, tq), lambda p, i, jl: (q_rows + p, i)),
                  pl.BlockSpec((s, PAIR), lambda p, i, jl: (0, p)),
                  pl.BlockSpec((PAIR, s), lambda p, i, jl: (q_rows + npair + p, 0)),
                  pl.BlockSpec((1, s, 2), lambda p, i, jl: (p, 0, 0))],
        out_specs=pl.BlockSpec((tq, PAIR), lambda p, i, jl: (i, p)),
        scratch_shapes=[pltpu.VMEM((2, 1, tq), F32), pltpu.VMEM((2, HEAD_DIM + SUM_ROWS, tq), F32)],
    )
    return pl.pallas_call(
        _fox_kernel,
        out_shape=jax.ShapeDtypeStruct((s, npair * PAIR), BF16),
        grid_spec=grid_spec,
        compiler_params=_cparams(("arbitrary", "arbitrary")),
        name="fox",
    )(jlo, cprev, tr, qkv, tr, nr)


def _t5_bucket_np(dist):
    dist = np.maximum(dist, 0)
    max_exact = NUM_BUCKETS // 2
    d = np.maximum(dist, 1).astype(np.float32)
    large = max_exact + (np.log(d / np.float32(max_exact)) / np.float32(math.log(MAX_DISTANCE / max_exact))
                         * np.float32(NUM_BUCKETS - max_exact)).astype(np.int32)
    large = np.minimum(large, NUM_BUCKETS - 1)
    return np.where(dist < max_exact, dist, large).astype(np.int32)


def _moba_kernel(rb_ref, qt_ref, k_ref, vt_ref, sel_ref, bkt_ref, o_ref,
                 m_ref, a_ref, bias_scr, *bufs):
    s_bufs = bufs[0:MOBA_UNROLL]
    p_bufs = bufs[MOBA_UNROLL:2 * MOBA_UNROLL]
    al_bufs = bufs[2 * MOBA_UNROLL:3 * MOBA_UNROLL]
    p = pl.program_id(0)
    g = pl.program_id(1)
    tq = qt_ref.shape[1]
    t = tq // 2

    @pl.when(g == 0)
    def _():
        r = lax.broadcasted_iota(I32, (t, t), 0)
        c = lax.broadcasted_iota(I32, (t, t), 1)
        zero = jnp.zeros((t, t), F32)
        for hd in range(2):
            h = 2 * p + hd
            far = rb_ref[(NUM_BUCKETS - 1) * N_HEADS_MOBA + h]
            tiles = []
            for w in range(2):
                bkt = bkt_ref[w]
                acc = jnp.zeros(bkt.shape, F32)
                for kk in range(NUM_BUCKETS):
                    acc = acc + jnp.where(bkt == kk, rb_ref[kk * N_HEADS_MOBA + h], 0.0)
                tiles.append((acc - far) * LOG2E)
            prev_t = tiles[0]
            own_t = jnp.where(r <= c, tiles[1], NEG)
            bias_scr[hd, 0] = jnp.concatenate([prev_t, zero], axis=1)
            bias_scr[hd, 1] = jnp.concatenate([own_t, prev_t], axis=1)
            bias_scr[hd, 2] = jnp.concatenate([zero, own_t], axis=1)

    qt = qt_ref[...]
    top = lax.broadcasted_iota(I32, qt.shape, 0) < HEAD_DIM
    zq = jnp.zeros_like(qt)
    qth = (jnp.where(top, qt, zq), jnp.where(top, zq, qt))
    m_ref[...] = jnp.full(m_ref.shape, M_INIT, F32)
    a_ref[...] = jnp.zeros(a_ref.shape, F32)

    ones_rows = jnp.where(lax.broadcasted_iota(I32, (SUM_ROWS, t), 0) == 0, 1.0, 0.0).astype(BF16)
    in_a = lax.broadcasted_iota(I32, (1, tq), 1) < t

    n_far = jnp.maximum(2 * g - 1, 0)

    n_key_tiles = k_ref.shape[0] // t

    def produce(j, s_buf):
        j = jnp.minimum(j, n_key_tiles - 1)
        k0 = pl.multiple_of(j * t, t)
        kt = k_ref[pl.ds(k0, t), :]
        for hd in range(2):
            s_buf[hd] = jnp.dot(kt, qth[hd], preferred_element_type=F32)

    def softmax(s_buf, p_buf, al_buf, selrows, w):
        for hd in range(2):
            s = s_buf[hd]
            if w is not None:
                s = s + bias_scr[hd, w]
            smax = jnp.max(s, axis=0, keepdims=True)
            m_old = m_ref[hd]
            m_new = jnp.maximum(m_old, smax + selrows[hd])
            shift = m_new - selrows[hd]
            al_buf[hd] = jnp.exp2(m_old - m_new)
            m_ref[hd] = m_new
            p_buf[hd * t:(hd + 1) * t, :] = jnp.exp2(s - shift).astype(BF16)

    def far_rows(j):
        return [jnp.where(j < n_far, sel_ref[hd, pl.ds(j, 1), :], NEG) for hd in range(2)]

    def accumulate(j, p_buf, al_buf):
        k0 = pl.multiple_of(j * t, t)
        for hd in range(2):
            vh = vt_ref[hd * HEAD_DIM:(hd + 1) * HEAD_DIM, pl.ds(k0, t)]
            vcat = jnp.concatenate([vh, ones_rows], axis=0)
            a_ref[hd] = (a_ref[hd] * al_buf[hd]
                         + jnp.dot(vcat, p_buf[hd * t:(hd + 1) * t, :], preferred_element_type=F32))

    un = MOBA_UNROLL
    lag = MOBA_LAG
    for n in range(un - lag, un):
        p_bufs[n][...] = jnp.zeros_like(p_bufs[n])
        al_bufs[n][...] = jnp.ones_like(al_bufs[n])
    produce(0, s_bufs[0])

    def body(u, carry):
        j0 = un * u
        for n in range(un):
            accumulate(jnp.maximum(j0 + n - lag, 0), p_bufs[(n - lag) % un], al_bufs[(n - lag) % un])
            softmax(s_bufs[n], p_bufs[n], al_bufs[n], far_rows(j0 + n), None)
            produce(j0 + n + 1, s_bufs[(n + 1) % un])
        return carry

    n_trips = (n_far + un - 1) // un
    lax.fori_loop(0, n_trips, body, 0)

    j1 = jnp.maximum(2 * g - 1, 0)
    j2 = 2 * g
    j3 = 2 * g + 1
    rows1 = [jnp.where(g >= 1, sel_ref[hd, pl.ds(j1, 1), :], NEG) for hd in range(2)]
    rows2 = [jnp.where(in_a, 0.0, sel_ref[hd, pl.ds(j2, 1), :]) for hd in range(2)]
    rows3 = [jnp.where(in_a, NEG, 0.0)] * 2
    produce(j1, s_bufs[0])
    produce(j2, s_bufs[1])
    for n in range(un - lag, un):
        accumulate(jnp.maximum(un * n_trips - un + n, 0), p_bufs[n], al_bufs[n])
    softmax(s_bufs[0], p_bufs[0], al_bufs[0], rows1, 0)
    produce(j3, s_bufs[0])
    softmax(s_bufs[1], p_bufs[1], al_bufs[1], rows2, 1)
    accumulate(j1, p_bufs[0], al_bufs[0])
    softmax(s_bufs[0], p_bufs[2], al_bufs[2], rows3, 2)
    accumulate(j2, p_bufs[1], al_bufs[1])
    accumulate(j3, p_bufs[2], al_bufs[2])
    out_t = jnp.concatenate([a_ref[hd, 0:HEAD_DIM, :] / a_ref[hd, HEAD_DIM:HEAD_DIM + 1, :]
                             for hd in range(2)], axis=0)
    o_ref[...] = out_t.T.astype(o_ref.dtype)


def _moba(qkv, tr, sel, rel_bias):
    s = qkv.shape[0]
    t = MOBA_BLOCK
    npair = N_HEADS_MOBA // 2
    nblk = s // t
    kcol = N_HEADS_FOX // 2
    a = np.arange(t)[None, :]
    b = np.arange(t)[:, None]
    bkt = jnp.asarray(np.stack([_t5_bucket_np(t + a - b), _t5_bucket_np(a - b)]))
    tq = 2 * t
    grid_spec = pltpu.PrefetchScalarGridSpec(
        num_scalar_prefetch=1,
        grid=(npair, s // tq),
        in_specs=[pl.BlockSpec((PAIR, tq), lambda p, i, rb: (p, i)),
                  pl.BlockSpec((s, PAIR), lambda p, i, rb: (0, kcol + p)),
                  pl.BlockSpec((PAIR, s), lambda p, i, rb: (npair + p, 0)),
                  pl.BlockSpec((2, nblk, tq), lambda p, i, rb: (p, 0, i)),
                  pl.BlockSpec((2, t, t), lambda p, i, rb: (0, 0, 0))],
        out_specs=pl.BlockSpec((tq, PAIR), lambda p, i, rb: (i, p)),
        scratch_shapes=[pltpu.VMEM((2, 1, tq), F32),
                        pltpu.VMEM((2, HEAD_DIM + SUM_ROWS, tq), F32), pltpu.VMEM((2, 3, t, tq), F32)]
        + [pltpu.VMEM((2, t, tq), F32)] * MOBA_UNROLL
        + [pltpu.VMEM((2 * t, tq), BF16)] * MOBA_UNROLL
        + [pltpu.VMEM((2, 1, tq), F32)] * MOBA_UNROLL,
    )
    return pl.pallas_call(
        _moba_kernel,
        out_shape=jax.ShapeDtypeStruct((s, npair * PAIR), BF16),
        grid_spec=grid_spec,
        compiler_params=_cparams(("arbitrary", "arbitrary")),
        name="moba",
    )(rel_bias.reshape(-1), tr, qkv, tr, sel, bkt)


def _store_token_tiles(ref, val):
    n = val.shape[0]
    for c in range(SUBLANES):
        ref[pl.ds(c, n, stride=SUBLANES), :] = val[:, c * LANES:(c + 1) * LANES]


def _load_token_tiles(ref, n):
    return jnp.concatenate([ref[pl.ds(c, n, stride=SUBLANES), :] for c in range(SUBLANES)], axis=1)


def _tile_rows(r0, n):
    start = r0 * SUBLANES
    if not isinstance(start, int):
        start = pl.multiple_of(start, SUBLANES)
    return pl.ds(start, n * SUBLANES)


def _outproj_kernel(ya_ref, yb_ref, x_ref, wa_ref, wb_ref, gpost_ref, gt_ref, gpre_ref,
                    sc_ref, sh_ref, wr_ref, br_ref,
                    x1_ref, h2_ref, route_ref, gate_ref, cnt_ref, carry_scr):
    i = pl.program_id(0)
    tm = x_ref.shape[0]

    @pl.when(i == 0)
    def _():
        carry_scr[...] = jnp.zeros_like(carry_scr)

    y = (jnp.dot(ya_ref[...], wa_ref[...], preferred_element_type=F32)
         + jnp.dot(yb_ref[...], wb_ref[...], preferred_element_type=F32))
    x1 = x_ref[...] + gt_ref[...] * (_rms(y) * gpost_ref[...])
    x1_ref[...] = x1
    h2 = _rms(x1) * gpre_ref[...] * (1.0 + sc_ref[...]) + sh_ref[...]
    _store_token_tiles(h2_ref, h2)

    h_hi = h2.astype(BF16)
    h_lo = (h2 - h_hi.astype(F32)).astype(BF16)
    logits = (jnp.dot(h_hi, wr_ref[0], preferred_element_type=F32)
              + jnp.dot(h_hi, wr_ref[1], preferred_element_type=F32)
              + jnp.dot(h_lo, wr_ref[0], preferred_element_type=F32)) + br_ref[...]
    ne = cnt_ref.shape[0]
    g = logits.T[0:ne, :]
    expert = lax.broadcasted_iota(I32, g.shape, 0)
    mask = jnp.zeros(g.shape, F32)
    vals, firsts, picks = [], [], []
    for _ in range(TOP_K):
        m = jnp.max(g, axis=0, keepdims=True)
        first = jnp.min(jnp.where(g == m, expert, ne), axis=0, keepdims=True)
        pick = expert == first
        mask = jnp.where(pick, 1.0, mask)
        g = jnp.where(pick, -jnp.inf, g)
        vals.append(m)
        firsts.append(first)
        picks.append(pick)
    ex = [jnp.exp(v - vals[0]) for v in vals]
    den = ex[0] + ex[1] + ex[2] + ex[3]
    gates = [e / den for e in ex]

    r = lax.broadcasted_iota(I32, (tm, tm), 0)
    c = lax.broadcasted_iota(I32, (tm, tm), 1)
    earlier = jnp.where(r < c, 1.0, 0.0).astype(BF16)
    before = jnp.dot(mask.astype(BF16), earlier, preferred_element_type=F32) + carry_scr[...]
    total = carry_scr[...] + jnp.sum(mask, axis=1, keepdims=True)
    carry_scr[...] = total
    cnt_ref[...] = jnp.broadcast_to(total, cnt_ref.shape)

    ranks = [jnp.sum(jnp.where(pk, before, 0.0), axis=0, keepdims=True).astype(I32) for pk in picks]
    route_ref[...] = jnp.concatenate(firsts + ranks, axis=0)
    gate_rows = jnp.concatenate(gates + [jnp.zeros((LANES - TOP_K, tm), F32)], axis=0)
    gate_ref[...] = gate_rows.T[:, 0:TOP_K]


def _outproj(mix_a, mix_b, x2, w_a, w_b, gpost, gt, gpre, sc, sh, w_router, b_router):
    s, d = x2.shape
    tm = ROW_TILE
    ne = w_router.shape[1]
    wa = mix_a.shape[1]
    w_router = jnp.pad(w_router, ((0, 0), (0, LANES - ne)))
    b_router = jnp.pad(b_router, ((0, 0), (0, LANES - ne)))
    w_hi = w_router.astype(BF16)
    w_router_hl = jnp.stack([w_hi, (w_router - w_hi.astype(F32)).astype(BF16)])
    row = lambda i: (i, 0)
    fix = lambda i: (0, 0)
    vec = pl.BlockSpec((1, d), fix)
    return pl.pallas_call(
        _outproj_kernel,
        out_shape=(jax.ShapeDtypeStruct((s, d), F32),
                   jax.ShapeDtypeStruct((s * SUBLANES, LANES), F32),
                   jax.ShapeDtypeStruct((2 * TOP_K, s), I32),
                   jax.ShapeDtypeStruct((s, TOP_K), F32),
                   jax.ShapeDtypeStruct((ne, LANES), F32)),
        grid=(s // tm,),
        in_specs=[pl.BlockSpec((tm, wa), row), pl.BlockSpec((tm, wa), row), pl.BlockSpec((tm, d), row),
                  pl.BlockSpec((wa, d), fix), pl.BlockSpec((wa, d), fix),
                  vec, vec, vec, vec, vec,
                  pl.BlockSpec((2, d, LANES), lambda i: (0, 0, 0)), pl.BlockSpec((1, LANES), fix)],
        out_specs=(pl.BlockSpec((tm, d), row), pl.BlockSpec((tm * SUBLANES, LANES), row),
                   pl.BlockSpec((2 * TOP_K, tm), lambda i: (0, i)),
                   pl.BlockSpec((tm, TOP_K), row), pl.BlockSpec((ne, LANES), fix)),
        scratch_shapes=[pltpu.VMEM((ne, 1), F32)],
        compiler_params=_cparams(("arbitrary",)),
        name="outproj",
    )(mix_a, mix_b, x2, w_a, w_b, gpost, gt, gpre, sc, sh, w_router_hl, b_router)


def _dest_kernel(route_ref, pstart_ref, o_ref):
    route = route_ref[...]
    tm = route.shape[1]
    ne = pstart_ref.shape[0]
    expert = lax.broadcasted_iota(I32, (ne, tm), 0)
    for k in range(TOP_K):
        start = jnp.sum(jnp.where(expert == route[k:k + 1, :], pstart_ref[...], 0), axis=0, keepdims=True)
        o_ref[k:k + 1, :] = start + route[TOP_K + k:TOP_K + k + 1, :]


def _dest(route, pstart):
    s = route.shape[1]
    tm = min(4 * ROW_TILE, s)
    return pl.pallas_call(
        _dest_kernel,
        out_shape=jax.ShapeDtypeStruct((TOP_K, s), I32),
        grid=(s // tm,),
        in_specs=[pl.BlockSpec((2 * TOP_K, tm), lambda i: (0, i)),
                  pl.BlockSpec((N_EXPERTS, 1), lambda i: (0, 0))],
        out_specs=pl.BlockSpec((TOP_K, tm), lambda i: (0, i)),
        compiler_params=_cparams(("arbitrary",)),
        name="dest",
    )(route, pstart.reshape(-1, 1))


def _dispatch_kernel(pstart_ref, pblk_ref, nu_ref, dest_ref, h_ref, xs_ref,
                     zero_scr, sem, zsem):
    tm = h_ref.shape[0] // SUBLANES
    bm = zero_scr.shape[0] // SUBLANES
    n_blk = xs_ref.shape[0] // (bm * SUBLANES)

    rows = _tile_rows

    @pl.when(pl.program_id(0) == 0)
    def _():
        zero_scr[...] = jnp.zeros_like(zero_scr)

        def zero_copy(row0):
            return pltpu.make_async_copy(zero_scr, xs_ref.at[rows(row0, bm)], zsem)

        for phase in range(2):
            for e in range(N_EXPERTS):
                last = pstart_ref[e] + (pblk_ref[e] - 1) * bm
                tail = (n_blk - N_EXPERTS + e) * bm
                for cond, row0 in ((pblk_ref[e] > 0, last), (n_blk - N_EXPERTS + e >= nu_ref[0], tail)):
                    @pl.when(cond)
                    def _():
                        if phase == 0:
                            zero_copy(row0).start()
                        else:
                            zero_copy(row0).wait()

    def row_copy(r, k):
        dst = dest_ref[k, r]
        return pltpu.make_async_copy(h_ref.at[rows(r, 1)], xs_ref.at[rows(dst, 1)], sem)

    def start(r, carry):
        for k in range(TOP_K):
            row_copy(r, k).start(priority=k % 2)
        return carry

    lax.fori_loop(0, tm, start, 0, unroll=4)
    for k in range(TOP_K):
        pltpu.make_async_copy(h_ref, xs_ref.at[rows(0, tm)], sem).wait()


def _dispatch(pstart, pblk, n_used, dest, h2, n_rows):
    s = h2.shape[0] // SUBLANES
    tm = COMBINE_TILE
    grid_spec = pltpu.PrefetchScalarGridSpec(
        num_scalar_prefetch=3,
        grid=(s // tm,),
        in_specs=[pl.BlockSpec((TOP_K, tm), lambda i, *_: (0, i), memory_space=pltpu.SMEM),
                  pl.BlockSpec((tm * SUBLANES, LANES), lambda i, *_: (i, 0))],
        out_specs=pl.BlockSpec(memory_space=pl.ANY),
        scratch_shapes=[pltpu.VMEM((EXPERT_BLOCK * SUBLANES, LANES), F32), pltpu.SemaphoreType.DMA,
                        pltpu.SemaphoreType.DMA],
    )
    return pl.pallas_call(
        _dispatch_kernel,
        out_shape=jax.ShapeDtypeStruct((n_rows * SUBLANES, LANES), F32),
        grid_spec=grid_spec,
        compiler_params=_cparams(("arbitrary",)),
        name="dispatch",
    )(pstart, pblk, n_used, dest, h2)


def _experts_kernel(be_ref, nu_ref, xs_ref, wgu_ref, bgu_ref, wd_ref, bd_ref, y_ref,
                    wgu_bf, wd_bf):
    b = pl.program_id(0)
    d_exp = wd_ref.shape[1]
    prev = be_ref[jnp.maximum(b - 1, 0)]
    changed = (b == 0) | (be_ref[b] != prev)

    @pl.when((b < nu_ref[0]) & changed)
    def _():
        rows = LANES

        def cast_gu(c, carry):
            r0 = pl.multiple_of(c * rows, rows)
            wgu_bf[pl.ds(r0, rows), :] = wgu_ref[0, pl.ds(r0, rows), :].astype(BF16)
            return carry

        def cast_d(c, carry):
            r0 = pl.multiple_of(c * rows, rows)
            wd_bf[pl.ds(r0, rows), :] = wd_ref[0, pl.ds(r0, rows), :].astype(BF16)
            return carry

        lax.fori_loop(0, wgu_ref.shape[1] // rows, cast_gu, 0)
        lax.fori_loop(0, wd_ref.shape[1] // rows, cast_d, 0)

    @pl.when(b < nu_ref[0])
    def _():
        bm = xs_ref.shape[0] // SUBLANES
        xb = _load_token_tiles(xs_ref, bm).astype(BF16)
        hdn = jnp.dot(xb, wgu_bf[...], preferred_element_type=F32) + bgu_ref[0]
        x_glu = jnp.minimum(hdn[:, :d_exp], SWIGLU_LIMIT)
        x_lin = jnp.clip(hdn[:, d_exp:], -SWIGLU_LIMIT, SWIGLU_LIMIT)
        act = x_glu * jax.nn.sigmoid(SWIGLU_ALPHA * x_glu) * (x_lin + 1.0)
        _store_token_tiles(y_ref, jnp.dot(act.astype(BF16), wd_bf[...], preferred_element_type=F32)
                           + bd_ref[0])

    @pl.when(b >= nu_ref[0])
    def _():
        y_ref[...] = jnp.zeros_like(y_ref)


def _experts(block_e, n_used, xs, w_gate_up, b_gate_up, w_down, b_down):
    n_rows = xs.shape[0] // SUBLANES
    bm = EXPERT_BLOCK
    n_blk = n_rows // bm
    ne, d, two_de = w_gate_up.shape
    de = w_down.shape[1]
    assert d == SUBLANES * LANES

    def blk(b, be, nu):
        return jnp.minimum(b, nu[0] - 1)

    grid_spec = pltpu.PrefetchScalarGridSpec(
        num_scalar_prefetch=2,
        grid=(n_blk,),
        in_specs=[pl.BlockSpec((bm * SUBLANES, LANES), lambda b, be, nu: (blk(b, be, nu), 0)),
                  pl.BlockSpec((1, d, two_de), lambda b, be, nu: (be[blk(b, be, nu)], 0, 0)),
                  pl.BlockSpec((1, 1, two_de), lambda b, be, nu: (be[blk(b, be, nu)], 0, 0)),
                  pl.BlockSpec((1, de, d), lambda b, be, nu: (be[blk(b, be, nu)], 0, 0)),
                  pl.BlockSpec((1, 1, d), lambda b, be, nu: (be[blk(b, be, nu)], 0, 0))],
        out_specs=pl.BlockSpec((bm * SUBLANES, LANES), lambda b, be, nu: (b, 0)),
        scratch_shapes=[pltpu.VMEM((d, two_de), BF16), pltpu.VMEM((de, d), BF16)],
    )
    return pl.pallas_call(
        _experts_kernel,
        out_shape=jax.ShapeDtypeStruct((n_rows * SUBLANES, LANES), F32),
        grid_spec=grid_spec,
        compiler_params=_cparams(("arbitrary",)),
        name="experts",
    )(block_e, n_used, xs, w_gate_up, b_gate_up.reshape(ne, 1, two_de), w_down, b_down.reshape(ne, 1, d))


def _combine_kernel(dest_ref, dest_next_ref, y_ref, gate_ref, x1_ref, gt_ref, gpost_ref, o_ref, buf, sem):
    i = pl.program_id(0)
    tm = x1_ref.shape[0]
    cur = i % 2

    def gather(d_ref, slot):
        def start(r, carry):
            for k in range(TOP_K):
                pltpu.make_async_copy(y_ref.at[_tile_rows(d_ref[k, r], 1)],
                                      buf.at[slot, k, _tile_rows(r, 1)], sem.at[slot]).start(priority=k % 2)
            return carry

        lax.fori_loop(0, tm, start, 0, unroll=4)

    @pl.when(i == 0)
    def _():
        gather(dest_ref, 0)

    @pl.when(i + 1 < pl.num_programs(0))
    def _():
        gather(dest_next_ref, 1 - cur)

    for k in range(TOP_K):
        pltpu.make_async_copy(y_ref.at[_tile_rows(0, tm)], buf.at[cur, k], sem.at[cur]).wait()

    gate = gate_ref[...]
    acc = gate[:, 0:1] * _load_token_tiles(buf.at[cur, 0], tm)
    for k in range(1, TOP_K):
        acc = acc + gate[:, k:k + 1] * _load_token_tiles(buf.at[cur, k], tm)
    o_ref[...] = x1_ref[...] + gt_ref[...] * (_rms(acc) * gpost_ref[...])


def _combine(dest, y, gate4, x1, gt, gpost):
    s, d = x1.shape
    tm = COMBINE_TILE // 2
    n = s // tm
    return pl.pallas_call(
        _combine_kernel,
        out_shape=jax.ShapeDtypeStruct((s, d), F32),
        grid=(n,),
        in_specs=[pl.BlockSpec((TOP_K, tm), lambda i: (0, i), memory_space=pltpu.SMEM),
                  pl.BlockSpec((TOP_K, tm), lambda i: (0, jnp.minimum(i + 1, n - 1)), memory_space=pltpu.SMEM),
                  pl.BlockSpec(memory_space=pl.ANY),
                  pl.BlockSpec((tm, TOP_K), lambda i: (i, 0)),
                  pl.BlockSpec((tm, d), lambda i: (i, 0)),
                  pl.BlockSpec((1, d), lambda i: (0, 0)),
                  pl.BlockSpec((1, d), lambda i: (0, 0))],
        out_specs=pl.BlockSpec((tm, d), lambda i: (i, 0)),
        scratch_shapes=[pltpu.VMEM((2, TOP_K, tm * SUBLANES, LANES), F32), pltpu.SemaphoreType.DMA((2,))],
        compiler_params=_cparams(("arbitrary",)),
        name="combine",
    )(dest, dest, y, gate4, x1, gt, gpost)


def _layer(x2, mod, g_pre_mix, g_post_mix, w_in, b_forget, rel_bias, w_out,
           g_pre_ffn, g_post_ffn, w_router, b_router, w_gate_up, b_gate_up, w_down, b_down):
    s, d = x2.shape
    sh_m, sc_m, gt_m, sh_f, sc_f, gt_f = [mod[:, k * d:(k + 1) * d] for k in range(6)]
    n_qkv = 3 * (N_HEADS_FOX + N_HEADS_MOBA) * HEAD_DIM
    fox_w = N_HEADS_FOX * HEAD_DIM

    w_qkv = w_in[:, :n_qkv].astype(BF16)
    w_f = w_in[:, n_qkv:].T
    qkv, cum, sel, nrm, tr, nr = _inproj(x2, g_pre_mix.reshape(1, d), sc_m, sh_m, w_qkv, w_f,
                                         b_forget.reshape(-1, 1))
    y_a = _fox(qkv, tr, nr, cum, nrm)
    y_b = _moba(qkv, tr, sel, rel_bias)

    w_out_bf = w_out.astype(BF16)
    x1, h2, route, gate4, cnt = _outproj(
        y_a, y_b, x2, w_out_bf[:fox_w], w_out_bf[fox_w:], g_post_mix.reshape(1, d), gt_m,
        g_pre_ffn.reshape(1, d), sc_f, sh_f, w_router, b_router.reshape(1, -1))

    bm = EXPERT_BLOCK
    counts = cnt[:, 0].astype(I32)
    pblk = (counts + bm - 1) // bm
    pend_blk = jnp.cumsum(pblk)
    pstart = ((pend_blk - pblk) * bm).astype(I32)
    n_rows = s * TOP_K + N_EXPERTS * bm
    n_blk = n_rows // bm
    block_e = jnp.minimum(jnp.sum(pend_blk[None, :] <= jnp.arange(n_blk)[:, None], axis=1),
                          N_EXPERTS - 1).astype(I32)
    n_used = pend_blk[-1:].astype(I32)

    dest = _dest(route, pstart)
    xs = _dispatch(pstart, pblk.astype(I32), n_used, dest, h2, n_rows)
    y = _experts(block_e, n_used, xs, w_gate_up, b_gate_up, w_down, b_down)
    return _combine(dest, y, gate4, x1, gt_f, g_post_ffn.reshape(1, d))


def kernel(x, c, w_ada, b_ada, g_pre_mix, g_post_mix, w_in, b_forget, rel_bias, w_out, g_pre_ffn, g_post_ffn, w_router, b_router, w_gate_up, b_gate_up, w_down, b_down):
    bsz, s, d = x.shape
    depth = w_ada.shape[0]
    outs = []
    for bi in range(bsz):
        x2 = x[bi]
        for l in range(depth):
            mod = _adaln(c[bi:bi + 1], w_ada[l], b_ada[l])
            x2 = _layer(x2, mod, g_pre_mix[l], g_post_mix[l], w_in[l], b_forget[l], rel_bias, w_out[l],
                        g_pre_ffn[l], g_post_ffn[l], w_router[l], b_router[l], w_gate_up[l], b_gate_up[l],
                        w_down[l], b_down[l])
        outs.append(x2)
    return outs[0].reshape(1, s, d) if bsz == 1 else jnp.stack(outs)
```

```python
import math

import numpy as np
import jax
import jax.numpy as jnp
from jax import lax
from jax.experimental import pallas as pl
from jax.experimental.pallas import tpu as pltpu

F32 = jnp.float32
BF16 = jnp.bfloat16
I32 = jnp.int32

HEAD_DIM = 64
N_HEADS_FOX = 8
N_HEADS_MOBA = 8
PAIR = 2 * HEAD_DIM
MOBA_BLOCK = 256
MOBA_TOPK = 3
NUM_BUCKETS = 32
MAX_DISTANCE = 128
N_EXPERTS = 32
TOP_K = 4
SWIGLU_LIMIT = 7.0
SWIGLU_ALPHA = 1.702
RMS_EPS = 1e-6
NEG = -(2.0 ** 100)
M_INIT = -(2.0 ** 99)
LOG2E = math.log2(math.e)
SUM_ROWS = 16
SUBLANES = 8
LANES = 128
EXP_UNDERFLOW = 90.0
VMEM_LIMIT = 56 * 1024 * 1024

ROW_TILE = 512
FOX_TILE = 256
EXPERT_BLOCK = 512
COMBINE_TILE = 1024
COMBINE_CHUNK = 64
MOBA_LAG = 1
MOBA_UNROLL = 4

NT_DIMS = (((1,), (1,)), ((), ()))


def _cparams(sem):
    return pltpu.CompilerParams(dimension_semantics=sem, vmem_limit_bytes=VMEM_LIMIT)


def _rms(x):
    return x * lax.rsqrt(jnp.mean(x * x, axis=-1, keepdims=True) + RMS_EPS)


def _adaln_kernel(c_ref, w_ref, b_ref, o_ref):
    c = c_ref[...]
    cond = c * jax.nn.sigmoid(c)
    o_ref[...] = jnp.dot(cond, w_ref[...], preferred_element_type=F32,
                         precision=lax.Precision.HIGHEST) + b_ref[...]


def _adaln(c, w_ada, b_ada):
    d = c.shape[-1]
    n = w_ada.shape[-1]
    c8 = jnp.broadcast_to(c.reshape(1, d), (8, d))
    out = pl.pallas_call(
        _adaln_kernel,
        out_shape=jax.ShapeDtypeStruct((8, n), F32),
        grid=(n // d,),
        in_specs=[pl.BlockSpec((8, d), lambda j: (0, 0)),
                  pl.BlockSpec((d, d), lambda j: (0, j)),
                  pl.BlockSpec((1, d), lambda j: (0, j))],
        out_specs=pl.BlockSpec((8, d), lambda j: (0, j)),
        compiler_params=_cparams(("arbitrary",)),
        name="adaln",
    )(c8, w_ada, b_ada.reshape(1, n))
    return out[0:1]


def _inproj_kernel(x_ref, g_ref, sc_ref, sh_ref, w_ref, wf_ref, bf_ref,
                   qkv_ref, cum_ref, sel_ref, nrm_ref, tr_ref, nr_ref, km_scr, carry_scr):
    i = pl.program_id(0)
    tm = x_ref.shape[0]
    nblk = km_scr.shape[0]

    @pl.when(i == 0)
    def _():
        km_scr[...] = jnp.zeros_like(km_scr)
        carry_scr[...] = jnp.zeros_like(carry_scr)

    x = x_ref[...]
    h = _rms(x) * g_ref[...] * (1.0 + sc_ref[...]) + sh_ref[...]
    hb = h.astype(BF16)

    width = N_HEADS_FOX * HEAD_DIM
    hsel = jnp.where(lax.broadcasted_iota(I32, (width, N_HEADS_FOX), 0) // HEAD_DIM
                     == lax.broadcasted_iota(I32, (width, N_HEADS_FOX), 1), 1.0, 0.0)
    kb = None
    tr_slot = {3: 0, 5: 1, 0: 2, 2: 3}
    qbt = None
    for c in range(6):
        pc = jnp.dot(hb, w_ref[:, c * width:(c + 1) * width], preferred_element_type=F32)
        if c in tr_slot:
            n = tr_slot[c]
            pct = pc.T
            if c == 3:
                qbt = pct
            if c == 0 or c == 3:
                pct = pct * (LOG2E * HEAD_DIM ** -0.5)
            tr_ref[n * width:(n + 1) * width, :] = pct.astype(BF16)
        if c == 1 or c == 4:
            qkv_ref[:, (c // 3) * width:(c // 3 + 1) * width] = pc.astype(BF16)
        if c < 2:
            sq = (pc * (HEAD_DIM ** -0.5) if c == 0 else pc).astype(BF16).astype(F32)
            n2 = jnp.dot((sq * sq).astype(BF16), hsel.astype(BF16), preferred_element_type=F32)
            nrm_ref[0, c:c + 1, :] = jnp.max(n2, axis=0, keepdims=True)
        if c == 4:
            kb = pc

    ft = lax.dot_general(wf_ref[...].astype(BF16), hb, NT_DIMS, preferred_element_type=F32)
    z = ft + bf_ref[...]
    logf = -(jnp.maximum(-z, 0.0) + jnp.log1p(jnp.exp(-jnp.abs(z))))
    lane = lax.broadcasted_iota(I32, logf.shape, 1)
    cs = logf
    sh = 1
    while sh < tm:
        cs = cs + jnp.where(lane >= sh, pltpu.roll(cs, sh, axis=1), 0.0)
        sh *= 2
    base = jnp.zeros_like(cs)
    for b in range(1, tm // FOX_TILE):
        base = jnp.where(lane >= b * FOX_TILE, cs[:, b * FOX_TILE - 1:b * FOX_TILE], base)
    nr_t = (-LOG2E * (cs - base)).T
    for pp in range(nr_ref.shape[0]):
        nr_ref[pp] = nr_t[:, 2 * pp:2 * pp + 2]
    cs = cs + carry_scr[...]
    cum_ref[...] = cs
    carry_scr[...] = cs[:, tm - 1:tm]

    nb_tile = tm // MOBA_BLOCK
    for b in range(nb_tile):
        kmean = jnp.sum(kb[b * MOBA_BLOCK:(b + 1) * MOBA_BLOCK], axis=0, keepdims=True) * (1.0 / MOBA_BLOCK)
        km_scr[pl.ds(i * nb_tile + b, 1), :] = kmean

    km = km_scr[...]
    blk = lax.broadcasted_iota(I32, (nblk, tm), 0)
    col = lax.broadcasted_iota(I32, (nblk, tm), 1)
    own = i * nb_tile + col // MOBA_BLOCK
    for hd in range(N_HEADS_MOBA):
        hs = slice(hd * HEAD_DIM, (hd + 1) * HEAD_DIM)
        g = jnp.dot(km[:, hs], qbt[hs, :], preferred_element_type=F32,
                    precision=lax.Precision.HIGHEST)
        g = jnp.where(blk < own, g, -jnp.inf)
        sel = jnp.zeros(g.shape, dtype=jnp.bool_)
        for _ in range(MOBA_TOPK):
            m = jnp.max(g, axis=0, keepdims=True)
            first = jnp.min(jnp.where(g == m, blk, nblk), axis=0, keepdims=True)
            pick = (blk == first) & (m > -jnp.inf)
            sel = sel | pick
            g = jnp.where(pick, -jnp.inf, g)
        sel_ref[hd] = jnp.where(sel, 0.0, NEG)


def _inproj(x2, g, sc, sh, w_qkv, w_f, b_f):
    s, d = x2.shape
    tm = ROW_TILE
    nblk = s // MOBA_BLOCK
    n = w_qkv.shape[1]
    width = N_HEADS_FOX * HEAD_DIM
    n_rows_out = 2 * width
    n_tr = 4 * width
    assert tm % FOX_TILE == 0 and N_HEADS_FOX == N_HEADS_MOBA
    return pl.pallas_call(
        _inproj_kernel,
        out_shape=(jax.ShapeDtypeStruct((s, n_rows_out), BF16),
                   jax.ShapeDtypeStruct((N_HEADS_FOX, s), F32),
                   jax.ShapeDtypeStruct((N_HEADS_MOBA, nblk, s), F32),
                   jax.ShapeDtypeStruct((s // tm, 2, N_HEADS_FOX), F32),
                   jax.ShapeDtypeStruct((n_tr, s), BF16),
                   jax.ShapeDtypeStruct((N_HEADS_FOX // 2, s, 2), F32)),
        grid=(s // tm,),
        in_specs=[pl.BlockSpec((tm, d), lambda i: (i, 0)),
                  pl.BlockSpec((1, d), lambda i: (0, 0)),
                  pl.BlockSpec((1, d), lambda i: (0, 0)),
                  pl.BlockSpec((1, d), lambda i: (0, 0)),
                  pl.BlockSpec((d, n), lambda i: (0, 0)),
                  pl.BlockSpec((N_HEADS_FOX, d), lambda i: (0, 0)),
                  pl.BlockSpec((N_HEADS_FOX, 1), lambda i: (0, 0))],
        out_specs=(pl.BlockSpec((tm, n_rows_out), lambda i: (i, 0)),
                   pl.BlockSpec((N_HEADS_FOX, tm), lambda i: (0, i)),
                   pl.BlockSpec((N_HEADS_MOBA, nblk, tm), lambda i: (0, 0, i)),
                   pl.BlockSpec((1, 2, N_HEADS_FOX), lambda i: (i, 0, 0)),
                   pl.BlockSpec((n_tr, tm), lambda i: (0, i)),
                   pl.BlockSpec((N_HEADS_FOX // 2, tm, 2), lambda i: (0, i, 0))),
        scratch_shapes=[pltpu.VMEM((nblk, N_HEADS_MOBA * HEAD_DIM), F32),
                        pltpu.VMEM((N_HEADS_FOX, 1), F32)],
        compiler_params=_cparams(("arbitrary",)),
        name="inproj",
    )(x2, g, sc, sh, w_qkv, w_f, b_f)


def _fox_kernel(jlo_ref, cp_ref, qt_ref, k_ref, vt_ref, nr_ref, o_ref, m_ref, a_ref):
    p = pl.program_id(0)
    g = pl.program_id(1)
    j_first = jlo_ref[p * pl.num_programs(1) + g]
    tq = qt_ref.shape[1]
    t = tq // 2

    m_ref[...] = jnp.full(m_ref.shape, M_INIT, F32)
    a_ref[...] = jnp.zeros(a_ref.shape, F32)
    ones_rows = jnp.where(lax.broadcasted_iota(I32, (SUM_ROWS, t), 0) == 0, 1.0, 0.0).astype(BF16)
    in_a = lax.broadcasted_iota(I32, (1, tq), 1) < t
    krow = lax.broadcasted_iota(I32, (t, tq), 0)
    qcol = lax.broadcasted_iota(I32, (t, tq), 1)
    future_a = (qcol < t) & (krow > qcol)
    future_b = (qcol >= t) & (krow > qcol - t)

    def scores(j, future):
        k0 = pl.multiple_of(j * t, t)
        out = []
        for hd in range(2):
            kh = k_ref[pl.ds(k0, t), hd * HEAD_DIM:(hd + 1) * HEAD_DIM]
            qh = qt_ref[hd * HEAD_DIM:(hd + 1) * HEAD_DIM, :]
            s = jnp.dot(kh, qh, preferred_element_type=F32) + nr_ref[0, pl.ds(k0, t), hd:hd + 1]
            out.append(s if future is None else jnp.where(future, NEG, s))
        return out

    def shifts(j, a_on, b_on):
        rows = []
        for hd in range(2):
            h = 2 * p + hd
            sa = LOG2E * (cp_ref[h, 2 * g] - cp_ref[h, j])
            sb = LOG2E * (cp_ref[h, 2 * g + 1] - cp_ref[h, j])
            rows.append(jnp.where(in_a, jnp.where(a_on, sa, NEG), jnp.where(b_on, sb, NEG)))
        return rows

    def softmax(ss, shift_rows):
        ps, alphas = [], []
        for hd in range(2):
            m_old = m_ref[hd]
            m_new = jnp.maximum(m_old, jnp.max(ss[hd], axis=0, keepdims=True) + shift_rows[hd])
            alphas.append(jnp.exp2(m_old - m_new))
            m_ref[hd] = m_new
            ps.append(jnp.exp2(ss[hd] - (m_new - shift_rows[hd])).astype(BF16))
        return ps, alphas

    def accumulate(j, ps, alphas):
        k0 = pl.multiple_of(j * t, t)
        for hd in range(2):
            vh = vt_ref[hd * HEAD_DIM:(hd + 1) * HEAD_DIM, pl.ds(k0, t)]
            vcat = jnp.concatenate([vh, ones_rows], axis=0)
            a_ref[hd] = a_ref[hd] * alphas[hd] + jnp.dot(vcat, ps[hd], preferred_element_type=F32)

    def body(j, carry):
        accumulate(j, *softmax(scores(j, None), shifts(j, True, True)))
        return carry

    lax.fori_loop(j_first, 2 * g - 1, body, 0)
    j1 = jnp.maximum(2 * g - 1, 0)
    j2 = 2 * g
    j3 = 2 * g + 1
    ss1 = scores(j1, None)
    ss2 = scores(j2, future_a)
    ss3 = scores(j3, future_b)
    w1 = softmax(ss1, shifts(j1, g >= 1, g >= 1))
    w2 = softmax(ss2, shifts(j2, True, True))
    accumulate(j1, *w1)
    w3 = softmax(ss3, shifts(j3, False, True))
    accumulate(j2, *w2)
    accumulate(j3, *w3)
    out_t = jnp.concatenate([a_ref[hd, 0:HEAD_DIM, :] / a_ref[hd, HEAD_DIM:HEAD_DIM + 1, :]
                             for hd in range(2)], axis=0)
    o_ref[...] = out_t.T.astype(o_ref.dtype)


def _fox_first_tile(cum, nrm, t):
    cend = cum[:, t - 1::t]
    nt = cend.shape[1]
    cprev = jnp.concatenate([jnp.zeros((cend.shape[0], 1), F32), cend[:, :-1]], axis=1)
    rep = nt // nrm.shape[0]
    qn = jnp.repeat(jnp.sqrt(nrm[:, 0, :]).T, rep, axis=1)
    kn = jnp.repeat(jnp.sqrt(nrm[:, 1, :]).T, rep, axis=1)
    gap = (1.02 * qn[:, :, None] * (kn[:, None, :] + kn[:, :, None])
           + cprev[:, :, None] - cend[:, None, :])
    jj = jnp.arange(nt)[None, None, :]
    ii = jnp.arange(nt)[None, :, None]
    needed = (jj < ii) & jnp.logical_not(gap <= -EXP_UNDERFLOW)
    needed = needed[0::2] | needed[1::2]
    first = jnp.min(jnp.where(needed, jj, ii), axis=2)
    return first.reshape(-1).astype(I32)


def _fox(qkv, tr, nr, cum, nrm):
    s = qkv.shape[0]
    t = FOX_TILE
    npair = N_HEADS_FOX // 2
    tq = 2 * t
    jlo = _fox_first_tile(cum, nrm, t).reshape(npair, s // tq, 2).min(axis=2).reshape(-1)
    cend = cum[:, t - 1::t]
    cprev = jnp.concatenate([jnp.zeros((cend.shape[0], 1), F32), cend[:, :-1]], axis=1)
    q_rows = 2 * (N_HEADS_MOBA // 2)
    grid_spec = pltpu.PrefetchScalarGridSpec(
        num_scalar_prefetch=1,
        grid=(npair, s // tq),
        in_specs=[pl.BlockSpec(memory_space=pltpu.SMEM),
                  pl.BlockSpec((PAIR, tq), lambda p, i, jl: (q_rows + p, i)),
                  pl.BlockSpec((s, PAIR), lambda p, i, jl: (0, p)),
                  pl.BlockSpec((PAIR, s), lambda p, i, jl: (q_rows + npair + p, 0)),
                  pl.BlockSpec((1, s, 2), lambda p, i, jl: (p, 0, 0))],
        out_specs=pl.BlockSpec((tq, PAIR), lambda p, i, jl: (i, p)),
        scratch_shapes=[pltpu.VMEM((2, 1, tq), F32), pltpu.VMEM((2, HEAD_DIM + SUM_ROWS, tq), F32)],
    )
    return pl.pallas_call(
        _fox_kernel,
        out_shape=jax.ShapeDtypeStruct((s, npair * PAIR), BF16),
        grid_spec=grid_spec,
        compiler_params=_cparams(("arbitrary", "arbitrary")),
        name="fox",
    )(jlo, cprev, tr, qkv, tr, nr)


def _t5_bucket_np(dist):
    dist = np.maximum(dist, 0)
    max_exact = NUM_BUCKETS // 2
    d = np.maximum(dist, 1).astype(np.float32)
    large = max_exact + (np.log(d / np.float32(max_exact)) / np.float32(math.log(MAX_DISTANCE / max_exact))
                         * np.float32(NUM_BUCKETS - max_exact)).astype(np.int32)
    large = np.minimum(large, NUM_BUCKETS - 1)
    return np.where(dist < max_exact, dist, large).astype(np.int32)


def _moba_kernel(rb_ref, qt_ref, k_ref, vt_ref, sel_ref, bkt_ref, o_ref,
                 m_ref, a_ref, bias_scr, *bufs):
    s_bufs = bufs[0:MOBA_UNROLL]
    p_bufs = bufs[MOBA_UNROLL:2 * MOBA_UNROLL]
    al_bufs = bufs[2 * MOBA_UNROLL:3 * MOBA_UNROLL]
    p = pl.program_id(0)
    g = pl.program_id(1)
    tq = qt_ref.shape[1]
    t = tq // 2

    @pl.when(g == 0)
    def _():
        r = lax.broadcasted_iota(I32, (t, t), 0)
        c = lax.broadcasted_iota(I32, (t, t), 1)
        zero = jnp.zeros((t, t), F32)
        for hd in range(2):
            h = 2 * p + hd
            far = rb_ref[(NUM_BUCKETS - 1) * N_HEADS_MOBA + h]
            tiles = []
            for w in range(2):
                bkt = bkt_ref[w]
                acc = jnp.zeros(bkt.shape, F32)
                for kk in range(NUM_BUCKETS):
                    acc = acc + jnp.where(bkt == kk, rb_ref[kk * N_HEADS_MOBA + h], 0.0)
                tiles.append((acc - far) * LOG2E)
            prev_t = tiles[0]
            own_t = jnp.where(r <= c, tiles[1], NEG)
            bias_scr[hd, 0] = jnp.concatenate([prev_t, zero], axis=1)
            bias_scr[hd, 1] = jnp.concatenate([own_t, prev_t], axis=1)
            bias_scr[hd, 2] = jnp.concatenate([zero, own_t], axis=1)

    qt = qt_ref[...]
    top = lax.broadcasted_iota(I32, qt.shape, 0) < HEAD_DIM
    zq = jnp.zeros_like(qt)
    qth = (jnp.where(top, qt, zq), jnp.where(top, zq, qt))
    m_ref[...] = jnp.full(m_ref.shape, M_INIT, F32)
    a_ref[...] = jnp.zeros(a_ref.shape, F32)

    ones_rows = jnp.where(lax.broadcasted_iota(I32, (SUM_ROWS, t), 0) == 0, 1.0, 0.0).astype(BF16)
    in_a = lax.broadcasted_iota(I32, (1, tq), 1) < t

    n_far = jnp.maximum(2 * g - 1, 0)

    n_key_tiles = k_ref.shape[0] // t

    def produce(j, s_buf):
        j = jnp.minimum(j, n_key_tiles - 1)
        k0 = pl.multiple_of(j * t, t)
        kt = k_ref[pl.ds(k0, t), :]
        for hd in range(2):
            s_buf[hd] = jnp.dot(kt, qth[hd], preferred_element_type=F32)

    def softmax(s_buf, p_buf, al_buf, selrows, w):
        for hd in range(2):
            s = s_buf[hd]
            if w is not None:
                s = s + bias_scr[hd, w]
            smax = jnp.max(s, axis=0, keepdims=True)
            m_old = m_ref[hd]
            m_new = jnp.maximum(m_old, smax + selrows[hd])
            shift = m_new - selrows[hd]
            al_buf[hd] = jnp.exp2(m_old - m_new)
            m_ref[hd] = m_new
            p_buf[hd * t:(hd + 1) * t, :] = jnp.exp2(s - shift).astype(BF16)

    def far_rows(j):
        return [jnp.where(j < n_far, sel_ref[hd, pl.ds(j, 1), :], NEG) for hd in range(2)]

    def accumulate(j, p_buf, al_buf):
        k0 = pl.multiple_of(j * t, t)
        for hd in range(2):
            vh = vt_ref[hd * HEAD_DIM:(hd + 1) * HEAD_DIM, pl.ds(k0, t)]
            vcat = jnp.concatenate([vh, ones_rows], axis=0)
            a_ref[hd] = (a_ref[hd] * al_buf[hd]
                         + jnp.dot(vcat, p_buf[hd * t:(hd + 1) * t, :], preferred_element_type=F32))

    un = MOBA_UNROLL
    lag = MOBA_LAG
    for n in range(un - lag, un):
        p_bufs[n][...] = jnp.zeros_like(p_bufs[n])
        al_bufs[n][...] = jnp.ones_like(al_bufs[n])
    produce(0, s_bufs[0])

    def body(u, carry):
        j0 = un * u
        for n in range(un):
            accumulate(jnp.maximum(j0 + n - lag, 0), p_bufs[(n - lag) % un], al_bufs[(n - lag) % un])
            softmax(s_bufs[n], p_bufs[n], al_bufs[n], far_rows(j0 + n), None)
            produce(j0 + n + 1, s_bufs[(n + 1) % un])
        return carry

    n_trips = (n_far + un - 1) // un
    lax.fori_loop(0, n_trips, body, 0)

    j1 = jnp.maximum(2 * g - 1, 0)
    j2 = 2 * g
    j3 = 2 * g + 1
    rows1 = [jnp.where(g >= 1, sel_ref[hd, pl.ds(j1, 1), :], NEG) for hd in range(2)]
    rows2 = [jnp.where(in_a, 0.0, sel_ref[hd, pl.ds(j2, 1), :]) for hd in range(2)]
    rows3 = [jnp.where(in_a, NEG, 0.0)] * 2
    produce(j1, s_bufs[0])
    produce(j2, s_bufs[1])
    for n in range(un - lag, un):
        accumulate(jnp.maximum(un * n_trips - un + n, 0), p_bufs[n], al_bufs[n])
    softmax(s_bufs[0], p_bufs[0], al_bufs[0], rows1, 0)
    produce(j3, s_bufs[0])
    softmax(s_bufs[1], p_bufs[1], al_bufs[1], rows2, 1)
    accumulate(j1, p_bufs[0], al_bufs[0])
    softmax(s_bufs[0], p_bufs[2], al_bufs[2], rows3, 2)
    accumulate(j2, p_bufs[1], al_bufs[1])
    accumulate(j3, p_bufs[2], al_bufs[2])
    out_t = jnp.concatenate([a_ref[hd, 0:HEAD_DIM, :] / a_ref[hd, HEAD_DIM:HEAD_DIM + 1, :]
                             for hd in range(2)], axis=0)
    o_ref[...] = out_t.T.astype(o_ref.dtype)


def _moba(qkv, tr, sel, rel_bias):
    s = qkv.shape[0]
    t = MOBA_BLOCK
    npair = N_HEADS_MOBA // 2
    nblk = s // t
    kcol = N_HEADS_FOX // 2
    a = np.arange(t)[None, :]
    b = np.arange(t)[:, None]
    bkt = jnp.asarray(np.stack([_t5_bucket_np(t + a - b), _t5_bucket_np(a - b)]))
    tq = 2 * t
    grid_spec = pltpu.PrefetchScalarGridSpec(
        num_scalar_prefetch=1,
        grid=(npair, s // tq),
        in_specs=[pl.BlockSpec((PAIR, tq), lambda p, i, rb: (p, i)),
                  pl.BlockSpec((s, PAIR), lambda p, i, rb: (0, kcol + p)),
                  pl.BlockSpec((PAIR, s), lambda p, i, rb: (npair + p, 0)),
                  pl.BlockSpec((2, nblk, tq), lambda p, i, rb: (p, 0, i)),
                  pl.BlockSpec((2, t, t), lambda p, i, rb: (0, 0, 0))],
        out_specs=pl.BlockSpec((tq, PAIR), lambda p, i, rb: (i, p)),
        scratch_shapes=[pltpu.VMEM((2, 1, tq), F32),
                        pltpu.VMEM((2, HEAD_DIM + SUM_ROWS, tq), F32), pltpu.VMEM((2, 3, t, tq), F32)]
        + [pltpu.VMEM((2, t, tq), F32)] * MOBA_UNROLL
        + [pltpu.VMEM((2 * t, tq), BF16)] * MOBA_UNROLL
        + [pltpu.VMEM((2, 1, tq), F32)] * MOBA_UNROLL,
    )
    return pl.pallas_call(
        _moba_kernel,
        out_shape=jax.ShapeDtypeStruct((s, npair * PAIR), BF16),
        grid_spec=grid_spec,
        compiler_params=_cparams(("arbitrary", "arbitrary")),
        name="moba",
    )(rel_bias.reshape(-1), tr, qkv, tr, sel, bkt)


def _store_token_tiles(ref, val):
    n = val.shape[0]
    for c in range(SUBLANES):
        ref[pl.ds(c, n, stride=SUBLANES), :] = val[:, c * LANES:(c + 1) * LANES]


def _load_token_tiles(ref, n):
    return jnp.concatenate([ref[pl.ds(c, n, stride=SUBLANES), :] for c in range(SUBLANES)], axis=1)


def _tile_rows(r0, n):
    start = r0 * SUBLANES
    if not isinstance(start, int):
        start = pl.multiple_of(start, SUBLANES)
    return pl.ds(start, n * SUBLANES)


def _outproj_kernel(ya_ref, yb_ref, x_ref, wa_ref, wb_ref, gpost_ref, gt_ref, gpre_ref,
                    sc_ref, sh_ref, wr_ref, br_ref,
                    x1_ref, h2_ref, route_ref, gate_ref, cnt_ref, carry_scr):
    i = pl.program_id(0)
    tm = x_ref.shape[0]

    @pl.when(i == 0)
    def _():
        carry_scr[...] = jnp.zeros_like(carry_scr)

    y = (jnp.dot(ya_ref[...], wa_ref[...], preferred_element_type=F32)
         + jnp.dot(yb_ref[...], wb_ref[...], preferred_element_type=F32))
    x1 = x_ref[...] + gt_ref[...] * (_rms(y) * gpost_ref[...])
    x1_ref[...] = x1
    h2 = _rms(x1) * gpre_ref[...] * (1.0 + sc_ref[...]) + sh_ref[...]
    _store_token_tiles(h2_ref, h2)

    h_hi = h2.astype(BF16)
    h_lo = (h2 - h_hi.astype(F32)).astype(BF16)
    logits = (jnp.dot(h_hi, wr_ref[0], preferred_element_type=F32)
              + jnp.dot(h_hi, wr_ref[1], preferred_element_type=F32)
              + jnp.dot(h_lo, wr_ref[0], preferred_element_type=F32)) + br_ref[...]
    ne = cnt_ref.shape[0]
    g = logits.T[0:ne, :]
    expert = lax.broadcasted_iota(I32, g.shape, 0)
    mask = jnp.zeros(g.shape, F32)
    vals, firsts, picks = [], [], []
    for _ in range(TOP_K):
        m = jnp.max(g, axis=0, keepdims=True)
        first = jnp.min(jnp.where(g == m, expert, ne), axis=0, keepdims=True)
        pick = expert == first
        mask = jnp.where(pick, 1.0, mask)
        g = jnp.where(pick, -jnp.inf, g)
        vals.append(m)
        firsts.append(first)
        picks.append(pick)
    ex = [jnp.exp(v - vals[0]) for v in vals]
    den = ex[0] + ex[1] + ex[2] + ex[3]
    gates = [e / den for e in ex]

    r = lax.broadcasted_iota(I32, (tm, tm), 0)
    c = lax.broadcasted_iota(I32, (tm, tm), 1)
    earlier = jnp.where(r < c, 1.0, 0.0).astype(BF16)
    before = jnp.dot(mask.astype(BF16), earlier, preferred_element_type=F32) + carry_scr[...]
    total = carry_scr[...] + jnp.sum(mask, axis=1, keepdims=True)
    carry_scr[...] = total
    cnt_ref[...] = jnp.broadcast_to(total, cnt_ref.shape)

    ranks = [jnp.sum(jnp.where(pk, before, 0.0), axis=0, keepdims=True).astype(I32) for pk in picks]
    route_ref[...] = jnp.concatenate(firsts + ranks, axis=0)
    gate_rows = jnp.concatenate(gates + [jnp.zeros((LANES - TOP_K, tm), F32)], axis=0)
    gate_ref[...] = gate_rows.T[:, 0:TOP_K]


def _outproj(mix_a, mix_b, x2, w_a, w_b, gpost, gt, gpre, sc, sh, w_router, b_router):
    s, d = x2.shape
    tm = ROW_TILE
    ne = w_router.shape[1]
    wa = mix_a.shape[1]
    w_router = jnp.pad(w_router, ((0, 0), (0, LANES - ne)))
    b_router = jnp.pad(b_router, ((0, 0), (0, LANES - ne)))
    w_hi = w_router.astype(BF16)
    w_router_hl = jnp.stack([w_hi, (w_router - w_hi.astype(F32)).astype(BF16)])
    row = lambda i: (i, 0)
    fix = lambda i: (0, 0)
    vec = pl.BlockSpec((1, d), fix)
    return pl.pallas_call(
        _outproj_kernel,
        out_shape=(jax.ShapeDtypeStruct((s, d), F32),
                   jax.ShapeDtypeStruct((s * SUBLANES, LANES), F32),
                   jax.ShapeDtypeStruct((2 * TOP_K, s), I32),
                   jax.ShapeDtypeStruct((s, TOP_K), F32),
                   jax.ShapeDtypeStruct((ne, LANES), F32)),
        grid=(s // tm,),
        in_specs=[pl.BlockSpec((tm, wa), row), pl.BlockSpec((tm, wa), row), pl.BlockSpec((tm, d), row),
                  pl.BlockSpec((wa, d), fix), pl.BlockSpec((wa, d), fix),
                  vec, vec, vec, vec, vec,
                  pl.BlockSpec((2, d, LANES), lambda i: (0, 0, 0)), pl.BlockSpec((1, LANES), fix)],
        out_specs=(pl.BlockSpec((tm, d), row), pl.BlockSpec((tm * SUBLANES, LANES), row),
                   pl.BlockSpec((2 * TOP_K, tm), lambda i: (0, i)),
                   pl.BlockSpec((tm, TOP_K), row), pl.BlockSpec((ne, LANES), fix)),
        scratch_shapes=[pltpu.VMEM((ne, 1), F32)],
        compiler_params=_cparams(("arbitrary",)),
        name="outproj",
    )(mix_a, mix_b, x2, w_a, w_b, gpost, gt, gpre, sc, sh, w_router_hl, b_router)


def _dest_kernel(route_ref, pstart_ref, o_ref):
    route = route_ref[...]
    tm = route.shape[1]
    ne = pstart_ref.shape[0]
    expert = lax.broadcasted_iota(I32, (ne, tm), 0)
    for k in range(TOP_K):
        start = jnp.sum(jnp.where(expert == route[k:k + 1, :], pstart_ref[...], 0), axis=0, keepdims=True)
        o_ref[k:k + 1, :] = start + route[TOP_K + k:TOP_K + k + 1, :]


def _dest(route, pstart):
    s = route.shape[1]
    tm = min(4 * ROW_TILE, s)
    return pl.pallas_call(
        _dest_kernel,
        out_shape=jax.ShapeDtypeStruct((TOP_K, s), I32),
        grid=(s // tm,),
        in_specs=[pl.BlockSpec((2 * TOP_K, tm), lambda i: (0, i)),
                  pl.BlockSpec((N_EXPERTS, 1), lambda i: (0, 0))],
        out_specs=pl.BlockSpec((TOP_K, tm), lambda i: (0, i)),
        compiler_params=_cparams(("arbitrary",)),
        name="dest",
    )(route, pstart.reshape(-1, 1))


def _dispatch_kernel(pstart_ref, pblk_ref, nu_ref, dest_ref, h_ref, xs_ref,
                     zero_scr, sem, zsem):
    tm = h_ref.shape[0] // SUBLANES
    bm = zero_scr.shape[0] // SUBLANES
    n_blk = xs_ref.shape[0] // (bm * SUBLANES)

    rows = _tile_rows

    @pl.when(pl.program_id(0) == 0)
    def _():
        zero_scr[...] = jnp.zeros_like(zero_scr)

        def zero_copy(row0):
            return pltpu.make_async_copy(zero_scr, xs_ref.at[rows(row0, bm)], zsem)

        for phase in range(2):
            for e in range(N_EXPERTS):
                last = pstart_ref[e] + (pblk_ref[e] - 1) * bm
                tail = (n_blk - N_EXPERTS + e) * bm
                for cond, row0 in ((pblk_ref[e] > 0, last), (n_blk - N_EXPERTS + e >= nu_ref[0], tail)):
                    @pl.when(cond)
                    def _():
                        if phase == 0:
                            zero_copy(row0).start()
                        else:
                            zero_copy(row0).wait()

    def row_copy(r, k):
        dst = dest_ref[k, r]
        return pltpu.make_async_copy(h_ref.at[rows(r, 1)], xs_ref.at[rows(dst, 1)], sem)

    def start(r, carry):
        for k in range(TOP_K):
            row_copy(r, k).start(priority=k % 2)
        return carry

    lax.fori_loop(0, tm, start, 0, unroll=4)
    for k in range(TOP_K):
        pltpu.make_async_copy(h_ref, xs_ref.at[rows(0, tm)], sem).wait()


def _dispatch(pstart, pblk, n_used, dest, h2, n_rows):
    s = h2.shape[0] // SUBLANES
    tm = COMBINE_TILE
    grid_spec = pltpu.PrefetchScalarGridSpec(
        num_scalar_prefetch=3,
        grid=(s // tm,),
        in_specs=[pl.BlockSpec((TOP_K, tm), lambda i, *_: (0, i), memory_space=pltpu.SMEM),
                  pl.BlockSpec((tm * SUBLANES, LANES), lambda i, *_: (i, 0))],
        out_specs=pl.BlockSpec(memory_space=pl.ANY),
        scratch_shapes=[pltpu.VMEM((EXPERT_BLOCK * SUBLANES, LANES), F32), pltpu.SemaphoreType.DMA,
                        pltpu.SemaphoreType.DMA],
    )
    return pl.pallas_call(
        _dispatch_kernel,
        out_shape=jax.ShapeDtypeStruct((n_rows * SUBLANES, LANES), F32),
        grid_spec=grid_spec,
        compiler_params=_cparams(("arbitrary",)),
        name="dispatch",
    )(pstart, pblk, n_used, dest, h2)


def _experts_kernel(be_ref, nu_ref, xs_ref, wgu_ref, bgu_ref, wd_ref, bd_ref, y_ref,
                    wgu_bf, wd_bf):
    b = pl.program_id(0)
    d_exp = wd_ref.shape[1]
    prev = be_ref[jnp.maximum(b - 1, 0)]
    changed = (b == 0) | (be_ref[b] != prev)

    @pl.when((b < nu_ref[0]) & changed)
    def _():
        rows = LANES

        def cast_gu(c, carry):
            r0 = pl.multiple_of(c * rows, rows)
            wgu_bf[pl.ds(r0, rows), :] = wgu_ref[0, pl.ds(r0, rows), :].astype(BF16)
            return carry

        def cast_d(c, carry):
            r0 = pl.multiple_of(c * rows, rows)
            wd_bf[pl.ds(r0, rows), :] = wd_ref[0, pl.ds(r0, rows), :].astype(BF16)
            return carry

        lax.fori_loop(0, wgu_ref.shape[1] // rows, cast_gu, 0)
        lax.fori_loop(0, wd_ref.shape[1] // rows, cast_d, 0)

    @pl.when(b < nu_ref[0])
    def _():
        bm = xs_ref.shape[0] // SUBLANES
        xb = _load_token_tiles(xs_ref, bm).astype(BF16)
        hdn = jnp.dot(xb, wgu_bf[...], preferred_element_type=F32) + bgu_ref[0]
        x_glu = jnp.minimum(hdn[:, :d_exp], SWIGLU_LIMIT)
        x_lin = jnp.clip(hdn[:, d_exp:], -SWIGLU_LIMIT, SWIGLU_LIMIT)
        act = x_glu * jax.nn.sigmoid(SWIGLU_ALPHA * x_glu) * (x_lin + 1.0)
        _store_token_tiles(y_ref, jnp.dot(act.astype(BF16), wd_bf[...], preferred_element_type=F32)
                           + bd_ref[0])

    @pl.when(b >= nu_ref[0])
    def _():
        y_ref[...] = jnp.zeros_like(y_ref)


def _experts(block_e, n_used, xs, w_gate_up, b_gate_up, w_down, b_down):
    n_rows = xs.shape[0] // SUBLANES
    bm = EXPERT_BLOCK
    n_blk = n_rows // bm
    ne, d, two_de = w_gate_up.shape
    de = w_down.shape[1]
    assert d == SUBLANES * LANES

    def blk(b, be, nu):
        return jnp.minimum(b, nu[0] - 1)

    grid_spec = pltpu.PrefetchScalarGridSpec(
        num_scalar_prefetch=2,
        grid=(n_blk,),
        in_specs=[pl.BlockSpec((bm * SUBLANES, LANES), lambda b, be, nu: (blk(b, be, nu), 0)),
                  pl.BlockSpec((1, d, two_de), lambda b, be, nu: (be[blk(b, be, nu)], 0, 0)),
                  pl.BlockSpec((1, 1, two_de), lambda b, be, nu: (be[blk(b, be, nu)], 0, 0)),
                  pl.BlockSpec((1, de, d), lambda b, be, nu: (be[blk(b, be, nu)], 0, 0)),
                  pl.BlockSpec((1, 1, d), lambda b, be, nu: (be[blk(b, be, nu)], 0, 0))],
        out_specs=pl.BlockSpec((bm * SUBLANES, LANES), lambda b, be, nu: (b, 0)),
        scratch_shapes=[pltpu.VMEM((d, two_de), BF16), pltpu.VMEM((de, d), BF16)],
    )
    return pl.pallas_call(
        _experts_kernel,
        out_shape=jax.ShapeDtypeStruct((n_rows * SUBLANES, LANES), F32),
        grid_spec=grid_spec,
        compiler_params=_cparams(("arbitrary",)),
        name="experts",
    )(block_e, n_used, xs, w_gate_up, b_gate_up.reshape(ne, 1, two_de), w_down, b_down.reshape(ne, 1, d))


def _combine_kernel(dest_ref, dest_next_ref, y_ref, gate_ref, x1_ref, gt_ref, gpost_ref, o_ref,
                    buf_even, buf_odd, sem):
    i = pl.program_id(0)
    tm = x1_ref.shape[0]
    ch = COMBINE_CHUNK
    bufs = (buf_even, buf_odd)

    def start_chunk(d_ref, slot, c):
        for rr in range(ch):
            r = c * ch + rr
            for k in range(TOP_K):
                pltpu.make_async_copy(y_ref.at[_tile_rows(d_ref[k, r], 1)],
                                      bufs[slot].at[k, _tile_rows(r, 1)], sem.at[slot, c]).start(priority=k % 2)

    @pl.when(i == 0)
    def _():
        def first(c, carry):
            start_chunk(dest_ref, 0, c)
            return carry

        lax.fori_loop(0, tm // ch, first, 0)

    def reduce_chunk(c, cur, gather_next):
        rows = pl.ds(pl.multiple_of(c * ch, ch), ch)
        buf = bufs[cur]
        for k in range(TOP_K):
            pltpu.make_async_copy(y_ref.at[_tile_rows(0, ch)], buf.at[k, _tile_rows(c * ch, ch)],
                                  sem.at[cur, c]).wait()
        if gather_next:
            start_chunk(dest_next_ref, 1 - cur, c)
        gate = gate_ref[rows, :]
        acc = gate[:, 0:1] * _load_token_tiles(buf.at[0, _tile_rows(c * ch, ch)], ch)
        for k in range(1, TOP_K):
            acc = acc + gate[:, k:k + 1] * _load_token_tiles(buf.at[k, _tile_rows(c * ch, ch)], ch)
        o_ref[rows, :] = x1_ref[rows, :] + gt_ref[...] * (_rms(acc) * gpost_ref[...])

    has_next = i + 1 < pl.num_programs(0)
    for cur in range(2):
        for gather_next in (True, False):
            @pl.when((i % 2 == cur) & (has_next == gather_next))
            def _(cur=cur, gather_next=gather_next):
                def body(c, carry):
                    reduce_chunk(c, cur, gather_next)
                    return carry

                lax.fori_loop(0, tm // ch, body, 0)


def _combine(dest, y, gate4, x1, gt, gpost):
    s, d = x1.shape
    tm = COMBINE_TILE // 2
    n = s // tm
    return pl.pallas_call(
        _combine_kernel,
        out_shape=jax.ShapeDtypeStruct((s, d), F32),
        grid=(n,),
        in_specs=[pl.BlockSpec((TOP_K, tm), lambda i: (0, i), memory_space=pltpu.SMEM),
                  pl.BlockSpec((TOP_K, tm), lambda i: (0, jnp.minimum(i + 1, n - 1)), memory_space=pltpu.SMEM),
                  pl.BlockSpec(memory_space=pl.ANY),
                  pl.BlockSpec((tm, TOP_K), lambda i: (i, 0)),
                  pl.BlockSpec((tm, d), lambda i: (i, 0)),
                  pl.BlockSpec((1, d), lambda i: (0, 0)),
                  pl.BlockSpec((1, d), lambda i: (0, 0))],
        out_specs=pl.BlockSpec((tm, d), lambda i: (i, 0)),
        scratch_shapes=[pltpu.VMEM((TOP_K, tm * SUBLANES, LANES), F32),
                        pltpu.VMEM((TOP_K, tm * SUBLANES, LANES), F32),
                        pltpu.SemaphoreType.DMA((2, tm // COMBINE_CHUNK))],
        compiler_params=_cparams(("arbitrary",)),
        name="combine",
    )(dest, dest, y, gate4, x1, gt, gpost)


def _layer(x2, mod, g_pre_mix, g_post_mix, w_in, b_forget, rel_bias, w_out,
           g_pre_ffn, g_post_ffn, w_router, b_router, w_gate_up, b_gate_up, w_down, b_down):
    s, d = x2.shape
    sh_m, sc_m, gt_m, sh_f, sc_f, gt_f = [mod[:, k * d:(k + 1) * d] for k in range(6)]
    n_qkv = 3 * (N_HEADS_FOX + N_HEADS_MOBA) * HEAD_DIM
    fox_w = N_HEADS_FOX * HEAD_DIM

    w_qkv = w_in[:, :n_qkv].astype(BF16)
    w_f = w_in[:, n_qkv:].T
    qkv, cum, sel, nrm, tr, nr = _inproj(x2, g_pre_mix.reshape(1, d), sc_m, sh_m, w_qkv, w_f,
                                         b_forget.reshape(-1, 1))
    y_a = _fox(qkv, tr, nr, cum, nrm)
    y_b = _moba(qkv, tr, sel, rel_bias)

    w_out_bf = w_out.astype(BF16)
    x1, h2, route, gate4, cnt = _outproj(
        y_a, y_b, x2, w_out_bf[:fox_w], w_out_bf[fox_w:], g_post_mix.reshape(1, d), gt_m,
        g_pre_ffn.reshape(1, d), sc_f, sh_f, w_router, b_router.reshape(1, -1))

    bm = EXPERT_BLOCK
    counts = cnt[:, 0].astype(I32)
    pblk = (counts + bm - 1) // bm
    pend_blk = jnp.cumsum(pblk)
    pstart = ((pend_blk - pblk) * bm).astype(I32)
    n_rows = s * TOP_K + N_EXPERTS * bm
    n_blk = n_rows // bm
    block_e = jnp.minimum(jnp.sum(pend_blk[None, :] <= jnp.arange(n_blk)[:, None], axis=1),
                          N_EXPERTS - 1).astype(I32)
    n_used = pend_blk[-1:].astype(I32)

    dest = _dest(route, pstart)
    xs = _dispatch(pstart, pblk.astype(I32), n_used, dest, h2, n_rows)
    y = _experts(block_e, n_used, xs, w_gate_up, b_gate_up, w_down, b_down)
    return _combine(dest, y, gate4, x1, gt_f, g_post_ffn.reshape(1, d))


def kernel(x, c, w_ada, b_ada, g_pre_mix, g_post_mix, w_in, b_forget, rel_bias, w_out, g_pre_ffn, g_post_ffn, w_router, b_router, w_gate_up, b_gate_up, w_down, b_down):
    bsz, s, d = x.shape
    depth = w_ada.shape[0]
    outs = []
    for bi in range(bsz):
        x2 = x[bi]
        for l in range(depth):
            mod = _adaln(c[bi:bi + 1], w_ada[l], b_ada[l])
            x2 = _layer(x2, mod, g_pre_mix[l], g_post_mix[l], w_in[l], b_forget[l], rel_bias, w_out[l],
                        g_pre_ffn[l], g_post_ffn[l], w_router[l], b_router[l], w_gate_up[l], b_gate_up[l],
                        w_down[l], b_down[l])
        outs.append(x2)
    return outs[0].reshape(1, s, d) if bsz == 1 else jnp.stack(outs)
```

```python
import math

import numpy as np
import jax
import jax.numpy as jnp
from jax import lax
from jax.experimental import pallas as pl
from jax.experimental.pallas import tpu as pltpu

F32 = jnp.float32
BF16 = jnp.bfloat16
I32 = jnp.int32

HEAD_DIM = 64
N_HEADS_FOX = 8
N_HEADS_MOBA = 8
PAIR = 2 * HEAD_DIM
MOBA_BLOCK = 256
MOBA_TOPK = 3
NUM_BUCKETS = 32
MAX_DISTANCE = 128
N_EXPERTS = 32
TOP_K = 4
SWIGLU_LIMIT = 7.0
SWIGLU_ALPHA = 1.702
RMS_EPS = 1e-6
NEG = -(2.0 ** 100)
M_INIT = -(2.0 ** 99)
LOG2E = math.log2(math.e)
SUM_ROWS = 16
SUBLANES = 8
LANES = 128
EXP_UNDERFLOW = 90.0
VMEM_LIMIT = 56 * 1024 * 1024

ROW_TILE = 512
FOX_TILE = 256
EXPERT_BLOCK = 512
COMBINE_TILE = 1024
COMBINE_CHUNK = LANES
COMBINE_REDUCE_ROWS = 64
MOBA_LAG = 1
MOBA_UNROLL = 4

NT_DIMS = (((1,), (1,)), ((), ()))


def _cparams(sem):
    return pltpu.CompilerParams(dimension_semantics=sem, vmem_limit_bytes=VMEM_LIMIT)


def _rms(x):
    return x * lax.rsqrt(jnp.mean(x * x, axis=-1, keepdims=True) + RMS_EPS)


def _adaln_kernel(c_ref, w_ref, b_ref, o_ref):
    c = c_ref[...]
    cond = c * jax.nn.sigmoid(c)
    o_ref[...] = jnp.dot(cond, w_ref[...], preferred_element_type=F32,
                         precision=lax.Precision.HIGHEST) + b_ref[...]


def _adaln(c, w_ada, b_ada):
    d = c.shape[-1]
    n = w_ada.shape[-1]
    c8 = jnp.broadcast_to(c.reshape(1, d), (8, d))
    out = pl.pallas_call(
        _adaln_kernel,
        out_shape=jax.ShapeDtypeStruct((8, n), F32),
        grid=(n // d,),
        in_specs=[pl.BlockSpec((8, d), lambda j: (0, 0)),
                  pl.BlockSpec((d, d), lambda j: (0, j)),
                  pl.BlockSpec((1, d), lambda j: (0, j))],
        out_specs=pl.BlockSpec((8, d), lambda j: (0, j)),
        compiler_params=_cparams(("arbitrary",)),
        name="adaln",
    )(c8, w_ada, b_ada.reshape(1, n))
    return out[0:1]


def _inproj_kernel(x_ref, g_ref, sc_ref, sh_ref, w_ref, wf_ref, bf_ref,
                   qkv_ref, cum_ref, sel_ref, nrm_ref, tr_ref, nr_ref, km_scr, carry_scr):
    i = pl.program_id(0)
    tm = x_ref.shape[0]
    nblk = km_scr.shape[0]

    @pl.when(i == 0)
    def _():
        km_scr[...] = jnp.zeros_like(km_scr)
        carry_scr[...] = jnp.zeros_like(carry_scr)

    x = x_ref[...]
    h = _rms(x) * g_ref[...] * (1.0 + sc_ref[...]) + sh_ref[...]
    hb = h.astype(BF16)

    width = N_HEADS_FOX * HEAD_DIM
    hsel = jnp.where(lax.broadcasted_iota(I32, (width, N_HEADS_FOX), 0) // HEAD_DIM
                     == lax.broadcasted_iota(I32, (width, N_HEADS_FOX), 1), 1.0, 0.0)
    kb = None
    tr_slot = {3: 0, 5: 1, 0: 2, 2: 3}
    qbt = None
    for c in range(6):
        pc = jnp.dot(hb, w_ref[:, c * width:(c + 1) * width], preferred_element_type=F32)
        if c in tr_slot:
            n = tr_slot[c]
            pct = pc.T
            if c == 3:
                qbt = pct
            if c == 0 or c == 3:
                pct = pct * (LOG2E * HEAD_DIM ** -0.5)
            tr_ref[n * width:(n + 1) * width, :] = pct.astype(BF16)
        if c == 1 or c == 4:
            qkv_ref[:, (c // 3) * width:(c // 3 + 1) * width] = pc.astype(BF16)
        if c < 2:
            sq = (pc * (HEAD_DIM ** -0.5) if c == 0 else pc).astype(BF16).astype(F32)
            n2 = jnp.dot((sq * sq).astype(BF16), hsel.astype(BF16), preferred_element_type=F32)
            nrm_ref[0, c:c + 1, :] = jnp.max(n2, axis=0, keepdims=True)
        if c == 4:
            kb = pc

    ft = lax.dot_general(wf_ref[...].astype(BF16), hb, NT_DIMS, preferred_element_type=F32)
    z = ft + bf_ref[...]
    logf = -(jnp.maximum(-z, 0.0) + jnp.log1p(jnp.exp(-jnp.abs(z))))
    lane = lax.broadcasted_iota(I32, logf.shape, 1)
    cs = logf
    sh = 1
    while sh < tm:
        cs = cs + jnp.where(lane >= sh, pltpu.roll(cs, sh, axis=1), 0.0)
        sh *= 2
    base = jnp.zeros_like(cs)
    for b in range(1, tm // FOX_TILE):
        base = jnp.where(lane >= b * FOX_TILE, cs[:, b * FOX_TILE - 1:b * FOX_TILE], base)
    nr_t = (-LOG2E * (cs - base)).T
    for pp in range(nr_ref.shape[0]):
        nr_ref[pp] = nr_t[:, 2 * pp:2 * pp + 2]
    cs = cs + carry_scr[...]
    cum_ref[...] = cs
    carry_scr[...] = cs[:, tm - 1:tm]

    nb_tile = tm // MOBA_BLOCK
    for b in range(nb_tile):
        kmean = jnp.sum(kb[b * MOBA_BLOCK:(b + 1) * MOBA_BLOCK], axis=0, keepdims=True) * (1.0 / MOBA_BLOCK)
        km_scr[pl.ds(i * nb_tile + b, 1), :] = kmean

    km = km_scr[...]
    blk = lax.broadcasted_iota(I32, (nblk, tm), 0)
    col = lax.broadcasted_iota(I32, (nblk, tm), 1)
    own = i * nb_tile + col // MOBA_BLOCK
    for hd in range(N_HEADS_MOBA):
        hs = slice(hd * HEAD_DIM, (hd + 1) * HEAD_DIM)
        g = jnp.dot(km[:, hs], qbt[hs, :], preferred_element_type=F32,
                    precision=lax.Precision.HIGHEST)
        g = jnp.where(blk < own, g, -jnp.inf)
        sel = jnp.zeros(g.shape, dtype=jnp.bool_)
        for _ in range(MOBA_TOPK):
            m = jnp.max(g, axis=0, keepdims=True)
            first = jnp.min(jnp.where(g == m, blk, nblk), axis=0, keepdims=True)
            pick = (blk == first) & (m > -jnp.inf)
            sel = sel | pick
            g = jnp.where(pick, -jnp.inf, g)
        sel_ref[hd] = jnp.where(sel, 0.0, NEG)


def _inproj(x2, g, sc, sh, w_qkv, w_f, b_f):
    s, d = x2.shape
    tm = ROW_TILE
    nblk = s // MOBA_BLOCK
    n = w_qkv.shape[1]
    width = N_HEADS_FOX * HEAD_DIM
    n_rows_out = 2 * width
    n_tr = 4 * width
    assert tm % FOX_TILE == 0 and N_HEADS_FOX == N_HEADS_MOBA
    return pl.pallas_call(
        _inproj_kernel,
        out_shape=(jax.ShapeDtypeStruct((s, n_rows_out), BF16),
                   jax.ShapeDtypeStruct((N_HEADS_FOX, s), F32),
                   jax.ShapeDtypeStruct((N_HEADS_MOBA, nblk, s), F32),
                   jax.ShapeDtypeStruct((s // tm, 2, N_HEADS_FOX), F32),
                   jax.ShapeDtypeStruct((n_tr, s), BF16),
                   jax.ShapeDtypeStruct((N_HEADS_FOX // 2, s, 2), F32)),
        grid=(s // tm,),
        in_specs=[pl.BlockSpec((tm, d), lambda i: (i, 0)),
                  pl.BlockSpec((1, d), lambda i: (0, 0)),
                  pl.BlockSpec((1, d), lambda i: (0, 0)),
                  pl.BlockSpec((1, d), lambda i: (0, 0)),
                  pl.BlockSpec((d, n), lambda i: (0, 0)),
                  pl.BlockSpec((N_HEADS_FOX, d), lambda i: (0, 0)),
                  pl.BlockSpec((N_HEADS_FOX, 1), lambda i: (0, 0))],
        out_specs=(pl.BlockSpec((tm, n_rows_out), lambda i: (i, 0)),
                   pl.BlockSpec((N_HEADS_FOX, tm), lambda i: (0, i)),
                   pl.BlockSpec((N_HEADS_MOBA, nblk, tm), lambda i: (0, 0, i)),
                   pl.BlockSpec((1, 2, N_HEADS_FOX), lambda i: (i, 0, 0)),
                   pl.BlockSpec((n_tr, tm), lambda i: (0, i)),
                   pl.BlockSpec((N_HEADS_FOX // 2, tm, 2), lambda i: (0, i, 0))),
        scratch_shapes=[pltpu.VMEM((nblk, N_HEADS_MOBA * HEAD_DIM), F32),
                        pltpu.VMEM((N_HEADS_FOX, 1), F32)],
        compiler_params=_cparams(("arbitrary",)),
        name="inproj",
    )(x2, g, sc, sh, w_qkv, w_f, b_f)


def _fox_kernel(jlo_ref, cp_ref, qt_ref, k_ref, vt_ref, nr_ref, o_ref, m_ref, a_ref):
    p = pl.program_id(0)
    g = pl.program_id(1)
    j_first = jlo_ref[p * pl.num_programs(1) + g]
    tq = qt_ref.shape[1]
    t = tq // 2

    m_ref[...] = jnp.full(m_ref.shape, M_INIT, F32)
    a_ref[...] = jnp.zeros(a_ref.shape, F32)
    ones_rows = jnp.where(lax.broadcasted_iota(I32, (SUM_ROWS, t), 0) == 0, 1.0, 0.0).astype(BF16)
    in_a = lax.broadcasted_iota(I32, (1, tq), 1) < t
    krow = lax.broadcasted_iota(I32, (t, tq), 0)
    qcol = lax.broadcasted_iota(I32, (t, tq), 1)
    future_a = (qcol < t) & (krow > qcol)
    future_b = (qcol >= t) & (krow > qcol - t)

    def scores(j, future):
        k0 = pl.multiple_of(j * t, t)
        out = []
        for hd in range(2):
            kh = k_ref[pl.ds(k0, t), hd * HEAD_DIM:(hd + 1) * HEAD_DIM]
            qh = qt_ref[hd * HEAD_DIM:(hd + 1) * HEAD_DIM, :]
            s = jnp.dot(kh, qh, preferred_element_type=F32) + nr_ref[0, pl.ds(k0, t), hd:hd + 1]
            out.append(s if future is None else jnp.where(future, NEG, s))
        return out

    def shifts(j, a_on, b_on):
        rows = []
        for hd in range(2):
            h = 2 * p + hd
            sa = LOG2E * (cp_ref[h, 2 * g] - cp_ref[h, j])
            sb = LOG2E * (cp_ref[h, 2 * g + 1] - cp_ref[h, j])
            rows.append(jnp.where(in_a, jnp.where(a_on, sa, NEG), jnp.where(b_on, sb, NEG)))
        return rows

    def softmax(ss, shift_rows):
        ps, alphas = [], []
        for hd in range(2):
            m_old = m_ref[hd]
            m_new = jnp.maximum(m_old, jnp.max(ss[hd], axis=0, keepdims=True) + shift_rows[hd])
            alphas.append(jnp.exp2(m_old - m_new))
            m_ref[hd] = m_new
            ps.append(jnp.exp2(ss[hd] - (m_new - shift_rows[hd])).astype(BF16))
        return ps, alphas

    def accumulate(j, ps, alphas):
        k0 = pl.multiple_of(j * t, t)
        for hd in range(2):
            vh = vt_ref[hd * HEAD_DIM:(hd + 1) * HEAD_DIM, pl.ds(k0, t)]
            vcat = jnp.concatenate([vh, ones_rows], axis=0)
            a_ref[hd] = a_ref[hd] * alphas[hd] + jnp.dot(vcat, ps[hd], preferred_element_type=F32)

    def body(j, carry):
        accumulate(j, *softmax(scores(j, None), shifts(j, True, True)))
        return carry

    lax.fori_loop(j_first, 2 * g - 1, body, 0)
    j1 = jnp.maximum(2 * g - 1, 0)
    j2 = 2 * g
    j3 = 2 * g + 1
    ss1 = scores(j1, None)
    ss2 = scores(j2, future_a)
    ss3 = scores(j3, future_b)
    w1 = softmax(ss1, shifts(j1, g >= 1, g >= 1))
    w2 = softmax(ss2, shifts(j2, True, True))
    accumulate(j1, *w1)
    w3 = softmax(ss3, shifts(j3, False, True))
    accumulate(j2, *w2)
    accumulate(j3, *w3)
    out_t = jnp.concatenate([a_ref[hd, 0:HEAD_DIM, :] / a_ref[hd, HEAD_DIM:HEAD_DIM + 1, :]
                             for hd in range(2)], axis=0)
    o_ref[...] = out_t.T.astype(o_ref.dtype)


def _fox_first_tile(cum, nrm, t):
    cend = cum[:, t - 1::t]
    nt = cend.shape[1]
    cprev = jnp.concatenate([jnp.zeros((cend.shape[0], 1), F32), cend[:, :-1]], axis=1)
    rep = nt // nrm.shape[0]
    qn = jnp.repeat(jnp.sqrt(nrm[:, 0, :]).T, rep, axis=1)
    kn = jnp.repeat(jnp.sqrt(nrm[:, 1, :]).T, rep, axis=1)
    gap = (1.02 * qn[:, :, None] * (kn[:, None, :] + kn[:, :, None])
           + cprev[:, :, None] - cend[:, None, :])
    jj = jnp.arange(nt)[None, None, :]
    ii = jnp.arange(nt)[None, :, None]
    needed = (jj < ii) & jnp.logical_not(gap <= -EXP_UNDERFLOW)
    needed = needed[0::2] | needed[1::2]
    first = jnp.min(jnp.where(needed, jj, ii), axis=2)
    return first.reshape(-1).astype(I32)


def _fox(qkv, tr, nr, cum, nrm):
    s = qkv.shape[0]
    t = FOX_TILE
    npair = N_HEADS_FOX // 2
    tq = 2 * t
    jlo = _fox_first_tile(cum, nrm, t).reshape(npair, s // tq, 2).min(axis=2).reshape(-1)
    cend = cum[:, t - 1::t]
    cprev = jnp.concatenate([jnp.zeros((cend.shape[0], 1), F32), cend[:, :-1]], axis=1)
    q_rows = 2 * (N_HEADS_MOBA // 2)
    grid_spec = pltpu.PrefetchScalarGridSpec(
        num_scalar_prefetch=1,
        grid=(npair, s // tq),
        in_specs=[pl.BlockSpec(memory_space=pltpu.SMEM),
                  pl.BlockSpec((PAIR, tq), lambda p, i, jl: (q_rows + p, i)),
                  pl.BlockSpec((s, PAIR), lambda p, i, jl: (0, p)),
                  pl.BlockSpec((PAIR, s), lambda p, i, jl: (q_rows + npair + p, 0)),
                  pl.BlockSpec((1, s, 2), lambda p, i, jl: (p, 0, 0))],
        out_specs=pl.BlockSpec((tq, PAIR), lambda p, i, jl: (i, p)),
        scratch_shapes=[pltpu.VMEM((2, 1, tq), F32), pltpu.VMEM((2, HEAD_DIM + SUM_ROWS, tq), F32)],
    )
    return pl.pallas_call(
        _fox_kernel,
        out_shape=jax.ShapeDtypeStruct((s, npair * PAIR), BF16),
        grid_spec=grid_spec,
        compiler_params=_cparams(("arbitrary", "arbitrary")),
        name="fox",
    )(jlo, cprev, tr, qkv, tr, nr)


def _t5_bucket_np(dist):
    dist = np.maximum(dist, 0)
    max_exact = NUM_BUCKETS // 2
    d = np.maximum(dist, 1).astype(np.float32)
    large = max_exact + (np.log(d / np.float32(max_exact)) / np.float32(math.log(MAX_DISTANCE / max_exact))
                         * np.float32(NUM_BUCKETS - max_exact)).astype(np.int32)
    large = np.minimum(large, NUM_BUCKETS - 1)
    return np.where(dist < max_exact, dist, large).astype(np.int32)


def _moba_kernel(rb_ref, qt_ref, k_ref, vt_ref, sel_ref, bkt_ref, o_ref,
                 m_ref, a_ref, bias_scr, *bufs):
    s_bufs = bufs[0:MOBA_UNROLL]
    p_bufs = bufs[MOBA_UNROLL:2 * MOBA_UNROLL]
    al_bufs = bufs[2 * MOBA_UNROLL:3 * MOBA_UNROLL]
    p = pl.program_id(0)
    g = pl.program_id(1)
    tq = qt_ref.shape[1]
    t = tq // 2

    @pl.when(g == 0)
    def _():
        r = lax.broadcasted_iota(I32, (t, t), 0)
        c = lax.broadcasted_iota(I32, (t, t), 1)
        zero = jnp.zeros((t, t), F32)
        for hd in range(2):
            h = 2 * p + hd
            far = rb_ref[(NUM_BUCKETS - 1) * N_HEADS_MOBA + h]
            tiles = []
            for w in range(2):
                bkt = bkt_ref[w]
                acc = jnp.zeros(bkt.shape, F32)
                for kk in range(NUM_BUCKETS):
                    acc = acc + jnp.where(bkt == kk, rb_ref[kk * N_HEADS_MOBA + h], 0.0)
                tiles.append((acc - far) * LOG2E)
            prev_t = tiles[0]
            own_t = jnp.where(r <= c, tiles[1], NEG)
            bias_scr[hd, 0] = jnp.concatenate([prev_t, zero], axis=1)
            bias_scr[hd, 1] = jnp.concatenate([own_t, prev_t], axis=1)
            bias_scr[hd, 2] = jnp.concatenate([zero, own_t], axis=1)

    qt = qt_ref[...]
    top = lax.broadcasted_iota(I32, qt.shape, 0) < HEAD_DIM
    zq = jnp.zeros_like(qt)
    qth = (jnp.where(top, qt, zq), jnp.where(top, zq, qt))
    m_ref[...] = jnp.full(m_ref.shape, M_INIT, F32)
    a_ref[...] = jnp.zeros(a_ref.shape, F32)

    ones_rows = jnp.where(lax.broadcasted_iota(I32, (SUM_ROWS, t), 0) == 0, 1.0, 0.0).astype(BF16)
    in_a = lax.broadcasted_iota(I32, (1, tq), 1) < t

    n_far = jnp.maximum(2 * g - 1, 0)

    n_key_tiles = k_ref.shape[0] // t

    def produce(j, s_buf):
        j = jnp.minimum(j, n_key_tiles - 1)
        k0 = pl.multiple_of(j * t, t)
        kt = k_ref[pl.ds(k0, t), :]
        for hd in range(2):
            s_buf[hd] = jnp.dot(kt, qth[hd], preferred_element_type=F32)

    def softmax(s_buf, p_buf, al_buf, selrows, w):
        for hd in range(2):
            s = s_buf[hd]
            if w is not None:
                s = s + bias_scr[hd, w]
            smax = jnp.max(s, axis=0, keepdims=True)
            m_old = m_ref[hd]
            m_new = jnp.maximum(m_old, smax + selrows[hd])
            shift = m_new - selrows[hd]
            al_buf[hd] = jnp.exp2(m_old - m_new)
            m_ref[hd] = m_new
            p_buf[hd * t:(hd + 1) * t, :] = jnp.exp2(s - shift).astype(BF16)

    def far_rows(j):
        return [jnp.where(j < n_far, sel_ref[hd, pl.ds(j, 1), :], NEG) for hd in range(2)]

    def accumulate(j, p_buf, al_buf):
        k0 = pl.multiple_of(j * t, t)
        for hd in range(2):
            vh = vt_ref[hd * HEAD_DIM:(hd + 1) * HEAD_DIM, pl.ds(k0, t)]
            vcat = jnp.concatenate([vh, ones_rows], axis=0)
            a_ref[hd] = (a_ref[hd] * al_buf[hd]
                         + jnp.dot(vcat, p_buf[hd * t:(hd + 1) * t, :], preferred_element_type=F32))

    un = MOBA_UNROLL
    lag = MOBA_LAG
    for n in range(un - lag, un):
        p_bufs[n][...] = jnp.zeros_like(p_bufs[n])
        al_bufs[n][...] = jnp.ones_like(al_bufs[n])
    produce(0, s_bufs[0])

    def body(u, carry):
        j0 = un * u
        for n in range(un):
            accumulate(jnp.maximum(j0 + n - lag, 0), p_bufs[(n - lag) % un], al_bufs[(n - lag) % un])
            softmax(s_bufs[n], p_bufs[n], al_bufs[n], far_rows(j0 + n), None)
            produce(j0 + n + 1, s_bufs[(n + 1) % un])
        return carry

    n_trips = (n_far + un - 1) // un
    lax.fori_loop(0, n_trips, body, 0)

    j1 = jnp.maximum(2 * g - 1, 0)
    j2 = 2 * g
    j3 = 2 * g + 1
    rows1 = [jnp.where(g >= 1, sel_ref[hd, pl.ds(j1, 1), :], NEG) for hd in range(2)]
    rows2 = [jnp.where(in_a, 0.0, sel_ref[hd, pl.ds(j2, 1), :]) for hd in range(2)]
    rows3 = [jnp.where(in_a, NEG, 0.0)] * 2
    produce(j1, s_bufs[0])
    produce(j2, s_bufs[1])
    for n in range(un - lag, un):
        accumulate(jnp.maximum(un * n_trips - un + n, 0), p_bufs[n], al_bufs[n])
    softmax(s_bufs[0], p_bufs[0], al_bufs[0], rows1, 0)
    produce(j3, s_bufs[0])
    softmax(s_bufs[1], p_bufs[1], al_bufs[1], rows2, 1)
    accumulate(j1, p_bufs[0], al_bufs[0])
    softmax(s_bufs[0], p_bufs[2], al_bufs[2], rows3, 2)
    accumulate(j2, p_bufs[1], al_bufs[1])
    accumulate(j3, p_bufs[2], al_bufs[2])
    out_t = jnp.concatenate([a_ref[hd, 0:HEAD_DIM, :] / a_ref[hd, HEAD_DIM:HEAD_DIM + 1, :]
                             for hd in range(2)], axis=0)
    o_ref[...] = out_t.T.astype(o_ref.dtype)


def _moba(qkv, tr, sel, rel_bias):
    s = qkv.shape[0]
    t = MOBA_BLOCK
    npair = N_HEADS_MOBA // 2
    nblk = s // t
    kcol = N_HEADS_FOX // 2
    a = np.arange(t)[None, :]
    b = np.arange(t)[:, None]
    bkt = jnp.asarray(np.stack([_t5_bucket_np(t + a - b), _t5_bucket_np(a - b)]))
    tq = 2 * t
    grid_spec = pltpu.PrefetchScalarGridSpec(
        num_scalar_prefetch=1,
        grid=(npair, s // tq),
        in_specs=[pl.BlockSpec((PAIR, tq), lambda p, i, rb: (p, i)),
                  pl.BlockSpec((s, PAIR), lambda p, i, rb: (0, kcol + p)),
                  pl.BlockSpec((PAIR, s), lambda p, i, rb: (npair + p, 0)),
                  pl.BlockSpec((2, nblk, tq), lambda p, i, rb: (p, 0, i)),
                  pl.BlockSpec((2, t, t), lambda p, i, rb: (0, 0, 0))],
        out_specs=pl.BlockSpec((tq, PAIR), lambda p, i, rb: (i, p)),
        scratch_shapes=[pltpu.VMEM((2, 1, tq), F32),
                        pltpu.VMEM((2, HEAD_DIM + SUM_ROWS, tq), F32), pltpu.VMEM((2, 3, t, tq), F32)]
        + [pltpu.VMEM((2, t, tq), F32)] * MOBA_UNROLL
        + [pltpu.VMEM((2 * t, tq), BF16)] * MOBA_UNROLL
        + [pltpu.VMEM((2, 1, tq), F32)] * MOBA_UNROLL,
    )
    return pl.pallas_call(
        _moba_kernel,
        out_shape=jax.ShapeDtypeStruct((s, npair * PAIR), BF16),
        grid_spec=grid_spec,
        compiler_params=_cparams(("arbitrary", "arbitrary")),
        name="moba",
    )(rel_bias.reshape(-1), tr, qkv, tr, sel, bkt)


def _store_token_tiles(ref, val):
    n = val.shape[0]
    for c in range(SUBLANES):
        ref[pl.ds(c, n, stride=SUBLANES), :] = val[:, c * LANES:(c + 1) * LANES]


def _load_token_tiles(ref, n):
    return jnp.concatenate([ref[pl.ds(c, n, stride=SUBLANES), :] for c in range(SUBLANES)], axis=1)


def _tile_rows(r0, n):
    start = r0 * SUBLANES
    if not isinstance(start, int):
        start = pl.multiple_of(start, SUBLANES)
    return pl.ds(start, n * SUBLANES)


def _outproj_kernel(ya_ref, yb_ref, x_ref, wa_ref, wb_ref, gpost_ref, gt_ref, gpre_ref,
                    sc_ref, sh_ref, wr_ref, br_ref,
                    x1_ref, h2_ref, route_ref, gate_ref, cnt_ref, carry_scr):
    i = pl.program_id(0)
    tm = x_ref.shape[0]

    @pl.when(i == 0)
    def _():
        carry_scr[...] = jnp.zeros_like(carry_scr)

    y = (jnp.dot(ya_ref[...], wa_ref[...], preferred_element_type=F32)
         + jnp.dot(yb_ref[...], wb_ref[...], preferred_element_type=F32))
    x1 = x_ref[...] + gt_ref[...] * (_rms(y) * gpost_ref[...])
    x1_ref[...] = x1
    h2 = _rms(x1) * gpre_ref[...] * (1.0 + sc_ref[...]) + sh_ref[...]
    _store_token_tiles(h2_ref, h2)

    h_hi = h2.astype(BF16)
    h_lo = (h2 - h_hi.astype(F32)).astype(BF16)
    logits = (jnp.dot(h_hi, wr_ref[0], preferred_element_type=F32)
              + jnp.dot(h_hi, wr_ref[1], preferred_element_type=F32)
              + jnp.dot(h_lo, wr_ref[0], preferred_element_type=F32)) + br_ref[...]
    ne = cnt_ref.shape[0]
    g = logits.T[0:ne, :]
    expert = lax.broadcasted_iota(I32, g.shape, 0)
    mask = jnp.zeros(g.shape, F32)
    vals, firsts, picks = [], [], []
    for _ in range(TOP_K):
        m = jnp.max(g, axis=0, keepdims=True)
        first = jnp.min(jnp.where(g == m, expert, ne), axis=0, keepdims=True)
        pick = expert == first
        mask = jnp.where(pick, 1.0, mask)
        g = jnp.where(pick, -jnp.inf, g)
        vals.append(m)
        firsts.append(first)
        picks.append(pick)
    ex = [jnp.exp(v - vals[0]) for v in vals]
    den = ex[0] + ex[1] + ex[2] + ex[3]
    gates = [e / den for e in ex]

    r = lax.broadcasted_iota(I32, (tm, tm), 0)
    c = lax.broadcasted_iota(I32, (tm, tm), 1)
    earlier = jnp.where(r < c, 1.0, 0.0).astype(BF16)
    before = jnp.dot(mask.astype(BF16), earlier, preferred_element_type=F32) + carry_scr[...]
    total = carry_scr[...] + jnp.sum(mask, axis=1, keepdims=True)
    carry_scr[...] = total
    cnt_ref[...] = jnp.broadcast_to(total, cnt_ref.shape)

    ranks = [jnp.sum(jnp.where(pk, before, 0.0), axis=0, keepdims=True).astype(I32) for pk in picks]
    route_ref[...] = jnp.concatenate(firsts + ranks, axis=0)
    gate_rows = jnp.concatenate(gates + [jnp.zeros((LANES - TOP_K, tm), F32)], axis=0)
    gate_ref[...] = gate_rows.T[:, 0:TOP_K]


def _outproj(mix_a, mix_b, x2, w_a, w_b, gpost, gt, gpre, sc, sh, w_router, b_router):
    s, d = x2.shape
    tm = ROW_TILE
    ne = w_router.shape[1]
    wa = mix_a.shape[1]
    w_router = jnp.pad(w_router, ((0, 0), (0, LANES - ne)))
    b_router = jnp.pad(b_router, ((0, 0), (0, LANES - ne)))
    w_hi = w_router.astype(BF16)
    w_router_hl = jnp.stack([w_hi, (w_router - w_hi.astype(F32)).astype(BF16)])
    row = lambda i: (i, 0)
    fix = lambda i: (0, 0)
    vec = pl.BlockSpec((1, d), fix)
    return pl.pallas_call(
        _outproj_kernel,
        out_shape=(jax.ShapeDtypeStruct((s, d), F32),
                   jax.ShapeDtypeStruct((s * SUBLANES, LANES), F32),
                   jax.ShapeDtypeStruct((2 * TOP_K, s), I32),
                   jax.ShapeDtypeStruct((s, TOP_K), F32),
                   jax.ShapeDtypeStruct((ne, LANES), F32)),
        grid=(s // tm,),
        in_specs=[pl.BlockSpec((tm, wa), row), pl.BlockSpec((tm, wa), row), pl.BlockSpec((tm, d), row),
                  pl.BlockSpec((wa, d), fix), pl.BlockSpec((wa, d), fix),
                  vec, vec, vec, vec, vec,
                  pl.BlockSpec((2, d, LANES), lambda i: (0, 0, 0)), pl.BlockSpec((1, LANES), fix)],
        out_specs=(pl.BlockSpec((tm, d), row), pl.BlockSpec((tm * SUBLANES, LANES), row),
                   pl.BlockSpec((2 * TOP_K, tm), lambda i: (0, i)),
                   pl.BlockSpec((tm, TOP_K), row), pl.BlockSpec((ne, LANES), fix)),
        scratch_shapes=[pltpu.VMEM((ne, 1), F32)],
        compiler_params=_cparams(("arbitrary",)),
        name="outproj",
    )(mix_a, mix_b, x2, w_a, w_b, gpost, gt, gpre, sc, sh, w_router_hl, b_router)


def _dest_kernel(route_ref, pstart_ref, o_ref):
    route = route_ref[...]
    tm = route.shape[1]
    ne = pstart_ref.shape[0]
    expert = lax.broadcasted_iota(I32, (ne, tm), 0)
    for k in range(TOP_K):
        start = jnp.sum(jnp.where(expert == route[k:k + 1, :], pstart_ref[...], 0), axis=0, keepdims=True)
        row = start + route[TOP_K + k:TOP_K + k + 1, :]
        for g in range(tm // LANES):
            o_ref[g * TOP_K + k:g * TOP_K + k + 1, :] = row[:, g * LANES:(g + 1) * LANES]


def _dest(route, pstart):
    s = route.shape[1]
    tm = min(4 * ROW_TILE, s)
    return pl.pallas_call(
        _dest_kernel,
        out_shape=jax.ShapeDtypeStruct((s // LANES * TOP_K, LANES), I32),
        grid=(s // tm,),
        in_specs=[pl.BlockSpec((2 * TOP_K, tm), lambda i: (0, i)),
                  pl.BlockSpec((N_EXPERTS, 1), lambda i: (0, 0))],
        out_specs=pl.BlockSpec((tm // LANES * TOP_K, LANES), lambda i: (i, 0)),
        compiler_params=_cparams(("arbitrary",)),
        name="dest",
    )(route, pstart.reshape(-1, 1))


def _dispatch_kernel(pstart_ref, pblk_ref, nu_ref, dest_ref, h_ref, xs_ref,
                     zero_scr, sem, zsem):
    tm = h_ref.shape[0] // SUBLANES
    bm = zero_scr.shape[0] // SUBLANES
    n_blk = xs_ref.shape[0] // (bm * SUBLANES)

    rows = _tile_rows

    @pl.when(pl.program_id(0) == 0)
    def _():
        zero_scr[...] = jnp.zeros_like(zero_scr)

        def zero_copy(row0):
            return pltpu.make_async_copy(zero_scr, xs_ref.at[rows(row0, bm)], zsem)

        for phase in range(2):
            for e in range(N_EXPERTS):
                last = pstart_ref[e] + (pblk_ref[e] - 1) * bm
                tail = (n_blk - N_EXPERTS + e) * bm
                for cond, row0 in ((pblk_ref[e] > 0, last), (n_blk - N_EXPERTS + e >= nu_ref[0], tail)):
                    @pl.when(cond)
                    def _():
                        if phase == 0:
                            zero_copy(row0).start()
                        else:
                            zero_copy(row0).wait()

    def start_group(g, carry):
        for lane in range(LANES):
            r = g * LANES + lane
            for k in range(TOP_K):
                dst = dest_ref[g * TOP_K + k, lane]
                pltpu.make_async_copy(h_ref.at[rows(r, 1)], xs_ref.at[rows(dst, 1)], sem).start(priority=k % 2)
        return carry

    lax.fori_loop(0, tm // LANES, start_group, 0)
    for k in range(TOP_K):
        pltpu.make_async_copy(h_ref, xs_ref.at[rows(0, tm)], sem).wait()


def _dispatch(pstart, pblk, n_used, dest, h2, n_rows):
    s = h2.shape[0] // SUBLANES
    tm = COMBINE_TILE
    grid_spec = pltpu.PrefetchScalarGridSpec(
        num_scalar_prefetch=3,
        grid=(s // tm,),
        in_specs=[pl.BlockSpec((tm // LANES * TOP_K, LANES), lambda i, *_: (i, 0), memory_space=pltpu.SMEM),
                  pl.BlockSpec((tm * SUBLANES, LANES), lambda i, *_: (i, 0))],
        out_specs=pl.BlockSpec(memory_space=pl.ANY),
        scratch_shapes=[pltpu.VMEM((EXPERT_BLOCK * SUBLANES, LANES), F32), pltpu.SemaphoreType.DMA,
                        pltpu.SemaphoreType.DMA],
    )
    return pl.pallas_call(
        _dispatch_kernel,
        out_shape=jax.ShapeDtypeStruct((n_rows * SUBLANES, LANES), F32),
        grid_spec=grid_spec,
        compiler_params=_cparams(("arbitrary",)),
        name="dispatch",
    )(pstart, pblk, n_used, dest, h2)


def _experts_kernel(be_ref, nu_ref, xs_ref, wgu_ref, bgu_ref, wd_ref, bd_ref, y_ref,
                    wgu_bf, wd_bf):
    b = pl.program_id(0)
    d_exp = wd_ref.shape[1]
    prev = be_ref[jnp.maximum(b - 1, 0)]
    changed = (b == 0) | (be_ref[b] != prev)

    @pl.when((b < nu_ref[0]) & changed)
    def _():
        rows = LANES

        def cast_gu(c, carry):
            r0 = pl.multiple_of(c * rows, rows)
            wgu_bf[pl.ds(r0, rows), :] = wgu_ref[0, pl.ds(r0, rows), :].astype(BF16)
            return carry

        def cast_d(c, carry):
            r0 = pl.multiple_of(c * rows, rows)
            wd_bf[pl.ds(r0, rows), :] = wd_ref[0, pl.ds(r0, rows), :].astype(BF16)
            return carry

        lax.fori_loop(0, wgu_ref.shape[1] // rows, cast_gu, 0)
        lax.fori_loop(0, wd_ref.shape[1] // rows, cast_d, 0)

    @pl.when(b < nu_ref[0])
    def _():
        bm = xs_ref.shape[0] // SUBLANES
        xb = _load_token_tiles(xs_ref, bm).astype(BF16)
        hdn = jnp.dot(xb, wgu_bf[...], preferred_element_type=F32) + bgu_ref[0]
        x_glu = jnp.minimum(hdn[:, :d_exp], SWIGLU_LIMIT)
        x_lin = jnp.clip(hdn[:, d_exp:], -SWIGLU_LIMIT, SWIGLU_LIMIT)
        act = x_glu * jax.nn.sigmoid(SWIGLU_ALPHA * x_glu) * (x_lin + 1.0)
        _store_token_tiles(y_ref, jnp.dot(act.astype(BF16), wd_bf[...], preferred_element_type=F32)
                           + bd_ref[0])

    @pl.when(b >= nu_ref[0])
    def _():
        y_ref[...] = jnp.zeros_like(y_ref)


def _experts(block_e, n_used, xs, w_gate_up, b_gate_up, w_down, b_down):
    n_rows = xs.shape[0] // SUBLANES
    bm = EXPERT_BLOCK
    n_blk = n_rows // bm
    ne, d, two_de = w_gate_up.shape
    de = w_down.shape[1]
    assert d == SUBLANES * LANES

    def blk(b, be, nu):
        return jnp.minimum(b, nu[0] - 1)

    grid_spec = pltpu.PrefetchScalarGridSpec(
        num_scalar_prefetch=2,
        grid=(n_blk,),
        in_specs=[pl.BlockSpec((bm * SUBLANES, LANES), lambda b, be, nu: (blk(b, be, nu), 0)),
                  pl.BlockSpec((1, d, two_de), lambda b, be, nu: (be[blk(b, be, nu)], 0, 0)),
                  pl.BlockSpec((1, 1, two_de), lambda b, be, nu: (be[blk(b, be, nu)], 0, 0)),
                  pl.BlockSpec((1, de, d), lambda b, be, nu: (be[blk(b, be, nu)], 0, 0)),
                  pl.BlockSpec((1, 1, d), lambda b, be, nu: (be[blk(b, be, nu)], 0, 0))],
        out_specs=pl.BlockSpec((bm * SUBLANES, LANES), lambda b, be, nu: (b, 0)),
        scratch_shapes=[pltpu.VMEM((d, two_de), BF16), pltpu.VMEM((de, d), BF16)],
    )
    return pl.pallas_call(
        _experts_kernel,
        out_shape=jax.ShapeDtypeStruct((n_rows * SUBLANES, LANES), F32),
        grid_spec=grid_spec,
        compiler_params=_cparams(("arbitrary",)),
        name="experts",
    )(block_e, n_used, xs, w_gate_up, b_gate_up.reshape(ne, 1, two_de), w_down, b_down.reshape(ne, 1, d))


def _combine_kernel(dest_ref, dest_next_ref, y_ref, gate_ref, x1_ref, gt_ref, gpost_ref, o_ref,
                    buf_even, buf_odd, sem):
    i = pl.program_id(0)
    tm = x1_ref.shape[0]
    ch = COMBINE_CHUNK
    bufs = (buf_even, buf_odd)

    def start_chunk(d_ref, slot, c):
        for lane in range(ch):
            r = c * ch + lane
            for k in range(TOP_K):
                pltpu.make_async_copy(y_ref.at[_tile_rows(d_ref[c * TOP_K + k, lane], 1)],
                                      bufs[slot].at[k, _tile_rows(r, 1)], sem.at[slot, c]).start(priority=k % 2)

    @pl.when(i == 0)
    def _():
        def first(c, carry):
            start_chunk(dest_ref, 0, c)
            return carry

        lax.fori_loop(0, tm // ch, first, 0)

    def reduce_chunk(c, cur, gather_next):
        buf = bufs[cur]
        for k in range(TOP_K):
            pltpu.make_async_copy(y_ref.at[_tile_rows(0, ch)], buf.at[k, _tile_rows(c * ch, ch)],
                                  sem.at[cur, c]).wait()
        if gather_next:
            start_chunk(dest_next_ref, 1 - cur, c)
        n = COMBINE_REDUCE_ROWS
        for part in range(ch // n):
            r0 = pl.multiple_of(c * ch + part * n, n)
            rows = pl.ds(r0, n)
            gate = gate_ref[rows, :]
            acc = gate[:, 0:1] * _load_token_tiles(buf.at[0, _tile_rows(r0, n)], n)
            for k in range(1, TOP_K):
                acc = acc + gate[:, k:k + 1] * _load_token_tiles(buf.at[k, _tile_rows(r0, n)], n)
            o_ref[rows, :] = x1_ref[rows, :] + gt_ref[...] * (_rms(acc) * gpost_ref[...])

    has_next = i + 1 < pl.num_programs(0)
    for cur in range(2):
        for gather_next in (True, False):
            @pl.when((i % 2 == cur) & (has_next == gather_next))
            def _(cur=cur, gather_next=gather_next):
                def body(c, carry):
                    reduce_chunk(c, cur, gather_next)
                    return carry

                lax.fori_loop(0, tm // ch, body, 0)


def _combine(dest, y, gate4, x1, gt, gpost):
    s, d = x1.shape
    tm = COMBINE_TILE // 2
    n = s // tm
    return pl.pallas_call(
        _combine_kernel,
        out_shape=jax.ShapeDtypeStruct((s, d), F32),
        grid=(n,),
        in_specs=[pl.BlockSpec((tm // LANES * TOP_K, LANES), lambda i: (i, 0), memory_space=pltpu.SMEM),
                  pl.BlockSpec((tm // LANES * TOP_K, LANES), lambda i: (jnp.minimum(i + 1, n - 1), 0),
                               memory_space=pltpu.SMEM),
                  pl.BlockSpec(memory_space=pl.ANY),
                  pl.BlockSpec((tm, TOP_K), lambda i: (i, 0)),
                  pl.BlockSpec((tm, d), lambda i: (i, 0)),
                  pl.BlockSpec((1, d), lambda i: (0, 0)),
                  pl.BlockSpec((1, d), lambda i: (0, 0))],
        out_specs=pl.BlockSpec((tm, d), lambda i: (i, 0)),
        scratch_shapes=[pltpu.VMEM((TOP_K, tm * SUBLANES, LANES), F32),
                        pltpu.VMEM((TOP_K, tm * SUBLANES, LANES), F32),
                        pltpu.SemaphoreType.DMA((2, tm // COMBINE_CHUNK))],
        compiler_params=_cparams(("arbitrary",)),
        name="combine",
    )(dest, dest, y, gate4, x1, gt, gpost)


def _layer(x2, mod, g_pre_mix, g_post_mix, w_in, b_forget, rel_bias, w_out,
           g_pre_ffn, g_post_ffn, w_router, b_router, w_gate_up, b_gate_up, w_down, b_down):
    s, d = x2.shape
    sh_m, sc_m, gt_m, sh_f, sc_f, gt_f = [mod[:, k * d:(k + 1) * d] for k in range(6)]
    n_qkv = 3 * (N_HEADS_FOX + N_HEADS_MOBA) * HEAD_DIM
    fox_w = N_HEADS_FOX * HEAD_DIM

    w_qkv = w_in[:, :n_qkv].astype(BF16)
    w_f = w_in[:, n_qkv:].T
    qkv, cum, sel, nrm, tr, nr = _inproj(x2, g_pre_mix.reshape(1, d), sc_m, sh_m, w_qkv, w_f,
                                         b_forget.reshape(-1, 1))
    y_a = _fox(qkv, tr, nr, cum, nrm)
    y_b = _moba(qkv, tr, sel, rel_bias)

    w_out_bf = w_out.astype(BF16)
    x1, h2, route, gate4, cnt = _outproj(
        y_a, y_b, x2, w_out_bf[:fox_w], w_out_bf[fox_w:], g_post_mix.reshape(1, d), gt_m,
        g_pre_ffn.reshape(1, d), sc_f, sh_f, w_router, b_router.reshape(1, -1))

    bm = EXPERT_BLOCK
    counts = cnt[:, 0].astype(I32)
    pblk = (counts + bm - 1) // bm
    pend_blk = jnp.cumsum(pblk)
    pstart = ((pend_blk - pblk) * bm).astype(I32)
    n_rows = s * TOP_K + N_EXPERTS * bm
    n_blk = n_rows // bm
    block_e = jnp.minimum(jnp.sum(pend_blk[None, :] <= jnp.arange(n_blk)[:, None], axis=1),
                          N_EXPERTS - 1).astype(I32)
    n_used = pend_blk[-1:].astype(I32)

    dest = _dest(route, pstart)
    xs = _dispatch(pstart, pblk.astype(I32), n_used, dest, h2, n_rows)
    y = _experts(block_e, n_used, xs, w_gate_up, b_gate_up, w_down, b_down)
    return _combine(dest, y, gate4, x1, gt_f, g_post_ffn.reshape(1, d))


def kernel(x, c, w_ada, b_ada, g_pre_mix, g_post_mix, w_in, b_forget, rel_bias, w_out, g_pre_ffn, g_post_ffn, w_router, b_router, w_gate_up, b_gate_up, w_down, b_down):
    bsz, s, d = x.shape
    depth = w_ada.shape[0]
    outs = []
    for bi in range(bsz):
        x2 = x[bi]
        for l in range(depth):
            mod = _adaln(c[bi:bi + 1], w_ada[l], b_ada[l])
            x2 = _layer(x2, mod, g_pre_mix[l], g_post_mix[l], w_in[l], b_forget[l], rel_bias, w_out[l],
                        g_pre_ffn[l], g_post_ffn[l], w_router[l], b_router[l], w_gate_up[l], b_gate_up[l],
                        w_down[l], b_down[l])
        outs.append(x2)
    return outs[0].reshape(1, s, d) if bsz == 1 else jnp.stack(outs)
```

```python
import math

import numpy as np
import jax
import jax.numpy as jnp
from jax import lax
from jax.experimental import pallas as pl
from jax.experimental.pallas import tpu as pltpu

F32 = jnp.float32
BF16 = jnp.bfloat16
I32 = jnp.int32

HEAD_DIM = 64
N_HEADS_FOX = 8
N_HEADS_MOBA = 8
PAIR = 2 * HEAD_DIM
MOBA_BLOCK = 256
MOBA_TOPK = 3
NUM_BUCKETS = 32
MAX_DISTANCE = 128
N_EXPERTS = 32
TOP_K = 4
SWIGLU_LIMIT = 7.0
SWIGLU_ALPHA = 1.702
RMS_EPS = 1e-6
NEG = -(2.0 ** 100)
M_INIT = -(2.0 ** 99)
LOG2E = math.log2(math.e)
SUM_ROWS = 16
SUBLANES = 8
LANES = 128
EXP_UNDERFLOW = 90.0
VMEM_LIMIT = 56 * 1024 * 1024

ROW_TILE = 512
FOX_TILE = 256
EXPERT_BLOCK = 512
COMBINE_TILE = 1024
COMBINE_CHUNK = LANES
COMBINE_REDUCE_ROWS = 64
MOBA_LAG = 1
MOBA_UNROLL = 4

NT_DIMS = (((1,), (1,)), ((), ()))


def _cparams(sem):
    return pltpu.CompilerParams(dimension_semantics=sem, vmem_limit_bytes=VMEM_LIMIT)


def _rms(x):
    return x * lax.rsqrt(jnp.mean(x * x, axis=-1, keepdims=True) + RMS_EPS)


def _adaln_kernel(c_ref, w_ref, b_ref, o_ref):
    c = c_ref[...]
    cond = c * jax.nn.sigmoid(c)
    o_ref[...] = jnp.dot(cond, w_ref[...], preferred_element_type=F32,
                         precision=lax.Precision.HIGHEST) + b_ref[...]


def _adaln(c, w_ada, b_ada):
    d = c.shape[-1]
    n = w_ada.shape[-1]
    c8 = jnp.broadcast_to(c.reshape(1, d), (8, d))
    out = pl.pallas_call(
        _adaln_kernel,
        out_shape=jax.ShapeDtypeStruct((8, n), F32),
        grid=(n // d,),
        in_specs=[pl.BlockSpec((8, d), lambda j: (0, 0)),
                  pl.BlockSpec((d, d), lambda j: (0, j)),
                  pl.BlockSpec((1, d), lambda j: (0, j))],
        out_specs=pl.BlockSpec((8, d), lambda j: (0, j)),
        compiler_params=_cparams(("arbitrary",)),
        name="adaln",
    )(c8, w_ada, b_ada.reshape(1, n))
    return out[0:1]


def _inproj_kernel(x_ref, g_ref, sc_ref, sh_ref, w_ref, wf_ref, bf_ref,
                   qkv_ref, cum_ref, sel_ref, nrm_ref, tr_ref, nr_ref, km_scr, carry_scr):
    i = pl.program_id(0)
    tm = x_ref.shape[0]
    nblk = km_scr.shape[0]

    @pl.when(i == 0)
    def _():
        km_scr[...] = jnp.zeros_like(km_scr)
        carry_scr[...] = jnp.zeros_like(carry_scr)

    x = x_ref[...]
    h = _rms(x) * g_ref[...] * (1.0 + sc_ref[...]) + sh_ref[...]
    hb = h.astype(BF16)

    width = N_HEADS_FOX * HEAD_DIM
    hsel = jnp.where(lax.broadcasted_iota(I32, (width, N_HEADS_FOX), 0) // HEAD_DIM
                     == lax.broadcasted_iota(I32, (width, N_HEADS_FOX), 1), 1.0, 0.0)
    kb = None
    tr_slot = {3: 0, 5: 1, 0: 2, 2: 3}
    qbt = None
    for c in range(6):
        pc = jnp.dot(hb, w_ref[:, c * width:(c + 1) * width], preferred_element_type=F32)
        if c in tr_slot:
            n = tr_slot[c]
            pct = pc.T
            if c == 3:
                qbt = pct
            if c == 0 or c == 3:
                pct = pct * (LOG2E * HEAD_DIM ** -0.5)
            tr_ref[n * width:(n + 1) * width, :] = pct.astype(BF16)
        if c == 1 or c == 4:
            qkv_ref[:, (c // 3) * width:(c // 3 + 1) * width] = pc.astype(BF16)
        if c < 2:
            sq = (pc * (HEAD_DIM ** -0.5) if c == 0 else pc).astype(BF16).astype(F32)
            n2 = jnp.dot((sq * sq).astype(BF16), hsel.astype(BF16), preferred_element_type=F32)
            nrm_ref[0, c:c + 1, :] = jnp.max(n2, axis=0, keepdims=True)
        if c == 4:
            kb = pc

    ft = lax.dot_general(wf_ref[...].astype(BF16), hb, NT_DIMS, preferred_element_type=F32)
    z = ft + bf_ref[...]
    logf = -(jnp.maximum(-z, 0.0) + jnp.log1p(jnp.exp(-jnp.abs(z))))
    lane = lax.broadcasted_iota(I32, logf.shape, 1)
    cs = logf
    sh = 1
    while sh < tm:
        cs = cs + jnp.where(lane >= sh, pltpu.roll(cs, sh, axis=1), 0.0)
        sh *= 2
    base = jnp.zeros_like(cs)
    for b in range(1, tm // FOX_TILE):
        base = jnp.where(lane >= b * FOX_TILE, cs[:, b * FOX_TILE - 1:b * FOX_TILE], base)
    nr_t = (-LOG2E * (cs - base)).T
    for pp in range(nr_ref.shape[0]):
        nr_ref[pp] = nr_t[:, 2 * pp:2 * pp + 2]
    cs = cs + carry_scr[...]
    cum_ref[...] = cs
    carry_scr[...] = cs[:, tm - 1:tm]

    nb_tile = tm // MOBA_BLOCK
    for b in range(nb_tile):
        kmean = jnp.sum(kb[b * MOBA_BLOCK:(b + 1) * MOBA_BLOCK], axis=0, keepdims=True) * (1.0 / MOBA_BLOCK)
        km_scr[pl.ds(i * nb_tile + b, 1), :] = kmean

    km = km_scr[...]
    blk = lax.broadcasted_iota(I32, (nblk, tm), 0)
    col = lax.broadcasted_iota(I32, (nblk, tm), 1)
    own = i * nb_tile + col // MOBA_BLOCK
    for hd in range(N_HEADS_MOBA):
        hs = slice(hd * HEAD_DIM, (hd + 1) * HEAD_DIM)
        g = jnp.dot(km[:, hs], qbt[hs, :], preferred_element_type=F32,
                    precision=lax.Precision.HIGHEST)
        g = jnp.where(blk < own, g, -jnp.inf)
        sel = jnp.zeros(g.shape, dtype=jnp.bool_)
        for _ in range(MOBA_TOPK):
            m = jnp.max(g, axis=0, keepdims=True)
            first = jnp.min(jnp.where(g == m, blk, nblk), axis=0, keepdims=True)
            pick = (blk == first) & (m > -jnp.inf)
            sel = sel | pick
            g = jnp.where(pick, -jnp.inf, g)
        sel_ref[hd] = jnp.where(sel, 0.0, NEG)


def _inproj(x2, g, sc, sh, w_qkv, w_f, b_f):
    s, d = x2.shape
    tm = ROW_TILE
    nblk = s // MOBA_BLOCK
    n = w_qkv.shape[1]
    width = N_HEADS_FOX * HEAD_DIM
    n_rows_out = 2 * width
    n_tr = 4 * width
    assert tm % FOX_TILE == 0 and N_HEADS_FOX == N_HEADS_MOBA
    return pl.pallas_call(
        _inproj_kernel,
        out_shape=(jax.ShapeDtypeStruct((s, n_rows_out), BF16),
                   jax.ShapeDtypeStruct((N_HEADS_FOX, s), F32),
                   jax.ShapeDtypeStruct((N_HEADS_MOBA, nblk, s), F32),
                   jax.ShapeDtypeStruct((s // tm, 2, N_HEADS_FOX), F32),
                   jax.ShapeDtypeStruct((n_tr, s), BF16),
                   jax.ShapeDtypeStruct((N_HEADS_FOX // 2, s, 2), F32)),
        grid=(s // tm,),
        in_specs=[pl.BlockSpec((tm, d), lambda i: (i, 0)),
                  pl.BlockSpec((1, d), lambda i: (0, 0)),
                  pl.BlockSpec((1, d), lambda i: (0, 0)),
                  pl.BlockSpec((1, d), lambda i: (0, 0)),
                  pl.BlockSpec((d, n), lambda i: (0, 0)),
                  pl.BlockSpec((N_HEADS_FOX, d), lambda i: (0, 0)),
                  pl.BlockSpec((N_HEADS_FOX, 1), lambda i: (0, 0))],
        out_specs=(pl.BlockSpec((tm, n_rows_out), lambda i: (i, 0)),
                   pl.BlockSpec((N_HEADS_FOX, tm), lambda i: (0, i)),
                   pl.BlockSpec((N_HEADS_MOBA, nblk, tm), lambda i: (0, 0, i)),
                   pl.BlockSpec((1, 2, N_HEADS_FOX), lambda i: (i, 0, 0)),
                   pl.BlockSpec((n_tr, tm), lambda i: (0, i)),
                   pl.BlockSpec((N_HEADS_FOX // 2, tm, 2), lambda i: (0, i, 0))),
        scratch_shapes=[pltpu.VMEM((nblk, N_HEADS_MOBA * HEAD_DIM), F32),
                        pltpu.VMEM((N_HEADS_FOX, 1), F32)],
        compiler_params=_cparams(("arbitrary",)),
        name="inproj",
    )(x2, g, sc, sh, w_qkv, w_f, b_f)


def _fox_kernel(jlo_ref, cp_ref, qt_ref, k_ref, vt_ref, nr_ref, o_ref, m_ref, a_ref):
    p = pl.program_id(0)
    g = pl.program_id(1)
    j_first = jlo_ref[p * pl.num_programs(1) + g]
    tq = qt_ref.shape[1]
    t = tq // 2

    m_ref[...] = jnp.full(m_ref.shape, M_INIT, F32)
    a_ref[...] = jnp.zeros(a_ref.shape, F32)
    ones_rows = jnp.where(lax.broadcasted_iota(I32, (SUM_ROWS, t), 0) == 0, 1.0, 0.0).astype(BF16)
    in_a = lax.broadcasted_iota(I32, (1, tq), 1) < t
    krow = lax.broadcasted_iota(I32, (t, tq), 0)
    qcol = lax.broadcasted_iota(I32, (t, tq), 1)
    future_a = (qcol < t) & (krow > qcol)
    future_b = (qcol >= t) & (krow > qcol - t)

    def scores(j, future):
        k0 = pl.multiple_of(j * t, t)
        out = []
        for hd in range(2):
            kh = k_ref[pl.ds(k0, t), hd * HEAD_DIM:(hd + 1) * HEAD_DIM]
            qh = qt_ref[hd * HEAD_DIM:(hd + 1) * HEAD_DIM, :]
            s = jnp.dot(kh, qh, preferred_element_type=F32) + nr_ref[0, pl.ds(k0, t), hd:hd + 1]
            out.append(s if future is None else jnp.where(future, NEG, s))
        return out

    def shifts(j, a_on, b_on):
        rows = []
        for hd in range(2):
            h = 2 * p + hd
            sa = LOG2E * (cp_ref[h, 2 * g] - cp_ref[h, j])
            sb = LOG2E * (cp_ref[h, 2 * g + 1] - cp_ref[h, j])
            rows.append(jnp.where(in_a, jnp.where(a_on, sa, NEG), jnp.where(b_on, sb, NEG)))
        return rows

    def softmax(ss, shift_rows):
        ps, alphas = [], []
        for hd in range(2):
            m_old = m_ref[hd]
            m_new = jnp.maximum(m_old, jnp.max(ss[hd], axis=0, keepdims=True) + shift_rows[hd])
            alphas.append(jnp.exp2(m_old - m_new))
            m_ref[hd] = m_new
            ps.append(jnp.exp2(ss[hd] - (m_new - shift_rows[hd])).astype(BF16))
        return ps, alphas

    def accumulate(j, ps, alphas):
        k0 = pl.multiple_of(j * t, t)
        for hd in range(2):
            vh = vt_ref[hd * HEAD_DIM:(hd + 1) * HEAD_DIM, pl.ds(k0, t)]
            vcat = jnp.concatenate([vh, ones_rows], axis=0)
            a_ref[hd] = a_ref[hd] * alphas[hd] + jnp.dot(vcat, ps[hd], preferred_element_type=F32)

    def body(j, carry):
        accumulate(j, *softmax(scores(j, None), shifts(j, True, True)))
        return carry

    lax.fori_loop(j_first, 2 * g - 1, body, 0)
    j1 = jnp.maximum(2 * g - 1, 0)
    j2 = 2 * g
    j3 = 2 * g + 1
    ss1 = scores(j1, None)
    ss2 = scores(j2, future_a)
    ss3 = scores(j3, future_b)
    w1 = softmax(ss1, shifts(j1, g >= 1, g >= 1))
    w2 = softmax(ss2, shifts(j2, True, True))
    accumulate(j1, *w1)
    w3 = softmax(ss3, shifts(j3, False, True))
    accumulate(j2, *w2)
    accumulate(j3, *w3)
    out_t = jnp.concatenate([a_ref[hd, 0:HEAD_DIM, :] / a_ref[hd, HEAD_DIM:HEAD_DIM + 1, :]
                             for hd in range(2)], axis=0)
    o_ref[...] = out_t.T.astype(o_ref.dtype)


def _fox_first_tile(cum, nrm, t):
    cend = cum[:, t - 1::t]
    nt = cend.shape[1]
    cprev = jnp.concatenate([jnp.zeros((cend.shape[0], 1), F32), cend[:, :-1]], axis=1)
    rep = nt // nrm.shape[0]
    qn = jnp.repeat(jnp.sqrt(nrm[:, 0, :]).T, rep, axis=1)
    kn = jnp.repeat(jnp.sqrt(nrm[:, 1, :]).T, rep, axis=1)
    gap = (1.02 * qn[:, :, None] * (kn[:, None, :] + kn[:, :, None])
           + cprev[:, :, None] - cend[:, None, :])
    jj = jnp.arange(nt)[None, None, :]
    ii = jnp.arange(nt)[None, :, None]
    needed = (jj < ii) & jnp.logical_not(gap <= -EXP_UNDERFLOW)
    needed = needed[0::2] | needed[1::2]
    first = jnp.min(jnp.where(needed, jj, ii), axis=2)
    return first.reshape(-1).astype(I32)


def _fox(qkv, tr, nr, cum, nrm):
    s = qkv.shape[0]
    t = FOX_TILE
    npair = N_HEADS_FOX // 2
    tq = 2 * t
    jlo = _fox_first_tile(cum, nrm, t).reshape(npair, s // tq, 2).min(axis=2).reshape(-1)
    cend = cum[:, t - 1::t]
    cprev = jnp.concatenate([jnp.zeros((cend.shape[0], 1), F32), cend[:, :-1]], axis=1)
    q_rows = 2 * (N_HEADS_MOBA // 2)
    grid_spec = pltpu.PrefetchScalarGridSpec(
        num_scalar_prefetch=1,
        grid=(npair, s // tq),
        in_specs=[pl.BlockSpec(memory_space=pltpu.SMEM),
                  pl.BlockSpec((PAIR, tq), lambda p, i, jl: (q_rows + p, i)),
                  pl.BlockSpec((s, PAIR), lambda p, i, jl: (0, p)),
                  pl.BlockSpec((PAIR, s), lambda p, i, jl: (q_rows + npair + p, 0)),
                  pl.BlockSpec((1, s, 2), lambda p, i, jl: (p, 0, 0))],
        out_specs=pl.BlockSpec((tq, PAIR), lambda p, i, jl: (i, p)),
        scratch_shapes=[pltpu.VMEM((2, 1, tq), F32), pltpu.VMEM((2, HEAD_DIM + SUM_ROWS, tq), F32)],
    )
    return pl.pallas_call(
        _fox_kernel,
        out_shape=jax.ShapeDtypeStruct((s, npair * PAIR), BF16),
        grid_spec=grid_spec,
        compiler_params=_cparams(("arbitrary", "arbitrary")),
        name="fox",
    )(jlo, cprev, tr, qkv, tr, nr)


def _t5_bucket_np(dist):
    dist = np.maximum(dist, 0)
    max_exact = NUM_BUCKETS // 2
    d = np.maximum(dist, 1).astype(np.float32)
    large = max_exact + (np.log(d / np.float32(max_exact)) / np.float32(math.log(MAX_DISTANCE / max_exact))
                         * np.float32(NUM_BUCKETS - max_exact)).astype(np.int32)
    large = np.minimum(large, NUM_BUCKETS - 1)
    return np.where(dist < max_exact, dist, large).astype(np.int32)


def _moba_kernel(rb_ref, qt_ref, k_ref, vt_ref, sel_ref, bkt_ref, o_ref,
                 m_ref, a_ref, bias_scr, *bufs):
    s_bufs = bufs[0:MOBA_UNROLL]
    p_bufs = bufs[MOBA_UNROLL:2 * MOBA_UNROLL]
    al_bufs = bufs[2 * MOBA_UNROLL:3 * MOBA_UNROLL]
    p = pl.program_id(0)
    g = pl.program_id(1)
    tq = qt_ref.shape[1]
    t = tq // 2

    @pl.when(g == 0)
    def _():
        r = lax.broadcasted_iota(I32, (t, t), 0)
        c = lax.broadcasted_iota(I32, (t, t), 1)
        zero = jnp.zeros((t, t), F32)
        for hd in range(2):
            h = 2 * p + hd
            far = rb_ref[(NUM_BUCKETS - 1) * N_HEADS_MOBA + h]
            tiles = []
            for w in range(2):
                bkt = bkt_ref[w]
                acc = jnp.zeros(bkt.shape, F32)
                for kk in range(NUM_BUCKETS):
                    acc = acc + jnp.where(bkt == kk, rb_ref[kk * N_HEADS_MOBA + h], 0.0)
                tiles.append((acc - far) * LOG2E)
            prev_t = tiles[0]
            own_t = jnp.where(r <= c, tiles[1], NEG)
            bias_scr[hd, 0] = jnp.concatenate([prev_t, zero], axis=1)
            bias_scr[hd, 1] = jnp.concatenate([own_t, prev_t], axis=1)
            bias_scr[hd, 2] = jnp.concatenate([zero, own_t], axis=1)

    qt = qt_ref[...]
    top = lax.broadcasted_iota(I32, qt.shape, 0) < HEAD_DIM
    zq = jnp.zeros_like(qt)
    qth = (jnp.where(top, qt, zq), jnp.where(top, zq, qt))
    m_ref[...] = jnp.full(m_ref.shape, M_INIT, F32)
    a_ref[...] = jnp.zeros(a_ref.shape, F32)

    ones_rows = jnp.where(lax.broadcasted_iota(I32, (SUM_ROWS, t), 0) == 0, 1.0, 0.0).astype(BF16)
    in_a = lax.broadcasted_iota(I32, (1, tq), 1) < t

    n_far = jnp.maximum(2 * g - 1, 0)

    n_key_tiles = k_ref.shape[0] // t

    def produce(j, s_buf):
        j = jnp.minimum(j, n_key_tiles - 1)
        k0 = pl.multiple_of(j * t, t)
        kt = k_ref[pl.ds(k0, t), :]
        for hd in range(2):
            s_buf[hd] = jnp.dot(kt, qth[hd], preferred_element_type=F32)

    def softmax(s_buf, p_buf, al_buf, selrows, w):
        for hd in range(2):
            s = s_buf[hd]
            if w is not None:
                s = s + bias_scr[hd, w]
            smax = jnp.max(s, axis=0, keepdims=True)
            m_old = m_ref[hd]
            m_new = jnp.maximum(m_old, smax + selrows[hd])
            shift = m_new - selrows[hd]
            al_buf[hd] = jnp.exp2(m_old - m_new)
            m_ref[hd] = m_new
            p_buf[hd * t:(hd + 1) * t, :] = jnp.exp2(s - shift).astype(BF16)

    def far_rows(j):
        return [jnp.where(j < n_far, sel_ref[hd, pl.ds(j, 1), :], NEG) for hd in range(2)]

    def accumulate(j, p_buf, al_buf):
        k0 = pl.multiple_of(j * t, t)
        for hd in range(2):
            vh = vt_ref[hd * HEAD_DIM:(hd + 1) * HEAD_DIM, pl.ds(k0, t)]
            vcat = jnp.concatenate([vh, ones_rows], axis=0)
            a_ref[hd] = (a_ref[hd] * al_buf[hd]
                         + jnp.dot(vcat, p_buf[hd * t:(hd + 1) * t, :], preferred_element_type=F32))

    un = MOBA_UNROLL
    lag = MOBA_LAG
    for n in range(un - lag, un):
        p_bufs[n][...] = jnp.zeros_like(p_bufs[n])
        al_bufs[n][...] = jnp.ones_like(al_bufs[n])
    produce(0, s_bufs[0])

    def body(u, carry):
        j0 = un * u
        for n in range(un):
            accumulate(jnp.maximum(j0 + n - lag, 0), p_bufs[(n - lag) % un], al_bufs[(n - lag) % un])
            softmax(s_bufs[n], p_bufs[n], al_bufs[n], far_rows(j0 + n), None)
            produce(j0 + n + 1, s_bufs[(n + 1) % un])
        return carry

    n_trips = (n_far + un - 1) // un
    lax.fori_loop(0, n_trips, body, 0)

    j1 = jnp.maximum(2 * g - 1, 0)
    j2 = 2 * g
    j3 = 2 * g + 1
    rows1 = [jnp.where(g >= 1, sel_ref[hd, pl.ds(j1, 1), :], NEG) for hd in range(2)]
    rows2 = [jnp.where(in_a, 0.0, sel_ref[hd, pl.ds(j2, 1), :]) for hd in range(2)]
    rows3 = [jnp.where(in_a, NEG, 0.0)] * 2
    produce(j1, s_bufs[0])
    produce(j2, s_bufs[1])
    for n in range(un - lag, un):
        accumulate(jnp.maximum(un * n_trips - un + n, 0), p_bufs[n], al_bufs[n])
    softmax(s_bufs[0], p_bufs[0], al_bufs[0], rows1, 0)
    produce(j3, s_bufs[0])
    softmax(s_bufs[1], p_bufs[1], al_bufs[1], rows2, 1)
    accumulate(j1, p_bufs[0], al_bufs[0])
    softmax(s_bufs[0], p_bufs[2], al_bufs[2], rows3, 2)
    accumulate(j2, p_bufs[1], al_bufs[1])
    accumulate(j3, p_bufs[2], al_bufs[2])
    out_t = jnp.concatenate([a_ref[hd, 0:HEAD_DIM, :] / a_ref[hd, HEAD_DIM:HEAD_DIM + 1, :]
                             for hd in range(2)], axis=0)
    o_ref[...] = out_t.T.astype(o_ref.dtype)


def _moba(qkv, tr, sel, rel_bias):
    s = qkv.shape[0]
    t = MOBA_BLOCK
    npair = N_HEADS_MOBA // 2
    nblk = s // t
    kcol = N_HEADS_FOX // 2
    a = np.arange(t)[None, :]
    b = np.arange(t)[:, None]
    bkt = jnp.asarray(np.stack([_t5_bucket_np(t + a - b), _t5_bucket_np(a - b)]))
    tq = 2 * t
    grid_spec = pltpu.PrefetchScalarGridSpec(
        num_scalar_prefetch=1,
        grid=(npair, s // tq),
        in_specs=[pl.BlockSpec((PAIR, tq), lambda p, i, rb: (p, i)),
                  pl.BlockSpec((s, PAIR), lambda p, i, rb: (0, kcol + p)),
                  pl.BlockSpec((PAIR, s), lambda p, i, rb: (npair + p, 0)),
                  pl.BlockSpec((2, nblk, tq), lambda p, i, rb: (p, 0, i)),
                  pl.BlockSpec((2, t, t), lambda p, i, rb: (0, 0, 0))],
        out_specs=pl.BlockSpec((tq, PAIR), lambda p, i, rb: (i, p)),
        scratch_shapes=[pltpu.VMEM((2, 1, tq), F32),
                        pltpu.VMEM((2, HEAD_DIM + SUM_ROWS, tq), F32), pltpu.VMEM((2, 3, t, tq), F32)]
        + [pltpu.VMEM((2, t, tq), F32)] * MOBA_UNROLL
        + [pltpu.VMEM((2 * t, tq), BF16)] * MOBA_UNROLL
        + [pltpu.VMEM((2, 1, tq), F32)] * MOBA_UNROLL,
    )
    return pl.pallas_call(
        _moba_kernel,
        out_shape=jax.ShapeDtypeStruct((s, npair * PAIR), BF16),
        grid_spec=grid_spec,
        compiler_params=_cparams(("arbitrary", "arbitrary")),
        name="moba",
    )(rel_bias.reshape(-1), tr, qkv, tr, sel, bkt)


def _store_token_tiles(ref, val):
    n = val.shape[0]
    for c in range(SUBLANES):
        ref[pl.ds(c, n, stride=SUBLANES), :] = val[:, c * LANES:(c + 1) * LANES]


def _load_token_tiles(ref, n):
    return jnp.concatenate([ref[pl.ds(c, n, stride=SUBLANES), :] for c in range(SUBLANES)], axis=1)


def _tile_rows(r0, n):
    start = r0 * SUBLANES
    if not isinstance(start, int):
        start = pl.multiple_of(start, SUBLANES)
    return pl.ds(start, n * SUBLANES)


def _outproj_kernel(ya_ref, yb_ref, x_ref, wa_ref, wb_ref, gpost_ref, gt_ref, gpre_ref,
                    sc_ref, sh_ref, wr_ref, br_ref,
                    x1_ref, h2_ref, route_ref, gate_ref, cnt_ref, carry_scr):
    i = pl.program_id(0)
    tm = x_ref.shape[0]

    @pl.when(i == 0)
    def _():
        carry_scr[...] = jnp.zeros_like(carry_scr)

    y = (jnp.dot(ya_ref[...], wa_ref[...], preferred_element_type=F32)
         + jnp.dot(yb_ref[...], wb_ref[...], preferred_element_type=F32))
    x1 = x_ref[...] + gt_ref[...] * (_rms(y) * gpost_ref[...])
    x1_ref[...] = x1
    h2 = _rms(x1) * gpre_ref[...] * (1.0 + sc_ref[...]) + sh_ref[...]
    _store_token_tiles(h2_ref, h2)

    h_hi = h2.astype(BF16)
    h_lo = (h2 - h_hi.astype(F32)).astype(BF16)
    logits = (jnp.dot(h_hi, wr_ref[0], preferred_element_type=F32)
              + jnp.dot(h_hi, wr_ref[1], preferred_element_type=F32)
              + jnp.dot(h_lo, wr_ref[0], preferred_element_type=F32)) + br_ref[...]
    ne = cnt_ref.shape[0]
    g = logits.T[0:ne, :]
    expert = lax.broadcasted_iota(I32, g.shape, 0)
    mask = jnp.zeros(g.shape, F32)
    vals, firsts, picks = [], [], []
    for _ in range(TOP_K):
        m = jnp.max(g, axis=0, keepdims=True)
        first = jnp.min(jnp.where(g == m, expert, ne), axis=0, keepdims=True)
        pick = expert == first
        mask = jnp.where(pick, 1.0, mask)
        g = jnp.where(pick, -jnp.inf, g)
        vals.append(m)
        firsts.append(first)
        picks.append(pick)
    ex = [jnp.exp(v - vals[0]) for v in vals]
    den = ex[0] + ex[1] + ex[2] + ex[3]
    gates = [e / den for e in ex]

    r = lax.broadcasted_iota(I32, (tm, tm), 0)
    c = lax.broadcasted_iota(I32, (tm, tm), 1)
    earlier = jnp.where(r < c, 1.0, 0.0).astype(BF16)
    before = jnp.dot(mask.astype(BF16), earlier, preferred_element_type=F32) + carry_scr[...]
    total = carry_scr[...] + jnp.sum(mask, axis=1, keepdims=True)
    carry_scr[...] = total
    cnt_ref[...] = jnp.broadcast_to(total, cnt_ref.shape)

    ranks = [jnp.sum(jnp.where(pk, before, 0.0), axis=0, keepdims=True).astype(I32) for pk in picks]
    route_ref[...] = jnp.concatenate(firsts + ranks, axis=0)
    gate_rows = jnp.concatenate(gates + [jnp.zeros((LANES - TOP_K, tm), F32)], axis=0)
    gate_ref[...] = gate_rows.T[:, 0:TOP_K]


def _outproj(mix_a, mix_b, x2, w_a, w_b, gpost, gt, gpre, sc, sh, w_router, b_router):
    s, d = x2.shape
    tm = ROW_TILE
    ne = w_router.shape[1]
    wa = mix_a.shape[1]
    w_router = jnp.pad(w_router, ((0, 0), (0, LANES - ne)))
    b_router = jnp.pad(b_router, ((0, 0), (0, LANES - ne)))
    w_hi = w_router.astype(BF16)
    w_router_hl = jnp.stack([w_hi, (w_router - w_hi.astype(F32)).astype(BF16)])
    row = lambda i: (i, 0)
    fix = lambda i: (0, 0)
    vec = pl.BlockSpec((1, d), fix)
    return pl.pallas_call(
        _outproj_kernel,
        out_shape=(jax.ShapeDtypeStruct((s, d), F32),
                   jax.ShapeDtypeStruct((s * SUBLANES, LANES), F32),
                   jax.ShapeDtypeStruct((2 * TOP_K, s), I32),
                   jax.ShapeDtypeStruct((s, TOP_K), F32),
                   jax.ShapeDtypeStruct((ne, LANES), F32)),
        grid=(s // tm,),
        in_specs=[pl.BlockSpec((tm, wa), row), pl.BlockSpec((tm, wa), row), pl.BlockSpec((tm, d), row),
                  pl.BlockSpec((wa, d), fix), pl.BlockSpec((wa, d), fix),
                  vec, vec, vec, vec, vec,
                  pl.BlockSpec((2, d, LANES), lambda i: (0, 0, 0)), pl.BlockSpec((1, LANES), fix)],
        out_specs=(pl.BlockSpec((tm, d), row), pl.BlockSpec((tm * SUBLANES, LANES), row),
                   pl.BlockSpec((2 * TOP_K, tm), lambda i: (0, i)),
                   pl.BlockSpec((tm, TOP_K), row), pl.BlockSpec((ne, LANES), fix)),
        scratch_shapes=[pltpu.VMEM((ne, 1), F32)],
        compiler_params=_cparams(("arbitrary",)),
        name="outproj",
    )(mix_a, mix_b, x2, w_a, w_b, gpost, gt, gpre, sc, sh, w_router_hl, b_router)


def _dest_kernel(route_ref, pstart_ref, o_ref):
    route = route_ref[...]
    tm = route.shape[1]
    ne = pstart_ref.shape[0]
    expert = lax.broadcasted_iota(I32, (ne, tm), 0)
    for k in range(TOP_K):
        start = jnp.sum(jnp.where(expert == route[k:k + 1, :], pstart_ref[...], 0), axis=0, keepdims=True)
        row = start + route[TOP_K + k:TOP_K + k + 1, :]
        for g in range(tm // LANES):
            o_ref[g * TOP_K + k:g * TOP_K + k + 1, :] = row[:, g * LANES:(g + 1) * LANES]


def _dest(route, pstart):
    s = route.shape[1]
    tm = min(4 * ROW_TILE, s)
    return pl.pallas_call(
        _dest_kernel,
        out_shape=jax.ShapeDtypeStruct((s // LANES * TOP_K, LANES), I32),
        grid=(s // tm,),
        in_specs=[pl.BlockSpec((2 * TOP_K, tm), lambda i: (0, i)),
                  pl.BlockSpec((N_EXPERTS, 1), lambda i: (0, 0))],
        out_specs=pl.BlockSpec((tm // LANES * TOP_K, LANES), lambda i: (i, 0)),
        compiler_params=_cparams(("arbitrary",)),
        name="dest",
    )(route, pstart.reshape(-1, 1))


def _dispatch_kernel(pstart_ref, pblk_ref, nu_ref, dest_ref, h_ref, xs_ref,
                     zero_scr, sem, zsem):
    tm = h_ref.shape[0] // SUBLANES
    bm = zero_scr.shape[0] // SUBLANES
    n_blk = xs_ref.shape[0] // (bm * SUBLANES)

    rows = _tile_rows

    @pl.when(pl.program_id(0) == 0)
    def _():
        zero_scr[...] = jnp.zeros_like(zero_scr)

        def zero_copy(row0):
            return pltpu.make_async_copy(zero_scr, xs_ref.at[rows(row0, bm)], zsem)

        for phase in range(2):
            for e in range(N_EXPERTS):
                last = pstart_ref[e] + (pblk_ref[e] - 1) * bm
                tail = (n_blk - N_EXPERTS + e) * bm
                for cond, row0 in ((pblk_ref[e] > 0, last), (n_blk - N_EXPERTS + e >= nu_ref[0], tail)):
                    @pl.when(cond)
                    def _():
                        if phase == 0:
                            zero_copy(row0).start()
                        else:
                            zero_copy(row0).wait()

    def start_group(g, carry):
        for lane in range(LANES):
            r = g * LANES + lane
            for k in range(TOP_K):
                dst = dest_ref[g * TOP_K + k, lane]
                pltpu.make_async_copy(h_ref.at[rows(r, 1)], xs_ref.at[rows(dst, 1)], sem).start(priority=k % 2)
        return carry

    lax.fori_loop(0, tm // LANES, start_group, 0)
    for k in range(TOP_K):
        pltpu.make_async_copy(h_ref, xs_ref.at[rows(0, tm)], sem).wait()


def _dispatch(pstart, pblk, n_used, dest, h2, n_rows):
    s = h2.shape[0] // SUBLANES
    tm = COMBINE_TILE
    grid_spec = pltpu.PrefetchScalarGridSpec(
        num_scalar_prefetch=3,
        grid=(s // tm,),
        in_specs=[pl.BlockSpec((tm // LANES * TOP_K, LANES), lambda i, *_: (i, 0), memory_space=pltpu.SMEM),
                  pl.BlockSpec((tm * SUBLANES, LANES), lambda i, *_: (i, 0))],
        out_specs=pl.BlockSpec(memory_space=pl.ANY),
        scratch_shapes=[pltpu.VMEM((EXPERT_BLOCK * SUBLANES, LANES), F32), pltpu.SemaphoreType.DMA,
                        pltpu.SemaphoreType.DMA],
    )
    return pl.pallas_call(
        _dispatch_kernel,
        out_shape=jax.ShapeDtypeStruct((n_rows * SUBLANES, LANES), F32),
        grid_spec=grid_spec,
        compiler_params=_cparams(("arbitrary",)),
        name="dispatch",
    )(pstart, pblk, n_used, dest, h2)


def _experts_kernel(be_ref, nxt_ref, nu_ref, xs_ref, wgu_hbm, bgu_ref, wd_hbm, bd_ref, y_ref,
                    wgu_f32, wd_f32, wgu_bf, wd_bf, sem):
    b = pl.program_id(0)
    d_exp = wd_hbm.shape[1]
    prev = be_ref[jnp.maximum(b - 1, 0)]
    changed = (b == 0) | (be_ref[b] != prev)

    def fetch(e):
        return (pltpu.make_async_copy(wgu_hbm.at[e], wgu_f32, sem.at[0]),
                pltpu.make_async_copy(wd_hbm.at[e], wd_f32, sem.at[1]))

    @pl.when(b == 0)
    def _():
        for copy in fetch(be_ref[0]):
            copy.start()

    @pl.when((b < nu_ref[0]) & changed)
    def _():
        for copy in fetch(be_ref[b]):
            copy.wait()
        rows = LANES

        def cast_gu(c, carry):
            r0 = pl.multiple_of(c * rows, rows)
            wgu_bf[pl.ds(r0, rows), :] = wgu_f32[pl.ds(r0, rows), :].astype(BF16)
            return carry

        def cast_d(c, carry):
            r0 = pl.multiple_of(c * rows, rows)
            wd_bf[pl.ds(r0, rows), :] = wd_f32[pl.ds(r0, rows), :].astype(BF16)
            return carry

        lax.fori_loop(0, wgu_f32.shape[0] // rows, cast_gu, 0)
        lax.fori_loop(0, wd_f32.shape[0] // rows, cast_d, 0)

        @pl.when(nxt_ref[b] >= 0)
        def _():
            for copy in fetch(nxt_ref[b]):
                copy.start()

    @pl.when(b < nu_ref[0])
    def _():
        bm = xs_ref.shape[0] // SUBLANES
        xb = _load_token_tiles(xs_ref, bm).astype(BF16)
        hdn = jnp.dot(xb, wgu_bf[...], preferred_element_type=F32) + bgu_ref[0]
        x_glu = jnp.minimum(hdn[:, :d_exp], SWIGLU_LIMIT)
        x_lin = jnp.clip(hdn[:, d_exp:], -SWIGLU_LIMIT, SWIGLU_LIMIT)
        act = x_glu * jax.nn.sigmoid(SWIGLU_ALPHA * x_glu) * (x_lin + 1.0)
        _store_token_tiles(y_ref, jnp.dot(act.astype(BF16), wd_bf[...], preferred_element_type=F32)
                           + bd_ref[0])

    @pl.when(b >= nu_ref[0])
    def _():
        y_ref[...] = jnp.zeros_like(y_ref)


def _experts(block_e, next_e, n_used, xs, w_gate_up, b_gate_up, w_down, b_down):
    n_rows = xs.shape[0] // SUBLANES
    bm = EXPERT_BLOCK
    n_blk = n_rows // bm
    ne, d, two_de = w_gate_up.shape
    de = w_down.shape[1]
    assert d == SUBLANES * LANES

    def blk(b, nu):
        return jnp.minimum(b, nu[0] - 1)

    grid_spec = pltpu.PrefetchScalarGridSpec(
        num_scalar_prefetch=3,
        grid=(n_blk,),
        in_specs=[pl.BlockSpec((bm * SUBLANES, LANES), lambda b, be, nxt, nu: (blk(b, nu), 0)),
                  pl.BlockSpec(memory_space=pl.ANY),
                  pl.BlockSpec((1, 1, two_de), lambda b, be, nxt, nu: (be[blk(b, nu)], 0, 0)),
                  pl.BlockSpec(memory_space=pl.ANY),
                  pl.BlockSpec((1, 1, d), lambda b, be, nxt, nu: (be[blk(b, nu)], 0, 0))],
        out_specs=pl.BlockSpec((bm * SUBLANES, LANES), lambda b, be, nxt, nu: (b, 0)),
        scratch_shapes=[pltpu.VMEM((d, two_de), F32), pltpu.VMEM((de, d), F32),
                        pltpu.VMEM((d, two_de), BF16), pltpu.VMEM((de, d), BF16),
                        pltpu.SemaphoreType.DMA((2,))],
    )
    return pl.pallas_call(
        _experts_kernel,
        out_shape=jax.ShapeDtypeStruct((n_rows * SUBLANES, LANES), F32),
        grid_spec=grid_spec,
        compiler_params=_cparams(("arbitrary",)),
        name="experts",
    )(block_e, next_e, n_used, xs, w_gate_up, b_gate_up.reshape(ne, 1, two_de), w_down,
      b_down.reshape(ne, 1, d))


def _combine_kernel(dest_ref, dest_next_ref, y_ref, gate_ref, x1_ref, gt_ref, gpost_ref, o_ref,
                    buf_even, buf_odd, sem):
    i = pl.program_id(0)
    tm = x1_ref.shape[0]
    ch = COMBINE_CHUNK
    bufs = (buf_even, buf_odd)

    def start_chunk(d_ref, slot, c):
        for lane in range(ch):
            r = c * ch + lane
            for k in range(TOP_K):
                pltpu.make_async_copy(y_ref.at[_tile_rows(d_ref[c * TOP_K + k, lane], 1)],
                                      bufs[slot].at[k, _tile_rows(r, 1)], sem.at[slot, c]).start(priority=k % 2)

    @pl.when(i == 0)
    def _():
        def first(c, carry):
            start_chunk(dest_ref, 0, c)
            return carry

        lax.fori_loop(0, tm // ch, first, 0)

    def reduce_chunk(c, cur, gather_next):
        buf = bufs[cur]
        for k in range(TOP_K):
            pltpu.make_async_copy(y_ref.at[_tile_rows(0, ch)], buf.at[k, _tile_rows(c * ch, ch)],
                                  sem.at[cur, c]).wait()
        if gather_next:
            start_chunk(dest_next_ref, 1 - cur, c)
        n = COMBINE_REDUCE_ROWS
        for part in range(ch // n):
            r0 = pl.multiple_of(c * ch + part * n, n)
            rows = pl.ds(r0, n)
            gate = gate_ref[rows, :]
            acc = gate[:, 0:1] * _load_token_tiles(buf.at[0, _tile_rows(r0, n)], n)
            for k in range(1, TOP_K):
                acc = acc + gate[:, k:k + 1] * _load_token_tiles(buf.at[k, _tile_rows(r0, n)], n)
            o_ref[rows, :] = x1_ref[rows, :] + gt_ref[...] * (_rms(acc) * gpost_ref[...])

    has_next = i + 1 < pl.num_programs(0)
    for cur in range(2):
        for gather_next in (True, False):
            @pl.when((i % 2 == cur) & (has_next == gather_next))
            def _(cur=cur, gather_next=gather_next):
                def body(c, carry):
                    reduce_chunk(c, cur, gather_next)
                    return carry

                lax.fori_loop(0, tm // ch, body, 0)


def _combine(dest, y, gate4, x1, gt, gpost):
    s, d = x1.shape
    tm = COMBINE_TILE // 2
    n = s // tm
    return pl.pallas_call(
        _combine_kernel,
        out_shape=jax.ShapeDtypeStruct((s, d), F32),
        grid=(n,),
        in_specs=[pl.BlockSpec((tm // LANES * TOP_K, LANES), lambda i: (i, 0), memory_space=pltpu.SMEM),
                  pl.BlockSpec((tm // LANES * TOP_K, LANES), lambda i: (jnp.minimum(i + 1, n - 1), 0),
                               memory_space=pltpu.SMEM),
                  pl.BlockSpec(memory_space=pl.ANY),
                  pl.BlockSpec((tm, TOP_K), lambda i: (i, 0)),
                  pl.BlockSpec((tm, d), lambda i: (i, 0)),
                  pl.BlockSpec((1, d), lambda i: (0, 0)),
                  pl.BlockSpec((1, d), lambda i: (0, 0))],
        out_specs=pl.BlockSpec((tm, d), lambda i: (i, 0)),
        scratch_shapes=[pltpu.VMEM((TOP_K, tm * SUBLANES, LANES), F32),
                        pltpu.VMEM((TOP_K, tm * SUBLANES, LANES), F32),
                        pltpu.SemaphoreType.DMA((2, tm // COMBINE_CHUNK))],
        compiler_params=_cparams(("arbitrary",)),
        name="combine",
    )(dest, dest, y, gate4, x1, gt, gpost)


def _layer(x2, mod, g_pre_mix, g_post_mix, w_in, b_forget, rel_bias, w_out,
           g_pre_ffn, g_post_ffn, w_router, b_router, w_gate_up, b_gate_up, w_down, b_down):
    s, d = x2.shape
    sh_m, sc_m, gt_m, sh_f, sc_f, gt_f = [mod[:, k * d:(k + 1) * d] for k in range(6)]
    n_qkv = 3 * (N_HEADS_FOX + N_HEADS_MOBA) * HEAD_DIM
    fox_w = N_HEADS_FOX * HEAD_DIM

    w_qkv = w_in[:, :n_qkv].astype(BF16)
    w_f = w_in[:, n_qkv:].T
    qkv, cum, sel, nrm, tr, nr = _inproj(x2, g_pre_mix.reshape(1, d), sc_m, sh_m, w_qkv, w_f,
                                         b_forget.reshape(-1, 1))
    y_a = _fox(qkv, tr, nr, cum, nrm)
    y_b = _moba(qkv, tr, sel, rel_bias)

    w_out_bf = w_out.astype(BF16)
    x1, h2, route, gate4, cnt = _outproj(
        y_a, y_b, x2, w_out_bf[:fox_w], w_out_bf[fox_w:], g_post_mix.reshape(1, d), gt_m,
        g_pre_ffn.reshape(1, d), sc_f, sh_f, w_router, b_router.reshape(1, -1))

    bm = EXPERT_BLOCK
    counts = cnt[:, 0].astype(I32)
    pblk = (counts + bm - 1) // bm
    pend_blk = jnp.cumsum(pblk)
    pstart = ((pend_blk - pblk) * bm).astype(I32)
    n_rows = s * TOP_K + N_EXPERTS * bm
    n_blk = n_rows // bm
    block_e = jnp.minimum(jnp.sum(pend_blk[None, :] <= jnp.arange(n_blk)[:, None], axis=1),
                          N_EXPERTS - 1).astype(I32)
    n_used = pend_blk[-1:].astype(I32)
    region_end = pend_blk[block_e]
    next_e = jnp.where(region_end < n_used, block_e[jnp.minimum(region_end, n_blk - 1)], -1).astype(I32)

    dest = _dest(route, pstart)
    xs = _dispatch(pstart, pblk.astype(I32), n_used, dest, h2, n_rows)
    y = _experts(block_e, next_e, n_used, xs, w_gate_up, b_gate_up, w_down, b_down)
    return _combine(dest, y, gate4, x1, gt_f, g_post_ffn.reshape(1, d))


def kernel(x, c, w_ada, b_ada, g_pre_mix, g_post_mix, w_in, b_forget, rel_bias, w_out, g_pre_ffn, g_post_ffn, w_router, b_router, w_gate_up, b_gate_up, w_down, b_down):
    bsz, s, d = x.shape
    depth = w_ada.shape[0]
    outs = []
    for bi in range(bsz):
        x2 = x[bi]
        for l in range(depth):
            mod = _adaln(c[bi:bi + 1], w_ada[l], b_ada[l])
            x2 = _layer(x2, mod, g_pre_mix[l], g_post_mix[l], w_in[l], b_forget[l], rel_bias, w_out[l],
                        g_pre_ffn[l], g_post_ffn[l], w_router[l], b_router[l], w_gate_up[l], b_gate_up[l],
                        w_down[l], b_down[l])
        outs.append(x2)
    return outs[0].reshape(1, s, d) if bsz == 1 else jnp.stack(outs)
```

```python
import math

import numpy as np
import jax
import jax.numpy as jnp
from jax import lax
from jax.experimental import pallas as pl
from jax.experimental.pallas import tpu as pltpu

F32 = jnp.float32
BF16 = jnp.bfloat16
I32 = jnp.int32

HEAD_DIM = 64
N_HEADS_FOX = 8
N_HEADS_MOBA = 8
PAIR = 2 * HEAD_DIM
MOBA_BLOCK = 256
MOBA_TOPK = 3
NUM_BUCKETS = 32
MAX_DISTANCE = 128
N_EXPERTS = 32
TOP_K = 4
SWIGLU_LIMIT = 7.0
SWIGLU_ALPHA = 1.702
RMS_EPS = 1e-6
NEG = -(2.0 ** 100)
M_INIT = -(2.0 ** 99)
LOG2E = math.log2(math.e)
SUM_ROWS = 16
SUBLANES = 8
LANES = 128
EXP_UNDERFLOW = 90.0
VMEM_LIMIT = 56 * 1024 * 1024

ROW_TILE = 512
FOX_TILE = 256
EXPERT_BLOCK = 512
COMBINE_TILE = 1024
COMBINE_CHUNK = LANES
COMBINE_REDUCE_ROWS = 64
MOBA_LAG = 1
MOBA_UNROLL = 4

NT_DIMS = (((1,), (1,)), ((), ()))


def _cparams(sem):
    return pltpu.CompilerParams(dimension_semantics=sem, vmem_limit_bytes=VMEM_LIMIT)


def _rms(x):
    return x * lax.rsqrt(jnp.mean(x * x, axis=-1, keepdims=True) + RMS_EPS)


def _adaln_kernel(c_ref, w_ref, b_ref, o_ref):
    c = c_ref[...]
    cond = c * jax.nn.sigmoid(c)
    o_ref[...] = jnp.dot(cond, w_ref[...], preferred_element_type=F32,
                         precision=lax.Precision.HIGHEST) + b_ref[...]


def _adaln(c, w_ada, b_ada):
    d = c.shape[-1]
    n = w_ada.shape[-1]
    c8 = jnp.broadcast_to(c.reshape(1, d), (8, d))
    out = pl.pallas_call(
        _adaln_kernel,
        out_shape=jax.ShapeDtypeStruct((8, n), F32),
        grid=(n // d,),
        in_specs=[pl.BlockSpec((8, d), lambda j: (0, 0)),
                  pl.BlockSpec((d, d), lambda j: (0, j)),
                  pl.BlockSpec((1, d), lambda j: (0, j))],
        out_specs=pl.BlockSpec((8, d), lambda j: (0, j)),
        compiler_params=_cparams(("arbitrary",)),
        name="adaln",
    )(c8, w_ada, b_ada.reshape(1, n))
    return out[0:1]


def _inproj_kernel(x_ref, g_ref, sc_ref, sh_ref, w_ref, wf_ref, bf_ref,
                   qkv_ref, cum_ref, sel_ref, nrm_ref, tr_ref, nr_ref, km_scr, carry_scr):
    i = pl.program_id(0)
    tm = x_ref.shape[0]
    nblk = km_scr.shape[0]

    @pl.when(i == 0)
    def _():
        km_scr[...] = jnp.zeros_like(km_scr)
        carry_scr[...] = jnp.zeros_like(carry_scr)

    x = x_ref[...]
    h = _rms(x) * g_ref[...] * (1.0 + sc_ref[...]) + sh_ref[...]
    hb = h.astype(BF16)

    width = N_HEADS_FOX * HEAD_DIM
    hsel = jnp.where(lax.broadcasted_iota(I32, (width, N_HEADS_FOX), 0) // HEAD_DIM
                     == lax.broadcasted_iota(I32, (width, N_HEADS_FOX), 1), 1.0, 0.0)
    kb = None
    tr_slot = {3: 0, 5: 1, 0: 2, 2: 3}
    qbt = None
    for c in range(6):
        pc = jnp.dot(hb, w_ref[:, c * width:(c + 1) * width], preferred_element_type=F32)
        if c in tr_slot:
            n = tr_slot[c]
            pct = pc.T
            if c == 3:
                qbt = pct
            if c == 0 or c == 3:
                pct = pct * (LOG2E * HEAD_DIM ** -0.5)
            tr_ref[n * width:(n + 1) * width, :] = pct.astype(BF16)
        if c == 1 or c == 4:
            qkv_ref[:, (c // 3) * width:(c // 3 + 1) * width] = pc.astype(BF16)
        if c < 2:
            sq = (pc * (HEAD_DIM ** -0.5) if c == 0 else pc).astype(BF16).astype(F32)
            n2 = jnp.dot((sq * sq).astype(BF16), hsel.astype(BF16), preferred_element_type=F32)
            nrm_ref[0, c:c + 1, :] = jnp.max(n2, axis=0, keepdims=True)
        if c == 4:
            kb = pc

    ft = lax.dot_general(wf_ref[...].astype(BF16), hb, NT_DIMS, preferred_element_type=F32)
    z = ft + bf_ref[...]
    logf = -(jnp.maximum(-z, 0.0) + jnp.log1p(jnp.exp(-jnp.abs(z))))
    lane = lax.broadcasted_iota(I32, logf.shape, 1)
    cs = logf
    sh = 1
    while sh < tm:
        cs = cs + jnp.where(lane >= sh, pltpu.roll(cs, sh, axis=1), 0.0)
        sh *= 2
    base = jnp.zeros_like(cs)
    for b in range(1, tm // FOX_TILE):
        base = jnp.where(lane >= b * FOX_TILE, cs[:, b * FOX_TILE - 1:b * FOX_TILE], base)
    nr_t = (-LOG2E * (cs - base)).T
    for pp in range(nr_ref.shape[0]):
        nr_ref[pp] = nr_t[:, 2 * pp:2 * pp + 2]
    cs = cs + carry_scr[...]
    cum_ref[...] = cs
    carry_scr[...] = cs[:, tm - 1:tm]

    nb_tile = tm // MOBA_BLOCK
    for b in range(nb_tile):
        kmean = jnp.sum(kb[b * MOBA_BLOCK:(b + 1) * MOBA_BLOCK], axis=0, keepdims=True) * (1.0 / MOBA_BLOCK)
        km_scr[pl.ds(i * nb_tile + b, 1), :] = kmean

    km = km_scr[...]
    blk = lax.broadcasted_iota(I32, (nblk, tm), 0)
    col = lax.broadcasted_iota(I32, (nblk, tm), 1)
    own = i * nb_tile + col // MOBA_BLOCK
    for hd in range(N_HEADS_MOBA):
        hs = slice(hd * HEAD_DIM, (hd + 1) * HEAD_DIM)
        g = jnp.dot(km[:, hs], qbt[hs, :], preferred_element_type=F32,
                    precision=lax.Precision.HIGHEST)
        g = jnp.where(blk < own, g, -jnp.inf)
        sel = jnp.zeros(g.shape, dtype=jnp.bool_)
        for _ in range(MOBA_TOPK):
            m = jnp.max(g, axis=0, keepdims=True)
            first = jnp.min(jnp.where(g == m, blk, nblk), axis=0, keepdims=True)
            pick = (blk == first) & (m > -jnp.inf)
            sel = sel | pick
            g = jnp.where(pick, -jnp.inf, g)
        sel_ref[hd] = jnp.where(sel, 0.0, NEG)


def _inproj(x2, g, sc, sh, w_qkv, w_f, b_f):
    s, d = x2.shape
    tm = ROW_TILE
    nblk = s // MOBA_BLOCK
    n = w_qkv.shape[1]
    width = N_HEADS_FOX * HEAD_DIM
    n_rows_out = 2 * width
    n_tr = 4 * width
    assert tm % FOX_TILE == 0 and N_HEADS_FOX == N_HEADS_MOBA
    return pl.pallas_call(
        _inproj_kernel,
        out_shape=(jax.ShapeDtypeStruct((s, n_rows_out), BF16),
                   jax.ShapeDtypeStruct((N_HEADS_FOX, s), F32),
                   jax.ShapeDtypeStruct((N_HEADS_MOBA, nblk, s), F32),
                   jax.ShapeDtypeStruct((s // tm, 2, N_HEADS_FOX), F32),
                   jax.ShapeDtypeStruct((n_tr, s), BF16),
                   jax.ShapeDtypeStruct((N_HEADS_FOX // 2, s, 2), F32)),
        grid=(s // tm,),
        in_specs=[pl.BlockSpec((tm, d), lambda i: (i, 0)),
                  pl.BlockSpec((1, d), lambda i: (0, 0)),
                  pl.BlockSpec((1, d), lambda i: (0, 0)),
                  pl.BlockSpec((1, d), lambda i: (0, 0)),
                  pl.BlockSpec((d, n), lambda i: (0, 0)),
                  pl.BlockSpec((N_HEADS_FOX, d), lambda i: (0, 0)),
                  pl.BlockSpec((N_HEADS_FOX, 1), lambda i: (0, 0))],
        out_specs=(pl.BlockSpec((tm, n_rows_out), lambda i: (i, 0)),
                   pl.BlockSpec((N_HEADS_FOX, tm), lambda i: (0, i)),
                   pl.BlockSpec((N_HEADS_MOBA, nblk, tm), lambda i: (0, 0, i)),
                   pl.BlockSpec((1, 2, N_HEADS_FOX), lambda i: (i, 0, 0)),
                   pl.BlockSpec((n_tr, tm), lambda i: (0, i)),
                   pl.BlockSpec((N_HEADS_FOX // 2, tm, 2), lambda i: (0, i, 0))),
        scratch_shapes=[pltpu.VMEM((nblk, N_HEADS_MOBA * HEAD_DIM), F32),
                        pltpu.VMEM((N_HEADS_FOX, 1), F32)],
        compiler_params=_cparams(("arbitrary",)),
        name="inproj",
    )(x2, g, sc, sh, w_qkv, w_f, b_f)


def _fox_kernel(jlo_ref, cp_ref, qt_ref, k_ref, vt_ref, nr_ref, o_ref, m_ref, a_ref):
    p = pl.program_id(0)
    g = pl.program_id(1)
    j_first = jlo_ref[p * pl.num_programs(1) + g]
    tq = qt_ref.shape[1]
    t = tq // 2

    m_ref[...] = jnp.full(m_ref.shape, M_INIT, F32)
    a_ref[...] = jnp.zeros(a_ref.shape, F32)
    ones_rows = jnp.where(lax.broadcasted_iota(I32, (SUM_ROWS, t), 0) == 0, 1.0, 0.0).astype(BF16)
    in_a = lax.broadcasted_iota(I32, (1, tq), 1) < t
    krow = lax.broadcasted_iota(I32, (t, tq), 0)
    qcol = lax.broadcasted_iota(I32, (t, tq), 1)
    future_a = (qcol < t) & (krow > qcol)
    future_b = (qcol >= t) & (krow > qcol - t)

    def scores(j, future):
        k0 = pl.multiple_of(j * t, t)
        out = []
        for hd in range(2):
            kh = k_ref[pl.ds(k0, t), hd * HEAD_DIM:(hd + 1) * HEAD_DIM]
            qh = qt_ref[hd * HEAD_DIM:(hd + 1) * HEAD_DIM, :]
            s = jnp.dot(kh, qh, preferred_element_type=F32) + nr_ref[0, pl.ds(k0, t), hd:hd + 1]
            out.append(s if future is None else jnp.where(future, NEG, s))
        return out

    def shifts(j, a_on, b_on):
        rows = []
        for hd in range(2):
            h = 2 * p + hd
            sa = LOG2E * (cp_ref[h, 2 * g] - cp_ref[h, j])
            sb = LOG2E * (cp_ref[h, 2 * g + 1] - cp_ref[h, j])
            rows.append(jnp.where(in_a, jnp.where(a_on, sa, NEG), jnp.where(b_on, sb, NEG)))
        return rows

    def softmax(ss, shift_rows):
        ps, alphas = [], []
        for hd in range(2):
            m_old = m_ref[hd]
            m_new = jnp.maximum(m_old, jnp.max(ss[hd], axis=0, keepdims=True) + shift_rows[hd])
            alphas.append(jnp.exp2(m_old - m_new))
            m_ref[hd] = m_new
            ps.append(jnp.exp2(ss[hd] - (m_new - shift_rows[hd])).astype(BF16))
        return ps, alphas

    def accumulate(j, ps, alphas):
        k0 = pl.multiple_of(j * t, t)
        for hd in range(2):
            vh = vt_ref[hd * HEAD_DIM:(hd + 1) * HEAD_DIM, pl.ds(k0, t)]
            vcat = jnp.concatenate([vh, ones_rows], axis=0)
            a_ref[hd] = a_ref[hd] * alphas[hd] + jnp.dot(vcat, ps[hd], preferred_element_type=F32)

    def body(j, carry):
        accumulate(j, *softmax(scores(j, None), shifts(j, True, True)))
        return carry

    lax.fori_loop(j_first, 2 * g - 1, body, 0)
    j1 = jnp.maximum(2 * g - 1, 0)
    j2 = 2 * g
    j3 = 2 * g + 1
    ss1 = scores(j1, None)
    ss2 = scores(j2, future_a)
    ss3 = scores(j3, future_b)
    w1 = softmax(ss1, shifts(j1, g >= 1, g >= 1))
    w2 = softmax(ss2, shifts(j2, True, True))
    accumulate(j1, *w1)
    w3 = softmax(ss3, shifts(j3, False, True))
    accumulate(j2, *w2)
    accumulate(j3, *w3)
    out_t = jnp.concatenate([a_ref[hd, 0:HEAD_DIM, :] / a_ref[hd, HEAD_DIM:HEAD_DIM + 1, :]
                             for hd in range(2)], axis=0)
    o_ref[...] = out_t.T.astype(o_ref.dtype)


def _fox_first_tile(cum, nrm, t):
    cend = cum[:, t - 1::t]
    nt = cend.shape[1]
    cprev = jnp.concatenate([jnp.zeros((cend.shape[0], 1), F32), cend[:, :-1]], axis=1)
    rep = nt // nrm.shape[0]
    qn = jnp.repeat(jnp.sqrt(nrm[:, 0, :]).T, rep, axis=1)
    kn = jnp.repeat(jnp.sqrt(nrm[:, 1, :]).T, rep, axis=1)
    gap = (1.02 * qn[:, :, None] * (kn[:, None, :] + kn[:, :, None])
           + cprev[:, :, None] - cend[:, None, :])
    jj = jnp.arange(nt)[None, None, :]
    ii = jnp.arange(nt)[None, :, None]
    needed = (jj < ii) & jnp.logical_not(gap <= -EXP_UNDERFLOW)
    needed = needed[0::2] | needed[1::2]
    first = jnp.min(jnp.where(needed, jj, ii), axis=2)
    return first.reshape(-1).astype(I32)


def _fox(qkv, tr, nr, cum, nrm):
    s = qkv.shape[0]
    t = FOX_TILE
    npair = N_HEADS_FOX // 2
    tq = 2 * t
    jlo = _fox_first_tile(cum, nrm, t).reshape(npair, s // tq, 2).min(axis=2).reshape(-1)
    cend = cum[:, t - 1::t]
    cprev = jnp.concatenate([jnp.zeros((cend.shape[0], 1), F32), cend[:, :-1]], axis=1)
    q_rows = 2 * (N_HEADS_MOBA // 2)
    grid_spec = pltpu.PrefetchScalarGridSpec(
        num_scalar_prefetch=1,
        grid=(npair, s // tq),
        in_specs=[pl.BlockSpec(memory_space=pltpu.SMEM),
                  pl.BlockSpec((PAIR, tq), lambda p, i, jl: (q_rows + p, i)),
                  pl.BlockSpec((s, PAIR), lambda p, i, jl: (0, p)),
                  pl.BlockSpec((PAIR, s), lambda p, i, jl: (q_rows + npair + p, 0)),
                  pl.BlockSpec((1, s, 2), lambda p, i, jl: (p, 0, 0))],
        out_specs=pl.BlockSpec((tq, PAIR), lambda p, i, jl: (i, p)),
        scratch_shapes=[pltpu.VMEM((2, 1, tq), F32), pltpu.VMEM((2, HEAD_DIM + SUM_ROWS, tq), F32)],
    )
    return pl.pallas_call(
        _fox_kernel,
        out_shape=jax.ShapeDtypeStruct((s, npair * PAIR), BF16),
        grid_spec=grid_spec,
        compiler_params=_cparams(("arbitrary", "arbitrary")),
        name="fox",
    )(jlo, cprev, tr, qkv, tr, nr)


def _t5_bucket_np(dist):
    dist = np.maximum(dist, 0)
    max_exact = NUM_BUCKETS // 2
    d = np.maximum(dist, 1).astype(np.float32)
    large = max_exact + (np.log(d / np.float32(max_exact)) / np.float32(math.log(MAX_DISTANCE / max_exact))
                         * np.float32(NUM_BUCKETS - max_exact)).astype(np.int32)
    large = np.minimum(large, NUM_BUCKETS - 1)
    return np.where(dist < max_exact, dist, large).astype(np.int32)


def _moba_kernel(rb_ref, qt_ref, k_ref, vt_ref, sel_ref, bkt_ref, o_ref,
                 m_ref, a_ref, bias_scr, *bufs):
    s_bufs = bufs[0:MOBA_UNROLL]
    p_bufs = bufs[MOBA_UNROLL:2 * MOBA_UNROLL]
    al_bufs = bufs[2 * MOBA_UNROLL:3 * MOBA_UNROLL]
    p = pl.program_id(0)
    g = pl.program_id(1)
    tq = qt_ref.shape[1]
    t = tq // 2

    @pl.when(g == 0)
    def _():
        r = lax.broadcasted_iota(I32, (t, t), 0)
        c = lax.broadcasted_iota(I32, (t, t), 1)
        zero = jnp.zeros((t, t), F32)
        for hd in range(2):
            h = 2 * p + hd
            far = rb_ref[(NUM_BUCKETS - 1) * N_HEADS_MOBA + h]
            tiles = []
            for w in range(2):
                bkt = bkt_ref[w]
                acc = jnp.zeros(bkt.shape, F32)
                for kk in range(NUM_BUCKETS):
                    acc = acc + jnp.where(bkt == kk, rb_ref[kk * N_HEADS_MOBA + h], 0.0)
                tiles.append((acc - far) * LOG2E)
            prev_t = tiles[0]
            own_t = jnp.where(r <= c, tiles[1], NEG)
            bias_scr[hd, 0] = jnp.concatenate([prev_t, zero], axis=1)
            bias_scr[hd, 1] = jnp.concatenate([own_t, prev_t], axis=1)
            bias_scr[hd, 2] = jnp.concatenate([zero, own_t], axis=1)

    qt = qt_ref[...]
    top = lax.broadcasted_iota(I32, qt.shape, 0) < HEAD_DIM
    zq = jnp.zeros_like(qt)
    qth = (jnp.where(top, qt, zq), jnp.where(top, zq, qt))
    m_ref[...] = jnp.full(m_ref.shape, M_INIT, F32)
    a_ref[...] = jnp.zeros(a_ref.shape, F32)

    ones_rows = jnp.where(lax.broadcasted_iota(I32, (SUM_ROWS, t), 0) == 0, 1.0, 0.0).astype(BF16)
    in_a = lax.broadcasted_iota(I32, (1, tq), 1) < t

    n_far = jnp.maximum(2 * g - 1, 0)

    n_key_tiles = k_ref.shape[0] // t

    def produce(j, s_buf):
        j = jnp.minimum(j, n_key_tiles - 1)
        k0 = pl.multiple_of(j * t, t)
        kt = k_ref[pl.ds(k0, t), :]
        for hd in range(2):
            s_buf[hd] = jnp.dot(kt, qth[hd], preferred_element_type=F32)

    def softmax(s_buf, p_buf, al_buf, selrows, w):
        for hd in range(2):
            s = s_buf[hd]
            if w is not None:
                s = s + bias_scr[hd, w]
            smax = jnp.max(s, axis=0, keepdims=True)
            m_old = m_ref[hd]
            m_new = jnp.maximum(m_old, smax + selrows[hd])
            shift = m_new - selrows[hd]
            al_buf[hd] = jnp.exp2(m_old - m_new)
            m_ref[hd] = m_new
            p_buf[hd * t:(hd + 1) * t, :] = jnp.exp2(s - shift).astype(BF16)

    def far_rows(j):
        return [jnp.where(j < n_far, sel_ref[hd, pl.ds(j, 1), :], NEG) for hd in range(2)]

    def accumulate(j, p_buf, al_buf):
        k0 = pl.multiple_of(j * t, t)
        for hd in range(2):
            vh = vt_ref[hd * HEAD_DIM:(hd + 1) * HEAD_DIM, pl.ds(k0, t)]
            vcat = jnp.concatenate([vh, ones_rows], axis=0)
            a_ref[hd] = (a_ref[hd] * al_buf[hd]
                         + jnp.dot(vcat, p_buf[hd * t:(hd + 1) * t, :], preferred_element_type=F32))

    un = MOBA_UNROLL
    lag = MOBA_LAG
    for n in range(un - lag, un):
        p_bufs[n][...] = jnp.zeros_like(p_bufs[n])
        al_bufs[n][...] = jnp.ones_like(al_bufs[n])
    produce(0, s_bufs[0])

    def body(u, carry):
        j0 = un * u
        for n in range(un):
            accumulate(jnp.maximum(j0 + n - lag, 0), p_bufs[(n - lag) % un], al_bufs[(n - lag) % un])
            softmax(s_bufs[n], p_bufs[n], al_bufs[n], far_rows(j0 + n), None)
            produce(j0 + n + 1, s_bufs[(n + 1) % un])
        return carry

    n_trips = (n_far + un - 1) // un
    lax.fori_loop(0, n_trips, body, 0)

    j1 = jnp.maximum(2 * g - 1, 0)
    j2 = 2 * g
    j3 = 2 * g + 1
    rows1 = [jnp.where(g >= 1, sel_ref[hd, pl.ds(j1, 1), :], NEG) for hd in range(2)]
    rows2 = [jnp.where(in_a, 0.0, sel_ref[hd, pl.ds(j2, 1), :]) for hd in range(2)]
    rows3 = [jnp.where(in_a, NEG, 0.0)] * 2
    produce(j1, s_bufs[0])
    produce(j2, s_bufs[1])
    for n in range(un - lag, un):
        accumulate(jnp.maximum(un * n_trips - un + n, 0), p_bufs[n], al_bufs[n])
    softmax(s_bufs[0], p_bufs[0], al_bufs[0], rows1, 0)
    produce(j3, s_bufs[0])
    softmax(s_bufs[1], p_bufs[1], al_bufs[1], rows2, 1)
    accumulate(j1, p_bufs[0], al_bufs[0])
    softmax(s_bufs[0], p_bufs[2], al_bufs[2], rows3, 2)
    accumulate(j2, p_bufs[1], al_bufs[1])
    accumulate(j3, p_bufs[2], al_bufs[2])
    out_t = jnp.concatenate([a_ref[hd, 0:HEAD_DIM, :] / a_ref[hd, HEAD_DIM:HEAD_DIM + 1, :]
                             for hd in range(2)], axis=0)
    o_ref[...] = out_t.T.astype(o_ref.dtype)


def _moba(qkv, tr, sel, rel_bias):
    s = qkv.shape[0]
    t = MOBA_BLOCK
    npair = N_HEADS_MOBA // 2
    nblk = s // t
    kcol = N_HEADS_FOX // 2
    a = np.arange(t)[None, :]
    b = np.arange(t)[:, None]
    bkt = jnp.asarray(np.stack([_t5_bucket_np(t + a - b), _t5_bucket_np(a - b)]))
    tq = 2 * t
    grid_spec = pltpu.PrefetchScalarGridSpec(
        num_scalar_prefetch=1,
        grid=(npair, s // tq),
        in_specs=[pl.BlockSpec((PAIR, tq), lambda p, i, rb: (p, i)),
                  pl.BlockSpec((s, PAIR), lambda p, i, rb: (0, kcol + p)),
                  pl.BlockSpec((PAIR, s), lambda p, i, rb: (npair + p, 0)),
                  pl.BlockSpec((2, nblk, tq), lambda p, i, rb: (p, 0, i)),
                  pl.BlockSpec((2, t, t), lambda p, i, rb: (0, 0, 0))],
        out_specs=pl.BlockSpec((tq, PAIR), lambda p, i, rb: (i, p)),
        scratch_shapes=[pltpu.VMEM((2, 1, tq), F32),
                        pltpu.VMEM((2, HEAD_DIM + SUM_ROWS, tq), F32), pltpu.VMEM((2, 3, t, tq), F32)]
        + [pltpu.VMEM((2, t, tq), F32)] * MOBA_UNROLL
        + [pltpu.VMEM((2 * t, tq), BF16)] * MOBA_UNROLL
        + [pltpu.VMEM((2, 1, tq), F32)] * MOBA_UNROLL,
    )
    return pl.pallas_call(
        _moba_kernel,
        out_shape=jax.ShapeDtypeStruct((s, npair * PAIR), BF16),
        grid_spec=grid_spec,
        compiler_params=_cparams(("arbitrary", "arbitrary")),
        name="moba",
    )(rel_bias.reshape(-1), tr, qkv, tr, sel, bkt)


def _store_token_tiles(ref, val):
    n = val.shape[0]
    for c in range(SUBLANES):
        ref[pl.ds(c, n, stride=SUBLANES), :] = val[:, c * LANES:(c + 1) * LANES]


def _load_token_tiles(ref, n):
    return jnp.concatenate([ref[pl.ds(c, n, stride=SUBLANES), :] for c in range(SUBLANES)], axis=1)


def _tile_rows(r0, n):
    start = r0 * SUBLANES
    if not isinstance(start, int):
        start = pl.multiple_of(start, SUBLANES)
    return pl.ds(start, n * SUBLANES)


def _outproj_kernel(ya_ref, yb_ref, x_ref, wa_ref, wb_ref, gpost_ref, gt_ref, gpre_ref,
                    sc_ref, sh_ref, wr_ref, br_ref,
                    x1_ref, h2_ref, route_ref, gate_ref, cnt_ref, carry_scr):
    i = pl.program_id(0)
    tm = x_ref.shape[0]

    @pl.when(i == 0)
    def _():
        carry_scr[...] = jnp.zeros_like(carry_scr)

    y = (jnp.dot(ya_ref[...], wa_ref[...], preferred_element_type=F32)
         + jnp.dot(yb_ref[...], wb_ref[...], preferred_element_type=F32))
    x1 = x_ref[...] + gt_ref[...] * (_rms(y) * gpost_ref[...])
    x1_ref[...] = x1
    h2 = _rms(x1) * gpre_ref[...] * (1.0 + sc_ref[...]) + sh_ref[...]
    _store_token_tiles(h2_ref, h2)

    h_hi = h2.astype(BF16)
    h_lo = (h2 - h_hi.astype(F32)).astype(BF16)
    logits = (jnp.dot(h_hi, wr_ref[0], preferred_element_type=F32)
              + jnp.dot(h_hi, wr_ref[1], preferred_element_type=F32)
              + jnp.dot(h_lo, wr_ref[0], preferred_element_type=F32)) + br_ref[...]
    ne = cnt_ref.shape[0]
    g = logits.T[0:ne, :]
    expert = lax.broadcasted_iota(I32, g.shape, 0)
    mask = jnp.zeros(g.shape, F32)
    vals, firsts, picks = [], [], []
    for _ in range(TOP_K):
        m = jnp.max(g, axis=0, keepdims=True)
        first = jnp.min(jnp.where(g == m, expert, ne), axis=0, keepdims=True)
        pick = expert == first
        mask = jnp.where(pick, 1.0, mask)
        g = jnp.where(pick, -jnp.inf, g)
        vals.append(m)
        firsts.append(first)
        picks.append(pick)
    ex = [jnp.exp(v - vals[0]) for v in vals]
    den = ex[0] + ex[1] + ex[2] + ex[3]
    gates = [e / den for e in ex]

    r = lax.broadcasted_iota(I32, (tm, tm), 0)
    c = lax.broadcasted_iota(I32, (tm, tm), 1)
    earlier = jnp.where(r < c, 1.0, 0.0).astype(BF16)
    before = jnp.dot(mask.astype(BF16), earlier, preferred_element_type=F32) + carry_scr[...]
    total = carry_scr[...] + jnp.sum(mask, axis=1, keepdims=True)
    carry_scr[...] = total
    cnt_ref[...] = jnp.broadcast_to(total, cnt_ref.shape)

    ranks = [jnp.sum(jnp.where(pk, before, 0.0), axis=0, keepdims=True).astype(I32) for pk in picks]
    route_ref[...] = jnp.concatenate(firsts + ranks, axis=0)
    gate_rows = jnp.concatenate(gates + [jnp.zeros((LANES - TOP_K, tm), F32)], axis=0)
    gate_ref[...] = gate_rows.T[:, 0:TOP_K]


def _outproj(mix_a, mix_b, x2, w_a, w_b, gpost, gt, gpre, sc, sh, w_router, b_router):
    s, d = x2.shape
    tm = ROW_TILE
    ne = w_router.shape[1]
    wa = mix_a.shape[1]
    w_router = jnp.pad(w_router, ((0, 0), (0, LANES - ne)))
    b_router = jnp.pad(b_router, ((0, 0), (0, LANES - ne)))
    w_hi = w_router.astype(BF16)
    w_router_hl = jnp.stack([w_hi, (w_router - w_hi.astype(F32)).astype(BF16)])
    row = lambda i: (i, 0)
    fix = lambda i: (0, 0)
    vec = pl.BlockSpec((1, d), fix)
    return pl.pallas_call(
        _outproj_kernel,
        out_shape=(jax.ShapeDtypeStruct((s, d), F32),
                   jax.ShapeDtypeStruct((s * SUBLANES, LANES), F32),
                   jax.ShapeDtypeStruct((2 * TOP_K, s), I32),
                   jax.ShapeDtypeStruct((s, TOP_K), F32),
                   jax.ShapeDtypeStruct((ne, LANES), F32)),
        grid=(s // tm,),
        in_specs=[pl.BlockSpec((tm, wa), row), pl.BlockSpec((tm, wa), row), pl.BlockSpec((tm, d), row),
                  pl.BlockSpec((wa, d), fix), pl.BlockSpec((wa, d), fix),
                  vec, vec, vec, vec, vec,
                  pl.BlockSpec((2, d, LANES), lambda i: (0, 0, 0)), pl.BlockSpec((1, LANES), fix)],
        out_specs=(pl.BlockSpec((tm, d), row), pl.BlockSpec((tm * SUBLANES, LANES), row),
                   pl.BlockSpec((2 * TOP_K, tm), lambda i: (0, i)),
                   pl.BlockSpec((tm, TOP_K), row), pl.BlockSpec((ne, LANES), fix)),
        scratch_shapes=[pltpu.VMEM((ne, 1), F32)],
        compiler_params=_cparams(("arbitrary",)),
        name="outproj",
    )(mix_a, mix_b, x2, w_a, w_b, gpost, gt, gpre, sc, sh, w_router_hl, b_router)


def _dest_kernel(route_ref, pstart_ref, o_ref):
    route = route_ref[...]
    tm = route.shape[1]
    ne = pstart_ref.shape[0]
    expert = lax.broadcasted_iota(I32, (ne, tm), 0)
    for k in range(TOP_K):
        start = jnp.sum(jnp.where(expert == route[k:k + 1, :], pstart_ref[...], 0), axis=0, keepdims=True)
        row = start + route[TOP_K + k:TOP_K + k + 1, :]
        for g in range(tm // LANES):
            o_ref[g * TOP_K + k:g * TOP_K + k + 1, :] = row[:, g * LANES:(g + 1) * LANES]


def _dest(route, pstart):
    s = route.shape[1]
    tm = min(4 * ROW_TILE, s)
    return pl.pallas_call(
        _dest_kernel,
        out_shape=jax.ShapeDtypeStruct((s // LANES * TOP_K, LANES), I32),
        grid=(s // tm,),
        in_specs=[pl.BlockSpec((2 * TOP_K, tm), lambda i: (0, i)),
                  pl.BlockSpec((N_EXPERTS, 1), lambda i: (0, 0))],
        out_specs=pl.BlockSpec((tm // LANES * TOP_K, LANES), lambda i: (i, 0)),
        compiler_params=_cparams(("arbitrary",)),
        name="dest",
    )(route, pstart.reshape(-1, 1))


def _dispatch_kernel(pstart_ref, pblk_ref, nu_ref, dest_ref, h_ref, xs_ref,
                     zero_scr, sem, zsem):
    tm = dest_ref.shape[0] // TOP_K * LANES
    first_token = pl.program_id(0) * tm
    bm = zero_scr.shape[0] // SUBLANES
    n_blk = xs_ref.shape[0] // (bm * SUBLANES)

    rows = _tile_rows

    @pl.when(pl.program_id(0) == 0)
    def _():
        zero_scr[...] = jnp.zeros_like(zero_scr)

        def zero_copy(row0):
            return pltpu.make_async_copy(zero_scr, xs_ref.at[rows(row0, bm)], zsem)

        for phase in range(2):
            for e in range(N_EXPERTS):
                last = pstart_ref[e] + (pblk_ref[e] - 1) * bm
                tail = (n_blk - N_EXPERTS + e) * bm
                for cond, row0 in ((pblk_ref[e] > 0, last), (n_blk - N_EXPERTS + e >= nu_ref[0], tail)):
                    @pl.when(cond)
                    def _():
                        if phase == 0:
                            zero_copy(row0).start()
                        else:
                            zero_copy(row0).wait()

    def start_group(g, carry):
        for lane in range(LANES):
            r = first_token + g * LANES + lane
            for k in range(TOP_K):
                dst = dest_ref[g * TOP_K + k, lane]
                pltpu.make_async_copy(h_ref.at[rows(r, 1)], xs_ref.at[rows(dst, 1)], sem).start(priority=k % 2)
        return carry

    lax.fori_loop(0, tm // LANES, start_group, 0)
    for k in range(TOP_K):
        pltpu.make_async_copy(h_ref.at[rows(0, tm)], xs_ref.at[rows(0, tm)], sem).wait()


def _dispatch(pstart, pblk, n_used, dest, h2, n_rows):
    s = h2.shape[0] // SUBLANES
    tm = COMBINE_TILE
    grid_spec = pltpu.PrefetchScalarGridSpec(
        num_scalar_prefetch=3,
        grid=(s // tm,),
        in_specs=[pl.BlockSpec((tm // LANES * TOP_K, LANES), lambda i, *_: (i, 0), memory_space=pltpu.SMEM),
                  pl.BlockSpec(memory_space=pl.ANY)],
        out_specs=pl.BlockSpec(memory_space=pl.ANY),
        scratch_shapes=[pltpu.VMEM((EXPERT_BLOCK * SUBLANES, LANES), F32), pltpu.SemaphoreType.DMA,
                        pltpu.SemaphoreType.DMA],
    )
    return pl.pallas_call(
        _dispatch_kernel,
        out_shape=jax.ShapeDtypeStruct((n_rows * SUBLANES, LANES), F32),
        grid_spec=grid_spec,
        compiler_params=_cparams(("arbitrary",)),
        name="dispatch",
    )(pstart, pblk, n_used, dest, h2)


def _experts_kernel(be_ref, nxt_ref, nu_ref, xs_ref, wgu_hbm, bgu_ref, wd_hbm, bd_ref, y_ref,
                    wgu_f32, wd_f32, wgu_bf, wd_bf, sem):
    b = pl.program_id(0)
    d_exp = wd_hbm.shape[1]
    prev = be_ref[jnp.maximum(b - 1, 0)]
    changed = (b == 0) | (be_ref[b] != prev)

    def fetch(e):
        return (pltpu.make_async_copy(wgu_hbm.at[e], wgu_f32, sem.at[0]),
                pltpu.make_async_copy(wd_hbm.at[e], wd_f32, sem.at[1]))

    @pl.when(b == 0)
    def _():
        for copy in fetch(be_ref[0]):
            copy.start()

    @pl.when((b < nu_ref[0]) & changed)
    def _():
        for copy in fetch(be_ref[b]):
            copy.wait()
        rows = LANES

        def cast_gu(c, carry):
            r0 = pl.multiple_of(c * rows, rows)
            wgu_bf[pl.ds(r0, rows), :] = wgu_f32[pl.ds(r0, rows), :].astype(BF16)
            return carry

        def cast_d(c, carry):
            r0 = pl.multiple_of(c * rows, rows)
            wd_bf[pl.ds(r0, rows), :] = wd_f32[pl.ds(r0, rows), :].astype(BF16)
            return carry

        lax.fori_loop(0, wgu_f32.shape[0] // rows, cast_gu, 0)
        lax.fori_loop(0, wd_f32.shape[0] // rows, cast_d, 0)

        @pl.when(nxt_ref[b] >= 0)
        def _():
            for copy in fetch(nxt_ref[b]):
                copy.start()

    @pl.when(b < nu_ref[0])
    def _():
        bm = xs_ref.shape[0] // SUBLANES
        xb = _load_token_tiles(xs_ref, bm).astype(BF16)
        hdn = jnp.dot(xb, wgu_bf[...], preferred_element_type=F32) + bgu_ref[0]
        x_glu = jnp.minimum(hdn[:, :d_exp], SWIGLU_LIMIT)
        x_lin = jnp.clip(hdn[:, d_exp:], -SWIGLU_LIMIT, SWIGLU_LIMIT)
        act = x_glu * jax.nn.sigmoid(SWIGLU_ALPHA * x_glu) * (x_lin + 1.0)
        _store_token_tiles(y_ref, jnp.dot(act.astype(BF16), wd_bf[...], preferred_element_type=F32)
                           + bd_ref[0])

    @pl.when(b >= nu_ref[0])
    def _():
        y_ref[...] = jnp.zeros_like(y_ref)


def _experts(block_e, next_e, n_used, xs, w_gate_up, b_gate_up, w_down, b_down):
    n_rows = xs.shape[0] // SUBLANES
    bm = EXPERT_BLOCK
    n_blk = n_rows // bm
    ne, d, two_de = w_gate_up.shape
    de = w_down.shape[1]
    assert d == SUBLANES * LANES

    def blk(b, nu):
        return jnp.minimum(b, nu[0] - 1)

    grid_spec = pltpu.PrefetchScalarGridSpec(
        num_scalar_prefetch=3,
        grid=(n_blk,),
        in_specs=[pl.BlockSpec((bm * SUBLANES, LANES), lambda b, be, nxt, nu: (blk(b, nu), 0)),
                  pl.BlockSpec(memory_space=pl.ANY),
                  pl.BlockSpec((1, 1, two_de), lambda b, be, nxt, nu: (be[blk(b, nu)], 0, 0)),
                  pl.BlockSpec(memory_space=pl.ANY),
                  pl.BlockSpec((1, 1, d), lambda b, be, nxt, nu: (be[blk(b, nu)], 0, 0))],
        out_specs=pl.BlockSpec((bm * SUBLANES, LANES), lambda b, be, nxt, nu: (b, 0)),
        scratch_shapes=[pltpu.VMEM((d, two_de), F32), pltpu.VMEM((de, d), F32),
                        pltpu.VMEM((d, two_de), BF16), pltpu.VMEM((de, d), BF16),
                        pltpu.SemaphoreType.DMA((2,))],
    )
    return pl.pallas_call(
        _experts_kernel,
        out_shape=jax.ShapeDtypeStruct((n_rows * SUBLANES, LANES), F32),
        grid_spec=grid_spec,
        compiler_params=_cparams(("arbitrary",)),
        name="experts",
    )(block_e, next_e, n_used, xs, w_gate_up, b_gate_up.reshape(ne, 1, two_de), w_down,
      b_down.reshape(ne, 1, d))


def _combine_kernel(dest_ref, dest_next_ref, y_ref, gate_ref, x1_ref, gt_ref, gpost_ref, o_ref,
                    buf_even, buf_odd, sem):
    i = pl.program_id(0)
    tm = x1_ref.shape[0]
    ch = COMBINE_CHUNK
    bufs = (buf_even, buf_odd)

    def start_chunk(d_ref, slot, c):
        for lane in range(ch):
            r = c * ch + lane
            for k in range(TOP_K):
                pltpu.make_async_copy(y_ref.at[_tile_rows(d_ref[c * TOP_K + k, lane], 1)],
                                      bufs[slot].at[k, _tile_rows(r, 1)], sem.at[slot, c]).start(priority=k % 2)

    @pl.when(i == 0)
    def _():
        def first(c, carry):
            start_chunk(dest_ref, 0, c)
            return carry

        lax.fori_loop(0, tm // ch, first, 0)

    def reduce_chunk(c, cur, gather_next):
        buf = bufs[cur]
        for k in range(TOP_K):
            pltpu.make_async_copy(y_ref.at[_tile_rows(0, ch)], buf.at[k, _tile_rows(c * ch, ch)],
                                  sem.at[cur, c]).wait()
        if gather_next:
            start_chunk(dest_next_ref, 1 - cur, c)
        n = COMBINE_REDUCE_ROWS
        for part in range(ch // n):
            r0 = pl.multiple_of(c * ch + part * n, n)
            rows = pl.ds(r0, n)
            gate = gate_ref[rows, :]
            acc = gate[:, 0:1] * _load_token_tiles(buf.at[0, _tile_rows(r0, n)], n)
            for k in range(1, TOP_K):
                acc = acc + gate[:, k:k + 1] * _load_token_tiles(buf.at[k, _tile_rows(r0, n)], n)
            o_ref[rows, :] = x1_ref[rows, :] + gt_ref[...] * (_rms(acc) * gpost_ref[...])

    has_next = i + 1 < pl.num_programs(0)
    for cur in range(2):
        for gather_next in (True, False):
            @pl.when((i % 2 == cur) & (has_next == gather_next))
            def _(cur=cur, gather_next=gather_next):
                def body(c, carry):
                    reduce_chunk(c, cur, gather_next)
                    return carry

                lax.fori_loop(0, tm // ch, body, 0)


def _combine(dest, y, gate4, x1, gt, gpost):
    s, d = x1.shape
    tm = COMBINE_TILE // 2
    n = s // tm
    return pl.pallas_call(
        _combine_kernel,
        out_shape=jax.ShapeDtypeStruct((s, d), F32),
        grid=(n,),
        in_specs=[pl.BlockSpec((tm // LANES * TOP_K, LANES), lambda i: (i, 0), memory_space=pltpu.SMEM),
                  pl.BlockSpec((tm // LANES * TOP_K, LANES), lambda i: (jnp.minimum(i + 1, n - 1), 0),
                               memory_space=pltpu.SMEM),
                  pl.BlockSpec(memory_space=pl.ANY),
                  pl.BlockSpec((tm, TOP_K), lambda i: (i, 0)),
                  pl.BlockSpec((tm, d), lambda i: (i, 0)),
                  pl.BlockSpec((1, d), lambda i: (0, 0)),
                  pl.BlockSpec((1, d), lambda i: (0, 0))],
        out_specs=pl.BlockSpec((tm, d), lambda i: (i, 0)),
        scratch_shapes=[pltpu.VMEM((TOP_K, tm * SUBLANES, LANES), F32),
                        pltpu.VMEM((TOP_K, tm * SUBLANES, LANES), F32),
                        pltpu.SemaphoreType.DMA((2, tm // COMBINE_CHUNK))],
        compiler_params=_cparams(("arbitrary",)),
        name="combine",
    )(dest, dest, y, gate4, x1, gt, gpost)


def _layer(x2, mod, g_pre_mix, g_post_mix, w_in, b_forget, rel_bias, w_out,
           g_pre_ffn, g_post_ffn, w_router, b_router, w_gate_up, b_gate_up, w_down, b_down):
    s, d = x2.shape
    sh_m, sc_m, gt_m, sh_f, sc_f, gt_f = [mod[:, k * d:(k + 1) * d] for k in range(6)]
    n_qkv = 3 * (N_HEADS_FOX + N_HEADS_MOBA) * HEAD_DIM
    fox_w = N_HEADS_FOX * HEAD_DIM

    w_qkv = w_in[:, :n_qkv].astype(BF16)
    w_f = w_in[:, n_qkv:].T
    qkv, cum, sel, nrm, tr, nr = _inproj(x2, g_pre_mix.reshape(1, d), sc_m, sh_m, w_qkv, w_f,
                                         b_forget.reshape(-1, 1))
    y_a = _fox(qkv, tr, nr, cum, nrm)
    y_b = _moba(qkv, tr, sel, rel_bias)

    w_out_bf = w_out.astype(BF16)
    x1, h2, route, gate4, cnt = _outproj(
        y_a, y_b, x2, w_out_bf[:fox_w], w_out_bf[fox_w:], g_post_mix.reshape(1, d), gt_m,
        g_pre_ffn.reshape(1, d), sc_f, sh_f, w_router, b_router.reshape(1, -1))

    bm = EXPERT_BLOCK
    counts = cnt[:, 0].astype(I32)
    pblk = (counts + bm - 1) // bm
    pend_blk = jnp.cumsum(pblk)
    pstart = ((pend_blk - pblk) * bm).astype(I32)
    n_rows = s * TOP_K + N_EXPERTS * bm
    n_blk = n_rows // bm
    ends = pend_blk[None, :]
    blocks = jnp.arange(n_blk)[:, None]
    block_e = jnp.minimum(jnp.sum(ends <= blocks, axis=1), N_EXPERTS - 1).astype(I32)
    n_used = pend_blk[-1:].astype(I32)
    region_end = jnp.min(jnp.where(ends > blocks, ends, n_blk), axis=1)
    next_e = jnp.where(region_end < n_used, jnp.sum(ends <= region_end[:, None], axis=1), -1).astype(I32)

    dest = _dest(route, pstart)
    xs = _dispatch(pstart, pblk.astype(I32), n_used, dest, h2, n_rows)
    y = _experts(block_e, next_e, n_used, xs, w_gate_up, b_gate_up, w_down, b_down)
    return _combine(dest, y, gate4, x1, gt_f, g_post_ffn.reshape(1, d))


def kernel(x, c, w_ada, b_ada, g_pre_mix, g_post_mix, w_in, b_forget, rel_bias, w_out, g_pre_ffn, g_post_ffn, w_router, b_router, w_gate_up, b_gate_up, w_down, b_down):
    bsz, s, d = x.shape
    depth = w_ada.shape[0]
    outs = []
    for bi in range(bsz):
        x2 = x[bi]
        for l in range(depth):
            mod = _adaln(c[bi:bi + 1], w_ada[l], b_ada[l])
            x2 = _layer(x2, mod, g_pre_mix[l], g_post_mix[l], w_in[l], b_forget[l], rel_bias, w_out[l],
                        g_pre_ffn[l], g_post_ffn[l], w_router[l], b_router[l], w_gate_up[l], b_gate_up[l],
                        w_down[l], b_down[l])
        outs.append(x2)
    return outs[0].reshape(1, s, d) if bsz == 1 else jnp.stack(outs)
```
